```python
import jax, jax.numpy as jnp
from jax import lax
import numpy as np

D_MODEL = 1024
BATCH = 8
SEQ = 8192
DEPTH = 1

HEAD_DIM = 64
N_HEADS_FOX = D_MODEL // (2 * HEAD_DIM)
N_HEADS_SB = D_MODEL // (2 * HEAD_DIM)
D_FOX = N_HEADS_FOX * HEAD_DIM
D_SB = N_HEADS_SB * HEAD_DIM
D_MIX = D_FOX + D_SB
N_IN = 3 * D_FOX + 3 * D_SB + N_HEADS_FOX
BLOCK_Q = 128
D_FF = int(round(8 * D_MODEL / 3 / 64)) * 64
CONV_WIDTH = 3
N_MOD = 6
EPS = 1e-6

kernel_name = 'hybrid_fox_stickbreaking_convffn_adaln'


def rms_norm(x, g):
    xf = x.astype(jnp.float32)
    y = xf * lax.rsqrt(jnp.mean(xf * xf, axis=-1, keepdims=True) + EPS)
    return (y * g.astype(jnp.float32)).astype(x.dtype)


def split_heads(t, n_heads):
    B, S, _ = t.shape
    return t.reshape(B, S, n_heads, HEAD_DIM).transpose(0, 2, 1, 3)


def head_rms_norm(o, g):
    B, H, S, Dh = o.shape
    o = o.transpose(0, 2, 1, 3)
    return rms_norm(o, g.reshape(H, Dh)).reshape(B, S, H * Dh)


def to_blocks(t):
    B, H, S = t.shape[:3]
    nb = S // BLOCK_Q
    t = t.reshape((B, H, nb, BLOCK_Q) + t.shape[3:])
    return jnp.moveaxis(t, 2, 0)


def from_blocks(t):
    nb, B, H, bq, Dh = t.shape
    return jnp.moveaxis(t, 0, 2).reshape(B, H, nb * bq, Dh)


def forgetting_attention(q, k, v, log_f):
    S = q.shape[2]
    scale = HEAD_DIM ** -0.5
    F = jnp.cumsum(log_f.astype(jnp.float32), axis=-1)
    kpos = jnp.arange(S)
    nb = S // BLOCK_Q

    def one_block(args):
        i, q_blk, F_blk = args
        qpos = i * BLOCK_Q + jnp.arange(BLOCK_Q)
        s = jnp.einsum('bhqd,bhkd->bhqk', q_blk, k, preferred_element_type=jnp.float32) * scale
        s = s + F_blk[..., None] - F[:, :, None, :]
        causal = kpos[None, :] <= qpos[:, None]
        s = jnp.where(causal, s, -jnp.inf)
        p = jax.nn.softmax(s, axis=-1)
        return jnp.einsum('bhqk,bhkd->bhqd', p.astype(v.dtype), v)

    out = lax.map(one_block, (jnp.arange(nb), to_blocks(q), to_blocks(F)))
    return from_blocks(out)


def stick_breaking_attention(q, k, v):
    S = q.shape[2]
    scale = HEAD_DIM ** -0.5
    kpos = jnp.arange(S)
    nb = S // BLOCK_Q

    def one_block(args):
        i, q_blk = args
        qpos = i * BLOCK_Q + jnp.arange(BLOCK_Q)
        z = jnp.einsum('bhqd,bhkd->bhqk', q_blk, k, preferred_element_type=jnp.float32) * scale
        strict = kpos[None, :] < qpos[:, None]
        log_one_minus_beta = jnp.where(strict, jax.nn.log_sigmoid(-z), 0.0)
        rest = lax.cumsum(log_one_minus_beta, axis=3, reverse=True) - log_one_minus_beta
        log_a = jax.nn.log_sigmoid(z) + rest
        a = jnp.where(strict, jnp.exp(log_a), 0.0)
        return jnp.einsum('bhqk,bhkd->bhqd', a.astype(v.dtype), v)

    out = lax.map(one_block, (jnp.arange(nb), to_blocks(q)))
    return from_blocks(out)


def causal_depthwise_conv(u, w, b):
    C = u.shape[-1]
    y = lax.conv_general_dilated(
        u, w.astype(u.dtype).reshape(CONV_WIDTH, 1, C),
        window_strides=(1,), padding=[(CONV_WIDTH - 1, 0)],
        dimension_numbers=('NWC', 'WIO', 'NWC'), feature_group_count=C)
    return y + b.astype(u.dtype)


def _fwd_setup_inputs(seed: int = 0) -> dict:
    key = jax.random.key(seed)
    ks = jax.random.split(key, 16)
    L, D = DEPTH, D_MODEL
    f32 = jnp.float32

    def nrm(k, shape, s):
        return jax.random.normal(k, shape, f32) * s

    return {
        'x': nrm(ks[0], (BATCH, SEQ, D), 1.0),
        'c': nrm(ks[1], (BATCH, D), 1.0),
        'w_ada': nrm(ks[2], (L, D, N_MOD * D), D ** -0.5),
        'b_ada': nrm(ks[3], (L, N_MOD * D), 0.02),
        'g_attn': 1.0 + nrm(ks[4], (L, D), 0.02),
        'w_in': nrm(ks[5], (L, D, N_IN), D ** -0.5),
        'b_fgate': 2.0 + nrm(ks[6], (L, N_HEADS_FOX), 0.5),
        'g_out_fox': 1.0 + nrm(ks[7], (L, D_FOX), 0.02),
        'g_out_sb': 1.0 + nrm(ks[8], (L, D_SB), 0.02),
        'w_out': nrm(ks[9], (L, D_MIX, D), D_MIX ** -0.5),
        'g_mlp': 1.0 + nrm(ks[10], (L, D), 0.02),
        'w_up': nrm(ks[11], (L, D, 2 * D_FF), D ** -0.5),
        'conv_w': nrm(ks[12], (L, CONV_WIDTH, 2 * D_FF), CONV_WIDTH ** -0.5),
        'conv_b': nrm(ks[13], (L, 2 * D_FF), 0.02),
        'w_down': nrm(ks[14], (L, D_FF, D), D_FF ** -0.5),
        'g_final': 1.0 + nrm(ks[15], (D,), 0.02),
    }


def _fwd_reference(x, c, w_ada, b_ada, g_attn, w_in, b_fgate, g_out_fox, g_out_sb, w_out,
              g_mlp, w_up, conv_w, conv_b, w_down, g_final):
    sizes = [D_FOX, D_FOX, D_FOX, D_SB, D_SB, D_SB]
    offsets = np.cumsum(sizes).tolist()
    for l in range(DEPTH):
        mod = jax.nn.silu(c) @ w_ada[l] + b_ada[l]
        shift_a, scale_a, gate_a, shift_m, scale_m, gate_m = [
            m[:, None, :] for m in jnp.split(mod, N_MOD, axis=-1)]

        h = rms_norm(x, g_attn[l]) * (1.0 + scale_a) + shift_a
        proj = h @ w_in[l]
        q_f, k_f, v_f, q_s, k_s, v_s, f_logit = jnp.split(proj, offsets, axis=-1)
        log_f = jax.nn.log_sigmoid((f_logit + b_fgate[l]).astype(jnp.float32))
        o_fox = forgetting_attention(split_heads(q_f, N_HEADS_FOX), split_heads(k_f, N_HEADS_FOX),
                                     split_heads(v_f, N_HEADS_FOX), log_f.transpose(0, 2, 1))
        o_sb = stick_breaking_attention(split_heads(q_s, N_HEADS_SB), split_heads(k_s, N_HEADS_SB),
                                        split_heads(v_s, N_HEADS_SB))
        mix = jnp.concatenate([head_rms_norm(o_fox, g_out_fox[l]),
                               head_rms_norm(o_sb, g_out_sb[l])], axis=-1)
        x = x + gate_a * (mix @ w_out[l])

        h = rms_norm(x, g_mlp[l]) * (1.0 + scale_m) + shift_m
        u = causal_depthwise_conv(h @ w_up[l], conv_w[l], conv_b[l])
        u_gate, u_val = jnp.split(u, 2, axis=-1)
        x = x + gate_m * ((jax.nn.silu(u_gate) * u_val) @ w_down[l])
    return rms_norm(x, g_final)


import jax as _jax
import jax.numpy as _jnp

TWIN_FORMAT = 'train_step'
FWD_PARAMS = ['x', 'c', 'w_ada', 'b_ada', 'g_attn', 'w_in', 'b_fgate', 'g_out_fox', 'g_out_sb', 'w_out', 'g_mlp', 'w_up', 'conv_w', 'conv_b', 'w_down', 'g_final']
TWIN_WEIGHTS = ['w_ada', 'b_ada', 'g_attn', 'w_in', 'b_fgate', 'g_out_fox', 'g_out_sb', 'w_out', 'g_mlp', 'w_up', 'conv_w', 'conv_b', 'w_down', 'g_final']
TWIN_DIFF_INPUT = 'x'
TWIN_INPUTS = ['x', 'c', 'w_ada', 'b_ada', 'g_attn', 'w_in', 'b_fgate', 'g_out_fox', 'g_out_sb', 'w_out', 'g_mlp', 'w_up', 'conv_w', 'conv_b', 'w_down', 'g_final', 'loss_target', 'm_w_ada', 'm_b_ada', 'm_g_attn', 'm_w_in', 'm_b_fgate', 'm_g_out_fox', 'm_g_out_sb', 'm_w_out', 'm_g_mlp', 'm_w_up', 'm_conv_w', 'm_conv_b', 'm_w_down', 'm_g_final', 'v_w_ada', 'v_b_ada', 'v_g_attn', 'v_w_in', 'v_b_fgate', 'v_g_out_fox', 'v_g_out_sb', 'v_w_out', 'v_g_mlp', 'v_w_up', 'v_conv_w', 'v_conv_b', 'v_w_down', 'v_g_final']
TWIN_OUTPUTS = ['loss', 'grad_x', 'grad_w_ada', 'grad_b_ada', 'grad_g_attn', 'grad_w_in', 'grad_b_fgate', 'grad_g_out_fox', 'grad_g_out_sb', 'grad_w_out', 'grad_g_mlp', 'grad_w_up', 'grad_conv_w', 'grad_conv_b', 'grad_w_down', 'grad_g_final', 'delta_w_ada', 'delta_b_ada', 'delta_g_attn', 'delta_w_in', 'delta_b_fgate', 'delta_g_out_fox', 'delta_g_out_sb', 'delta_w_out', 'delta_g_mlp', 'delta_w_up', 'delta_conv_w', 'delta_conv_b', 'delta_w_down', 'delta_g_final', 'new_m_w_ada', 'new_m_b_ada', 'new_m_g_attn', 'new_m_w_in', 'new_m_b_fgate', 'new_m_g_out_fox', 'new_m_g_out_sb', 'new_m_w_out', 'new_m_g_mlp', 'new_m_w_up', 'new_m_conv_w', 'new_m_conv_b', 'new_m_w_down', 'new_m_g_final', 'new_v_w_ada', 'new_v_b_ada', 'new_v_g_attn', 'new_v_w_in', 'new_v_b_fgate', 'new_v_g_out_fox', 'new_v_g_out_sb', 'new_v_w_out', 'new_v_g_mlp', 'new_v_w_up', 'new_v_conv_w', 'new_v_conv_b', 'new_v_w_down', 'new_v_g_final']
TWIN_LEAF_KINDS = {'loss': 'loss', 'grad_x': 'grad_x', 'grad_w_ada': 'grad_w', 'grad_b_ada': 'grad_w', 'grad_g_attn': 'grad_w', 'grad_w_in': 'grad_w', 'grad_b_fgate': 'grad_w', 'grad_g_out_fox': 'grad_w', 'grad_g_out_sb': 'grad_w', 'grad_w_out': 'grad_w', 'grad_g_mlp': 'grad_w', 'grad_w_up': 'grad_w', 'grad_conv_w': 'grad_w', 'grad_conv_b': 'grad_w', 'grad_w_down': 'grad_w', 'grad_g_final': 'grad_w', 'delta_w_ada': 'delta_w', 'delta_b_ada': 'delta_w', 'delta_g_attn': 'delta_w', 'delta_w_in': 'delta_w', 'delta_b_fgate': 'delta_w', 'delta_g_out_fox': 'delta_w', 'delta_g_out_sb': 'delta_w', 'delta_w_out': 'delta_w', 'delta_g_mlp': 'delta_w', 'delta_w_up': 'delta_w', 'delta_conv_w': 'delta_w', 'delta_conv_b': 'delta_w', 'delta_w_down': 'delta_w', 'delta_g_final': 'delta_w', 'new_m_w_ada': 'new_m', 'new_m_b_ada': 'new_m', 'new_m_g_attn': 'new_m', 'new_m_w_in': 'new_m', 'new_m_b_fgate': 'new_m', 'new_m_g_out_fox': 'new_m', 'new_m_g_out_sb': 'new_m', 'new_m_w_out': 'new_m', 'new_m_g_mlp': 'new_m', 'new_m_w_up': 'new_m', 'new_m_conv_w': 'new_m', 'new_m_conv_b': 'new_m', 'new_m_w_down': 'new_m', 'new_m_g_final': 'new_m', 'new_v_w_ada': 'new_v', 'new_v_b_ada': 'new_v', 'new_v_g_attn': 'new_v', 'new_v_w_in': 'new_v', 'new_v_b_fgate': 'new_v', 'new_v_g_out_fox': 'new_v', 'new_v_g_out_sb': 'new_v', 'new_v_w_out': 'new_v', 'new_v_g_mlp': 'new_v', 'new_v_w_up': 'new_v', 'new_v_conv_w': 'new_v', 'new_v_conv_b': 'new_v', 'new_v_w_down': 'new_v', 'new_v_g_final': 'new_v'}


def _forward(args):
    return _fwd_reference(*[args[k] for k in FWD_PARAMS])


def _output_shape():
    def fwd():
        inp = _fwd_setup_inputs(0)
        return _fwd_reference(*[inp[k] for k in FWD_PARAMS])
    out = _jax.eval_shape(fwd)
    return out.shape, out.dtype

N_MICROBATCH = 1
ADAM_LR = 0.001
ADAM_B1 = 0.9
ADAM_B2 = 0.999
ADAM_EPS = 1e-08
ADAM_WD = 0.01
ADAM_STEP = 10
PER_EXAMPLE_BATCH_AXIS = {'x': 0, 'c': 0, 'loss_target': 0}
SHARED_INPUTS = []
_WEIGHT_DTYPES = {'w_ada': _jnp.float32, 'b_ada': _jnp.float32, 'g_attn': _jnp.float32, 'w_in': _jnp.float32, 'b_fgate': _jnp.float32, 'g_out_fox': _jnp.float32, 'g_out_sb': _jnp.float32, 'w_out': _jnp.float32, 'g_mlp': _jnp.float32, 'w_up': _jnp.float32, 'conv_w': _jnp.float32, 'conv_b': _jnp.float32, 'w_down': _jnp.float32, 'g_final': _jnp.float32}
MOMENT_SCALE = {'w_ada': 1.446804e-01, 'b_ada': 2.778852e-01, 'g_attn': 1.615620e-01, 'w_in': 1.226975e-01, 'b_fgate': 1.049322e+00, 'g_out_fox': 1.713738e-01, 'g_out_sb': 1.550806e-01, 'w_out': 1.714170e-01, 'g_mlp': 1.636669e-01, 'w_up': 8.131410e-02, 'conv_w': 7.986877e-02, 'conv_b': 6.613278e-02, 'w_down': 1.370508e-01, 'g_final': 6.534837e+01}


def _to_microbatches(a, axis):
    t = _jnp.moveaxis(a, axis, 0)
    t = t.reshape((N_MICROBATCH, t.shape[0] // N_MICROBATCH) + t.shape[1:])
    return _jnp.moveaxis(t, 1, axis + 1)


def setup_inputs(seed: int = 0) -> dict:
    inp = _fwd_setup_inputs(seed)
    key = _jax.random.fold_in(_jax.random.key(seed), 7919)
    shape, _ = _output_shape()
    out = dict(inp)
    out["loss_target"] = _jax.random.normal(_jax.random.fold_in(key, 0), shape, _jnp.float32)
    for i, name in enumerate(TWIN_WEIGHTS):
        w = inp[name].astype(_jnp.float32)
        if MOMENT_SCALE is None:
            s = _jnp.sqrt(_jnp.mean(_jnp.square(w)) + 1e-30)
        else:
            s = MOMENT_SCALE[name]
        km, kv = _jax.random.split(_jax.random.fold_in(key, i + 1))
        out[name] = w
        out["m_" + name] = s * _jax.random.normal(km, w.shape, _jnp.float32)
        out["v_" + name] = (s * s) * _jax.random.uniform(kv, w.shape, _jnp.float32, 0.5, 1.5)
    if N_MICROBATCH > 1:
        for name, axis in PER_EXAMPLE_BATCH_AXIS.items():
            out[name] = _to_microbatches(out[name], axis)
    return {'x': out['x'], 'c': out['c'], 'w_ada': out['w_ada'], 'b_ada': out['b_ada'], 'g_attn': out['g_attn'], 'w_in': out['w_in'], 'b_fgate': out['b_fgate'], 'g_out_fox': out['g_out_fox'], 'g_out_sb': out['g_out_sb'], 'w_out': out['w_out'], 'g_mlp': out['g_mlp'], 'w_up': out['w_up'], 'conv_w': out['conv_w'], 'conv_b': out['conv_b'], 'w_down': out['w_down'], 'g_final': out['g_final'], 'loss_target': out['loss_target'], 'm_w_ada': out['m_w_ada'], 'm_b_ada': out['m_b_ada'], 'm_g_attn': out['m_g_attn'], 'm_w_in': out['m_w_in'], 'm_b_fgate': out['m_b_fgate'], 'm_g_out_fox': out['m_g_out_fox'], 'm_g_out_sb': out['m_g_out_sb'], 'm_w_out': out['m_w_out'], 'm_g_mlp': out['m_g_mlp'], 'm_w_up': out['m_w_up'], 'm_conv_w': out['m_conv_w'], 'm_conv_b': out['m_conv_b'], 'm_w_down': out['m_w_down'], 'm_g_final': out['m_g_final'], 'v_w_ada': out['v_w_ada'], 'v_b_ada': out['v_b_ada'], 'v_g_attn': out['v_g_attn'], 'v_w_in': out['v_w_in'], 'v_b_fgate': out['v_b_fgate'], 'v_g_out_fox': out['v_g_out_fox'], 'v_g_out_sb': out['v_g_out_sb'], 'v_w_out': out['v_w_out'], 'v_g_mlp': out['v_g_mlp'], 'v_w_up': out['v_w_up'], 'v_conv_w': out['v_conv_w'], 'v_conv_b': out['v_conv_b'], 'v_w_down': out['v_w_down'], 'v_g_final': out['v_g_final']}


def _loss(weights, diff, rest, loss_target):
    with _jax.named_scope("forward"):
        args = {**rest, TWIN_DIFF_INPUT: diff, **{k: w.astype(_WEIGHT_DTYPES[k]) for k, w in weights.items()}}
        y = _forward(args)
    with _jax.named_scope("loss_head"):
        err = _jnp.square(y.astype(_jnp.float32) - loss_target)
        return 0.5 * _jnp.sum(_jnp.mean(err, axis=-1)) if err.ndim else 0.5 * err


def _adamw(w, g, m, v):
    m = ADAM_B1 * m + (1.0 - ADAM_B1) * g
    v = ADAM_B2 * v + (1.0 - ADAM_B2) * _jnp.square(g)
    m_hat = m / (1.0 - ADAM_B1 ** ADAM_STEP)
    v_hat = v / (1.0 - ADAM_B2 ** ADAM_STEP)
    delta = -ADAM_LR * (m_hat / (_jnp.sqrt(v_hat) + ADAM_EPS) + ADAM_WD * w)
    return delta, m, v


def reference(x, c, w_ada, b_ada, g_attn, w_in, b_fgate, g_out_fox, g_out_sb, w_out, g_mlp, w_up, conv_w, conv_b, w_down, g_final, loss_target, m_w_ada, m_b_ada, m_g_attn, m_w_in, m_b_fgate, m_g_out_fox, m_g_out_sb, m_w_out, m_g_mlp, m_w_up, m_conv_w, m_conv_b, m_w_down, m_g_final, v_w_ada, v_b_ada, v_g_attn, v_w_in, v_b_fgate, v_g_out_fox, v_g_out_sb, v_w_out, v_g_mlp, v_w_up, v_conv_w, v_conv_b, v_w_down, v_g_final):
    given = dict(x=x, c=c, w_ada=w_ada, b_ada=b_ada, g_attn=g_attn, w_in=w_in, b_fgate=b_fgate, g_out_fox=g_out_fox, g_out_sb=g_out_sb, w_out=w_out, g_mlp=g_mlp, w_up=w_up, conv_w=conv_w, conv_b=conv_b, w_down=w_down, g_final=g_final, loss_target=loss_target, m_w_ada=m_w_ada, m_b_ada=m_b_ada, m_g_attn=m_g_attn, m_w_in=m_w_in, m_b_fgate=m_b_fgate, m_g_out_fox=m_g_out_fox, m_g_out_sb=m_g_out_sb, m_w_out=m_w_out, m_g_mlp=m_g_mlp, m_w_up=m_w_up, m_conv_w=m_conv_w, m_conv_b=m_conv_b, m_w_down=m_w_down, m_g_final=m_g_final, v_w_ada=v_w_ada, v_b_ada=v_b_ada, v_g_attn=v_g_attn, v_w_in=v_w_in, v_b_fgate=v_b_fgate, v_g_out_fox=v_g_out_fox, v_g_out_sb=v_g_out_sb, v_w_out=v_w_out, v_g_mlp=v_g_mlp, v_w_up=v_w_up, v_conv_w=v_conv_w, v_conv_b=v_conv_b, v_w_down=v_w_down, v_g_final=v_g_final)
    weights = {n: given[n] for n in TWIN_WEIGHTS}
    shared = {n: given[n] for n in SHARED_INPUTS}
    per_example = {n: given[n] for n in ['x', 'c']}
    grad_fn = _jax.value_and_grad(_loss, argnums=(0, 1))

    def one_microbatch(ex, loss_target):
        ex = dict(ex)
        diff = ex.pop(TWIN_DIFF_INPUT)
        return grad_fn(weights, diff, {**shared, **ex}, loss_target)

    if N_MICROBATCH == 1:
        loss, (grad_w, grad_x) = one_microbatch(per_example, given["loss_target"])
    else:
        def body(carry, xs):
            loss_sum, grad_sum = carry
            l_k, (gw_k, gx_k) = one_microbatch(xs[0], xs[1])
            with _jax.named_scope("update"):
                return (loss_sum + l_k, _jax.tree.map(_jnp.add, grad_sum, gw_k)), gx_k

        init = (_jnp.zeros((), _jnp.float32), _jax.tree.map(_jnp.zeros_like, weights))
        (loss, grad_w), grad_x = _jax.lax.scan(body, init, (per_example, given["loss_target"]))
    with _jax.named_scope("update"):
        delta_w, new_m, new_v = {}, {}, {}
        for n in TWIN_WEIGHTS:
            delta_w[n], new_m[n], new_v[n] = _adamw(weights[n], grad_w[n], given["m_" + n], given["v_" + n])
    return (loss, grad_x, *[grad_w[n] for n in TWIN_WEIGHTS], *[delta_w[n] for n in TWIN_WEIGHTS],
            *[new_m[n] for n in TWIN_WEIGHTS], *[new_v[n] for n in TWIN_WEIGHTS])
```

```python
import jax
import jax.numpy as jnp
from jax import lax
from jax.experimental import pallas as pl
from jax.experimental.pallas import tpu as pltpu

F32 = jnp.float32
BF16 = jnp.bfloat16
HIGHEST = lax.Precision.HIGHEST

N_DEV = 8
LANES = 128
HEAD_DIM = 64
EPS = 1e-6
CONV_W = 3
HALO = 16
ATT_BQ = 256
ATT_BK = 128
VMEM_LIMIT = 48 * 1024 * 1024

ADAM_LR = 0.001
ADAM_B1 = 0.9
ADAM_B2 = 0.999
ADAM_EPS = 1e-08
ADAM_WD = 0.01
ADAM_STEP = 10


def _params(**kw):
    return pltpu.CompilerParams(vmem_limit_bytes=VMEM_LIMIT, **kw)


def _tile(n, cap):
    if n <= cap:
        return n
    best = None
    for t in range(LANES, cap + 1, LANES):
        if n % t == 0:
            best = t
    assert best is not None, (n, cap)
    return best


def _dot(a, b, **kw):
    return jnp.dot(a, b, preferred_element_type=F32, **kw)


def _dot_nt(a, b):
    return lax.dot_general(a, b, (((1,), (1,)), ((), ())), preferred_element_type=F32)


def _dot_tn(a, b):
    return lax.dot_general(a, b, (((0,), (0,)), ((), ())), preferred_element_type=F32)


def _exchange(arrays, scatter, name):
    n = len(arrays)
    out_shape = []
    for a in arrays:
        shp = a.shape[1:] if scatter else a.shape
        out_shape.append(jax.ShapeDtypeStruct((N_DEV,) + tuple(shp), a.dtype))

    def body(*refs):
        ins, outs = refs[:n], refs[n:2 * n]
        send_sems, recv_sems, loc_sems = refs[2 * n:]
        x, y, c = lax.axis_index("x"), lax.axis_index("y"), lax.axis_index("c")
        me = 4 * x + 2 * y + c
        copies = []
        for a in range(n):
            src = ins[a].at[me] if scatter else ins[a]
            cp = pltpu.make_async_copy(src, outs[a].at[me], loc_sems.at[a])
            cp.start()
            copies.append(cp)
        for k in range(1, N_DEV):
            px = 1 - x if k & 4 else x
            py = 1 - y if k & 2 else y
            pc = 1 - c if k & 1 else c
            peer = 4 * px + 2 * py + pc
            for a in range(n):
                src = ins[a].at[peer] if scatter else ins[a]
                cp = pltpu.make_async_remote_copy(
                    src_ref=src, dst_ref=outs[a].at[me],
                    send_sem=send_sems.at[a, k - 1], recv_sem=recv_sems.at[a, k - 1],
                    device_id=(px, py, pc), device_id_type=pl.DeviceIdType.MESH)
                cp.start()
                copies.append(cp)
        for cp in copies:
            cp.wait()

    any_spec = pl.BlockSpec(memory_space=pl.ANY)
    return pl.pallas_call(
        body, name=name, out_shape=tuple(out_shape),
        in_specs=[any_spec] * n, out_specs=tuple([any_spec] * n),
        scratch_shapes=[pltpu.SemaphoreType.DMA((n, N_DEV - 1)), pltpu.SemaphoreType.DMA((n, N_DEV - 1)),
                        pltpu.SemaphoreType.DMA((n,))],
        compiler_params=pltpu.CompilerParams(has_side_effects=True),
    )(*arrays)


def _mm(a, b, out_dtype, name, tm=512, tn=512):
    M, K = a.shape
    _, N = b.shape
    tm, tn = _tile(M, tm), _tile(N, tn)

    def body(a_ref, b_ref, o_ref):
        o_ref[...] = _dot(a_ref[...], b_ref[...]).astype(out_dtype)

    return pl.pallas_call(
        body, name=name, out_shape=jax.ShapeDtypeStruct((M, N), out_dtype),
        grid=(M // tm, N // tn),
        in_specs=[pl.BlockSpec((tm, K), lambda i, j: (i, 0)), pl.BlockSpec((K, tn), lambda i, j: (0, j))],
        out_specs=pl.BlockSpec((tm, tn), lambda i, j: (i, j)),
        compiler_params=_params(dimension_semantics=("parallel", "parallel")),
    )(a, b)


def _mm_tn(a, b, name, tm=1024, tn=640, tk=512):
    S, M = a.shape
    _, N = b.shape
    tm, tn, tk = _tile(M, tm), _tile(N, tn), _tile(S, tk)
    nk = S // tk

    def body(a_ref, b_ref, o_ref):
        @pl.when(pl.program_id(2) == 0)
        def _():
            o_ref[...] = jnp.zeros_like(o_ref)

        o_ref[...] += _dot_tn(a_ref[...], b_ref[...])

    return pl.pallas_call(
        body, name=name, out_shape=jax.ShapeDtypeStruct((M, N), F32),
        grid=(M // tm, N // tn, nk),
        in_specs=[pl.BlockSpec((tk, tm), lambda i, j, k: (k, i)), pl.BlockSpec((tk, tn), lambda i, j, k: (k, j))],
        out_specs=pl.BlockSpec((tm, tn), lambda i, j, k: (i, j)),
        compiler_params=_params(dimension_semantics=("parallel", "parallel", "arbitrary")),
    )(a, b)


def _silu(z):
    return z * (1.0 / (1.0 + jnp.exp(-z)))


def _ada_fwd(c_all, w_shard, b_shard):
    n = w_shard.shape[1]

    def body(c_ref, w_ref, b_ref, o_ref):
        o_ref[...] = _dot(_silu(c_ref[...]), w_ref[...], precision=HIGHEST) + b_ref[...]

    return pl.pallas_call(body, name="ada_fwd", out_shape=jax.ShapeDtypeStruct((N_DEV, n), F32),
                          compiler_params=_params())(c_all, w_shard, b_shard)


def _ada_bwd(c_all_t, dmod_cols):
    D = c_all_t.shape[0]
    n = dmod_cols.shape[1]

    def body(ct_ref, dm_ref, o_ref):
        sc = _silu(ct_ref[...])
        dm = dm_ref[...]
        acc = sc[:, 0:1] * dm[0:1, :]
        for b in range(1, N_DEV):
            acc = acc + sc[:, b:b + 1] * dm[b:b + 1, :]
        o_ref[...] = acc

    return pl.pallas_call(body, name="ada_bwd", out_shape=jax.ShapeDtypeStruct((D, n), F32),
                          compiler_params=_params())(c_all_t, dmod_cols)


def _row_specs(tm, widths):
    return [pl.BlockSpec((tm, w), lambda i: (i, 0)) for w in widths]


def _vec_spec(w):
    return pl.BlockSpec((1, w), lambda i: (0, 0))


def _prenorm(x, g, scale, shift, name):
    S, D = x.shape
    tm = _tile(S, 512)

    def body(x_ref, g_ref, sc_ref, sh_ref, h_ref):
        xv = x_ref[...]
        r = lax.rsqrt(jnp.mean(xv * xv, axis=-1, keepdims=True) + EPS)
        h_ref[...] = ((xv * r) * g_ref[...] * (1.0 + sc_ref[...]) + sh_ref[...]).astype(BF16)

    return pl.pallas_call(
        body, name=name, out_shape=jax.ShapeDtypeStruct((S, D), BF16), grid=(S // tm,),
        in_specs=_row_specs(tm, [D]) + [_vec_spec(D)] * 3, out_specs=_row_specs(tm, [D])[0],
        compiler_params=_params(dimension_semantics=("parallel",)),
    )(x, g, scale, shift)


def _group_ones():
    r = lax.broadcasted_iota(jnp.int32, (LANES, LANES), 0) // HEAD_DIM
    c = lax.broadcasted_iota(jnp.int32, (LANES, LANES), 1) // HEAD_DIM
    return (r == c).astype(F32)


def _headnorm_fwd(o_f, o_s, g_f, g_s):
    S, dh = o_f.shape
    tm = _tile(S, 512)

    def body(of_ref, os_ref, gf_ref, gs_ref, mix_ref):
        ones = _group_ones()
        for part, (o_ref, g_ref) in enumerate(((of_ref, gf_ref), (os_ref, gs_ref))):
            for t in range(dh // LANES):
                cols = slice(t * LANES, (t + 1) * LANES)
                o = o_ref[:, cols]
                ms = _dot(o * o, ones, precision=HIGHEST) * (1.0 / HEAD_DIM)
                mix_ref[:, part * dh + t * LANES: part * dh + (t + 1) * LANES] = (
                    o * lax.rsqrt(ms + EPS) * g_ref[:, cols]).astype(BF16)

    return pl.pallas_call(
        body, name="headnorm_fwd", out_shape=jax.ShapeDtypeStruct((S, 2 * dh), BF16), grid=(S // tm,),
        in_specs=_row_specs(tm, [dh, dh]) + [_vec_spec(dh)] * 2, out_specs=_row_specs(tm, [2 * dh])[0],
        compiler_params=_params(dimension_semantics=("parallel",)),
    )(o_f, o_s, g_f, g_s)


def _resid_prenorm(x, a_out, gate, g, scale, shift):
    S, D = x.shape
    tm = _tile(S, 512)

    def body(x_ref, a_ref, gt_ref, g_ref, sc_ref, sh_ref, x1_ref, h_ref):
        x1 = x_ref[...] + gt_ref[...] * a_ref[...]
        x1_ref[...] = x1
        r = lax.rsqrt(jnp.mean(x1 * x1, axis=-1, keepdims=True) + EPS)
        h_ref[...] = ((x1 * r) * g_ref[...] * (1.0 + sc_ref[...]) + sh_ref[...]).astype(BF16)

    return pl.pallas_call(
        body, name="resid_prenorm", grid=(S // tm,),
        out_shape=(jax.ShapeDtypeStruct((S, D), F32), jax.ShapeDtypeStruct((S, D), BF16)),
        in_specs=_row_specs(tm, [D, D]) + [_vec_spec(D)] * 4, out_specs=tuple(_row_specs(tm, [D, D])),
        compiler_params=_params(dimension_semantics=("parallel",)),
    )(x, a_out, gate, g, scale, shift)


def _shift_down(main, halo, k):
    ext = jnp.concatenate([halo, main], axis=0)
    return pltpu.roll(ext, k, 0)[halo.shape[0]:]


def _shift_up(main, halo, k):
    ext = jnp.concatenate([main, halo], axis=0)
    n = ext.shape[0]
    return pltpu.roll(ext, n - k, 0)[:main.shape[0]]


def _conv(up, up_halo, w_ref, b_ref):
    return (w_ref[2:3, :] * up + w_ref[1:2, :] * _shift_down(up, up_halo, 1)
            + w_ref[0:1, :] * _shift_down(up, up_halo, 2) + b_ref[...])


def _prev_halo_map(tm, col_off):
    step = tm // HALO
    return lambda j, i: (jnp.maximum(i * step - 1, 0), j + col_off)


def _conv_act(up, cw, cb):
    S, F2 = up.shape
    F = F2 // 2
    tm, ct = _tile(S, 256), _tile(F, 256)
    nct = F // ct

    def body(ug_ref, uv_ref, hg_ref, hv_ref, wg_ref, wv_ref, bg_ref, bv_ref, act_ref):
        first = pl.program_id(1) == 0
        hg = jnp.where(first, 0.0, hg_ref[...])
        hv = jnp.where(first, 0.0, hv_ref[...])
        u_g = _conv(ug_ref[...], hg, wg_ref, bg_ref)
        u_v = _conv(uv_ref[...], hv, wv_ref, bv_ref)
        act_ref[...] = (_silu(u_g) * u_v).astype(BF16)

    main = lambda off: pl.BlockSpec((tm, ct), lambda j, i: (i, j + off))
    halo = lambda off: pl.BlockSpec((HALO, ct), _prev_halo_map(tm, off))
    wspec = lambda off: pl.BlockSpec((CONV_W, ct), lambda j, i: (0, j + off))
    bspec = lambda off: pl.BlockSpec((1, ct), lambda j, i: (0, j + off))
    return pl.pallas_call(
        body, name="conv_act", out_shape=jax.ShapeDtypeStruct((S, F), BF16), grid=(nct, S // tm),
        in_specs=[main(0), main(nct), halo(0), halo(nct), wspec(0), wspec(nct), bspec(0), bspec(nct)],
        out_specs=pl.BlockSpec((tm, ct), lambda j, i: (i, j)),
        compiler_params=_params(dimension_semantics=("parallel", "parallel")),
    )(up, up, up, up, cw, cw, cb, cb)


def _conv_act_bwd(dact, up, cw, cb):
    S, F2 = up.shape
    F = F2 // 2
    tm, ct = _tile(S, 256), _tile(F, 256)
    nct = F // ct

    def body(da_ref, ug_ref, uv_ref, hg_ref, hv_ref, wg_ref, wv_ref, bg_ref, bv_ref,
             dug_ref, duv_ref, pg_ref, pv_ref):
        first = pl.program_id(1) == 0

        @pl.when(first)
        def _():
            pg_ref[...] = jnp.zeros_like(pg_ref)
            pv_ref[...] = jnp.zeros_like(pv_ref)

        da = da_ref[...]
        taps = []
        for u_ref, h_ref in ((ug_ref, hg_ref), (uv_ref, hv_ref)):
            h = jnp.where(first, 0.0, h_ref[...])
            uu = u_ref[...]
            taps.append((_shift_down(uu, h, 2), _shift_down(uu, h, 1), uu))
        u_g = wg_ref[0:1, :] * taps[0][0] + wg_ref[1:2, :] * taps[0][1] + wg_ref[2:3, :] * taps[0][2] + bg_ref[...]
        u_v = wv_ref[0:1, :] * taps[1][0] + wv_ref[1:2, :] * taps[1][1] + wv_ref[2:3, :] * taps[1][2] + bv_ref[...]
        sg = 1.0 / (1.0 + jnp.exp(-u_g))
        du_g = da * u_v * (sg * (1.0 + u_g * (1.0 - sg)))
        du_v = da * (u_g * sg)
        dug_ref[...] = du_g.astype(BF16)
        duv_ref[...] = du_v.astype(BF16)
        for du, tp, p_ref in ((du_g, taps[0], pg_ref), (du_v, taps[1], pv_ref)):
            for k in range(CONV_W):
                p_ref[k:k + 1, :] += jnp.sum(du * tp[k], axis=0, keepdims=True)
            p_ref[CONV_W:CONV_W + 1, :] += jnp.sum(du, axis=0, keepdims=True)

    main = lambda off: pl.BlockSpec((tm, ct), lambda j, i: (i, j + off))
    halo = lambda off: pl.BlockSpec((HALO, ct), _prev_halo_map(tm, off))
    wspec = lambda off: pl.BlockSpec((CONV_W, ct), lambda j, i: (0, j + off))
    bspec = lambda off: pl.BlockSpec((1, ct), lambda j, i: (0, j + off))
    pspec = lambda off: pl.BlockSpec((8, ct), lambda j, i: (0, j + off))
    du_g, du_v, p_g, p_v = pl.pallas_call(
        body, name="conv_act_bwd", grid=(nct, S // tm),
        out_shape=(jax.ShapeDtypeStruct((S, F), BF16), jax.ShapeDtypeStruct((S, F), BF16),
                   jax.ShapeDtypeStruct((8, F), F32), jax.ShapeDtypeStruct((8, F), F32)),
        in_specs=[main(0), main(0), main(nct), halo(0), halo(nct), wspec(0), wspec(nct), bspec(0), bspec(nct)],
        out_specs=(main(0), main(0), pspec(0), pspec(0)),
        compiler_params=_params(dimension_semantics=("parallel", "arbitrary")),
    )(dact, up, up, up, up, cw, cw, cb, cb)
    return du_g, du_v, p_g, p_v


def _conv_bwd_input(du, cw):
    S, C = du.shape
    tm, ct = _tile(S, 256), _tile(C, 256)
    step = tm // HALO
    last_halo = S // HALO - 1

    def body(du_ref, h_ref, w_ref, o_ref):
        last = pl.program_id(1) == pl.num_programs(1) - 1
        d = du_ref[...].astype(F32)
        h = jnp.where(last, 0.0, h_ref[...].astype(F32))
        o_ref[...] = (w_ref[2:3, :] * d + w_ref[1:2, :] * _shift_up(d, h, 1)
                      + w_ref[0:1, :] * _shift_up(d, h, 2)).astype(BF16)

    return pl.pallas_call(
        body, name="conv_bwd_input", out_shape=jax.ShapeDtypeStruct((S, C), BF16), grid=(C // ct, S // tm),
        in_specs=[pl.BlockSpec((tm, ct), lambda j, i: (i, j)),
                  pl.BlockSpec((HALO, ct), lambda j, i: (jnp.minimum((i + 1) * step, last_halo), j)),
                  pl.BlockSpec((CONV_W, ct), lambda j, i: (0, j))],
        out_specs=pl.BlockSpec((tm, ct), lambda j, i: (i, j)),
        compiler_params=_params(dimension_semantics=("parallel", "parallel")),
    )(du, du, cw)


def _scan_mats(R, nc, reverse):
    i = lax.broadcasted_iota(jnp.int32, (LANES, LANES), 0)
    j = lax.broadcasted_iota(jnp.int32, (LANES, LANES), 1)
    inner = ((i >= j) if reverse else (i <= j)).astype(F32)
    r = lax.broadcasted_iota(jnp.int32, (R, R), 0)
    c = lax.broadcasted_iota(jnp.int32, (R, R), 1)
    same = (r // nc) == (c // nc)
    outer = (same & ((c > r) if reverse else (c < r))).astype(F32)
    return inner, outer


def _chunk_scan(v, inner, outer, reverse):
    w = _dot(v, inner, precision=HIGHEST)
    col = 0 if reverse else LANES - 1
    carry = _dot(outer, w, precision=HIGHEST)[:, col:col + 1]
    return w + carry


def _fgate_fwd(z_rows, nc):
    R = z_rows.shape[0]

    def body(z_ref, f_ref):
        z = z_ref[...]
        logf = jnp.minimum(z, 0.0) - jnp.log(1.0 + jnp.exp(-jnp.abs(z)))
        inner, outer = _scan_mats(R, nc, False)
        f_ref[...] = _chunk_scan(logf, inner, outer, False)

    return pl.pallas_call(body, name="fgate_fwd", out_shape=jax.ShapeDtypeStruct((R, LANES), F32),
                          compiler_params=_params())(z_rows)


def _fgate_bwd(dfk_rows, dfq_rows, z_rows, nc):
    R = z_rows.shape[0]
    nh = R // nc

    def body(dfk_ref, dfq_ref, z_ref, dz_ref, db_ref):
        inner, outer = _scan_mats(R, nc, True)
        dlogf = _chunk_scan(dfk_ref[...] + dfq_ref[...], inner, outer, True)
        dz = dlogf * (1.0 / (1.0 + jnp.exp(z_ref[...])))
        dz_ref[...] = dz
        hr = lax.broadcasted_iota(jnp.int32, (nh, R), 0)
        hc = lax.broadcasted_iota(jnp.int32, (nh, R), 1) // nc
        per_head = _dot((hr == hc).astype(F32), dz, precision=HIGHEST)
        db_ref[...] = jnp.sum(per_head, axis=1, keepdims=True)

    return pl.pallas_call(
        body, name="fgate_bwd",
        out_shape=(jax.ShapeDtypeStruct((R, LANES), F32), jax.ShapeDtypeStruct((nh, 1), F32)),
        compiler_params=_params())(dfk_rows, dfq_rows, z_rows)


def _lane_lo():
    return lax.broadcasted_iota(jnp.int32, (1, LANES), 1) < HEAD_DIM


def _causal_mask(i, j, strict):
    row = lax.broadcasted_iota(jnp.int32, (ATT_BQ, ATT_BK), 0) + i * ATT_BQ
    col = lax.broadcasted_iota(jnp.int32, (ATT_BQ, ATT_BK), 1) + j * ATT_BK
    return (col < row) if strict else (col <= row)


def _kv_slice(j):
    return pl.ds(pl.multiple_of(j * ATT_BK, ATT_BK), ATT_BK)


def _split_heads(t, lo):
    zero = jnp.zeros_like(t)
    return jnp.where(lo, t, zero), jnp.where(lo, zero, t)


def _pair_cols(vals):
    two = lax.broadcasted_iota(jnp.int32, (vals[0].shape[0], 2), 1)
    return jnp.where(two == 0, vals[0], vals[1])


_NEG = -1e30
_SCALE = HEAD_DIM ** -0.5
_RATIO = ATT_BQ // ATT_BK


def _att_specs(S, n_pairs, q_col, k_col, v_col):
    qs = pl.BlockSpec((ATT_BQ, LANES), lambda p, i: (i, q_col + p))
    ks = pl.BlockSpec((S, LANES), lambda p, i: (0, k_col + p))
    vs = pl.BlockSpec((S, LANES), lambda p, i: (0, v_col + p))
    return qs, ks, vs


def _row_block(off=0):
    return pl.BlockSpec((ATT_BQ, LANES), lambda p, i: (i, p + off))


def _fox_fwd(qkv, fcol, frow, n_pairs):
    S = qkv.shape[0]
    nq = S // ATT_BQ

    def body(q_ref, k_ref, v_ref, fc_ref, fr_ref, o_ref, lse_ref):
        i = pl.program_id(1)
        lo = _lane_lo()
        qh = _split_heads(q_ref[...] * _SCALE, lo)
        fc = fc_ref[...]
        fch = (fc[:, 0:1], fc[:, 1:2])

        def step(j, carry, masked):
            m0, l0, m1, l1, acc = carry
            ks = _kv_slice(j)
            k, v = k_ref[ks, :], v_ref[ks, :]
            fr = fr_ref[:, ks]
            mask = _causal_mask(i, j, False) if masked else None
            new, alphas, pvs = [], [], []
            for hd, (m, l) in enumerate(((m0, l0), (m1, l1))):
                s = _dot_nt(qh[hd], k) + fch[hd] - fr[hd:hd + 1, :]
                if masked:
                    s = jnp.where(mask, s, _NEG)
                mn = jnp.maximum(m, jnp.max(s, axis=1, keepdims=True))
                alpha = jnp.exp(m - mn)
                p = jnp.exp(s - mn)
                new += [mn, alpha * l + jnp.sum(p, axis=1, keepdims=True)]
                alphas.append(alpha)
                pvs.append(_dot(p.astype(BF16), v))
            acc = acc * jnp.where(lo, alphas[0], alphas[1]) + jnp.where(lo, pvs[0], pvs[1])
            return (new[0], new[1], new[2], new[3], acc)

        col = jnp.full((ATT_BQ, 1), _NEG, F32)
        zcol = jnp.zeros((ATT_BQ, 1), F32)
        carry = (col, zcol, col, zcol, jnp.zeros((ATT_BQ, LANES), F32))
        carry = lax.fori_loop(0, i * _RATIO, lambda j, cr: step(j, cr, False), carry)
        for jj in range(_RATIO):
            carry = step(i * _RATIO + jj, carry, True)
        m0, l0, m1, l1, acc = carry
        o_ref[...] = acc / jnp.where(lo, l0, l1)
        lse_ref[...] = _pair_cols((m0 + jnp.log(l0), m1 + jnp.log(l1)))

    qs, ks, vs = _att_specs(S, n_pairs, 0, n_pairs, 2 * n_pairs)
    return pl.pallas_call(
        body, name="fox_fwd", grid=(n_pairs, nq),
        out_shape=(jax.ShapeDtypeStruct((S, n_pairs * LANES), F32), jax.ShapeDtypeStruct((n_pairs, S, 2), F32)),
        in_specs=[qs, ks, vs, pl.BlockSpec((None, ATT_BQ, 2), lambda p, i: (p, i, 0)),
                  pl.BlockSpec((None, 2, S), lambda p, i: (p, 0, 0))],
        out_specs=(_row_block(), pl.BlockSpec((None, ATT_BQ, 2), lambda p, i: (p, i, 0))),
        compiler_params=_params(dimension_semantics=("parallel", "parallel")),
    )(qkv, qkv, qkv, fcol, frow)


def _fox_bwd(qkv, fcol, frow, o, lse, do, n_pairs):
    S = qkv.shape[0]
    nq = S // ATT_BQ

    def body(q_ref, k_ref, v_ref, fc_ref, fr_ref, o_ref, lse_ref, do_ref,
             dq_ref, dk_ref, dv_ref, dfr_ref, dfc_ref):
        i = pl.program_id(1)

        @pl.when(i == 0)
        def _():
            dk_ref[...] = jnp.zeros_like(dk_ref)
            dv_ref[...] = jnp.zeros_like(dv_ref)
            dfr_ref[...] = jnp.zeros_like(dfr_ref)

        lo = _lane_lo()
        qh = _split_heads(q_ref[...] * _SCALE, lo)
        fc, lse = fc_ref[...], lse_ref[...]
        fch = (fc[:, 0:1], fc[:, 1:2])
        lseh = (lse[:, 0:1], lse[:, 1:2])
        do = do_ref[...]
        dd = do * o_ref[...]
        zero = jnp.zeros_like(dd)
        delta = (jnp.sum(jnp.where(lo, dd, zero), axis=1, keepdims=True),
                 jnp.sum(jnp.where(lo, zero, dd), axis=1, keepdims=True))
        doh = _split_heads(do.astype(BF16), lo)
        two = lax.broadcasted_iota(jnp.int32, (2, ATT_BK), 0)

        def step(j, carry, masked):
            dq_acc, rs0, rs1 = carry
            ks = _kv_slice(j)
            k, v = k_ref[ks, :], v_ref[ks, :]
            fr = fr_ref[:, ks]
            mask = _causal_mask(i, j, False) if masked else None
            dqs, dk_add, dv_add, cs, rs = [], None, None, [], []
            for hd in range(2):
                s = _dot_nt(qh[hd], k) + fch[hd] - fr[hd:hd + 1, :]
                if masked:
                    s = jnp.where(mask, s, _NEG)
                p = jnp.exp(s - lseh[hd])
                dp = _dot_nt(doh[hd], v)
                ds = p * (dp - delta[hd])
                dsb = ds.astype(BF16)
                dqs.append(_dot(dsb, k))
                dk_h = _dot_tn(dsb, qh[hd])
                dv_h = _dot_tn(p.astype(BF16), doh[hd])
                dk_add = dk_h if dk_add is None else dk_add + dk_h
                dv_add = dv_h if dv_add is None else dv_add + dv_h
                cs.append(jnp.sum(ds, axis=0, keepdims=True))
                rs.append(jnp.sum(ds, axis=1, keepdims=True))
            dk_ref[ks, :] += dk_add
            dv_ref[ks, :] += dv_add
            dfr_ref[:, ks] -= jnp.where(two == 0, cs[0], cs[1])
            return (dq_acc + jnp.where(lo, dqs[0], dqs[1]), rs0 + rs[0], rs1 + rs[1])

        zcol = jnp.zeros((ATT_BQ, 1), F32)
        carry = (jnp.zeros((ATT_BQ, LANES), F32), zcol, zcol)
        carry = lax.fori_loop(0, i * _RATIO, lambda j, cr: step(j, cr, False), carry)
        for jj in range(_RATIO):
            carry = step(i * _RATIO + jj, carry, True)
        dq_ref[...] = (carry[0] * _SCALE).astype(BF16)
        dfc_ref[...] = _pair_cols((carry[1], carry[2]))

    qs, ks, vs = _att_specs(S, n_pairs, 0, n_pairs, 2 * n_pairs)
    col2 = pl.BlockSpec((None, ATT_BQ, 2), lambda p, i: (p, i, 0))
    row2 = pl.BlockSpec((None, 2, S), lambda p, i: (p, 0, 0))
    full = pl.BlockSpec((S, LANES), lambda p, i: (0, p))
    dh = n_pairs * LANES
    return pl.pallas_call(
        body, name="fox_bwd", grid=(n_pairs, nq),
        out_shape=(jax.ShapeDtypeStruct((S, dh), BF16), jax.ShapeDtypeStruct((S, dh), F32),
                   jax.ShapeDtypeStruct((S, dh), F32), jax.ShapeDtypeStruct((n_pairs, 2, S), F32),
                   jax.ShapeDtypeStruct((n_pairs, S, 2), F32)),
        in_specs=[qs, ks, vs, col2, row2, _row_block(), col2, _row_block()],
        out_specs=(_row_block(), full, full, row2, col2),
        compiler_params=_params(dimension_semantics=("parallel", "arbitrary")),
    )(qkv, qkv, qkv, fcol, frow, o, lse, do)


def _scan_rhs():
    r = lax.broadcasted_iota(jnp.int32, (2 * ATT_BK, ATT_BK), 0) % ATT_BK
    c = lax.broadcasted_iota(jnp.int32, (2 * ATT_BK, ATT_BK), 1)
    return (r >= c).astype(BF16)


def _suffix_sum(t, rhs):
    hi = t.astype(BF16)
    lo = (t - hi.astype(F32)).astype(BF16)
    return _dot(jnp.concatenate([hi, lo], axis=1), rhs)


def _sb_scores(qh, k, mask):
    z = _dot_nt(qh, k)
    e = jnp.exp(-jnp.abs(z))
    lb = -(jnp.maximum(z, 0.0) + jnp.log(1.0 + e))
    if mask is not None:
        lb = jnp.where(mask, lb, 0.0)
    return z, e, lb


def _sb_fwd(qkv, n_pairs):
    S = qkv.shape[0]
    nq = S // ATT_BQ

    def body(q_ref, k_ref, v_ref, o_ref):
        i = pl.program_id(1)
        lo = _lane_lo()
        qh = _split_heads(q_ref[...] * _SCALE, lo)
        rhs = _scan_rhs()

        def step(j, carry, masked):
            c0, c1, acc = carry
            ks = _kv_slice(j)
            k, v = k_ref[ks, :], v_ref[ks, :]
            mask = _causal_mask(i, j, True) if masked else None
            cn, pvs = [], []
            for hd, c in enumerate((c0, c1)):
                z, _, lb = _sb_scores(qh[hd], k, mask)
                rin = _suffix_sum(lb, rhs)
                a = jnp.exp(z + rin + c)
                if masked:
                    a = jnp.where(mask, a, 0.0)
                pvs.append(_dot(a.astype(BF16), v))
                cn.append(c + rin[:, 0:1])
            return (cn[0], cn[1], acc + jnp.where(lo, pvs[0], pvs[1]))

        zcol = jnp.zeros((ATT_BQ, 1), F32)
        carry = (zcol, zcol, jnp.zeros((ATT_BQ, LANES), F32))
        for jj in reversed(range(_RATIO)):
            carry = step(i * _RATIO + jj, carry, True)
        n_full = i * _RATIO
        carry = lax.fori_loop(0, n_full, lambda n, cr: step(n_full - 1 - n, cr, False), carry)
        o_ref[...] = carry[2]

    qs, ks, vs = _att_specs(S, n_pairs, 3 * n_pairs, 4 * n_pairs, 5 * n_pairs)
    return pl.pallas_call(
        body, name="sb_fwd", grid=(n_pairs, nq),
        out_shape=jax.ShapeDtypeStruct((S, n_pairs * LANES), F32),
        in_specs=[qs, ks, vs], out_specs=_row_block(),
        compiler_params=_params(dimension_semantics=("parallel", "parallel")),
    )(qkv, qkv, qkv)


def _sb_bwd(qkv, o, do, n_pairs):
    S = qkv.shape[0]
    nq = S // ATT_BQ

    def body(q_ref, k_ref, v_ref, o_ref, do_ref, dq_ref, dk_ref, dv_ref):
        i = pl.program_id(1)

        @pl.when(i == 0)
        def _():
            dk_ref[...] = jnp.zeros_like(dk_ref)
            dv_ref[...] = jnp.zeros_like(dv_ref)

        lo = _lane_lo()
        qh = _split_heads(q_ref[...] * _SCALE, lo)
        rhs = _scan_rhs()
        do = do_ref[...].astype(BF16)
        dd = do.astype(F32) * o_ref[...]
        zero = jnp.zeros_like(dd)
        delta = (jnp.sum(jnp.where(lo, dd, zero), axis=1, keepdims=True),
                 jnp.sum(jnp.where(lo, zero, dd), axis=1, keepdims=True))
        doh = _split_heads(do, lo)

        def step(j, carry, masked):
            c0, g0, c1, g1, dq_acc = carry
            ks = _kv_slice(j)
            k, v = k_ref[ks, :], v_ref[ks, :]
            mask = _causal_mask(i, j, True) if masked else None
            new, dqs, dk_add, dv_add = [], [], None, None
            for hd, (c, g) in enumerate(((c0, g0), (c1, g1))):
                z, e, lb = _sb_scores(qh[hd], k, mask)
                rin = _suffix_sum(lb, rhs)
                a = jnp.exp(z + rin + c)
                if masked:
                    a = jnp.where(mask, a, 0.0)
                ab = a.astype(BF16)
                gg = ab.astype(F32) * _dot_nt(doh[hd], v)
                rgin = _suffix_sum(gg, rhs)
                rinv = 1.0 / (1.0 + e)
                sig = jnp.where(z >= 0.0, rinv, e * rinv)
                dz = gg - sig * (delta[hd] - g - (rgin - gg))
                if masked:
                    dz = jnp.where(mask, dz, 0.0)
                dzb = dz.astype(BF16)
                dqs.append(_dot(dzb, k))
                dk_h = _dot_tn(dzb, qh[hd])
                dv_h = _dot_tn(ab, doh[hd])
                dk_add = dk_h if dk_add is None else dk_add + dk_h
                dv_add = dv_h if dv_add is None else dv_add + dv_h
                new += [c + rin[:, 0:1], g + rgin[:, 0:1]]
            dk_ref[ks, :] += dk_add
            dv_ref[ks, :] += dv_add
            return (new[0], new[1], new[2], new[3], dq_acc + jnp.where(lo, dqs[0], dqs[1]))

        zcol = jnp.zeros((ATT_BQ, 1), F32)
        carry = (zcol, zcol, zcol, zcol, jnp.zeros((ATT_BQ, LANES), F32))
        for jj in reversed(range(_RATIO)):
            carry = step(i * _RATIO + jj, carry, True)
        n_full = i * _RATIO
        carry = lax.fori_loop(0, n_full, lambda n, cr: step(n_full - 1 - n, cr, False), carry)
        dq_ref[...] = (carry[4] * _SCALE).astype(BF16)

    qs, ks, vs = _att_specs(S, n_pairs, 3 * n_pairs, 4 * n_pairs, 5 * n_pairs)
    full = pl.BlockSpec((S, LANES), lambda p, i: (0, p))
    dh = n_pairs * LANES
    return pl.pallas_call(
        body, name="sb_bwd", grid=(n_pairs, nq),
        out_shape=(jax.ShapeDtypeStruct((S, dh), BF16), jax.ShapeDtypeStruct((S, dh), F32),
                   jax.ShapeDtypeStruct((S, dh), F32)),
        in_specs=[qs, ks, vs, _row_block(), _row_block()],
        out_specs=(_row_block(), full, full),
        compiler_params=_params(dimension_semantics=("parallel", "arbitrary")),
    )(qkv, qkv, qkv, o, do)


def _acc_spec(w):
    return pl.BlockSpec((1, w), lambda i: (0, 0))


def _loss_head(x1, m_out, gate_m, g_final, target):
    S, D = x1.shape
    tm = _tile(S, 256)

    def body(x1_ref, mo_ref, gt_ref, gf_ref, tg_ref, dx2_ref, gm_ref, loss_ref, dgf_ref, dgt_ref):
        @pl.when(pl.program_id(0) == 0)
        def _():
            loss_ref[...] = jnp.zeros_like(loss_ref)
            dgf_ref[...] = jnp.zeros_like(dgf_ref)
            dgt_ref[...] = jnp.zeros_like(dgt_ref)

        mo = mo_ref[...]
        x2 = x1_ref[...] + gt_ref[...] * mo
        r = lax.rsqrt(jnp.mean(x2 * x2, axis=-1, keepdims=True) + EPS)
        xh = x2 * r
        diff = xh * gf_ref[...] - tg_ref[...]
        loss_ref[...] += (0.5 / D) * jnp.sum(diff * diff)
        dy = diff * (1.0 / D)
        dgf_ref[...] += jnp.sum(dy * xh, axis=0, keepdims=True)
        dxh = dy * gf_ref[...]
        dx2 = r * (dxh - xh * jnp.mean(dxh * xh, axis=-1, keepdims=True))
        dx2_ref[...] = dx2
        gm_ref[...] = (dx2 * gt_ref[...]).astype(BF16)
        dgt_ref[...] += jnp.sum(dx2 * mo, axis=0, keepdims=True)

    return pl.pallas_call(
        body, name="loss_head", grid=(S // tm,),
        out_shape=(jax.ShapeDtypeStruct((S, D), F32), jax.ShapeDtypeStruct((S, D), BF16),
                   jax.ShapeDtypeStruct((1, LANES), F32), jax.ShapeDtypeStruct((1, D), F32),
                   jax.ShapeDtypeStruct((1, D), F32)),
        in_specs=_row_specs(tm, [D, D]) + [_vec_spec(D)] * 2 + _row_specs(tm, [D]),
        out_specs=tuple(_row_specs(tm, [D, D]) + [_acc_spec(LANES), _acc_spec(D), _acc_spec(D)]),
        compiler_params=_params(dimension_semantics=("arbitrary",)),
    )(x1, m_out, gate_m, g_final, target)


def _norm_bwd(dh, xin, dres, g, scale, name, gate=None, branch=None):
    S, D = xin.shape
    tm = _tile(S, 256)
    gated = gate is not None

    def body(*refs):
        if gated:
            (dh_ref, x_ref, dr_ref, g_ref, sc_ref, gt_ref, br_ref,
             dx_ref, dsc_ref, dsh_ref, dg_ref, ga_ref, dgt_ref) = refs
            sums = (dsc_ref, dsh_ref, dg_ref, dgt_ref)
        else:
            dh_ref, x_ref, dr_ref, g_ref, sc_ref, dx_ref, dsc_ref, dsh_ref, dg_ref = refs
            sums = (dsc_ref, dsh_ref, dg_ref)

        @pl.when(pl.program_id(0) == 0)
        def _():
            for s_ref in sums:
                s_ref[...] = jnp.zeros_like(s_ref)

        dhv, xv = dh_ref[...], x_ref[...]
        r = lax.rsqrt(jnp.mean(xv * xv, axis=-1, keepdims=True) + EPS)
        xh = xv * r
        dsc_ref[...] += jnp.sum(dhv * (xh * g_ref[...]), axis=0, keepdims=True)
        dsh_ref[...] += jnp.sum(dhv, axis=0, keepdims=True)
        dn = dhv * (1.0 + sc_ref[...])
        dg_ref[...] += jnp.sum(dn * xh, axis=0, keepdims=True)
        dxh = dn * g_ref[...]
        dx = dr_ref[...] + r * (dxh - xh * jnp.mean(dxh * xh, axis=-1, keepdims=True))
        dx_ref[...] = dx
        if gated:
            ga_ref[...] = (dx * gt_ref[...]).astype(BF16)
            dgt_ref[...] += jnp.sum(dx * br_ref[...], axis=0, keepdims=True)

    vec = jax.ShapeDtypeStruct((1, D), F32)
    out_shape = [jax.ShapeDtypeStruct((S, D), F32), vec, vec, vec]
    out_specs = _row_specs(tm, [D]) + [_acc_spec(D)] * 3
    in_specs = _row_specs(tm, [D, D, D]) + [_vec_spec(D)] * 2
    args = [dh, xin, dres, g, scale]
    if gated:
        out_shape += [jax.ShapeDtypeStruct((S, D), BF16), vec]
        out_specs += _row_specs(tm, [D]) + [_acc_spec(D)]
        in_specs += [_vec_spec(D)] + _row_specs(tm, [D])
        args += [gate, branch]
    return pl.pallas_call(
        body, name=name, grid=(S // tm,), out_shape=tuple(out_shape),
        in_specs=in_specs, out_specs=tuple(out_specs),
        compiler_params=_params(dimension_semantics=("arbitrary",)),
    )(*args)


def _headnorm_bwd(dmix, o_f, o_s, g_f, g_s):
    S, dh = o_f.shape
    tm = _tile(S, 256)

    def body(dm_ref, of_ref, os_ref, gf_ref, gs_ref, dof_ref, dos_ref, dgf_ref, dgs_ref):
        @pl.when(pl.program_id(0) == 0)
        def _():
            dgf_ref[...] = jnp.zeros_like(dgf_ref)
            dgs_ref[...] = jnp.zeros_like(dgs_ref)

        ones = _group_ones()
        parts = ((of_ref, gf_ref, dof_ref, dgf_ref), (os_ref, gs_ref, dos_ref, dgs_ref))
        for part, (o_ref, g_ref, do_ref, dg_ref) in enumerate(parts):
            for t in range(dh // LANES):
                cols = slice(t * LANES, (t + 1) * LANES)
                o = o_ref[:, cols]
                dm = dm_ref[:, part * dh + t * LANES: part * dh + (t + 1) * LANES]
                r = lax.rsqrt(_dot(o * o, ones, precision=HIGHEST) * (1.0 / HEAD_DIM) + EPS)
                oh = o * r
                dg_ref[:, cols] += jnp.sum(dm * oh, axis=0, keepdims=True)
                dn = dm * g_ref[:, cols]
                mean = _dot(dn * oh, ones, precision=HIGHEST) * (1.0 / HEAD_DIM)
                do_ref[:, cols] = r * (dn - oh * mean)

    vec = jax.ShapeDtypeStruct((1, dh), F32)
    return pl.pallas_call(
        body, name="headnorm_bwd", grid=(S // tm,),
        out_shape=(jax.ShapeDtypeStruct((S, dh), F32), jax.ShapeDtypeStruct((S, dh), F32), vec, vec),
        in_specs=_row_specs(tm, [2 * dh, dh, dh]) + [_vec_spec(dh)] * 2,
        out_specs=tuple(_row_specs(tm, [dh, dh]) + [_acc_spec(dh)] * 2),
        compiler_params=_params(dimension_semantics=("arbitrary",)),
    )(dmix, o_f, o_s, g_f, g_s)


def _adamw(w, gslots, m, v, name):
    R, C = w.shape
    n = gslots.shape[0]
    tr = 256 if (R % 256 == 0 and R > 256) else R
    bc1 = 1.0 - ADAM_B1 ** ADAM_STEP
    bc2 = 1.0 - ADAM_B2 ** ADAM_STEP

    def body(w_ref, gs_ref, m_ref, v_ref, g_ref, d_ref, nm_ref, nv_ref):
        g = gs_ref[0]
        for s in range(1, n):
            g = g + gs_ref[s]
        nm = ADAM_B1 * m_ref[...] + (1.0 - ADAM_B1) * g
        nv = ADAM_B2 * v_ref[...] + (1.0 - ADAM_B2) * (g * g)
        g_ref[...] = g
        nm_ref[...] = nm
        nv_ref[...] = nv
        d_ref[...] = -ADAM_LR * ((nm / bc1) / (jnp.sqrt(nv / bc2) + ADAM_EPS) + ADAM_WD * w_ref[...])

    blk = pl.BlockSpec((tr, C), lambda i: (i, 0))
    sds = jax.ShapeDtypeStruct((R, C), F32)
    return pl.pallas_call(
        body, name=name, grid=(R // tr,), out_shape=(sds,) * 4,
        in_specs=[blk, pl.BlockSpec((n, tr, C), lambda i: (0, i, 0)), blk, blk], out_specs=(blk,) * 4,
        compiler_params=_params(dimension_semantics=("parallel",)),
    )(w, gslots, m, v)


def _slot_sum(slots, name):
    n, _, C = slots.shape

    def body(s_ref, o_ref):
        acc = s_ref[0]
        for s in range(1, n):
            acc = acc + s_ref[s]
        o_ref[...] = acc

    return pl.pallas_call(body, name=name, out_shape=jax.ShapeDtypeStruct((1, C), F32),
                          compiler_params=_params())(slots)


def _pad_cols(a, n):
    return jnp.pad(a, ((0, 0), (0, n - a.shape[1])))


def _ungather(g, axis):
    if axis == 0:
        return g.reshape(g.shape[0] * g.shape[1], g.shape[2])
    return jnp.transpose(g, (1, 0, 2)).reshape(g.shape[1], g.shape[0] * g.shape[2])


def _to_slots(full, axis):
    R, C = full.shape
    if axis == 0:
        return full.reshape(N_DEV, R // N_DEV, C)
    return jnp.transpose(full.reshape(R, N_DEV, C // N_DEV), (1, 0, 2))


def kernel(x, c, w_ada, b_ada, g_attn, w_in, b_fgate, g_out_fox, g_out_sb, w_out, g_mlp, w_up, conv_w, conv_b, w_down, g_final, loss_target, m_w_ada, m_b_ada, m_g_attn, m_w_in, m_b_fgate, m_g_out_fox, m_g_out_sb, m_w_out, m_g_mlp, m_w_up, m_conv_w, m_conv_b, m_w_down, m_g_final, v_w_ada, v_b_ada, v_g_attn, v_w_in, v_b_fgate, v_g_out_fox, v_g_out_sb, v_w_out, v_g_mlp, v_w_up, v_conv_w, v_conv_b, v_w_down, v_g_final):
    S, D = x.shape[1], x.shape[2]
    dh = D // 2
    n_pairs = dh // LANES
    n_heads = dh // HEAD_DIM
    n_qkv = 6 * dh
    n_in = w_in.shape[2] * N_DEV
    ff = w_down.shape[1] * N_DEV
    ffp = -(-ff // (2 * LANES)) * (2 * LANES)
    nc = S // LANES
    me = 4 * lax.axis_index("x") + 2 * lax.axis_index("y") + lax.axis_index("c")
    xs, tgt = x[0], loss_target[0]

    c_all, win_g, wout_g, wup_g, wdown_g, convw_g = _exchange(
        [c, w_in[0].astype(BF16), w_out[0].astype(BF16), w_up[0].astype(BF16), w_down[0].astype(BF16), conv_w[0]],
        scatter=False, name="gather_weights")
    c_all = c_all.reshape(N_DEV, D)
    W_in = _ungather(win_g, 1)
    W_qkv, W_f = W_in[:, :n_qkv], _pad_cols(W_in[:, n_qkv:], LANES)
    W_inp = jnp.concatenate([W_qkv, W_f], axis=1)
    W_out = _ungather(wout_g, 0)
    W_up = _ungather(wup_g, 1)
    W_upp = jnp.concatenate([_pad_cols(W_up[:, :ff], ffp), _pad_cols(W_up[:, ff:], ffp)], axis=1)
    W_down = jnp.pad(_ungather(wdown_g, 0), ((0, ffp - ff), (0, 0)))
    cw_full = _ungather(convw_g, 1)
    cw = jnp.concatenate([_pad_cols(cw_full[:, :ff], ffp), _pad_cols(cw_full[:, ff:], ffp)], axis=1)
    cb = jnp.concatenate([_pad_cols(conv_b[:, :ff], ffp), _pad_cols(conv_b[:, ff:], ffp)], axis=1)

    n_ada = w_ada.shape[2]
    b_shard = lax.dynamic_slice(b_ada, (0, me * n_ada), (1, n_ada))
    mod_cols = _ada_fwd(c_all, w_ada[0], b_shard)
    (mod_g,) = _exchange([mod_cols], scatter=False, name="gather_mod")
    mod = lax.dynamic_index_in_dim(mod_g, me, axis=1, keepdims=False).reshape(6, 1, D)
    shift_a, scale_a, gate_a, shift_m, scale_m, gate_m = [mod[k] for k in range(6)]

    h1 = _prenorm(xs, g_attn, scale_a, shift_a, "prenorm_attn")
    qkv = _mm(h1, W_qkv, BF16, "proj_qkv")
    flog = _mm(h1, W_f, F32, "proj_fgate")
    zf = flog[:, :n_heads] + b_fgate
    z_rows = zf.T.reshape(n_heads * nc, LANES)
    f_rows = _fgate_fwd(z_rows, nc).reshape(n_heads, S)
    frow = f_rows.reshape(n_pairs, 2, S)
    fcol = jnp.transpose(frow, (0, 2, 1))
    o_f, lse = _fox_fwd(qkv, fcol, frow, n_pairs)
    o_s = _sb_fwd(qkv, n_pairs)
    mix = _headnorm_fwd(o_f, o_s, g_out_fox, g_out_sb)
    a_out = _mm(mix, W_out, F32, "proj_out")
    x1, h2 = _resid_prenorm(xs, a_out, gate_a, g_mlp, scale_m, shift_m)
    up = _mm(h2, W_upp, F32, "proj_up")
    act = _conv_act(up, cw, cb)
    m_out = _mm(act, W_down, F32, "proj_down")

    dx2, gm, loss_p, dg_final, dgate_m = _loss_head(x1, m_out, gate_m, g_final.reshape(1, D), tgt)
    dact = _mm(gm, W_down.T, F32, "bwd_down_act")
    dW_down = _mm_tn(act, gm, "bwd_down_w")
    du_g, du_v, p_g, p_v = _conv_act_bwd(dact, up, cw, cb)
    du = jnp.concatenate([du_g, du_v], axis=1)
    dup = _conv_bwd_input(du, cw)
    dh2 = _mm(dup, W_upp.T, F32, "bwd_up_act")
    dW_up = _mm_tn(h2, dup, "bwd_up_w")
    dx1, dscale_m, dshift_m, dg_mlp, ga, dgate_a = _norm_bwd(
        dh2, x1, dx2, g_mlp, scale_m, "norm_mlp_bwd", gate=gate_a, branch=a_out)
    dmix = _mm(ga, W_out.T, F32, "bwd_out_act")
    dW_out = _mm_tn(mix, ga, "bwd_out_w")
    do_f, do_s, dg_fox, dg_sb = _headnorm_bwd(dmix, o_f, o_s, g_out_fox, g_out_sb)
    dq_f, dk_f, dv_f, dfr, dfc = _fox_bwd(qkv, fcol, frow, o_f, lse, do_f, n_pairs)
    dq_s, dk_s, dv_s = _sb_bwd(qkv, o_s, do_s, n_pairs)
    dz_rows, db_fgate = _fgate_bwd(dfr.reshape(n_heads * nc, LANES),
                                   jnp.transpose(dfc, (0, 2, 1)).reshape(n_heads * nc, LANES), z_rows, nc)
    dzf = dz_rows.reshape(n_heads, S).T
    dproj = jnp.concatenate(
        [dq_f, dk_f.astype(BF16), dv_f.astype(BF16), dq_s, dk_s.astype(BF16), dv_s.astype(BF16),
         _pad_cols(dzf, LANES).astype(BF16)], axis=1)
    dh1 = _mm(dproj, W_inp.T, F32, "bwd_in_act")
    dW_inp = _mm_tn(h1, dproj, "bwd_in_w")
    grad_x, dscale_a, dshift_a, dg_attn = _norm_bwd(dh1, xs, dx1, g_attn, scale_a, "norm_attn_bwd")

    dconv_b = jnp.concatenate([p_g[CONV_W:CONV_W + 1, :ff], p_v[CONV_W:CONV_W + 1, :ff]], axis=1)
    parts = [dshift_a, dscale_a, dgate_a, dshift_m, dscale_m, dgate_m,
             dg_attn, db_fgate.reshape(1, n_heads), dg_fox, dg_sb, dg_mlp, dconv_b, dg_final,
             loss_p[:, :1]]
    sizes = [p.shape[1] for p in parts]
    vec = jnp.concatenate(parts, axis=1)
    n_vec = -(-vec.shape[1] // LANES) * LANES
    vec = _pad_cols(vec, n_vec)
    (vec_g,) = _exchange([vec], scatter=False, name="gather_small")
    offs = [0]
    for s in sizes:
        offs.append(offs[-1] + s)

    def small(k0, k1=None):
        k1 = k0 if k1 is None else k1
        return vec_g[:, :, offs[k0]:offs[k1 + 1]]

    dmod_all = small(0, 5).reshape(N_DEV, 6 * D)
    dmod_cols = lax.dynamic_slice(dmod_all, (0, me * n_ada), (N_DEV, n_ada))
    dW_ada = _ada_bwd(c_all.T, dmod_cols)

    dW_in = jnp.concatenate([dW_inp[:, :n_qkv], dW_inp[:, n_qkv:n_qkv + n_heads]], axis=1)
    dW_upf = jnp.concatenate([dW_up[:, :ff], dW_up[:, ffp:ffp + ff]], axis=1)
    dcw = jnp.concatenate([p_g[:CONV_W, :ff], p_v[:CONV_W, :ff]], axis=1)
    s_in, s_out, s_up, s_down, s_cw = _exchange(
        [_to_slots(dW_in, 1), _to_slots(dW_out, 0), _to_slots(dW_upf, 1), _to_slots(dW_down[:ff], 0),
         _to_slots(dcw, 1)], scatter=True, name="scatter_grads")

    res = {}
    res["w_ada"] = _adamw(w_ada[0], dW_ada[None], m_w_ada[0], v_w_ada[0], "adamw_w_ada")
    res["w_in"] = _adamw(w_in[0], s_in, m_w_in[0], v_w_in[0], "adamw_w_in")
    res["w_out"] = _adamw(w_out[0], s_out, m_w_out[0], v_w_out[0], "adamw_w_out")
    res["w_up"] = _adamw(w_up[0], s_up, m_w_up[0], v_w_up[0], "adamw_w_up")
    res["w_down"] = _adamw(w_down[0], s_down, m_w_down[0], v_w_down[0], "adamw_w_down")
    res["conv_w"] = _adamw(conv_w[0], s_cw, m_conv_w[0], v_conv_w[0], "adamw_conv_w")
    res["b_ada"] = _adamw(b_ada, small(0, 5), m_b_ada, v_b_ada, "adamw_b_ada")
    res["g_attn"] = _adamw(g_attn, small(6), m_g_attn, v_g_attn, "adamw_g_attn")
    res["b_fgate"] = _adamw(b_fgate, small(7), m_b_fgate, v_b_fgate, "adamw_b_fgate")
    res["g_out_fox"] = _adamw(g_out_fox, small(8), m_g_out_fox, v_g_out_fox, "adamw_g_out_fox")
    res["g_out_sb"] = _adamw(g_out_sb, small(9), m_g_out_sb, v_g_out_sb, "adamw_g_out_sb")
    res["g_mlp"] = _adamw(g_mlp, small(10), m_g_mlp, v_g_mlp, "adamw_g_mlp")
    res["conv_b"] = _adamw(conv_b, small(11), m_conv_b, v_conv_b, "adamw_conv_b")
    res["g_final"] = _adamw(g_final.reshape(1, D), small(12), m_g_final.reshape(1, D),
                            v_g_final.reshape(1, D), "adamw_g_final")
    loss = _slot_sum(_pad_cols(small(13).reshape(N_DEV, 1), LANES).reshape(N_DEV, 1, LANES), "loss_sum")[0, 0]

    names = ["w_ada", "b_ada", "g_attn", "w_in", "b_fgate", "g_out_fox", "g_out_sb", "w_out", "g_mlp",
             "w_up", "conv_w", "conv_b", "w_down", "g_final"]

    def shaped(n, a):
        if n == "g_final":
            return a.reshape(D)
        if n in ("b_ada", "g_attn", "b_fgate", "g_out_fox", "g_out_sb", "g_mlp", "conv_b"):
            return a
        return a[None]

    outs = [loss, grad_x[None]]
    for k in range(4):
        outs += [shaped(n, res[n][k]) for n in names]
    return tuple(outs)
```

```python
import jax
import jax.numpy as jnp
from jax import lax
from jax.experimental import pallas as pl
from jax.experimental.pallas import tpu as pltpu

F32 = jnp.float32
BF16 = jnp.bfloat16
HIGHEST = lax.Precision.HIGHEST

N_DEV = 8
LANES = 128
HEAD_DIM = 64
EPS = 1e-6
CONV_W = 3
HALO = 16
ATT_BQ = 512
ATT_BK = 512
SCAN_BK = 128
VMEM_LIMIT = 56 * 1024 * 1024

ADAM_LR = 0.001
ADAM_B1 = 0.9
ADAM_B2 = 0.999
ADAM_EPS = 1e-08
ADAM_WD = 0.01
ADAM_STEP = 10


def _params(**kw):
    return pltpu.CompilerParams(vmem_limit_bytes=VMEM_LIMIT, **kw)


def _tile(n, cap):
    if n <= cap:
        return n
    best = None
    for t in range(LANES, cap + 1, LANES):
        if n % t == 0:
            best = t
    assert best is not None, (n, cap)
    return best


def _dot(a, b, **kw):
    return jnp.dot(a, b, preferred_element_type=F32, **kw)


def _dot_tn(a, b):
    return lax.dot_general(a, b, (((0,), (0,)), ((), ())), preferred_element_type=F32)


def _exchange(arrays, scatter, name):
    n = len(arrays)
    out_shape = []
    for a in arrays:
        shp = a.shape[1:] if scatter else a.shape
        out_shape.append(jax.ShapeDtypeStruct((N_DEV,) + tuple(shp), a.dtype))

    def body(*refs):
        ins, outs = refs[:n], refs[n:2 * n]
        send_sems, recv_sems, loc_sems = refs[2 * n:]
        x, y, c = lax.axis_index("x"), lax.axis_index("y"), lax.axis_index("c")
        me = 4 * x + 2 * y + c
        copies = []
        for a in range(n):
            src = ins[a].at[me] if scatter else ins[a]
            cp = pltpu.make_async_copy(src, outs[a].at[me], loc_sems.at[a])
            cp.start()
            copies.append(cp)
        for k in range(1, N_DEV):
            px = 1 - x if k & 4 else x
            py = 1 - y if k & 2 else y
            pc = 1 - c if k & 1 else c
            peer = 4 * px + 2 * py + pc
            for a in range(n):
                src = ins[a].at[peer] if scatter else ins[a]
                cp = pltpu.make_async_remote_copy(
                    src_ref=src, dst_ref=outs[a].at[me],
                    send_sem=send_sems.at[a, k - 1], recv_sem=recv_sems.at[a, k - 1],
                    device_id=(px, py, pc), device_id_type=pl.DeviceIdType.MESH)
                cp.start()
                copies.append(cp)
        for cp in copies:
            cp.wait()

    any_spec = pl.BlockSpec(memory_space=pl.ANY)
    return pl.pallas_call(
        body, name=name, out_shape=tuple(out_shape),
        in_specs=[any_spec] * n, out_specs=tuple([any_spec] * n),
        scratch_shapes=[pltpu.SemaphoreType.DMA((n, N_DEV - 1)), pltpu.SemaphoreType.DMA((n, N_DEV - 1)),
                        pltpu.SemaphoreType.DMA((n,))],
        compiler_params=pltpu.CompilerParams(has_side_effects=True),
    )(*arrays)


def _mm(a, b, out_dtype, name, tm=512, tn=512):
    M, K = a.shape
    _, N = b.shape
    tm, tn = _tile(M, tm), _tile(N, tn)

    def body(a_ref, b_ref, o_ref):
        o_ref[...] = _dot(a_ref[...], b_ref[...]).astype(out_dtype)

    return pl.pallas_call(
        body, name=name, out_shape=jax.ShapeDtypeStruct((M, N), out_dtype),
        grid=(M // tm, N // tn),
        in_specs=[pl.BlockSpec((tm, K), lambda i, j: (i, 0)), pl.BlockSpec((K, tn), lambda i, j: (0, j))],
        out_specs=pl.BlockSpec((tm, tn), lambda i, j: (i, j)),
        compiler_params=_params(dimension_semantics=("parallel", "parallel")),
    )(a, b)


def _mm_tn(a, b, name, tm=1024, tn=640, tk=512):
    S, M = a.shape
    _, N = b.shape
    tm, tn, tk = _tile(M, tm), _tile(N, tn), _tile(S, tk)
    nk = S // tk

    def body(a_ref, b_ref, o_ref):
        @pl.when(pl.program_id(2) == 0)
        def _():
            o_ref[...] = jnp.zeros_like(o_ref)

        o_ref[...] += _dot_tn(a_ref[...], b_ref[...])

    return pl.pallas_call(
        body, name=name, out_shape=jax.ShapeDtypeStruct((M, N), F32),
        grid=(M // tm, N // tn, nk),
        in_specs=[pl.BlockSpec((tk, tm), lambda i, j, k: (k, i)), pl.BlockSpec((tk, tn), lambda i, j, k: (k, j))],
        out_specs=pl.BlockSpec((tm, tn), lambda i, j, k: (i, j)),
        compiler_params=_params(dimension_semantics=("parallel", "parallel", "arbitrary")),
    )(a, b)


def _silu(z):
    return z * (1.0 / (1.0 + jnp.exp(-z)))


def _ada_fwd(c_all, w_shard, b_shard):
    n = w_shard.shape[1]

    def body(c_ref, w_ref, b_ref, o_ref):
        o_ref[...] = _dot(_silu(c_ref[...]), w_ref[...], precision=HIGHEST) + b_ref[...]

    return pl.pallas_call(body, name="ada_fwd", out_shape=jax.ShapeDtypeStruct((N_DEV, n), F32),
                          compiler_params=_params())(c_all, w_shard, b_shard)


def _ada_bwd(c_all_t, dmod_cols):
    D = c_all_t.shape[0]
    n = dmod_cols.shape[1]

    def body(ct_ref, dm_ref, o_ref):
        sc = _silu(ct_ref[...])
        dm = dm_ref[...]
        acc = sc[:, 0:1] * dm[0:1, :]
        for b in range(1, N_DEV):
            acc = acc + sc[:, b:b + 1] * dm[b:b + 1, :]
        o_ref[...] = acc

    return pl.pallas_call(body, name="ada_bwd", out_shape=jax.ShapeDtypeStruct((D, n), F32),
                          compiler_params=_params())(c_all_t, dmod_cols)


def _row_specs(tm, widths):
    return [pl.BlockSpec((tm, w), lambda i: (i, 0)) for w in widths]


def _vec_spec(w):
    return pl.BlockSpec((1, w), lambda i: (0, 0))


def _prenorm(x, g, scale, shift, name):
    S, D = x.shape
    tm = _tile(S, 512)

    def body(x_ref, g_ref, sc_ref, sh_ref, h_ref):
        xv = x_ref[...]
        r = lax.rsqrt(jnp.mean(xv * xv, axis=-1, keepdims=True) + EPS)
        h_ref[...] = ((xv * r) * g_ref[...] * (1.0 + sc_ref[...]) + sh_ref[...]).astype(BF16)

    return pl.pallas_call(
        body, name=name, out_shape=jax.ShapeDtypeStruct((S, D), BF16), grid=(S // tm,),
        in_specs=_row_specs(tm, [D]) + [_vec_spec(D)] * 3, out_specs=_row_specs(tm, [D])[0],
        compiler_params=_params(dimension_semantics=("parallel",)),
    )(x, g, scale, shift)


def _group_ones():
    r = lax.broadcasted_iota(jnp.int32, (LANES, LANES), 0) // HEAD_DIM
    c = lax.broadcasted_iota(jnp.int32, (LANES, LANES), 1) // HEAD_DIM
    return (r == c).astype(F32)


def _headnorm_fwd(o_f, o_s, g_f, g_s):
    S, dh = o_f.shape
    tm = _tile(S, 512)

    def body(of_ref, os_ref, gf_ref, gs_ref, mix_ref):
        ones = _group_ones()
        for part, (o_ref, g_ref) in enumerate(((of_ref, gf_ref), (os_ref, gs_ref))):
            for t in range(dh // LANES):
                cols = slice(t * LANES, (t + 1) * LANES)
                o = o_ref[:, cols]
                ms = _dot(o * o, ones, precision=HIGHEST) * (1.0 / HEAD_DIM)
                mix_ref[:, part * dh + t * LANES: part * dh + (t + 1) * LANES] = (
                    o * lax.rsqrt(ms + EPS) * g_ref[:, cols]).astype(BF16)

    return pl.pallas_call(
        body, name="headnorm_fwd", out_shape=jax.ShapeDtypeStruct((S, 2 * dh), BF16), grid=(S // tm,),
        in_specs=_row_specs(tm, [dh, dh]) + [_vec_spec(dh)] * 2, out_specs=_row_specs(tm, [2 * dh])[0],
        compiler_params=_params(dimension_semantics=("parallel",)),
    )(o_f, o_s, g_f, g_s)


def _resid_prenorm(x, a_out, gate, g, scale, shift):
    S, D = x.shape
    tm = _tile(S, 512)

    def body(x_ref, a_ref, gt_ref, g_ref, sc_ref, sh_ref, x1_ref, h_ref):
        x1 = x_ref[...] + gt_ref[...] * a_ref[...]
        x1_ref[...] = x1
        r = lax.rsqrt(jnp.mean(x1 * x1, axis=-1, keepdims=True) + EPS)
        h_ref[...] = ((x1 * r) * g_ref[...] * (1.0 + sc_ref[...]) + sh_ref[...]).astype(BF16)

    return pl.pallas_call(
        body, name="resid_prenorm", grid=(S // tm,),
        out_shape=(jax.ShapeDtypeStruct((S, D), F32), jax.ShapeDtypeStruct((S, D), BF16)),
        in_specs=_row_specs(tm, [D, D]) + [_vec_spec(D)] * 4, out_specs=tuple(_row_specs(tm, [D, D])),
        compiler_params=_params(dimension_semantics=("parallel",)),
    )(x, a_out, gate, g, scale, shift)


def _shift_down(main, halo, k):
    ext = jnp.concatenate([halo, main], axis=0)
    return pltpu.roll(ext, k, 0)[halo.shape[0]:]


def _shift_up(main, halo, k):
    ext = jnp.concatenate([main, halo], axis=0)
    n = ext.shape[0]
    return pltpu.roll(ext, n - k, 0)[:main.shape[0]]


def _conv(up, up_halo, w_ref, b_ref):
    return (w_ref[2:3, :] * up + w_ref[1:2, :] * _shift_down(up, up_halo, 1)
            + w_ref[0:1, :] * _shift_down(up, up_halo, 2) + b_ref[...])


def _prev_halo_map(tm, col_off):
    step = tm // HALO
    return lambda j, i: (jnp.maximum(i * step - 1, 0), j + col_off)


def _conv_act(up, cw, cb):
    S, F2 = up.shape
    F = F2 // 2
    tm, ct = _tile(S, 256), _tile(F, 256)
    nct = F // ct

    def body(ug_ref, uv_ref, hg_ref, hv_ref, wg_ref, wv_ref, bg_ref, bv_ref, act_ref):
        first = pl.program_id(1) == 0
        hg = jnp.where(first, 0.0, hg_ref[...])
        hv = jnp.where(first, 0.0, hv_ref[...])
        u_g = _conv(ug_ref[...], hg, wg_ref, bg_ref)
        u_v = _conv(uv_ref[...], hv, wv_ref, bv_ref)
        act_ref[...] = (_silu(u_g) * u_v).astype(BF16)

    main = lambda off: pl.BlockSpec((tm, ct), lambda j, i: (i, j + off))
    halo = lambda off: pl.BlockSpec((HALO, ct), _prev_halo_map(tm, off))
    wspec = lambda off: pl.BlockSpec((CONV_W, ct), lambda j, i: (0, j + off))
    bspec = lambda off: pl.BlockSpec((1, ct), lambda j, i: (0, j + off))
    return pl.pallas_call(
        body, name="conv_act", out_shape=jax.ShapeDtypeStruct((S, F), BF16), grid=(nct, S // tm),
        in_specs=[main(0), main(nct), halo(0), halo(nct), wspec(0), wspec(nct), bspec(0), bspec(nct)],
        out_specs=pl.BlockSpec((tm, ct), lambda j, i: (i, j)),
        compiler_params=_params(dimension_semantics=("parallel", "parallel")),
    )(up, up, up, up, cw, cw, cb, cb)


def _conv_act_bwd(dact, up, cw, cb):
    S, F2 = up.shape
    F = F2 // 2
    tm, ct = _tile(S, 256), _tile(F, 256)
    nct = F // ct

    def body(da_ref, ug_ref, uv_ref, hg_ref, hv_ref, wg_ref, wv_ref, bg_ref, bv_ref,
             dug_ref, duv_ref, pg_ref, pv_ref):
        first = pl.program_id(1) == 0

        @pl.when(first)
        def _():
            pg_ref[...] = jnp.zeros_like(pg_ref)
            pv_ref[...] = jnp.zeros_like(pv_ref)

        da = da_ref[...]
        taps = []
        for u_ref, h_ref in ((ug_ref, hg_ref), (uv_ref, hv_ref)):
            h = jnp.where(first, 0.0, h_ref[...])
            uu = u_ref[...]
            taps.append((_shift_down(uu, h, 2), _shift_down(uu, h, 1), uu))
        u_g = wg_ref[0:1, :] * taps[0][0] + wg_ref[1:2, :] * taps[0][1] + wg_ref[2:3, :] * taps[0][2] + bg_ref[...]
        u_v = wv_ref[0:1, :] * taps[1][0] + wv_ref[1:2, :] * taps[1][1] + wv_ref[2:3, :] * taps[1][2] + bv_ref[...]
        sg = 1.0 / (1.0 + jnp.exp(-u_g))
        du_g = da * u_v * (sg * (1.0 + u_g * (1.0 - sg)))
        du_v = da * (u_g * sg)
        dug_ref[...] = du_g.astype(BF16)
        duv_ref[...] = du_v.astype(BF16)
        for du, tp, p_ref in ((du_g, taps[0], pg_ref), (du_v, taps[1], pv_ref)):
            for k in range(CONV_W):
                p_ref[k:k + 1, :] += jnp.sum(du * tp[k], axis=0, keepdims=True)
            p_ref[CONV_W:CONV_W + 1, :] += jnp.sum(du, axis=0, keepdims=True)

    main = lambda off: pl.BlockSpec((tm, ct), lambda j, i: (i, j + off))
    halo = lambda off: pl.BlockSpec((HALO, ct), _prev_halo_map(tm, off))
    wspec = lambda off: pl.BlockSpec((CONV_W, ct), lambda j, i: (0, j + off))
    bspec = lambda off: pl.BlockSpec((1, ct), lambda j, i: (0, j + off))
    pspec = lambda off: pl.BlockSpec((8, ct), lambda j, i: (0, j + off))
    du_g, du_v, p_g, p_v = pl.pallas_call(
        body, name="conv_act_bwd", grid=(nct, S // tm),
        out_shape=(jax.ShapeDtypeStruct((S, F), BF16), jax.ShapeDtypeStruct((S, F), BF16),
                   jax.ShapeDtypeStruct((8, F), F32), jax.ShapeDtypeStruct((8, F), F32)),
        in_specs=[main(0), main(0), main(nct), halo(0), halo(nct), wspec(0), wspec(nct), bspec(0), bspec(nct)],
        out_specs=(main(0), main(0), pspec(0), pspec(0)),
        compiler_params=_params(dimension_semantics=("parallel", "arbitrary")),
    )(dact, up, up, up, up, cw, cw, cb, cb)
    return du_g, du_v, p_g, p_v


def _conv_bwd_input(du, cw):
    S, C = du.shape
    tm, ct = _tile(S, 256), _tile(C, 256)
    step = tm // HALO
    last_halo = S // HALO - 1

    def body(du_ref, h_ref, w_ref, o_ref):
        last = pl.program_id(1) == pl.num_programs(1) - 1
        d = du_ref[...].astype(F32)
        h = jnp.where(last, 0.0, h_ref[...].astype(F32))
        o_ref[...] = (w_ref[2:3, :] * d + w_ref[1:2, :] * _shift_up(d, h, 1)
                      + w_ref[0:1, :] * _shift_up(d, h, 2)).astype(BF16)

    return pl.pallas_call(
        body, name="conv_bwd_input", out_shape=jax.ShapeDtypeStruct((S, C), BF16), grid=(C // ct, S // tm),
        in_specs=[pl.BlockSpec((tm, ct), lambda j, i: (i, j)),
                  pl.BlockSpec((HALO, ct), lambda j, i: (jnp.minimum((i + 1) * step, last_halo), j)),
                  pl.BlockSpec((CONV_W, ct), lambda j, i: (0, j))],
        out_specs=pl.BlockSpec((tm, ct), lambda j, i: (i, j)),
        compiler_params=_params(dimension_semantics=("parallel", "parallel")),
    )(du, du, cw)


def _scan_mats(R, nc, reverse):
    i = lax.broadcasted_iota(jnp.int32, (LANES, LANES), 0)
    j = lax.broadcasted_iota(jnp.int32, (LANES, LANES), 1)
    inner = ((i >= j) if reverse else (i <= j)).astype(F32)
    r = lax.broadcasted_iota(jnp.int32, (R, R), 0)
    c = lax.broadcasted_iota(jnp.int32, (R, R), 1)
    same = (r // nc) == (c // nc)
    outer = (same & ((c > r) if reverse else (c < r))).astype(F32)
    return inner, outer


def _chunk_scan(v, inner, outer, reverse):
    w = _dot(v, inner, precision=HIGHEST)
    col = 0 if reverse else LANES - 1
    carry = _dot(outer, w, precision=HIGHEST)[:, col:col + 1]
    return w + carry


def _fgate_fwd(z_rows, nc):
    R = z_rows.shape[0]

    def body(z_ref, f_ref):
        z = z_ref[...]
        logf = jnp.minimum(z, 0.0) - jnp.log(1.0 + jnp.exp(-jnp.abs(z)))
        inner, outer = _scan_mats(R, nc, False)
        f_ref[...] = _chunk_scan(logf, inner, outer, False)

    return pl.pallas_call(body, name="fgate_fwd", out_shape=jax.ShapeDtypeStruct((R, LANES), F32),
                          compiler_params=_params())(z_rows)


def _fgate_bwd(dfk_neg_rows, dfq_rows, z_rows, nc):
    R = z_rows.shape[0]
    nh = R // nc

    def body(dfk_ref, dfq_ref, z_ref, dz_ref, db_ref):
        inner, outer = _scan_mats(R, nc, True)
        dlogf = _chunk_scan(dfq_ref[...] - dfk_ref[...], inner, outer, True)
        dz = dlogf * (1.0 / (1.0 + jnp.exp(z_ref[...])))
        dz_ref[...] = dz
        hr = lax.broadcasted_iota(jnp.int32, (nh, R), 0)
        hc = lax.broadcasted_iota(jnp.int32, (nh, R), 1) // nc
        per_head = _dot((hr == hc).astype(F32), dz, precision=HIGHEST)
        db_ref[...] = jnp.sum(per_head, axis=1, keepdims=True)

    return pl.pallas_call(
        body, name="fgate_bwd",
        out_shape=(jax.ShapeDtypeStruct((R, LANES), F32), jax.ShapeDtypeStruct((nh, 1), F32)),
        compiler_params=_params())(dfk_neg_rows, dfq_rows, z_rows)


_NEG = -1e30
_SCALE = HEAD_DIM ** -0.5
N_SCAN = ATT_BK // SCAN_BK
F_PARTS = 3
Q_F_LANE = HEAD_DIM
Q_ONE_LANE = HEAD_DIM + F_PARTS


def _kv_slice(j):
    return pl.ds(pl.multiple_of(j * ATT_BK, ATT_BK), ATT_BK)


def _mask_t(strict):
    s = lax.broadcasted_iota(jnp.int32, (ATT_BK, ATT_BQ), 0)
    t = lax.broadcasted_iota(jnp.int32, (ATT_BK, ATT_BQ), 1)
    return (s < t) if strict else (s <= t)


def _walk(i, step, carry, descending):
    if descending:
        carry = step(i, carry, True)
        return lax.fori_loop(0, i, lambda n, cr: step(i - 1 - n, cr, False), carry)
    carry = lax.fori_loop(0, i, lambda j, cr: step(j, cr, False), carry)
    return step(i, carry, True)


def _t_block(rows):
    return pl.BlockSpec((None, rows, ATT_BQ), lambda h, i: (h, 0, i))


def _t_full(rows, S):
    return pl.BlockSpec((None, rows, S), lambda h, i: (h, 0, 0))


def _n_block():
    return pl.BlockSpec((None, ATT_BQ, LANES), lambda h, i: (h, i, 0))


def _n_full(S):
    return pl.BlockSpec((None, S, LANES), lambda h, i: (h, 0, 0))


def _heads(t):
    S = t.shape[0]
    return jnp.transpose(t.reshape(S, -1, HEAD_DIM), (1, 0, 2))


def _unheads(t):
    return jnp.transpose(t, (1, 0, 2)).reshape(t.shape[1], -1)


def _pad_lanes(t):
    return jnp.pad(t, ((0, 0), (0, 0), (0, LANES - t.shape[2])))


def _tr(t):
    return jnp.transpose(t, (0, 2, 1))


def _split3(f):
    def bf16_round(t):
        return lax.reduce_precision(t, exponent_bits=8, mantissa_bits=7)

    hi = bf16_round(f)
    mid = bf16_round(f - hi)
    lo = bf16_round(f - hi - mid)
    return jnp.stack([hi, mid, lo], axis=-1).astype(BF16)


def _fox_operands(q, k, v, f_rows):
    qh, kh, vh = _heads(q) * _SCALE, _heads(k), _heads(v)
    f3 = _split3(f_rows)
    ones = jnp.ones_like(f3)
    q_n = _pad_lanes(jnp.concatenate([qh, f3, ones], axis=-1))
    k_n = _pad_lanes(jnp.concatenate([kh, ones, -f3], axis=-1))
    return dict(q_n=q_n, q_t=_tr(q_n), k_n=k_n, k_t=_tr(k_n), v_n=_pad_lanes(vh), v_t=_tr(vh))


def _sb_operands(q, k, v):
    qh, kh, vh = _heads(q) * _SCALE, _heads(k), _heads(v)
    q_n, k_n = _pad_lanes(qh), _pad_lanes(kh)
    return dict(q_n=q_n, q_t=_tr(q_n), k_n=k_n, k_t=_tr(k_n), v_n=_pad_lanes(vh), v_t=_tr(vh))


def _do_operands(do):
    do_n = _pad_lanes(_heads(do).astype(BF16))
    return do_n, _tr(do_n)


def _fox_fwd(q_t, k_n, v_t):
    H, _, S = q_t.shape

    def body(qt_ref, k_ref, vt_ref, ot_ref, lse_ref):
        i = pl.program_id(1)
        qt = qt_ref[...]

        def step(j, carry, masked):
            m, l, acc = carry
            ks = _kv_slice(j)
            s = _dot(k_ref[ks, :], qt)
            if masked:
                s = jnp.where(_mask_t(False), s, _NEG)
            mn = jnp.maximum(m, jnp.max(s, axis=0, keepdims=True))
            alpha = jnp.exp(m - mn)
            p = jnp.exp(s - mn)
            l = alpha * l + jnp.sum(p, axis=0, keepdims=True)
            acc = acc * alpha + _dot(vt_ref[:, ks], p.astype(BF16))
            return mn, l, acc

        row = jnp.zeros((1, ATT_BQ), F32)
        m, l, acc = _walk(i, step, (row + _NEG, row, jnp.zeros((HEAD_DIM, ATT_BQ), F32)), False)
        ot_ref[...] = acc / l
        lse_ref[...] = m + jnp.log(l)

    return pl.pallas_call(
        body, name="fox_fwd", grid=(H, S // ATT_BQ),
        out_shape=(jax.ShapeDtypeStruct((H, HEAD_DIM, S), F32), jax.ShapeDtypeStruct((H, 1, S), F32)),
        in_specs=[_t_block(LANES), _n_full(S), _t_full(HEAD_DIM, S)],
        out_specs=(_t_block(HEAD_DIM), _t_block(1)),
        compiler_params=_params(dimension_semantics=("parallel", "parallel")),
    )(q_t, k_n, v_t)


def _fox_bwd(q_t, q_n, k_n, k_t, v_n, do_t, do_n, o_t, lse):
    H, _, S = q_t.shape

    def body(qt_ref, qn_ref, k_ref, kt_ref, v_ref, dot_ref, don_ref, ot_ref, lse_ref, dqt_ref, dk_ref, dv_ref):
        i = pl.program_id(1)

        @pl.when(i == 0)
        def _():
            dk_ref[...] = jnp.zeros_like(dk_ref)
            dv_ref[...] = jnp.zeros_like(dv_ref)

        qt, qn, dot, don = qt_ref[...], qn_ref[...], dot_ref[...], don_ref[...]
        lse = lse_ref[...]
        delta = jnp.sum(dot[:HEAD_DIM].astype(F32) * ot_ref[...], axis=0, keepdims=True)

        def step(j, dq, masked):
            ks = _kv_slice(j)
            s = _dot(k_ref[ks, :], qt)
            if masked:
                s = jnp.where(_mask_t(False), s, _NEG)
            p = jnp.exp(s - lse)
            ds = (p * (_dot(v_ref[ks, :], dot) - delta)).astype(BF16)
            dk_ref[ks, :] += _dot(ds, qn)
            dv_ref[ks, :] += _dot(p.astype(BF16), don)
            return dq + _dot(kt_ref[:, ks], ds)

        dqt_ref[...] = _walk(i, step, jnp.zeros((LANES, ATT_BQ), F32), False)

    return pl.pallas_call(
        body, name="fox_bwd", grid=(H, S // ATT_BQ),
        out_shape=(jax.ShapeDtypeStruct((H, LANES, S), F32), jax.ShapeDtypeStruct((H, S, LANES), F32),
                   jax.ShapeDtypeStruct((H, S, LANES), F32)),
        in_specs=[_t_block(LANES), _n_block(), _n_full(S), _t_full(LANES, S), _n_full(S),
                  _t_block(LANES), _n_block(), _t_block(HEAD_DIM), _t_block(1)],
        out_specs=(_t_block(LANES), _n_full(S), _n_full(S)),
        compiler_params=_params(dimension_semantics=("parallel", "arbitrary")),
    )(q_t, q_n, k_n, k_t, v_n, do_t, do_n, o_t, lse)


def _scan_lhs():
    r = lax.broadcasted_iota(jnp.int32, (SCAN_BK, 2 * SCAN_BK), 0)
    c = lax.broadcasted_iota(jnp.int32, (SCAN_BK, 2 * SCAN_BK), 1) % SCAN_BK
    return (c >= r).astype(BF16)


def _suffix_sum(t, lhs):
    hi = t.astype(BF16)
    lo = (t - hi.astype(F32)).astype(BF16)
    return _dot(lhs, jnp.concatenate([hi, lo], axis=0))


def _sb_scores(k, qt, mask):
    z = _dot(k, qt)
    e = jnp.exp(-jnp.abs(z))
    lb = -(jnp.maximum(z, 0.0) + jnp.log(1.0 + e))
    if mask is not None:
        lb = jnp.where(mask, lb, 0.0)
    return z, e, lb


def _scan_blocks():
    return [slice(u * SCAN_BK, (u + 1) * SCAN_BK) for u in reversed(range(N_SCAN))]


def _sb_fwd(q_t, k_n, v_t):
    H, _, S = q_t.shape

    def body(qt_ref, k_ref, vt_ref, ot_ref):
        i = pl.program_id(1)
        qt = qt_ref[...]
        lhs = _scan_lhs()

        def step(j, carry, masked):
            c, acc = carry
            ks = _kv_slice(j)
            mask = _mask_t(True) if masked else None
            z, _, lb = _sb_scores(k_ref[ks, :], qt, mask)
            parts = []
            for sl in _scan_blocks():
                rin = _suffix_sum(lb[sl], lhs)
                a = jnp.exp(z[sl] + rin + c)
                if masked:
                    a = jnp.where(mask[sl], a, 0.0)
                parts.append(a.astype(BF16))
                c = c + rin[0:1, :]
            a_all = jnp.concatenate(parts[::-1], axis=0)
            return c, acc + _dot(vt_ref[:, ks], a_all)

        carry = (jnp.zeros((1, ATT_BQ), F32), jnp.zeros((HEAD_DIM, ATT_BQ), F32))
        ot_ref[...] = _walk(i, step, carry, True)[1]

    return pl.pallas_call(
        body, name="sb_fwd", grid=(H, S // ATT_BQ),
        out_shape=jax.ShapeDtypeStruct((H, HEAD_DIM, S), F32),
        in_specs=[_t_block(LANES), _n_full(S), _t_full(HEAD_DIM, S)],
        out_specs=_t_block(HEAD_DIM),
        compiler_params=_params(dimension_semantics=("parallel", "parallel")),
    )(q_t, k_n, v_t)


def _sb_bwd(q_t, q_n, k_n, k_t, v_n, do_t, do_n, o_t):
    H, _, S = q_t.shape

    def body(qt_ref, qn_ref, k_ref, kt_ref, v_ref, dot_ref, don_ref, ot_ref, dqt_ref, dk_ref, dv_ref):
        i = pl.program_id(1)

        @pl.when(i == 0)
        def _():
            dk_ref[...] = jnp.zeros_like(dk_ref)
            dv_ref[...] = jnp.zeros_like(dv_ref)

        qt, qn, dot, don = qt_ref[...], qn_ref[...], dot_ref[...], don_ref[...]
        lhs = _scan_lhs()
        delta = jnp.sum(dot[:HEAD_DIM].astype(F32) * ot_ref[...], axis=0, keepdims=True)

        def step(j, carry, masked):
            c, g, dq = carry
            ks = _kv_slice(j)
            mask = _mask_t(True) if masked else None
            z, e, lb = _sb_scores(k_ref[ks, :], qt, mask)
            da = _dot(v_ref[ks, :], dot)
            a_parts, dz_parts = [], []
            for sl in _scan_blocks():
                rin = _suffix_sum(lb[sl], lhs)
                a = jnp.exp(z[sl] + rin + c)
                if masked:
                    a = jnp.where(mask[sl], a, 0.0)
                ab = a.astype(BF16)
                gg = ab.astype(F32) * da[sl]
                rgin = _suffix_sum(gg, lhs)
                rinv = 1.0 / (1.0 + e[sl])
                sig = jnp.where(z[sl] >= 0.0, rinv, e[sl] * rinv)
                dz = gg - sig * (delta - g - (rgin - gg))
                if masked:
                    dz = jnp.where(mask[sl], dz, 0.0)
                a_parts.append(ab)
                dz_parts.append(dz.astype(BF16))
                c = c + rin[0:1, :]
                g = g + rgin[0:1, :]
            ab_all = jnp.concatenate(a_parts[::-1], axis=0)
            dzb = jnp.concatenate(dz_parts[::-1], axis=0)
            dk_ref[ks, :] += _dot(dzb, qn)
            dv_ref[ks, :] += _dot(ab_all, don)
            return c, g, dq + _dot(kt_ref[:, ks], dzb)

        row = jnp.zeros((1, ATT_BQ), F32)
        dqt_ref[...] = _walk(i, step, (row, row, jnp.zeros((LANES, ATT_BQ), F32)), True)[2]

    return pl.pallas_call(
        body, name="sb_bwd", grid=(H, S // ATT_BQ),
        out_shape=(jax.ShapeDtypeStruct((H, LANES, S), F32), jax.ShapeDtypeStruct((H, S, LANES), F32),
                   jax.ShapeDtypeStruct((H, S, LANES), F32)),
        in_specs=[_t_block(LANES), _n_block(), _n_full(S), _t_full(LANES, S), _n_full(S),
                  _t_block(LANES), _n_block(), _t_block(HEAD_DIM)],
        out_specs=(_t_block(LANES), _n_full(S), _n_full(S)),
        compiler_params=_params(dimension_semantics=("parallel", "arbitrary")),
    )(q_t, q_n, k_n, k_t, v_n, do_t, do_n, o_t)


def _acc_spec(w):
    return pl.BlockSpec((1, w), lambda i: (0, 0))


def _loss_head(x1, m_out, gate_m, g_final, target):
    S, D = x1.shape
    tm = _tile(S, 256)

    def body(x1_ref, mo_ref, gt_ref, gf_ref, tg_ref, dx2_ref, gm_ref, loss_ref, dgf_ref, dgt_ref):
        @pl.when(pl.program_id(0) == 0)
        def _():
            loss_ref[...] = jnp.zeros_like(loss_ref)
            dgf_ref[...] = jnp.zeros_like(dgf_ref)
            dgt_ref[...] = jnp.zeros_like(dgt_ref)

        mo = mo_ref[...]
        x2 = x1_ref[...] + gt_ref[...] * mo
        r = lax.rsqrt(jnp.mean(x2 * x2, axis=-1, keepdims=True) + EPS)
        xh = x2 * r
        diff = xh * gf_ref[...] - tg_ref[...]
        loss_ref[...] += (0.5 / D) * jnp.sum(diff * diff)
        dy = diff * (1.0 / D)
        dgf_ref[...] += jnp.sum(dy * xh, axis=0, keepdims=True)
        dxh = dy * gf_ref[...]
        dx2 = r * (dxh - xh * jnp.mean(dxh * xh, axis=-1, keepdims=True))
        dx2_ref[...] = dx2
        gm_ref[...] = (dx2 * gt_ref[...]).astype(BF16)
        dgt_ref[...] += jnp.sum(dx2 * mo, axis=0, keepdims=True)

    return pl.pallas_call(
        body, name="loss_head", grid=(S // tm,),
        out_shape=(jax.ShapeDtypeStruct((S, D), F32), jax.ShapeDtypeStruct((S, D), BF16),
                   jax.ShapeDtypeStruct((1, LANES), F32), jax.ShapeDtypeStruct((1, D), F32),
                   jax.ShapeDtypeStruct((1, D), F32)),
        in_specs=_row_specs(tm, [D, D]) + [_vec_spec(D)] * 2 + _row_specs(tm, [D]),
        out_specs=tuple(_row_specs(tm, [D, D]) + [_acc_spec(LANES), _acc_spec(D), _acc_spec(D)]),
        compiler_params=_params(dimension_semantics=("arbitrary",)),
    )(x1, m_out, gate_m, g_final, target)


def _norm_bwd(dh, xin, dres, g, scale, name, gate=None, branch=None):
    S, D = xin.shape
    tm = _tile(S, 256)
    gated = gate is not None

    def body(*refs):
        if gated:
            (dh_ref, x_ref, dr_ref, g_ref, sc_ref, gt_ref, br_ref,
             dx_ref, dsc_ref, dsh_ref, dg_ref, ga_ref, dgt_ref) = refs
            sums = (dsc_ref, dsh_ref, dg_ref, dgt_ref)
        else:
            dh_ref, x_ref, dr_ref, g_ref, sc_ref, dx_ref, dsc_ref, dsh_ref, dg_ref = refs
            sums = (dsc_ref, dsh_ref, dg_ref)

        @pl.when(pl.program_id(0) == 0)
        def _():
            for s_ref in sums:
                s_ref[...] = jnp.zeros_like(s_ref)

        dhv, xv = dh_ref[...], x_ref[...]
        r = lax.rsqrt(jnp.mean(xv * xv, axis=-1, keepdims=True) + EPS)
        xh = xv * r
        dsc_ref[...] += jnp.sum(dhv * (xh * g_ref[...]), axis=0, keepdims=True)
        dsh_ref[...] += jnp.sum(dhv, axis=0, keepdims=True)
        dn = dhv * (1.0 + sc_ref[...])
        dg_ref[...] += jnp.sum(dn * xh, axis=0, keepdims=True)
        dxh = dn * g_ref[...]
        dx = dr_ref[...] + r * (dxh - xh * jnp.mean(dxh * xh, axis=-1, keepdims=True))
        dx_ref[...] = dx
        if gated:
            ga_ref[...] = (dx * gt_ref[...]).astype(BF16)
            dgt_ref[...] += jnp.sum(dx * br_ref[...], axis=0, keepdims=True)

    vec = jax.ShapeDtypeStruct((1, D), F32)
    out_shape = [jax.ShapeDtypeStruct((S, D), F32), vec, vec, vec]
    out_specs = _row_specs(tm, [D]) + [_acc_spec(D)] * 3
    in_specs = _row_specs(tm, [D, D, D]) + [_vec_spec(D)] * 2
    args = [dh, xin, dres, g, scale]
    if gated:
        out_shape += [jax.ShapeDtypeStruct((S, D), BF16), vec]
        out_specs += _row_specs(tm, [D]) + [_acc_spec(D)]
        in_specs += [_vec_spec(D)] + _row_specs(tm, [D])
        args += [gate, branch]
    return pl.pallas_call(
        body, name=name, grid=(S // tm,), out_shape=tuple(out_shape),
        in_specs=in_specs, out_specs=tuple(out_specs),
        compiler_params=_params(dimension_semantics=("arbitrary",)),
    )(*args)


def _headnorm_bwd(dmix, o_f, o_s, g_f, g_s):
    S, dh = o_f.shape
    tm = _tile(S, 256)

    def body(dm_ref, of_ref, os_ref, gf_ref, gs_ref, dof_ref, dos_ref, dgf_ref, dgs_ref):
        @pl.when(pl.program_id(0) == 0)
        def _():
            dgf_ref[...] = jnp.zeros_like(dgf_ref)
            dgs_ref[...] = jnp.zeros_like(dgs_ref)

        ones = _group_ones()
        parts = ((of_ref, gf_ref, dof_ref, dgf_ref), (os_ref, gs_ref, dos_ref, dgs_ref))
        for part, (o_ref, g_ref, do_ref, dg_ref) in enumerate(parts):
            for t in range(dh // LANES):
                cols = slice(t * LANES, (t + 1) * LANES)
                o = o_ref[:, cols]
                dm = dm_ref[:, part * dh + t * LANES: part * dh + (t + 1) * LANES]
                r = lax.rsqrt(_dot(o * o, ones, precision=HIGHEST) * (1.0 / HEAD_DIM) + EPS)
                oh = o * r
                dg_ref[:, cols] += jnp.sum(dm * oh, axis=0, keepdims=True)
                dn = dm * g_ref[:, cols]
                mean = _dot(dn * oh, ones, precision=HIGHEST) * (1.0 / HEAD_DIM)
                do_ref[:, cols] = r * (dn - oh * mean)

    vec = jax.ShapeDtypeStruct((1, dh), F32)
    return pl.pallas_call(
        body, name="headnorm_bwd", grid=(S // tm,),
        out_shape=(jax.ShapeDtypeStruct((S, dh), F32), jax.ShapeDtypeStruct((S, dh), F32), vec, vec),
        in_specs=_row_specs(tm, [2 * dh, dh, dh]) + [_vec_spec(dh)] * 2,
        out_specs=tuple(_row_specs(tm, [dh, dh]) + [_acc_spec(dh)] * 2),
        compiler_params=_params(dimension_semantics=("arbitrary",)),
    )(dmix, o_f, o_s, g_f, g_s)


def _adamw(w, gslots, m, v, name):
    R, C = w.shape
    n = gslots.shape[0]
    tr = 256 if (R % 256 == 0 and R > 256) else R
    bc1 = 1.0 - ADAM_B1 ** ADAM_STEP
    bc2 = 1.0 - ADAM_B2 ** ADAM_STEP

    def body(w_ref, gs_ref, m_ref, v_ref, g_ref, d_ref, nm_ref, nv_ref):
        g = gs_ref[0]
        for s in range(1, n):
            g = g + gs_ref[s]
        nm = ADAM_B1 * m_ref[...] + (1.0 - ADAM_B1) * g
        nv = ADAM_B2 * v_ref[...] + (1.0 - ADAM_B2) * (g * g)
        g_ref[...] = g
        nm_ref[...] = nm
        nv_ref[...] = nv
        d_ref[...] = -ADAM_LR * ((nm / bc1) / (jnp.sqrt(nv / bc2) + ADAM_EPS) + ADAM_WD * w_ref[...])

    blk = pl.BlockSpec((tr, C), lambda i: (i, 0))
    sds = jax.ShapeDtypeStruct((R, C), F32)
    return pl.pallas_call(
        body, name=name, grid=(R // tr,), out_shape=(sds,) * 4,
        in_specs=[blk, pl.BlockSpec((n, tr, C), lambda i: (0, i, 0)), blk, blk], out_specs=(blk,) * 4,
        compiler_params=_params(dimension_semantics=("parallel",)),
    )(w, gslots, m, v)


def _slot_sum(slots, name):
    n, _, C = slots.shape

    def body(s_ref, o_ref):
        acc = s_ref[0]
        for s in range(1, n):
            acc = acc + s_ref[s]
        o_ref[...] = acc

    return pl.pallas_call(body, name=name, out_shape=jax.ShapeDtypeStruct((1, C), F32),
                          compiler_params=_params())(slots)


def _pad_cols(a, n):
    return jnp.pad(a, ((0, 0), (0, n - a.shape[1])))


def _ungather(g, axis):
    if axis == 0:
        return g.reshape(g.shape[0] * g.shape[1], g.shape[2])
    return jnp.transpose(g, (1, 0, 2)).reshape(g.shape[1], g.shape[0] * g.shape[2])


def _to_slots(full, axis):
    R, C = full.shape
    if axis == 0:
        return full.reshape(N_DEV, R // N_DEV, C)
    return jnp.transpose(full.reshape(R, N_DEV, C // N_DEV), (1, 0, 2))


def kernel(x, c, w_ada, b_ada, g_attn, w_in, b_fgate, g_out_fox, g_out_sb, w_out, g_mlp, w_up, conv_w, conv_b, w_down, g_final, loss_target, m_w_ada, m_b_ada, m_g_attn, m_w_in, m_b_fgate, m_g_out_fox, m_g_out_sb, m_w_out, m_g_mlp, m_w_up, m_conv_w, m_conv_b, m_w_down, m_g_final, v_w_ada, v_b_ada, v_g_attn, v_w_in, v_b_fgate, v_g_out_fox, v_g_out_sb, v_w_out, v_g_mlp, v_w_up, v_conv_w, v_conv_b, v_w_down, v_g_final):
    S, D = x.shape[1], x.shape[2]
    dh = D // 2
    n_heads = dh // HEAD_DIM
    n_qkv = 6 * dh
    ff = w_down.shape[1] * N_DEV
    ffp = -(-ff // (2 * LANES)) * (2 * LANES)
    nc = S // LANES
    me = 4 * lax.axis_index("x") + 2 * lax.axis_index("y") + lax.axis_index("c")
    xs, tgt = x[0], loss_target[0]

    c_all, win_g, wout_g, wup_g, wdown_g, convw_g = _exchange(
        [c, w_in[0].astype(BF16), w_out[0].astype(BF16), w_up[0].astype(BF16), w_down[0].astype(BF16), conv_w[0]],
        scatter=False, name="gather_weights")
    c_all = c_all.reshape(N_DEV, D)
    W_in = _ungather(win_g, 1)
    W_qkv, W_f = W_in[:, :n_qkv], _pad_cols(W_in[:, n_qkv:], LANES)
    W_inp = jnp.concatenate([W_qkv, W_f], axis=1)
    W_out = _ungather(wout_g, 0)
    W_up = _ungather(wup_g, 1)
    W_upp = jnp.concatenate([_pad_cols(W_up[:, :ff], ffp), _pad_cols(W_up[:, ff:], ffp)], axis=1)
    W_down = jnp.pad(_ungather(wdown_g, 0), ((0, ffp - ff), (0, 0)))
    cw_full = _ungather(convw_g, 1)
    cw = jnp.concatenate([_pad_cols(cw_full[:, :ff], ffp), _pad_cols(cw_full[:, ff:], ffp)], axis=1)
    cb = jnp.concatenate([_pad_cols(conv_b[:, :ff], ffp), _pad_cols(conv_b[:, ff:], ffp)], axis=1)

    n_ada = w_ada.shape[2]
    b_shard = lax.dynamic_slice(b_ada, (0, me * n_ada), (1, n_ada))
    mod_cols = _ada_fwd(c_all, w_ada[0], b_shard)
    (mod_g,) = _exchange([mod_cols], scatter=False, name="gather_mod")
    mod = lax.dynamic_index_in_dim(mod_g, me, axis=1, keepdims=False).reshape(6, 1, D)
    shift_a, scale_a, gate_a, shift_m, scale_m, gate_m = [mod[k] for k in range(6)]

    h1 = _prenorm(xs, g_attn, scale_a, shift_a, "prenorm_attn")
    qkv = _mm(h1, W_qkv, BF16, "proj_qkv")
    flog = _mm(h1, W_f, F32, "proj_fgate")
    zf = flog[:, :n_heads] + b_fgate
    z_rows = zf.T.reshape(n_heads * nc, LANES)
    f_rows = _fgate_fwd(z_rows, nc).reshape(n_heads, S)
    fox = _fox_operands(qkv[:, 0:dh], qkv[:, dh:2 * dh], qkv[:, 2 * dh:3 * dh], f_rows)
    sb = _sb_operands(qkv[:, 3 * dh:4 * dh], qkv[:, 4 * dh:5 * dh], qkv[:, 5 * dh:6 * dh])
    of_t, lse = _fox_fwd(fox["q_t"], fox["k_n"], fox["v_t"])
    os_t = _sb_fwd(sb["q_t"], sb["k_n"], sb["v_t"])
    o_f, o_s = _unheads(_tr(of_t)), _unheads(_tr(os_t))
    mix = _headnorm_fwd(o_f, o_s, g_out_fox, g_out_sb)
    a_out = _mm(mix, W_out, F32, "proj_out")
    x1, h2 = _resid_prenorm(xs, a_out, gate_a, g_mlp, scale_m, shift_m)
    up = _mm(h2, W_upp, F32, "proj_up")
    act = _conv_act(up, cw, cb)
    m_out = _mm(act, W_down, F32, "proj_down")

    dx2, gm, loss_p, dg_final, dgate_m = _loss_head(x1, m_out, gate_m, g_final.reshape(1, D), tgt)
    dact = _mm(gm, W_down.T, F32, "bwd_down_act")
    dW_down = _mm_tn(act, gm, "bwd_down_w")
    du_g, du_v, p_g, p_v = _conv_act_bwd(dact, up, cw, cb)
    du = jnp.concatenate([du_g, du_v], axis=1)
    dup = _conv_bwd_input(du, cw)
    dh2 = _mm(dup, W_upp.T, F32, "bwd_up_act")
    dW_up = _mm_tn(h2, dup, "bwd_up_w")
    dx1, dscale_m, dshift_m, dg_mlp, ga, dgate_a = _norm_bwd(
        dh2, x1, dx2, g_mlp, scale_m, "norm_mlp_bwd", gate=gate_a, branch=a_out)
    dmix = _mm(ga, W_out.T, F32, "bwd_out_act")
    dW_out = _mm_tn(mix, ga, "bwd_out_w")
    do_f, do_s, dg_fox, dg_sb = _headnorm_bwd(dmix, o_f, o_s, g_out_fox, g_out_sb)
    dof_n, dof_t = _do_operands(do_f)
    dos_n, dos_t = _do_operands(do_s)
    dqf_t, dkf, dvf = _fox_bwd(fox["q_t"], fox["q_n"], fox["k_n"], fox["k_t"], fox["v_n"], dof_t, dof_n, of_t, lse)
    dqs_t, dks, dvs = _sb_bwd(sb["q_t"], sb["q_n"], sb["k_n"], sb["k_t"], sb["v_n"], dos_t, dos_n, os_t)
    dz_rows, db_fgate = _fgate_bwd(dkf[:, :, Q_ONE_LANE].reshape(n_heads * nc, LANES),
                                   dqf_t[:, Q_F_LANE, :].reshape(n_heads * nc, LANES), z_rows, nc)
    dzf = dz_rows.reshape(n_heads, S).T

    def dq_cols(dq_t):
        return (_unheads(_tr(dq_t[:, :HEAD_DIM, :])) * _SCALE).astype(BF16)

    def dkv_cols(d):
        return _unheads(d[:, :, :HEAD_DIM]).astype(BF16)

    dproj = jnp.concatenate(
        [dq_cols(dqf_t), dkv_cols(dkf), dkv_cols(dvf), dq_cols(dqs_t), dkv_cols(dks), dkv_cols(dvs),
         _pad_cols(dzf, LANES).astype(BF16)], axis=1)
    dh1 = _mm(dproj, W_inp.T, F32, "bwd_in_act")
    dW_inp = _mm_tn(h1, dproj, "bwd_in_w")
    grad_x, dscale_a, dshift_a, dg_attn = _norm_bwd(dh1, xs, dx1, g_attn, scale_a, "norm_attn_bwd")

    dconv_b = jnp.concatenate([p_g[CONV_W:CONV_W + 1, :ff], p_v[CONV_W:CONV_W + 1, :ff]], axis=1)
    parts = [dshift_a, dscale_a, dgate_a, dshift_m, dscale_m, dgate_m,
             dg_attn, db_fgate.reshape(1, n_heads), dg_fox, dg_sb, dg_mlp, dconv_b, dg_final,
             loss_p[:, :1]]
    sizes = [p.shape[1] for p in parts]
    vec = jnp.concatenate(parts, axis=1)
    n_vec = -(-vec.shape[1] // LANES) * LANES
    vec = _pad_cols(vec, n_vec)
    (vec_g,) = _exchange([vec], scatter=False, name="gather_small")
    offs = [0]
    for s in sizes:
        offs.append(offs[-1] + s)

    def small(k0, k1=None):
        k1 = k0 if k1 is None else k1
        return vec_g[:, :, offs[k0]:offs[k1 + 1]]

    dmod_all = small(0, 5).reshape(N_DEV, 6 * D)
    dmod_cols = lax.dynamic_slice(dmod_all, (0, me * n_ada), (N_DEV, n_ada))
    dW_ada = _ada_bwd(c_all.T, dmod_cols)

    dW_in = jnp.concatenate([dW_inp[:, :n_qkv], dW_inp[:, n_qkv:n_qkv + n_heads]], axis=1)
    dW_upf = jnp.concatenate([dW_up[:, :ff], dW_up[:, ffp:ffp + ff]], axis=1)
    dcw = jnp.concatenate([p_g[:CONV_W, :ff], p_v[:CONV_W, :ff]], axis=1)
    s_in, s_out, s_up, s_down, s_cw = _exchange(
        [_to_slots(dW_in, 1), _to_slots(dW_out, 0), _to_slots(dW_upf, 1), _to_slots(dW_down[:ff], 0),
         _to_slots(dcw, 1)], scatter=True, name="scatter_grads")

    res = {}
    res["w_ada"] = _adamw(w_ada[0], dW_ada[None], m_w_ada[0], v_w_ada[0], "adamw_w_ada")
    res["w_in"] = _adamw(w_in[0], s_in, m_w_in[0], v_w_in[0], "adamw_w_in")
    res["w_out"] = _adamw(w_out[0], s_out, m_w_out[0], v_w_out[0], "adamw_w_out")
    res["w_up"] = _adamw(w_up[0], s_up, m_w_up[0], v_w_up[0], "adamw_w_up")
    res["w_down"] = _adamw(w_down[0], s_down, m_w_down[0], v_w_down[0], "adamw_w_down")
    res["conv_w"] = _adamw(conv_w[0], s_cw, m_conv_w[0], v_conv_w[0], "adamw_conv_w")
    res["b_ada"] = _adamw(b_ada, small(0, 5), m_b_ada, v_b_ada, "adamw_b_ada")
    res["g_attn"] = _adamw(g_attn, small(6), m_g_attn, v_g_attn, "adamw_g_attn")
    res["b_fgate"] = _adamw(b_fgate, small(7), m_b_fgate, v_b_fgate, "adamw_b_fgate")
    res["g_out_fox"] = _adamw(g_out_fox, small(8), m_g_out_fox, v_g_out_fox, "adamw_g_out_fox")
    res["g_out_sb"] = _adamw(g_out_sb, small(9), m_g_out_sb, v_g_out_sb, "adamw_g_out_sb")
    res["g_mlp"] = _adamw(g_mlp, small(10), m_g_mlp, v_g_mlp, "adamw_g_mlp")
    res["conv_b"] = _adamw(conv_b, small(11), m_conv_b, v_conv_b, "adamw_conv_b")
    res["g_final"] = _adamw(g_final.reshape(1, D), small(12), m_g_final.reshape(1, D),
                            v_g_final.reshape(1, D), "adamw_g_final")
    loss = _slot_sum(_pad_cols(small(13).reshape(N_DEV, 1), LANES).reshape(N_DEV, 1, LANES), "loss_sum")[0, 0]

    names = ["w_ada", "b_ada", "g_attn", "w_in", "b_fgate", "g_out_fox", "g_out_sb", "w_out", "g_mlp",
             "w_up", "conv_w", "conv_b", "w_down", "g_final"]

    def shaped(n, a):
        if n == "g_final":
            return a.reshape(D)
        if n in ("b_ada", "g_attn", "b_fgate", "g_out_fox", "g_out_sb", "g_mlp", "conv_b"):
            return a
        return a[None]

    outs = [loss, grad_x[None]]
    for k in range(4):
        outs += [shaped(n, res[n][k]) for n in names]
    return tuple(outs)
```

```python
import jax
import jax.numpy as jnp
from jax import lax
from jax.experimental import pallas as pl
from jax.experimental.pallas import tpu as pltpu

F32 = jnp.float32
BF16 = jnp.bfloat16
HIGHEST = lax.Precision.HIGHEST

N_DEV = 8
LANES = 128
HEAD_DIM = 64
EPS = 1e-6
CONV_W = 3
CONV_COLS = 1408
HALO = 16
ATT_BQ = 512
ATT_BK = 512
SCAN_BK = 128
VMEM_LIMIT = 56 * 1024 * 1024

ADAM_LR = 0.001
ADAM_B1 = 0.9
ADAM_B2 = 0.999
ADAM_EPS = 1e-08
ADAM_WD = 0.01
ADAM_STEP = 10


def _params(**kw):
    return pltpu.CompilerParams(vmem_limit_bytes=VMEM_LIMIT, **kw)


def _tile(n, cap):
    if n <= cap:
        return n
    best = None
    for t in range(LANES, cap + 1, LANES):
        if n % t == 0:
            best = t
    assert best is not None, (n, cap)
    return best


def _dot(a, b, **kw):
    return jnp.dot(a, b, preferred_element_type=F32, **kw)


def _dot_tn(a, b):
    return lax.dot_general(a, b, (((0,), (0,)), ((), ())), preferred_element_type=F32)


def _exchange(arrays, scatter, name):
    n = len(arrays)
    out_shape = []
    for a in arrays:
        shp = a.shape[1:] if scatter else a.shape
        out_shape.append(jax.ShapeDtypeStruct((N_DEV,) + tuple(shp), a.dtype))

    def body(*refs):
        ins, outs = refs[:n], refs[n:2 * n]
        send_sems, recv_sems, loc_sems = refs[2 * n:]
        x, y, c = lax.axis_index("x"), lax.axis_index("y"), lax.axis_index("c")
        me = 4 * x + 2 * y + c
        copies = []
        for a in range(n):
            src = ins[a].at[me] if scatter else ins[a]
            cp = pltpu.make_async_copy(src, outs[a].at[me], loc_sems.at[a])
            cp.start()
            copies.append(cp)
        for k in range(1, N_DEV):
            px = 1 - x if k & 4 else x
            py = 1 - y if k & 2 else y
            pc = 1 - c if k & 1 else c
            peer = 4 * px + 2 * py + pc
            for a in range(n):
                src = ins[a].at[peer] if scatter else ins[a]
                cp = pltpu.make_async_remote_copy(
                    src_ref=src, dst_ref=outs[a].at[me],
                    send_sem=send_sems.at[a, k - 1], recv_sem=recv_sems.at[a, k - 1],
                    device_id=(px, py, pc), device_id_type=pl.DeviceIdType.MESH)
                cp.start()
                copies.append(cp)
        for cp in copies:
            cp.wait()

    any_spec = pl.BlockSpec(memory_space=pl.ANY)
    return pl.pallas_call(
        body, name=name, out_shape=tuple(out_shape),
        in_specs=[any_spec] * n, out_specs=tuple([any_spec] * n),
        scratch_shapes=[pltpu.SemaphoreType.DMA((n, N_DEV - 1)), pltpu.SemaphoreType.DMA((n, N_DEV - 1)),
                        pltpu.SemaphoreType.DMA((n,))],
        compiler_params=pltpu.CompilerParams(has_side_effects=True),
    )(*arrays)


def _gather_two_level(arrays, name):
    n = len(arrays)
    out_shape = [jax.ShapeDtypeStruct((N_DEV,) + tuple(a.shape), a.dtype) for a in arrays]

    def body(*refs):
        ins, outs = refs[:n], refs[n:2 * n]
        send_sems, recv_sems, loc_sems = refs[2 * n:]
        x, y, c = lax.axis_index("x"), lax.axis_index("y"), lax.axis_index("c")
        me, sibling = (x, y, c), (x, y, 1 - c)
        chips = [(1 - x, y), (x, 1 - y), (1 - x, 1 - y)]

        def slot(px, py, pc):
            return 4 * px + 2 * py + pc

        def copy(a, k, block, to, src=None):
            dst = outs[a].at[slot(*block)]
            return pltpu.make_async_remote_copy(
                src_ref=dst if src is None else src, dst_ref=dst,
                send_sem=send_sems.at[a, k], recv_sem=recv_sems.at[a, k],
                device_id=to, device_id_type=pl.DeviceIdType.MESH)

        local = [pltpu.make_async_copy(ins[a], outs[a].at[slot(*me)], loc_sems.at[a]) for a in range(n)]
        for cp in local:
            cp.start()
        first = []
        for a in range(n):
            first.append(copy(a, 0, me, sibling, src=ins[a]))
            first += [copy(a, 1 + j, me, (*chip, c), src=ins[a]) for j, chip in enumerate(chips)]
        for cp in first:
            cp.start()
        passed = []
        for j, chip in enumerate(chips):
            for a in range(n):
                copy(a, 1 + j, (*chip, c), me).wait_recv()
                cp = copy(a, 4 + j, (*chip, c), sibling)
                cp.start()
                passed.append(cp)
        for a in range(n):
            copy(a, 0, sibling, me).wait_recv()
            for j, chip in enumerate(chips):
                copy(a, 4 + j, (*chip, 1 - c), me).wait_recv()
        for cp in first + passed:
            cp.wait_send()
        for cp in local:
            cp.wait()

    any_spec = pl.BlockSpec(memory_space=pl.ANY)
    return pl.pallas_call(
        body, name=name, out_shape=tuple(out_shape),
        in_specs=[any_spec] * n, out_specs=tuple([any_spec] * n),
        scratch_shapes=[pltpu.SemaphoreType.DMA((n, N_DEV - 1)), pltpu.SemaphoreType.DMA((n, N_DEV - 1)),
                        pltpu.SemaphoreType.DMA((n,))],
        compiler_params=pltpu.CompilerParams(has_side_effects=True),
    )(*arrays)


N_CHIPS = 4


def _scatter_to_sibling(arrays, name):
    n = len(arrays)
    out_shape = [jax.ShapeDtypeStruct((N_CHIPS,) + tuple(a.shape[2:]), a.dtype) for a in arrays]

    def body(*refs):
        ins, outs = refs[:n], refs[n:2 * n]
        send_sems, recv_sems = refs[2 * n:]
        x, y, c = lax.axis_index("x"), lax.axis_index("y"), lax.axis_index("c")
        copies = []
        for a in range(n):
            for q in range(N_CHIPS):
                cp = pltpu.make_async_remote_copy(
                    src_ref=ins[a].at[q, 1 - c], dst_ref=outs[a].at[q],
                    send_sem=send_sems.at[a, q], recv_sem=recv_sems.at[a, q],
                    device_id=(x, y, 1 - c), device_id_type=pl.DeviceIdType.MESH)
                cp.start()
                copies.append(cp)
        for cp in copies:
            cp.wait()

    any_spec = pl.BlockSpec(memory_space=pl.ANY)
    return pl.pallas_call(
        body, name=name, out_shape=tuple(out_shape),
        in_specs=[any_spec] * n, out_specs=tuple([any_spec] * n),
        scratch_shapes=[pltpu.SemaphoreType.DMA((n, N_CHIPS)), pltpu.SemaphoreType.DMA((n, N_CHIPS))],
        compiler_params=pltpu.CompilerParams(has_side_effects=True),
    )(*arrays)


def _pair_add(mine, got, c_idx, name):
    _, _, R, C = mine.shape
    tr = 256 if (R % 256 == 0 and R > 256) else R

    def body(c_ref, m_ref, g_ref, o_ref):
        o_ref[...] = m_ref[...] + g_ref[...]

    grid_spec = pltpu.PrefetchScalarGridSpec(
        num_scalar_prefetch=1, grid=(N_CHIPS, R // tr),
        in_specs=[pl.BlockSpec((None, None, tr, C), lambda q, i, c_ref: (q, c_ref[0], i, 0)),
                  pl.BlockSpec((None, tr, C), lambda q, i, c_ref: (q, i, 0))],
        out_specs=pl.BlockSpec((None, tr, C), lambda q, i, c_ref: (q, i, 0)))
    return pl.pallas_call(
        body, name=name, grid_spec=grid_spec, out_shape=jax.ShapeDtypeStruct((N_CHIPS, R, C), mine.dtype),
        compiler_params=_params(dimension_semantics=("parallel", "parallel")),
    )(c_idx, mine, got)


def _scatter_to_chips(arrays, name):
    n = len(arrays)
    out_shape = [jax.ShapeDtypeStruct(a.shape, a.dtype) for a in arrays]

    def body(*refs):
        ins, outs = refs[:n], refs[n:2 * n]
        send_sems, recv_sems, loc_sems = refs[2 * n:]
        x, y, c = lax.axis_index("x"), lax.axis_index("y"), lax.axis_index("c")
        myq = 2 * x + y
        copies = []
        for a in range(n):
            cp = pltpu.make_async_copy(ins[a].at[myq], outs[a].at[myq], loc_sems.at[a])
            cp.start()
            copies.append(cp)
        for k in range(1, N_CHIPS):
            qx = 1 - x if k & 2 else x
            qy = 1 - y if k & 1 else y
            for a in range(n):
                cp = pltpu.make_async_remote_copy(
                    src_ref=ins[a].at[2 * qx + qy], dst_ref=outs[a].at[myq],
                    send_sem=send_sems.at[a, k - 1], recv_sem=recv_sems.at[a, k - 1],
                    device_id=(qx, qy, c), device_id_type=pl.DeviceIdType.MESH)
                cp.start()
                copies.append(cp)
        for cp in copies:
            cp.wait()

    any_spec = pl.BlockSpec(memory_space=pl.ANY)
    return pl.pallas_call(
        body, name=name, out_shape=tuple(out_shape),
        in_specs=[any_spec] * n, out_specs=tuple([any_spec] * n),
        scratch_shapes=[pltpu.SemaphoreType.DMA((n, N_CHIPS - 1)), pltpu.SemaphoreType.DMA((n, N_CHIPS - 1)),
                        pltpu.SemaphoreType.DMA((n,))],
        compiler_params=pltpu.CompilerParams(has_side_effects=True),
    )(*arrays)


def _mm(a, b, out_dtype, name, tm=1024, tn=512):
    M, K = a.shape
    _, N = b.shape
    tm, tn = _tile(M, tm), _tile(N, tn)

    def body(a_ref, b_ref, o_ref):
        o_ref[...] = _dot(a_ref[...], b_ref[...]).astype(out_dtype)

    return pl.pallas_call(
        body, name=name, out_shape=jax.ShapeDtypeStruct((M, N), out_dtype),
        grid=(M // tm, N // tn),
        in_specs=[pl.BlockSpec((tm, K), lambda i, j: (i, 0)), pl.BlockSpec((K, tn), lambda i, j: (0, j))],
        out_specs=pl.BlockSpec((tm, tn), lambda i, j: (i, j)),
        compiler_params=_params(dimension_semantics=("parallel", "parallel")),
    )(a, b)


def _mm_tn(a, b, name, tm=1408, tn=640, tk=512):
    S, M = a.shape
    _, N = b.shape
    tm, tn, tk = _tile(M, tm), _tile(N, tn), _tile(S, tk)
    nk = S // tk

    def body(a_ref, b_ref, o_ref):
        @pl.when(pl.program_id(2) == 0)
        def _():
            o_ref[...] = jnp.zeros_like(o_ref)

        o_ref[...] += _dot_tn(a_ref[...], b_ref[...])

    return pl.pallas_call(
        body, name=name, out_shape=jax.ShapeDtypeStruct((M, N), F32),
        grid=(M // tm, N // tn, nk),
        in_specs=[pl.BlockSpec((tk, tm), lambda i, j, k: (k, i)), pl.BlockSpec((tk, tn), lambda i, j, k: (k, j))],
        out_specs=pl.BlockSpec((tm, tn), lambda i, j, k: (i, j)),
        compiler_params=_params(dimension_semantics=("parallel", "parallel", "arbitrary")),
    )(a, b)


def _silu(z):
    return z * (1.0 / (1.0 + jnp.exp(-z)))


def _ada_fwd(c_all, w_shard, b_shard):
    n = w_shard.shape[1]

    def body(c_ref, w_ref, b_ref, o_ref):
        o_ref[...] = _dot(_silu(c_ref[...]), w_ref[...], precision=HIGHEST) + b_ref[...]

    return pl.pallas_call(body, name="ada_fwd", out_shape=jax.ShapeDtypeStruct((N_DEV, n), F32),
                          compiler_params=_params())(c_all, w_shard, b_shard)


def _ada_bwd(c_all_t, dmod_cols):
    D = c_all_t.shape[0]
    n = dmod_cols.shape[1]

    def body(ct_ref, dm_ref, o_ref):
        sc = _silu(ct_ref[...])
        dm = dm_ref[...]
        acc = sc[:, 0:1] * dm[0:1, :]
        for b in range(1, N_DEV):
            acc = acc + sc[:, b:b + 1] * dm[b:b + 1, :]
        o_ref[...] = acc

    return pl.pallas_call(body, name="ada_bwd", out_shape=jax.ShapeDtypeStruct((D, n), F32),
                          compiler_params=_params())(c_all_t, dmod_cols)


def _row_specs(tm, widths):
    return [pl.BlockSpec((tm, w), lambda i: (i, 0)) for w in widths]


def _vec_spec(w):
    return pl.BlockSpec((1, w), lambda i: (0, 0))


def _prenorm(x, g, scale, shift, name):
    S, D = x.shape
    tm = _tile(S, 512)

    def body(x_ref, g_ref, sc_ref, sh_ref, h_ref):
        xv = x_ref[...]
        r = lax.rsqrt(jnp.mean(xv * xv, axis=-1, keepdims=True) + EPS)
        h_ref[...] = ((xv * r) * g_ref[...] * (1.0 + sc_ref[...]) + sh_ref[...]).astype(BF16)

    return pl.pallas_call(
        body, name=name, out_shape=jax.ShapeDtypeStruct((S, D), BF16), grid=(S // tm,),
        in_specs=_row_specs(tm, [D]) + [_vec_spec(D)] * 3, out_specs=_row_specs(tm, [D])[0],
        compiler_params=_params(dimension_semantics=("parallel",)),
    )(x, g, scale, shift)


def _group_ones():
    r = lax.broadcasted_iota(jnp.int32, (LANES, LANES), 0) // HEAD_DIM
    c = lax.broadcasted_iota(jnp.int32, (LANES, LANES), 1) // HEAD_DIM
    return (r == c).astype(F32)


def _headnorm_fwd(o_f, o_s, g_f, g_s):
    S, dh = o_f.shape
    tm = _tile(S, 512)

    def body(of_ref, os_ref, gf_ref, gs_ref, mix_ref):
        ones = _group_ones()
        for part, (o_ref, g_ref) in enumerate(((of_ref, gf_ref), (os_ref, gs_ref))):
            for t in range(dh // LANES):
                cols = slice(t * LANES, (t + 1) * LANES)
                o = o_ref[:, cols]
                ms = _dot(o * o, ones, precision=HIGHEST) * (1.0 / HEAD_DIM)
                mix_ref[:, part * dh + t * LANES: part * dh + (t + 1) * LANES] = (
                    o * lax.rsqrt(ms + EPS) * g_ref[:, cols]).astype(BF16)

    return pl.pallas_call(
        body, name="headnorm_fwd", out_shape=jax.ShapeDtypeStruct((S, 2 * dh), BF16), grid=(S // tm,),
        in_specs=_row_specs(tm, [dh, dh]) + [_vec_spec(dh)] * 2, out_specs=_row_specs(tm, [2 * dh])[0],
        compiler_params=_params(dimension_semantics=("parallel",)),
    )(o_f, o_s, g_f, g_s)


def _resid_prenorm(x, a_out, gate, g, scale, shift):
    S, D = x.shape
    tm = _tile(S, 512)

    def body(x_ref, a_ref, gt_ref, g_ref, sc_ref, sh_ref, x1_ref, h_ref):
        x1 = x_ref[...] + gt_ref[...] * a_ref[...]
        x1_ref[...] = x1
        r = lax.rsqrt(jnp.mean(x1 * x1, axis=-1, keepdims=True) + EPS)
        h_ref[...] = ((x1 * r) * g_ref[...] * (1.0 + sc_ref[...]) + sh_ref[...]).astype(BF16)

    return pl.pallas_call(
        body, name="resid_prenorm", grid=(S // tm,),
        out_shape=(jax.ShapeDtypeStruct((S, D), F32), jax.ShapeDtypeStruct((S, D), BF16)),
        in_specs=_row_specs(tm, [D, D]) + [_vec_spec(D)] * 4, out_specs=tuple(_row_specs(tm, [D, D])),
        compiler_params=_params(dimension_semantics=("parallel",)),
    )(x, a_out, gate, g, scale, shift)


def _shift_down(main, halo, k):
    ext = jnp.concatenate([halo, main], axis=0)
    return pltpu.roll(ext, k, 0)[halo.shape[0]:]


def _shift_up(main, halo, k):
    ext = jnp.concatenate([main, halo], axis=0)
    n = ext.shape[0]
    return pltpu.roll(ext, n - k, 0)[:main.shape[0]]


def _conv(up, up_halo, w_ref, b_ref):
    return (w_ref[2:3, :] * up + w_ref[1:2, :] * _shift_down(up, up_halo, 1)
            + w_ref[0:1, :] * _shift_down(up, up_halo, 2) + b_ref[...])


def _prev_halo_map(tm, col_off):
    step = tm // HALO
    return lambda j, i: (jnp.maximum(i * step - 1, 0), j + col_off)


def _conv_act(up, cw, cb):
    S, F2 = up.shape
    F = F2 // 2
    tm, ct = _tile(S, 256), _tile(F, CONV_COLS)
    nct = F // ct

    def body(ug_ref, uv_ref, hg_ref, hv_ref, wg_ref, wv_ref, bg_ref, bv_ref, act_ref):
        first = pl.program_id(1) == 0
        hg = jnp.where(first, 0.0, hg_ref[...].astype(F32))
        hv = jnp.where(first, 0.0, hv_ref[...].astype(F32))
        u_g = _conv(ug_ref[...].astype(F32), hg, wg_ref, bg_ref)
        u_v = _conv(uv_ref[...].astype(F32), hv, wv_ref, bv_ref)
        act_ref[...] = (_silu(u_g) * u_v).astype(BF16)

    main = lambda off: pl.BlockSpec((tm, ct), lambda j, i: (i, j + off))
    halo = lambda off: pl.BlockSpec((HALO, ct), _prev_halo_map(tm, off))
    wspec = lambda off: pl.BlockSpec((CONV_W, ct), lambda j, i: (0, j + off))
    bspec = lambda off: pl.BlockSpec((1, ct), lambda j, i: (0, j + off))
    return pl.pallas_call(
        body, name="conv_act", out_shape=jax.ShapeDtypeStruct((S, F), BF16), grid=(nct, S // tm),
        in_specs=[main(0), main(nct), halo(0), halo(nct), wspec(0), wspec(nct), bspec(0), bspec(nct)],
        out_specs=pl.BlockSpec((tm, ct), lambda j, i: (i, j)),
        compiler_params=_params(dimension_semantics=("parallel", "parallel")),
    )(up, up, up, up, cw, cw, cb, cb)


def _conv_act_bwd(dact, up, cw, cb):
    S, F2 = up.shape
    F = F2 // 2
    tm, ct = _tile(S, 256), _tile(F, CONV_COLS)
    nct = F // ct

    def body(da_ref, ug_ref, uv_ref, hg_ref, hv_ref, wg_ref, wv_ref, bg_ref, bv_ref,
             dug_ref, duv_ref, pg_ref, pv_ref):
        first = pl.program_id(1) == 0

        @pl.when(first)
        def _():
            pg_ref[...] = jnp.zeros_like(pg_ref)
            pv_ref[...] = jnp.zeros_like(pv_ref)

        da = da_ref[...].astype(F32)
        taps = []
        for u_ref, h_ref in ((ug_ref, hg_ref), (uv_ref, hv_ref)):
            h = jnp.where(first, 0.0, h_ref[...].astype(F32))
            uu = u_ref[...].astype(F32)
            taps.append((_shift_down(uu, h, 2), _shift_down(uu, h, 1), uu))
        u_g = wg_ref[0:1, :] * taps[0][0] + wg_ref[1:2, :] * taps[0][1] + wg_ref[2:3, :] * taps[0][2] + bg_ref[...]
        u_v = wv_ref[0:1, :] * taps[1][0] + wv_ref[1:2, :] * taps[1][1] + wv_ref[2:3, :] * taps[1][2] + bv_ref[...]
        sg = 1.0 / (1.0 + jnp.exp(-u_g))
        du_g = da * u_v * (sg * (1.0 + u_g * (1.0 - sg)))
        du_v = da * (u_g * sg)
        dug_ref[...] = du_g.astype(BF16)
        duv_ref[...] = du_v.astype(BF16)
        for du, tp, p_ref in ((du_g, taps[0], pg_ref), (du_v, taps[1], pv_ref)):
            for k in range(CONV_W):
                p_ref[k:k + 1, :] += jnp.sum(du * tp[k], axis=0, keepdims=True)
            p_ref[CONV_W:CONV_W + 1, :] += jnp.sum(du, axis=0, keepdims=True)

    main = lambda off: pl.BlockSpec((tm, ct), lambda j, i: (i, j + off))
    halo = lambda off: pl.BlockSpec((HALO, ct), _prev_halo_map(tm, off))
    wspec = lambda off: pl.BlockSpec((CONV_W, ct), lambda j, i: (0, j + off))
    bspec = lambda off: pl.BlockSpec((1, ct), lambda j, i: (0, j + off))
    pspec = lambda off: pl.BlockSpec((8, ct), lambda j, i: (0, j + off))
    du_g, du_v, p_g, p_v = pl.pallas_call(
        body, name="conv_act_bwd", grid=(nct, S // tm),
        out_shape=(jax.ShapeDtypeStruct((S, F), BF16), jax.ShapeDtypeStruct((S, F), BF16),
                   jax.ShapeDtypeStruct((8, F), F32), jax.ShapeDtypeStruct((8, F), F32)),
        in_specs=[main(0), main(0), main(nct), halo(0), halo(nct), wspec(0), wspec(nct), bspec(0), bspec(nct)],
        out_specs=(main(0), main(0), pspec(0), pspec(0)),
        compiler_params=_params(dimension_semantics=("parallel", "arbitrary")),
    )(dact, up, up, up, up, cw, cw, cb, cb)
    return du_g, du_v, p_g, p_v


def _conv_bwd_input(du, cw):
    S, C = du.shape
    tm, ct = _tile(S, 256), _tile(C, CONV_COLS)
    step = tm // HALO
    last_halo = S // HALO - 1

    def body(du_ref, h_ref, w_ref, o_ref):
        last = pl.program_id(1) == pl.num_programs(1) - 1
        d = du_ref[...].astype(F32)
        h = jnp.where(last, 0.0, h_ref[...].astype(F32))
        o_ref[...] = (w_ref[2:3, :] * d + w_ref[1:2, :] * _shift_up(d, h, 1)
                      + w_ref[0:1, :] * _shift_up(d, h, 2)).astype(BF16)

    return pl.pallas_call(
        body, name="conv_bwd_input", out_shape=jax.ShapeDtypeStruct((S, C), BF16), grid=(C // ct, S // tm),
        in_specs=[pl.BlockSpec((tm, ct), lambda j, i: (i, j)),
                  pl.BlockSpec((HALO, ct), lambda j, i: (jnp.minimum((i + 1) * step, last_halo), j)),
                  pl.BlockSpec((CONV_W, ct), lambda j, i: (0, j))],
        out_specs=pl.BlockSpec((tm, ct), lambda j, i: (i, j)),
        compiler_params=_params(dimension_semantics=("parallel", "parallel")),
    )(du, du, cw)


def _scan_mats(R, nc, reverse):
    i = lax.broadcasted_iota(jnp.int32, (LANES, LANES), 0)
    j = lax.broadcasted_iota(jnp.int32, (LANES, LANES), 1)
    inner = ((i >= j) if reverse else (i <= j)).astype(F32)
    r = lax.broadcasted_iota(jnp.int32, (R, R), 0)
    c = lax.broadcasted_iota(jnp.int32, (R, R), 1)
    same = (r // nc) == (c // nc)
    outer = (same & ((c > r) if reverse else (c < r))).astype(F32)
    return inner, outer


def _chunk_scan(v, inner, outer, reverse):
    w = _dot(v, inner, precision=HIGHEST)
    col = 0 if reverse else LANES - 1
    carry = _dot(outer, w, precision=HIGHEST)[:, col:col + 1]
    return w + carry


def _fgate_fwd(z_rows, nc):
    R = z_rows.shape[0]

    def body(z_ref, f_ref):
        z = z_ref[...]
        logf = jnp.minimum(z, 0.0) - jnp.log(1.0 + jnp.exp(-jnp.abs(z)))
        inner, outer = _scan_mats(R, nc, False)
        f_ref[...] = _chunk_scan(logf, inner, outer, False)

    return pl.pallas_call(body, name="fgate_fwd", out_shape=jax.ShapeDtypeStruct((R, LANES), F32),
                          compiler_params=_params())(z_rows)


def _fgate_bwd(dfk_neg_rows, dfq_rows, z_rows, nc):
    R = z_rows.shape[0]
    nh = R // nc

    def body(dfk_ref, dfq_ref, z_ref, dz_ref, db_ref):
        inner, outer = _scan_mats(R, nc, True)
        dlogf = _chunk_scan(dfq_ref[...] - dfk_ref[...], inner, outer, True)
        dz = dlogf * (1.0 / (1.0 + jnp.exp(z_ref[...])))
        dz_ref[...] = dz
        hr = lax.broadcasted_iota(jnp.int32, (nh, R), 0)
        hc = lax.broadcasted_iota(jnp.int32, (nh, R), 1) // nc
        per_head = _dot((hr == hc).astype(F32), dz, precision=HIGHEST)
        db_ref[...] = jnp.sum(per_head, axis=1, keepdims=True)

    return pl.pallas_call(
        body, name="fgate_bwd",
        out_shape=(jax.ShapeDtypeStruct((R, LANES), F32), jax.ShapeDtypeStruct((nh, 1), F32)),
        compiler_params=_params())(dfk_neg_rows, dfq_rows, z_rows)


_NEG = -1e30
_SCALE = HEAD_DIM ** -0.5
N_SCAN = ATT_BK // SCAN_BK
F_PARTS = 3
Q_F_LANE = HEAD_DIM
Q_ONE_LANE = HEAD_DIM + F_PARTS


def _kv_slice(j):
    return pl.ds(pl.multiple_of(j * ATT_BK, ATT_BK), ATT_BK)


def _mask_t(strict):
    s = lax.broadcasted_iota(jnp.int32, (ATT_BK, ATT_BQ), 0)
    t = lax.broadcasted_iota(jnp.int32, (ATT_BK, ATT_BQ), 1)
    return (s < t) if strict else (s <= t)


def _walk(i, step, carry, descending):
    if descending:
        carry = step(i, carry, True)
        return lax.fori_loop(0, i, lambda n, cr: step(i - 1 - n, cr, False), carry)
    carry = lax.fori_loop(0, i, lambda j, cr: step(j, cr, False), carry)
    return step(i, carry, True)


def _t_block(rows):
    return pl.BlockSpec((None, rows, ATT_BQ), lambda h, i: (h, 0, i))


def _t_full(rows, S):
    return pl.BlockSpec((None, rows, S), lambda h, i: (h, 0, 0))


def _n_block():
    return pl.BlockSpec((None, ATT_BQ, LANES), lambda h, i: (h, i, 0))


def _n_full(S):
    return pl.BlockSpec((None, S, LANES), lambda h, i: (h, 0, 0))


def _heads(t):
    S = t.shape[0]
    return jnp.transpose(t.reshape(S, -1, HEAD_DIM), (1, 0, 2))


def _unheads(t):
    return jnp.transpose(t, (1, 0, 2)).reshape(t.shape[1], -1)


def _pad_lanes(t):
    return jnp.pad(t, ((0, 0), (0, 0), (0, LANES - t.shape[2])))


def _tr(t):
    return jnp.transpose(t, (0, 2, 1))


def _split3(f):
    def bf16_round(t):
        return lax.reduce_precision(t, exponent_bits=8, mantissa_bits=7)

    hi = bf16_round(f)
    mid = bf16_round(f - hi)
    lo = bf16_round(f - hi - mid)
    return jnp.stack([hi, mid, lo], axis=-1).astype(BF16)


def _fox_operands(q, k, v, f_rows):
    qh, kh, vh = _heads(q) * _SCALE, _heads(k), _heads(v)
    f3 = _split3(f_rows)
    ones = jnp.ones_like(f3)
    q_n = _pad_lanes(jnp.concatenate([qh, f3, ones], axis=-1))
    k_n = _pad_lanes(jnp.concatenate([kh, ones, -f3], axis=-1))
    return dict(q_n=q_n, q_t=_tr(q_n), k_n=k_n, k_t=_tr(k_n), v_n=_pad_lanes(vh), v_t=_tr(vh))


def _sb_operands(q, k, v):
    qh, kh, vh = _heads(q) * _SCALE, _heads(k), _heads(v)
    q_n, k_n = _pad_lanes(qh), _pad_lanes(kh)
    return dict(q_n=q_n, q_t=_tr(q_n), k_n=k_n, k_t=_tr(k_n), v_n=_pad_lanes(vh), v_t=_tr(vh))


def _do_operands(do):
    do_n = _pad_lanes(_heads(do).astype(BF16))
    return do_n, _tr(do_n)


def _fox_fwd(q_t, k_n, v_t):
    H, _, S = q_t.shape

    def body(qt_ref, k_ref, vt_ref, ot_ref, lse_ref):
        i = pl.program_id(1)
        qt = qt_ref[...]

        def step(j, carry, masked):
            m, l, acc = carry
            ks = _kv_slice(j)
            s = _dot(k_ref[ks, :], qt)
            if masked:
                s = jnp.where(_mask_t(False), s, _NEG)
            mn = jnp.maximum(m, jnp.max(s, axis=0, keepdims=True))
            alpha = jnp.exp(m - mn)
            p = jnp.exp(s - mn)
            l = alpha * l + jnp.sum(p, axis=0, keepdims=True)
            acc = acc * alpha + _dot(vt_ref[:, ks], p.astype(BF16))
            return mn, l, acc

        row = jnp.zeros((1, ATT_BQ), F32)
        m, l, acc = _walk(i, step, (row + _NEG, row, jnp.zeros((HEAD_DIM, ATT_BQ), F32)), False)
        ot_ref[...] = acc / l
        lse_ref[...] = m + jnp.log(l)

    return pl.pallas_call(
        body, name="fox_fwd", grid=(H, S // ATT_BQ),
        out_shape=(jax.ShapeDtypeStruct((H, HEAD_DIM, S), F32), jax.ShapeDtypeStruct((H, 1, S), F32)),
        in_specs=[_t_block(LANES), _n_full(S), _t_full(HEAD_DIM, S)],
        out_specs=(_t_block(HEAD_DIM), _t_block(1)),
        compiler_params=_params(dimension_semantics=("parallel", "parallel")),
    )(q_t, k_n, v_t)


def _fox_bwd(q_t, q_n, k_n, k_t, v_n, do_t, do_n, o_t, lse):
    H, _, S = q_t.shape

    def body(qt_ref, qn_ref, k_ref, kt_ref, v_ref, dot_ref, don_ref, ot_ref, lse_ref, dqt_ref, dk_ref, dv_ref):
        i = pl.program_id(1)

        @pl.when(i == 0)
        def _():
            dk_ref[...] = jnp.zeros_like(dk_ref)
            dv_ref[...] = jnp.zeros_like(dv_ref)

        qt, qn, dot, don = qt_ref[...], qn_ref[...], dot_ref[...], don_ref[...]
        lse = lse_ref[...]
        delta = jnp.sum(dot[:HEAD_DIM].astype(F32) * ot_ref[...], axis=0, keepdims=True)

        def step(j, dq, masked):
            ks = _kv_slice(j)
            s = _dot(k_ref[ks, :], qt)
            if masked:
                s = jnp.where(_mask_t(False), s, _NEG)
            p = jnp.exp(s - lse)
            ds = (p * (_dot(v_ref[ks, :], dot) - delta)).astype(BF16)
            dk_ref[ks, :] += _dot(ds, qn)
            dv_ref[ks, :] += _dot(p.astype(BF16), don)
            return dq + _dot(kt_ref[:, ks], ds)

        dqt_ref[...] = _walk(i, step, jnp.zeros((LANES, ATT_BQ), F32), False)

    return pl.pallas_call(
        body, name="fox_bwd", grid=(H, S // ATT_BQ),
        out_shape=(jax.ShapeDtypeStruct((H, LANES, S), F32), jax.ShapeDtypeStruct((H, S, LANES), F32),
                   jax.ShapeDtypeStruct((H, S, LANES), F32)),
        in_specs=[_t_block(LANES), _n_block(), _n_full(S), _t_full(LANES, S), _n_full(S),
                  _t_block(LANES), _n_block(), _t_block(HEAD_DIM), _t_block(1)],
        out_specs=(_t_block(LANES), _n_full(S), _n_full(S)),
        compiler_params=_params(dimension_semantics=("parallel", "arbitrary")),
    )(q_t, q_n, k_n, k_t, v_n, do_t, do_n, o_t, lse)


def _scan_lhs():
    r = lax.broadcasted_iota(jnp.int32, (SCAN_BK, 2 * SCAN_BK), 0)
    c = lax.broadcasted_iota(jnp.int32, (SCAN_BK, 2 * SCAN_BK), 1) % SCAN_BK
    return (c >= r).astype(BF16)


def _suffix_sum(t, lhs):
    hi = t.astype(BF16)
    lo = (t - hi.astype(F32)).astype(BF16)
    return _dot(lhs, jnp.concatenate([hi, lo], axis=0))


def _sb_scores(k, qt, mask):
    z = _dot(k, qt)
    e = jnp.exp(-jnp.abs(z))
    lb = -(jnp.maximum(z, 0.0) + jnp.log(1.0 + e))
    if mask is not None:
        lb = jnp.where(mask, lb, 0.0)
    return z, e, lb


def _scan_blocks():
    return [slice(u * SCAN_BK, (u + 1) * SCAN_BK) for u in reversed(range(N_SCAN))]


def _sb_fwd(q_t, k_n, v_t):
    H, _, S = q_t.shape

    def body(qt_ref, k_ref, vt_ref, ot_ref):
        i = pl.program_id(1)
        qt = qt_ref[...]
        lhs = _scan_lhs()

        def step(j, carry, masked):
            c, acc = carry
            ks = _kv_slice(j)
            mask = _mask_t(True) if masked else None
            z, _, lb = _sb_scores(k_ref[ks, :], qt, mask)
            parts = []
            for sl in _scan_blocks():
                rin = _suffix_sum(lb[sl], lhs)
                a = jnp.exp(z[sl] + rin + c)
                if masked:
                    a = jnp.where(mask[sl], a, 0.0)
                parts.append(a.astype(BF16))
                c = c + rin[0:1, :]
            a_all = jnp.concatenate(parts[::-1], axis=0)
            return c, acc + _dot(vt_ref[:, ks], a_all)

        carry = (jnp.zeros((1, ATT_BQ), F32), jnp.zeros((HEAD_DIM, ATT_BQ), F32))
        ot_ref[...] = _walk(i, step, carry, True)[1]

    return pl.pallas_call(
        body, name="sb_fwd", grid=(H, S // ATT_BQ),
        out_shape=jax.ShapeDtypeStruct((H, HEAD_DIM, S), F32),
        in_specs=[_t_block(LANES), _n_full(S), _t_full(HEAD_DIM, S)],
        out_specs=_t_block(HEAD_DIM),
        compiler_params=_params(dimension_semantics=("parallel", "parallel")),
    )(q_t, k_n, v_t)


def _sb_bwd(q_t, q_n, k_n, k_t, v_n, do_t, do_n, o_t):
    H, _, S = q_t.shape

    def body(qt_ref, qn_ref, k_ref, kt_ref, v_ref, dot_ref, don_ref, ot_ref, dqt_ref, dk_ref, dv_ref):
        i = pl.program_id(1)

        @pl.when(i == 0)
        def _():
            dk_ref[...] = jnp.zeros_like(dk_ref)
            dv_ref[...] = jnp.zeros_like(dv_ref)

        qt, qn, dot, don = qt_ref[...], qn_ref[...], dot_ref[...], don_ref[...]
        lhs = _scan_lhs()
        delta = jnp.sum(dot[:HEAD_DIM].astype(F32) * ot_ref[...], axis=0, keepdims=True)

        def step(j, carry, masked):
            c, g, dq = carry
            ks = _kv_slice(j)
            mask = _mask_t(True) if masked else None
            z, e, lb = _sb_scores(k_ref[ks, :], qt, mask)
            da = _dot(v_ref[ks, :], dot)
            a_parts, dz_parts = [], []
            for sl in _scan_blocks():
                rin = _suffix_sum(lb[sl], lhs)
                a = jnp.exp(z[sl] + rin + c)
                if masked:
                    a = jnp.where(mask[sl], a, 0.0)
                ab = a.astype(BF16)
                gg = ab.astype(F32) * da[sl]
                rgin = _suffix_sum(gg, lhs)
                rinv = 1.0 / (1.0 + e[sl])
                sig = jnp.where(z[sl] >= 0.0, rinv, e[sl] * rinv)
                dz = gg - sig * (delta - g - (rgin - gg))
                if masked:
                    dz = jnp.where(mask[sl], dz, 0.0)
                a_parts.append(ab)
                dz_parts.append(dz.astype(BF16))
                c = c + rin[0:1, :]
                g = g + rgin[0:1, :]
            ab_all = jnp.concatenate(a_parts[::-1], axis=0)
            dzb = jnp.concatenate(dz_parts[::-1], axis=0)
            dk_ref[ks, :] += _dot(dzb, qn)
            dv_ref[ks, :] += _dot(ab_all, don)
            return c, g, dq + _dot(kt_ref[:, ks], dzb)

        row = jnp.zeros((1, ATT_BQ), F32)
        dqt_ref[...] = _walk(i, step, (row, row, jnp.zeros((LANES, ATT_BQ), F32)), True)[2]

    return pl.pallas_call(
        body, name="sb_bwd", grid=(H, S // ATT_BQ),
        out_shape=(jax.ShapeDtypeStruct((H, LANES, S), F32), jax.ShapeDtypeStruct((H, S, LANES), F32),
                   jax.ShapeDtypeStruct((H, S, LANES), F32)),
        in_specs=[_t_block(LANES), _n_block(), _n_full(S), _t_full(LANES, S), _n_full(S),
                  _t_block(LANES), _n_block(), _t_block(HEAD_DIM)],
        out_specs=(_t_block(LANES), _n_full(S), _n_full(S)),
        compiler_params=_params(dimension_semantics=("parallel", "arbitrary")),
    )(q_t, q_n, k_n, k_t, v_n, do_t, do_n, o_t)


def _acc_spec(w):
    return pl.BlockSpec((1, w), lambda i: (0, 0))


def _loss_head(x1, m_out, gate_m, g_final, target):
    S, D = x1.shape
    tm = _tile(S, 256)

    def body(x1_ref, mo_ref, gt_ref, gf_ref, tg_ref, dx2_ref, gm_ref, loss_ref, dgf_ref, dgt_ref):
        @pl.when(pl.program_id(0) == 0)
        def _():
            loss_ref[...] = jnp.zeros_like(loss_ref)
            dgf_ref[...] = jnp.zeros_like(dgf_ref)
            dgt_ref[...] = jnp.zeros_like(dgt_ref)

        mo = mo_ref[...]
        x2 = x1_ref[...] + gt_ref[...] * mo
        r = lax.rsqrt(jnp.mean(x2 * x2, axis=-1, keepdims=True) + EPS)
        xh = x2 * r
        diff = xh * gf_ref[...] - tg_ref[...]
        loss_ref[...] += (0.5 / D) * jnp.sum(diff * diff)
        dy = diff * (1.0 / D)
        dgf_ref[...] += jnp.sum(dy * xh, axis=0, keepdims=True)
        dxh = dy * gf_ref[...]
        dx2 = r * (dxh - xh * jnp.mean(dxh * xh, axis=-1, keepdims=True))
        dx2_ref[...] = dx2
        gm_ref[...] = (dx2 * gt_ref[...]).astype(BF16)
        dgt_ref[...] += jnp.sum(dx2 * mo, axis=0, keepdims=True)

    return pl.pallas_call(
        body, name="loss_head", grid=(S // tm,),
        out_shape=(jax.ShapeDtypeStruct((S, D), F32), jax.ShapeDtypeStruct((S, D), BF16),
                   jax.ShapeDtypeStruct((1, LANES), F32), jax.ShapeDtypeStruct((1, D), F32),
                   jax.ShapeDtypeStruct((1, D), F32)),
        in_specs=_row_specs(tm, [D, D]) + [_vec_spec(D)] * 2 + _row_specs(tm, [D]),
        out_specs=tuple(_row_specs(tm, [D, D]) + [_acc_spec(LANES), _acc_spec(D), _acc_spec(D)]),
        compiler_params=_params(dimension_semantics=("arbitrary",)),
    )(x1, m_out, gate_m, g_final, target)


def _norm_bwd(dh, xin, dres, g, scale, name, gate=None, branch=None):
    S, D = xin.shape
    tm = _tile(S, 256)
    gated = gate is not None

    def body(*refs):
        if gated:
            (dh_ref, x_ref, dr_ref, g_ref, sc_ref, gt_ref, br_ref,
             dx_ref, dsc_ref, dsh_ref, dg_ref, ga_ref, dgt_ref) = refs
            sums = (dsc_ref, dsh_ref, dg_ref, dgt_ref)
        else:
            dh_ref, x_ref, dr_ref, g_ref, sc_ref, dx_ref, dsc_ref, dsh_ref, dg_ref = refs
            sums = (dsc_ref, dsh_ref, dg_ref)

        @pl.when(pl.program_id(0) == 0)
        def _():
            for s_ref in sums:
                s_ref[...] = jnp.zeros_like(s_ref)

        dhv, xv = dh_ref[...], x_ref[...]
        r = lax.rsqrt(jnp.mean(xv * xv, axis=-1, keepdims=True) + EPS)
        xh = xv * r
        dsc_ref[...] += jnp.sum(dhv * (xh * g_ref[...]), axis=0, keepdims=True)
        dsh_ref[...] += jnp.sum(dhv, axis=0, keepdims=True)
        dn = dhv * (1.0 + sc_ref[...])
        dg_ref[...] += jnp.sum(dn * xh, axis=0, keepdims=True)
        dxh = dn * g_ref[...]
        dx = dr_ref[...] + r * (dxh - xh * jnp.mean(dxh * xh, axis=-1, keepdims=True))
        dx_ref[...] = dx
        if gated:
            ga_ref[...] = (dx * gt_ref[...]).astype(BF16)
            dgt_ref[...] += jnp.sum(dx * br_ref[...], axis=0, keepdims=True)

    vec = jax.ShapeDtypeStruct((1, D), F32)
    out_shape = [jax.ShapeDtypeStruct((S, D), F32), vec, vec, vec]
    out_specs = _row_specs(tm, [D]) + [_acc_spec(D)] * 3
    in_specs = _row_specs(tm, [D, D, D]) + [_vec_spec(D)] * 2
    args = [dh, xin, dres, g, scale]
    if gated:
        out_shape += [jax.ShapeDtypeStruct((S, D), BF16), vec]
        out_specs += _row_specs(tm, [D]) + [_acc_spec(D)]
        in_specs += [_vec_spec(D)] + _row_specs(tm, [D])
        args += [gate, branch]
    return pl.pallas_call(
        body, name=name, grid=(S // tm,), out_shape=tuple(out_shape),
        in_specs=in_specs, out_specs=tuple(out_specs),
        compiler_params=_params(dimension_semantics=("arbitrary",)),
    )(*args)


def _headnorm_bwd(dmix, o_f, o_s, g_f, g_s):
    S, dh = o_f.shape
    tm = _tile(S, 256)

    def body(dm_ref, of_ref, os_ref, gf_ref, gs_ref, dof_ref, dos_ref, dgf_ref, dgs_ref):
        @pl.when(pl.program_id(0) == 0)
        def _():
            dgf_ref[...] = jnp.zeros_like(dgf_ref)
            dgs_ref[...] = jnp.zeros_like(dgs_ref)

        ones = _group_ones()
        parts = ((of_ref, gf_ref, dof_ref, dgf_ref), (os_ref, gs_ref, dos_ref, dgs_ref))
        for part, (o_ref, g_ref, do_ref, dg_ref) in enumerate(parts):
            for t in range(dh // LANES):
                cols = slice(t * LANES, (t + 1) * LANES)
                o = o_ref[:, cols]
                dm = dm_ref[:, part * dh + t * LANES: part * dh + (t + 1) * LANES]
                r = lax.rsqrt(_dot(o * o, ones, precision=HIGHEST) * (1.0 / HEAD_DIM) + EPS)
                oh = o * r
                dg_ref[:, cols] += jnp.sum(dm * oh, axis=0, keepdims=True)
                dn = dm * g_ref[:, cols]
                mean = _dot(dn * oh, ones, precision=HIGHEST) * (1.0 / HEAD_DIM)
                do_ref[:, cols] = r * (dn - oh * mean)

    vec = jax.ShapeDtypeStruct((1, dh), F32)
    return pl.pallas_call(
        body, name="headnorm_bwd", grid=(S // tm,),
        out_shape=(jax.ShapeDtypeStruct((S, dh), F32), jax.ShapeDtypeStruct((S, dh), F32), vec, vec),
        in_specs=_row_specs(tm, [2 * dh, dh, dh]) + [_vec_spec(dh)] * 2,
        out_specs=tuple(_row_specs(tm, [dh, dh]) + [_acc_spec(dh)] * 2),
        compiler_params=_params(dimension_semantics=("arbitrary",)),
    )(dmix, o_f, o_s, g_f, g_s)


def _adamw(w, gslots, m, v, name):
    R, C = w.shape
    n = gslots.shape[0]
    tr = 256 if (R % 256 == 0 and R > 256) else R
    bc1 = 1.0 - ADAM_B1 ** ADAM_STEP
    bc2 = 1.0 - ADAM_B2 ** ADAM_STEP

    def body(w_ref, gs_ref, m_ref, v_ref, g_ref, d_ref, nm_ref, nv_ref):
        g = gs_ref[0]
        for s in range(1, n):
            g = g + gs_ref[s]
        nm = ADAM_B1 * m_ref[...] + (1.0 - ADAM_B1) * g
        nv = ADAM_B2 * v_ref[...] + (1.0 - ADAM_B2) * (g * g)
        g_ref[...] = g
        nm_ref[...] = nm
        nv_ref[...] = nv
        d_ref[...] = -ADAM_LR * ((nm / bc1) / (jnp.sqrt(nv / bc2) + ADAM_EPS) + ADAM_WD * w_ref[...])

    blk = pl.BlockSpec((tr, C), lambda i: (i, 0))
    sds = jax.ShapeDtypeStruct((R, C), F32)
    return pl.pallas_call(
        body, name=name, grid=(R // tr,), out_shape=(sds,) * 4,
        in_specs=[blk, pl.BlockSpec((n, tr, C), lambda i: (0, i, 0)), blk, blk], out_specs=(blk,) * 4,
        compiler_params=_params(dimension_semantics=("parallel",)),
    )(w, gslots, m, v)


def _slot_sum(slots, name):
    n, _, C = slots.shape

    def body(s_ref, o_ref):
        acc = s_ref[0]
        for s in range(1, n):
            acc = acc + s_ref[s]
        o_ref[...] = acc

    return pl.pallas_call(body, name=name, out_shape=jax.ShapeDtypeStruct((1, C), F32),
                          compiler_params=_params())(slots)


def _pad_cols(a, n):
    return jnp.pad(a, ((0, 0), (0, n - a.shape[1])))


def _ungather(g, axis):
    if axis == 0:
        return g.reshape(g.shape[0] * g.shape[1], g.shape[2])
    return jnp.transpose(g, (1, 0, 2)).reshape(g.shape[1], g.shape[0] * g.shape[2])


def _to_slots(full, axis):
    R, C = full.shape
    if axis == 0:
        return full.reshape(N_DEV, R // N_DEV, C)
    return jnp.transpose(full.reshape(R, N_DEV, C // N_DEV), (1, 0, 2))


def kernel(x, c, w_ada, b_ada, g_attn, w_in, b_fgate, g_out_fox, g_out_sb, w_out, g_mlp, w_up, conv_w, conv_b, w_down, g_final, loss_target, m_w_ada, m_b_ada, m_g_attn, m_w_in, m_b_fgate, m_g_out_fox, m_g_out_sb, m_w_out, m_g_mlp, m_w_up, m_conv_w, m_conv_b, m_w_down, m_g_final, v_w_ada, v_b_ada, v_g_attn, v_w_in, v_b_fgate, v_g_out_fox, v_g_out_sb, v_w_out, v_g_mlp, v_w_up, v_conv_w, v_conv_b, v_w_down, v_g_final):
    S, D = x.shape[1], x.shape[2]
    dh = D // 2
    n_heads = dh // HEAD_DIM
    n_qkv = 6 * dh
    ff = w_down.shape[1] * N_DEV
    ffp = -(-ff // (2 * LANES)) * (2 * LANES)
    nc = S // LANES
    me = 4 * lax.axis_index("x") + 2 * lax.axis_index("y") + lax.axis_index("c")
    xs, tgt = x[0], loss_target[0]

    c_all, win_g, wout_g, wup_g, wdown_g, convw_g = _gather_two_level(
        [c, w_in[0].astype(BF16), w_out[0].astype(BF16), w_up[0].astype(BF16), w_down[0].astype(BF16), conv_w[0]],
        name="gather_weights")
    c_all = c_all.reshape(N_DEV, D)
    W_in = _ungather(win_g, 1)
    W_qkv, W_f = W_in[:, :n_qkv], _pad_cols(W_in[:, n_qkv:], LANES)
    W_inp = jnp.concatenate([W_qkv, W_f], axis=1)
    W_out = _ungather(wout_g, 0)
    W_up = _ungather(wup_g, 1)
    W_upp = jnp.concatenate([_pad_cols(W_up[:, :ff], ffp), _pad_cols(W_up[:, ff:], ffp)], axis=1)
    W_down = jnp.pad(_ungather(wdown_g, 0), ((0, ffp - ff), (0, 0)))
    cw_full = _ungather(convw_g, 1)
    cw = jnp.concatenate([_pad_cols(cw_full[:, :ff], ffp), _pad_cols(cw_full[:, ff:], ffp)], axis=1)
    cb = jnp.concatenate([_pad_cols(conv_b[:, :ff], ffp), _pad_cols(conv_b[:, ff:], ffp)], axis=1)

    n_ada = w_ada.shape[2]
    b_shard = lax.dynamic_slice(b_ada, (0, me * n_ada), (1, n_ada))
    mod_cols = _ada_fwd(c_all, w_ada[0], b_shard)
    (mod_g,) = _exchange([mod_cols], scatter=False, name="gather_mod")
    mod = lax.dynamic_index_in_dim(mod_g, me, axis=1, keepdims=False).reshape(6, 1, D)
    shift_a, scale_a, gate_a, shift_m, scale_m, gate_m = [mod[k] for k in range(6)]

    h1 = _prenorm(xs, g_attn, scale_a, shift_a, "prenorm_attn")
    qkv = _mm(h1, W_qkv, BF16, "proj_qkv")
    flog = _mm(h1, W_f, F32, "proj_fgate")
    zf = flog[:, :n_heads] + b_fgate
    z_rows = zf.T.reshape(n_heads * nc, LANES)
    f_rows = _fgate_fwd(z_rows, nc).reshape(n_heads, S)
    fox = _fox_operands(qkv[:, 0:dh], qkv[:, dh:2 * dh], qkv[:, 2 * dh:3 * dh], f_rows)
    sb = _sb_operands(qkv[:, 3 * dh:4 * dh], qkv[:, 4 * dh:5 * dh], qkv[:, 5 * dh:6 * dh])
    of_t, lse = _fox_fwd(fox["q_t"], fox["k_n"], fox["v_t"])
    os_t = _sb_fwd(sb["q_t"], sb["k_n"], sb["v_t"])
    o_f, o_s = _unheads(_tr(of_t)), _unheads(_tr(os_t))
    mix = _headnorm_fwd(o_f, o_s, g_out_fox, g_out_sb)
    a_out = _mm(mix, W_out, F32, "proj_out")
    x1, h2 = _resid_prenorm(xs, a_out, gate_a, g_mlp, scale_m, shift_m)
    up = _mm(h2, W_upp, BF16, "proj_up")
    act = _conv_act(up, cw, cb)
    m_out = _mm(act, W_down, F32, "proj_down")

    dx2, gm, loss_p, dg_final, dgate_m = _loss_head(x1, m_out, gate_m, g_final.reshape(1, D), tgt)
    dact = _mm(gm, W_down.T, BF16, "bwd_down_act")
    dW_down = _mm_tn(act, gm, "bwd_down_w")
    du_g, du_v, p_g, p_v = _conv_act_bwd(dact, up, cw, cb)
    du = jnp.concatenate([du_g, du_v], axis=1)
    dup = _conv_bwd_input(du, cw)
    dh2 = _mm(dup, W_upp.T, F32, "bwd_up_act")
    dW_up = _mm_tn(h2, dup, "bwd_up_w")
    dx1, dscale_m, dshift_m, dg_mlp, ga, dgate_a = _norm_bwd(
        dh2, x1, dx2, g_mlp, scale_m, "norm_mlp_bwd", gate=gate_a, branch=a_out)
    dmix = _mm(ga, W_out.T, F32, "bwd_out_act")
    dW_out = _mm_tn(mix, ga, "bwd_out_w")
    do_f, do_s, dg_fox, dg_sb = _headnorm_bwd(dmix, o_f, o_s, g_out_fox, g_out_sb)
    dof_n, dof_t = _do_operands(do_f)
    dos_n, dos_t = _do_operands(do_s)
    dqf_t, dkf, dvf = _fox_bwd(fox["q_t"], fox["q_n"], fox["k_n"], fox["k_t"], fox["v_n"], dof_t, dof_n, of_t, lse)
    dqs_t, dks, dvs = _sb_bwd(sb["q_t"], sb["q_n"], sb["k_n"], sb["k_t"], sb["v_n"], dos_t, dos_n, os_t)
    dz_rows, db_fgate = _fgate_bwd(dkf[:, :, Q_ONE_LANE].reshape(n_heads * nc, LANES),
                                   dqf_t[:, Q_F_LANE, :].reshape(n_heads * nc, LANES), z_rows, nc)
    dzf = dz_rows.reshape(n_heads, S).T

    def dq_cols(dq_t):
        return (_unheads(_tr(dq_t[:, :HEAD_DIM, :])) * _SCALE).astype(BF16)

    def dkv_cols(d):
        return _unheads(d[:, :, :HEAD_DIM]).astype(BF16)

    dproj = jnp.concatenate(
        [dq_cols(dqf_t), dkv_cols(dkf), dkv_cols(dvf), dq_cols(dqs_t), dkv_cols(dks), dkv_cols(dvs),
         _pad_cols(dzf, LANES).astype(BF16)], axis=1)
    dh1 = _mm(dproj, W_inp.T, F32, "bwd_in_act")
    dW_inp = _mm_tn(h1, dproj, "bwd_in_w")
    grad_x, dscale_a, dshift_a, dg_attn = _norm_bwd(dh1, xs, dx1, g_attn, scale_a, "norm_attn_bwd")

    dconv_b = jnp.concatenate([p_g[CONV_W:CONV_W + 1, :ff], p_v[CONV_W:CONV_W + 1, :ff]], axis=1)
    parts = [dshift_a, dscale_a, dgate_a, dshift_m, dscale_m, dgate_m,
             dg_attn, db_fgate.reshape(1, n_heads), dg_fox, dg_sb, dg_mlp, dconv_b, dg_final,
             loss_p[:, :1]]
    sizes = [p.shape[1] for p in parts]
    vec = jnp.concatenate(parts, axis=1)
    n_vec = -(-vec.shape[1] // LANES) * LANES
    vec = _pad_cols(vec, n_vec)
    (vec_g,) = _exchange([vec], scatter=False, name="gather_small")
    offs = [0]
    for s in sizes:
        offs.append(offs[-1] + s)

    def small(k0, k1=None):
        k1 = k0 if k1 is None else k1
        return vec_g[:, :, offs[k0]:offs[k1 + 1]]

    dmod_all = small(0, 5).reshape(N_DEV, 6 * D)
    dmod_cols = lax.dynamic_slice(dmod_all, (0, me * n_ada), (N_DEV, n_ada))
    dW_ada = _ada_bwd(c_all.T, dmod_cols)

    dW_in = jnp.concatenate([dW_inp[:, :n_qkv], dW_inp[:, n_qkv:n_qkv + n_heads]], axis=1)
    dW_upf = jnp.concatenate([dW_up[:, :ff], dW_up[:, ffp:ffp + ff]], axis=1)
    dcw = jnp.concatenate([p_g[:CONV_W, :ff], p_v[:CONV_W, :ff]], axis=1)
    bound = [_to_slots(dW_in, 1), _to_slots(dW_out, 0), _to_slots(dW_upf, 1), _to_slots(dW_down[:ff], 0),
             _to_slots(dcw, 1)]
    bound = [b.reshape((N_CHIPS, 2) + b.shape[1:]) for b in bound]
    got = _scatter_to_sibling(bound, "scatter_sibling")
    c_idx = lax.axis_index("c").astype(jnp.int32).reshape(1)
    chip_sums = [_pair_add(b, g, c_idx, "pair_add_%d" % k) for k, (b, g) in enumerate(zip(bound, got))]
    s_in, s_out, s_up, s_down, s_cw = _scatter_to_chips(chip_sums, "scatter_chips")

    res = {}
    res["w_ada"] = _adamw(w_ada[0], dW_ada[None], m_w_ada[0], v_w_ada[0], "adamw_w_ada")
    res["w_in"] = _adamw(w_in[0], s_in, m_w_in[0], v_w_in[0], "adamw_w_in")
    res["w_out"] = _adamw(w_out[0], s_out, m_w_out[0], v_w_out[0], "adamw_w_out")
    res["w_up"] = _adamw(w_up[0], s_up, m_w_up[0], v_w_up[0], "adamw_w_up")
    res["w_down"] = _adamw(w_down[0], s_down, m_w_down[0], v_w_down[0], "adamw_w_down")
    res["conv_w"] = _adamw(conv_w[0], s_cw, m_conv_w[0], v_conv_w[0], "adamw_conv_w")
    res["b_ada"] = _adamw(b_ada, small(0, 5), m_b_ada, v_b_ada, "adamw_b_ada")
    res["g_attn"] = _adamw(g_attn, small(6), m_g_attn, v_g_attn, "adamw_g_attn")
    res["b_fgate"] = _adamw(b_fgate, small(7), m_b_fgate, v_b_fgate, "adamw_b_fgate")
    res["g_out_fox"] = _adamw(g_out_fox, small(8), m_g_out_fox, v_g_out_fox, "adamw_g_out_fox")
    res["g_out_sb"] = _adamw(g_out_sb, small(9), m_g_out_sb, v_g_out_sb, "adamw_g_out_sb")
    res["g_mlp"] = _adamw(g_mlp, small(10), m_g_mlp, v_g_mlp, "adamw_g_mlp")
    res["conv_b"] = _adamw(conv_b, small(11), m_conv_b, v_conv_b, "adamw_conv_b")
    res["g_final"] = _adamw(g_final.reshape(1, D), small(12), m_g_final.reshape(1, D),
                            v_g_final.reshape(1, D), "adamw_g_final")
    loss = _slot_sum(_pad_cols(small(13).reshape(N_DEV, 1), LANES).reshape(N_DEV, 1, LANES), "loss_sum")[0, 0]

    names = ["w_ada", "b_ada", "g_attn", "w_in", "b_fgate", "g_out_fox", "g_out_sb", "w_out", "g_mlp",
             "w_up", "conv_w", "conv_b", "w_down", "g_final"]

    def shaped(n, a):
        if n == "g_final":
            return a.reshape(D)
        if n in ("b_ada", "g_attn", "b_fgate", "g_out_fox", "g_out_sb", "g_mlp", "conv_b"):
            return a
        return a[None]

    outs = [loss, grad_x[None]]
    for k in range(4):
        outs += [shaped(n, res[n][k]) for n in names]
    return tuple(outs)
```

```python
import jax
import jax.numpy as jnp
from jax import lax
from jax.experimental import pallas as pl
from jax.experimental.pallas import tpu as pltpu

F32 = jnp.float32
BF16 = jnp.bfloat16
HIGHEST = lax.Precision.HIGHEST

N_DEV = 8
LANES = 128
HEAD_DIM = 64
EPS = 1e-6
CONV_W = 3
CONV_COLS = 1408
HALO = 16
ATT_BQ = 512
ATT_BK = 512
SCAN_BK = 128
VMEM_LIMIT = 56 * 1024 * 1024

ADAM_LR = 0.001
ADAM_B1 = 0.9
ADAM_B2 = 0.999
ADAM_EPS = 1e-08
ADAM_WD = 0.01
ADAM_STEP = 10


def _params(**kw):
    return pltpu.CompilerParams(vmem_limit_bytes=VMEM_LIMIT, **kw)


def _tile(n, cap):
    if n <= cap:
        return n
    best = None
    for t in range(LANES, cap + 1, LANES):
        if n % t == 0:
            best = t
    assert best is not None, (n, cap)
    return best


def _dot(a, b, **kw):
    return jnp.dot(a, b, preferred_element_type=F32, **kw)


def _dot_tn(a, b):
    return lax.dot_general(a, b, (((0,), (0,)), ((), ())), preferred_element_type=F32)


def _exchange(arrays, scatter, name):
    n = len(arrays)
    out_shape = []
    for a in arrays:
        shp = a.shape[1:] if scatter else a.shape
        out_shape.append(jax.ShapeDtypeStruct((N_DEV,) + tuple(shp), a.dtype))

    def body(*refs):
        ins, outs = refs[:n], refs[n:2 * n]
        send_sems, recv_sems, loc_sems = refs[2 * n:]
        x, y, c = lax.axis_index("x"), lax.axis_index("y"), lax.axis_index("c")
        me = 4 * x + 2 * y + c
        copies = []
        for a in range(n):
            src = ins[a].at[me] if scatter else ins[a]
            cp = pltpu.make_async_copy(src, outs[a].at[me], loc_sems.at[a])
            cp.start()
            copies.append(cp)
        for k in range(1, N_DEV):
            px = 1 - x if k & 4 else x
            py = 1 - y if k & 2 else y
            pc = 1 - c if k & 1 else c
            peer = 4 * px + 2 * py + pc
            for a in range(n):
                src = ins[a].at[peer] if scatter else ins[a]
                cp = pltpu.make_async_remote_copy(
                    src_ref=src, dst_ref=outs[a].at[me],
                    send_sem=send_sems.at[a, k - 1], recv_sem=recv_sems.at[a, k - 1],
                    device_id=(px, py, pc), device_id_type=pl.DeviceIdType.MESH)
                cp.start()
                copies.append(cp)
        for cp in copies:
            cp.wait()

    any_spec = pl.BlockSpec(memory_space=pl.ANY)
    return pl.pallas_call(
        body, name=name, out_shape=tuple(out_shape),
        in_specs=[any_spec] * n, out_specs=tuple([any_spec] * n),
        scratch_shapes=[pltpu.SemaphoreType.DMA((n, N_DEV - 1)), pltpu.SemaphoreType.DMA((n, N_DEV - 1)),
                        pltpu.SemaphoreType.DMA((n,))],
        compiler_params=pltpu.CompilerParams(has_side_effects=True),
    )(*arrays)


def _gather_two_level(arrays, name):
    n = len(arrays)
    out_shape = [jax.ShapeDtypeStruct((N_DEV,) + tuple(a.shape), a.dtype) for a in arrays]

    def body(*refs):
        ins, outs = refs[:n], refs[n:2 * n]
        send_sems, recv_sems, loc_sems = refs[2 * n:]
        x, y, c = lax.axis_index("x"), lax.axis_index("y"), lax.axis_index("c")
        me, sibling = (x, y, c), (x, y, 1 - c)
        chips = [(1 - x, y), (x, 1 - y), (1 - x, 1 - y)]

        def slot(px, py, pc):
            return 4 * px + 2 * py + pc

        def copy(a, k, block, to, src=None):
            dst = outs[a].at[slot(*block)]
            return pltpu.make_async_remote_copy(
                src_ref=dst if src is None else src, dst_ref=dst,
                send_sem=send_sems.at[a, k], recv_sem=recv_sems.at[a, k],
                device_id=to, device_id_type=pl.DeviceIdType.MESH)

        local = [pltpu.make_async_copy(ins[a], outs[a].at[slot(*me)], loc_sems.at[a]) for a in range(n)]
        for cp in local:
            cp.start()
        first = []
        for a in range(n):
            first.append(copy(a, 0, me, sibling, src=ins[a]))
            first += [copy(a, 1 + j, me, (*chip, c), src=ins[a]) for j, chip in enumerate(chips)]
        for cp in first:
            cp.start()
        passed = []
        for j, chip in enumerate(chips):
            for a in range(n):
                copy(a, 1 + j, (*chip, c), me).wait_recv()
                cp = copy(a, 4 + j, (*chip, c), sibling)
                cp.start()
                passed.append(cp)
        for a in range(n):
            copy(a, 0, sibling, me).wait_recv()
            for j, chip in enumerate(chips):
                copy(a, 4 + j, (*chip, 1 - c), me).wait_recv()
        for cp in first + passed:
            cp.wait_send()
        for cp in local:
            cp.wait()

    any_spec = pl.BlockSpec(memory_space=pl.ANY)
    return pl.pallas_call(
        body, name=name, out_shape=tuple(out_shape),
        in_specs=[any_spec] * n, out_specs=tuple([any_spec] * n),
        scratch_shapes=[pltpu.SemaphoreType.DMA((n, N_DEV - 1)), pltpu.SemaphoreType.DMA((n, N_DEV - 1)),
                        pltpu.SemaphoreType.DMA((n,))],
        compiler_params=pltpu.CompilerParams(has_side_effects=True),
    )(*arrays)


N_CHIPS = 4


def _scatter_to_sibling(arrays, name):
    n = len(arrays)
    out_shape = [jax.ShapeDtypeStruct((N_CHIPS,) + tuple(a.shape[2:]), a.dtype) for a in arrays]

    def body(*refs):
        ins, outs = refs[:n], refs[n:2 * n]
        send_sems, recv_sems = refs[2 * n:]
        x, y, c = lax.axis_index("x"), lax.axis_index("y"), lax.axis_index("c")
        copies = []
        for a in range(n):
            for q in range(N_CHIPS):
                cp = pltpu.make_async_remote_copy(
                    src_ref=ins[a].at[q, 1 - c], dst_ref=outs[a].at[q],
                    send_sem=send_sems.at[a, q], recv_sem=recv_sems.at[a, q],
                    device_id=(x, y, 1 - c), device_id_type=pl.DeviceIdType.MESH)
                cp.start()
                copies.append(cp)
        for cp in copies:
            cp.wait()

    any_spec = pl.BlockSpec(memory_space=pl.ANY)
    return pl.pallas_call(
        body, name=name, out_shape=tuple(out_shape),
        in_specs=[any_spec] * n, out_specs=tuple([any_spec] * n),
        scratch_shapes=[pltpu.SemaphoreType.DMA((n, N_CHIPS)), pltpu.SemaphoreType.DMA((n, N_CHIPS))],
        compiler_params=pltpu.CompilerParams(has_side_effects=True),
    )(*arrays)


def _pair_add(mine, got, c_idx, name):
    _, _, R, C = mine.shape
    tr = 256 if (R % 256 == 0 and R > 256) else R

    def body(c_ref, m_ref, g_ref, o_ref):
        o_ref[...] = m_ref[...] + g_ref[...]

    grid_spec = pltpu.PrefetchScalarGridSpec(
        num_scalar_prefetch=1, grid=(N_CHIPS, R // tr),
        in_specs=[pl.BlockSpec((None, None, tr, C), lambda q, i, c_ref: (q, c_ref[0], i, 0)),
                  pl.BlockSpec((None, tr, C), lambda q, i, c_ref: (q, i, 0))],
        out_specs=pl.BlockSpec((None, tr, C), lambda q, i, c_ref: (q, i, 0)))
    return pl.pallas_call(
        body, name=name, grid_spec=grid_spec, out_shape=jax.ShapeDtypeStruct((N_CHIPS, R, C), mine.dtype),
        compiler_params=_params(dimension_semantics=("parallel", "parallel")),
    )(c_idx, mine, got)


def _scatter_to_chips(arrays, name):
    n = len(arrays)
    out_shape = [jax.ShapeDtypeStruct(a.shape, a.dtype) for a in arrays]

    def body(*refs):
        ins, outs = refs[:n], refs[n:2 * n]
        send_sems, recv_sems, loc_sems = refs[2 * n:]
        x, y, c = lax.axis_index("x"), lax.axis_index("y"), lax.axis_index("c")
        myq = 2 * x + y
        copies = []
        for a in range(n):
            cp = pltpu.make_async_copy(ins[a].at[myq], outs[a].at[myq], loc_sems.at[a])
            cp.start()
            copies.append(cp)
        for k in range(1, N_CHIPS):
            qx = 1 - x if k & 2 else x
            qy = 1 - y if k & 1 else y
            for a in range(n):
                cp = pltpu.make_async_remote_copy(
                    src_ref=ins[a].at[2 * qx + qy], dst_ref=outs[a].at[myq],
                    send_sem=send_sems.at[a, k - 1], recv_sem=recv_sems.at[a, k - 1],
                    device_id=(qx, qy, c), device_id_type=pl.DeviceIdType.MESH)
                cp.start()
                copies.append(cp)
        for cp in copies:
            cp.wait()

    any_spec = pl.BlockSpec(memory_space=pl.ANY)
    return pl.pallas_call(
        body, name=name, out_shape=tuple(out_shape),
        in_specs=[any_spec] * n, out_specs=tuple([any_spec] * n),
        scratch_shapes=[pltpu.SemaphoreType.DMA((n, N_CHIPS - 1)), pltpu.SemaphoreType.DMA((n, N_CHIPS - 1)),
                        pltpu.SemaphoreType.DMA((n,))],
        compiler_params=pltpu.CompilerParams(has_side_effects=True),
    )(*arrays)


def _mm(a, b, out_dtype, name, tm=1024, tn=512):
    M, K = a.shape
    _, N = b.shape
    tm, tn = _tile(M, tm), _tile(N, tn)

    def body(a_ref, b_ref, o_ref):
        o_ref[...] = _dot(a_ref[...], b_ref[...]).astype(out_dtype)

    return pl.pallas_call(
        body, name=name, out_shape=jax.ShapeDtypeStruct((M, N), out_dtype),
        grid=(M // tm, N // tn),
        in_specs=[pl.BlockSpec((tm, K), lambda i, j: (i, 0)), pl.BlockSpec((K, tn), lambda i, j: (0, j))],
        out_specs=pl.BlockSpec((tm, tn), lambda i, j: (i, j)),
        compiler_params=_params(dimension_semantics=("parallel", "parallel")),
    )(a, b)


def _mm_tn(a, b, name, tm=1408, tn=640, tk=512):
    S, M = a.shape
    _, N = b.shape
    tm, tn, tk = _tile(M, tm), _tile(N, tn), _tile(S, tk)
    nk = S // tk

    def body(a_ref, b_ref, o_ref):
        @pl.when(pl.program_id(2) == 0)
        def _():
            o_ref[...] = jnp.zeros_like(o_ref)

        o_ref[...] += _dot_tn(a_ref[...], b_ref[...])

    return pl.pallas_call(
        body, name=name, out_shape=jax.ShapeDtypeStruct((M, N), F32),
        grid=(M // tm, N // tn, nk),
        in_specs=[pl.BlockSpec((tk, tm), lambda i, j, k: (k, i)), pl.BlockSpec((tk, tn), lambda i, j, k: (k, j))],
        out_specs=pl.BlockSpec((tm, tn), lambda i, j, k: (i, j)),
        compiler_params=_params(dimension_semantics=("parallel", "parallel", "arbitrary")),
    )(a, b)


def _silu(z):
    return z * (1.0 / (1.0 + jnp.exp(-z)))


def _ada_fwd(c_all, w_shard, b_shard):
    n = w_shard.shape[1]

    def body(c_ref, w_ref, b_ref, o_ref):
        o_ref[...] = _dot(_silu(c_ref[...]), w_ref[...], precision=HIGHEST) + b_ref[...]

    return pl.pallas_call(body, name="ada_fwd", out_shape=jax.ShapeDtypeStruct((N_DEV, n), F32),
                          compiler_params=_params())(c_all, w_shard, b_shard)


def _ada_bwd(c_all_t, dmod_cols):
    D = c_all_t.shape[0]
    n = dmod_cols.shape[1]

    def body(ct_ref, dm_ref, o_ref):
        sc = _silu(ct_ref[...])
        dm = dm_ref[...]
        acc = sc[:, 0:1] * dm[0:1, :]
        for b in range(1, N_DEV):
            acc = acc + sc[:, b:b + 1] * dm[b:b + 1, :]
        o_ref[...] = acc

    return pl.pallas_call(body, name="ada_bwd", out_shape=jax.ShapeDtypeStruct((D, n), F32),
                          compiler_params=_params())(c_all_t, dmod_cols)


def _row_specs(tm, widths):
    return [pl.BlockSpec((tm, w), lambda i: (i, 0)) for w in widths]


def _vec_spec(w):
    return pl.BlockSpec((1, w), lambda i: (0, 0))


def _prenorm(x, g, scale, shift, name):
    S, D = x.shape
    tm = _tile(S, 512)

    def body(x_ref, g_ref, sc_ref, sh_ref, h_ref):
        xv = x_ref[...]
        r = lax.rsqrt(jnp.mean(xv * xv, axis=-1, keepdims=True) + EPS)
        h_ref[...] = ((xv * r) * g_ref[...] * (1.0 + sc_ref[...]) + sh_ref[...]).astype(BF16)

    return pl.pallas_call(
        body, name=name, out_shape=jax.ShapeDtypeStruct((S, D), BF16), grid=(S // tm,),
        in_specs=_row_specs(tm, [D]) + [_vec_spec(D)] * 3, out_specs=_row_specs(tm, [D])[0],
        compiler_params=_params(dimension_semantics=("parallel",)),
    )(x, g, scale, shift)


def _group_ones():
    r = lax.broadcasted_iota(jnp.int32, (LANES, LANES), 0) // HEAD_DIM
    c = lax.broadcasted_iota(jnp.int32, (LANES, LANES), 1) // HEAD_DIM
    return (r == c).astype(F32)


def _headnorm_fwd(o_f, o_s, g_f, g_s):
    S, dh = o_f.shape
    tm = _tile(S, 512)

    def body(of_ref, os_ref, gf_ref, gs_ref, mix_ref):
        ones = _group_ones()
        for part, (o_ref, g_ref) in enumerate(((of_ref, gf_ref), (os_ref, gs_ref))):
            for t in range(dh // LANES):
                cols = slice(t * LANES, (t + 1) * LANES)
                o = o_ref[:, cols]
                ms = _dot(o * o, ones, precision=HIGHEST) * (1.0 / HEAD_DIM)
                mix_ref[:, part * dh + t * LANES: part * dh + (t + 1) * LANES] = (
                    o * lax.rsqrt(ms + EPS) * g_ref[:, cols]).astype(BF16)

    return pl.pallas_call(
        body, name="headnorm_fwd", out_shape=jax.ShapeDtypeStruct((S, 2 * dh), BF16), grid=(S // tm,),
        in_specs=_row_specs(tm, [dh, dh]) + [_vec_spec(dh)] * 2, out_specs=_row_specs(tm, [2 * dh])[0],
        compiler_params=_params(dimension_semantics=("parallel",)),
    )(o_f, o_s, g_f, g_s)


def _resid_prenorm(x, a_out, gate, g, scale, shift):
    S, D = x.shape
    tm = _tile(S, 512)

    def body(x_ref, a_ref, gt_ref, g_ref, sc_ref, sh_ref, x1_ref, h_ref):
        x1 = x_ref[...] + gt_ref[...] * a_ref[...]
        x1_ref[...] = x1
        r = lax.rsqrt(jnp.mean(x1 * x1, axis=-1, keepdims=True) + EPS)
        h_ref[...] = ((x1 * r) * g_ref[...] * (1.0 + sc_ref[...]) + sh_ref[...]).astype(BF16)

    return pl.pallas_call(
        body, name="resid_prenorm", grid=(S // tm,),
        out_shape=(jax.ShapeDtypeStruct((S, D), F32), jax.ShapeDtypeStruct((S, D), BF16)),
        in_specs=_row_specs(tm, [D, D]) + [_vec_spec(D)] * 4, out_specs=tuple(_row_specs(tm, [D, D])),
        compiler_params=_params(dimension_semantics=("parallel",)),
    )(x, a_out, gate, g, scale, shift)


def _shift_down(main, halo, k):
    ext = jnp.concatenate([halo, main], axis=0)
    return pltpu.roll(ext, k, 0)[halo.shape[0]:]


def _shift_up(main, halo, k):
    ext = jnp.concatenate([main, halo], axis=0)
    n = ext.shape[0]
    return pltpu.roll(ext, n - k, 0)[:main.shape[0]]


def _conv(up, up_halo, w_ref, b_ref):
    return (w_ref[2:3, :] * up + w_ref[1:2, :] * _shift_down(up, up_halo, 1)
            + w_ref[0:1, :] * _shift_down(up, up_halo, 2) + b_ref[...])


def _prev_halo_map(tm, col_off):
    step = tm // HALO
    return lambda j, i: (jnp.maximum(i * step - 1, 0), j + col_off)


def _conv_act(up, cw, cb):
    S, F2 = up.shape
    F = F2 // 2
    tm, ct = _tile(S, 256), _tile(F, CONV_COLS)
    nct = F // ct

    def body(ug_ref, uv_ref, hg_ref, hv_ref, wg_ref, wv_ref, bg_ref, bv_ref, act_ref):
        first = pl.program_id(1) == 0
        hg = jnp.where(first, 0.0, hg_ref[...].astype(F32))
        hv = jnp.where(first, 0.0, hv_ref[...].astype(F32))
        u_g = _conv(ug_ref[...].astype(F32), hg, wg_ref, bg_ref)
        u_v = _conv(uv_ref[...].astype(F32), hv, wv_ref, bv_ref)
        act_ref[...] = (_silu(u_g) * u_v).astype(BF16)

    main = lambda off: pl.BlockSpec((tm, ct), lambda j, i: (i, j + off))
    halo = lambda off: pl.BlockSpec((HALO, ct), _prev_halo_map(tm, off))
    wspec = lambda off: pl.BlockSpec((CONV_W, ct), lambda j, i: (0, j + off))
    bspec = lambda off: pl.BlockSpec((1, ct), lambda j, i: (0, j + off))
    return pl.pallas_call(
        body, name="conv_act", out_shape=jax.ShapeDtypeStruct((S, F), BF16), grid=(nct, S // tm),
        in_specs=[main(0), main(nct), halo(0), halo(nct), wspec(0), wspec(nct), bspec(0), bspec(nct)],
        out_specs=pl.BlockSpec((tm, ct), lambda j, i: (i, j)),
        compiler_params=_params(dimension_semantics=("parallel", "parallel")),
    )(up, up, up, up, cw, cw, cb, cb)


def _conv_act_bwd(dact, up, cw, cb):
    S, F2 = up.shape
    F = F2 // 2
    tm, ct = _tile(S, 256), _tile(F, CONV_COLS)
    nct = F // ct

    def body(da_ref, ug_ref, uv_ref, hg_ref, hv_ref, wg_ref, wv_ref, bg_ref, bv_ref,
             dug_ref, duv_ref, pg_ref, pv_ref):
        first = pl.program_id(1) == 0

        @pl.when(first)
        def _():
            pg_ref[...] = jnp.zeros_like(pg_ref)
            pv_ref[...] = jnp.zeros_like(pv_ref)

        da = da_ref[...].astype(F32)
        taps = []
        for u_ref, h_ref in ((ug_ref, hg_ref), (uv_ref, hv_ref)):
            h = jnp.where(first, 0.0, h_ref[...].astype(F32))
            uu = u_ref[...].astype(F32)
            taps.append((_shift_down(uu, h, 2), _shift_down(uu, h, 1), uu))
        u_g = wg_ref[0:1, :] * taps[0][0] + wg_ref[1:2, :] * taps[0][1] + wg_ref[2:3, :] * taps[0][2] + bg_ref[...]
        u_v = wv_ref[0:1, :] * taps[1][0] + wv_ref[1:2, :] * taps[1][1] + wv_ref[2:3, :] * taps[1][2] + bv_ref[...]
        sg = 1.0 / (1.0 + jnp.exp(-u_g))
        du_g = da * u_v * (sg * (1.0 + u_g * (1.0 - sg)))
        du_v = da * (u_g * sg)
        dug_ref[...] = du_g.astype(BF16)
        duv_ref[...] = du_v.astype(BF16)
        for du, tp, p_ref in ((du_g, taps[0], pg_ref), (du_v, taps[1], pv_ref)):
            for k in range(CONV_W):
                p_ref[k:k + 1, :] += jnp.sum(du * tp[k], axis=0, keepdims=True)
            p_ref[CONV_W:CONV_W + 1, :] += jnp.sum(du, axis=0, keepdims=True)

    main = lambda off: pl.BlockSpec((tm, ct), lambda j, i: (i, j + off))
    halo = lambda off: pl.BlockSpec((HALO, ct), _prev_halo_map(tm, off))
    wspec = lambda off: pl.BlockSpec((CONV_W, ct), lambda j, i: (0, j + off))
    bspec = lambda off: pl.BlockSpec((1, ct), lambda j, i: (0, j + off))
    pspec = lambda off: pl.BlockSpec((8, ct), lambda j, i: (0, j + off))
    du_g, du_v, p_g, p_v = pl.pallas_call(
        body, name="conv_act_bwd", grid=(nct, S // tm),
        out_shape=(jax.ShapeDtypeStruct((S, F), BF16), jax.ShapeDtypeStruct((S, F), BF16),
                   jax.ShapeDtypeStruct((8, F), F32), jax.ShapeDtypeStruct((8, F), F32)),
        in_specs=[main(0), main(0), main(nct), halo(0), halo(nct), wspec(0), wspec(nct), bspec(0), bspec(nct)],
        out_specs=(main(0), main(0), pspec(0), pspec(0)),
        compiler_params=_params(dimension_semantics=("parallel", "arbitrary")),
    )(dact, up, up, up, up, cw, cw, cb, cb)
    return du_g, du_v, p_g, p_v


def _conv_bwd_input(du, cw):
    S, C = du.shape
    tm, ct = _tile(S, 256), _tile(C, CONV_COLS)
    step = tm // HALO
    last_halo = S // HALO - 1

    def body(du_ref, h_ref, w_ref, o_ref):
        last = pl.program_id(1) == pl.num_programs(1) - 1
        d = du_ref[...].astype(F32)
        h = jnp.where(last, 0.0, h_ref[...].astype(F32))
        o_ref[...] = (w_ref[2:3, :] * d + w_ref[1:2, :] * _shift_up(d, h, 1)
                      + w_ref[0:1, :] * _shift_up(d, h, 2)).astype(BF16)

    return pl.pallas_call(
        body, name="conv_bwd_input", out_shape=jax.ShapeDtypeStruct((S, C), BF16), grid=(C // ct, S // tm),
        in_specs=[pl.BlockSpec((tm, ct), lambda j, i: (i, j)),
                  pl.BlockSpec((HALO, ct), lambda j, i: (jnp.minimum((i + 1) * step, last_halo), j)),
                  pl.BlockSpec((CONV_W, ct), lambda j, i: (0, j))],
        out_specs=pl.BlockSpec((tm, ct), lambda j, i: (i, j)),
        compiler_params=_params(dimension_semantics=("parallel", "parallel")),
    )(du, du, cw)


def _scan_mats(R, nc, reverse):
    i = lax.broadcasted_iota(jnp.int32, (LANES, LANES), 0)
    j = lax.broadcasted_iota(jnp.int32, (LANES, LANES), 1)
    inner = ((i >= j) if reverse else (i <= j)).astype(F32)
    r = lax.broadcasted_iota(jnp.int32, (R, R), 0)
    c = lax.broadcasted_iota(jnp.int32, (R, R), 1)
    same = (r // nc) == (c // nc)
    outer = (same & ((c > r) if reverse else (c < r))).astype(F32)
    return inner, outer


def _chunk_scan(v, inner, outer, reverse):
    w = _dot(v, inner, precision=HIGHEST)
    col = 0 if reverse else LANES - 1
    carry = _dot(outer, w, precision=HIGHEST)[:, col:col + 1]
    return w + carry


def _fgate_fwd(z_rows, nc):
    R = z_rows.shape[0]

    def body(z_ref, f_ref):
        z = z_ref[...]
        logf = jnp.minimum(z, 0.0) - jnp.log(1.0 + jnp.exp(-jnp.abs(z)))
        inner, outer = _scan_mats(R, nc, False)
        f_ref[...] = _chunk_scan(logf, inner, outer, False)

    return pl.pallas_call(body, name="fgate_fwd", out_shape=jax.ShapeDtypeStruct((R, LANES), F32),
                          compiler_params=_params())(z_rows)


def _fgate_bwd(dfk_neg_rows, dfq_rows, z_rows, nc):
    R = z_rows.shape[0]
    nh = R // nc

    def body(dfk_ref, dfq_ref, z_ref, dz_ref, db_ref):
        inner, outer = _scan_mats(R, nc, True)
        dlogf = _chunk_scan(dfq_ref[...] - dfk_ref[...], inner, outer, True)
        dz = dlogf * (1.0 / (1.0 + jnp.exp(z_ref[...])))
        dz_ref[...] = dz
        hr = lax.broadcasted_iota(jnp.int32, (nh, R), 0)
        hc = lax.broadcasted_iota(jnp.int32, (nh, R), 1) // nc
        per_head = _dot((hr == hc).astype(F32), dz, precision=HIGHEST)
        db_ref[...] = jnp.sum(per_head, axis=1, keepdims=True)

    return pl.pallas_call(
        body, name="fgate_bwd",
        out_shape=(jax.ShapeDtypeStruct((R, LANES), F32), jax.ShapeDtypeStruct((nh, 1), F32)),
        compiler_params=_params())(dfk_neg_rows, dfq_rows, z_rows)


_NEG = -1e30
SKIP_BELOW = -106.0
_SCALE = HEAD_DIM ** -0.5
N_SCAN = ATT_BK // SCAN_BK
F_PARTS = 3
Q_F_LANE = HEAD_DIM
Q_ONE_LANE = HEAD_DIM + F_PARTS


def _kv_slice(j):
    return pl.ds(pl.multiple_of(j * ATT_BK, ATT_BK), ATT_BK)


def _mask_t(strict):
    s = lax.broadcasted_iota(jnp.int32, (ATT_BK, ATT_BQ), 0)
    t = lax.broadcasted_iota(jnp.int32, (ATT_BK, ATT_BQ), 1)
    return (s < t) if strict else (s <= t)


def _walk_down(i, step, alive, carry):
    carry = step(i, carry, True)

    def cond(st):
        n, go, _ = st
        return jnp.logical_and(n < i, go)

    def body(st):
        n, _, cr = st
        j = i - 1 - n
        cr = step(j, cr, False)
        return n + 1, alive(jnp.maximum(j - 1, 0), cr), cr

    return lax.while_loop(cond, body, (jnp.int32(0), alive(jnp.maximum(i - 1, 0), carry), carry))[2]


def _t_block(rows):
    return pl.BlockSpec((None, rows, ATT_BQ), lambda h, i, *_: (h, 0, i))


def _t_full(rows, S):
    return pl.BlockSpec((None, rows, S), lambda h, i, *_: (h, 0, 0))


def _n_block():
    return pl.BlockSpec((None, ATT_BQ, LANES), lambda h, i, *_: (h, i, 0))


def _n_full(S):
    return pl.BlockSpec((None, S, LANES), lambda h, i, *_: (h, 0, 0))


def _heads(t):
    S = t.shape[0]
    return jnp.transpose(t.reshape(S, -1, HEAD_DIM), (1, 0, 2))


def _unheads(t):
    return jnp.transpose(t, (1, 0, 2)).reshape(t.shape[1], -1)


def _pad_lanes(t):
    return jnp.pad(t, ((0, 0), (0, 0), (0, LANES - t.shape[2])))


def _tr(t):
    return jnp.transpose(t, (0, 2, 1))


def _split3(f):
    def bf16_round(t):
        return lax.reduce_precision(t, exponent_bits=8, mantissa_bits=7)

    hi = bf16_round(f)
    mid = bf16_round(f - hi)
    lo = bf16_round(f - hi - mid)
    return jnp.stack([hi, mid, lo], axis=-1).astype(BF16)


def _fox_operands(q, k, v, f_rows):
    qh, kh, vh = _heads(q) * _SCALE, _heads(k), _heads(v)
    f3 = _split3(f_rows)
    ones = jnp.ones_like(f3)
    q_n = _pad_lanes(jnp.concatenate([qh, f3, ones], axis=-1))
    k_n = _pad_lanes(jnp.concatenate([kh, ones, -f3], axis=-1))
    H, S = f_rows.shape
    f_end = f_rows.reshape(H, S // ATT_BK, ATT_BK)[:, :, -1]
    k_sq = jnp.sum(jnp.square(kh.astype(F32)), axis=-1).reshape(H, S // ATT_BK, ATT_BK)
    k_max = lax.cummax(jnp.sqrt(jnp.max(k_sq, axis=-1)), axis=1)
    return dict(q_n=q_n, q_t=_tr(q_n), k_n=k_n, k_t=_tr(k_n), v_n=_pad_lanes(vh), v_t=_tr(vh),
                f_end=f_end, k_max=k_max)


def _sb_operands(q, k, v):
    qh, kh, vh = _heads(q) * _SCALE, _heads(k), _heads(v)
    q_n, k_n = _pad_lanes(qh), _pad_lanes(kh)
    return dict(q_n=q_n, q_t=_tr(q_n), k_n=k_n, k_t=_tr(k_n), v_n=_pad_lanes(vh), v_t=_tr(vh))


def _do_operands(do):
    do_n = _pad_lanes(_heads(do).astype(BF16))
    return do_n, _tr(do_n)


def _fox_reach(qt, fend_ref, kmax_ref, h):
    qf = qt.astype(F32)
    q_norm = jnp.sqrt(jnp.sum(jnp.square(qf[:HEAD_DIM]), axis=0, keepdims=True))
    f_t = jnp.sum(qf[Q_F_LANE:Q_F_LANE + F_PARTS], axis=0, keepdims=True)
    return lambda j: q_norm * kmax_ref[h, j] + f_t - fend_ref[h, j]


def _fox_fwd(q_t, k_n, v_t, f_end, k_max):
    H, _, S = q_t.shape

    def body(fend_ref, kmax_ref, qt_ref, k_ref, vt_ref, ot_ref, lse_ref):
        h, i = pl.program_id(0), pl.program_id(1)
        qt = qt_ref[...]
        reach = _fox_reach(qt, fend_ref, kmax_ref, h)

        def step(j, carry, masked):
            m, l, acc = carry
            ks = _kv_slice(j)
            s = _dot(k_ref[ks, :], qt)
            if masked:
                s = jnp.where(_mask_t(False), s, _NEG)
            mn = jnp.maximum(m, jnp.max(s, axis=0, keepdims=True))
            alpha = jnp.exp(m - mn)
            p = jnp.exp(s - mn)
            l = alpha * l + jnp.sum(p, axis=0, keepdims=True)
            acc = acc * alpha + _dot(vt_ref[:, ks], p.astype(BF16))
            return mn, l, acc

        def alive(j, carry):
            return jnp.max(reach(j) - carry[0]) > SKIP_BELOW

        row = jnp.zeros((1, ATT_BQ), F32)
        m, l, acc = _walk_down(i, step, alive, (row + _NEG, row, jnp.zeros((HEAD_DIM, ATT_BQ), F32)))
        ot_ref[...] = acc / l
        lse_ref[...] = m + jnp.log(l)

    grid_spec = pltpu.PrefetchScalarGridSpec(
        num_scalar_prefetch=2, grid=(H, S // ATT_BQ),
        in_specs=[_t_block(LANES), _n_full(S), _t_full(HEAD_DIM, S)],
        out_specs=(_t_block(HEAD_DIM), _t_block(1)))
    return pl.pallas_call(
        body, name="fox_fwd", grid_spec=grid_spec,
        out_shape=(jax.ShapeDtypeStruct((H, HEAD_DIM, S), F32), jax.ShapeDtypeStruct((H, 1, S), F32)),
        compiler_params=_params(dimension_semantics=("parallel", "parallel")),
    )(f_end, k_max, q_t, k_n, v_t)


def _fox_bwd(q_t, q_n, k_n, k_t, v_n, do_t, do_n, o_t, lse, f_end, k_max):
    H, _, S = q_t.shape

    def body(fend_ref, kmax_ref, qt_ref, qn_ref, k_ref, kt_ref, v_ref, dot_ref, don_ref, ot_ref, lse_ref,
             dqt_ref, dk_ref, dv_ref):
        h, i = pl.program_id(0), pl.program_id(1)

        @pl.when(i == 0)
        def _():
            dk_ref[...] = jnp.zeros_like(dk_ref)
            dv_ref[...] = jnp.zeros_like(dv_ref)

        qt, qn, dot, don = qt_ref[...], qn_ref[...], dot_ref[...], don_ref[...]
        lse = lse_ref[...]
        delta = jnp.sum(dot[:HEAD_DIM].astype(F32) * ot_ref[...], axis=0, keepdims=True)
        reach = _fox_reach(qt, fend_ref, kmax_ref, h)

        def alive(j, dq):
            return jnp.max(reach(j) - lse) > SKIP_BELOW

        def step(j, dq, masked):
            ks = _kv_slice(j)
            s = _dot(k_ref[ks, :], qt)
            if masked:
                s = jnp.where(_mask_t(False), s, _NEG)
            p = jnp.exp(s - lse)
            ds = (p * (_dot(v_ref[ks, :], dot) - delta)).astype(BF16)
            dk_ref[ks, :] += _dot(ds, qn)
            dv_ref[ks, :] += _dot(p.astype(BF16), don)
            return dq + _dot(kt_ref[:, ks], ds)

        dqt_ref[...] = _walk_down(i, step, alive, jnp.zeros((LANES, ATT_BQ), F32))

    grid_spec = pltpu.PrefetchScalarGridSpec(
        num_scalar_prefetch=2, grid=(H, S // ATT_BQ),
        in_specs=[_t_block(LANES), _n_block(), _n_full(S), _t_full(LANES, S), _n_full(S),
                  _t_block(LANES), _n_block(), _t_block(HEAD_DIM), _t_block(1)],
        out_specs=(_t_block(LANES), _n_full(S), _n_full(S)))
    return pl.pallas_call(
        body, name="fox_bwd", grid_spec=grid_spec,
        out_shape=(jax.ShapeDtypeStruct((H, LANES, S), F32), jax.ShapeDtypeStruct((H, S, LANES), F32),
                   jax.ShapeDtypeStruct((H, S, LANES), F32)),
        compiler_params=_params(dimension_semantics=("parallel", "arbitrary")),
    )(f_end, k_max, q_t, q_n, k_n, k_t, v_n, do_t, do_n, o_t, lse)


def _scan_lhs():
    r = lax.broadcasted_iota(jnp.int32, (SCAN_BK, 2 * SCAN_BK), 0)
    c = lax.broadcasted_iota(jnp.int32, (SCAN_BK, 2 * SCAN_BK), 1) % SCAN_BK
    return (c >= r).astype(BF16)


def _suffix_sum(t, lhs):
    hi = t.astype(BF16)
    lo = (t - hi.astype(F32)).astype(BF16)
    return _dot(lhs, jnp.concatenate([hi, lo], axis=0))


def _sb_scores(k, qt, mask):
    z = _dot(k, qt)
    e = jnp.exp(-jnp.abs(z))
    lb = -(jnp.maximum(z, 0.0) + jnp.log(1.0 + e))
    if mask is not None:
        lb = jnp.where(mask, lb, 0.0)
    return z, e, lb


def _scan_blocks():
    return [slice(u * SCAN_BK, (u + 1) * SCAN_BK) for u in reversed(range(N_SCAN))]


def _sb_fwd(q_t, k_n, v_t):
    H, _, S = q_t.shape

    def body(qt_ref, k_ref, vt_ref, ot_ref):
        i = pl.program_id(1)
        qt = qt_ref[...]
        lhs = _scan_lhs()

        def step(j, carry, masked):
            c, acc = carry
            ks = _kv_slice(j)
            mask = _mask_t(True) if masked else None
            z, _, lb = _sb_scores(k_ref[ks, :], qt, mask)
            parts = []
            for sl in _scan_blocks():
                rin = _suffix_sum(lb[sl], lhs)
                a = jnp.exp(z[sl] + rin + c)
                if masked:
                    a = jnp.where(mask[sl], a, 0.0)
                parts.append(a.astype(BF16))
                c = c + rin[0:1, :]
            a_all = jnp.concatenate(parts[::-1], axis=0)
            return c, acc + _dot(vt_ref[:, ks], a_all)

        carry = (jnp.zeros((1, ATT_BQ), F32), jnp.zeros((HEAD_DIM, ATT_BQ), F32))
        ot_ref[...] = _walk_down(i, step, lambda j, cr: jnp.max(cr[0]) > SKIP_BELOW, carry)[1]

    return pl.pallas_call(
        body, name="sb_fwd", grid=(H, S // ATT_BQ),
        out_shape=jax.ShapeDtypeStruct((H, HEAD_DIM, S), F32),
        in_specs=[_t_block(LANES), _n_full(S), _t_full(HEAD_DIM, S)],
        out_specs=_t_block(HEAD_DIM),
        compiler_params=_params(dimension_semantics=("parallel", "parallel")),
    )(q_t, k_n, v_t)


def _sb_bwd(q_t, q_n, k_n, k_t, v_n, do_t, do_n, o_t):
    H, _, S = q_t.shape

    def body(qt_ref, qn_ref, k_ref, kt_ref, v_ref, dot_ref, don_ref, ot_ref, dqt_ref, dk_ref, dv_ref):
        i = pl.program_id(1)

        @pl.when(i == 0)
        def _():
            dk_ref[...] = jnp.zeros_like(dk_ref)
            dv_ref[...] = jnp.zeros_like(dv_ref)

        qt, qn, dot, don = qt_ref[...], qn_ref[...], dot_ref[...], don_ref[...]
        lhs = _scan_lhs()
        delta = jnp.sum(dot[:HEAD_DIM].astype(F32) * ot_ref[...], axis=0, keepdims=True)

        def step(j, carry, masked):
            c, g, dq = carry
            ks = _kv_slice(j)
            mask = _mask_t(True) if masked else None
            z, e, lb = _sb_scores(k_ref[ks, :], qt, mask)
            da = _dot(v_ref[ks, :], dot)
            a_parts, dz_parts = [], []
            for sl in _scan_blocks():
                rin = _suffix_sum(lb[sl], lhs)
                a = jnp.exp(z[sl] + rin + c)
                if masked:
                    a = jnp.where(mask[sl], a, 0.0)
                ab = a.astype(BF16)
                gg = ab.astype(F32) * da[sl]
                rgin = _suffix_sum(gg, lhs)
                rinv = 1.0 / (1.0 + e[sl])
                sig = jnp.where(z[sl] >= 0.0, rinv, e[sl] * rinv)
                dz = gg - sig * (delta - g - (rgin - gg))
                if masked:
                    dz = jnp.where(mask[sl], dz, 0.0)
                a_parts.append(ab)
                dz_parts.append(dz.astype(BF16))
                c = c + rin[0:1, :]
                g = g + rgin[0:1, :]
            ab_all = jnp.concatenate(a_parts[::-1], axis=0)
            dzb = jnp.concatenate(dz_parts[::-1], axis=0)
            dk_ref[ks, :] += _dot(dzb, qn)
            dv_ref[ks, :] += _dot(ab_all, don)
            return c, g, dq + _dot(kt_ref[:, ks], dzb)

        row = jnp.zeros((1, ATT_BQ), F32)
        carry = (row, row, jnp.zeros((LANES, ATT_BQ), F32))
        dqt_ref[...] = _walk_down(i, step, lambda j, cr: jnp.max(cr[0]) > SKIP_BELOW, carry)[2]

    return pl.pallas_call(
        body, name="sb_bwd", grid=(H, S // ATT_BQ),
        out_shape=(jax.ShapeDtypeStruct((H, LANES, S), F32), jax.ShapeDtypeStruct((H, S, LANES), F32),
                   jax.ShapeDtypeStruct((H, S, LANES), F32)),
        in_specs=[_t_block(LANES), _n_block(), _n_full(S), _t_full(LANES, S), _n_full(S),
                  _t_block(LANES), _n_block(), _t_block(HEAD_DIM)],
        out_specs=(_t_block(LANES), _n_full(S), _n_full(S)),
        compiler_params=_params(dimension_semantics=("parallel", "arbitrary")),
    )(q_t, q_n, k_n, k_t, v_n, do_t, do_n, o_t)


def _acc_spec(w):
    return pl.BlockSpec((1, w), lambda i: (0, 0))


def _loss_head(x1, m_out, gate_m, g_final, target):
    S, D = x1.shape
    tm = _tile(S, 256)

    def body(x1_ref, mo_ref, gt_ref, gf_ref, tg_ref, dx2_ref, gm_ref, loss_ref, dgf_ref, dgt_ref):
        @pl.when(pl.program_id(0) == 0)
        def _():
            loss_ref[...] = jnp.zeros_like(loss_ref)
            dgf_ref[...] = jnp.zeros_like(dgf_ref)
            dgt_ref[...] = jnp.zeros_like(dgt_ref)

        mo = mo_ref[...]
        x2 = x1_ref[...] + gt_ref[...] * mo
        r = lax.rsqrt(jnp.mean(x2 * x2, axis=-1, keepdims=True) + EPS)
        xh = x2 * r
        diff = xh * gf_ref[...] - tg_ref[...]
        loss_ref[...] += (0.5 / D) * jnp.sum(diff * diff)
        dy = diff * (1.0 / D)
        dgf_ref[...] += jnp.sum(dy * xh, axis=0, keepdims=True)
        dxh = dy * gf_ref[...]
        dx2 = r * (dxh - xh * jnp.mean(dxh * xh, axis=-1, keepdims=True))
        dx2_ref[...] = dx2
        gm_ref[...] = (dx2 * gt_ref[...]).astype(BF16)
        dgt_ref[...] += jnp.sum(dx2 * mo, axis=0, keepdims=True)

    return pl.pallas_call(
        body, name="loss_head", grid=(S // tm,),
        out_shape=(jax.ShapeDtypeStruct((S, D), F32), jax.ShapeDtypeStruct((S, D), BF16),
                   jax.ShapeDtypeStruct((1, LANES), F32), jax.ShapeDtypeStruct((1, D), F32),
                   jax.ShapeDtypeStruct((1, D), F32)),
        in_specs=_row_specs(tm, [D, D]) + [_vec_spec(D)] * 2 + _row_specs(tm, [D]),
        out_specs=tuple(_row_specs(tm, [D, D]) + [_acc_spec(LANES), _acc_spec(D), _acc_spec(D)]),
        compiler_params=_params(dimension_semantics=("arbitrary",)),
    )(x1, m_out, gate_m, g_final, target)


def _norm_bwd(dh, xin, dres, g, scale, name, gate=None, branch=None):
    S, D = xin.shape
    tm = _tile(S, 256)
    gated = gate is not None

    def body(*refs):
        if gated:
            (dh_ref, x_ref, dr_ref, g_ref, sc_ref, gt_ref, br_ref,
             dx_ref, dsc_ref, dsh_ref, dg_ref, ga_ref, dgt_ref) = refs
            sums = (dsc_ref, dsh_ref, dg_ref, dgt_ref)
        else:
            dh_ref, x_ref, dr_ref, g_ref, sc_ref, dx_ref, dsc_ref, dsh_ref, dg_ref = refs
            sums = (dsc_ref, dsh_ref, dg_ref)

        @pl.when(pl.program_id(0) == 0)
        def _():
            for s_ref in sums:
                s_ref[...] = jnp.zeros_like(s_ref)

        dhv, xv = dh_ref[...], x_ref[...]
        r = lax.rsqrt(jnp.mean(xv * xv, axis=-1, keepdims=True) + EPS)
        xh = xv * r
        dsc_ref[...] += jnp.sum(dhv * (xh * g_ref[...]), axis=0, keepdims=True)
        dsh_ref[...] += jnp.sum(dhv, axis=0, keepdims=True)
        dn = dhv * (1.0 + sc_ref[...])
        dg_ref[...] += jnp.sum(dn * xh, axis=0, keepdims=True)
        dxh = dn * g_ref[...]
        dx = dr_ref[...] + r * (dxh - xh * jnp.mean(dxh * xh, axis=-1, keepdims=True))
        dx_ref[...] = dx
        if gated:
            ga_ref[...] = (dx * gt_ref[...]).astype(BF16)
            dgt_ref[...] += jnp.sum(dx * br_ref[...], axis=0, keepdims=True)

    vec = jax.ShapeDtypeStruct((1, D), F32)
    out_shape = [jax.ShapeDtypeStruct((S, D), F32), vec, vec, vec]
    out_specs = _row_specs(tm, [D]) + [_acc_spec(D)] * 3
    in_specs = _row_specs(tm, [D, D, D]) + [_vec_spec(D)] * 2
    args = [dh, xin, dres, g, scale]
    if gated:
        out_shape += [jax.ShapeDtypeStruct((S, D), BF16), vec]
        out_specs += _row_specs(tm, [D]) + [_acc_spec(D)]
        in_specs += [_vec_spec(D)] + _row_specs(tm, [D])
        args += [gate, branch]
    return pl.pallas_call(
        body, name=name, grid=(S // tm,), out_shape=tuple(out_shape),
        in_specs=in_specs, out_specs=tuple(out_specs),
        compiler_params=_params(dimension_semantics=("arbitrary",)),
    )(*args)


def _headnorm_bwd(dmix, o_f, o_s, g_f, g_s):
    S, dh = o_f.shape
    tm = _tile(S, 256)

    def body(dm_ref, of_ref, os_ref, gf_ref, gs_ref, dof_ref, dos_ref, dgf_ref, dgs_ref):
        @pl.when(pl.program_id(0) == 0)
        def _():
            dgf_ref[...] = jnp.zeros_like(dgf_ref)
            dgs_ref[...] = jnp.zeros_like(dgs_ref)

        ones = _group_ones()
        parts = ((of_ref, gf_ref, dof_ref, dgf_ref), (os_ref, gs_ref, dos_ref, dgs_ref))
        for part, (o_ref, g_ref, do_ref, dg_ref) in enumerate(parts):
            for t in range(dh // LANES):
                cols = slice(t * LANES, (t + 1) * LANES)
                o = o_ref[:, cols]
                dm = dm_ref[:, part * dh + t * LANES: part * dh + (t + 1) * LANES]
                r = lax.rsqrt(_dot(o * o, ones, precision=HIGHEST) * (1.0 / HEAD_DIM) + EPS)
                oh = o * r
                dg_ref[:, cols] += jnp.sum(dm * oh, axis=0, keepdims=True)
                dn = dm * g_ref[:, cols]
                mean = _dot(dn * oh, ones, precision=HIGHEST) * (1.0 / HEAD_DIM)
                do_ref[:, cols] = r * (dn - oh * mean)

    vec = jax.ShapeDtypeStruct((1, dh), F32)
    return pl.pallas_call(
        body, name="headnorm_bwd", grid=(S // tm,),
        out_shape=(jax.ShapeDtypeStruct((S, dh), F32), jax.ShapeDtypeStruct((S, dh), F32), vec, vec),
        in_specs=_row_specs(tm, [2 * dh, dh, dh]) + [_vec_spec(dh)] * 2,
        out_specs=tuple(_row_specs(tm, [dh, dh]) + [_acc_spec(dh)] * 2),
        compiler_params=_params(dimension_semantics=("arbitrary",)),
    )(dmix, o_f, o_s, g_f, g_s)


def _adamw(w, gslots, m, v, name):
    R, C = w.shape
    n = gslots.shape[0]
    tr = 256 if (R % 256 == 0 and R > 256) else R
    bc1 = 1.0 - ADAM_B1 ** ADAM_STEP
    bc2 = 1.0 - ADAM_B2 ** ADAM_STEP

    def body(w_ref, gs_ref, m_ref, v_ref, g_ref, d_ref, nm_ref, nv_ref):
        g = gs_ref[0]
        for s in range(1, n):
            g = g + gs_ref[s]
        nm = ADAM_B1 * m_ref[...] + (1.0 - ADAM_B1) * g
        nv = ADAM_B2 * v_ref[...] + (1.0 - ADAM_B2) * (g * g)
        g_ref[...] = g
        nm_ref[...] = nm
        nv_ref[...] = nv
        d_ref[...] = -ADAM_LR * ((nm / bc1) / (jnp.sqrt(nv / bc2) + ADAM_EPS) + ADAM_WD * w_ref[...])

    blk = pl.BlockSpec((tr, C), lambda i: (i, 0))
    sds = jax.ShapeDtypeStruct((R, C), F32)
    return pl.pallas_call(
        body, name=name, grid=(R // tr,), out_shape=(sds,) * 4,
        in_specs=[blk, pl.BlockSpec((n, tr, C), lambda i: (0, i, 0)), blk, blk], out_specs=(blk,) * 4,
        compiler_params=_params(dimension_semantics=("parallel",)),
    )(w, gslots, m, v)


def _slot_sum(slots, name):
    n, _, C = slots.shape

    def body(s_ref, o_ref):
        acc = s_ref[0]
        for s in range(1, n):
            acc = acc + s_ref[s]
        o_ref[...] = acc

    return pl.pallas_call(body, name=name, out_shape=jax.ShapeDtypeStruct((1, C), F32),
                          compiler_params=_params())(slots)


def _pad_cols(a, n):
    return jnp.pad(a, ((0, 0), (0, n - a.shape[1])))


def _ungather(g, axis):
    if axis == 0:
        return g.reshape(g.shape[0] * g.shape[1], g.shape[2])
    return jnp.transpose(g, (1, 0, 2)).reshape(g.shape[1], g.shape[0] * g.shape[2])


def _to_slots(full, axis):
    R, C = full.shape
    if axis == 0:
        return full.reshape(N_DEV, R // N_DEV, C)
    return jnp.transpose(full.reshape(R, N_DEV, C // N_DEV), (1, 0, 2))


def kernel(x, c, w_ada, b_ada, g_attn, w_in, b_fgate, g_out_fox, g_out_sb, w_out, g_mlp, w_up, conv_w, conv_b, w_down, g_final, loss_target, m_w_ada, m_b_ada, m_g_attn, m_w_in, m_b_fgate, m_g_out_fox, m_g_out_sb, m_w_out, m_g_mlp, m_w_up, m_conv_w, m_conv_b, m_w_down, m_g_final, v_w_ada, v_b_ada, v_g_attn, v_w_in, v_b_fgate, v_g_out_fox, v_g_out_sb, v_w_out, v_g_mlp, v_w_up, v_conv_w, v_conv_b, v_w_down, v_g_final):
    S, D = x.shape[1], x.shape[2]
    dh = D // 2
    n_heads = dh // HEAD_DIM
    n_qkv = 6 * dh
    ff = w_down.shape[1] * N_DEV
    ffp = -(-ff // (2 * LANES)) * (2 * LANES)
    nc = S // LANES
    me = 4 * lax.axis_index("x") + 2 * lax.axis_index("y") + lax.axis_index("c")
    xs, tgt = x[0], loss_target[0]

    c_all, win_g, wout_g, wup_g, wdown_g, convw_g = _gather_two_level(
        [c, w_in[0].astype(BF16), w_out[0].astype(BF16), w_up[0].astype(BF16), w_down[0].astype(BF16), conv_w[0]],
        name="gather_weights")
    c_all = c_all.reshape(N_DEV, D)
    W_in = _ungather(win_g, 1)
    W_qkv, W_f = W_in[:, :n_qkv], _pad_cols(W_in[:, n_qkv:], LANES)
    W_inp = jnp.concatenate([W_qkv, W_f], axis=1)
    W_out = _ungather(wout_g, 0)
    W_up = _ungather(wup_g, 1)
    W_upp = jnp.concatenate([_pad_cols(W_up[:, :ff], ffp), _pad_cols(W_up[:, ff:], ffp)], axis=1)
    W_down = jnp.pad(_ungather(wdown_g, 0), ((0, ffp - ff), (0, 0)))
    cw_full = _ungather(convw_g, 1)
    cw = jnp.concatenate([_pad_cols(cw_full[:, :ff], ffp), _pad_cols(cw_full[:, ff:], ffp)], axis=1)
    cb = jnp.concatenate([_pad_cols(conv_b[:, :ff], ffp), _pad_cols(conv_b[:, ff:], ffp)], axis=1)

    n_ada = w_ada.shape[2]
    b_shard = lax.dynamic_slice(b_ada, (0, me * n_ada), (1, n_ada))
    mod_cols = _ada_fwd(c_all, w_ada[0], b_shard)
    (mod_g,) = _exchange([mod_cols], scatter=False, name="gather_mod")
    mod = lax.dynamic_index_in_dim(mod_g, me, axis=1, keepdims=False).reshape(6, 1, D)
    shift_a, scale_a, gate_a, shift_m, scale_m, gate_m = [mod[k] for k in range(6)]

    h1 = _prenorm(xs, g_attn, scale_a, shift_a, "prenorm_attn")
    qkv = _mm(h1, W_qkv, BF16, "proj_qkv")
    flog = _mm(h1, W_f, F32, "proj_fgate")
    zf = flog[:, :n_heads] + b_fgate
    z_rows = zf.T.reshape(n_heads * nc, LANES)
    f_rows = _fgate_fwd(z_rows, nc).reshape(n_heads, S)
    fox = _fox_operands(qkv[:, 0:dh], qkv[:, dh:2 * dh], qkv[:, 2 * dh:3 * dh], f_rows)
    sb = _sb_operands(qkv[:, 3 * dh:4 * dh], qkv[:, 4 * dh:5 * dh], qkv[:, 5 * dh:6 * dh])
    of_t, lse = _fox_fwd(fox["q_t"], fox["k_n"], fox["v_t"], fox["f_end"], fox["k_max"])
    os_t = _sb_fwd(sb["q_t"], sb["k_n"], sb["v_t"])
    o_f, o_s = _unheads(_tr(of_t)), _unheads(_tr(os_t))
    mix = _headnorm_fwd(o_f, o_s, g_out_fox, g_out_sb)
    a_out = _mm(mix, W_out, F32, "proj_out")
    x1, h2 = _resid_prenorm(xs, a_out, gate_a, g_mlp, scale_m, shift_m)
    up = _mm(h2, W_upp, BF16, "proj_up")
    act = _conv_act(up, cw, cb)
    m_out = _mm(act, W_down, F32, "proj_down")

    dx2, gm, loss_p, dg_final, dgate_m = _loss_head(x1, m_out, gate_m, g_final.reshape(1, D), tgt)
    dact = _mm(gm, W_down.T, BF16, "bwd_down_act")
    dW_down = _mm_tn(act, gm, "bwd_down_w")
    du_g, du_v, p_g, p_v = _conv_act_bwd(dact, up, cw, cb)
    du = jnp.concatenate([du_g, du_v], axis=1)
    dup = _conv_bwd_input(du, cw)
    dh2 = _mm(dup, W_upp.T, F32, "bwd_up_act")
    dW_up = _mm_tn(h2, dup, "bwd_up_w")
    dx1, dscale_m, dshift_m, dg_mlp, ga, dgate_a = _norm_bwd(
        dh2, x1, dx2, g_mlp, scale_m, "norm_mlp_bwd", gate=gate_a, branch=a_out)
    dmix = _mm(ga, W_out.T, F32, "bwd_out_act")
    dW_out = _mm_tn(mix, ga, "bwd_out_w")
    do_f, do_s, dg_fox, dg_sb = _headnorm_bwd(dmix, o_f, o_s, g_out_fox, g_out_sb)
    dof_n, dof_t = _do_operands(do_f)
    dos_n, dos_t = _do_operands(do_s)
    dqf_t, dkf, dvf = _fox_bwd(fox["q_t"], fox["q_n"], fox["k_n"], fox["k_t"], fox["v_n"], dof_t, dof_n, of_t, lse,
                              fox["f_end"], fox["k_max"])
    dqs_t, dks, dvs = _sb_bwd(sb["q_t"], sb["q_n"], sb["k_n"], sb["k_t"], sb["v_n"], dos_t, dos_n, os_t)
    dz_rows, db_fgate = _fgate_bwd(dkf[:, :, Q_ONE_LANE].reshape(n_heads * nc, LANES),
                                   dqf_t[:, Q_F_LANE, :].reshape(n_heads * nc, LANES), z_rows, nc)
    dzf = dz_rows.reshape(n_heads, S).T

    def dq_cols(dq_t):
        return (_unheads(_tr(dq_t[:, :HEAD_DIM, :])) * _SCALE).astype(BF16)

    def dkv_cols(d):
        return _unheads(d[:, :, :HEAD_DIM]).astype(BF16)

    dproj = jnp.concatenate(
        [dq_cols(dqf_t), dkv_cols(dkf), dkv_cols(dvf), dq_cols(dqs_t), dkv_cols(dks), dkv_cols(dvs),
         _pad_cols(dzf, LANES).astype(BF16)], axis=1)
    dh1 = _mm(dproj, W_inp.T, F32, "bwd_in_act")
    dW_inp = _mm_tn(h1, dproj, "bwd_in_w")
    grad_x, dscale_a, dshift_a, dg_attn = _norm_bwd(dh1, xs, dx1, g_attn, scale_a, "norm_attn_bwd")

    dconv_b = jnp.concatenate([p_g[CONV_W:CONV_W + 1, :ff], p_v[CONV_W:CONV_W + 1, :ff]], axis=1)
    parts = [dshift_a, dscale_a, dgate_a, dshift_m, dscale_m, dgate_m,
             dg_attn, db_fgate.reshape(1, n_heads), dg_fox, dg_sb, dg_mlp, dconv_b, dg_final,
             loss_p[:, :1]]
    sizes = [p.shape[1] for p in parts]
    vec = jnp.concatenate(parts, axis=1)
    n_vec = -(-vec.shape[1] // LANES) * LANES
    vec = _pad_cols(vec, n_vec)
    (vec_g,) = _exchange([vec], scatter=False, name="gather_small")
    offs = [0]
    for s in sizes:
        offs.append(offs[-1] + s)

    def small(k0, k1=None):
        k1 = k0 if k1 is None else k1
        return vec_g[:, :, offs[k0]:offs[k1 + 1]]

    dmod_all = small(0, 5).reshape(N_DEV, 6 * D)
    dmod_cols = lax.dynamic_slice(dmod_all, (0, me * n_ada), (N_DEV, n_ada))
    dW_ada = _ada_bwd(c_all.T, dmod_cols)

    dW_in = jnp.concatenate([dW_inp[:, :n_qkv], dW_inp[:, n_qkv:n_qkv + n_heads]], axis=1)
    dW_upf = jnp.concatenate([dW_up[:, :ff], dW_up[:, ffp:ffp + ff]], axis=1)
    dcw = jnp.concatenate([p_g[:CONV_W, :ff], p_v[:CONV_W, :ff]], axis=1)
    bound = [_to_slots(dW_in, 1), _to_slots(dW_out, 0), _to_slots(dW_upf, 1), _to_slots(dW_down[:ff], 0),
             _to_slots(dcw, 1)]
    bound = [b.reshape((N_CHIPS, 2) + b.shape[1:]) for b in bound]
    got = _scatter_to_sibling(bound, "scatter_sibling")
    c_idx = lax.axis_index("c").astype(jnp.int32).reshape(1)
    chip_sums = [_pair_add(b, g, c_idx, "pair_add_%d" % k) for k, (b, g) in enumerate(zip(bound, got))]
    s_in, s_out, s_up, s_down, s_cw = _scatter_to_chips(chip_sums, "scatter_chips")

    res = {}
    res["w_ada"] = _adamw(w_ada[0], dW_ada[None], m_w_ada[0], v_w_ada[0], "adamw_w_ada")
    res["w_in"] = _adamw(w_in[0], s_in, m_w_in[0], v_w_in[0], "adamw_w_in")
    res["w_out"] = _adamw(w_out[0], s_out, m_w_out[0], v_w_out[0], "adamw_w_out")
    res["w_up"] = _adamw(w_up[0], s_up, m_w_up[0], v_w_up[0], "adamw_w_up")
    res["w_down"] = _adamw(w_down[0], s_down, m_w_down[0], v_w_down[0], "adamw_w_down")
    res["conv_w"] = _adamw(conv_w[0], s_cw, m_conv_w[0], v_conv_w[0], "adamw_conv_w")
    res["b_ada"] = _adamw(b_ada, small(0, 5), m_b_ada, v_b_ada, "adamw_b_ada")
    res["g_attn"] = _adamw(g_attn, small(6), m_g_attn, v_g_attn, "adamw_g_attn")
    res["b_fgate"] = _adamw(b_fgate, small(7), m_b_fgate, v_b_fgate, "adamw_b_fgate")
    res["g_out_fox"] = _adamw(g_out_fox, small(8), m_g_out_fox, v_g_out_fox, "adamw_g_out_fox")
    res["g_out_sb"] = _adamw(g_out_sb, small(9), m_g_out_sb, v_g_out_sb, "adamw_g_out_sb")
    res["g_mlp"] = _adamw(g_mlp, small(10), m_g_mlp, v_g_mlp, "adamw_g_mlp")
    res["conv_b"] = _adamw(conv_b, small(11), m_conv_b, v_conv_b, "adamw_conv_b")
    res["g_final"] = _adamw(g_final.reshape(1, D), small(12), m_g_final.reshape(1, D),
                            v_g_final.reshape(1, D), "adamw_g_final")
    loss = _slot_sum(_pad_cols(small(13).reshape(N_DEV, 1), LANES).reshape(N_DEV, 1, LANES), "loss_sum")[0, 0]

    names = ["w_ada", "b_ada", "g_attn", "w_in", "b_fgate", "g_out_fox", "g_out_sb", "w_out", "g_mlp",
             "w_up", "conv_w", "conv_b", "w_down", "g_final"]

    def shaped(n, a):
        if n == "g_final":
            return a.reshape(D)
        if n in ("b_ada", "g_attn", "b_fgate", "g_out_fox", "g_out_sb", "g_mlp", "conv_b"):
            return a
        return a[None]

    outs = [loss, grad_x[None]]
    for k in range(4):
        outs += [shaped(n, res[n][k]) for n in names]
    return tuple(outs)
```

```python
import jax
import jax.numpy as jnp
from jax import lax
from jax.experimental import pallas as pl
from jax.experimental.pallas import tpu as pltpu

F32 = jnp.float32
BF16 = jnp.bfloat16
HIGHEST = lax.Precision.HIGHEST

N_DEV = 8
LANES = 128
HEAD_DIM = 64
EPS = 1e-6
CONV_W = 3
CONV_COLS = 1408
HALO = 16
ATT_BQ = 512
ATT_BK = 512
SCAN_BK = 128
VMEM_LIMIT = 56 * 1024 * 1024

ADAM_LR = 0.001
ADAM_B1 = 0.9
ADAM_B2 = 0.999
ADAM_EPS = 1e-08
ADAM_WD = 0.01
ADAM_STEP = 10


def _params(**kw):
    return pltpu.CompilerParams(vmem_limit_bytes=VMEM_LIMIT, **kw)


def _tile(n, cap):
    if n <= cap:
        return n
    best = None
    for t in range(LANES, cap + 1, LANES):
        if n % t == 0:
            best = t
    assert best is not None, (n, cap)
    return best


def _dot(a, b, **kw):
    return jnp.dot(a, b, preferred_element_type=F32, **kw)


def _exchange(arrays, scatter, name):
    n = len(arrays)
    out_shape = []
    for a in arrays:
        shp = a.shape[1:] if scatter else a.shape
        out_shape.append(jax.ShapeDtypeStruct((N_DEV,) + tuple(shp), a.dtype))

    def body(*refs):
        ins, outs = refs[:n], refs[n:2 * n]
        send_sems, recv_sems, loc_sems = refs[2 * n:]
        x, y, c = lax.axis_index("x"), lax.axis_index("y"), lax.axis_index("c")
        me = 4 * x + 2 * y + c
        copies = []
        for a in range(n):
            src = ins[a].at[me] if scatter else ins[a]
            cp = pltpu.make_async_copy(src, outs[a].at[me], loc_sems.at[a])
            cp.start()
            copies.append(cp)
        for k in range(1, N_DEV):
            px = 1 - x if k & 4 else x
            py = 1 - y if k & 2 else y
            pc = 1 - c if k & 1 else c
            peer = 4 * px + 2 * py + pc
            for a in range(n):
                src = ins[a].at[peer] if scatter else ins[a]
                cp = pltpu.make_async_remote_copy(
                    src_ref=src, dst_ref=outs[a].at[me],
                    send_sem=send_sems.at[a, k - 1], recv_sem=recv_sems.at[a, k - 1],
                    device_id=(px, py, pc), device_id_type=pl.DeviceIdType.MESH)
                cp.start()
                copies.append(cp)
        for cp in copies:
            cp.wait()

    any_spec = pl.BlockSpec(memory_space=pl.ANY)
    return pl.pallas_call(
        body, name=name, out_shape=tuple(out_shape),
        in_specs=[any_spec] * n, out_specs=tuple([any_spec] * n),
        scratch_shapes=[pltpu.SemaphoreType.DMA((n, N_DEV - 1)), pltpu.SemaphoreType.DMA((n, N_DEV - 1)),
                        pltpu.SemaphoreType.DMA((n,))],
        compiler_params=pltpu.CompilerParams(has_side_effects=True),
    )(*arrays)


def _gather_two_level(arrays, name):
    n = len(arrays)
    out_shape = [jax.ShapeDtypeStruct((N_DEV,) + tuple(a.shape), a.dtype) for a in arrays]

    def body(*refs):
        ins, outs = refs[:n], refs[n:2 * n]
        send_sems, recv_sems, loc_sems = refs[2 * n:]
        x, y, c = lax.axis_index("x"), lax.axis_index("y"), lax.axis_index("c")
        me, sibling = (x, y, c), (x, y, 1 - c)
        chips = [(1 - x, y), (x, 1 - y), (1 - x, 1 - y)]

        def slot(px, py, pc):
            return 4 * px + 2 * py + pc

        def copy(a, k, block, to, src=None):
            dst = outs[a].at[slot(*block)]
            return pltpu.make_async_remote_copy(
                src_ref=dst if src is None else src, dst_ref=dst,
                send_sem=send_sems.at[a, k], recv_sem=recv_sems.at[a, k],
                device_id=to, device_id_type=pl.DeviceIdType.MESH)

        local = [pltpu.make_async_copy(ins[a], outs[a].at[slot(*me)], loc_sems.at[a]) for a in range(n)]
        for cp in local:
            cp.start()
        first = []
        for a in range(n):
            first.append(copy(a, 0, me, sibling, src=ins[a]))
            first += [copy(a, 1 + j, me, (*chip, c), src=ins[a]) for j, chip in enumerate(chips)]
        for cp in first:
            cp.start()
        passed = []
        for j, chip in enumerate(chips):
            for a in range(n):
                copy(a, 1 + j, (*chip, c), me).wait_recv()
                cp = copy(a, 4 + j, (*chip, c), sibling)
                cp.start()
                passed.append(cp)
        for a in range(n):
            copy(a, 0, sibling, me).wait_recv()
            for j, chip in enumerate(chips):
                copy(a, 4 + j, (*chip, 1 - c), me).wait_recv()
        for cp in first + passed:
            cp.wait_send()
        for cp in local:
            cp.wait()

    any_spec = pl.BlockSpec(memory_space=pl.ANY)
    return pl.pallas_call(
        body, name=name, out_shape=tuple(out_shape),
        in_specs=[any_spec] * n, out_specs=tuple([any_spec] * n),
        scratch_shapes=[pltpu.SemaphoreType.DMA((n, N_DEV - 1)), pltpu.SemaphoreType.DMA((n, N_DEV - 1)),
                        pltpu.SemaphoreType.DMA((n,))],
        compiler_params=pltpu.CompilerParams(has_side_effects=True),
    )(*arrays)


N_CHIPS = 4


def _scatter_to_sibling(arrays, name):
    n = len(arrays)
    out_shape = [jax.ShapeDtypeStruct((N_CHIPS,) + tuple(a.shape[2:]), a.dtype) for a in arrays]

    def body(*refs):
        ins, outs = refs[:n], refs[n:2 * n]
        send_sems, recv_sems = refs[2 * n:]
        x, y, c = lax.axis_index("x"), lax.axis_index("y"), lax.axis_index("c")
        copies = []
        for a in range(n):
            for q in range(N_CHIPS):
                cp = pltpu.make_async_remote_copy(
                    src_ref=ins[a].at[q, 1 - c], dst_ref=outs[a].at[q],
                    send_sem=send_sems.at[a, q], recv_sem=recv_sems.at[a, q],
                    device_id=(x, y, 1 - c), device_id_type=pl.DeviceIdType.MESH)
                cp.start()
                copies.append(cp)
        for cp in copies:
            cp.wait()

    any_spec = pl.BlockSpec(memory_space=pl.ANY)
    return pl.pallas_call(
        body, name=name, out_shape=tuple(out_shape),
        in_specs=[any_spec] * n, out_specs=tuple([any_spec] * n),
        scratch_shapes=[pltpu.SemaphoreType.DMA((n, N_CHIPS)), pltpu.SemaphoreType.DMA((n, N_CHIPS))],
        compiler_params=pltpu.CompilerParams(has_side_effects=True),
    )(*arrays)


def _pair_add(mine, got, c_idx, name):
    _, _, R, C = mine.shape
    tr = 256 if (R % 256 == 0 and R > 256) else R

    def body(c_ref, m_ref, g_ref, o_ref):
        o_ref[...] = m_ref[...] + g_ref[...]

    grid_spec = pltpu.PrefetchScalarGridSpec(
        num_scalar_prefetch=1, grid=(N_CHIPS, R // tr),
        in_specs=[pl.BlockSpec((None, None, tr, C), lambda q, i, c_ref: (q, c_ref[0], i, 0)),
                  pl.BlockSpec((None, tr, C), lambda q, i, c_ref: (q, i, 0))],
        out_specs=pl.BlockSpec((None, tr, C), lambda q, i, c_ref: (q, i, 0)))
    return pl.pallas_call(
        body, name=name, grid_spec=grid_spec, out_shape=jax.ShapeDtypeStruct((N_CHIPS, R, C), mine.dtype),
        compiler_params=_params(dimension_semantics=("parallel", "parallel")),
    )(c_idx, mine, got)


def _scatter_to_chips(arrays, name):
    n = len(arrays)
    out_shape = [jax.ShapeDtypeStruct(a.shape, a.dtype) for a in arrays]

    def body(*refs):
        ins, outs = refs[:n], refs[n:2 * n]
        send_sems, recv_sems, loc_sems = refs[2 * n:]
        x, y, c = lax.axis_index("x"), lax.axis_index("y"), lax.axis_index("c")
        myq = 2 * x + y
        copies = []
        for a in range(n):
            cp = pltpu.make_async_copy(ins[a].at[myq], outs[a].at[myq], loc_sems.at[a])
            cp.start()
            copies.append(cp)
        for k in range(1, N_CHIPS):
            qx = 1 - x if k & 2 else x
            qy = 1 - y if k & 1 else y
            for a in range(n):
                cp = pltpu.make_async_remote_copy(
                    src_ref=ins[a].at[2 * qx + qy], dst_ref=outs[a].at[myq],
                    send_sem=send_sems.at[a, k - 1], recv_sem=recv_sems.at[a, k - 1],
                    device_id=(qx, qy, c), device_id_type=pl.DeviceIdType.MESH)
                cp.start()
                copies.append(cp)
        for cp in copies:
            cp.wait()

    any_spec = pl.BlockSpec(memory_space=pl.ANY)
    return pl.pallas_call(
        body, name=name, out_shape=tuple(out_shape),
        in_specs=[any_spec] * n, out_specs=tuple([any_spec] * n),
        scratch_shapes=[pltpu.SemaphoreType.DMA((n, N_CHIPS - 1)), pltpu.SemaphoreType.DMA((n, N_CHIPS - 1)),
                        pltpu.SemaphoreType.DMA((n,))],
        compiler_params=pltpu.CompilerParams(has_side_effects=True),
    )(*arrays)


def _mm(a, b, out_dtype, name, tm=1024, tn=512):
    M, K = a.shape
    _, N = b.shape
    tm, tn = _tile(M, tm), _tile(N, tn)

    def body(a_ref, b_ref, o_ref):
        o_ref[...] = _dot(a_ref[...], b_ref[...]).astype(out_dtype)

    return pl.pallas_call(
        body, name=name, out_shape=jax.ShapeDtypeStruct((M, N), out_dtype),
        grid=(M // tm, N // tn),
        in_specs=[pl.BlockSpec((tm, K), lambda i, j: (i, 0)), pl.BlockSpec((K, tn), lambda i, j: (0, j))],
        out_specs=pl.BlockSpec((tm, tn), lambda i, j: (i, j)),
        compiler_params=_params(dimension_semantics=("parallel", "parallel")),
    )(a, b)


def _mm2(a1, b1, a2, b2, out_dtype, name, tm=1024, tn=512):
    M, K1 = a1.shape
    _, K2 = a2.shape
    _, N = b1.shape
    tm, tn = _tile(M, tm), _tile(N, tn)

    def body(a1_ref, b1_ref, a2_ref, b2_ref, o_ref):
        o_ref[...] = (_dot(a1_ref[...], b1_ref[...]) + _dot(a2_ref[...], b2_ref[...])).astype(out_dtype)

    return pl.pallas_call(
        body, name=name, out_shape=jax.ShapeDtypeStruct((M, N), out_dtype),
        grid=(M // tm, N // tn),
        in_specs=[pl.BlockSpec((tm, K1), lambda i, j: (i, 0)), pl.BlockSpec((K1, tn), lambda i, j: (0, j)),
                  pl.BlockSpec((tm, K2), lambda i, j: (i, 0)), pl.BlockSpec((K2, tn), lambda i, j: (0, j))],
        out_specs=pl.BlockSpec((tm, tn), lambda i, j: (i, j)),
        compiler_params=_params(dimension_semantics=("parallel", "parallel")),
    )(a1, b1, a2, b2)


def _mm_acc(a, b, name, tm=1408, tn=1408, tk=512):
    M, S = a.shape
    _, N = b.shape
    tm, tn, tk = _tile(M, tm), _tile(N, tn), _tile(S, tk)

    def body(a_ref, b_ref, o_ref):
        @pl.when(pl.program_id(2) == 0)
        def _():
            o_ref[...] = jnp.zeros_like(o_ref)

        o_ref[...] += _dot(a_ref[...], b_ref[...])

    return pl.pallas_call(
        body, name=name, out_shape=jax.ShapeDtypeStruct((M, N), F32),
        grid=(M // tm, N // tn, S // tk),
        in_specs=[pl.BlockSpec((tm, tk), lambda i, j, k: (i, k)), pl.BlockSpec((tk, tn), lambda i, j, k: (k, j))],
        out_specs=pl.BlockSpec((tm, tn), lambda i, j, k: (i, j)),
        compiler_params=_params(dimension_semantics=("parallel", "parallel", "arbitrary")),
    )(a, b)


def _silu(z):
    return z * (1.0 / (1.0 + jnp.exp(-z)))


def _ada_fwd(c_all, w_shard, b_shard):
    n = w_shard.shape[1]

    def body(c_ref, w_ref, b_ref, o_ref):
        o_ref[...] = _dot(_silu(c_ref[...]), w_ref[...], precision=HIGHEST) + b_ref[...]

    return pl.pallas_call(body, name="ada_fwd", out_shape=jax.ShapeDtypeStruct((N_DEV, n), F32),
                          compiler_params=_params())(c_all, w_shard, b_shard)


def _ada_bwd(c_all_t, dmod_cols):
    D = c_all_t.shape[0]
    n = dmod_cols.shape[1]

    def body(ct_ref, dm_ref, o_ref):
        sc = _silu(ct_ref[...])
        dm = dm_ref[...]
        acc = sc[:, 0:1] * dm[0:1, :]
        for b in range(1, N_DEV):
            acc = acc + sc[:, b:b + 1] * dm[b:b + 1, :]
        o_ref[...] = acc

    return pl.pallas_call(body, name="ada_bwd", out_shape=jax.ShapeDtypeStruct((D, n), F32),
                          compiler_params=_params())(c_all_t, dmod_cols)


def _row_specs(tm, widths):
    return [pl.BlockSpec((tm, w), lambda i: (i, 0)) for w in widths]


def _vec_spec(w):
    return pl.BlockSpec((1, w), lambda i: (0, 0))


def _col_spec(tm, w):
    return pl.BlockSpec((w, tm), lambda i: (0, i))


def _prenorm(x, g, scale, shift, name):
    S, D = x.shape
    tm = _tile(S, 512)

    def body(x_ref, g_ref, sc_ref, sh_ref, h_ref, ht_ref):
        xv = x_ref[...]
        r = lax.rsqrt(jnp.mean(xv * xv, axis=-1, keepdims=True) + EPS)
        h = (xv * r) * g_ref[...] * (1.0 + sc_ref[...]) + sh_ref[...]
        h_ref[...] = h.astype(BF16)
        ht_ref[...] = h.T.astype(BF16)

    return pl.pallas_call(
        body, name=name, grid=(S // tm,),
        out_shape=(jax.ShapeDtypeStruct((S, D), BF16), jax.ShapeDtypeStruct((D, S), BF16)),
        in_specs=_row_specs(tm, [D]) + [_vec_spec(D)] * 3,
        out_specs=(_row_specs(tm, [D])[0], _col_spec(tm, D)),
        compiler_params=_params(dimension_semantics=("parallel",)),
    )(x, g, scale, shift)


def _group_ones():
    r = lax.broadcasted_iota(jnp.int32, (LANES, LANES), 0) // HEAD_DIM
    c = lax.broadcasted_iota(jnp.int32, (LANES, LANES), 1) // HEAD_DIM
    return (r == c).astype(F32)


def _headnorm_fwd(o_f, o_s, g_f, g_s):
    S, dh = o_f.shape
    tm = _tile(S, 512)

    def body(of_ref, os_ref, gf_ref, gs_ref, mix_ref, mixt_ref):
        ones = _group_ones()
        for part, (o_ref, g_ref) in enumerate(((of_ref, gf_ref), (os_ref, gs_ref))):
            for t in range(dh // LANES):
                cols = slice(t * LANES, (t + 1) * LANES)
                out = slice(part * dh + t * LANES, part * dh + (t + 1) * LANES)
                o = o_ref[:, cols]
                ms = _dot(o * o, ones, precision=HIGHEST) * (1.0 / HEAD_DIM)
                mixn = o * lax.rsqrt(ms + EPS) * g_ref[:, cols]
                mix_ref[:, out] = mixn.astype(BF16)
                mixt_ref[out, :] = mixn.T.astype(BF16)

    return pl.pallas_call(
        body, name="headnorm_fwd", grid=(S // tm,),
        out_shape=(jax.ShapeDtypeStruct((S, 2 * dh), BF16), jax.ShapeDtypeStruct((2 * dh, S), BF16)),
        in_specs=_row_specs(tm, [dh, dh]) + [_vec_spec(dh)] * 2,
        out_specs=(_row_specs(tm, [2 * dh])[0], _col_spec(tm, 2 * dh)),
        compiler_params=_params(dimension_semantics=("parallel",)),
    )(o_f, o_s, g_f, g_s)


def _resid_prenorm(x, a_out, gate, g, scale, shift):
    S, D = x.shape
    tm = _tile(S, 512)

    def body(x_ref, a_ref, gt_ref, g_ref, sc_ref, sh_ref, x1_ref, h_ref, ht_ref):
        x1 = x_ref[...] + gt_ref[...] * a_ref[...]
        x1_ref[...] = x1
        r = lax.rsqrt(jnp.mean(x1 * x1, axis=-1, keepdims=True) + EPS)
        h = (x1 * r) * g_ref[...] * (1.0 + sc_ref[...]) + sh_ref[...]
        h_ref[...] = h.astype(BF16)
        ht_ref[...] = h.T.astype(BF16)

    return pl.pallas_call(
        body, name="resid_prenorm", grid=(S // tm,),
        out_shape=(jax.ShapeDtypeStruct((S, D), F32), jax.ShapeDtypeStruct((S, D), BF16),
                   jax.ShapeDtypeStruct((D, S), BF16)),
        in_specs=_row_specs(tm, [D, D]) + [_vec_spec(D)] * 4,
        out_specs=tuple(_row_specs(tm, [D, D]) + [_col_spec(tm, D)]),
        compiler_params=_params(dimension_semantics=("parallel",)),
    )(x, a_out, gate, g, scale, shift)


def _shift_down(main, halo, k):
    ext = jnp.concatenate([halo, main], axis=0)
    return pltpu.roll(ext, k, 0)[halo.shape[0]:]


def _shift_up(main, halo, k):
    ext = jnp.concatenate([main, halo], axis=0)
    n = ext.shape[0]
    return pltpu.roll(ext, n - k, 0)[:main.shape[0]]


def _conv(up, up_halo, w_ref, b_ref):
    return (w_ref[2:3, :] * up + w_ref[1:2, :] * _shift_down(up, up_halo, 1)
            + w_ref[0:1, :] * _shift_down(up, up_halo, 2) + b_ref[...])


def _prev_halo_map(tm):
    step = tm // HALO
    return lambda j, i: (jnp.maximum(i * step - 1, 0), j)


MLP_TM = 512
MLP_CT = 256
CARRY = 8


def _mlp_up(h, wg, wv, cwg, cwv, cbg, cbv):
    S, D = h.shape
    F = wg.shape[1]
    tm, ct = _tile(S, MLP_TM), _tile(F, MLP_CT)
    nct = F // ct

    def body(h_ref, wg_ref, wv_ref, cwg_ref, cwv_ref, cbg_ref, cbv_ref,
             upg_ref, upv_ref, act_ref, actt_ref, hg_scr, hv_scr):
        i, j = pl.program_id(0), pl.program_id(1)
        hv = h_ref[...]
        us = []
        for w_ref, cw_ref, cb_ref, up_ref, scr in ((wg_ref, cwg_ref, cbg_ref, upg_ref, hg_scr),
                                                   (wv_ref, cwv_ref, cbv_ref, upv_ref, hv_scr)):
            up = _dot(hv, w_ref[...]).astype(BF16)
            up_ref[...] = up
            upf = up.astype(F32)
            halo = jnp.where(i == 0, 0.0, scr[j])
            us.append(_conv(upf, halo, cw_ref, cb_ref))
            scr[j] = upf[tm - CARRY:, :]
        act = _silu(us[0]) * us[1]
        act_ref[...] = act.astype(BF16)
        actt_ref[...] = act.T.astype(BF16)

    blk = pl.BlockSpec((tm, ct), lambda i, j: (i, j))
    wspec = pl.BlockSpec((D, ct), lambda i, j: (0, j))
    cwspec = pl.BlockSpec((CONV_W, ct), lambda i, j: (0, j))
    cbspec = pl.BlockSpec((1, ct), lambda i, j: (0, j))
    sds = jax.ShapeDtypeStruct((S, F), BF16)
    return pl.pallas_call(
        body, name="mlp_up", grid=(S // tm, nct),
        out_shape=(sds, sds, sds, jax.ShapeDtypeStruct((F, S), BF16)),
        in_specs=[pl.BlockSpec((tm, D), lambda i, j: (i, 0)), wspec, wspec, cwspec, cwspec, cbspec, cbspec],
        out_specs=(blk, blk, blk, pl.BlockSpec((ct, tm), lambda i, j: (j, i))),
        scratch_shapes=[pltpu.VMEM((nct, CARRY, ct), F32), pltpu.VMEM((nct, CARRY, ct), F32)],
        compiler_params=_params(dimension_semantics=("arbitrary", "arbitrary")),
    )(h, wg, wv, cwg, cwv, cbg, cbv)


def _conv_act_bwd(dact, up_g, up_v, cwg, cwv, cbg, cbv):
    S, F = up_g.shape
    tm, ct = _tile(S, 256), _tile(F, CONV_COLS)
    nct = F // ct

    def body(da_ref, ug_ref, uv_ref, hg_ref, hv_ref, wg_ref, wv_ref, bg_ref, bv_ref,
             dug_ref, duv_ref, pg_ref, pv_ref):
        first = pl.program_id(1) == 0

        @pl.when(first)
        def _():
            pg_ref[...] = jnp.zeros_like(pg_ref)
            pv_ref[...] = jnp.zeros_like(pv_ref)

        da = da_ref[...].astype(F32)
        taps = []
        for u_ref, h_ref in ((ug_ref, hg_ref), (uv_ref, hv_ref)):
            h = jnp.where(first, 0.0, h_ref[...].astype(F32))
            uu = u_ref[...].astype(F32)
            taps.append((_shift_down(uu, h, 2), _shift_down(uu, h, 1), uu))
        u_g = wg_ref[0:1, :] * taps[0][0] + wg_ref[1:2, :] * taps[0][1] + wg_ref[2:3, :] * taps[0][2] + bg_ref[...]
        u_v = wv_ref[0:1, :] * taps[1][0] + wv_ref[1:2, :] * taps[1][1] + wv_ref[2:3, :] * taps[1][2] + bv_ref[...]
        sg = 1.0 / (1.0 + jnp.exp(-u_g))
        du_g = da * u_v * (sg * (1.0 + u_g * (1.0 - sg)))
        du_v = da * (u_g * sg)
        dug_ref[...] = du_g.astype(BF16)
        duv_ref[...] = du_v.astype(BF16)
        for du, tp, p_ref in ((du_g, taps[0], pg_ref), (du_v, taps[1], pv_ref)):
            for k in range(CONV_W):
                p_ref[k:k + 1, :] += jnp.sum(du * tp[k], axis=0, keepdims=True)
            p_ref[CONV_W:CONV_W + 1, :] += jnp.sum(du, axis=0, keepdims=True)

    main = pl.BlockSpec((tm, ct), lambda j, i: (i, j))
    halo = pl.BlockSpec((HALO, ct), _prev_halo_map(tm))
    wspec = pl.BlockSpec((CONV_W, ct), lambda j, i: (0, j))
    bspec = pl.BlockSpec((1, ct), lambda j, i: (0, j))
    pspec = pl.BlockSpec((8, ct), lambda j, i: (0, j))
    return pl.pallas_call(
        body, name="conv_act_bwd", grid=(nct, S // tm),
        out_shape=(jax.ShapeDtypeStruct((S, F), BF16), jax.ShapeDtypeStruct((S, F), BF16),
                   jax.ShapeDtypeStruct((8, F), F32), jax.ShapeDtypeStruct((8, F), F32)),
        in_specs=[main, main, main, halo, halo, wspec, wspec, bspec, bspec],
        out_specs=(main, main, pspec, pspec),
        compiler_params=_params(dimension_semantics=("parallel", "arbitrary")),
    )(dact, up_g, up_v, up_g, up_v, cwg, cwv, cbg, cbv)


def _conv_bwd_input(du, cw, name):
    S, C = du.shape
    tm, ct = _tile(S, 256), _tile(C, CONV_COLS)
    step = tm // HALO
    last_halo = S // HALO - 1

    def body(du_ref, h_ref, w_ref, o_ref):
        last = pl.program_id(1) == pl.num_programs(1) - 1
        d = du_ref[...].astype(F32)
        h = jnp.where(last, 0.0, h_ref[...].astype(F32))
        o_ref[...] = (w_ref[2:3, :] * d + w_ref[1:2, :] * _shift_up(d, h, 1)
                      + w_ref[0:1, :] * _shift_up(d, h, 2)).astype(BF16)

    return pl.pallas_call(
        body, name=name, out_shape=jax.ShapeDtypeStruct((S, C), BF16), grid=(C // ct, S // tm),
        in_specs=[pl.BlockSpec((tm, ct), lambda j, i: (i, j)),
                  pl.BlockSpec((HALO, ct), lambda j, i: (jnp.minimum((i + 1) * step, last_halo), j)),
                  pl.BlockSpec((CONV_W, ct), lambda j, i: (0, j))],
        out_specs=pl.BlockSpec((tm, ct), lambda j, i: (i, j)),
        compiler_params=_params(dimension_semantics=("parallel", "parallel")),
    )(du, du, cw)


def _scan_mats(R, nc, reverse):
    i = lax.broadcasted_iota(jnp.int32, (LANES, LANES), 0)
    j = lax.broadcasted_iota(jnp.int32, (LANES, LANES), 1)
    inner = ((i >= j) if reverse else (i <= j)).astype(F32)
    r = lax.broadcasted_iota(jnp.int32, (R, R), 0)
    c = lax.broadcasted_iota(jnp.int32, (R, R), 1)
    same = (r // nc) == (c // nc)
    outer = (same & ((c > r) if reverse else (c < r))).astype(F32)
    return inner, outer


def _chunk_scan(v, inner, outer, reverse):
    w = _dot(v, inner, precision=HIGHEST)
    col = 0 if reverse else LANES - 1
    carry = _dot(outer, w, precision=HIGHEST)[:, col:col + 1]
    return w + carry


def _fgate_fwd(z_rows, nc):
    R = z_rows.shape[0]

    def body(z_ref, f_ref):
        z = z_ref[...]
        logf = jnp.minimum(z, 0.0) - jnp.log(1.0 + jnp.exp(-jnp.abs(z)))
        inner, outer = _scan_mats(R, nc, False)
        f_ref[...] = _chunk_scan(logf, inner, outer, False)

    return pl.pallas_call(body, name="fgate_fwd", out_shape=jax.ShapeDtypeStruct((R, LANES), F32),
                          compiler_params=_params())(z_rows)


def _fgate_bwd(dfk_neg_rows, dfq_rows, z_rows, nc):
    R = z_rows.shape[0]
    nh = R // nc

    def body(dfk_ref, dfq_ref, z_ref, dz_ref, db_ref):
        inner, outer = _scan_mats(R, nc, True)
        dlogf = _chunk_scan(dfq_ref[...] - dfk_ref[...], inner, outer, True)
        dz = dlogf * (1.0 / (1.0 + jnp.exp(z_ref[...])))
        dz_ref[...] = dz
        hr = lax.broadcasted_iota(jnp.int32, (nh, R), 0)
        hc = lax.broadcasted_iota(jnp.int32, (nh, R), 1) // nc
        per_head = _dot((hr == hc).astype(F32), dz, precision=HIGHEST)
        db_ref[...] = jnp.sum(per_head, axis=1, keepdims=True)

    return pl.pallas_call(
        body, name="fgate_bwd",
        out_shape=(jax.ShapeDtypeStruct((R, LANES), F32), jax.ShapeDtypeStruct((nh, 1), F32)),
        compiler_params=_params())(dfk_neg_rows, dfq_rows, z_rows)


_NEG = -1e30
SKIP_BELOW = -106.0
_SCALE = HEAD_DIM ** -0.5
N_SCAN = ATT_BK // SCAN_BK
F_PARTS = 3
Q_F_LANE = HEAD_DIM
Q_ONE_LANE = HEAD_DIM + F_PARTS


def _kv_slice(j):
    return pl.ds(pl.multiple_of(j * ATT_BK, ATT_BK), ATT_BK)


def _mask_t(strict):
    s = lax.broadcasted_iota(jnp.int32, (ATT_BK, ATT_BQ), 0)
    t = lax.broadcasted_iota(jnp.int32, (ATT_BK, ATT_BQ), 1)
    return (s < t) if strict else (s <= t)


def _walk_down(i, step, alive, carry):
    carry = step(i, carry, True)

    def cond(st):
        n, go, _ = st
        return jnp.logical_and(n < i, go)

    def body(st):
        n, _, cr = st
        j = i - 1 - n
        cr = step(j, cr, False)
        return n + 1, alive(jnp.maximum(j - 1, 0), cr), cr

    return lax.while_loop(cond, body, (jnp.int32(0), alive(jnp.maximum(i - 1, 0), carry), carry))[2]


def _t_block(rows):
    return pl.BlockSpec((None, rows, ATT_BQ), lambda h, i, *_: (h, 0, i))


def _t_full(rows, S):
    return pl.BlockSpec((None, rows, S), lambda h, i, *_: (h, 0, 0))


def _n_block():
    return pl.BlockSpec((None, ATT_BQ, LANES), lambda h, i, *_: (h, i, 0))


def _n_full(S):
    return pl.BlockSpec((None, S, LANES), lambda h, i, *_: (h, 0, 0))


def _heads(t):
    S = t.shape[0]
    return jnp.transpose(t.reshape(S, -1, HEAD_DIM), (1, 0, 2))


def _unheads(t):
    return jnp.transpose(t, (1, 0, 2)).reshape(t.shape[1], -1)


def _pad_lanes(t):
    return jnp.pad(t, ((0, 0), (0, 0), (0, LANES - t.shape[2])))


def _tr(t):
    return jnp.transpose(t, (0, 2, 1))


def _split3(f):
    def bf16_round(t):
        return lax.reduce_precision(t, exponent_bits=8, mantissa_bits=7)

    hi = bf16_round(f)
    mid = bf16_round(f - hi)
    lo = bf16_round(f - hi - mid)
    return jnp.stack([hi, mid, lo], axis=-1).astype(BF16)


def _fox_operands(q, k, v, f_rows):
    qh, kh, vh = _heads(q) * _SCALE, _heads(k), _heads(v)
    f3 = _split3(f_rows)
    ones = jnp.ones_like(f3)
    q_n = _pad_lanes(jnp.concatenate([qh, f3, ones], axis=-1))
    k_n = _pad_lanes(jnp.concatenate([kh, ones, -f3], axis=-1))
    H, S = f_rows.shape
    f_end = f_rows.reshape(H, S // ATT_BK, ATT_BK)[:, :, -1]
    k_sq = jnp.sum(jnp.square(kh.astype(F32)), axis=-1).reshape(H, S // ATT_BK, ATT_BK)
    k_max = lax.cummax(jnp.sqrt(jnp.max(k_sq, axis=-1)), axis=1)
    return dict(q_n=q_n, q_t=_tr(q_n), k_n=k_n, k_t=_tr(k_n), v_n=_pad_lanes(vh), v_t=_tr(vh),
                f_end=f_end, k_max=k_max)


def _sb_operands(q, k, v):
    qh, kh, vh = _heads(q) * _SCALE, _heads(k), _heads(v)
    q_n, k_n = _pad_lanes(qh), _pad_lanes(kh)
    return dict(q_n=q_n, q_t=_tr(q_n), k_n=k_n, k_t=_tr(k_n), v_n=_pad_lanes(vh), v_t=_tr(vh))


def _do_operands(do):
    do_n = _pad_lanes(_heads(do).astype(BF16))
    return do_n, _tr(do_n)


def _fox_reach(qt, fend_ref, kmax_ref, h):
    qf = qt.astype(F32)
    q_norm = jnp.sqrt(jnp.sum(jnp.square(qf[:HEAD_DIM]), axis=0, keepdims=True))
    f_t = jnp.sum(qf[Q_F_LANE:Q_F_LANE + F_PARTS], axis=0, keepdims=True)
    return lambda j: q_norm * kmax_ref[h, j] + f_t - fend_ref[h, j]


def _fox_fwd(q_t, k_n, v_t, f_end, k_max):
    H, _, S = q_t.shape

    def body(fend_ref, kmax_ref, qt_ref, k_ref, vt_ref, ot_ref, lse_ref):
        h, i = pl.program_id(0), pl.program_id(1)
        qt = qt_ref[...]
        reach = _fox_reach(qt, fend_ref, kmax_ref, h)

        def step(j, carry, masked):
            m, l, acc = carry
            ks = _kv_slice(j)
            s = _dot(k_ref[ks, :], qt)
            if masked:
                s = jnp.where(_mask_t(False), s, _NEG)
            mn = jnp.maximum(m, jnp.max(s, axis=0, keepdims=True))
            alpha = jnp.exp(m - mn)
            p = jnp.exp(s - mn)
            l = alpha * l + jnp.sum(p, axis=0, keepdims=True)
            acc = acc * alpha + _dot(vt_ref[:, ks], p.astype(BF16))
            return mn, l, acc

        def alive(j, carry):
            return jnp.max(reach(j) - carry[0]) > SKIP_BELOW

        row = jnp.zeros((1, ATT_BQ), F32)
        m, l, acc = _walk_down(i, step, alive, (row + _NEG, row, jnp.zeros((HEAD_DIM, ATT_BQ), F32)))
        ot_ref[...] = acc / l
        lse_ref[...] = m + jnp.log(l)

    grid_spec = pltpu.PrefetchScalarGridSpec(
        num_scalar_prefetch=2, grid=(H, S // ATT_BQ),
        in_specs=[_t_block(LANES), _n_full(S), _t_full(HEAD_DIM, S)],
        out_specs=(_t_block(HEAD_DIM), _t_block(1)))
    return pl.pallas_call(
        body, name="fox_fwd", grid_spec=grid_spec,
        out_shape=(jax.ShapeDtypeStruct((H, HEAD_DIM, S), F32), jax.ShapeDtypeStruct((H, 1, S), F32)),
        compiler_params=_params(dimension_semantics=("parallel", "parallel")),
    )(f_end, k_max, q_t, k_n, v_t)


def _fox_bwd(q_t, q_n, k_n, k_t, v_n, do_t, do_n, o_t, lse, f_end, k_max):
    H, _, S = q_t.shape

    def body(fend_ref, kmax_ref, qt_ref, qn_ref, k_ref, kt_ref, v_ref, dot_ref, don_ref, ot_ref, lse_ref,
             dqt_ref, dk_ref, dv_ref):
        h, i = pl.program_id(0), pl.program_id(1)

        @pl.when(i == 0)
        def _():
            dk_ref[...] = jnp.zeros_like(dk_ref)
            dv_ref[...] = jnp.zeros_like(dv_ref)

        qt, qn, dot, don = qt_ref[...], qn_ref[...], dot_ref[...], don_ref[...]
        lse = lse_ref[...]
        delta = jnp.sum(dot[:HEAD_DIM].astype(F32) * ot_ref[...], axis=0, keepdims=True)
        reach = _fox_reach(qt, fend_ref, kmax_ref, h)

        def alive(j, dq):
            return jnp.max(reach(j) - lse) > SKIP_BELOW

        def step(j, dq, masked):
            ks = _kv_slice(j)
            s = _dot(k_ref[ks, :], qt)
            if masked:
                s = jnp.where(_mask_t(False), s, _NEG)
            p = jnp.exp(s - lse)
            ds = (p * (_dot(v_ref[ks, :], dot) - delta)).astype(BF16)
            dk_ref[ks, :] += _dot(ds, qn)
            dv_ref[ks, :] += _dot(p.astype(BF16), don)
            return dq + _dot(kt_ref[:, ks], ds)

        dqt_ref[...] = _walk_down(i, step, alive, jnp.zeros((LANES, ATT_BQ), F32))

    grid_spec = pltpu.PrefetchScalarGridSpec(
        num_scalar_prefetch=2, grid=(H, S // ATT_BQ),
        in_specs=[_t_block(LANES), _n_block(), _n_full(S), _t_full(LANES, S), _n_full(S),
                  _t_block(LANES), _n_block(), _t_block(HEAD_DIM), _t_block(1)],
        out_specs=(_t_block(LANES), _n_full(S), _n_full(S)))
    return pl.pallas_call(
        body, name="fox_bwd", grid_spec=grid_spec,
        out_shape=(jax.ShapeDtypeStruct((H, LANES, S), F32), jax.ShapeDtypeStruct((H, S, LANES), F32),
                   jax.ShapeDtypeStruct((H, S, LANES), F32)),
        compiler_params=_params(dimension_semantics=("parallel", "arbitrary")),
    )(f_end, k_max, q_t, q_n, k_n, k_t, v_n, do_t, do_n, o_t, lse)


def _scan_lhs():
    r = lax.broadcasted_iota(jnp.int32, (SCAN_BK, 2 * SCAN_BK), 0)
    c = lax.broadcasted_iota(jnp.int32, (SCAN_BK, 2 * SCAN_BK), 1) % SCAN_BK
    return (c >= r).astype(BF16)


def _suffix_sum(t, lhs):
    hi = t.astype(BF16)
    lo = (t - hi.astype(F32)).astype(BF16)
    return _dot(lhs, jnp.concatenate([hi, lo], axis=0))


def _sb_scores(k, qt, mask):
    z = _dot(k, qt)
    e = jnp.exp(-jnp.abs(z))
    lb = -(jnp.maximum(z, 0.0) + jnp.log(1.0 + e))
    if mask is not None:
        lb = jnp.where(mask, lb, 0.0)
    return z, e, lb


def _scan_blocks():
    return [slice(u * SCAN_BK, (u + 1) * SCAN_BK) for u in reversed(range(N_SCAN))]


def _sb_fwd(q_t, k_n, v_t):
    H, _, S = q_t.shape

    def body(qt_ref, k_ref, vt_ref, ot_ref):
        i = pl.program_id(1)
        qt = qt_ref[...]
        lhs = _scan_lhs()

        def step(j, carry, masked):
            c, acc = carry
            ks = _kv_slice(j)
            mask = _mask_t(True) if masked else None
            z, _, lb = _sb_scores(k_ref[ks, :], qt, mask)
            parts = []
            for sl in _scan_blocks():
                rin = _suffix_sum(lb[sl], lhs)
                a = jnp.exp(z[sl] + rin + c)
                if masked:
                    a = jnp.where(mask[sl], a, 0.0)
                parts.append(a.astype(BF16))
                c = c + rin[0:1, :]
            a_all = jnp.concatenate(parts[::-1], axis=0)
            return c, acc + _dot(vt_ref[:, ks], a_all)

        carry = (jnp.zeros((1, ATT_BQ), F32), jnp.zeros((HEAD_DIM, ATT_BQ), F32))
        ot_ref[...] = _walk_down(i, step, lambda j, cr: jnp.max(cr[0]) > SKIP_BELOW, carry)[1]

    return pl.pallas_call(
        body, name="sb_fwd", grid=(H, S // ATT_BQ),
        out_shape=jax.ShapeDtypeStruct((H, HEAD_DIM, S), F32),
        in_specs=[_t_block(LANES), _n_full(S), _t_full(HEAD_DIM, S)],
        out_specs=_t_block(HEAD_DIM),
        compiler_params=_params(dimension_semantics=("parallel", "parallel")),
    )(q_t, k_n, v_t)


def _sb_bwd(q_t, q_n, k_n, k_t, v_n, do_t, do_n, o_t):
    H, _, S = q_t.shape

    def body(qt_ref, qn_ref, k_ref, kt_ref, v_ref, dot_ref, don_ref, ot_ref, dqt_ref, dk_ref, dv_ref):
        i = pl.program_id(1)

        @pl.when(i == 0)
        def _():
            dk_ref[...] = jnp.zeros_like(dk_ref)
            dv_ref[...] = jnp.zeros_like(dv_ref)

        qt, qn, dot, don = qt_ref[...], qn_ref[...], dot_ref[...], don_ref[...]
        lhs = _scan_lhs()
        delta = jnp.sum(dot[:HEAD_DIM].astype(F32) * ot_ref[...], axis=0, keepdims=True)

        def step(j, carry, masked):
            c, g, dq = carry
            ks = _kv_slice(j)
            mask = _mask_t(True) if masked else None
            z, e, lb = _sb_scores(k_ref[ks, :], qt, mask)
            da = _dot(v_ref[ks, :], dot)
            a_parts, dz_parts = [], []
            for sl in _scan_blocks():
                rin = _suffix_sum(lb[sl], lhs)
                a = jnp.exp(z[sl] + rin + c)
                if masked:
                    a = jnp.where(mask[sl], a, 0.0)
                ab = a.astype(BF16)
                gg = ab.astype(F32) * da[sl]
                rgin = _suffix_sum(gg, lhs)
                rinv = 1.0 / (1.0 + e[sl])
                sig = jnp.where(z[sl] >= 0.0, rinv, e[sl] * rinv)
                dz = gg - sig * (delta - g - (rgin - gg))
                if masked:
                    dz = jnp.where(mask[sl], dz, 0.0)
                a_parts.append(ab)
                dz_parts.append(dz.astype(BF16))
                c = c + rin[0:1, :]
                g = g + rgin[0:1, :]
            ab_all = jnp.concatenate(a_parts[::-1], axis=0)
            dzb = jnp.concatenate(dz_parts[::-1], axis=0)
            dk_ref[ks, :] += _dot(dzb, qn)
            dv_ref[ks, :] += _dot(ab_all, don)
            return c, g, dq + _dot(kt_ref[:, ks], dzb)

        row = jnp.zeros((1, ATT_BQ), F32)
        carry = (row, row, jnp.zeros((LANES, ATT_BQ), F32))
        dqt_ref[...] = _walk_down(i, step, lambda j, cr: jnp.max(cr[0]) > SKIP_BELOW, carry)[2]

    return pl.pallas_call(
        body, name="sb_bwd", grid=(H, S // ATT_BQ),
        out_shape=(jax.ShapeDtypeStruct((H, LANES, S), F32), jax.ShapeDtypeStruct((H, S, LANES), F32),
                   jax.ShapeDtypeStruct((H, S, LANES), F32)),
        in_specs=[_t_block(LANES), _n_block(), _n_full(S), _t_full(LANES, S), _n_full(S),
                  _t_block(LANES), _n_block(), _t_block(HEAD_DIM)],
        out_specs=(_t_block(LANES), _n_full(S), _n_full(S)),
        compiler_params=_params(dimension_semantics=("parallel", "arbitrary")),
    )(q_t, q_n, k_n, k_t, v_n, do_t, do_n, o_t)


def _acc_spec(w):
    return pl.BlockSpec((1, w), lambda i: (0, 0))


def _loss_head(x1, m_out, gate_m, g_final, target):
    S, D = x1.shape
    tm = _tile(S, 256)

    def body(x1_ref, mo_ref, gt_ref, gf_ref, tg_ref, dx2_ref, gm_ref, loss_ref, dgf_ref, dgt_ref):
        @pl.when(pl.program_id(0) == 0)
        def _():
            loss_ref[...] = jnp.zeros_like(loss_ref)
            dgf_ref[...] = jnp.zeros_like(dgf_ref)
            dgt_ref[...] = jnp.zeros_like(dgt_ref)

        mo = mo_ref[...]
        x2 = x1_ref[...] + gt_ref[...] * mo
        r = lax.rsqrt(jnp.mean(x2 * x2, axis=-1, keepdims=True) + EPS)
        xh = x2 * r
        diff = xh * gf_ref[...] - tg_ref[...]
        loss_ref[...] += (0.5 / D) * jnp.sum(diff * diff)
        dy = diff * (1.0 / D)
        dgf_ref[...] += jnp.sum(dy * xh, axis=0, keepdims=True)
        dxh = dy * gf_ref[...]
        dx2 = r * (dxh - xh * jnp.mean(dxh * xh, axis=-1, keepdims=True))
        dx2_ref[...] = dx2
        gm_ref[...] = (dx2 * gt_ref[...]).astype(BF16)
        dgt_ref[...] += jnp.sum(dx2 * mo, axis=0, keepdims=True)

    return pl.pallas_call(
        body, name="loss_head", grid=(S // tm,),
        out_shape=(jax.ShapeDtypeStruct((S, D), F32), jax.ShapeDtypeStruct((S, D), BF16),
                   jax.ShapeDtypeStruct((1, LANES), F32), jax.ShapeDtypeStruct((1, D), F32),
                   jax.ShapeDtypeStruct((1, D), F32)),
        in_specs=_row_specs(tm, [D, D]) + [_vec_spec(D)] * 2 + _row_specs(tm, [D]),
        out_specs=tuple(_row_specs(tm, [D, D]) + [_acc_spec(LANES), _acc_spec(D), _acc_spec(D)]),
        compiler_params=_params(dimension_semantics=("arbitrary",)),
    )(x1, m_out, gate_m, g_final, target)


def _norm_bwd(dh, xin, dres, g, scale, name, gate=None, branch=None):
    S, D = xin.shape
    tm = _tile(S, 256)
    gated = gate is not None

    def body(*refs):
        if gated:
            (dh_ref, x_ref, dr_ref, g_ref, sc_ref, gt_ref, br_ref,
             dx_ref, dsc_ref, dsh_ref, dg_ref, ga_ref, dgt_ref) = refs
            sums = (dsc_ref, dsh_ref, dg_ref, dgt_ref)
        else:
            dh_ref, x_ref, dr_ref, g_ref, sc_ref, dx_ref, dsc_ref, dsh_ref, dg_ref = refs
            sums = (dsc_ref, dsh_ref, dg_ref)

        @pl.when(pl.program_id(0) == 0)
        def _():
            for s_ref in sums:
                s_ref[...] = jnp.zeros_like(s_ref)

        dhv, xv = dh_ref[...], x_ref[...]
        r = lax.rsqrt(jnp.mean(xv * xv, axis=-1, keepdims=True) + EPS)
        xh = xv * r
        dsc_ref[...] += jnp.sum(dhv * (xh * g_ref[...]), axis=0, keepdims=True)
        dsh_ref[...] += jnp.sum(dhv, axis=0, keepdims=True)
        dn = dhv * (1.0 + sc_ref[...])
        dg_ref[...] += jnp.sum(dn * xh, axis=0, keepdims=True)
        dxh = dn * g_ref[...]
        dx = dr_ref[...] + r * (dxh - xh * jnp.mean(dxh * xh, axis=-1, keepdims=True))
        dx_ref[...] = dx
        if gated:
            ga_ref[...] = (dx * gt_ref[...]).astype(BF16)
            dgt_ref[...] += jnp.sum(dx * br_ref[...], axis=0, keepdims=True)

    vec = jax.ShapeDtypeStruct((1, D), F32)
    out_shape = [jax.ShapeDtypeStruct((S, D), F32), vec, vec, vec]
    out_specs = _row_specs(tm, [D]) + [_acc_spec(D)] * 3
    in_specs = _row_specs(tm, [D, D, D]) + [_vec_spec(D)] * 2
    args = [dh, xin, dres, g, scale]
    if gated:
        out_shape += [jax.ShapeDtypeStruct((S, D), BF16), vec]
        out_specs += _row_specs(tm, [D]) + [_acc_spec(D)]
        in_specs += [_vec_spec(D)] + _row_specs(tm, [D])
        args += [gate, branch]
    return pl.pallas_call(
        body, name=name, grid=(S // tm,), out_shape=tuple(out_shape),
        in_specs=in_specs, out_specs=tuple(out_specs),
        compiler_params=_params(dimension_semantics=("arbitrary",)),
    )(*args)


def _headnorm_bwd(dmix, o_f, o_s, g_f, g_s):
    S, dh = o_f.shape
    tm = _tile(S, 256)

    def body(dm_ref, of_ref, os_ref, gf_ref, gs_ref, dof_ref, dos_ref, dgf_ref, dgs_ref):
        @pl.when(pl.program_id(0) == 0)
        def _():
            dgf_ref[...] = jnp.zeros_like(dgf_ref)
            dgs_ref[...] = jnp.zeros_like(dgs_ref)

        ones = _group_ones()
        parts = ((of_ref, gf_ref, dof_ref, dgf_ref), (os_ref, gs_ref, dos_ref, dgs_ref))
        for part, (o_ref, g_ref, do_ref, dg_ref) in enumerate(parts):
            for t in range(dh // LANES):
                cols = slice(t * LANES, (t + 1) * LANES)
                o = o_ref[:, cols]
                dm = dm_ref[:, part * dh + t * LANES: part * dh + (t + 1) * LANES]
                r = lax.rsqrt(_dot(o * o, ones, precision=HIGHEST) * (1.0 / HEAD_DIM) + EPS)
                oh = o * r
                dg_ref[:, cols] += jnp.sum(dm * oh, axis=0, keepdims=True)
                dn = dm * g_ref[:, cols]
                mean = _dot(dn * oh, ones, precision=HIGHEST) * (1.0 / HEAD_DIM)
                do_ref[:, cols] = r * (dn - oh * mean)

    vec = jax.ShapeDtypeStruct((1, dh), F32)
    return pl.pallas_call(
        body, name="headnorm_bwd", grid=(S // tm,),
        out_shape=(jax.ShapeDtypeStruct((S, dh), F32), jax.ShapeDtypeStruct((S, dh), F32), vec, vec),
        in_specs=_row_specs(tm, [2 * dh, dh, dh]) + [_vec_spec(dh)] * 2,
        out_specs=tuple(_row_specs(tm, [dh, dh]) + [_acc_spec(dh)] * 2),
        compiler_params=_params(dimension_semantics=("arbitrary",)),
    )(dmix, o_f, o_s, g_f, g_s)


def _adamw(w, gslots, m, v, name):
    R, C = w.shape
    n = gslots.shape[0]
    tr = 256 if (R % 256 == 0 and R > 256) else R
    bc1 = 1.0 - ADAM_B1 ** ADAM_STEP
    bc2 = 1.0 - ADAM_B2 ** ADAM_STEP

    def body(w_ref, gs_ref, m_ref, v_ref, g_ref, d_ref, nm_ref, nv_ref):
        g = gs_ref[0]
        for s in range(1, n):
            g = g + gs_ref[s]
        nm = ADAM_B1 * m_ref[...] + (1.0 - ADAM_B1) * g
        nv = ADAM_B2 * v_ref[...] + (1.0 - ADAM_B2) * (g * g)
        g_ref[...] = g
        nm_ref[...] = nm
        nv_ref[...] = nv
        d_ref[...] = -ADAM_LR * ((nm / bc1) / (jnp.sqrt(nv / bc2) + ADAM_EPS) + ADAM_WD * w_ref[...])

    blk = pl.BlockSpec((tr, C), lambda i: (i, 0))
    sds = jax.ShapeDtypeStruct((R, C), F32)
    return pl.pallas_call(
        body, name=name, grid=(R // tr,), out_shape=(sds,) * 4,
        in_specs=[blk, pl.BlockSpec((n, tr, C), lambda i: (0, i, 0)), blk, blk], out_specs=(blk,) * 4,
        compiler_params=_params(dimension_semantics=("parallel",)),
    )(w, gslots, m, v)


def _slot_sum(slots, name):
    n, _, C = slots.shape

    def body(s_ref, o_ref):
        acc = s_ref[0]
        for s in range(1, n):
            acc = acc + s_ref[s]
        o_ref[...] = acc

    return pl.pallas_call(body, name=name, out_shape=jax.ShapeDtypeStruct((1, C), F32),
                          compiler_params=_params())(slots)


def _pad_cols(a, n):
    return jnp.pad(a, ((0, 0), (0, n - a.shape[1])))


def _ungather(g, axis):
    if axis == 0:
        return g.reshape(g.shape[0] * g.shape[1], g.shape[2])
    return jnp.transpose(g, (1, 0, 2)).reshape(g.shape[1], g.shape[0] * g.shape[2])


def _to_slots(full, axis):
    R, C = full.shape
    if axis == 0:
        return full.reshape(N_DEV, R // N_DEV, C)
    return jnp.transpose(full.reshape(R, N_DEV, C // N_DEV), (1, 0, 2))


def kernel(x, c, w_ada, b_ada, g_attn, w_in, b_fgate, g_out_fox, g_out_sb, w_out, g_mlp, w_up, conv_w, conv_b, w_down, g_final, loss_target, m_w_ada, m_b_ada, m_g_attn, m_w_in, m_b_fgate, m_g_out_fox, m_g_out_sb, m_w_out, m_g_mlp, m_w_up, m_conv_w, m_conv_b, m_w_down, m_g_final, v_w_ada, v_b_ada, v_g_attn, v_w_in, v_b_fgate, v_g_out_fox, v_g_out_sb, v_w_out, v_g_mlp, v_w_up, v_conv_w, v_conv_b, v_w_down, v_g_final):
    S, D = x.shape[1], x.shape[2]
    dh = D // 2
    n_heads = dh // HEAD_DIM
    n_qkv = 6 * dh
    ff = w_down.shape[1] * N_DEV
    ffp = -(-ff // (2 * LANES)) * (2 * LANES)
    nc = S // LANES
    me = 4 * lax.axis_index("x") + 2 * lax.axis_index("y") + lax.axis_index("c")
    xs, tgt = x[0], loss_target[0]

    c_all, win_g, wout_g, wup_g, wdown_g, convw_g = _gather_two_level(
        [c, w_in[0].astype(BF16), w_out[0].astype(BF16), w_up[0].astype(BF16), w_down[0].astype(BF16), conv_w[0]],
        name="gather_weights")
    c_all = c_all.reshape(N_DEV, D)
    W_in = _ungather(win_g, 1)
    W_qkv, W_f = W_in[:, :n_qkv], _pad_cols(W_in[:, n_qkv:], LANES)
    W_inp = jnp.concatenate([W_qkv, W_f], axis=1)
    W_out = _ungather(wout_g, 0)
    W_up = _ungather(wup_g, 1)
    W_g, W_v = _pad_cols(W_up[:, :ff], ffp), _pad_cols(W_up[:, ff:], ffp)
    W_down = jnp.pad(_ungather(wdown_g, 0), ((0, ffp - ff), (0, 0)))
    cw_full = _ungather(convw_g, 1)
    cw_g, cw_v = _pad_cols(cw_full[:, :ff], ffp), _pad_cols(cw_full[:, ff:], ffp)
    cb_g, cb_v = _pad_cols(conv_b[:, :ff], ffp), _pad_cols(conv_b[:, ff:], ffp)

    n_ada = w_ada.shape[2]
    b_shard = lax.dynamic_slice(b_ada, (0, me * n_ada), (1, n_ada))
    mod_cols = _ada_fwd(c_all, w_ada[0], b_shard)
    (mod_g,) = _exchange([mod_cols], scatter=False, name="gather_mod")
    mod = lax.dynamic_index_in_dim(mod_g, me, axis=1, keepdims=False).reshape(6, 1, D)
    shift_a, scale_a, gate_a, shift_m, scale_m, gate_m = [mod[k] for k in range(6)]

    h1, h1_t = _prenorm(xs, g_attn, scale_a, shift_a, "prenorm_attn")
    qkv = _mm(h1, W_qkv, BF16, "proj_qkv")
    flog = _mm(h1, W_f, F32, "proj_fgate")
    zf = flog[:, :n_heads] + b_fgate
    z_rows = zf.T.reshape(n_heads * nc, LANES)
    f_rows = _fgate_fwd(z_rows, nc).reshape(n_heads, S)
    fox = _fox_operands(qkv[:, 0:dh], qkv[:, dh:2 * dh], qkv[:, 2 * dh:3 * dh], f_rows)
    sb = _sb_operands(qkv[:, 3 * dh:4 * dh], qkv[:, 4 * dh:5 * dh], qkv[:, 5 * dh:6 * dh])
    of_t, lse = _fox_fwd(fox["q_t"], fox["k_n"], fox["v_t"], fox["f_end"], fox["k_max"])
    os_t = _sb_fwd(sb["q_t"], sb["k_n"], sb["v_t"])
    o_f, o_s = _unheads(_tr(of_t)), _unheads(_tr(os_t))
    mix, mix_t = _headnorm_fwd(o_f, o_s, g_out_fox, g_out_sb)
    a_out = _mm(mix, W_out, F32, "proj_out")
    x1, h2, h2_t = _resid_prenorm(xs, a_out, gate_a, g_mlp, scale_m, shift_m)
    up_g, up_v, act, act_t = _mlp_up(h2, W_g, W_v, cw_g, cw_v, cb_g, cb_v)
    m_out = _mm(act, W_down, F32, "proj_down")

    dx2, gm, loss_p, dg_final, dgate_m = _loss_head(x1, m_out, gate_m, g_final.reshape(1, D), tgt)
    dact = _mm(gm, W_down.T, BF16, "bwd_down_act")
    dW_down = _mm_acc(act_t, gm, "bwd_down_w")
    du_g, du_v, p_g, p_v = _conv_act_bwd(dact, up_g, up_v, cw_g, cw_v, cb_g, cb_v)
    dup_g = _conv_bwd_input(du_g, cw_g, "conv_bwd_input_g")
    dup_v = _conv_bwd_input(du_v, cw_v, "conv_bwd_input_v")
    dh2 = _mm2(dup_g, W_g.T, dup_v, W_v.T, F32, "bwd_up_act")
    dW_g = _mm_acc(h2_t, dup_g, "bwd_up_w_g")
    dW_v = _mm_acc(h2_t, dup_v, "bwd_up_w_v")
    dx1, dscale_m, dshift_m, dg_mlp, ga, dgate_a = _norm_bwd(
        dh2, x1, dx2, g_mlp, scale_m, "norm_mlp_bwd", gate=gate_a, branch=a_out)
    dmix = _mm(ga, W_out.T, F32, "bwd_out_act")
    dW_out = _mm_acc(mix_t, ga, "bwd_out_w")
    do_f, do_s, dg_fox, dg_sb = _headnorm_bwd(dmix, o_f, o_s, g_out_fox, g_out_sb)
    dof_n, dof_t = _do_operands(do_f)
    dos_n, dos_t = _do_operands(do_s)
    dqf_t, dkf, dvf = _fox_bwd(fox["q_t"], fox["q_n"], fox["k_n"], fox["k_t"], fox["v_n"], dof_t, dof_n, of_t, lse,
                              fox["f_end"], fox["k_max"])
    dqs_t, dks, dvs = _sb_bwd(sb["q_t"], sb["q_n"], sb["k_n"], sb["k_t"], sb["v_n"], dos_t, dos_n, os_t)
    dz_rows, db_fgate = _fgate_bwd(dkf[:, :, Q_ONE_LANE].reshape(n_heads * nc, LANES),
                                   dqf_t[:, Q_F_LANE, :].reshape(n_heads * nc, LANES), z_rows, nc)
    dzf = dz_rows.reshape(n_heads, S).T

    def dq_cols(dq_t):
        return (_unheads(_tr(dq_t[:, :HEAD_DIM, :])) * _SCALE).astype(BF16)

    def dkv_cols(d):
        return _unheads(d[:, :, :HEAD_DIM]).astype(BF16)

    dproj = jnp.concatenate(
        [dq_cols(dqf_t), dkv_cols(dkf), dkv_cols(dvf), dq_cols(dqs_t), dkv_cols(dks), dkv_cols(dvs),
         _pad_cols(dzf, LANES).astype(BF16)], axis=1)
    dh1 = _mm(dproj, W_inp.T, F32, "bwd_in_act")
    dW_inp = _mm_acc(h1_t, dproj, "bwd_in_w")
    grad_x, dscale_a, dshift_a, dg_attn = _norm_bwd(dh1, xs, dx1, g_attn, scale_a, "norm_attn_bwd")

    dconv_b = jnp.concatenate([p_g[CONV_W:CONV_W + 1, :ff], p_v[CONV_W:CONV_W + 1, :ff]], axis=1)
    parts = [dshift_a, dscale_a, dgate_a, dshift_m, dscale_m, dgate_m,
             dg_attn, db_fgate.reshape(1, n_heads), dg_fox, dg_sb, dg_mlp, dconv_b, dg_final,
             loss_p[:, :1]]
    sizes = [p.shape[1] for p in parts]
    vec = jnp.concatenate(parts, axis=1)
    n_vec = -(-vec.shape[1] // LANES) * LANES
    vec = _pad_cols(vec, n_vec)
    (vec_g,) = _exchange([vec], scatter=False, name="gather_small")
    offs = [0]
    for s in sizes:
        offs.append(offs[-1] + s)

    def small(k0, k1=None):
        k1 = k0 if k1 is None else k1
        return vec_g[:, :, offs[k0]:offs[k1 + 1]]

    dmod_all = small(0, 5).reshape(N_DEV, 6 * D)
    dmod_cols = lax.dynamic_slice(dmod_all, (0, me * n_ada), (N_DEV, n_ada))
    dW_ada = _ada_bwd(c_all.T, dmod_cols)

    dW_in = jnp.concatenate([dW_inp[:, :n_qkv], dW_inp[:, n_qkv:n_qkv + n_heads]], axis=1)
    dW_upf = jnp.concatenate([dW_g[:, :ff], dW_v[:, :ff]], axis=1)
    dcw = jnp.concatenate([p_g[:CONV_W, :ff], p_v[:CONV_W, :ff]], axis=1)
    bound = [_to_slots(dW_in, 1), _to_slots(dW_out, 0), _to_slots(dW_upf, 1), _to_slots(dW_down[:ff], 0),
             _to_slots(dcw, 1)]
    bound = [b.reshape((N_CHIPS, 2) + b.shape[1:]) for b in bound]
    got = _scatter_to_sibling(bound, "scatter_sibling")
    c_idx = lax.axis_index("c").astype(jnp.int32).reshape(1)
    chip_sums = [_pair_add(b, g, c_idx, "pair_add_%d" % k) for k, (b, g) in enumerate(zip(bound, got))]
    s_in, s_out, s_up, s_down, s_cw = _scatter_to_chips(chip_sums, "scatter_chips")

    res = {}
    res["w_ada"] = _adamw(w_ada[0], dW_ada[None], m_w_ada[0], v_w_ada[0], "adamw_w_ada")
    res["w_in"] = _adamw(w_in[0], s_in, m_w_in[0], v_w_in[0], "adamw_w_in")
    res["w_out"] = _adamw(w_out[0], s_out, m_w_out[0], v_w_out[0], "adamw_w_out")
    res["w_up"] = _adamw(w_up[0], s_up, m_w_up[0], v_w_up[0], "adamw_w_up")
    res["w_down"] = _adamw(w_down[0], s_down, m_w_down[0], v_w_down[0], "adamw_w_down")
    res["conv_w"] = _adamw(conv_w[0], s_cw, m_conv_w[0], v_conv_w[0], "adamw_conv_w")
    res["b_ada"] = _adamw(b_ada, small(0, 5), m_b_ada, v_b_ada, "adamw_b_ada")
    res["g_attn"] = _adamw(g_attn, small(6), m_g_attn, v_g_attn, "adamw_g_attn")
    res["b_fgate"] = _adamw(b_fgate, small(7), m_b_fgate, v_b_fgate, "adamw_b_fgate")
    res["g_out_fox"] = _adamw(g_out_fox, small(8), m_g_out_fox, v_g_out_fox, "adamw_g_out_fox")
    res["g_out_sb"] = _adamw(g_out_sb, small(9), m_g_out_sb, v_g_out_sb, "adamw_g_out_sb")
    res["g_mlp"] = _adamw(g_mlp, small(10), m_g_mlp, v_g_mlp, "adamw_g_mlp")
    res["conv_b"] = _adamw(conv_b, small(11), m_conv_b, v_conv_b, "adamw_conv_b")
    res["g_final"] = _adamw(g_final.reshape(1, D), small(12), m_g_final.reshape(1, D),
                            v_g_final.reshape(1, D), "adamw_g_final")
    loss = _slot_sum(_pad_cols(small(13).reshape(N_DEV, 1), LANES).reshape(N_DEV, 1, LANES), "loss_sum")[0, 0]

    names = ["w_ada", "b_ada", "g_attn", "w_in", "b_fgate", "g_out_fox", "g_out_sb", "w_out", "g_mlp",
             "w_up", "conv_w", "conv_b", "w_down", "g_final"]

    def shaped(n, a):
        if n == "g_final":
            return a.reshape(D)
        if n in ("b_ada", "g_attn", "b_fgate", "g_out_fox", "g_out_sb", "g_mlp", "conv_b"):
            return a
        return a[None]

    outs = [loss, grad_x[None]]
    for k in range(4):
        outs += [shaped(n, res[n][k]) for n in names]
    return tuple(outs)
```

```python
import jax
import jax.numpy as jnp
from jax import lax
from jax.experimental import pallas as pl
from jax.experimental.pallas import tpu as pltpu

F32 = jnp.float32
BF16 = jnp.bfloat16
HIGHEST = lax.Precision.HIGHEST

N_DEV = 8
LANES = 128
HEAD_DIM = 64
EPS = 1e-6
CONV_W = 3
CONV_COLS = 1408
HALO = 16
ATT_BQ = 512
ATT_BK = 512
SCAN_BK = 128
VMEM_LIMIT = 56 * 1024 * 1024

ADAM_LR = 0.001
ADAM_B1 = 0.9
ADAM_B2 = 0.999
ADAM_EPS = 1e-08
ADAM_WD = 0.01
ADAM_STEP = 10


def _params(**kw):
    return pltpu.CompilerParams(vmem_limit_bytes=VMEM_LIMIT, **kw)


def _tile(n, cap):
    if n <= cap:
        return n
    best = None
    for t in range(LANES, cap + 1, LANES):
        if n % t == 0:
            best = t
    assert best is not None, (n, cap)
    return best


def _dot(a, b, **kw):
    return jnp.dot(a, b, preferred_element_type=F32, **kw)


def _exchange_copies(ins, outs, send_sems, recv_sems, loc_sems, scatter):
    n = len(ins)
    x, y, c = lax.axis_index("x"), lax.axis_index("y"), lax.axis_index("c")
    me = 4 * x + 2 * y + c
    copies = []
    for a in range(n):
        src = ins[a].at[me] if scatter else ins[a]
        copies.append(pltpu.make_async_copy(src, outs[a].at[me], loc_sems.at[a]))
    for k in range(1, N_DEV):
        px = 1 - x if k & 4 else x
        py = 1 - y if k & 2 else y
        pc = 1 - c if k & 1 else c
        peer = 4 * px + 2 * py + pc
        for a in range(n):
            src = ins[a].at[peer] if scatter else ins[a]
            copies.append(pltpu.make_async_remote_copy(
                src_ref=src, dst_ref=outs[a].at[me],
                send_sem=send_sems.at[a, k - 1], recv_sem=recv_sems.at[a, k - 1],
                device_id=(px, py, pc), device_id_type=pl.DeviceIdType.MESH))
    return copies


def _exchange_out_shapes(arrays, scatter):
    return [jax.ShapeDtypeStruct((N_DEV,) + tuple(a.shape[1:] if scatter else a.shape), a.dtype) for a in arrays]


def _exchange_sems(n):
    return [pltpu.SemaphoreType.DMA((n, N_DEV - 1)), pltpu.SemaphoreType.DMA((n, N_DEV - 1)),
            pltpu.SemaphoreType.DMA((n,))]


def _exchange(arrays, scatter, name):
    n = len(arrays)

    def body(*refs):
        copies = _exchange_copies(refs[:n], refs[n:2 * n], *refs[2 * n:], scatter)
        for cp in copies:
            cp.start()
        for cp in copies:
            cp.wait()

    any_spec = pl.BlockSpec(memory_space=pl.ANY)
    return pl.pallas_call(
        body, name=name, out_shape=tuple(_exchange_out_shapes(arrays, scatter)),
        in_specs=[any_spec] * n, out_specs=tuple([any_spec] * n),
        scratch_shapes=_exchange_sems(n),
        compiler_params=pltpu.CompilerParams(has_side_effects=True),
    )(*arrays)


def _gather_two_level(arrays, name):
    n = len(arrays)
    out_shape = [jax.ShapeDtypeStruct((N_DEV,) + tuple(a.shape), a.dtype) for a in arrays]

    def body(*refs):
        ins, outs = refs[:n], refs[n:2 * n]
        send_sems, recv_sems, loc_sems = refs[2 * n:]
        x, y, c = lax.axis_index("x"), lax.axis_index("y"), lax.axis_index("c")
        me, sibling = (x, y, c), (x, y, 1 - c)
        chips = [(1 - x, y), (x, 1 - y), (1 - x, 1 - y)]

        def slot(px, py, pc):
            return 4 * px + 2 * py + pc

        def copy(a, k, block, to, src=None):
            dst = outs[a].at[slot(*block)]
            return pltpu.make_async_remote_copy(
                src_ref=dst if src is None else src, dst_ref=dst,
                send_sem=send_sems.at[a, k], recv_sem=recv_sems.at[a, k],
                device_id=to, device_id_type=pl.DeviceIdType.MESH)

        local = [pltpu.make_async_copy(ins[a], outs[a].at[slot(*me)], loc_sems.at[a]) for a in range(n)]
        for cp in local:
            cp.start()
        first = []
        for a in range(n):
            first.append(copy(a, 0, me, sibling, src=ins[a]))
            first += [copy(a, 1 + j, me, (*chip, c), src=ins[a]) for j, chip in enumerate(chips)]
        for cp in first:
            cp.start()
        passed = []
        for j, chip in enumerate(chips):
            for a in range(n):
                copy(a, 1 + j, (*chip, c), me).wait_recv()
                cp = copy(a, 4 + j, (*chip, c), sibling)
                cp.start()
                passed.append(cp)
        for a in range(n):
            copy(a, 0, sibling, me).wait_recv()
            for j, chip in enumerate(chips):
                copy(a, 4 + j, (*chip, 1 - c), me).wait_recv()
        for cp in first + passed:
            cp.wait_send()
        for cp in local:
            cp.wait()

    any_spec = pl.BlockSpec(memory_space=pl.ANY)
    return pl.pallas_call(
        body, name=name, out_shape=tuple(out_shape),
        in_specs=[any_spec] * n, out_specs=tuple([any_spec] * n),
        scratch_shapes=[pltpu.SemaphoreType.DMA((n, N_DEV - 1)), pltpu.SemaphoreType.DMA((n, N_DEV - 1)),
                        pltpu.SemaphoreType.DMA((n,))],
        compiler_params=pltpu.CompilerParams(has_side_effects=True),
    )(*arrays)


N_CHIPS = 4


def _scatter_to_sibling(arrays, name):
    n = len(arrays)
    out_shape = [jax.ShapeDtypeStruct((N_CHIPS,) + tuple(a.shape[2:]), a.dtype) for a in arrays]

    def body(*refs):
        ins, outs = refs[:n], refs[n:2 * n]
        send_sems, recv_sems = refs[2 * n:]
        x, y, c = lax.axis_index("x"), lax.axis_index("y"), lax.axis_index("c")
        copies = []
        for a in range(n):
            for q in range(N_CHIPS):
                cp = pltpu.make_async_remote_copy(
                    src_ref=ins[a].at[q, 1 - c], dst_ref=outs[a].at[q],
                    send_sem=send_sems.at[a, q], recv_sem=recv_sems.at[a, q],
                    device_id=(x, y, 1 - c), device_id_type=pl.DeviceIdType.MESH)
                cp.start()
                copies.append(cp)
        for cp in copies:
            cp.wait()

    any_spec = pl.BlockSpec(memory_space=pl.ANY)
    return pl.pallas_call(
        body, name=name, out_shape=tuple(out_shape),
        in_specs=[any_spec] * n, out_specs=tuple([any_spec] * n),
        scratch_shapes=[pltpu.SemaphoreType.DMA((n, N_CHIPS)), pltpu.SemaphoreType.DMA((n, N_CHIPS))],
        compiler_params=pltpu.CompilerParams(has_side_effects=True),
    )(*arrays)


def _pair_add(mine, got, c_idx, name):
    _, _, R, C = mine.shape
    tr = 256 if (R % 256 == 0 and R > 256) else R

    def body(c_ref, m_ref, g_ref, o_ref):
        o_ref[...] = m_ref[...] + g_ref[...]

    grid_spec = pltpu.PrefetchScalarGridSpec(
        num_scalar_prefetch=1, grid=(N_CHIPS, R // tr),
        in_specs=[pl.BlockSpec((None, None, tr, C), lambda q, i, c_ref: (q, c_ref[0], i, 0)),
                  pl.BlockSpec((None, tr, C), lambda q, i, c_ref: (q, i, 0))],
        out_specs=pl.BlockSpec((None, tr, C), lambda q, i, c_ref: (q, i, 0)))
    return pl.pallas_call(
        body, name=name, grid_spec=grid_spec, out_shape=jax.ShapeDtypeStruct((N_CHIPS, R, C), mine.dtype),
        compiler_params=_params(dimension_semantics=("parallel", "parallel")),
    )(c_idx, mine, got)


def _scatter_to_chips(arrays, name):
    n = len(arrays)
    out_shape = [jax.ShapeDtypeStruct(a.shape, a.dtype) for a in arrays]

    def body(*refs):
        ins, outs = refs[:n], refs[n:2 * n]
        send_sems, recv_sems, loc_sems = refs[2 * n:]
        x, y, c = lax.axis_index("x"), lax.axis_index("y"), lax.axis_index("c")
        myq = 2 * x + y
        copies = []
        for a in range(n):
            cp = pltpu.make_async_copy(ins[a].at[myq], outs[a].at[myq], loc_sems.at[a])
            cp.start()
            copies.append(cp)
        for k in range(1, N_CHIPS):
            qx = 1 - x if k & 2 else x
            qy = 1 - y if k & 1 else y
            for a in range(n):
                cp = pltpu.make_async_remote_copy(
                    src_ref=ins[a].at[2 * qx + qy], dst_ref=outs[a].at[myq],
                    send_sem=send_sems.at[a, k - 1], recv_sem=recv_sems.at[a, k - 1],
                    device_id=(qx, qy, c), device_id_type=pl.DeviceIdType.MESH)
                cp.start()
                copies.append(cp)
        for cp in copies:
            cp.wait()

    any_spec = pl.BlockSpec(memory_space=pl.ANY)
    return pl.pallas_call(
        body, name=name, out_shape=tuple(out_shape),
        in_specs=[any_spec] * n, out_specs=tuple([any_spec] * n),
        scratch_shapes=[pltpu.SemaphoreType.DMA((n, N_CHIPS - 1)), pltpu.SemaphoreType.DMA((n, N_CHIPS - 1)),
                        pltpu.SemaphoreType.DMA((n,))],
        compiler_params=pltpu.CompilerParams(has_side_effects=True),
    )(*arrays)


def _mm(a, b, out_dtype, name, tm=1024, tn=512):
    M, K = a.shape
    _, N = b.shape
    tm, tn = _tile(M, tm), _tile(N, tn)

    def body(a_ref, b_ref, o_ref):
        o_ref[...] = _dot(a_ref[...], b_ref[...]).astype(out_dtype)

    return pl.pallas_call(
        body, name=name, out_shape=jax.ShapeDtypeStruct((M, N), out_dtype),
        grid=(M // tm, N // tn),
        in_specs=[pl.BlockSpec((tm, K), lambda i, j: (i, 0)), pl.BlockSpec((K, tn), lambda i, j: (0, j))],
        out_specs=pl.BlockSpec((tm, tn), lambda i, j: (i, j)),
        compiler_params=_params(dimension_semantics=("parallel", "parallel")),
    )(a, b)


def _mm2(a1, b1, a2, b2, out_dtype, name, tm=1024, tn=512):
    M, K1 = a1.shape
    _, K2 = a2.shape
    _, N = b1.shape
    tm, tn = _tile(M, tm), _tile(N, tn)

    def body(a1_ref, b1_ref, a2_ref, b2_ref, o_ref):
        o_ref[...] = (_dot(a1_ref[...], b1_ref[...]) + _dot(a2_ref[...], b2_ref[...])).astype(out_dtype)

    return pl.pallas_call(
        body, name=name, out_shape=jax.ShapeDtypeStruct((M, N), out_dtype),
        grid=(M // tm, N // tn),
        in_specs=[pl.BlockSpec((tm, K1), lambda i, j: (i, 0)), pl.BlockSpec((K1, tn), lambda i, j: (0, j)),
                  pl.BlockSpec((tm, K2), lambda i, j: (i, 0)), pl.BlockSpec((K2, tn), lambda i, j: (0, j))],
        out_specs=pl.BlockSpec((tm, tn), lambda i, j: (i, j)),
        compiler_params=_params(dimension_semantics=("parallel", "parallel")),
    )(a1, b1, a2, b2)


def _mm_acc(a, b, name, tm=1408, tn=1408, tk=512):
    M, S = a.shape
    _, N = b.shape
    tm, tn, tk = _tile(M, tm), _tile(N, tn), _tile(S, tk)

    def body(a_ref, b_ref, o_ref):
        @pl.when(pl.program_id(2) == 0)
        def _():
            o_ref[...] = jnp.zeros_like(o_ref)

        o_ref[...] += _dot(a_ref[...], b_ref[...])

    return pl.pallas_call(
        body, name=name, out_shape=jax.ShapeDtypeStruct((M, N), F32),
        grid=(M // tm, N // tn, S // tk),
        in_specs=[pl.BlockSpec((tm, tk), lambda i, j, k: (i, k)), pl.BlockSpec((tk, tn), lambda i, j, k: (k, j))],
        out_specs=pl.BlockSpec((tm, tn), lambda i, j, k: (i, j)),
        compiler_params=_params(dimension_semantics=("parallel", "parallel", "arbitrary")),
    )(a, b)


def _silu(z):
    return z * (1.0 / (1.0 + jnp.exp(-z)))


def _ada_fwd(c_all, w_shard, b_shard):
    n = w_shard.shape[1]

    def body(c_ref, w_ref, b_ref, o_ref):
        o_ref[...] = _dot(_silu(c_ref[...]), w_ref[...], precision=HIGHEST) + b_ref[...]

    return pl.pallas_call(body, name="ada_fwd", out_shape=jax.ShapeDtypeStruct((N_DEV, n), F32),
                          compiler_params=_params())(c_all, w_shard, b_shard)


def _ada_bwd(c_all_t, dmod_cols):
    D = c_all_t.shape[0]
    n = dmod_cols.shape[1]

    def body(ct_ref, dm_ref, o_ref):
        sc = _silu(ct_ref[...])
        dm = dm_ref[...]
        acc = sc[:, 0:1] * dm[0:1, :]
        for b in range(1, N_DEV):
            acc = acc + sc[:, b:b + 1] * dm[b:b + 1, :]
        o_ref[...] = acc

    return pl.pallas_call(body, name="ada_bwd", out_shape=jax.ShapeDtypeStruct((D, n), F32),
                          compiler_params=_params())(c_all_t, dmod_cols)


def _row_specs(tm, widths):
    return [pl.BlockSpec((tm, w), lambda i: (i, 0)) for w in widths]


def _vec_spec(w):
    return pl.BlockSpec((1, w), lambda i: (0, 0))


def _col_spec(tm, w):
    return pl.BlockSpec((w, tm), lambda i: (0, i))


def _prenorm(x, g, scale, shift, name):
    S, D = x.shape
    tm = _tile(S, 512)

    def body(x_ref, g_ref, sc_ref, sh_ref, h_ref, ht_ref):
        xv = x_ref[...]
        r = lax.rsqrt(jnp.mean(xv * xv, axis=-1, keepdims=True) + EPS)
        h = (xv * r) * g_ref[...] * (1.0 + sc_ref[...]) + sh_ref[...]
        h_ref[...] = h.astype(BF16)
        ht_ref[...] = h.T.astype(BF16)

    return pl.pallas_call(
        body, name=name, grid=(S // tm,),
        out_shape=(jax.ShapeDtypeStruct((S, D), BF16), jax.ShapeDtypeStruct((D, S), BF16)),
        in_specs=_row_specs(tm, [D]) + [_vec_spec(D)] * 3,
        out_specs=(_row_specs(tm, [D])[0], _col_spec(tm, D)),
        compiler_params=_params(dimension_semantics=("parallel",)),
    )(x, g, scale, shift)


def _group_ones():
    r = lax.broadcasted_iota(jnp.int32, (LANES, LANES), 0) // HEAD_DIM
    c = lax.broadcasted_iota(jnp.int32, (LANES, LANES), 1) // HEAD_DIM
    return (r == c).astype(F32)


def _headnorm_fwd(o_f, o_s, g_f, g_s):
    S, dh = o_f.shape
    tm = _tile(S, 512)

    def body(of_ref, os_ref, gf_ref, gs_ref, mix_ref, mixt_ref):
        ones = _group_ones()
        for part, (o_ref, g_ref) in enumerate(((of_ref, gf_ref), (os_ref, gs_ref))):
            for t in range(dh // LANES):
                cols = slice(t * LANES, (t + 1) * LANES)
                out = slice(part * dh + t * LANES, part * dh + (t + 1) * LANES)
                o = o_ref[:, cols]
                ms = _dot(o * o, ones, precision=HIGHEST) * (1.0 / HEAD_DIM)
                mixn = o * lax.rsqrt(ms + EPS) * g_ref[:, cols]
                mix_ref[:, out] = mixn.astype(BF16)
                mixt_ref[out, :] = mixn.T.astype(BF16)

    return pl.pallas_call(
        body, name="headnorm_fwd", grid=(S // tm,),
        out_shape=(jax.ShapeDtypeStruct((S, 2 * dh), BF16), jax.ShapeDtypeStruct((2 * dh, S), BF16)),
        in_specs=_row_specs(tm, [dh, dh]) + [_vec_spec(dh)] * 2,
        out_specs=(_row_specs(tm, [2 * dh])[0], _col_spec(tm, 2 * dh)),
        compiler_params=_params(dimension_semantics=("parallel",)),
    )(o_f, o_s, g_f, g_s)


def _resid_prenorm(x, a_out, gate, g, scale, shift):
    S, D = x.shape
    tm = _tile(S, 512)

    def body(x_ref, a_ref, gt_ref, g_ref, sc_ref, sh_ref, x1_ref, h_ref, ht_ref):
        x1 = x_ref[...] + gt_ref[...] * a_ref[...]
        x1_ref[...] = x1
        r = lax.rsqrt(jnp.mean(x1 * x1, axis=-1, keepdims=True) + EPS)
        h = (x1 * r) * g_ref[...] * (1.0 + sc_ref[...]) + sh_ref[...]
        h_ref[...] = h.astype(BF16)
        ht_ref[...] = h.T.astype(BF16)

    return pl.pallas_call(
        body, name="resid_prenorm", grid=(S // tm,),
        out_shape=(jax.ShapeDtypeStruct((S, D), F32), jax.ShapeDtypeStruct((S, D), BF16),
                   jax.ShapeDtypeStruct((D, S), BF16)),
        in_specs=_row_specs(tm, [D, D]) + [_vec_spec(D)] * 4,
        out_specs=tuple(_row_specs(tm, [D, D]) + [_col_spec(tm, D)]),
        compiler_params=_params(dimension_semantics=("parallel",)),
    )(x, a_out, gate, g, scale, shift)


def _shift_down(main, halo, k):
    ext = jnp.concatenate([halo, main], axis=0)
    return pltpu.roll(ext, k, 0)[halo.shape[0]:]


def _shift_up(main, halo, k):
    ext = jnp.concatenate([main, halo], axis=0)
    n = ext.shape[0]
    return pltpu.roll(ext, n - k, 0)[:main.shape[0]]


def _conv(up, up_halo, w_ref, b_ref):
    return (w_ref[2:3, :] * up + w_ref[1:2, :] * _shift_down(up, up_halo, 1)
            + w_ref[0:1, :] * _shift_down(up, up_halo, 2) + b_ref[...])


def _prev_halo_map(tm):
    step = tm // HALO
    return lambda j, i: (jnp.maximum(i * step - 1, 0), j)


MLP_TM = 512
MLP_CT = 256
CARRY = 8


def _mlp_up(h, wg, wv, cwg, cwv, cbg, cbv):
    S, D = h.shape
    F = wg.shape[1]
    tm, ct = _tile(S, MLP_TM), _tile(F, MLP_CT)
    nct = F // ct

    def body(h_ref, wg_ref, wv_ref, cwg_ref, cwv_ref, cbg_ref, cbv_ref,
             upg_ref, upv_ref, act_ref, actt_ref, hg_scr, hv_scr):
        i, j = pl.program_id(0), pl.program_id(1)
        hv = h_ref[...]
        us = []
        for w_ref, cw_ref, cb_ref, up_ref, scr in ((wg_ref, cwg_ref, cbg_ref, upg_ref, hg_scr),
                                                   (wv_ref, cwv_ref, cbv_ref, upv_ref, hv_scr)):
            up = _dot(hv, w_ref[...]).astype(BF16)
            up_ref[...] = up
            upf = up.astype(F32)
            halo = jnp.where(i == 0, 0.0, scr[j])
            us.append(_conv(upf, halo, cw_ref, cb_ref))
            scr[j] = upf[tm - CARRY:, :]
        act = _silu(us[0]) * us[1]
        act_ref[...] = act.astype(BF16)
        actt_ref[...] = act.T.astype(BF16)

    blk = pl.BlockSpec((tm, ct), lambda i, j: (i, j))
    wspec = pl.BlockSpec((D, ct), lambda i, j: (0, j))
    cwspec = pl.BlockSpec((CONV_W, ct), lambda i, j: (0, j))
    cbspec = pl.BlockSpec((1, ct), lambda i, j: (0, j))
    sds = jax.ShapeDtypeStruct((S, F), BF16)
    return pl.pallas_call(
        body, name="mlp_up", grid=(S // tm, nct),
        out_shape=(sds, sds, sds, jax.ShapeDtypeStruct((F, S), BF16)),
        in_specs=[pl.BlockSpec((tm, D), lambda i, j: (i, 0)), wspec, wspec, cwspec, cwspec, cbspec, cbspec],
        out_specs=(blk, blk, blk, pl.BlockSpec((ct, tm), lambda i, j: (j, i))),
        scratch_shapes=[pltpu.VMEM((nct, CARRY, ct), F32), pltpu.VMEM((nct, CARRY, ct), F32)],
        compiler_params=_params(dimension_semantics=("arbitrary", "arbitrary")),
    )(h, wg, wv, cwg, cwv, cbg, cbv)


def _conv_act_bwd(dact, up_g, up_v, cwg, cwv, cbg, cbv):
    S, F = up_g.shape
    tm, ct = _tile(S, 256), _tile(F, CONV_COLS)
    nct = F // ct

    def body(da_ref, ug_ref, uv_ref, hg_ref, hv_ref, wg_ref, wv_ref, bg_ref, bv_ref,
             dug_ref, duv_ref, pg_ref, pv_ref):
        first = pl.program_id(1) == 0

        @pl.when(first)
        def _():
            pg_ref[...] = jnp.zeros_like(pg_ref)
            pv_ref[...] = jnp.zeros_like(pv_ref)

        da = da_ref[...].astype(F32)
        taps = []
        for u_ref, h_ref in ((ug_ref, hg_ref), (uv_ref, hv_ref)):
            h = jnp.where(first, 0.0, h_ref[...].astype(F32))
            uu = u_ref[...].astype(F32)
            taps.append((_shift_down(uu, h, 2), _shift_down(uu, h, 1), uu))
        u_g = wg_ref[0:1, :] * taps[0][0] + wg_ref[1:2, :] * taps[0][1] + wg_ref[2:3, :] * taps[0][2] + bg_ref[...]
        u_v = wv_ref[0:1, :] * taps[1][0] + wv_ref[1:2, :] * taps[1][1] + wv_ref[2:3, :] * taps[1][2] + bv_ref[...]
        sg = 1.0 / (1.0 + jnp.exp(-u_g))
        du_g = da * u_v * (sg * (1.0 + u_g * (1.0 - sg)))
        du_v = da * (u_g * sg)
        dug_ref[...] = du_g.astype(BF16)
        duv_ref[...] = du_v.astype(BF16)
        for du, tp, p_ref in ((du_g, taps[0], pg_ref), (du_v, taps[1], pv_ref)):
            for k in range(CONV_W):
                p_ref[k:k + 1, :] += jnp.sum(du * tp[k], axis=0, keepdims=True)
            p_ref[CONV_W:CONV_W + 1, :] += jnp.sum(du, axis=0, keepdims=True)

    main = pl.BlockSpec((tm, ct), lambda j, i: (i, j))
    halo = pl.BlockSpec((HALO, ct), _prev_halo_map(tm))
    wspec = pl.BlockSpec((CONV_W, ct), lambda j, i: (0, j))
    bspec = pl.BlockSpec((1, ct), lambda j, i: (0, j))
    pspec = pl.BlockSpec((8, ct), lambda j, i: (0, j))
    return pl.pallas_call(
        body, name="conv_act_bwd", grid=(nct, S // tm),
        out_shape=(jax.ShapeDtypeStruct((S, F), BF16), jax.ShapeDtypeStruct((S, F), BF16),
                   jax.ShapeDtypeStruct((8, F), F32), jax.ShapeDtypeStruct((8, F), F32)),
        in_specs=[main, main, main, halo, halo, wspec, wspec, bspec, bspec],
        out_specs=(main, main, pspec, pspec),
        compiler_params=_params(dimension_semantics=("parallel", "arbitrary")),
    )(dact, up_g, up_v, up_g, up_v, cwg, cwv, cbg, cbv)


def _conv_bwd_input(du, cw, name):
    S, C = du.shape
    tm, ct = _tile(S, 256), _tile(C, CONV_COLS)
    step = tm // HALO
    last_halo = S // HALO - 1

    def body(du_ref, h_ref, w_ref, o_ref):
        last = pl.program_id(1) == pl.num_programs(1) - 1
        d = du_ref[...].astype(F32)
        h = jnp.where(last, 0.0, h_ref[...].astype(F32))
        o_ref[...] = (w_ref[2:3, :] * d + w_ref[1:2, :] * _shift_up(d, h, 1)
                      + w_ref[0:1, :] * _shift_up(d, h, 2)).astype(BF16)

    return pl.pallas_call(
        body, name=name, out_shape=jax.ShapeDtypeStruct((S, C), BF16), grid=(C // ct, S // tm),
        in_specs=[pl.BlockSpec((tm, ct), lambda j, i: (i, j)),
                  pl.BlockSpec((HALO, ct), lambda j, i: (jnp.minimum((i + 1) * step, last_halo), j)),
                  pl.BlockSpec((CONV_W, ct), lambda j, i: (0, j))],
        out_specs=pl.BlockSpec((tm, ct), lambda j, i: (i, j)),
        compiler_params=_params(dimension_semantics=("parallel", "parallel")),
    )(du, du, cw)


def _scan_mats(R, nc, reverse):
    i = lax.broadcasted_iota(jnp.int32, (LANES, LANES), 0)
    j = lax.broadcasted_iota(jnp.int32, (LANES, LANES), 1)
    inner = ((i >= j) if reverse else (i <= j)).astype(F32)
    r = lax.broadcasted_iota(jnp.int32, (R, R), 0)
    c = lax.broadcasted_iota(jnp.int32, (R, R), 1)
    same = (r // nc) == (c // nc)
    outer = (same & ((c > r) if reverse else (c < r))).astype(F32)
    return inner, outer


def _chunk_scan(v, inner, outer, reverse):
    w = _dot(v, inner, precision=HIGHEST)
    col = 0 if reverse else LANES - 1
    carry = _dot(outer, w, precision=HIGHEST)[:, col:col + 1]
    return w + carry


def _fgate_fwd(z_rows, nc):
    R = z_rows.shape[0]

    def body(z_ref, f_ref):
        z = z_ref[...]
        logf = jnp.minimum(z, 0.0) - jnp.log(1.0 + jnp.exp(-jnp.abs(z)))
        inner, outer = _scan_mats(R, nc, False)
        f_ref[...] = _chunk_scan(logf, inner, outer, False)

    return pl.pallas_call(body, name="fgate_fwd", out_shape=jax.ShapeDtypeStruct((R, LANES), F32),
                          compiler_params=_params())(z_rows)


def _fgate_bwd(dfk_neg_rows, dfq_rows, z_rows, nc):
    R = z_rows.shape[0]
    nh = R // nc

    def body(dfk_ref, dfq_ref, z_ref, dz_ref, db_ref):
        inner, outer = _scan_mats(R, nc, True)
        dlogf = _chunk_scan(dfq_ref[...] - dfk_ref[...], inner, outer, True)
        dz = dlogf * (1.0 / (1.0 + jnp.exp(z_ref[...])))
        dz_ref[...] = dz
        hr = lax.broadcasted_iota(jnp.int32, (nh, R), 0)
        hc = lax.broadcasted_iota(jnp.int32, (nh, R), 1) // nc
        per_head = _dot((hr == hc).astype(F32), dz, precision=HIGHEST)
        db_ref[...] = jnp.sum(per_head, axis=1, keepdims=True)

    return pl.pallas_call(
        body, name="fgate_bwd",
        out_shape=(jax.ShapeDtypeStruct((R, LANES), F32), jax.ShapeDtypeStruct((nh, 1), F32)),
        compiler_params=_params())(dfk_neg_rows, dfq_rows, z_rows)


_NEG = -1e30
SKIP_BELOW = -106.0
_SCALE = HEAD_DIM ** -0.5
N_SCAN = ATT_BK // SCAN_BK
F_PARTS = 3
Q_F_LANE = HEAD_DIM
Q_ONE_LANE = HEAD_DIM + F_PARTS


def _kv_slice(j):
    return pl.ds(pl.multiple_of(j * ATT_BK, ATT_BK), ATT_BK)


def _mask_t(strict):
    s = lax.broadcasted_iota(jnp.int32, (ATT_BK, ATT_BQ), 0)
    t = lax.broadcasted_iota(jnp.int32, (ATT_BK, ATT_BQ), 1)
    return (s < t) if strict else (s <= t)


def _walk_down(i, step, alive, carry):
    carry = step(i, carry, True)

    def cond(st):
        n, go, _ = st
        return jnp.logical_and(n < i, go)

    def body(st):
        n, _, cr = st
        j = i - 1 - n
        cr = step(j, cr, False)
        return n + 1, alive(jnp.maximum(j - 1, 0), cr), cr

    return lax.while_loop(cond, body, (jnp.int32(0), alive(jnp.maximum(i - 1, 0), carry), carry))[2]


def _t_block(rows):
    return pl.BlockSpec((None, rows, ATT_BQ), lambda h, i, *_: (h, 0, i))


def _t_full(rows, S):
    return pl.BlockSpec((None, rows, S), lambda h, i, *_: (h, 0, 0))


def _n_block():
    return pl.BlockSpec((None, ATT_BQ, LANES), lambda h, i, *_: (h, i, 0))


def _n_full(S):
    return pl.BlockSpec((None, S, LANES), lambda h, i, *_: (h, 0, 0))


def _heads(t):
    S = t.shape[0]
    return jnp.transpose(t.reshape(S, -1, HEAD_DIM), (1, 0, 2))


def _unheads(t):
    return jnp.transpose(t, (1, 0, 2)).reshape(t.shape[1], -1)


def _pad_lanes(t):
    return jnp.pad(t, ((0, 0), (0, 0), (0, LANES - t.shape[2])))


def _tr(t):
    return jnp.transpose(t, (0, 2, 1))


def _split3(f):
    def bf16_round(t):
        return lax.reduce_precision(t, exponent_bits=8, mantissa_bits=7)

    hi = bf16_round(f)
    mid = bf16_round(f - hi)
    lo = bf16_round(f - hi - mid)
    return jnp.stack([hi, mid, lo], axis=-1).astype(BF16)


def _fox_operands(q, k, v, f_rows):
    qh, kh, vh = _heads(q) * _SCALE, _heads(k), _heads(v)
    f3 = _split3(f_rows)
    ones = jnp.ones_like(f3)
    q_n = _pad_lanes(jnp.concatenate([qh, f3, ones], axis=-1))
    k_n = _pad_lanes(jnp.concatenate([kh, ones, -f3], axis=-1))
    H, S = f_rows.shape
    f_end = f_rows.reshape(H, S // ATT_BK, ATT_BK)[:, :, -1]
    k_sq = jnp.sum(jnp.square(kh.astype(F32)), axis=-1).reshape(H, S // ATT_BK, ATT_BK)
    k_max = lax.cummax(jnp.sqrt(jnp.max(k_sq, axis=-1)), axis=1)
    return dict(q_n=q_n, q_t=_tr(q_n), k_n=k_n, k_t=_tr(k_n), v_n=_pad_lanes(vh), v_t=_tr(vh),
                f_end=f_end, k_max=k_max)


def _sb_operands(q, k, v):
    qh, kh, vh = _heads(q) * _SCALE, _heads(k), _heads(v)
    q_n, k_n = _pad_lanes(qh), _pad_lanes(kh)
    return dict(q_n=q_n, q_t=_tr(q_n), k_n=k_n, k_t=_tr(k_n), v_n=_pad_lanes(vh), v_t=_tr(vh))


def _do_operands(do):
    do_n = _pad_lanes(_heads(do).astype(BF16))
    return do_n, _tr(do_n)


def _fox_reach(qt, fend_ref, kmax_ref, h):
    qf = qt.astype(F32)
    q_norm = jnp.sqrt(jnp.sum(jnp.square(qf[:HEAD_DIM]), axis=0, keepdims=True))
    f_t = jnp.sum(qf[Q_F_LANE:Q_F_LANE + F_PARTS], axis=0, keepdims=True)
    return lambda j: q_norm * kmax_ref[h, j] + f_t - fend_ref[h, j]


def _fox_fwd(q_t, k_n, v_t, f_end, k_max, shards):
    H, _, S = q_t.shape
    n, nq = len(shards), S // ATT_BQ

    def body(fend_ref, kmax_ref, qt_ref, k_ref, vt_ref, *rest):
        ins, (ot_ref, lse_ref), outs, sems = rest[:n], rest[n:n + 2], rest[n + 2:2 * n + 2], rest[2 * n + 2:]
        h, i = pl.program_id(0), pl.program_id(1)

        @pl.when(jnp.logical_and(h == 0, i == 0))
        def _():
            for cp in _exchange_copies(ins, outs, *sems, False):
                cp.start()

        qt = qt_ref[...]
        reach = _fox_reach(qt, fend_ref, kmax_ref, h)

        def step(j, carry, masked):
            m, l, acc = carry
            ks = _kv_slice(j)
            s = _dot(k_ref[ks, :], qt)
            if masked:
                s = jnp.where(_mask_t(False), s, _NEG)
            mn = jnp.maximum(m, jnp.max(s, axis=0, keepdims=True))
            alpha = jnp.exp(m - mn)
            p = jnp.exp(s - mn)
            l = alpha * l + jnp.sum(p, axis=0, keepdims=True)
            acc = acc * alpha + _dot(vt_ref[:, ks], p.astype(BF16))
            return mn, l, acc

        def alive(j, carry):
            return jnp.max(reach(j) - carry[0]) > SKIP_BELOW

        row = jnp.zeros((1, ATT_BQ), F32)
        m, l, acc = _walk_down(i, step, alive, (row + _NEG, row, jnp.zeros((HEAD_DIM, ATT_BQ), F32)))
        ot_ref[...] = acc / l
        lse_ref[...] = m + jnp.log(l)

        @pl.when(jnp.logical_and(h == H - 1, i == nq - 1))
        def _():
            for cp in _exchange_copies(ins, outs, *sems, False):
                cp.wait()

    any_spec = pl.BlockSpec(memory_space=pl.ANY)
    grid_spec = pltpu.PrefetchScalarGridSpec(
        num_scalar_prefetch=2, grid=(H, nq),
        in_specs=[_t_block(LANES), _n_full(S), _t_full(HEAD_DIM, S)] + [any_spec] * n,
        out_specs=tuple([_t_block(HEAD_DIM), _t_block(1)] + [any_spec] * n),
        scratch_shapes=_exchange_sems(n))
    res = pl.pallas_call(
        body, name="fox_fwd", grid_spec=grid_spec,
        out_shape=tuple([jax.ShapeDtypeStruct((H, HEAD_DIM, S), F32), jax.ShapeDtypeStruct((H, 1, S), F32)]
                        + _exchange_out_shapes(shards, False)),
        compiler_params=_params(dimension_semantics=("arbitrary", "arbitrary"), has_side_effects=True),
    )(f_end, k_max, q_t, k_n, v_t, *shards)
    return res[0], res[1], res[2:]


def _fox_bwd(q_t, q_n, k_n, k_t, v_n, do_t, do_n, o_t, lse, f_end, k_max):
    H, _, S = q_t.shape

    def body(fend_ref, kmax_ref, qt_ref, qn_ref, k_ref, kt_ref, v_ref, dot_ref, don_ref, ot_ref, lse_ref,
             dqt_ref, dk_ref, dv_ref):
        h, i = pl.program_id(0), pl.program_id(1)

        @pl.when(i == 0)
        def _():
            dk_ref[...] = jnp.zeros_like(dk_ref)
            dv_ref[...] = jnp.zeros_like(dv_ref)

        qt, qn, dot, don = qt_ref[...], qn_ref[...], dot_ref[...], don_ref[...]
        lse = lse_ref[...]
        delta = jnp.sum(dot[:HEAD_DIM].astype(F32) * ot_ref[...], axis=0, keepdims=True)
        reach = _fox_reach(qt, fend_ref, kmax_ref, h)

        def alive(j, dq):
            return jnp.max(reach(j) - lse) > SKIP_BELOW

        def step(j, dq, masked):
            ks = _kv_slice(j)
            s = _dot(k_ref[ks, :], qt)
            if masked:
                s = jnp.where(_mask_t(False), s, _NEG)
            p = jnp.exp(s - lse)
            ds = (p * (_dot(v_ref[ks, :], dot) - delta)).astype(BF16)
            dk_ref[ks, :] += _dot(ds, qn)
            dv_ref[ks, :] += _dot(p.astype(BF16), don)
            return dq + _dot(kt_ref[:, ks], ds)

        dqt_ref[...] = _walk_down(i, step, alive, jnp.zeros((LANES, ATT_BQ), F32))

    grid_spec = pltpu.PrefetchScalarGridSpec(
        num_scalar_prefetch=2, grid=(H, S // ATT_BQ),
        in_specs=[_t_block(LANES), _n_block(), _n_full(S), _t_full(LANES, S), _n_full(S),
                  _t_block(LANES), _n_block(), _t_block(HEAD_DIM), _t_block(1)],
        out_specs=(_t_block(LANES), _n_full(S), _n_full(S)))
    return pl.pallas_call(
        body, name="fox_bwd", grid_spec=grid_spec,
        out_shape=(jax.ShapeDtypeStruct((H, LANES, S), F32), jax.ShapeDtypeStruct((H, S, LANES), F32),
                   jax.ShapeDtypeStruct((H, S, LANES), F32)),
        compiler_params=_params(dimension_semantics=("parallel", "arbitrary")),
    )(f_end, k_max, q_t, q_n, k_n, k_t, v_n, do_t, do_n, o_t, lse)


def _scan_lhs():
    r = lax.broadcasted_iota(jnp.int32, (SCAN_BK, 2 * SCAN_BK), 0)
    c = lax.broadcasted_iota(jnp.int32, (SCAN_BK, 2 * SCAN_BK), 1) % SCAN_BK
    return (c >= r).astype(BF16)


def _suffix_sum(t, lhs):
    hi = t.astype(BF16)
    lo = (t - hi.astype(F32)).astype(BF16)
    return _dot(lhs, jnp.concatenate([hi, lo], axis=0))


def _sb_scores(k, qt, mask):
    z = _dot(k, qt)
    e = jnp.exp(-jnp.abs(z))
    lb = -(jnp.maximum(z, 0.0) + jnp.log(1.0 + e))
    if mask is not None:
        lb = jnp.where(mask, lb, 0.0)
    return z, e, lb


def _scan_blocks():
    return [slice(u * SCAN_BK, (u + 1) * SCAN_BK) for u in reversed(range(N_SCAN))]


def _sb_fwd(q_t, k_n, v_t):
    H, _, S = q_t.shape

    def body(qt_ref, k_ref, vt_ref, ot_ref):
        i = pl.program_id(1)
        qt = qt_ref[...]
        lhs = _scan_lhs()

        def step(j, carry, masked):
            c, acc = carry
            ks = _kv_slice(j)
            mask = _mask_t(True) if masked else None
            z, _, lb = _sb_scores(k_ref[ks, :], qt, mask)
            parts = []
            for sl in _scan_blocks():
                rin = _suffix_sum(lb[sl], lhs)
                a = jnp.exp(z[sl] + rin + c)
                if masked:
                    a = jnp.where(mask[sl], a, 0.0)
                parts.append(a.astype(BF16))
                c = c + rin[0:1, :]
            a_all = jnp.concatenate(parts[::-1], axis=0)
            return c, acc + _dot(vt_ref[:, ks], a_all)

        carry = (jnp.zeros((1, ATT_BQ), F32), jnp.zeros((HEAD_DIM, ATT_BQ), F32))
        ot_ref[...] = _walk_down(i, step, lambda j, cr: jnp.max(cr[0]) > SKIP_BELOW, carry)[1]

    return pl.pallas_call(
        body, name="sb_fwd", grid=(H, S // ATT_BQ),
        out_shape=jax.ShapeDtypeStruct((H, HEAD_DIM, S), F32),
        in_specs=[_t_block(LANES), _n_full(S), _t_full(HEAD_DIM, S)],
        out_specs=_t_block(HEAD_DIM),
        compiler_params=_params(dimension_semantics=("parallel", "parallel")),
    )(q_t, k_n, v_t)


def _sb_bwd(q_t, q_n, k_n, k_t, v_n, do_t, do_n, o_t, bound):
    H, _, S = q_t.shape
    n, nq = len(bound), S // ATT_BQ

    def body(qt_ref, qn_ref, k_ref, kt_ref, v_ref, dot_ref, don_ref, ot_ref, *rest):
        ins, (dqt_ref, dk_ref, dv_ref) = rest[:n], rest[n:n + 3]
        outs, sems = rest[n + 3:2 * n + 3], rest[2 * n + 3:]
        h, i = pl.program_id(0), pl.program_id(1)

        @pl.when(jnp.logical_and(h == 0, i == 0))
        def _():
            for cp in _exchange_copies(ins, outs, *sems, True):
                cp.start()

        @pl.when(i == 0)
        def _():
            dk_ref[...] = jnp.zeros_like(dk_ref)
            dv_ref[...] = jnp.zeros_like(dv_ref)

        qt, qn, dot, don = qt_ref[...], qn_ref[...], dot_ref[...], don_ref[...]
        lhs = _scan_lhs()
        delta = jnp.sum(dot[:HEAD_DIM].astype(F32) * ot_ref[...], axis=0, keepdims=True)

        def step(j, carry, masked):
            c, g, dq = carry
            ks = _kv_slice(j)
            mask = _mask_t(True) if masked else None
            z, e, lb = _sb_scores(k_ref[ks, :], qt, mask)
            da = _dot(v_ref[ks, :], dot)
            a_parts, dz_parts = [], []
            for sl in _scan_blocks():
                rin = _suffix_sum(lb[sl], lhs)
                a = jnp.exp(z[sl] + rin + c)
                if masked:
                    a = jnp.where(mask[sl], a, 0.0)
                ab = a.astype(BF16)
                gg = ab.astype(F32) * da[sl]
                rgin = _suffix_sum(gg, lhs)
                rinv = 1.0 / (1.0 + e[sl])
                sig = jnp.where(z[sl] >= 0.0, rinv, e[sl] * rinv)
                dz = gg - sig * (delta - g - (rgin - gg))
                if masked:
                    dz = jnp.where(mask[sl], dz, 0.0)
                a_parts.append(ab)
                dz_parts.append(dz.astype(BF16))
                c = c + rin[0:1, :]
                g = g + rgin[0:1, :]
            ab_all = jnp.concatenate(a_parts[::-1], axis=0)
            dzb = jnp.concatenate(dz_parts[::-1], axis=0)
            dk_ref[ks, :] += _dot(dzb, qn)
            dv_ref[ks, :] += _dot(ab_all, don)
            return c, g, dq + _dot(kt_ref[:, ks], dzb)

        row = jnp.zeros((1, ATT_BQ), F32)
        carry = (row, row, jnp.zeros((LANES, ATT_BQ), F32))
        dqt_ref[...] = _walk_down(i, step, lambda j, cr: jnp.max(cr[0]) > SKIP_BELOW, carry)[2]

        @pl.when(jnp.logical_and(h == H - 1, i == nq - 1))
        def _():
            for cp in _exchange_copies(ins, outs, *sems, True):
                cp.wait()

    any_spec = pl.BlockSpec(memory_space=pl.ANY)
    res = pl.pallas_call(
        body, name="sb_bwd", grid=(H, nq),
        out_shape=tuple([jax.ShapeDtypeStruct((H, LANES, S), F32), jax.ShapeDtypeStruct((H, S, LANES), F32),
                         jax.ShapeDtypeStruct((H, S, LANES), F32)] + _exchange_out_shapes(bound, True)),
        in_specs=[_t_block(LANES), _n_block(), _n_full(S), _t_full(LANES, S), _n_full(S),
                  _t_block(LANES), _n_block(), _t_block(HEAD_DIM)] + [any_spec] * n,
        out_specs=tuple([_t_block(LANES), _n_full(S), _n_full(S)] + [any_spec] * n),
        scratch_shapes=_exchange_sems(n),
        compiler_params=_params(dimension_semantics=("arbitrary", "arbitrary"), has_side_effects=True),
    )(q_t, q_n, k_n, k_t, v_n, do_t, do_n, o_t, *bound)
    return res[0], res[1], res[2], res[3:]


def _acc_spec(w):
    return pl.BlockSpec((1, w), lambda i: (0, 0))


def _loss_head(x1, m_out, gate_m, g_final, target):
    S, D = x1.shape
    tm = _tile(S, 256)

    def body(x1_ref, mo_ref, gt_ref, gf_ref, tg_ref, dx2_ref, gm_ref, loss_ref, dgf_ref, dgt_ref):
        @pl.when(pl.program_id(0) == 0)
        def _():
            loss_ref[...] = jnp.zeros_like(loss_ref)
            dgf_ref[...] = jnp.zeros_like(dgf_ref)
            dgt_ref[...] = jnp.zeros_like(dgt_ref)

        mo = mo_ref[...]
        x2 = x1_ref[...] + gt_ref[...] * mo
        r = lax.rsqrt(jnp.mean(x2 * x2, axis=-1, keepdims=True) + EPS)
        xh = x2 * r
        diff = xh * gf_ref[...] - tg_ref[...]
        loss_ref[...] += (0.5 / D) * jnp.sum(diff * diff)
        dy = diff * (1.0 / D)
        dgf_ref[...] += jnp.sum(dy * xh, axis=0, keepdims=True)
        dxh = dy * gf_ref[...]
        dx2 = r * (dxh - xh * jnp.mean(dxh * xh, axis=-1, keepdims=True))
        dx2_ref[...] = dx2
        gm_ref[...] = (dx2 * gt_ref[...]).astype(BF16)
        dgt_ref[...] += jnp.sum(dx2 * mo, axis=0, keepdims=True)

    return pl.pallas_call(
        body, name="loss_head", grid=(S // tm,),
        out_shape=(jax.ShapeDtypeStruct((S, D), F32), jax.ShapeDtypeStruct((S, D), BF16),
                   jax.ShapeDtypeStruct((1, LANES), F32), jax.ShapeDtypeStruct((1, D), F32),
                   jax.ShapeDtypeStruct((1, D), F32)),
        in_specs=_row_specs(tm, [D, D]) + [_vec_spec(D)] * 2 + _row_specs(tm, [D]),
        out_specs=tuple(_row_specs(tm, [D, D]) + [_acc_spec(LANES), _acc_spec(D), _acc_spec(D)]),
        compiler_params=_params(dimension_semantics=("arbitrary",)),
    )(x1, m_out, gate_m, g_final, target)


def _norm_bwd(dh, xin, dres, g, scale, name, gate=None, branch=None):
    S, D = xin.shape
    tm = _tile(S, 256)
    gated = gate is not None

    def body(*refs):
        if gated:
            (dh_ref, x_ref, dr_ref, g_ref, sc_ref, gt_ref, br_ref,
             dx_ref, dsc_ref, dsh_ref, dg_ref, ga_ref, dgt_ref) = refs
            sums = (dsc_ref, dsh_ref, dg_ref, dgt_ref)
        else:
            dh_ref, x_ref, dr_ref, g_ref, sc_ref, dx_ref, dsc_ref, dsh_ref, dg_ref = refs
            sums = (dsc_ref, dsh_ref, dg_ref)

        @pl.when(pl.program_id(0) == 0)
        def _():
            for s_ref in sums:
                s_ref[...] = jnp.zeros_like(s_ref)

        dhv, xv = dh_ref[...], x_ref[...]
        r = lax.rsqrt(jnp.mean(xv * xv, axis=-1, keepdims=True) + EPS)
        xh = xv * r
        dsc_ref[...] += jnp.sum(dhv * (xh * g_ref[...]), axis=0, keepdims=True)
        dsh_ref[...] += jnp.sum(dhv, axis=0, keepdims=True)
        dn = dhv * (1.0 + sc_ref[...])
        dg_ref[...] += jnp.sum(dn * xh, axis=0, keepdims=True)
        dxh = dn * g_ref[...]
        dx = dr_ref[...] + r * (dxh - xh * jnp.mean(dxh * xh, axis=-1, keepdims=True))
        dx_ref[...] = dx
        if gated:
            ga_ref[...] = (dx * gt_ref[...]).astype(BF16)
            dgt_ref[...] += jnp.sum(dx * br_ref[...], axis=0, keepdims=True)

    vec = jax.ShapeDtypeStruct((1, D), F32)
    out_shape = [jax.ShapeDtypeStruct((S, D), F32), vec, vec, vec]
    out_specs = _row_specs(tm, [D]) + [_acc_spec(D)] * 3
    in_specs = _row_specs(tm, [D, D, D]) + [_vec_spec(D)] * 2
    args = [dh, xin, dres, g, scale]
    if gated:
        out_shape += [jax.ShapeDtypeStruct((S, D), BF16), vec]
        out_specs += _row_specs(tm, [D]) + [_acc_spec(D)]
        in_specs += [_vec_spec(D)] + _row_specs(tm, [D])
        args += [gate, branch]
    return pl.pallas_call(
        body, name=name, grid=(S // tm,), out_shape=tuple(out_shape),
        in_specs=in_specs, out_specs=tuple(out_specs),
        compiler_params=_params(dimension_semantics=("arbitrary",)),
    )(*args)


def _headnorm_bwd(dmix, o_f, o_s, g_f, g_s):
    S, dh = o_f.shape
    tm = _tile(S, 256)

    def body(dm_ref, of_ref, os_ref, gf_ref, gs_ref, dof_ref, dos_ref, dgf_ref, dgs_ref):
        @pl.when(pl.program_id(0) == 0)
        def _():
            dgf_ref[...] = jnp.zeros_like(dgf_ref)
            dgs_ref[...] = jnp.zeros_like(dgs_ref)

        ones = _group_ones()
        parts = ((of_ref, gf_ref, dof_ref, dgf_ref), (os_ref, gs_ref, dos_ref, dgs_ref))
        for part, (o_ref, g_ref, do_ref, dg_ref) in enumerate(parts):
            for t in range(dh // LANES):
                cols = slice(t * LANES, (t + 1) * LANES)
                o = o_ref[:, cols]
                dm = dm_ref[:, part * dh + t * LANES: part * dh + (t + 1) * LANES]
                r = lax.rsqrt(_dot(o * o, ones, precision=HIGHEST) * (1.0 / HEAD_DIM) + EPS)
                oh = o * r
                dg_ref[:, cols] += jnp.sum(dm * oh, axis=0, keepdims=True)
                dn = dm * g_ref[:, cols]
                mean = _dot(dn * oh, ones, precision=HIGHEST) * (1.0 / HEAD_DIM)
                do_ref[:, cols] = r * (dn - oh * mean)

    vec = jax.ShapeDtypeStruct((1, dh), F32)
    return pl.pallas_call(
        body, name="headnorm_bwd", grid=(S // tm,),
        out_shape=(jax.ShapeDtypeStruct((S, dh), F32), jax.ShapeDtypeStruct((S, dh), F32), vec, vec),
        in_specs=_row_specs(tm, [2 * dh, dh, dh]) + [_vec_spec(dh)] * 2,
        out_specs=tuple(_row_specs(tm, [dh, dh]) + [_acc_spec(dh)] * 2),
        compiler_params=_params(dimension_semantics=("arbitrary",)),
    )(dmix, o_f, o_s, g_f, g_s)


def _adamw(w, gslots, m, v, name):
    R, C = w.shape
    n = gslots.shape[0]
    tr = 256 if (R % 256 == 0 and R > 256) else R
    bc1 = 1.0 - ADAM_B1 ** ADAM_STEP
    bc2 = 1.0 - ADAM_B2 ** ADAM_STEP

    def body(w_ref, gs_ref, m_ref, v_ref, g_ref, d_ref, nm_ref, nv_ref):
        g = gs_ref[0]
        for s in range(1, n):
            g = g + gs_ref[s]
        nm = ADAM_B1 * m_ref[...] + (1.0 - ADAM_B1) * g
        nv = ADAM_B2 * v_ref[...] + (1.0 - ADAM_B2) * (g * g)
        g_ref[...] = g
        nm_ref[...] = nm
        nv_ref[...] = nv
        d_ref[...] = -ADAM_LR * ((nm / bc1) / (jnp.sqrt(nv / bc2) + ADAM_EPS) + ADAM_WD * w_ref[...])

    blk = pl.BlockSpec((tr, C), lambda i: (i, 0))
    sds = jax.ShapeDtypeStruct((R, C), F32)
    return pl.pallas_call(
        body, name=name, grid=(R // tr,), out_shape=(sds,) * 4,
        in_specs=[blk, pl.BlockSpec((n, tr, C), lambda i: (0, i, 0)), blk, blk], out_specs=(blk,) * 4,
        compiler_params=_params(dimension_semantics=("parallel",)),
    )(w, gslots, m, v)


def _slot_sum(slots, name):
    n, _, C = slots.shape

    def body(s_ref, o_ref):
        acc = s_ref[0]
        for s in range(1, n):
            acc = acc + s_ref[s]
        o_ref[...] = acc

    return pl.pallas_call(body, name=name, out_shape=jax.ShapeDtypeStruct((1, C), F32),
                          compiler_params=_params())(slots)


def _pad_cols(a, n):
    return jnp.pad(a, ((0, 0), (0, n - a.shape[1])))


def _ungather(g, axis):
    if axis == 0:
        return g.reshape(g.shape[0] * g.shape[1], g.shape[2])
    return jnp.transpose(g, (1, 0, 2)).reshape(g.shape[1], g.shape[0] * g.shape[2])


def _to_slots(full, axis):
    R, C = full.shape
    if axis == 0:
        return full.reshape(N_DEV, R // N_DEV, C)
    return jnp.transpose(full.reshape(R, N_DEV, C // N_DEV), (1, 0, 2))


def kernel(x, c, w_ada, b_ada, g_attn, w_in, b_fgate, g_out_fox, g_out_sb, w_out, g_mlp, w_up, conv_w, conv_b, w_down, g_final, loss_target, m_w_ada, m_b_ada, m_g_attn, m_w_in, m_b_fgate, m_g_out_fox, m_g_out_sb, m_w_out, m_g_mlp, m_w_up, m_conv_w, m_conv_b, m_w_down, m_g_final, v_w_ada, v_b_ada, v_g_attn, v_w_in, v_b_fgate, v_g_out_fox, v_g_out_sb, v_w_out, v_g_mlp, v_w_up, v_conv_w, v_conv_b, v_w_down, v_g_final):
    S, D = x.shape[1], x.shape[2]
    dh = D // 2
    n_heads = dh // HEAD_DIM
    n_qkv = 6 * dh
    ff = w_down.shape[1] * N_DEV
    ffp = -(-ff // (2 * LANES)) * (2 * LANES)
    nc = S // LANES
    me = 4 * lax.axis_index("x") + 2 * lax.axis_index("y") + lax.axis_index("c")
    xs, tgt = x[0], loss_target[0]

    c_all, win_g = _gather_two_level([c, w_in[0].astype(BF16)], name="gather_first")
    c_all = c_all.reshape(N_DEV, D)
    W_in = _ungather(win_g, 1)
    W_qkv, W_f = W_in[:, :n_qkv], _pad_cols(W_in[:, n_qkv:], LANES)
    W_inp = jnp.concatenate([W_qkv, W_f], axis=1)
    cb_g, cb_v = _pad_cols(conv_b[:, :ff], ffp), _pad_cols(conv_b[:, ff:], ffp)

    n_ada = w_ada.shape[2]
    b_shard = lax.dynamic_slice(b_ada, (0, me * n_ada), (1, n_ada))
    mod_cols = _ada_fwd(c_all, w_ada[0], b_shard)
    (mod_g,) = _exchange([mod_cols], scatter=False, name="gather_mod")
    mod = lax.dynamic_index_in_dim(mod_g, me, axis=1, keepdims=False).reshape(6, 1, D)
    shift_a, scale_a, gate_a, shift_m, scale_m, gate_m = [mod[k] for k in range(6)]

    h1, h1_t = _prenorm(xs, g_attn, scale_a, shift_a, "prenorm_attn")
    qkv = _mm(h1, W_qkv, BF16, "proj_qkv")
    flog = _mm(h1, W_f, F32, "proj_fgate")
    zf = flog[:, :n_heads] + b_fgate
    z_rows = zf.T.reshape(n_heads * nc, LANES)
    f_rows = _fgate_fwd(z_rows, nc).reshape(n_heads, S)
    fox = _fox_operands(qkv[:, 0:dh], qkv[:, dh:2 * dh], qkv[:, 2 * dh:3 * dh], f_rows)
    sb = _sb_operands(qkv[:, 3 * dh:4 * dh], qkv[:, 4 * dh:5 * dh], qkv[:, 5 * dh:6 * dh])
    of_t, lse, (wout_g, wup_g, wdown_g, convw_g) = _fox_fwd(
        fox["q_t"], fox["k_n"], fox["v_t"], fox["f_end"], fox["k_max"],
        [w_out[0].astype(BF16), w_up[0].astype(BF16), w_down[0].astype(BF16), conv_w[0]])
    W_out = _ungather(wout_g, 0)
    W_up = _ungather(wup_g, 1)
    W_g, W_v = _pad_cols(W_up[:, :ff], ffp), _pad_cols(W_up[:, ff:], ffp)
    W_down = jnp.pad(_ungather(wdown_g, 0), ((0, ffp - ff), (0, 0)))
    cw_full = _ungather(convw_g, 1)
    cw_g, cw_v = _pad_cols(cw_full[:, :ff], ffp), _pad_cols(cw_full[:, ff:], ffp)
    os_t = _sb_fwd(sb["q_t"], sb["k_n"], sb["v_t"])
    o_f, o_s = _unheads(_tr(of_t)), _unheads(_tr(os_t))
    mix, mix_t = _headnorm_fwd(o_f, o_s, g_out_fox, g_out_sb)
    a_out = _mm(mix, W_out, F32, "proj_out")
    x1, h2, h2_t = _resid_prenorm(xs, a_out, gate_a, g_mlp, scale_m, shift_m)
    up_g, up_v, act, act_t = _mlp_up(h2, W_g, W_v, cw_g, cw_v, cb_g, cb_v)
    m_out = _mm(act, W_down, F32, "proj_down")

    dx2, gm, loss_p, dg_final, dgate_m = _loss_head(x1, m_out, gate_m, g_final.reshape(1, D), tgt)
    dact = _mm(gm, W_down.T, BF16, "bwd_down_act")
    dW_down = _mm_acc(act_t, gm, "bwd_down_w")
    du_g, du_v, p_g, p_v = _conv_act_bwd(dact, up_g, up_v, cw_g, cw_v, cb_g, cb_v)
    dup_g = _conv_bwd_input(du_g, cw_g, "conv_bwd_input_g")
    dup_v = _conv_bwd_input(du_v, cw_v, "conv_bwd_input_v")
    dh2 = _mm2(dup_g, W_g.T, dup_v, W_v.T, F32, "bwd_up_act")
    dW_g = _mm_acc(h2_t, dup_g, "bwd_up_w_g")
    dW_v = _mm_acc(h2_t, dup_v, "bwd_up_w_v")
    dx1, dscale_m, dshift_m, dg_mlp, ga, dgate_a = _norm_bwd(
        dh2, x1, dx2, g_mlp, scale_m, "norm_mlp_bwd", gate=gate_a, branch=a_out)
    dmix = _mm(ga, W_out.T, F32, "bwd_out_act")
    dW_out = _mm_acc(mix_t, ga, "bwd_out_w")
    do_f, do_s, dg_fox, dg_sb = _headnorm_bwd(dmix, o_f, o_s, g_out_fox, g_out_sb)
    dof_n, dof_t = _do_operands(do_f)
    dos_n, dos_t = _do_operands(do_s)
    dqf_t, dkf, dvf = _fox_bwd(fox["q_t"], fox["q_n"], fox["k_n"], fox["k_t"], fox["v_n"], dof_t, dof_n, of_t, lse,
                              fox["f_end"], fox["k_max"])
    dW_upf = jnp.concatenate([dW_g[:, :ff], dW_v[:, :ff]], axis=1)
    dcw = jnp.concatenate([p_g[:CONV_W, :ff], p_v[:CONV_W, :ff]], axis=1)
    dqs_t, dks, dvs, (s_out, s_up, s_down, s_cw) = _sb_bwd(
        sb["q_t"], sb["q_n"], sb["k_n"], sb["k_t"], sb["v_n"], dos_t, dos_n, os_t,
        [_to_slots(dW_out, 0), _to_slots(dW_upf, 1), _to_slots(dW_down[:ff], 0), _to_slots(dcw, 1)])
    dz_rows, db_fgate = _fgate_bwd(dkf[:, :, Q_ONE_LANE].reshape(n_heads * nc, LANES),
                                   dqf_t[:, Q_F_LANE, :].reshape(n_heads * nc, LANES), z_rows, nc)
    dzf = dz_rows.reshape(n_heads, S).T

    def dq_cols(dq_t):
        return (_unheads(_tr(dq_t[:, :HEAD_DIM, :])) * _SCALE).astype(BF16)

    def dkv_cols(d):
        return _unheads(d[:, :, :HEAD_DIM]).astype(BF16)

    dproj = jnp.concatenate(
        [dq_cols(dqf_t), dkv_cols(dkf), dkv_cols(dvf), dq_cols(dqs_t), dkv_cols(dks), dkv_cols(dvs),
         _pad_cols(dzf, LANES).astype(BF16)], axis=1)
    dh1 = _mm(dproj, W_inp.T, F32, "bwd_in_act")
    dW_inp = _mm_acc(h1_t, dproj, "bwd_in_w")
    grad_x, dscale_a, dshift_a, dg_attn = _norm_bwd(dh1, xs, dx1, g_attn, scale_a, "norm_attn_bwd")

    dconv_b = jnp.concatenate([p_g[CONV_W:CONV_W + 1, :ff], p_v[CONV_W:CONV_W + 1, :ff]], axis=1)
    parts = [dshift_a, dscale_a, dgate_a, dshift_m, dscale_m, dgate_m,
             dg_attn, db_fgate.reshape(1, n_heads), dg_fox, dg_sb, dg_mlp, dconv_b, dg_final,
             loss_p[:, :1]]
    sizes = [p.shape[1] for p in parts]
    vec = jnp.concatenate(parts, axis=1)
    n_vec = -(-vec.shape[1] // LANES) * LANES
    vec = _pad_cols(vec, n_vec)
    (vec_g,) = _exchange([vec], scatter=False, name="gather_small")
    offs = [0]
    for s in sizes:
        offs.append(offs[-1] + s)

    def small(k0, k1=None):
        k1 = k0 if k1 is None else k1
        return vec_g[:, :, offs[k0]:offs[k1 + 1]]

    dmod_all = small(0, 5).reshape(N_DEV, 6 * D)
    dmod_cols = lax.dynamic_slice(dmod_all, (0, me * n_ada), (N_DEV, n_ada))
    dW_ada = _ada_bwd(c_all.T, dmod_cols)

    dW_in = jnp.concatenate([dW_inp[:, :n_qkv], dW_inp[:, n_qkv:n_qkv + n_heads]], axis=1)
    bound_in = _to_slots(dW_in, 1)
    bound_in = bound_in.reshape((N_CHIPS, 2) + bound_in.shape[1:])
    (got_in,) = _scatter_to_sibling([bound_in], "scatter_sibling")
    c_idx = lax.axis_index("c").astype(jnp.int32).reshape(1)
    (s_in,) = _scatter_to_chips([_pair_add(bound_in, got_in, c_idx, "pair_add")], "scatter_chips")

    res = {}
    res["w_ada"] = _adamw(w_ada[0], dW_ada[None], m_w_ada[0], v_w_ada[0], "adamw_w_ada")
    res["w_in"] = _adamw(w_in[0], s_in, m_w_in[0], v_w_in[0], "adamw_w_in")
    res["w_out"] = _adamw(w_out[0], s_out, m_w_out[0], v_w_out[0], "adamw_w_out")
    res["w_up"] = _adamw(w_up[0], s_up, m_w_up[0], v_w_up[0], "adamw_w_up")
    res["w_down"] = _adamw(w_down[0], s_down, m_w_down[0], v_w_down[0], "adamw_w_down")
    res["conv_w"] = _adamw(conv_w[0], s_cw, m_conv_w[0], v_conv_w[0], "adamw_conv_w")
    res["b_ada"] = _adamw(b_ada, small(0, 5), m_b_ada, v_b_ada, "adamw_b_ada")
    res["g_attn"] = _adamw(g_attn, small(6), m_g_attn, v_g_attn, "adamw_g_attn")
    res["b_fgate"] = _adamw(b_fgate, small(7), m_b_fgate, v_b_fgate, "adamw_b_fgate")
    res["g_out_fox"] = _adamw(g_out_fox, small(8), m_g_out_fox, v_g_out_fox, "adamw_g_out_fox")
    res["g_out_sb"] = _adamw(g_out_sb, small(9), m_g_out_sb, v_g_out_sb, "adamw_g_out_sb")
    res["g_mlp"] = _adamw(g_mlp, small(10), m_g_mlp, v_g_mlp, "adamw_g_mlp")
    res["conv_b"] = _adamw(conv_b, small(11), m_conv_b, v_conv_b, "adamw_conv_b")
    res["g_final"] = _adamw(g_final.reshape(1, D), small(12), m_g_final.reshape(1, D),
                            v_g_final.reshape(1, D), "adamw_g_final")
    loss = _slot_sum(_pad_cols(small(13).reshape(N_DEV, 1), LANES).reshape(N_DEV, 1, LANES), "loss_sum")[0, 0]

    names = ["w_ada", "b_ada", "g_attn", "w_in", "b_fgate", "g_out_fox", "g_out_sb", "w_out", "g_mlp",
             "w_up", "conv_w", "conv_b", "w_down", "g_final"]

    def shaped(n, a):
        if n == "g_final":
            return a.reshape(D)
        if n in ("b_ada", "g_attn", "b_fgate", "g_out_fox", "g_out_sb", "g_mlp", "conv_b"):
            return a
        return a[None]

    outs = [loss, grad_x[None]]
    for k in range(4):
        outs += [shaped(n, res[n][k]) for n in names]
    return tuple(outs)
```

```python
import jax
import jax.numpy as jnp
from jax import lax
from jax.experimental import pallas as pl
from jax.experimental.pallas import tpu as pltpu

F32 = jnp.float32
BF16 = jnp.bfloat16
HIGHEST = lax.Precision.HIGHEST

N_DEV = 8
LANES = 128
HEAD_DIM = 64
EPS = 1e-6
CONV_W = 3
CONV_COLS = 1408
HALO = 16
ATT_BQ = 512
ATT_BK = 512
SCAN_BK = 128
VMEM_LIMIT = 56 * 1024 * 1024

ADAM_LR = 0.001
ADAM_B1 = 0.9
ADAM_B2 = 0.999
ADAM_EPS = 1e-08
ADAM_WD = 0.01
ADAM_STEP = 10


def _params(**kw):
    return pltpu.CompilerParams(vmem_limit_bytes=VMEM_LIMIT, **kw)


def _tile(n, cap):
    if n <= cap:
        return n
    best = None
    for t in range(LANES, cap + 1, LANES):
        if n % t == 0:
            best = t
    assert best is not None, (n, cap)
    return best


def _dot(a, b, **kw):
    return jnp.dot(a, b, preferred_element_type=F32, **kw)


def _exchange_copies(ins, outs, send_sems, recv_sems, loc_sems, scatter):
    n = len(ins)
    if n == 0:
        return []
    x, y, c = lax.axis_index("x"), lax.axis_index("y"), lax.axis_index("c")
    me = 4 * x + 2 * y + c
    copies = []
    for a in range(n):
        src = ins[a].at[me] if scatter else ins[a]
        copies.append(pltpu.make_async_copy(src, outs[a].at[me], loc_sems.at[a]))
    for k in range(1, N_DEV):
        px = 1 - x if k & 4 else x
        py = 1 - y if k & 2 else y
        pc = 1 - c if k & 1 else c
        peer = 4 * px + 2 * py + pc
        for a in range(n):
            src = ins[a].at[peer] if scatter else ins[a]
            copies.append(pltpu.make_async_remote_copy(
                src_ref=src, dst_ref=outs[a].at[me],
                send_sem=send_sems.at[a, k - 1], recv_sem=recv_sems.at[a, k - 1],
                device_id=(px, py, pc), device_id_type=pl.DeviceIdType.MESH))
    return copies


def _exchange_out_shapes(arrays, scatter):
    return [jax.ShapeDtypeStruct((N_DEV,) + tuple(a.shape[1:] if scatter else a.shape), a.dtype) for a in arrays]


def _exchange_sems(n):
    return [pltpu.SemaphoreType.DMA((n, N_DEV - 1)), pltpu.SemaphoreType.DMA((n, N_DEV - 1)),
            pltpu.SemaphoreType.DMA((n,))]


def _exchange(arrays, scatter, name):
    n = len(arrays)

    def body(*refs):
        copies = _exchange_copies(refs[:n], refs[n:2 * n], *refs[2 * n:], scatter)
        for cp in copies:
            cp.start()
        for cp in copies:
            cp.wait()

    any_spec = pl.BlockSpec(memory_space=pl.ANY)
    return pl.pallas_call(
        body, name=name, out_shape=tuple(_exchange_out_shapes(arrays, scatter)),
        in_specs=[any_spec] * n, out_specs=tuple([any_spec] * n),
        scratch_shapes=_exchange_sems(n),
        compiler_params=pltpu.CompilerParams(has_side_effects=True),
    )(*arrays)


def _gather_two_level(arrays, name):
    n = len(arrays)
    out_shape = [jax.ShapeDtypeStruct((N_DEV,) + tuple(a.shape), a.dtype) for a in arrays]

    def body(*refs):
        ins, outs = refs[:n], refs[n:2 * n]
        send_sems, recv_sems, loc_sems = refs[2 * n:]
        x, y, c = lax.axis_index("x"), lax.axis_index("y"), lax.axis_index("c")
        me, sibling = (x, y, c), (x, y, 1 - c)
        chips = [(1 - x, y), (x, 1 - y), (1 - x, 1 - y)]

        def slot(px, py, pc):
            return 4 * px + 2 * py + pc

        def copy(a, k, block, to, src=None):
            dst = outs[a].at[slot(*block)]
            return pltpu.make_async_remote_copy(
                src_ref=dst if src is None else src, dst_ref=dst,
                send_sem=send_sems.at[a, k], recv_sem=recv_sems.at[a, k],
                device_id=to, device_id_type=pl.DeviceIdType.MESH)

        local = [pltpu.make_async_copy(ins[a], outs[a].at[slot(*me)], loc_sems.at[a]) for a in range(n)]
        for cp in local:
            cp.start()
        first = []
        for a in range(n):
            first.append(copy(a, 0, me, sibling, src=ins[a]))
            first += [copy(a, 1 + j, me, (*chip, c), src=ins[a]) for j, chip in enumerate(chips)]
        for cp in first:
            cp.start()
        passed = []
        for j, chip in enumerate(chips):
            for a in range(n):
                copy(a, 1 + j, (*chip, c), me).wait_recv()
                cp = copy(a, 4 + j, (*chip, c), sibling)
                cp.start()
                passed.append(cp)
        for a in range(n):
            copy(a, 0, sibling, me).wait_recv()
            for j, chip in enumerate(chips):
                copy(a, 4 + j, (*chip, 1 - c), me).wait_recv()
        for cp in first + passed:
            cp.wait_send()
        for cp in local:
            cp.wait()

    any_spec = pl.BlockSpec(memory_space=pl.ANY)
    return pl.pallas_call(
        body, name=name, out_shape=tuple(out_shape),
        in_specs=[any_spec] * n, out_specs=tuple([any_spec] * n),
        scratch_shapes=[pltpu.SemaphoreType.DMA((n, N_DEV - 1)), pltpu.SemaphoreType.DMA((n, N_DEV - 1)),
                        pltpu.SemaphoreType.DMA((n,))],
        compiler_params=pltpu.CompilerParams(has_side_effects=True),
    )(*arrays)


N_CHIPS = 4


def _scatter_to_sibling(arrays, name):
    n = len(arrays)
    out_shape = [jax.ShapeDtypeStruct((N_CHIPS,) + tuple(a.shape[2:]), a.dtype) for a in arrays]

    def body(*refs):
        ins, outs = refs[:n], refs[n:2 * n]
        send_sems, recv_sems = refs[2 * n:]
        x, y, c = lax.axis_index("x"), lax.axis_index("y"), lax.axis_index("c")
        copies = []
        for a in range(n):
            for q in range(N_CHIPS):
                cp = pltpu.make_async_remote_copy(
                    src_ref=ins[a].at[q, 1 - c], dst_ref=outs[a].at[q],
                    send_sem=send_sems.at[a, q], recv_sem=recv_sems.at[a, q],
                    device_id=(x, y, 1 - c), device_id_type=pl.DeviceIdType.MESH)
                cp.start()
                copies.append(cp)
        for cp in copies:
            cp.wait()

    any_spec = pl.BlockSpec(memory_space=pl.ANY)
    return pl.pallas_call(
        body, name=name, out_shape=tuple(out_shape),
        in_specs=[any_spec] * n, out_specs=tuple([any_spec] * n),
        scratch_shapes=[pltpu.SemaphoreType.DMA((n, N_CHIPS)), pltpu.SemaphoreType.DMA((n, N_CHIPS))],
        compiler_params=pltpu.CompilerParams(has_side_effects=True),
    )(*arrays)


def _pair_add(mine, got, c_idx, name):
    _, _, R, C = mine.shape
    tr = 256 if (R % 256 == 0 and R > 256) else R

    def body(c_ref, m_ref, g_ref, o_ref):
        o_ref[...] = m_ref[...] + g_ref[...]

    grid_spec = pltpu.PrefetchScalarGridSpec(
        num_scalar_prefetch=1, grid=(N_CHIPS, R // tr),
        in_specs=[pl.BlockSpec((None, None, tr, C), lambda q, i, c_ref: (q, c_ref[0], i, 0)),
                  pl.BlockSpec((None, tr, C), lambda q, i, c_ref: (q, i, 0))],
        out_specs=pl.BlockSpec((None, tr, C), lambda q, i, c_ref: (q, i, 0)))
    return pl.pallas_call(
        body, name=name, grid_spec=grid_spec, out_shape=jax.ShapeDtypeStruct((N_CHIPS, R, C), mine.dtype),
        compiler_params=_params(dimension_semantics=("parallel", "parallel")),
    )(c_idx, mine, got)


def _scatter_to_chips(arrays, name):
    n = len(arrays)
    out_shape = [jax.ShapeDtypeStruct(a.shape, a.dtype) for a in arrays]

    def body(*refs):
        ins, outs = refs[:n], refs[n:2 * n]
        send_sems, recv_sems, loc_sems = refs[2 * n:]
        x, y, c = lax.axis_index("x"), lax.axis_index("y"), lax.axis_index("c")
        myq = 2 * x + y
        copies = []
        for a in range(n):
            cp = pltpu.make_async_copy(ins[a].at[myq], outs[a].at[myq], loc_sems.at[a])
            cp.start()
            copies.append(cp)
        for k in range(1, N_CHIPS):
            qx = 1 - x if k & 2 else x
            qy = 1 - y if k & 1 else y
            for a in range(n):
                cp = pltpu.make_async_remote_copy(
                    src_ref=ins[a].at[2 * qx + qy], dst_ref=outs[a].at[myq],
                    send_sem=send_sems.at[a, k - 1], recv_sem=recv_sems.at[a, k - 1],
                    device_id=(qx, qy, c), device_id_type=pl.DeviceIdType.MESH)
                cp.start()
                copies.append(cp)
        for cp in copies:
            cp.wait()

    any_spec = pl.BlockSpec(memory_space=pl.ANY)
    return pl.pallas_call(
        body, name=name, out_shape=tuple(out_shape),
        in_specs=[any_spec] * n, out_specs=tuple([any_spec] * n),
        scratch_shapes=[pltpu.SemaphoreType.DMA((n, N_CHIPS - 1)), pltpu.SemaphoreType.DMA((n, N_CHIPS - 1)),
                        pltpu.SemaphoreType.DMA((n,))],
        compiler_params=pltpu.CompilerParams(has_side_effects=True),
    )(*arrays)


def _dot_nt(a, b):
    return lax.dot_general(a, b, (((1,), (1,)), ((), ())), preferred_element_type=F32)


def _rhs_spec(b, tn, nt):
    if nt:
        return pl.BlockSpec((tn, b.shape[1]), lambda i, j: (j, 0))
    return pl.BlockSpec((b.shape[0], tn), lambda i, j: (0, j))


def _mm(a, b, out_dtype, name, tm=1024, tn=512, nt=False):
    M, K = a.shape
    N = b.shape[0] if nt else b.shape[1]
    tm, tn = _tile(M, tm), _tile(N, tn)
    dot = _dot_nt if nt else _dot

    def body(a_ref, b_ref, o_ref):
        o_ref[...] = dot(a_ref[...], b_ref[...]).astype(out_dtype)

    return pl.pallas_call(
        body, name=name, out_shape=jax.ShapeDtypeStruct((M, N), out_dtype),
        grid=(M // tm, N // tn),
        in_specs=[pl.BlockSpec((tm, K), lambda i, j: (i, 0)), _rhs_spec(b, tn, nt)],
        out_specs=pl.BlockSpec((tm, tn), lambda i, j: (i, j)),
        compiler_params=_params(dimension_semantics=("parallel", "parallel")),
    )(a, b)


def _mm2(a1, b1, a2, b2, out_dtype, name, tm=1024, tn=512, nt=False):
    M, K1 = a1.shape
    _, K2 = a2.shape
    N = b1.shape[0] if nt else b1.shape[1]
    tm, tn = _tile(M, tm), _tile(N, tn)
    dot = _dot_nt if nt else _dot

    def body(a1_ref, b1_ref, a2_ref, b2_ref, o_ref):
        o_ref[...] = (dot(a1_ref[...], b1_ref[...]) + dot(a2_ref[...], b2_ref[...])).astype(out_dtype)

    return pl.pallas_call(
        body, name=name, out_shape=jax.ShapeDtypeStruct((M, N), out_dtype),
        grid=(M // tm, N // tn),
        in_specs=[pl.BlockSpec((tm, K1), lambda i, j: (i, 0)), _rhs_spec(b1, tn, nt),
                  pl.BlockSpec((tm, K2), lambda i, j: (i, 0)), _rhs_spec(b2, tn, nt)],
        out_specs=pl.BlockSpec((tm, tn), lambda i, j: (i, j)),
        compiler_params=_params(dimension_semantics=("parallel", "parallel")),
    )(a1, b1, a2, b2)


def _mm_acc(a, b, name, tm=1408, tn=1408, tk=512):
    M, S = a.shape
    _, N = b.shape
    tm, tn, tk = _tile(M, tm), _tile(N, tn), _tile(S, tk)

    def body(a_ref, b_ref, o_ref):
        @pl.when(pl.program_id(2) == 0)
        def _():
            o_ref[...] = jnp.zeros_like(o_ref)

        o_ref[...] += _dot(a_ref[...], b_ref[...])

    return pl.pallas_call(
        body, name=name, out_shape=jax.ShapeDtypeStruct((M, N), F32),
        grid=(M // tm, N // tn, S // tk),
        in_specs=[pl.BlockSpec((tm, tk), lambda i, j, k: (i, k)), pl.BlockSpec((tk, tn), lambda i, j, k: (k, j))],
        out_specs=pl.BlockSpec((tm, tn), lambda i, j, k: (i, j)),
        compiler_params=_params(dimension_semantics=("parallel", "parallel", "arbitrary")),
    )(a, b)


def _silu(z):
    return z * (1.0 / (1.0 + jnp.exp(-z)))


def _ada_fwd(c_all, w_shard, b_shard):
    n = w_shard.shape[1]

    def body(c_ref, w_ref, b_ref, o_ref):
        o_ref[...] = _dot(_silu(c_ref[...]), w_ref[...], precision=HIGHEST) + b_ref[...]

    return pl.pallas_call(body, name="ada_fwd", out_shape=jax.ShapeDtypeStruct((N_DEV, n), F32),
                          compiler_params=_params())(c_all, w_shard, b_shard)


def _ada_bwd(c_all_t, dmod_cols):
    D = c_all_t.shape[0]
    n = dmod_cols.shape[1]

    def body(ct_ref, dm_ref, o_ref):
        sc = _silu(ct_ref[...])
        dm = dm_ref[...]
        acc = sc[:, 0:1] * dm[0:1, :]
        for b in range(1, N_DEV):
            acc = acc + sc[:, b:b + 1] * dm[b:b + 1, :]
        o_ref[...] = acc

    return pl.pallas_call(body, name="ada_bwd", out_shape=jax.ShapeDtypeStruct((D, n), F32),
                          compiler_params=_params())(c_all_t, dmod_cols)


def _row_specs(tm, widths):
    return [pl.BlockSpec((tm, w), lambda i: (i, 0)) for w in widths]


def _vec_spec(w):
    return pl.BlockSpec((1, w), lambda i: (0, 0))


def _col_spec(tm, w):
    return pl.BlockSpec((w, tm), lambda i: (0, i))


def _prenorm(x, g, scale, shift, name):
    S, D = x.shape
    tm = _tile(S, 512)

    def body(x_ref, g_ref, sc_ref, sh_ref, h_ref, ht_ref):
        xv = x_ref[...]
        r = lax.rsqrt(jnp.mean(xv * xv, axis=-1, keepdims=True) + EPS)
        h = (xv * r) * g_ref[...] * (1.0 + sc_ref[...]) + sh_ref[...]
        h_ref[...] = h.astype(BF16)
        ht_ref[...] = h.T.astype(BF16)

    return pl.pallas_call(
        body, name=name, grid=(S // tm,),
        out_shape=(jax.ShapeDtypeStruct((S, D), BF16), jax.ShapeDtypeStruct((D, S), BF16)),
        in_specs=_row_specs(tm, [D]) + [_vec_spec(D)] * 3,
        out_specs=(_row_specs(tm, [D])[0], _col_spec(tm, D)),
        compiler_params=_params(dimension_semantics=("parallel",)),
    )(x, g, scale, shift)


def _group_ones():
    r = lax.broadcasted_iota(jnp.int32, (LANES, LANES), 0) // HEAD_DIM
    c = lax.broadcasted_iota(jnp.int32, (LANES, LANES), 1) // HEAD_DIM
    return (r == c).astype(F32)


def _headnorm_fwd(o_f, o_s, g_f, g_s):
    S, dh = o_f.shape
    tm = _tile(S, 512)

    def body(of_ref, os_ref, gf_ref, gs_ref, mix_ref, mixt_ref):
        ones = _group_ones()
        for part, (o_ref, g_ref) in enumerate(((of_ref, gf_ref), (os_ref, gs_ref))):
            for t in range(dh // LANES):
                cols = slice(t * LANES, (t + 1) * LANES)
                out = slice(part * dh + t * LANES, part * dh + (t + 1) * LANES)
                o = o_ref[:, cols]
                ms = _dot(o * o, ones, precision=HIGHEST) * (1.0 / HEAD_DIM)
                mixn = o * lax.rsqrt(ms + EPS) * g_ref[:, cols]
                mix_ref[:, out] = mixn.astype(BF16)
                mixt_ref[out, :] = mixn.T.astype(BF16)

    return pl.pallas_call(
        body, name="headnorm_fwd", grid=(S // tm,),
        out_shape=(jax.ShapeDtypeStruct((S, 2 * dh), BF16), jax.ShapeDtypeStruct((2 * dh, S), BF16)),
        in_specs=_row_specs(tm, [dh, dh]) + [_vec_spec(dh)] * 2,
        out_specs=(_row_specs(tm, [2 * dh])[0], _col_spec(tm, 2 * dh)),
        compiler_params=_params(dimension_semantics=("parallel",)),
    )(o_f, o_s, g_f, g_s)


def _resid_prenorm(x, a_out, gate, g, scale, shift):
    S, D = x.shape
    tm = _tile(S, 512)

    def body(x_ref, a_ref, gt_ref, g_ref, sc_ref, sh_ref, x1_ref, h_ref, ht_ref):
        x1 = x_ref[...] + gt_ref[...] * a_ref[...]
        x1_ref[...] = x1
        r = lax.rsqrt(jnp.mean(x1 * x1, axis=-1, keepdims=True) + EPS)
        h = (x1 * r) * g_ref[...] * (1.0 + sc_ref[...]) + sh_ref[...]
        h_ref[...] = h.astype(BF16)
        ht_ref[...] = h.T.astype(BF16)

    return pl.pallas_call(
        body, name="resid_prenorm", grid=(S // tm,),
        out_shape=(jax.ShapeDtypeStruct((S, D), F32), jax.ShapeDtypeStruct((S, D), BF16),
                   jax.ShapeDtypeStruct((D, S), BF16)),
        in_specs=_row_specs(tm, [D, D]) + [_vec_spec(D)] * 4,
        out_specs=tuple(_row_specs(tm, [D, D]) + [_col_spec(tm, D)]),
        compiler_params=_params(dimension_semantics=("parallel",)),
    )(x, a_out, gate, g, scale, shift)


def _shift_down(main, halo, k):
    ext = jnp.concatenate([halo, main], axis=0)
    return pltpu.roll(ext, k, 0)[halo.shape[0]:]


def _shift_up(main, halo, k):
    ext = jnp.concatenate([main, halo], axis=0)
    n = ext.shape[0]
    return pltpu.roll(ext, n - k, 0)[:main.shape[0]]


def _conv(up, up_halo, w_ref, b_ref):
    return (w_ref[2:3, :] * up + w_ref[1:2, :] * _shift_down(up, up_halo, 1)
            + w_ref[0:1, :] * _shift_down(up, up_halo, 2) + b_ref[...])


def _prev_halo_map(tm):
    step = tm // HALO
    return lambda j, i: (jnp.maximum(i * step - 1, 0), j)


MLP_TM = 512
MLP_CT = 256
CARRY = 8


def _mlp_up(h, wg, wv, cwg, cwv, cbg, cbv):
    S, D = h.shape
    F = wg.shape[1]
    tm, ct = _tile(S, MLP_TM), _tile(F, MLP_CT)
    nct = F // ct

    def body(h_ref, wg_ref, wv_ref, cwg_ref, cwv_ref, cbg_ref, cbv_ref,
             upg_ref, upv_ref, act_ref, actt_ref, hg_scr, hv_scr):
        i, j = pl.program_id(0), pl.program_id(1)
        hv = h_ref[...]
        us = []
        for w_ref, cw_ref, cb_ref, up_ref, scr in ((wg_ref, cwg_ref, cbg_ref, upg_ref, hg_scr),
                                                   (wv_ref, cwv_ref, cbv_ref, upv_ref, hv_scr)):
            up = _dot(hv, w_ref[...]).astype(BF16)
            up_ref[...] = up
            upf = up.astype(F32)
            halo = jnp.where(i == 0, 0.0, scr[j])
            us.append(_conv(upf, halo, cw_ref, cb_ref))
            scr[j] = upf[tm - CARRY:, :]
        act = _silu(us[0]) * us[1]
        act_ref[...] = act.astype(BF16)
        actt_ref[...] = act.T.astype(BF16)

    blk = pl.BlockSpec((tm, ct), lambda i, j: (i, j))
    wspec = pl.BlockSpec((D, ct), lambda i, j: (0, j))
    cwspec = pl.BlockSpec((CONV_W, ct), lambda i, j: (0, j))
    cbspec = pl.BlockSpec((1, ct), lambda i, j: (0, j))
    sds = jax.ShapeDtypeStruct((S, F), BF16)
    return pl.pallas_call(
        body, name="mlp_up", grid=(S // tm, nct),
        out_shape=(sds, sds, sds, jax.ShapeDtypeStruct((F, S), BF16)),
        in_specs=[pl.BlockSpec((tm, D), lambda i, j: (i, 0)), wspec, wspec, cwspec, cwspec, cbspec, cbspec],
        out_specs=(blk, blk, blk, pl.BlockSpec((ct, tm), lambda i, j: (j, i))),
        scratch_shapes=[pltpu.VMEM((nct, CARRY, ct), F32), pltpu.VMEM((nct, CARRY, ct), F32)],
        compiler_params=_params(dimension_semantics=("arbitrary", "arbitrary")),
    )(h, wg, wv, cwg, cwv, cbg, cbv)


def _conv_act_bwd(dact, up_g, up_v, cwg, cwv, cbg, cbv):
    S, F = up_g.shape
    tm, ct = _tile(S, 256), _tile(F, CONV_COLS)
    nct = F // ct

    def body(da_ref, ug_ref, uv_ref, hg_ref, hv_ref, wg_ref, wv_ref, bg_ref, bv_ref,
             dug_ref, duv_ref, pg_ref, pv_ref):
        first = pl.program_id(1) == 0

        @pl.when(first)
        def _():
            pg_ref[...] = jnp.zeros_like(pg_ref)
            pv_ref[...] = jnp.zeros_like(pv_ref)

        da = da_ref[...].astype(F32)
        taps = []
        for u_ref, h_ref in ((ug_ref, hg_ref), (uv_ref, hv_ref)):
            h = jnp.where(first, 0.0, h_ref[...].astype(F32))
            uu = u_ref[...].astype(F32)
            taps.append((_shift_down(uu, h, 2), _shift_down(uu, h, 1), uu))
        u_g = wg_ref[0:1, :] * taps[0][0] + wg_ref[1:2, :] * taps[0][1] + wg_ref[2:3, :] * taps[0][2] + bg_ref[...]
        u_v = wv_ref[0:1, :] * taps[1][0] + wv_ref[1:2, :] * taps[1][1] + wv_ref[2:3, :] * taps[1][2] + bv_ref[...]
        sg = 1.0 / (1.0 + jnp.exp(-u_g))
        du_g = da * u_v * (sg * (1.0 + u_g * (1.0 - sg)))
        du_v = da * (u_g * sg)
        dug_ref[...] = du_g.astype(BF16)
        duv_ref[...] = du_v.astype(BF16)
        for du, tp, p_ref in ((du_g, taps[0], pg_ref), (du_v, taps[1], pv_ref)):
            for k in range(CONV_W):
                p_ref[k:k + 1, :] += jnp.sum(du * tp[k], axis=0, keepdims=True)
            p_ref[CONV_W:CONV_W + 1, :] += jnp.sum(du, axis=0, keepdims=True)

    main = pl.BlockSpec((tm, ct), lambda j, i: (i, j))
    halo = pl.BlockSpec((HALO, ct), _prev_halo_map(tm))
    wspec = pl.BlockSpec((CONV_W, ct), lambda j, i: (0, j))
    bspec = pl.BlockSpec((1, ct), lambda j, i: (0, j))
    pspec = pl.BlockSpec((8, ct), lambda j, i: (0, j))
    return pl.pallas_call(
        body, name="conv_act_bwd", grid=(nct, S // tm),
        out_shape=(jax.ShapeDtypeStruct((S, F), BF16), jax.ShapeDtypeStruct((S, F), BF16),
                   jax.ShapeDtypeStruct((8, F), F32), jax.ShapeDtypeStruct((8, F), F32)),
        in_specs=[main, main, main, halo, halo, wspec, wspec, bspec, bspec],
        out_specs=(main, main, pspec, pspec),
        compiler_params=_params(dimension_semantics=("parallel", "arbitrary")),
    )(dact, up_g, up_v, up_g, up_v, cwg, cwv, cbg, cbv)


def _conv_bwd_input(du, cw, name):
    S, C = du.shape
    tm, ct = _tile(S, 256), _tile(C, CONV_COLS)
    step = tm // HALO
    last_halo = S // HALO - 1

    def body(du_ref, h_ref, w_ref, o_ref):
        last = pl.program_id(1) == pl.num_programs(1) - 1
        d = du_ref[...].astype(F32)
        h = jnp.where(last, 0.0, h_ref[...].astype(F32))
        o_ref[...] = (w_ref[2:3, :] * d + w_ref[1:2, :] * _shift_up(d, h, 1)
                      + w_ref[0:1, :] * _shift_up(d, h, 2)).astype(BF16)

    return pl.pallas_call(
        body, name=name, out_shape=jax.ShapeDtypeStruct((S, C), BF16), grid=(C // ct, S // tm),
        in_specs=[pl.BlockSpec((tm, ct), lambda j, i: (i, j)),
                  pl.BlockSpec((HALO, ct), lambda j, i: (jnp.minimum((i + 1) * step, last_halo), j)),
                  pl.BlockSpec((CONV_W, ct), lambda j, i: (0, j))],
        out_specs=pl.BlockSpec((tm, ct), lambda j, i: (i, j)),
        compiler_params=_params(dimension_semantics=("parallel", "parallel")),
    )(du, du, cw)


def _scan_mats(R, nc, reverse):
    i = lax.broadcasted_iota(jnp.int32, (LANES, LANES), 0)
    j = lax.broadcasted_iota(jnp.int32, (LANES, LANES), 1)
    inner = ((i >= j) if reverse else (i <= j)).astype(F32)
    r = lax.broadcasted_iota(jnp.int32, (R, R), 0)
    c = lax.broadcasted_iota(jnp.int32, (R, R), 1)
    same = (r // nc) == (c // nc)
    outer = (same & ((c > r) if reverse else (c < r))).astype(F32)
    return inner, outer


def _chunk_scan(v, inner, outer, reverse):
    w = _dot(v, inner, precision=HIGHEST)
    col = 0 if reverse else LANES - 1
    carry = _dot(outer, w, precision=HIGHEST)[:, col:col + 1]
    return w + carry


def _fgate_fwd(z_rows, nc):
    R = z_rows.shape[0]

    def body(z_ref, f_ref):
        z = z_ref[...]
        logf = jnp.minimum(z, 0.0) - jnp.log(1.0 + jnp.exp(-jnp.abs(z)))
        inner, outer = _scan_mats(R, nc, False)
        f_ref[...] = _chunk_scan(logf, inner, outer, False)

    return pl.pallas_call(body, name="fgate_fwd", out_shape=jax.ShapeDtypeStruct((R, LANES), F32),
                          compiler_params=_params())(z_rows)


def _fgate_bwd(dfk_neg_rows, dfq_rows, z_rows, nc):
    R = z_rows.shape[0]
    nh = R // nc

    def body(dfk_ref, dfq_ref, z_ref, dz_ref, db_ref):
        inner, outer = _scan_mats(R, nc, True)
        dlogf = _chunk_scan(dfq_ref[...] - dfk_ref[...], inner, outer, True)
        dz = dlogf * (1.0 / (1.0 + jnp.exp(z_ref[...])))
        dz_ref[...] = dz
        hr = lax.broadcasted_iota(jnp.int32, (nh, R), 0)
        hc = lax.broadcasted_iota(jnp.int32, (nh, R), 1) // nc
        per_head = _dot((hr == hc).astype(F32), dz, precision=HIGHEST)
        db_ref[...] = jnp.sum(per_head, axis=1, keepdims=True)

    return pl.pallas_call(
        body, name="fgate_bwd",
        out_shape=(jax.ShapeDtypeStruct((R, LANES), F32), jax.ShapeDtypeStruct((nh, 1), F32)),
        compiler_params=_params())(dfk_neg_rows, dfq_rows, z_rows)


_NEG = -1e30
SKIP_BELOW = -106.0
_SCALE = HEAD_DIM ** -0.5
N_SCAN = ATT_BK // SCAN_BK
F_PARTS = 3
Q_F_LANE = HEAD_DIM
Q_ONE_LANE = HEAD_DIM + F_PARTS


def _kv_slice(j):
    return pl.ds(pl.multiple_of(j * ATT_BK, ATT_BK), ATT_BK)


def _mask_t(strict):
    s = lax.broadcasted_iota(jnp.int32, (ATT_BK, ATT_BQ), 0)
    t = lax.broadcasted_iota(jnp.int32, (ATT_BK, ATT_BQ), 1)
    return (s < t) if strict else (s <= t)


def _walk_down(i, step, alive, carry):
    carry = step(i, carry, True)

    def cond(st):
        n, go, _ = st
        return jnp.logical_and(n < i, go)

    def body(st):
        n, _, cr = st
        j = i - 1 - n
        cr = step(j, cr, False)
        return n + 1, alive(jnp.maximum(j - 1, 0), cr), cr

    return lax.while_loop(cond, body, (jnp.int32(0), alive(jnp.maximum(i - 1, 0), carry), carry))[2]


def _t_block(rows):
    return pl.BlockSpec((None, rows, ATT_BQ), lambda h, i, *_: (h, 0, i))


def _t_full(rows, S):
    return pl.BlockSpec((None, rows, S), lambda h, i, *_: (h, 0, 0))


def _n_block():
    return pl.BlockSpec((None, ATT_BQ, LANES), lambda h, i, *_: (h, i, 0))


def _n_full(S):
    return pl.BlockSpec((None, S, LANES), lambda h, i, *_: (h, 0, 0))


def _heads(t):
    S = t.shape[0]
    return jnp.transpose(t.reshape(S, -1, HEAD_DIM), (1, 0, 2))


def _unheads(t):
    return jnp.transpose(t, (1, 0, 2)).reshape(t.shape[1], -1)


def _tr(t):
    return jnp.transpose(t, (0, 2, 1))


def _skip_bounds(k_cols, f_rows):
    H, S = f_rows.shape
    f_end = f_rows.reshape(H, S // ATT_BK, ATT_BK)[:, :, -1]
    k_sq = jnp.sum(jnp.square(_heads(k_cols).astype(F32)), axis=-1).reshape(H, S // ATT_BK, ATT_BK)
    k_max = lax.cummax(jnp.sqrt(jnp.max(k_sq, axis=-1)), axis=1)
    return f_end, k_max


def _bf16_parts(f):
    hi = f.astype(BF16).astype(F32)
    mid = (f - hi).astype(BF16).astype(F32)
    return hi, mid, (f - hi - mid).astype(BF16).astype(F32)


def _att_prep(qkv, f_pairs):
    S = qkv.shape[0]
    n_pairs = qkv.shape[1] // (6 * LANES)
    H = 2 * n_pairs
    tm = _tile(S, 512)

    def body(qf_ref, kf_ref, vf_ref, qs_ref, ks_ref, vs_ref, f_ref,
             fqn, fqt, fkn, fkt, fvn, fvt, sqn, sqt, skn, skt, svn, svt):
        lane = lax.broadcasted_iota(jnp.int32, (1, LANES), 1)
        f = f_ref[...]

        def head(ref, e):
            t = ref[...].astype(F32)
            if e == 1:
                t = pltpu.roll(t, HEAD_DIM, 1)
            return jnp.where(lane < HEAD_DIM, t, 0.0)

        def at(first):
            return jnp.logical_and(lane >= first, lane < first + F_PARTS)

        for e in range(2):
            parts = _bf16_parts(f[:, e:e + 1])
            f_lanes = sum(jnp.where(lane == Q_F_LANE + k, parts[k], 0.0) for k in range(F_PARTS))
            nf_lanes = sum(jnp.where(lane == Q_ONE_LANE + k, parts[k], 0.0) for k in range(F_PARTS))
            vals = (
                (fqn, fqt, LANES, head(qf_ref, e) * _SCALE + f_lanes + jnp.where(at(Q_ONE_LANE), 1.0, 0.0)),
                (fkn, fkt, LANES, head(kf_ref, e) + jnp.where(at(Q_F_LANE), 1.0, 0.0) - nf_lanes),
                (fvn, fvt, HEAD_DIM, head(vf_ref, e)),
                (sqn, sqt, LANES, head(qs_ref, e) * _SCALE),
                (skn, skt, LANES, head(ks_ref, e)),
                (svn, svt, HEAD_DIM, head(vs_ref, e)),
            )
            for n_ref, t_ref, rows, val in vals:
                n_ref[e] = val.astype(BF16)
                t_ref[e] = val.T[:rows].astype(BF16)

    col = lambda base: pl.BlockSpec((tm, LANES), lambda i, p: (i, base + p))
    n_spec = pl.BlockSpec((2, tm, LANES), lambda i, p: (p, i, 0))
    t_spec = lambda rows: pl.BlockSpec((2, rows, tm), lambda i, p: (p, 0, i))
    n_sds = jax.ShapeDtypeStruct((H, S, LANES), BF16)
    t_sds = lambda rows: jax.ShapeDtypeStruct((H, rows, S), BF16)
    group = ([n_sds, t_sds(LANES), n_sds, t_sds(LANES), n_sds, t_sds(HEAD_DIM)],
             [n_spec, t_spec(LANES), n_spec, t_spec(LANES), n_spec, t_spec(HEAD_DIM)])
    res = pl.pallas_call(
        body, name="att_prep", grid=(S // tm, n_pairs),
        out_shape=tuple(group[0] * 2),
        in_specs=[col(k * n_pairs) for k in range(6)] + [pl.BlockSpec((None, tm, 2), lambda i, p: (p, i, 0))],
        out_specs=tuple(group[1] * 2),
        compiler_params=_params(dimension_semantics=("parallel", "parallel")),
    )(qkv, qkv, qkv, qkv, qkv, qkv, f_pairs)
    names = ("q_n", "q_t", "k_n", "k_t", "v_n", "v_t")
    return dict(zip(names, res[:6])), dict(zip(names, res[6:]))


def _fox_reach(qt, fend_ref, kmax_ref, h):
    qf = qt.astype(F32)
    q_norm = jnp.sqrt(jnp.sum(jnp.square(qf[:HEAD_DIM]), axis=0, keepdims=True))
    f_t = jnp.sum(qf[Q_F_LANE:Q_F_LANE + F_PARTS], axis=0, keepdims=True)
    return lambda j: q_norm * kmax_ref[h, j] + f_t - fend_ref[h, j]


def _fox_fwd(q_t, k_n, v_t, f_end, k_max, shards):
    H, _, S = q_t.shape
    n, nq = len(shards), S // ATT_BQ

    def body(fend_ref, kmax_ref, qt_ref, k_ref, vt_ref, *rest):
        ins, (ot_ref, lse_ref), outs, sems = rest[:n], rest[n:n + 2], rest[n + 2:2 * n + 2], rest[2 * n + 2:]
        h, i = pl.program_id(0), pl.program_id(1)

        @pl.when(jnp.logical_and(h == 0, i == 0))
        def _():
            for cp in _exchange_copies(ins, outs, *sems, False):
                cp.start()

        qt = qt_ref[...]
        reach = _fox_reach(qt, fend_ref, kmax_ref, h)

        def step(j, carry, masked):
            m, l, acc = carry
            ks = _kv_slice(j)
            s = _dot(k_ref[ks, :], qt)
            if masked:
                s = jnp.where(_mask_t(False), s, _NEG)
            mn = jnp.maximum(m, jnp.max(s, axis=0, keepdims=True))
            alpha = jnp.exp(m - mn)
            p = jnp.exp(s - mn)
            l = alpha * l + jnp.sum(p, axis=0, keepdims=True)
            acc = acc * alpha + _dot(vt_ref[:, ks], p.astype(BF16))
            return mn, l, acc

        def alive(j, carry):
            return jnp.max(reach(j) - carry[0]) > SKIP_BELOW

        row = jnp.zeros((1, ATT_BQ), F32)
        m, l, acc = _walk_down(i, step, alive, (row + _NEG, row, jnp.zeros((HEAD_DIM, ATT_BQ), F32)))
        ot_ref[...] = acc / l
        lse_ref[...] = m + jnp.log(l)

        @pl.when(jnp.logical_and(h == H - 1, i == nq - 1))
        def _():
            for cp in _exchange_copies(ins, outs, *sems, False):
                cp.wait()

    any_spec = pl.BlockSpec(memory_space=pl.ANY)
    grid_spec = pltpu.PrefetchScalarGridSpec(
        num_scalar_prefetch=2, grid=(H, nq),
        in_specs=[_t_block(LANES), _n_full(S), _t_full(HEAD_DIM, S)] + [any_spec] * n,
        out_specs=tuple([_t_block(HEAD_DIM), _t_block(1)] + [any_spec] * n),
        scratch_shapes=_exchange_sems(n))
    res = pl.pallas_call(
        body, name="fox_fwd", grid_spec=grid_spec,
        out_shape=tuple([jax.ShapeDtypeStruct((H, HEAD_DIM, S), F32), jax.ShapeDtypeStruct((H, 1, S), F32)]
                        + _exchange_out_shapes(shards, False)),
        compiler_params=_params(dimension_semantics=("arbitrary", "arbitrary"), has_side_effects=True),
    )(f_end, k_max, q_t, k_n, v_t, *shards)
    return res[0], res[1], res[2:]


def _fox_bwd(q_t, q_n, k_n, k_t, v_n, do_t, do_n, o_t, lse, f_end, k_max):
    H, _, S = q_t.shape

    def body(fend_ref, kmax_ref, qt_ref, qn_ref, k_ref, kt_ref, v_ref, dot_ref, don_ref, ot_ref, lse_ref,
             dqt_ref, dk_ref, dv_ref):
        h, i = pl.program_id(0), pl.program_id(1)

        @pl.when(i == 0)
        def _():
            dk_ref[...] = jnp.zeros_like(dk_ref)
            dv_ref[...] = jnp.zeros_like(dv_ref)

        qt, qn, dot, don = qt_ref[...], qn_ref[...], dot_ref[...], don_ref[...]
        lse = lse_ref[...]
        delta = jnp.sum(dot[:HEAD_DIM].astype(F32) * ot_ref[...], axis=0, keepdims=True)
        reach = _fox_reach(qt, fend_ref, kmax_ref, h)

        def alive(j, dq):
            return jnp.max(reach(j) - lse) > SKIP_BELOW

        def step(j, dq, masked):
            ks = _kv_slice(j)
            s = _dot(k_ref[ks, :], qt)
            if masked:
                s = jnp.where(_mask_t(False), s, _NEG)
            p = jnp.exp(s - lse)
            ds = (p * (_dot(v_ref[ks, :], dot) - delta)).astype(BF16)
            dk_ref[ks, :] += _dot(ds, qn)
            dv_ref[ks, :] += _dot(p.astype(BF16), don)
            return dq + _dot(kt_ref[:, ks], ds)

        dqt_ref[...] = _walk_down(i, step, alive, jnp.zeros((LANES, ATT_BQ), F32))

    grid_spec = pltpu.PrefetchScalarGridSpec(
        num_scalar_prefetch=2, grid=(H, S // ATT_BQ),
        in_specs=[_t_block(LANES), _n_block(), _n_full(S), _t_full(LANES, S), _n_full(S),
                  _t_block(LANES), _n_block(), _t_block(HEAD_DIM), _t_block(1)],
        out_specs=(_t_block(LANES), _n_full(S), _n_full(S)))
    return pl.pallas_call(
        body, name="fox_bwd", grid_spec=grid_spec,
        out_shape=(jax.ShapeDtypeStruct((H, LANES, S), F32), jax.ShapeDtypeStruct((H, S, LANES), F32),
                   jax.ShapeDtypeStruct((H, S, LANES), F32)),
        compiler_params=_params(dimension_semantics=("parallel", "arbitrary")),
    )(f_end, k_max, q_t, q_n, k_n, k_t, v_n, do_t, do_n, o_t, lse)


def _scan_lhs():
    r = lax.broadcasted_iota(jnp.int32, (SCAN_BK, 2 * SCAN_BK), 0)
    c = lax.broadcasted_iota(jnp.int32, (SCAN_BK, 2 * SCAN_BK), 1) % SCAN_BK
    return (c >= r).astype(BF16)


def _suffix_sum(t, lhs):
    hi = t.astype(BF16)
    lo = (t - hi.astype(F32)).astype(BF16)
    return _dot(lhs, jnp.concatenate([hi, lo], axis=0))


def _sb_scores(k, qt, mask):
    z = _dot(k, qt)
    e = jnp.exp(-jnp.abs(z))
    lb = -(jnp.maximum(z, 0.0) + jnp.log(1.0 + e))
    if mask is not None:
        lb = jnp.where(mask, lb, 0.0)
    return z, e, lb


def _scan_blocks():
    return [slice(u * SCAN_BK, (u + 1) * SCAN_BK) for u in reversed(range(N_SCAN))]


def _sb_fwd(q_t, k_n, v_t):
    H, _, S = q_t.shape

    def body(qt_ref, k_ref, vt_ref, ot_ref):
        i = pl.program_id(1)
        qt = qt_ref[...]
        lhs = _scan_lhs()

        def step(j, carry, masked):
            c, acc = carry
            ks = _kv_slice(j)
            mask = _mask_t(True) if masked else None
            z, _, lb = _sb_scores(k_ref[ks, :], qt, mask)
            parts = []
            for sl in _scan_blocks():
                rin = _suffix_sum(lb[sl], lhs)
                a = jnp.exp(z[sl] + rin + c)
                if masked:
                    a = jnp.where(mask[sl], a, 0.0)
                parts.append(a.astype(BF16))
                c = c + rin[0:1, :]
            a_all = jnp.concatenate(parts[::-1], axis=0)
            return c, acc + _dot(vt_ref[:, ks], a_all)

        carry = (jnp.zeros((1, ATT_BQ), F32), jnp.zeros((HEAD_DIM, ATT_BQ), F32))
        ot_ref[...] = _walk_down(i, step, lambda j, cr: jnp.max(cr[0]) > SKIP_BELOW, carry)[1]

    return pl.pallas_call(
        body, name="sb_fwd", grid=(H, S // ATT_BQ),
        out_shape=jax.ShapeDtypeStruct((H, HEAD_DIM, S), F32),
        in_specs=[_t_block(LANES), _n_full(S), _t_full(HEAD_DIM, S)],
        out_specs=_t_block(HEAD_DIM),
        compiler_params=_params(dimension_semantics=("parallel", "parallel")),
    )(q_t, k_n, v_t)


def _sb_bwd(q_t, q_n, k_n, k_t, v_n, do_t, do_n, o_t, bound):
    H, _, S = q_t.shape
    n, nq = len(bound), S // ATT_BQ

    def body(qt_ref, qn_ref, k_ref, kt_ref, v_ref, dot_ref, don_ref, ot_ref, *rest):
        ins, (dqt_ref, dk_ref, dv_ref) = rest[:n], rest[n:n + 3]
        outs, sems = rest[n + 3:2 * n + 3], rest[2 * n + 3:]
        h, i = pl.program_id(0), pl.program_id(1)

        @pl.when(jnp.logical_and(h == 0, i == 0))
        def _():
            for cp in _exchange_copies(ins, outs, *sems, True):
                cp.start()

        @pl.when(i == 0)
        def _():
            dk_ref[...] = jnp.zeros_like(dk_ref)
            dv_ref[...] = jnp.zeros_like(dv_ref)

        qt, qn, dot, don = qt_ref[...], qn_ref[...], dot_ref[...], don_ref[...]
        lhs = _scan_lhs()
        delta = jnp.sum(dot[:HEAD_DIM].astype(F32) * ot_ref[...], axis=0, keepdims=True)

        def step(j, carry, masked):
            c, g, dq = carry
            ks = _kv_slice(j)
            mask = _mask_t(True) if masked else None
            z, e, lb = _sb_scores(k_ref[ks, :], qt, mask)
            da = _dot(v_ref[ks, :], dot)
            a_parts, dz_parts = [], []
            for sl in _scan_blocks():
                rin = _suffix_sum(lb[sl], lhs)
                a = jnp.exp(z[sl] + rin + c)
                if masked:
                    a = jnp.where(mask[sl], a, 0.0)
                ab = a.astype(BF16)
                gg = ab.astype(F32) * da[sl]
                rgin = _suffix_sum(gg, lhs)
                rinv = 1.0 / (1.0 + e[sl])
                sig = jnp.where(z[sl] >= 0.0, rinv, e[sl] * rinv)
                dz = gg - sig * (delta - g - (rgin - gg))
                if masked:
                    dz = jnp.where(mask[sl], dz, 0.0)
                a_parts.append(ab)
                dz_parts.append(dz.astype(BF16))
                c = c + rin[0:1, :]
                g = g + rgin[0:1, :]
            ab_all = jnp.concatenate(a_parts[::-1], axis=0)
            dzb = jnp.concatenate(dz_parts[::-1], axis=0)
            dk_ref[ks, :] += _dot(dzb, qn)
            dv_ref[ks, :] += _dot(ab_all, don)
            return c, g, dq + _dot(kt_ref[:, ks], dzb)

        row = jnp.zeros((1, ATT_BQ), F32)
        carry = (row, row, jnp.zeros((LANES, ATT_BQ), F32))
        dqt_ref[...] = _walk_down(i, step, lambda j, cr: jnp.max(cr[0]) > SKIP_BELOW, carry)[2]

        @pl.when(jnp.logical_and(h == H - 1, i == nq - 1))
        def _():
            for cp in _exchange_copies(ins, outs, *sems, True):
                cp.wait()

    any_spec = pl.BlockSpec(memory_space=pl.ANY)
    res = pl.pallas_call(
        body, name="sb_bwd", grid=(H, nq),
        out_shape=tuple([jax.ShapeDtypeStruct((H, LANES, S), F32), jax.ShapeDtypeStruct((H, S, LANES), F32),
                         jax.ShapeDtypeStruct((H, S, LANES), F32)] + _exchange_out_shapes(bound, True)),
        in_specs=[_t_block(LANES), _n_block(), _n_full(S), _t_full(LANES, S), _n_full(S),
                  _t_block(LANES), _n_block(), _t_block(HEAD_DIM)] + [any_spec] * n,
        out_specs=tuple([_t_block(LANES), _n_full(S), _n_full(S)] + [any_spec] * n),
        scratch_shapes=_exchange_sems(n),
        compiler_params=_params(dimension_semantics=("arbitrary", "arbitrary"), has_side_effects=True),
    )(q_t, q_n, k_n, k_t, v_n, do_t, do_n, o_t, *bound)
    return res[0], res[1], res[2], res[3:]


def _acc_spec(w):
    return pl.BlockSpec((1, w), lambda i: (0, 0))


def _loss_head(x1, m_out, gate_m, g_final, target):
    S, D = x1.shape
    tm = _tile(S, 256)

    def body(x1_ref, mo_ref, gt_ref, gf_ref, tg_ref, dx2_ref, gm_ref, loss_ref, dgf_ref, dgt_ref):
        @pl.when(pl.program_id(0) == 0)
        def _():
            loss_ref[...] = jnp.zeros_like(loss_ref)
            dgf_ref[...] = jnp.zeros_like(dgf_ref)
            dgt_ref[...] = jnp.zeros_like(dgt_ref)

        mo = mo_ref[...]
        x2 = x1_ref[...] + gt_ref[...] * mo
        r = lax.rsqrt(jnp.mean(x2 * x2, axis=-1, keepdims=True) + EPS)
        xh = x2 * r
        diff = xh * gf_ref[...] - tg_ref[...]
        loss_ref[...] += (0.5 / D) * jnp.sum(diff * diff)
        dy = diff * (1.0 / D)
        dgf_ref[...] += jnp.sum(dy * xh, axis=0, keepdims=True)
        dxh = dy * gf_ref[...]
        dx2 = r * (dxh - xh * jnp.mean(dxh * xh, axis=-1, keepdims=True))
        dx2_ref[...] = dx2
        gm_ref[...] = (dx2 * gt_ref[...]).astype(BF16)
        dgt_ref[...] += jnp.sum(dx2 * mo, axis=0, keepdims=True)

    return pl.pallas_call(
        body, name="loss_head", grid=(S // tm,),
        out_shape=(jax.ShapeDtypeStruct((S, D), F32), jax.ShapeDtypeStruct((S, D), BF16),
                   jax.ShapeDtypeStruct((1, LANES), F32), jax.ShapeDtypeStruct((1, D), F32),
                   jax.ShapeDtypeStruct((1, D), F32)),
        in_specs=_row_specs(tm, [D, D]) + [_vec_spec(D)] * 2 + _row_specs(tm, [D]),
        out_specs=tuple(_row_specs(tm, [D, D]) + [_acc_spec(LANES), _acc_spec(D), _acc_spec(D)]),
        compiler_params=_params(dimension_semantics=("arbitrary",)),
    )(x1, m_out, gate_m, g_final, target)


def _norm_bwd(dh, xin, dres, g, scale, name, gate=None, branch=None):
    S, D = xin.shape
    tm = _tile(S, 256)
    gated = gate is not None

    def body(*refs):
        if gated:
            (dh_ref, x_ref, dr_ref, g_ref, sc_ref, gt_ref, br_ref,
             dx_ref, dsc_ref, dsh_ref, dg_ref, ga_ref, dgt_ref) = refs
            sums = (dsc_ref, dsh_ref, dg_ref, dgt_ref)
        else:
            dh_ref, x_ref, dr_ref, g_ref, sc_ref, dx_ref, dsc_ref, dsh_ref, dg_ref = refs
            sums = (dsc_ref, dsh_ref, dg_ref)

        @pl.when(pl.program_id(0) == 0)
        def _():
            for s_ref in sums:
                s_ref[...] = jnp.zeros_like(s_ref)

        dhv, xv = dh_ref[...], x_ref[...]
        r = lax.rsqrt(jnp.mean(xv * xv, axis=-1, keepdims=True) + EPS)
        xh = xv * r
        dsc_ref[...] += jnp.sum(dhv * (xh * g_ref[...]), axis=0, keepdims=True)
        dsh_ref[...] += jnp.sum(dhv, axis=0, keepdims=True)
        dn = dhv * (1.0 + sc_ref[...])
        dg_ref[...] += jnp.sum(dn * xh, axis=0, keepdims=True)
        dxh = dn * g_ref[...]
        dx = dr_ref[...] + r * (dxh - xh * jnp.mean(dxh * xh, axis=-1, keepdims=True))
        dx_ref[...] = dx
        if gated:
            ga_ref[...] = (dx * gt_ref[...]).astype(BF16)
            dgt_ref[...] += jnp.sum(dx * br_ref[...], axis=0, keepdims=True)

    vec = jax.ShapeDtypeStruct((1, D), F32)
    out_shape = [jax.ShapeDtypeStruct((S, D), F32), vec, vec, vec]
    out_specs = _row_specs(tm, [D]) + [_acc_spec(D)] * 3
    in_specs = _row_specs(tm, [D, D, D]) + [_vec_spec(D)] * 2
    args = [dh, xin, dres, g, scale]
    if gated:
        out_shape += [jax.ShapeDtypeStruct((S, D), BF16), vec]
        out_specs += _row_specs(tm, [D]) + [_acc_spec(D)]
        in_specs += [_vec_spec(D)] + _row_specs(tm, [D])
        args += [gate, branch]
    return pl.pallas_call(
        body, name=name, grid=(S // tm,), out_shape=tuple(out_shape),
        in_specs=in_specs, out_specs=tuple(out_specs),
        compiler_params=_params(dimension_semantics=("arbitrary",)),
    )(*args)


def _headnorm_bwd(dmix, o_f, o_s, g_f, g_s):
    S, dh = o_f.shape
    H = dh // HEAD_DIM
    tm = _tile(S, 256)

    def body(dm_ref, of_ref, os_ref, gf_ref, gs_ref, fn_ref, ft_ref, sn_ref, st_ref, dgf_ref, dgs_ref):
        @pl.when(pl.program_id(0) == 0)
        def _():
            dgf_ref[...] = jnp.zeros_like(dgf_ref)
            dgs_ref[...] = jnp.zeros_like(dgs_ref)

        ones = _group_ones()
        lane = lax.broadcasted_iota(jnp.int32, (1, LANES), 1)
        parts = ((of_ref, gf_ref, fn_ref, ft_ref, dgf_ref), (os_ref, gs_ref, sn_ref, st_ref, dgs_ref))
        for part, (o_ref, g_ref, n_ref, t_ref, dg_ref) in enumerate(parts):
            for t in range(dh // LANES):
                cols = slice(t * LANES, (t + 1) * LANES)
                o = o_ref[:, cols]
                dm = dm_ref[:, part * dh + t * LANES: part * dh + (t + 1) * LANES]
                r = lax.rsqrt(_dot(o * o, ones, precision=HIGHEST) * (1.0 / HEAD_DIM) + EPS)
                oh = o * r
                dg_ref[:, cols] += jnp.sum(dm * oh, axis=0, keepdims=True)
                dn = dm * g_ref[:, cols]
                mean = _dot(dn * oh, ones, precision=HIGHEST) * (1.0 / HEAD_DIM)
                do = r * (dn - oh * mean)
                for e in range(2):
                    d = do if e == 0 else pltpu.roll(do, HEAD_DIM, 1)
                    d = jnp.where(lane < HEAD_DIM, d, 0.0)
                    n_ref[2 * t + e] = d.astype(BF16)
                    t_ref[2 * t + e] = d.T.astype(BF16)

    vec = jax.ShapeDtypeStruct((1, dh), F32)
    n_sds = jax.ShapeDtypeStruct((H, S, LANES), BF16)
    t_sds = jax.ShapeDtypeStruct((H, LANES, S), BF16)
    n_spec = pl.BlockSpec((H, tm, LANES), lambda i: (0, i, 0))
    t_spec = pl.BlockSpec((H, LANES, tm), lambda i: (0, 0, i))
    return pl.pallas_call(
        body, name="headnorm_bwd", grid=(S // tm,),
        out_shape=(n_sds, t_sds, n_sds, t_sds, vec, vec),
        in_specs=_row_specs(tm, [2 * dh, dh, dh]) + [_vec_spec(dh)] * 2,
        out_specs=(n_spec, t_spec, n_spec, t_spec, _acc_spec(dh), _acc_spec(dh)),
        compiler_params=_params(dimension_semantics=("arbitrary",)),
    )(dmix, o_f, o_s, g_f, g_s)


def _adamw(w, gslots, m, v, name):
    R, C = w.shape
    n = gslots.shape[0]
    tr = 256 if (R % 256 == 0 and R > 256) else R
    bc1 = 1.0 - ADAM_B1 ** ADAM_STEP
    bc2 = 1.0 - ADAM_B2 ** ADAM_STEP

    def body(w_ref, gs_ref, m_ref, v_ref, g_ref, d_ref, nm_ref, nv_ref):
        g = gs_ref[0]
        for s in range(1, n):
            g = g + gs_ref[s]
        nm = ADAM_B1 * m_ref[...] + (1.0 - ADAM_B1) * g
        nv = ADAM_B2 * v_ref[...] + (1.0 - ADAM_B2) * (g * g)
        g_ref[...] = g
        nm_ref[...] = nm
        nv_ref[...] = nv
        d_ref[...] = -ADAM_LR * ((nm / bc1) / (jnp.sqrt(nv / bc2) + ADAM_EPS) + ADAM_WD * w_ref[...])

    blk = pl.BlockSpec((tr, C), lambda i: (i, 0))
    sds = jax.ShapeDtypeStruct((R, C), F32)
    return pl.pallas_call(
        body, name=name, grid=(R // tr,), out_shape=(sds,) * 4,
        in_specs=[blk, pl.BlockSpec((n, tr, C), lambda i: (0, i, 0)), blk, blk], out_specs=(blk,) * 4,
        compiler_params=_params(dimension_semantics=("parallel",)),
    )(w, gslots, m, v)


def _slot_sum(slots, name):
    n, _, C = slots.shape

    def body(s_ref, o_ref):
        acc = s_ref[0]
        for s in range(1, n):
            acc = acc + s_ref[s]
        o_ref[...] = acc

    return pl.pallas_call(body, name=name, out_shape=jax.ShapeDtypeStruct((1, C), F32),
                          compiler_params=_params())(slots)


def _pad_cols(a, n):
    return jnp.pad(a, ((0, 0), (0, n - a.shape[1])))


def _ungather(g, axis):
    if axis == 0:
        return g.reshape(g.shape[0] * g.shape[1], g.shape[2])
    return jnp.transpose(g, (1, 0, 2)).reshape(g.shape[1], g.shape[0] * g.shape[2])


def _to_slots(full, axis):
    R, C = full.shape
    if axis == 0:
        return full.reshape(N_DEV, R // N_DEV, C)
    return jnp.transpose(full.reshape(R, N_DEV, C // N_DEV), (1, 0, 2))


def kernel(x, c, w_ada, b_ada, g_attn, w_in, b_fgate, g_out_fox, g_out_sb, w_out, g_mlp, w_up, conv_w, conv_b, w_down, g_final, loss_target, m_w_ada, m_b_ada, m_g_attn, m_w_in, m_b_fgate, m_g_out_fox, m_g_out_sb, m_w_out, m_g_mlp, m_w_up, m_conv_w, m_conv_b, m_w_down, m_g_final, v_w_ada, v_b_ada, v_g_attn, v_w_in, v_b_fgate, v_g_out_fox, v_g_out_sb, v_w_out, v_g_mlp, v_w_up, v_conv_w, v_conv_b, v_w_down, v_g_final):
    S, D = x.shape[1], x.shape[2]
    dh = D // 2
    n_heads = dh // HEAD_DIM
    n_qkv = 6 * dh
    ff = w_down.shape[1] * N_DEV
    ffp = -(-ff // (2 * LANES)) * (2 * LANES)
    nc = S // LANES
    me = 4 * lax.axis_index("x") + 2 * lax.axis_index("y") + lax.axis_index("c")
    xs, tgt = x[0], loss_target[0]

    c_all, win_g = _gather_two_level([c, w_in[0].astype(BF16)], name="gather_first")
    c_all = c_all.reshape(N_DEV, D)
    W_in = _ungather(win_g, 1)
    W_qkv, W_f = W_in[:, :n_qkv], _pad_cols(W_in[:, n_qkv:], LANES)
    W_inp = jnp.concatenate([W_qkv, W_f], axis=1)
    cb_g, cb_v = _pad_cols(conv_b[:, :ff], ffp), _pad_cols(conv_b[:, ff:], ffp)

    n_ada = w_ada.shape[2]
    b_shard = lax.dynamic_slice(b_ada, (0, me * n_ada), (1, n_ada))
    mod_cols = _ada_fwd(c_all, w_ada[0], b_shard)
    (mod_g,) = _exchange([mod_cols], scatter=False, name="gather_mod")
    mod = lax.dynamic_index_in_dim(mod_g, me, axis=1, keepdims=False).reshape(6, 1, D)
    shift_a, scale_a, gate_a, shift_m, scale_m, gate_m = [mod[k] for k in range(6)]

    h1, h1_t = _prenorm(xs, g_attn, scale_a, shift_a, "prenorm_attn")
    qkv = _mm(h1, W_qkv, BF16, "proj_qkv")
    flog = _mm(h1, W_f, F32, "proj_fgate")
    zf = flog[:, :n_heads] + b_fgate
    z_rows = zf.T.reshape(n_heads * nc, LANES)
    f_rows = _fgate_fwd(z_rows, nc).reshape(n_heads, S)
    f_pairs = jnp.transpose(f_rows.reshape(n_heads // 2, 2, S), (0, 2, 1))
    fox, sb = _att_prep(qkv, f_pairs)
    f_end, k_max = _skip_bounds(qkv[:, dh:2 * dh], f_rows)
    of_t, lse, (wout_g, wup_g, wdown_g, convw_g) = _fox_fwd(
        fox["q_t"], fox["k_n"], fox["v_t"], f_end, k_max,
        [w_out[0].astype(BF16), w_up[0].astype(BF16), w_down[0].astype(BF16), conv_w[0]])
    W_out = _ungather(wout_g, 0)
    W_up = _ungather(wup_g, 1)
    W_g, W_v = _pad_cols(W_up[:, :ff], ffp), _pad_cols(W_up[:, ff:], ffp)
    W_down = jnp.pad(_ungather(wdown_g, 0), ((0, ffp - ff), (0, 0)))
    cw_full = _ungather(convw_g, 1)
    cw_g, cw_v = _pad_cols(cw_full[:, :ff], ffp), _pad_cols(cw_full[:, ff:], ffp)
    os_t = _sb_fwd(sb["q_t"], sb["k_n"], sb["v_t"])
    o_f, o_s = _unheads(_tr(of_t)), _unheads(_tr(os_t))
    mix, mix_t = _headnorm_fwd(o_f, o_s, g_out_fox, g_out_sb)
    a_out = _mm(mix, W_out, F32, "proj_out")
    x1, h2, h2_t = _resid_prenorm(xs, a_out, gate_a, g_mlp, scale_m, shift_m)
    up_g, up_v, act, act_t = _mlp_up(h2, W_g, W_v, cw_g, cw_v, cb_g, cb_v)
    m_out = _mm(act, W_down, F32, "proj_down")

    dx2, gm, loss_p, dg_final, dgate_m = _loss_head(x1, m_out, gate_m, g_final.reshape(1, D), tgt)
    dact = _mm(gm, W_down, BF16, "bwd_down_act", nt=True)
    dW_down = _mm_acc(act_t, gm, "bwd_down_w")
    du_g, du_v, p_g, p_v = _conv_act_bwd(dact, up_g, up_v, cw_g, cw_v, cb_g, cb_v)
    dup_g = _conv_bwd_input(du_g, cw_g, "conv_bwd_input_g")
    dup_v = _conv_bwd_input(du_v, cw_v, "conv_bwd_input_v")
    dh2 = _mm2(dup_g, W_g, dup_v, W_v, F32, "bwd_up_act", nt=True)
    dW_g = _mm_acc(h2_t, dup_g, "bwd_up_w_g")
    dW_v = _mm_acc(h2_t, dup_v, "bwd_up_w_v")
    dx1, dscale_m, dshift_m, dg_mlp, ga, dgate_a = _norm_bwd(
        dh2, x1, dx2, g_mlp, scale_m, "norm_mlp_bwd", gate=gate_a, branch=a_out)
    dmix = _mm(ga, W_out, F32, "bwd_out_act", nt=True)
    dW_out = _mm_acc(mix_t, ga, "bwd_out_w")
    dof_n, dof_t, dos_n, dos_t, dg_fox, dg_sb = _headnorm_bwd(dmix, o_f, o_s, g_out_fox, g_out_sb)
    dqf_t, dkf, dvf = _fox_bwd(fox["q_t"], fox["q_n"], fox["k_n"], fox["k_t"], fox["v_n"], dof_t, dof_n, of_t, lse,
                              f_end, k_max)
    dW_upf = jnp.concatenate([dW_g[:, :ff], dW_v[:, :ff]], axis=1)
    dcw = jnp.concatenate([p_g[:CONV_W, :ff], p_v[:CONV_W, :ff]], axis=1)
    dqs_t, dks, dvs, (s_out, s_up, s_down, s_cw) = _sb_bwd(
        sb["q_t"], sb["q_n"], sb["k_n"], sb["k_t"], sb["v_n"], dos_t, dos_n, os_t,
        [_to_slots(dW_out, 0), _to_slots(dW_upf, 1), _to_slots(dW_down[:ff], 0), _to_slots(dcw, 1)])
    dz_rows, db_fgate = _fgate_bwd(dkf[:, :, Q_ONE_LANE].reshape(n_heads * nc, LANES),
                                   dqf_t[:, Q_F_LANE, :].reshape(n_heads * nc, LANES), z_rows, nc)
    dzf = dz_rows.reshape(n_heads, S).T

    def dq_cols(dq_t):
        return (_unheads(_tr(dq_t[:, :HEAD_DIM, :])) * _SCALE).astype(BF16)

    def dkv_cols(d):
        return _unheads(d[:, :, :HEAD_DIM]).astype(BF16)

    dproj = jnp.concatenate(
        [dq_cols(dqf_t), dkv_cols(dkf), dkv_cols(dvf), dq_cols(dqs_t), dkv_cols(dks), dkv_cols(dvs),
         _pad_cols(dzf, LANES).astype(BF16)], axis=1)
    dh1 = _mm(dproj, W_inp, F32, "bwd_in_act", nt=True)
    dW_inp = _mm_acc(h1_t, dproj, "bwd_in_w")
    grad_x, dscale_a, dshift_a, dg_attn = _norm_bwd(dh1, xs, dx1, g_attn, scale_a, "norm_attn_bwd")

    dconv_b = jnp.concatenate([p_g[CONV_W:CONV_W + 1, :ff], p_v[CONV_W:CONV_W + 1, :ff]], axis=1)
    parts = [dshift_a, dscale_a, dgate_a, dshift_m, dscale_m, dgate_m,
             dg_attn, db_fgate.reshape(1, n_heads), dg_fox, dg_sb, dg_mlp, dconv_b, dg_final,
             loss_p[:, :1]]
    sizes = [p.shape[1] for p in parts]
    vec = jnp.concatenate(parts, axis=1)
    n_vec = -(-vec.shape[1] // LANES) * LANES
    vec = _pad_cols(vec, n_vec)
    (vec_g,) = _exchange([vec], scatter=False, name="gather_small")
    offs = [0]
    for s in sizes:
        offs.append(offs[-1] + s)

    def small(k0, k1=None):
        k1 = k0 if k1 is None else k1
        return vec_g[:, :, offs[k0]:offs[k1 + 1]]

    dmod_all = small(0, 5).reshape(N_DEV, 6 * D)
    dmod_cols = lax.dynamic_slice(dmod_all, (0, me * n_ada), (N_DEV, n_ada))
    dW_ada = _ada_bwd(c_all.T, dmod_cols)

    dW_in = jnp.concatenate([dW_inp[:, :n_qkv], dW_inp[:, n_qkv:n_qkv + n_heads]], axis=1)
    bound_in = _to_slots(dW_in, 1)
    bound_in = bound_in.reshape((N_CHIPS, 2) + bound_in.shape[1:])
    (got_in,) = _scatter_to_sibling([bound_in], "scatter_sibling")
    c_idx = lax.axis_index("c").astype(jnp.int32).reshape(1)
    (s_in,) = _scatter_to_chips([_pair_add(bound_in, got_in, c_idx, "pair_add")], "scatter_chips")

    res = {}
    res["w_ada"] = _adamw(w_ada[0], dW_ada[None], m_w_ada[0], v_w_ada[0], "adamw_w_ada")
    res["w_in"] = _adamw(w_in[0], s_in, m_w_in[0], v_w_in[0], "adamw_w_in")
    res["w_out"] = _adamw(w_out[0], s_out, m_w_out[0], v_w_out[0], "adamw_w_out")
    res["w_up"] = _adamw(w_up[0], s_up, m_w_up[0], v_w_up[0], "adamw_w_up")
    res["w_down"] = _adamw(w_down[0], s_down, m_w_down[0], v_w_down[0], "adamw_w_down")
    res["conv_w"] = _adamw(conv_w[0], s_cw, m_conv_w[0], v_conv_w[0], "adamw_conv_w")
    res["b_ada"] = _adamw(b_ada, small(0, 5), m_b_ada, v_b_ada, "adamw_b_ada")
    res["g_attn"] = _adamw(g_attn, small(6), m_g_attn, v_g_attn, "adamw_g_attn")
    res["b_fgate"] = _adamw(b_fgate, small(7), m_b_fgate, v_b_fgate, "adamw_b_fgate")
    res["g_out_fox"] = _adamw(g_out_fox, small(8), m_g_out_fox, v_g_out_fox, "adamw_g_out_fox")
    res["g_out_sb"] = _adamw(g_out_sb, small(9), m_g_out_sb, v_g_out_sb, "adamw_g_out_sb")
    res["g_mlp"] = _adamw(g_mlp, small(10), m_g_mlp, v_g_mlp, "adamw_g_mlp")
    res["conv_b"] = _adamw(conv_b, small(11), m_conv_b, v_conv_b, "adamw_conv_b")
    res["g_final"] = _adamw(g_final.reshape(1, D), small(12), m_g_final.reshape(1, D),
                            v_g_final.reshape(1, D), "adamw_g_final")
    loss = _slot_sum(_pad_cols(small(13).reshape(N_DEV, 1), LANES).reshape(N_DEV, 1, LANES), "loss_sum")[0, 0]

    names = ["w_ada", "b_ada", "g_attn", "w_in", "b_fgate", "g_out_fox", "g_out_sb", "w_out", "g_mlp",
             "w_up", "conv_w", "conv_b", "w_down", "g_final"]

    def shaped(n, a):
        if n == "g_final":
            return a.reshape(D)
        if n in ("b_ada", "g_attn", "b_fgate", "g_out_fox", "g_out_sb", "g_mlp", "conv_b"):
            return a
        return a[None]

    outs = [loss, grad_x[None]]
    for k in range(4):
        outs += [shaped(n, res[n][k]) for n in names]
    return tuple(outs)
```

```python
import jax
import jax.numpy as jnp
from jax import lax
from jax.experimental import pallas as pl
from jax.experimental.pallas import tpu as pltpu

F32 = jnp.float32
BF16 = jnp.bfloat16
HIGHEST = lax.Precision.HIGHEST

N_DEV = 8
LANES = 128
HEAD_DIM = 64
EPS = 1e-6
CONV_W = 3
CONV_COLS = 1408
HALO = 16
ATT_BQ = 512
ATT_BK = 512
SCAN_BK = 128
VMEM_LIMIT = 56 * 1024 * 1024

ADAM_LR = 0.001
ADAM_B1 = 0.9
ADAM_B2 = 0.999
ADAM_EPS = 1e-08
ADAM_WD = 0.01
ADAM_STEP = 10


def _params(**kw):
    return pltpu.CompilerParams(vmem_limit_bytes=VMEM_LIMIT, **kw)


def _tile(n, cap):
    if n <= cap:
        return n
    best = None
    for t in range(LANES, cap + 1, LANES):
        if n % t == 0:
            best = t
    assert best is not None, (n, cap)
    return best


def _dot(a, b, **kw):
    return jnp.dot(a, b, preferred_element_type=F32, **kw)


def _exchange_copies(ins, outs, send_sems, recv_sems, loc_sems, scatter):
    n = len(ins)
    if n == 0:
        return []
    x, y, c = lax.axis_index("x"), lax.axis_index("y"), lax.axis_index("c")
    me = 4 * x + 2 * y + c
    copies = []
    for a in range(n):
        src = ins[a].at[me] if scatter else ins[a]
        copies.append(pltpu.make_async_copy(src, outs[a].at[me], loc_sems.at[a]))
    for k in range(1, N_DEV):
        px = 1 - x if k & 4 else x
        py = 1 - y if k & 2 else y
        pc = 1 - c if k & 1 else c
        peer = 4 * px + 2 * py + pc
        for a in range(n):
            src = ins[a].at[peer] if scatter else ins[a]
            copies.append(pltpu.make_async_remote_copy(
                src_ref=src, dst_ref=outs[a].at[me],
                send_sem=send_sems.at[a, k - 1], recv_sem=recv_sems.at[a, k - 1],
                device_id=(px, py, pc), device_id_type=pl.DeviceIdType.MESH))
    return copies


def _exchange_out_shapes(arrays, scatter):
    return [jax.ShapeDtypeStruct((N_DEV,) + tuple(a.shape[1:] if scatter else a.shape), a.dtype) for a in arrays]


def _exchange_sems(n):
    return [pltpu.SemaphoreType.DMA((n, N_DEV - 1)), pltpu.SemaphoreType.DMA((n, N_DEV - 1)),
            pltpu.SemaphoreType.DMA((n,))]


def _exchange(arrays, scatter, name):
    n = len(arrays)

    def body(*refs):
        copies = _exchange_copies(refs[:n], refs[n:2 * n], *refs[2 * n:], scatter)
        for cp in copies:
            cp.start()
        for cp in copies:
            cp.wait()

    any_spec = pl.BlockSpec(memory_space=pl.ANY)
    return pl.pallas_call(
        body, name=name, out_shape=tuple(_exchange_out_shapes(arrays, scatter)),
        in_specs=[any_spec] * n, out_specs=tuple([any_spec] * n),
        scratch_shapes=_exchange_sems(n),
        compiler_params=pltpu.CompilerParams(has_side_effects=True),
    )(*arrays)


def _gather_two_level(arrays, name):
    n = len(arrays)
    out_shape = [jax.ShapeDtypeStruct((N_DEV,) + tuple(a.shape), a.dtype) for a in arrays]

    def body(*refs):
        ins, outs = refs[:n], refs[n:2 * n]
        send_sems, recv_sems, loc_sems = refs[2 * n:]
        x, y, c = lax.axis_index("x"), lax.axis_index("y"), lax.axis_index("c")
        me, sibling = (x, y, c), (x, y, 1 - c)
        chips = [(1 - x, y), (x, 1 - y), (1 - x, 1 - y)]

        def slot(px, py, pc):
            return 4 * px + 2 * py + pc

        def copy(a, k, block, to, src=None):
            dst = outs[a].at[slot(*block)]
            return pltpu.make_async_remote_copy(
                src_ref=dst if src is None else src, dst_ref=dst,
                send_sem=send_sems.at[a, k], recv_sem=recv_sems.at[a, k],
                device_id=to, device_id_type=pl.DeviceIdType.MESH)

        local = [pltpu.make_async_copy(ins[a], outs[a].at[slot(*me)], loc_sems.at[a]) for a in range(n)]
        for cp in local:
            cp.start()
        first = []
        for a in range(n):
            first.append(copy(a, 0, me, sibling, src=ins[a]))
            first += [copy(a, 1 + j, me, (*chip, c), src=ins[a]) for j, chip in enumerate(chips)]
        for cp in first:
            cp.start()
        passed = []
        for j, chip in enumerate(chips):
            for a in range(n):
                copy(a, 1 + j, (*chip, c), me).wait_recv()
                cp = copy(a, 4 + j, (*chip, c), sibling)
                cp.start()
                passed.append(cp)
        for a in range(n):
            copy(a, 0, sibling, me).wait_recv()
            for j, chip in enumerate(chips):
                copy(a, 4 + j, (*chip, 1 - c), me).wait_recv()
        for cp in first + passed:
            cp.wait_send()
        for cp in local:
            cp.wait()

    any_spec = pl.BlockSpec(memory_space=pl.ANY)
    return pl.pallas_call(
        body, name=name, out_shape=tuple(out_shape),
        in_specs=[any_spec] * n, out_specs=tuple([any_spec] * n),
        scratch_shapes=[pltpu.SemaphoreType.DMA((n, N_DEV - 1)), pltpu.SemaphoreType.DMA((n, N_DEV - 1)),
                        pltpu.SemaphoreType.DMA((n,))],
        compiler_params=pltpu.CompilerParams(has_side_effects=True),
    )(*arrays)


N_CHIPS = 4


def _scatter_to_sibling(arrays, name):
    n = len(arrays)
    out_shape = [jax.ShapeDtypeStruct((N_CHIPS,) + tuple(a.shape[2:]), a.dtype) for a in arrays]

    def body(*refs):
        ins, outs = refs[:n], refs[n:2 * n]
        send_sems, recv_sems = refs[2 * n:]
        x, y, c = lax.axis_index("x"), lax.axis_index("y"), lax.axis_index("c")
        copies = []
        for a in range(n):
            for q in range(N_CHIPS):
                cp = pltpu.make_async_remote_copy(
                    src_ref=ins[a].at[q, 1 - c], dst_ref=outs[a].at[q],
                    send_sem=send_sems.at[a, q], recv_sem=recv_sems.at[a, q],
                    device_id=(x, y, 1 - c), device_id_type=pl.DeviceIdType.MESH)
                cp.start()
                copies.append(cp)
        for cp in copies:
            cp.wait()

    any_spec = pl.BlockSpec(memory_space=pl.ANY)
    return pl.pallas_call(
        body, name=name, out_shape=tuple(out_shape),
        in_specs=[any_spec] * n, out_specs=tuple([any_spec] * n),
        scratch_shapes=[pltpu.SemaphoreType.DMA((n, N_CHIPS)), pltpu.SemaphoreType.DMA((n, N_CHIPS))],
        compiler_params=pltpu.CompilerParams(has_side_effects=True),
    )(*arrays)


def _pair_add(mine, got, c_idx, name):
    _, _, R, C = mine.shape
    tr = 256 if (R % 256 == 0 and R > 256) else R

    def body(c_ref, m_ref, g_ref, o_ref):
        o_ref[...] = m_ref[...] + g_ref[...]

    grid_spec = pltpu.PrefetchScalarGridSpec(
        num_scalar_prefetch=1, grid=(N_CHIPS, R // tr),
        in_specs=[pl.BlockSpec((None, None, tr, C), lambda q, i, c_ref: (q, c_ref[0], i, 0)),
                  pl.BlockSpec((None, tr, C), lambda q, i, c_ref: (q, i, 0))],
        out_specs=pl.BlockSpec((None, tr, C), lambda q, i, c_ref: (q, i, 0)))
    return pl.pallas_call(
        body, name=name, grid_spec=grid_spec, out_shape=jax.ShapeDtypeStruct((N_CHIPS, R, C), mine.dtype),
        compiler_params=_params(dimension_semantics=("parallel", "parallel")),
    )(c_idx, mine, got)


def _scatter_to_chips(arrays, name):
    n = len(arrays)
    out_shape = [jax.ShapeDtypeStruct(a.shape, a.dtype) for a in arrays]

    def body(*refs):
        ins, outs = refs[:n], refs[n:2 * n]
        send_sems, recv_sems, loc_sems = refs[2 * n:]
        x, y, c = lax.axis_index("x"), lax.axis_index("y"), lax.axis_index("c")
        myq = 2 * x + y
        copies = []
        for a in range(n):
            cp = pltpu.make_async_copy(ins[a].at[myq], outs[a].at[myq], loc_sems.at[a])
            cp.start()
            copies.append(cp)
        for k in range(1, N_CHIPS):
            qx = 1 - x if k & 2 else x
            qy = 1 - y if k & 1 else y
            for a in range(n):
                cp = pltpu.make_async_remote_copy(
                    src_ref=ins[a].at[2 * qx + qy], dst_ref=outs[a].at[myq],
                    send_sem=send_sems.at[a, k - 1], recv_sem=recv_sems.at[a, k - 1],
                    device_id=(qx, qy, c), device_id_type=pl.DeviceIdType.MESH)
                cp.start()
                copies.append(cp)
        for cp in copies:
            cp.wait()

    any_spec = pl.BlockSpec(memory_space=pl.ANY)
    return pl.pallas_call(
        body, name=name, out_shape=tuple(out_shape),
        in_specs=[any_spec] * n, out_specs=tuple([any_spec] * n),
        scratch_shapes=[pltpu.SemaphoreType.DMA((n, N_CHIPS - 1)), pltpu.SemaphoreType.DMA((n, N_CHIPS - 1)),
                        pltpu.SemaphoreType.DMA((n,))],
        compiler_params=pltpu.CompilerParams(has_side_effects=True),
    )(*arrays)


def _dot_nt(a, b):
    return lax.dot_general(a, b, (((1,), (1,)), ((), ())), preferred_element_type=F32)


def _rhs_spec(b, tn, nt):
    if nt:
        return pl.BlockSpec((tn, b.shape[1]), lambda i, j: (j, 0))
    return pl.BlockSpec((b.shape[0], tn), lambda i, j: (0, j))


def _mm(a, b, out_dtype, name, tm=1024, tn=512, nt=False):
    M, K = a.shape
    N = b.shape[0] if nt else b.shape[1]
    tm, tn = _tile(M, tm), _tile(N, tn)
    dot = _dot_nt if nt else _dot

    def body(a_ref, b_ref, o_ref):
        o_ref[...] = dot(a_ref[...], b_ref[...]).astype(out_dtype)

    return pl.pallas_call(
        body, name=name, out_shape=jax.ShapeDtypeStruct((M, N), out_dtype),
        grid=(M // tm, N // tn),
        in_specs=[pl.BlockSpec((tm, K), lambda i, j: (i, 0)), _rhs_spec(b, tn, nt)],
        out_specs=pl.BlockSpec((tm, tn), lambda i, j: (i, j)),
        compiler_params=_params(dimension_semantics=("parallel", "parallel")),
    )(a, b)


def _mm2(a1, b1, a2, b2, out_dtype, name, tm=1024, tn=512, nt=False):
    M, K1 = a1.shape
    _, K2 = a2.shape
    N = b1.shape[0] if nt else b1.shape[1]
    tm, tn = _tile(M, tm), _tile(N, tn)
    dot = _dot_nt if nt else _dot

    def body(a1_ref, b1_ref, a2_ref, b2_ref, o_ref):
        o_ref[...] = (dot(a1_ref[...], b1_ref[...]) + dot(a2_ref[...], b2_ref[...])).astype(out_dtype)

    return pl.pallas_call(
        body, name=name, out_shape=jax.ShapeDtypeStruct((M, N), out_dtype),
        grid=(M // tm, N // tn),
        in_specs=[pl.BlockSpec((tm, K1), lambda i, j: (i, 0)), _rhs_spec(b1, tn, nt),
                  pl.BlockSpec((tm, K2), lambda i, j: (i, 0)), _rhs_spec(b2, tn, nt)],
        out_specs=pl.BlockSpec((tm, tn), lambda i, j: (i, j)),
        compiler_params=_params(dimension_semantics=("parallel", "parallel")),
    )(a1, b1, a2, b2)


def _mm_acc(a, b, name, tm=1408, tn=1408, tk=512):
    M, S = a.shape
    _, N = b.shape
    tm, tn, tk = _tile(M, tm), _tile(N, tn), _tile(S, tk)

    def body(a_ref, b_ref, o_ref):
        @pl.when(pl.program_id(2) == 0)
        def _():
            o_ref[...] = jnp.zeros_like(o_ref)

        o_ref[...] += _dot(a_ref[...], b_ref[...])

    return pl.pallas_call(
        body, name=name, out_shape=jax.ShapeDtypeStruct((M, N), F32),
        grid=(M // tm, N // tn, S // tk),
        in_specs=[pl.BlockSpec((tm, tk), lambda i, j, k: (i, k)), pl.BlockSpec((tk, tn), lambda i, j, k: (k, j))],
        out_specs=pl.BlockSpec((tm, tn), lambda i, j, k: (i, j)),
        compiler_params=_params(dimension_semantics=("parallel", "parallel", "arbitrary")),
    )(a, b)


def _mm_parts(parts, w, a2, w2, name, tm=1024, tn=512):
    P, S, K = parts.shape
    N, K2 = w2.shape
    tm, tn = _tile(S, tm), _tile(N, tn)

    def body(p_ref, w_ref, a2_ref, w2_ref, o_ref):
        acc = _dot_nt(a2_ref[...], w2_ref[...])
        for k in range(P):
            acc = acc + _dot_nt(p_ref[k], w_ref[:, k * K:(k + 1) * K])
        o_ref[...] = acc

    return pl.pallas_call(
        body, name=name, out_shape=jax.ShapeDtypeStruct((S, N), F32), grid=(S // tm, N // tn),
        in_specs=[pl.BlockSpec((P, tm, K), lambda i, j: (0, i, 0)), pl.BlockSpec((tn, P * K), lambda i, j: (j, 0)),
                  pl.BlockSpec((tm, K2), lambda i, j: (i, 0)), pl.BlockSpec((tn, K2), lambda i, j: (j, 0))],
        out_specs=pl.BlockSpec((tm, tn), lambda i, j: (i, j)),
        compiler_params=_params(dimension_semantics=("parallel", "parallel")),
    )(parts, w, a2, w2)


def _mm_acc_parts(a, parts, name, tm=1024, tk=512):
    M, S = a.shape
    P, _, K = parts.shape
    tm, tk = _tile(M, tm), _tile(S, tk)

    def body(a_ref, b_ref, o_ref):
        @pl.when(pl.program_id(2) == 0)
        def _():
            o_ref[...] = jnp.zeros_like(o_ref)

        o_ref[...] += _dot(a_ref[...], b_ref[...])

    return pl.pallas_call(
        body, name=name, out_shape=jax.ShapeDtypeStruct((P, M, K), F32), grid=(P, M // tm, S // tk),
        in_specs=[pl.BlockSpec((tm, tk), lambda k, i, s: (i, s)), pl.BlockSpec((None, tk, K), lambda k, i, s: (k, s, 0))],
        out_specs=pl.BlockSpec((None, tm, K), lambda k, i, s: (k, i, 0)),
        compiler_params=_params(dimension_semantics=("parallel", "parallel", "arbitrary")),
    )(a, parts)


def _silu(z):
    return z * (1.0 / (1.0 + jnp.exp(-z)))


def _ada_fwd(c_all, w_shard, b_shard):
    n = w_shard.shape[1]

    def body(c_ref, w_ref, b_ref, o_ref):
        o_ref[...] = _dot(_silu(c_ref[...]), w_ref[...], precision=HIGHEST) + b_ref[...]

    return pl.pallas_call(body, name="ada_fwd", out_shape=jax.ShapeDtypeStruct((N_DEV, n), F32),
                          compiler_params=_params())(c_all, w_shard, b_shard)


def _ada_bwd(c_all_t, dmod_cols):
    D = c_all_t.shape[0]
    n = dmod_cols.shape[1]

    def body(ct_ref, dm_ref, o_ref):
        sc = _silu(ct_ref[...])
        dm = dm_ref[...]
        acc = sc[:, 0:1] * dm[0:1, :]
        for b in range(1, N_DEV):
            acc = acc + sc[:, b:b + 1] * dm[b:b + 1, :]
        o_ref[...] = acc

    return pl.pallas_call(body, name="ada_bwd", out_shape=jax.ShapeDtypeStruct((D, n), F32),
                          compiler_params=_params())(c_all_t, dmod_cols)


def _row_specs(tm, widths):
    return [pl.BlockSpec((tm, w), lambda i: (i, 0)) for w in widths]


def _vec_spec(w):
    return pl.BlockSpec((1, w), lambda i: (0, 0))


def _col_spec(tm, w):
    return pl.BlockSpec((w, tm), lambda i: (0, i))


def _prenorm(x, g, scale, shift, name):
    S, D = x.shape
    tm = _tile(S, 512)

    def body(x_ref, g_ref, sc_ref, sh_ref, h_ref, ht_ref):
        xv = x_ref[...]
        r = lax.rsqrt(jnp.mean(xv * xv, axis=-1, keepdims=True) + EPS)
        h = (xv * r) * g_ref[...] * (1.0 + sc_ref[...]) + sh_ref[...]
        h_ref[...] = h.astype(BF16)
        ht_ref[...] = h.T.astype(BF16)

    return pl.pallas_call(
        body, name=name, grid=(S // tm,),
        out_shape=(jax.ShapeDtypeStruct((S, D), BF16), jax.ShapeDtypeStruct((D, S), BF16)),
        in_specs=_row_specs(tm, [D]) + [_vec_spec(D)] * 3,
        out_specs=(_row_specs(tm, [D])[0], _col_spec(tm, D)),
        compiler_params=_params(dimension_semantics=("parallel",)),
    )(x, g, scale, shift)


def _group_ones():
    r = lax.broadcasted_iota(jnp.int32, (LANES, LANES), 0) // HEAD_DIM
    c = lax.broadcasted_iota(jnp.int32, (LANES, LANES), 1) // HEAD_DIM
    return (r == c).astype(F32)


def _headnorm_fwd(o_f, o_s, g_f, g_s):
    S, dh = o_f.shape
    tm = _tile(S, 512)

    def body(of_ref, os_ref, gf_ref, gs_ref, mix_ref, mixt_ref):
        ones = _group_ones()
        for part, (o_ref, g_ref) in enumerate(((of_ref, gf_ref), (os_ref, gs_ref))):
            for t in range(dh // LANES):
                cols = slice(t * LANES, (t + 1) * LANES)
                out = slice(part * dh + t * LANES, part * dh + (t + 1) * LANES)
                o = o_ref[:, cols]
                ms = _dot(o * o, ones, precision=HIGHEST) * (1.0 / HEAD_DIM)
                mixn = o * lax.rsqrt(ms + EPS) * g_ref[:, cols]
                mix_ref[:, out] = mixn.astype(BF16)
                mixt_ref[out, :] = mixn.T.astype(BF16)

    return pl.pallas_call(
        body, name="headnorm_fwd", grid=(S // tm,),
        out_shape=(jax.ShapeDtypeStruct((S, 2 * dh), BF16), jax.ShapeDtypeStruct((2 * dh, S), BF16)),
        in_specs=_row_specs(tm, [dh, dh]) + [_vec_spec(dh)] * 2,
        out_specs=(_row_specs(tm, [2 * dh])[0], _col_spec(tm, 2 * dh)),
        compiler_params=_params(dimension_semantics=("parallel",)),
    )(o_f, o_s, g_f, g_s)


def _resid_prenorm(x, a_out, gate, g, scale, shift):
    S, D = x.shape
    tm = _tile(S, 512)

    def body(x_ref, a_ref, gt_ref, g_ref, sc_ref, sh_ref, x1_ref, h_ref, ht_ref):
        x1 = x_ref[...] + gt_ref[...] * a_ref[...]
        x1_ref[...] = x1
        r = lax.rsqrt(jnp.mean(x1 * x1, axis=-1, keepdims=True) + EPS)
        h = (x1 * r) * g_ref[...] * (1.0 + sc_ref[...]) + sh_ref[...]
        h_ref[...] = h.astype(BF16)
        ht_ref[...] = h.T.astype(BF16)

    return pl.pallas_call(
        body, name="resid_prenorm", grid=(S // tm,),
        out_shape=(jax.ShapeDtypeStruct((S, D), F32), jax.ShapeDtypeStruct((S, D), BF16),
                   jax.ShapeDtypeStruct((D, S), BF16)),
        in_specs=_row_specs(tm, [D, D]) + [_vec_spec(D)] * 4,
        out_specs=tuple(_row_specs(tm, [D, D]) + [_col_spec(tm, D)]),
        compiler_params=_params(dimension_semantics=("parallel",)),
    )(x, a_out, gate, g, scale, shift)


def _shift_down(main, halo, k):
    ext = jnp.concatenate([halo, main], axis=0)
    return pltpu.roll(ext, k, 0)[halo.shape[0]:]


def _shift_up(main, halo, k):
    ext = jnp.concatenate([main, halo], axis=0)
    n = ext.shape[0]
    return pltpu.roll(ext, n - k, 0)[:main.shape[0]]


def _conv(up, up_halo, w_ref, b_ref):
    return (w_ref[2:3, :] * up + w_ref[1:2, :] * _shift_down(up, up_halo, 1)
            + w_ref[0:1, :] * _shift_down(up, up_halo, 2) + b_ref[...])


def _prev_halo_map(tm):
    step = tm // HALO
    return lambda j, i: (jnp.maximum(i * step - 1, 0), j)


MLP_TM = 512
MLP_CT = 256
CARRY = 8


def _mlp_up(h, wg, wv, cwg, cwv, cbg, cbv):
    S, D = h.shape
    F = wg.shape[1]
    tm, ct = _tile(S, MLP_TM), _tile(F, MLP_CT)
    nct = F // ct

    def body(h_ref, wg_ref, wv_ref, cwg_ref, cwv_ref, cbg_ref, cbv_ref,
             upg_ref, upv_ref, act_ref, actt_ref, hg_scr, hv_scr):
        i, j = pl.program_id(0), pl.program_id(1)
        hv = h_ref[...]
        us = []
        for w_ref, cw_ref, cb_ref, up_ref, scr in ((wg_ref, cwg_ref, cbg_ref, upg_ref, hg_scr),
                                                   (wv_ref, cwv_ref, cbv_ref, upv_ref, hv_scr)):
            up = _dot(hv, w_ref[...]).astype(BF16)
            up_ref[...] = up
            upf = up.astype(F32)
            halo = jnp.where(i == 0, 0.0, scr[j])
            us.append(_conv(upf, halo, cw_ref, cb_ref))
            scr[j] = upf[tm - CARRY:, :]
        act = _silu(us[0]) * us[1]
        act_ref[...] = act.astype(BF16)
        actt_ref[...] = act.T.astype(BF16)

    blk = pl.BlockSpec((tm, ct), lambda i, j: (i, j))
    wspec = pl.BlockSpec((D, ct), lambda i, j: (0, j))
    cwspec = pl.BlockSpec((CONV_W, ct), lambda i, j: (0, j))
    cbspec = pl.BlockSpec((1, ct), lambda i, j: (0, j))
    sds = jax.ShapeDtypeStruct((S, F), BF16)
    return pl.pallas_call(
        body, name="mlp_up", grid=(S // tm, nct),
        out_shape=(sds, sds, sds, jax.ShapeDtypeStruct((F, S), BF16)),
        in_specs=[pl.BlockSpec((tm, D), lambda i, j: (i, 0)), wspec, wspec, cwspec, cwspec, cbspec, cbspec],
        out_specs=(blk, blk, blk, pl.BlockSpec((ct, tm), lambda i, j: (j, i))),
        scratch_shapes=[pltpu.VMEM((nct, CARRY, ct), F32), pltpu.VMEM((nct, CARRY, ct), F32)],
        compiler_params=_params(dimension_semantics=("arbitrary", "arbitrary")),
    )(h, wg, wv, cwg, cwv, cbg, cbv)


def _conv_act_bwd(dact, up_g, up_v, cwg, cwv, cbg, cbv):
    S, F = up_g.shape
    tm, ct = _tile(S, 256), _tile(F, CONV_COLS)
    nct = F // ct

    def body(da_ref, ug_ref, uv_ref, hg_ref, hv_ref, wg_ref, wv_ref, bg_ref, bv_ref,
             dug_ref, duv_ref, pg_ref, pv_ref):
        first = pl.program_id(1) == 0

        @pl.when(first)
        def _():
            pg_ref[...] = jnp.zeros_like(pg_ref)
            pv_ref[...] = jnp.zeros_like(pv_ref)

        da = da_ref[...].astype(F32)
        taps = []
        for u_ref, h_ref in ((ug_ref, hg_ref), (uv_ref, hv_ref)):
            h = jnp.where(first, 0.0, h_ref[...].astype(F32))
            uu = u_ref[...].astype(F32)
            taps.append((_shift_down(uu, h, 2), _shift_down(uu, h, 1), uu))
        u_g = wg_ref[0:1, :] * taps[0][0] + wg_ref[1:2, :] * taps[0][1] + wg_ref[2:3, :] * taps[0][2] + bg_ref[...]
        u_v = wv_ref[0:1, :] * taps[1][0] + wv_ref[1:2, :] * taps[1][1] + wv_ref[2:3, :] * taps[1][2] + bv_ref[...]
        sg = 1.0 / (1.0 + jnp.exp(-u_g))
        du_g = da * u_v * (sg * (1.0 + u_g * (1.0 - sg)))
        du_v = da * (u_g * sg)
        dug_ref[...] = du_g.astype(BF16)
        duv_ref[...] = du_v.astype(BF16)
        for du, tp, p_ref in ((du_g, taps[0], pg_ref), (du_v, taps[1], pv_ref)):
            for k in range(CONV_W):
                p_ref[k:k + 1, :] += jnp.sum(du * tp[k], axis=0, keepdims=True)
            p_ref[CONV_W:CONV_W + 1, :] += jnp.sum(du, axis=0, keepdims=True)

    main = pl.BlockSpec((tm, ct), lambda j, i: (i, j))
    halo = pl.BlockSpec((HALO, ct), _prev_halo_map(tm))
    wspec = pl.BlockSpec((CONV_W, ct), lambda j, i: (0, j))
    bspec = pl.BlockSpec((1, ct), lambda j, i: (0, j))
    pspec = pl.BlockSpec((8, ct), lambda j, i: (0, j))
    return pl.pallas_call(
        body, name="conv_act_bwd", grid=(nct, S // tm),
        out_shape=(jax.ShapeDtypeStruct((S, F), BF16), jax.ShapeDtypeStruct((S, F), BF16),
                   jax.ShapeDtypeStruct((8, F), F32), jax.ShapeDtypeStruct((8, F), F32)),
        in_specs=[main, main, main, halo, halo, wspec, wspec, bspec, bspec],
        out_specs=(main, main, pspec, pspec),
        compiler_params=_params(dimension_semantics=("parallel", "arbitrary")),
    )(dact, up_g, up_v, up_g, up_v, cwg, cwv, cbg, cbv)


def _conv_bwd_input(du, cw, name):
    S, C = du.shape
    tm, ct = _tile(S, 256), _tile(C, CONV_COLS)
    step = tm // HALO
    last_halo = S // HALO - 1

    def body(du_ref, h_ref, w_ref, o_ref):
        last = pl.program_id(1) == pl.num_programs(1) - 1
        d = du_ref[...].astype(F32)
        h = jnp.where(last, 0.0, h_ref[...].astype(F32))
        o_ref[...] = (w_ref[2:3, :] * d + w_ref[1:2, :] * _shift_up(d, h, 1)
                      + w_ref[0:1, :] * _shift_up(d, h, 2)).astype(BF16)

    return pl.pallas_call(
        body, name=name, out_shape=jax.ShapeDtypeStruct((S, C), BF16), grid=(C // ct, S // tm),
        in_specs=[pl.BlockSpec((tm, ct), lambda j, i: (i, j)),
                  pl.BlockSpec((HALO, ct), lambda j, i: (jnp.minimum((i + 1) * step, last_halo), j)),
                  pl.BlockSpec((CONV_W, ct), lambda j, i: (0, j))],
        out_specs=pl.BlockSpec((tm, ct), lambda j, i: (i, j)),
        compiler_params=_params(dimension_semantics=("parallel", "parallel")),
    )(du, du, cw)


def _scan_mats(R, nc, reverse):
    i = lax.broadcasted_iota(jnp.int32, (LANES, LANES), 0)
    j = lax.broadcasted_iota(jnp.int32, (LANES, LANES), 1)
    inner = ((i >= j) if reverse else (i <= j)).astype(F32)
    r = lax.broadcasted_iota(jnp.int32, (R, R), 0)
    c = lax.broadcasted_iota(jnp.int32, (R, R), 1)
    same = (r // nc) == (c // nc)
    outer = (same & ((c > r) if reverse else (c < r))).astype(F32)
    return inner, outer


def _chunk_scan(v, inner, outer, reverse):
    w = _dot(v, inner, precision=HIGHEST)
    col = 0 if reverse else LANES - 1
    carry = _dot(outer, w, precision=HIGHEST)[:, col:col + 1]
    return w + carry


def _fgate_fwd(z_rows, nc):
    R = z_rows.shape[0]

    def body(z_ref, f_ref):
        z = z_ref[...]
        logf = jnp.minimum(z, 0.0) - jnp.log(1.0 + jnp.exp(-jnp.abs(z)))
        inner, outer = _scan_mats(R, nc, False)
        f_ref[...] = _chunk_scan(logf, inner, outer, False)

    return pl.pallas_call(body, name="fgate_fwd", out_shape=jax.ShapeDtypeStruct((R, LANES), F32),
                          compiler_params=_params())(z_rows)


def _fgate_bwd(dfk_neg_rows, dfq_rows, z_rows, nc):
    R = z_rows.shape[0]
    nh = R // nc

    def body(dfk_ref, dfq_ref, z_ref, dz_ref, db_ref):
        inner, outer = _scan_mats(R, nc, True)
        dlogf = _chunk_scan(dfq_ref[...] - dfk_ref[...], inner, outer, True)
        dz = dlogf * (1.0 / (1.0 + jnp.exp(z_ref[...])))
        dz_ref[...] = dz
        hr = lax.broadcasted_iota(jnp.int32, (nh, R), 0)
        hc = lax.broadcasted_iota(jnp.int32, (nh, R), 1) // nc
        per_head = _dot((hr == hc).astype(F32), dz, precision=HIGHEST)
        db_ref[...] = jnp.sum(per_head, axis=1, keepdims=True)

    return pl.pallas_call(
        body, name="fgate_bwd",
        out_shape=(jax.ShapeDtypeStruct((R, LANES), F32), jax.ShapeDtypeStruct((nh, 1), F32)),
        compiler_params=_params())(dfk_neg_rows, dfq_rows, z_rows)


_NEG = -1e30
SKIP_BELOW = -106.0
_SCALE = HEAD_DIM ** -0.5
N_SCAN = ATT_BK // SCAN_BK
F_PARTS = 3
Q_F_LANE = HEAD_DIM
Q_ONE_LANE = HEAD_DIM + F_PARTS


def _kv_slice(j):
    return pl.ds(pl.multiple_of(j * ATT_BK, ATT_BK), ATT_BK)


def _mask_t(strict):
    s = lax.broadcasted_iota(jnp.int32, (ATT_BK, ATT_BQ), 0)
    t = lax.broadcasted_iota(jnp.int32, (ATT_BK, ATT_BQ), 1)
    return (s < t) if strict else (s <= t)


def _walk_down(i, step, alive, carry):
    carry = step(i, carry, True)

    def cond(st):
        n, go, _ = st
        return jnp.logical_and(n < i, go)

    def body(st):
        n, _, cr = st
        j = i - 1 - n
        cr = step(j, cr, False)
        return n + 1, alive(jnp.maximum(j - 1, 0), cr), cr

    return lax.while_loop(cond, body, (jnp.int32(0), alive(jnp.maximum(i - 1, 0), carry), carry))[2]


def _t_block(rows):
    return pl.BlockSpec((None, rows, ATT_BQ), lambda h, i, *_: (h, 0, i))


def _t_full(rows, S):
    return pl.BlockSpec((None, rows, S), lambda h, i, *_: (h, 0, 0))


def _n_block():
    return pl.BlockSpec((None, ATT_BQ, LANES), lambda h, i, *_: (h, i, 0))


def _n_full(S):
    return pl.BlockSpec((None, S, LANES), lambda h, i, *_: (h, 0, 0))


def _heads(t):
    S = t.shape[0]
    return jnp.transpose(t.reshape(S, -1, HEAD_DIM), (1, 0, 2))


def _unheads(t):
    return jnp.transpose(t, (1, 0, 2)).reshape(t.shape[1], -1)


def _tr(t):
    return jnp.transpose(t, (0, 2, 1))


def _skip_bounds(k_cols, f_rows):
    H, S = f_rows.shape
    f_end = f_rows.reshape(H, S // ATT_BK, ATT_BK)[:, :, -1]
    k_sq = jnp.sum(jnp.square(_heads(k_cols).astype(F32)), axis=-1).reshape(H, S // ATT_BK, ATT_BK)
    k_max = lax.cummax(jnp.sqrt(jnp.max(k_sq, axis=-1)), axis=1)
    return f_end, k_max


def _bf16_parts(f):
    hi = f.astype(BF16).astype(F32)
    mid = (f - hi).astype(BF16).astype(F32)
    return hi, mid, (f - hi - mid).astype(BF16).astype(F32)


def _att_prep(qkv, f_pairs):
    S = qkv.shape[0]
    n_pairs = qkv.shape[1] // (6 * LANES)
    H = 2 * n_pairs
    tm = _tile(S, 512)

    def body(qf_ref, kf_ref, vf_ref, qs_ref, ks_ref, vs_ref, f_ref,
             fqn, fqt, fkn, fkt, fvn, fvt, sqn, sqt, skn, skt, svn, svt):
        lane = lax.broadcasted_iota(jnp.int32, (1, LANES), 1)
        f = f_ref[...]

        def head(ref, e):
            t = ref[...].astype(F32)
            if e == 1:
                t = pltpu.roll(t, HEAD_DIM, 1)
            return jnp.where(lane < HEAD_DIM, t, 0.0)

        def at(first):
            return jnp.logical_and(lane >= first, lane < first + F_PARTS)

        for e in range(2):
            parts = _bf16_parts(f[:, e:e + 1])
            f_lanes = sum(jnp.where(lane == Q_F_LANE + k, parts[k], 0.0) for k in range(F_PARTS))
            nf_lanes = sum(jnp.where(lane == Q_ONE_LANE + k, parts[k], 0.0) for k in range(F_PARTS))
            vals = (
                (fqn, fqt, LANES, head(qf_ref, e) * _SCALE + f_lanes + jnp.where(at(Q_ONE_LANE), 1.0, 0.0)),
                (fkn, fkt, LANES, head(kf_ref, e) + jnp.where(at(Q_F_LANE), 1.0, 0.0) - nf_lanes),
                (fvn, fvt, HEAD_DIM, head(vf_ref, e)),
                (sqn, sqt, LANES, head(qs_ref, e) * _SCALE),
                (skn, skt, LANES, head(ks_ref, e)),
                (svn, svt, HEAD_DIM, head(vs_ref, e)),
            )
            for n_ref, t_ref, rows, val in vals:
                n_ref[e] = val.astype(BF16)
                t_ref[e] = val.T[:rows].astype(BF16)

    col = lambda base: pl.BlockSpec((tm, LANES), lambda i, p: (i, base + p))
    n_spec = pl.BlockSpec((2, tm, LANES), lambda i, p: (p, i, 0))
    t_spec = lambda rows: pl.BlockSpec((2, rows, tm), lambda i, p: (p, 0, i))
    n_sds = jax.ShapeDtypeStruct((H, S, LANES), BF16)
    t_sds = lambda rows: jax.ShapeDtypeStruct((H, rows, S), BF16)
    group = ([n_sds, t_sds(LANES), n_sds, t_sds(LANES), n_sds, t_sds(HEAD_DIM)],
             [n_spec, t_spec(LANES), n_spec, t_spec(LANES), n_spec, t_spec(HEAD_DIM)])
    res = pl.pallas_call(
        body, name="att_prep", grid=(S // tm, n_pairs),
        out_shape=tuple(group[0] * 2),
        in_specs=[col(k * n_pairs) for k in range(6)] + [pl.BlockSpec((None, tm, 2), lambda i, p: (p, i, 0))],
        out_specs=tuple(group[1] * 2),
        compiler_params=_params(dimension_semantics=("parallel", "parallel")),
    )(qkv, qkv, qkv, qkv, qkv, qkv, f_pairs)
    names = ("q_n", "q_t", "k_n", "k_t", "v_n", "v_t")
    return dict(zip(names, res[:6])), dict(zip(names, res[6:]))


def _fox_reach(qt, fend_ref, kmax_ref, h):
    qf = qt.astype(F32)
    q_norm = jnp.sqrt(jnp.sum(jnp.square(qf[:HEAD_DIM]), axis=0, keepdims=True))
    f_t = jnp.sum(qf[Q_F_LANE:Q_F_LANE + F_PARTS], axis=0, keepdims=True)
    return lambda j: q_norm * kmax_ref[h, j] + f_t - fend_ref[h, j]


def _fox_fwd(q_t, k_n, v_t, f_end, k_max, shards):
    H, _, S = q_t.shape
    n, nq = len(shards), S // ATT_BQ

    def body(fend_ref, kmax_ref, qt_ref, k_ref, vt_ref, *rest):
        ins, (ot_ref, lse_ref), outs, sems = rest[:n], rest[n:n + 2], rest[n + 2:2 * n + 2], rest[2 * n + 2:]
        h, i = pl.program_id(0), pl.program_id(1)

        @pl.when(jnp.logical_and(h == 0, i == 0))
        def _():
            for cp in _exchange_copies(ins, outs, *sems, False):
                cp.start()

        qt = qt_ref[...]
        reach = _fox_reach(qt, fend_ref, kmax_ref, h)

        def step(j, carry, masked):
            m, l, acc = carry
            ks = _kv_slice(j)
            s = _dot(k_ref[ks, :], qt)
            if masked:
                s = jnp.where(_mask_t(False), s, _NEG)
            mn = jnp.maximum(m, jnp.max(s, axis=0, keepdims=True))
            alpha = jnp.exp(m - mn)
            p = jnp.exp(s - mn)
            l = alpha * l + jnp.sum(p, axis=0, keepdims=True)
            acc = acc * alpha + _dot(vt_ref[:, ks], p.astype(BF16))
            return mn, l, acc

        def alive(j, carry):
            return jnp.max(reach(j) - carry[0]) > SKIP_BELOW

        row = jnp.zeros((1, ATT_BQ), F32)
        m, l, acc = _walk_down(i, step, alive, (row + _NEG, row, jnp.zeros((HEAD_DIM, ATT_BQ), F32)))
        ot_ref[...] = acc / l
        lse_ref[...] = m + jnp.log(l)

        @pl.when(jnp.logical_and(h == H - 1, i == nq - 1))
        def _():
            for cp in _exchange_copies(ins, outs, *sems, False):
                cp.wait()

    any_spec = pl.BlockSpec(memory_space=pl.ANY)
    grid_spec = pltpu.PrefetchScalarGridSpec(
        num_scalar_prefetch=2, grid=(H, nq),
        in_specs=[_t_block(LANES), _n_full(S), _t_full(HEAD_DIM, S)] + [any_spec] * n,
        out_specs=tuple([_t_block(HEAD_DIM), _t_block(1)] + [any_spec] * n),
        scratch_shapes=_exchange_sems(n))
    res = pl.pallas_call(
        body, name="fox_fwd", grid_spec=grid_spec,
        out_shape=tuple([jax.ShapeDtypeStruct((H, HEAD_DIM, S), F32), jax.ShapeDtypeStruct((H, 1, S), F32)]
                        + _exchange_out_shapes(shards, False)),
        compiler_params=_params(dimension_semantics=("arbitrary", "arbitrary"), has_side_effects=True),
    )(f_end, k_max, q_t, k_n, v_t, *shards)
    return res[0], res[1], res[2:]


def _fox_bwd(q_t, q_n, k_n, k_t, v_n, do_t, do_n, o_t, lse, f_end, k_max):
    H, _, S = q_t.shape

    def body(fend_ref, kmax_ref, qt_ref, qn_ref, k_ref, kt_ref, v_ref, dot_ref, don_ref, ot_ref, lse_ref,
             dqt_ref, dk_ref, dv_ref):
        h, i = pl.program_id(0), pl.program_id(1)

        @pl.when(i == 0)
        def _():
            dk_ref[...] = jnp.zeros_like(dk_ref)
            dv_ref[...] = jnp.zeros_like(dv_ref)

        qt, qn, dot, don = qt_ref[...], qn_ref[...], dot_ref[...], don_ref[...]
        lse = lse_ref[...]
        delta = jnp.sum(dot[:HEAD_DIM].astype(F32) * ot_ref[...], axis=0, keepdims=True)
        reach = _fox_reach(qt, fend_ref, kmax_ref, h)

        def alive(j, dq):
            return jnp.max(reach(j) - lse) > SKIP_BELOW

        def step(j, dq, masked):
            ks = _kv_slice(j)
            s = _dot(k_ref[ks, :], qt)
            if masked:
                s = jnp.where(_mask_t(False), s, _NEG)
            p = jnp.exp(s - lse)
            ds = (p * (_dot(v_ref[ks, :], dot) - delta)).astype(BF16)
            dk_ref[ks, :] += _dot(ds, qn)
            dv_ref[ks, :] += _dot(p.astype(BF16), don)
            return dq + _dot(kt_ref[:, ks], ds)

        dqt_ref[...] = _walk_down(i, step, alive, jnp.zeros((LANES, ATT_BQ), F32))

    grid_spec = pltpu.PrefetchScalarGridSpec(
        num_scalar_prefetch=2, grid=(H, S // ATT_BQ),
        in_specs=[_t_block(LANES), _n_block(), _n_full(S), _t_full(LANES, S), _n_full(S),
                  _t_block(LANES), _n_block(), _t_block(HEAD_DIM), _t_block(1)],
        out_specs=(_t_block(LANES), _n_full(S), _n_full(S)))
    return pl.pallas_call(
        body, name="fox_bwd", grid_spec=grid_spec,
        out_shape=(jax.ShapeDtypeStruct((H, LANES, S), F32), jax.ShapeDtypeStruct((H, S, LANES), F32),
                   jax.ShapeDtypeStruct((H, S, LANES), F32)),
        compiler_params=_params(dimension_semantics=("parallel", "arbitrary")),
    )(f_end, k_max, q_t, q_n, k_n, k_t, v_n, do_t, do_n, o_t, lse)


def _scan_lhs():
    r = lax.broadcasted_iota(jnp.int32, (SCAN_BK, 2 * SCAN_BK), 0)
    c = lax.broadcasted_iota(jnp.int32, (SCAN_BK, 2 * SCAN_BK), 1) % SCAN_BK
    return (c >= r).astype(BF16)


def _suffix_sum(t, lhs):
    hi = t.astype(BF16)
    lo = (t - hi.astype(F32)).astype(BF16)
    return _dot(lhs, jnp.concatenate([hi, lo], axis=0))


def _sb_scores(k, qt, mask):
    z = _dot(k, qt)
    e = jnp.exp(-jnp.abs(z))
    lb = -(jnp.maximum(z, 0.0) + jnp.log(1.0 + e))
    if mask is not None:
        lb = jnp.where(mask, lb, 0.0)
    return z, e, lb


def _scan_blocks():
    return [slice(u * SCAN_BK, (u + 1) * SCAN_BK) for u in reversed(range(N_SCAN))]


def _sb_fwd(q_t, k_n, v_t):
    H, _, S = q_t.shape

    def body(qt_ref, k_ref, vt_ref, ot_ref):
        i = pl.program_id(1)
        qt = qt_ref[...]
        lhs = _scan_lhs()

        def step(j, carry, masked):
            c, acc = carry
            ks = _kv_slice(j)
            mask = _mask_t(True) if masked else None
            z, _, lb = _sb_scores(k_ref[ks, :], qt, mask)
            parts = []
            for sl in _scan_blocks():
                rin = _suffix_sum(lb[sl], lhs)
                a = jnp.exp(z[sl] + rin + c)
                if masked:
                    a = jnp.where(mask[sl], a, 0.0)
                parts.append(a.astype(BF16))
                c = c + rin[0:1, :]
            a_all = jnp.concatenate(parts[::-1], axis=0)
            return c, acc + _dot(vt_ref[:, ks], a_all)

        carry = (jnp.zeros((1, ATT_BQ), F32), jnp.zeros((HEAD_DIM, ATT_BQ), F32))
        ot_ref[...] = _walk_down(i, step, lambda j, cr: jnp.max(cr[0]) > SKIP_BELOW, carry)[1]

    return pl.pallas_call(
        body, name="sb_fwd", grid=(H, S // ATT_BQ),
        out_shape=jax.ShapeDtypeStruct((H, HEAD_DIM, S), F32),
        in_specs=[_t_block(LANES), _n_full(S), _t_full(HEAD_DIM, S)],
        out_specs=_t_block(HEAD_DIM),
        compiler_params=_params(dimension_semantics=("parallel", "parallel")),
    )(q_t, k_n, v_t)


def _sb_bwd(q_t, q_n, k_n, k_t, v_n, do_t, do_n, o_t, bound):
    H, _, S = q_t.shape
    n, nq = len(bound), S // ATT_BQ

    def body(qt_ref, qn_ref, k_ref, kt_ref, v_ref, dot_ref, don_ref, ot_ref, *rest):
        ins, (dqt_ref, dk_ref, dv_ref) = rest[:n], rest[n:n + 3]
        outs, sems = rest[n + 3:2 * n + 3], rest[2 * n + 3:]
        h, i = pl.program_id(0), pl.program_id(1)

        @pl.when(jnp.logical_and(h == 0, i == 0))
        def _():
            for cp in _exchange_copies(ins, outs, *sems, True):
                cp.start()

        @pl.when(i == 0)
        def _():
            dk_ref[...] = jnp.zeros_like(dk_ref)
            dv_ref[...] = jnp.zeros_like(dv_ref)

        qt, qn, dot, don = qt_ref[...], qn_ref[...], dot_ref[...], don_ref[...]
        lhs = _scan_lhs()
        delta = jnp.sum(dot[:HEAD_DIM].astype(F32) * ot_ref[...], axis=0, keepdims=True)

        def step(j, carry, masked):
            c, g, dq = carry
            ks = _kv_slice(j)
            mask = _mask_t(True) if masked else None
            z, e, lb = _sb_scores(k_ref[ks, :], qt, mask)
            da = _dot(v_ref[ks, :], dot)
            a_parts, dz_parts = [], []
            for sl in _scan_blocks():
                rin = _suffix_sum(lb[sl], lhs)
                a = jnp.exp(z[sl] + rin + c)
                if masked:
                    a = jnp.where(mask[sl], a, 0.0)
                ab = a.astype(BF16)
                gg = ab.astype(F32) * da[sl]
                rgin = _suffix_sum(gg, lhs)
                rinv = 1.0 / (1.0 + e[sl])
                sig = jnp.where(z[sl] >= 0.0, rinv, e[sl] * rinv)
                dz = gg - sig * (delta - g - (rgin - gg))
                if masked:
                    dz = jnp.where(mask[sl], dz, 0.0)
                a_parts.append(ab)
                dz_parts.append(dz.astype(BF16))
                c = c + rin[0:1, :]
                g = g + rgin[0:1, :]
            ab_all = jnp.concatenate(a_parts[::-1], axis=0)
            dzb = jnp.concatenate(dz_parts[::-1], axis=0)
            dk_ref[ks, :] += _dot(dzb, qn)
            dv_ref[ks, :] += _dot(ab_all, don)
            return c, g, dq + _dot(kt_ref[:, ks], dzb)

        row = jnp.zeros((1, ATT_BQ), F32)
        carry = (row, row, jnp.zeros((LANES, ATT_BQ), F32))
        dqt_ref[...] = _walk_down(i, step, lambda j, cr: jnp.max(cr[0]) > SKIP_BELOW, carry)[2]

        @pl.when(jnp.logical_and(h == H - 1, i == nq - 1))
        def _():
            for cp in _exchange_copies(ins, outs, *sems, True):
                cp.wait()

    any_spec = pl.BlockSpec(memory_space=pl.ANY)
    res = pl.pallas_call(
        body, name="sb_bwd", grid=(H, nq),
        out_shape=tuple([jax.ShapeDtypeStruct((H, LANES, S), F32), jax.ShapeDtypeStruct((H, S, LANES), F32),
                         jax.ShapeDtypeStruct((H, S, LANES), F32)] + _exchange_out_shapes(bound, True)),
        in_specs=[_t_block(LANES), _n_block(), _n_full(S), _t_full(LANES, S), _n_full(S),
                  _t_block(LANES), _n_block(), _t_block(HEAD_DIM)] + [any_spec] * n,
        out_specs=tuple([_t_block(LANES), _n_full(S), _n_full(S)] + [any_spec] * n),
        scratch_shapes=_exchange_sems(n),
        compiler_params=_params(dimension_semantics=("arbitrary", "arbitrary"), has_side_effects=True),
    )(q_t, q_n, k_n, k_t, v_n, do_t, do_n, o_t, *bound)
    return res[0], res[1], res[2], res[3:]


def _dqkv_assemble(dqf_t, dkf, dvf, dqs_t, dks, dvs):
    H, _, S = dqf_t.shape
    n_pairs = H // 2
    tm = _tile(S, 512)

    def body(dqf_ref, dkf_ref, dvf_ref, dqs_ref, dks_ref, dvs_ref, out_ref, dfk_ref):
        lane = lax.broadcasted_iota(jnp.int32, (1, LANES), 1)
        slabs = ((dqf_ref, True), (dkf_ref, False), (dvf_ref, False),
                 (dqs_ref, True), (dks_ref, False), (dvs_ref, False))
        for k, (ref, transposed) in enumerate(slabs):
            if transposed:
                t0, t1 = ref[0].T * _SCALE, ref[1].T * _SCALE
            else:
                t0, t1 = ref[0], ref[1]
            out_ref[k] = jnp.where(lane < HEAD_DIM, t0, pltpu.roll(t1, HEAD_DIM, 1)).astype(BF16)
        for e in range(2):
            dfk_ref[e] = dkf_ref[e].T[Q_ONE_LANE:Q_ONE_LANE + 1, :]

    t_spec = pl.BlockSpec((2, LANES, tm), lambda i, p: (p, 0, i))
    n_spec = pl.BlockSpec((2, tm, LANES), lambda i, p: (p, i, 0))
    return pl.pallas_call(
        body, name="dqkv_assemble", grid=(S // tm, n_pairs),
        out_shape=(jax.ShapeDtypeStruct((6, S, n_pairs * LANES), BF16), jax.ShapeDtypeStruct((H, 1, S), F32)),
        in_specs=[t_spec, n_spec, n_spec, t_spec, n_spec, n_spec],
        out_specs=(pl.BlockSpec((6, tm, LANES), lambda i, p: (0, i, p)),
                   pl.BlockSpec((2, 1, tm), lambda i, p: (p, 0, i))),
        compiler_params=_params(dimension_semantics=("parallel", "parallel")),
    )(dqf_t, dkf, dvf, dqs_t, dks, dvs)


def _acc_spec(w):
    return pl.BlockSpec((1, w), lambda i: (0, 0))


def _loss_head(x1, m_out, gate_m, g_final, target):
    S, D = x1.shape
    tm = _tile(S, 256)

    def body(x1_ref, mo_ref, gt_ref, gf_ref, tg_ref, dx2_ref, gm_ref, loss_ref, dgf_ref, dgt_ref):
        @pl.when(pl.program_id(0) == 0)
        def _():
            loss_ref[...] = jnp.zeros_like(loss_ref)
            dgf_ref[...] = jnp.zeros_like(dgf_ref)
            dgt_ref[...] = jnp.zeros_like(dgt_ref)

        mo = mo_ref[...]
        x2 = x1_ref[...] + gt_ref[...] * mo
        r = lax.rsqrt(jnp.mean(x2 * x2, axis=-1, keepdims=True) + EPS)
        xh = x2 * r
        diff = xh * gf_ref[...] - tg_ref[...]
        loss_ref[...] += (0.5 / D) * jnp.sum(diff * diff)
        dy = diff * (1.0 / D)
        dgf_ref[...] += jnp.sum(dy * xh, axis=0, keepdims=True)
        dxh = dy * gf_ref[...]
        dx2 = r * (dxh - xh * jnp.mean(dxh * xh, axis=-1, keepdims=True))
        dx2_ref[...] = dx2
        gm_ref[...] = (dx2 * gt_ref[...]).astype(BF16)
        dgt_ref[...] += jnp.sum(dx2 * mo, axis=0, keepdims=True)

    return pl.pallas_call(
        body, name="loss_head", grid=(S // tm,),
        out_shape=(jax.ShapeDtypeStruct((S, D), F32), jax.ShapeDtypeStruct((S, D), BF16),
                   jax.ShapeDtypeStruct((1, LANES), F32), jax.ShapeDtypeStruct((1, D), F32),
                   jax.ShapeDtypeStruct((1, D), F32)),
        in_specs=_row_specs(tm, [D, D]) + [_vec_spec(D)] * 2 + _row_specs(tm, [D]),
        out_specs=tuple(_row_specs(tm, [D, D]) + [_acc_spec(LANES), _acc_spec(D), _acc_spec(D)]),
        compiler_params=_params(dimension_semantics=("arbitrary",)),
    )(x1, m_out, gate_m, g_final, target)


def _norm_bwd(dh, xin, dres, g, scale, name, gate=None, branch=None):
    S, D = xin.shape
    tm = _tile(S, 256)
    gated = gate is not None

    def body(*refs):
        if gated:
            (dh_ref, x_ref, dr_ref, g_ref, sc_ref, gt_ref, br_ref,
             dx_ref, dsc_ref, dsh_ref, dg_ref, ga_ref, dgt_ref) = refs
            sums = (dsc_ref, dsh_ref, dg_ref, dgt_ref)
        else:
            dh_ref, x_ref, dr_ref, g_ref, sc_ref, dx_ref, dsc_ref, dsh_ref, dg_ref = refs
            sums = (dsc_ref, dsh_ref, dg_ref)

        @pl.when(pl.program_id(0) == 0)
        def _():
            for s_ref in sums:
                s_ref[...] = jnp.zeros_like(s_ref)

        dhv, xv = dh_ref[...], x_ref[...]
        r = lax.rsqrt(jnp.mean(xv * xv, axis=-1, keepdims=True) + EPS)
        xh = xv * r
        dsc_ref[...] += jnp.sum(dhv * (xh * g_ref[...]), axis=0, keepdims=True)
        dsh_ref[...] += jnp.sum(dhv, axis=0, keepdims=True)
        dn = dhv * (1.0 + sc_ref[...])
        dg_ref[...] += jnp.sum(dn * xh, axis=0, keepdims=True)
        dxh = dn * g_ref[...]
        dx = dr_ref[...] + r * (dxh - xh * jnp.mean(dxh * xh, axis=-1, keepdims=True))
        dx_ref[...] = dx
        if gated:
            ga_ref[...] = (dx * gt_ref[...]).astype(BF16)
            dgt_ref[...] += jnp.sum(dx * br_ref[...], axis=0, keepdims=True)

    vec = jax.ShapeDtypeStruct((1, D), F32)
    out_shape = [jax.ShapeDtypeStruct((S, D), F32), vec, vec, vec]
    out_specs = _row_specs(tm, [D]) + [_acc_spec(D)] * 3
    in_specs = _row_specs(tm, [D, D, D]) + [_vec_spec(D)] * 2
    args = [dh, xin, dres, g, scale]
    if gated:
        out_shape += [jax.ShapeDtypeStruct((S, D), BF16), vec]
        out_specs += _row_specs(tm, [D]) + [_acc_spec(D)]
        in_specs += [_vec_spec(D)] + _row_specs(tm, [D])
        args += [gate, branch]
    return pl.pallas_call(
        body, name=name, grid=(S // tm,), out_shape=tuple(out_shape),
        in_specs=in_specs, out_specs=tuple(out_specs),
        compiler_params=_params(dimension_semantics=("arbitrary",)),
    )(*args)


def _headnorm_bwd(dmix, o_f, o_s, g_f, g_s):
    S, dh = o_f.shape
    H = dh // HEAD_DIM
    tm = _tile(S, 256)

    def body(dm_ref, of_ref, os_ref, gf_ref, gs_ref, fn_ref, ft_ref, sn_ref, st_ref, dgf_ref, dgs_ref):
        @pl.when(pl.program_id(0) == 0)
        def _():
            dgf_ref[...] = jnp.zeros_like(dgf_ref)
            dgs_ref[...] = jnp.zeros_like(dgs_ref)

        ones = _group_ones()
        lane = lax.broadcasted_iota(jnp.int32, (1, LANES), 1)
        parts = ((of_ref, gf_ref, fn_ref, ft_ref, dgf_ref), (os_ref, gs_ref, sn_ref, st_ref, dgs_ref))
        for part, (o_ref, g_ref, n_ref, t_ref, dg_ref) in enumerate(parts):
            for t in range(dh // LANES):
                cols = slice(t * LANES, (t + 1) * LANES)
                o = o_ref[:, cols]
                dm = dm_ref[:, part * dh + t * LANES: part * dh + (t + 1) * LANES]
                r = lax.rsqrt(_dot(o * o, ones, precision=HIGHEST) * (1.0 / HEAD_DIM) + EPS)
                oh = o * r
                dg_ref[:, cols] += jnp.sum(dm * oh, axis=0, keepdims=True)
                dn = dm * g_ref[:, cols]
                mean = _dot(dn * oh, ones, precision=HIGHEST) * (1.0 / HEAD_DIM)
                do = r * (dn - oh * mean)
                for e in range(2):
                    d = do if e == 0 else pltpu.roll(do, HEAD_DIM, 1)
                    d = jnp.where(lane < HEAD_DIM, d, 0.0)
                    n_ref[2 * t + e] = d.astype(BF16)
                    t_ref[2 * t + e] = d.T.astype(BF16)

    vec = jax.ShapeDtypeStruct((1, dh), F32)
    n_sds = jax.ShapeDtypeStruct((H, S, LANES), BF16)
    t_sds = jax.ShapeDtypeStruct((H, LANES, S), BF16)
    n_spec = pl.BlockSpec((H, tm, LANES), lambda i: (0, i, 0))
    t_spec = pl.BlockSpec((H, LANES, tm), lambda i: (0, 0, i))
    return pl.pallas_call(
        body, name="headnorm_bwd", grid=(S // tm,),
        out_shape=(n_sds, t_sds, n_sds, t_sds, vec, vec),
        in_specs=_row_specs(tm, [2 * dh, dh, dh]) + [_vec_spec(dh)] * 2,
        out_specs=(n_spec, t_spec, n_spec, t_spec, _acc_spec(dh), _acc_spec(dh)),
        compiler_params=_params(dimension_semantics=("arbitrary",)),
    )(dmix, o_f, o_s, g_f, g_s)


def _adamw(w, gslots, m, v, name):
    R, C = w.shape
    n = gslots.shape[0]
    tr = 256 if (R % 256 == 0 and R > 256) else R
    bc1 = 1.0 - ADAM_B1 ** ADAM_STEP
    bc2 = 1.0 - ADAM_B2 ** ADAM_STEP

    def body(w_ref, gs_ref, m_ref, v_ref, g_ref, d_ref, nm_ref, nv_ref):
        g = gs_ref[0]
        for s in range(1, n):
            g = g + gs_ref[s]
        nm = ADAM_B1 * m_ref[...] + (1.0 - ADAM_B1) * g
        nv = ADAM_B2 * v_ref[...] + (1.0 - ADAM_B2) * (g * g)
        g_ref[...] = g
        nm_ref[...] = nm
        nv_ref[...] = nv
        d_ref[...] = -ADAM_LR * ((nm / bc1) / (jnp.sqrt(nv / bc2) + ADAM_EPS) + ADAM_WD * w_ref[...])

    blk = pl.BlockSpec((tr, C), lambda i: (i, 0))
    sds = jax.ShapeDtypeStruct((R, C), F32)
    return pl.pallas_call(
        body, name=name, grid=(R // tr,), out_shape=(sds,) * 4,
        in_specs=[blk, pl.BlockSpec((n, tr, C), lambda i: (0, i, 0)), blk, blk], out_specs=(blk,) * 4,
        compiler_params=_params(dimension_semantics=("parallel",)),
    )(w, gslots, m, v)


def _slot_sum(slots, name):
    n, _, C = slots.shape

    def body(s_ref, o_ref):
        acc = s_ref[0]
        for s in range(1, n):
            acc = acc + s_ref[s]
        o_ref[...] = acc

    return pl.pallas_call(body, name=name, out_shape=jax.ShapeDtypeStruct((1, C), F32),
                          compiler_params=_params())(slots)


def _pad_cols(a, n):
    return jnp.pad(a, ((0, 0), (0, n - a.shape[1])))


def _ungather(g, axis):
    if axis == 0:
        return g.reshape(g.shape[0] * g.shape[1], g.shape[2])
    return jnp.transpose(g, (1, 0, 2)).reshape(g.shape[1], g.shape[0] * g.shape[2])


def _to_slots(full, axis):
    R, C = full.shape
    if axis == 0:
        return full.reshape(N_DEV, R // N_DEV, C)
    return jnp.transpose(full.reshape(R, N_DEV, C // N_DEV), (1, 0, 2))


def kernel(x, c, w_ada, b_ada, g_attn, w_in, b_fgate, g_out_fox, g_out_sb, w_out, g_mlp, w_up, conv_w, conv_b, w_down, g_final, loss_target, m_w_ada, m_b_ada, m_g_attn, m_w_in, m_b_fgate, m_g_out_fox, m_g_out_sb, m_w_out, m_g_mlp, m_w_up, m_conv_w, m_conv_b, m_w_down, m_g_final, v_w_ada, v_b_ada, v_g_attn, v_w_in, v_b_fgate, v_g_out_fox, v_g_out_sb, v_w_out, v_g_mlp, v_w_up, v_conv_w, v_conv_b, v_w_down, v_g_final):
    S, D = x.shape[1], x.shape[2]
    dh = D // 2
    n_heads = dh // HEAD_DIM
    n_qkv = 6 * dh
    ff = w_down.shape[1] * N_DEV
    ffp = -(-ff // (2 * LANES)) * (2 * LANES)
    nc = S // LANES
    me = 4 * lax.axis_index("x") + 2 * lax.axis_index("y") + lax.axis_index("c")
    xs, tgt = x[0], loss_target[0]

    c_all, win_g = _gather_two_level([c, w_in[0].astype(BF16)], name="gather_first")
    c_all = c_all.reshape(N_DEV, D)
    W_in = _ungather(win_g, 1)
    W_qkv, W_f = W_in[:, :n_qkv], _pad_cols(W_in[:, n_qkv:], LANES)
    cb_g, cb_v = _pad_cols(conv_b[:, :ff], ffp), _pad_cols(conv_b[:, ff:], ffp)

    n_ada = w_ada.shape[2]
    b_shard = lax.dynamic_slice(b_ada, (0, me * n_ada), (1, n_ada))
    mod_cols = _ada_fwd(c_all, w_ada[0], b_shard)
    (mod_g,) = _exchange([mod_cols], scatter=False, name="gather_mod")
    mod = lax.dynamic_index_in_dim(mod_g, me, axis=1, keepdims=False).reshape(6, 1, D)
    shift_a, scale_a, gate_a, shift_m, scale_m, gate_m = [mod[k] for k in range(6)]

    h1, h1_t = _prenorm(xs, g_attn, scale_a, shift_a, "prenorm_attn")
    qkv = _mm(h1, W_qkv, BF16, "proj_qkv")
    flog = _mm(h1, W_f, F32, "proj_fgate")
    zf = flog[:, :n_heads] + b_fgate
    z_rows = zf.T.reshape(n_heads * nc, LANES)
    f_rows = _fgate_fwd(z_rows, nc).reshape(n_heads, S)
    f_pairs = jnp.transpose(f_rows.reshape(n_heads // 2, 2, S), (0, 2, 1))
    fox, sb = _att_prep(qkv, f_pairs)
    f_end, k_max = _skip_bounds(qkv[:, dh:2 * dh], f_rows)
    of_t, lse, (wout_g, wup_g, wdown_g, convw_g) = _fox_fwd(
        fox["q_t"], fox["k_n"], fox["v_t"], f_end, k_max,
        [w_out[0].astype(BF16), w_up[0].astype(BF16), w_down[0].astype(BF16), conv_w[0]])
    W_out = _ungather(wout_g, 0)
    W_up = _ungather(wup_g, 1)
    W_g, W_v = _pad_cols(W_up[:, :ff], ffp), _pad_cols(W_up[:, ff:], ffp)
    W_down = jnp.pad(_ungather(wdown_g, 0), ((0, ffp - ff), (0, 0)))
    cw_full = _ungather(convw_g, 1)
    cw_g, cw_v = _pad_cols(cw_full[:, :ff], ffp), _pad_cols(cw_full[:, ff:], ffp)
    os_t = _sb_fwd(sb["q_t"], sb["k_n"], sb["v_t"])
    o_f, o_s = _unheads(_tr(of_t)), _unheads(_tr(os_t))
    mix, mix_t = _headnorm_fwd(o_f, o_s, g_out_fox, g_out_sb)
    a_out = _mm(mix, W_out, F32, "proj_out")
    x1, h2, h2_t = _resid_prenorm(xs, a_out, gate_a, g_mlp, scale_m, shift_m)
    up_g, up_v, act, act_t = _mlp_up(h2, W_g, W_v, cw_g, cw_v, cb_g, cb_v)
    m_out = _mm(act, W_down, F32, "proj_down")

    dx2, gm, loss_p, dg_final, dgate_m = _loss_head(x1, m_out, gate_m, g_final.reshape(1, D), tgt)
    dact = _mm(gm, W_down, BF16, "bwd_down_act", tn=1408, nt=True)
    dW_down = _mm_acc(act_t, gm, "bwd_down_w")
    du_g, du_v, p_g, p_v = _conv_act_bwd(dact, up_g, up_v, cw_g, cw_v, cb_g, cb_v)
    dup_g = _conv_bwd_input(du_g, cw_g, "conv_bwd_input_g")
    dup_v = _conv_bwd_input(du_v, cw_v, "conv_bwd_input_v")
    dh2 = _mm2(dup_g, W_g, dup_v, W_v, F32, "bwd_up_act", nt=True)
    dW_g = _mm_acc(h2_t, dup_g, "bwd_up_w_g")
    dW_v = _mm_acc(h2_t, dup_v, "bwd_up_w_v")
    dx1, dscale_m, dshift_m, dg_mlp, ga, dgate_a = _norm_bwd(
        dh2, x1, dx2, g_mlp, scale_m, "norm_mlp_bwd", gate=gate_a, branch=a_out)
    dmix = _mm(ga, W_out, F32, "bwd_out_act", nt=True)
    dW_out = _mm_acc(mix_t, ga, "bwd_out_w")
    dof_n, dof_t, dos_n, dos_t, dg_fox, dg_sb = _headnorm_bwd(dmix, o_f, o_s, g_out_fox, g_out_sb)
    dqf_t, dkf, dvf = _fox_bwd(fox["q_t"], fox["q_n"], fox["k_n"], fox["k_t"], fox["v_n"], dof_t, dof_n, of_t, lse,
                              f_end, k_max)
    dW_upf = jnp.concatenate([dW_g[:, :ff], dW_v[:, :ff]], axis=1)
    dcw = jnp.concatenate([p_g[:CONV_W, :ff], p_v[:CONV_W, :ff]], axis=1)
    dqs_t, dks, dvs, (s_out, s_up, s_down, s_cw) = _sb_bwd(
        sb["q_t"], sb["q_n"], sb["k_n"], sb["k_t"], sb["v_n"], dos_t, dos_n, os_t,
        [_to_slots(dW_out, 0), _to_slots(dW_upf, 1), _to_slots(dW_down[:ff], 0), _to_slots(dcw, 1)])
    dparts, dfk = _dqkv_assemble(dqf_t, dkf, dvf, dqs_t, dks, dvs)
    dz_rows, db_fgate = _fgate_bwd(dfk.reshape(n_heads * nc, LANES),
                                   dqf_t[:, Q_F_LANE, :].reshape(n_heads * nc, LANES), z_rows, nc)
    dzf = _pad_cols(dz_rows.reshape(n_heads, S).T, LANES).astype(BF16)
    dh1 = _mm_parts(dparts, W_qkv, dzf, W_f, "bwd_in_act")
    dW_qkv = _mm_acc_parts(h1_t, dparts, "bwd_in_w")
    dW_f = _mm_acc(h1_t, dzf, "bwd_in_w_fgate")
    grad_x, dscale_a, dshift_a, dg_attn = _norm_bwd(dh1, xs, dx1, g_attn, scale_a, "norm_attn_bwd")

    dconv_b = jnp.concatenate([p_g[CONV_W:CONV_W + 1, :ff], p_v[CONV_W:CONV_W + 1, :ff]], axis=1)
    parts = [dshift_a, dscale_a, dgate_a, dshift_m, dscale_m, dgate_m,
             dg_attn, db_fgate.reshape(1, n_heads), dg_fox, dg_sb, dg_mlp, dconv_b, dg_final,
             loss_p[:, :1]]
    sizes = [p.shape[1] for p in parts]
    vec = jnp.concatenate(parts, axis=1)
    n_vec = -(-vec.shape[1] // LANES) * LANES
    vec = _pad_cols(vec, n_vec)
    (vec_g,) = _exchange([vec], scatter=False, name="gather_small")
    offs = [0]
    for s in sizes:
        offs.append(offs[-1] + s)

    def small(k0, k1=None):
        k1 = k0 if k1 is None else k1
        return vec_g[:, :, offs[k0]:offs[k1 + 1]]

    dmod_all = small(0, 5).reshape(N_DEV, 6 * D)
    dmod_cols = lax.dynamic_slice(dmod_all, (0, me * n_ada), (N_DEV, n_ada))
    dW_ada = _ada_bwd(c_all.T, dmod_cols)

    dW_in = jnp.concatenate([jnp.transpose(dW_qkv, (1, 0, 2)).reshape(D, n_qkv), dW_f[:, :n_heads]], axis=1)
    bound_in = _to_slots(dW_in, 1)
    bound_in = bound_in.reshape((N_CHIPS, 2) + bound_in.shape[1:])
    (got_in,) = _scatter_to_sibling([bound_in], "scatter_sibling")
    c_idx = lax.axis_index("c").astype(jnp.int32).reshape(1)
    (s_in,) = _scatter_to_chips([_pair_add(bound_in, got_in, c_idx, "pair_add")], "scatter_chips")

    res = {}
    res["w_ada"] = _adamw(w_ada[0], dW_ada[None], m_w_ada[0], v_w_ada[0], "adamw_w_ada")
    res["w_in"] = _adamw(w_in[0], s_in, m_w_in[0], v_w_in[0], "adamw_w_in")
    res["w_out"] = _adamw(w_out[0], s_out, m_w_out[0], v_w_out[0], "adamw_w_out")
    res["w_up"] = _adamw(w_up[0], s_up, m_w_up[0], v_w_up[0], "adamw_w_up")
    res["w_down"] = _adamw(w_down[0], s_down, m_w_down[0], v_w_down[0], "adamw_w_down")
    res["conv_w"] = _adamw(conv_w[0], s_cw, m_conv_w[0], v_conv_w[0], "adamw_conv_w")
    small_names = ["b_ada", "g_attn", "b_fgate", "g_out_fox", "g_out_sb", "g_mlp", "conv_b", "g_final"]
    small_w = [b_ada, g_attn, b_fgate, g_out_fox, g_out_sb, g_mlp, conv_b, g_final.reshape(1, D)]
    small_m = [m_b_ada, m_g_attn, m_b_fgate, m_g_out_fox, m_g_out_sb, m_g_mlp, m_conv_b, m_g_final.reshape(1, D)]
    small_v = [v_b_ada, v_g_attn, v_b_fgate, v_g_out_fox, v_g_out_sb, v_g_mlp, v_conv_b, v_g_final.reshape(1, D)]
    small_res = _adamw(jnp.concatenate(small_w, axis=1), small(0, 12), jnp.concatenate(small_m, axis=1),
                       jnp.concatenate(small_v, axis=1), "adamw_small")
    lo = 0
    for nm, wv in zip(small_names, small_w):
        res[nm] = tuple(r[:, lo:lo + wv.shape[1]] for r in small_res)
        lo += wv.shape[1]
    loss = _slot_sum(_pad_cols(small(13).reshape(N_DEV, 1), LANES).reshape(N_DEV, 1, LANES), "loss_sum")[0, 0]

    names = ["w_ada", "b_ada", "g_attn", "w_in", "b_fgate", "g_out_fox", "g_out_sb", "w_out", "g_mlp",
             "w_up", "conv_w", "conv_b", "w_down", "g_final"]

    def shaped(n, a):
        if n == "g_final":
            return a.reshape(D)
        if n in ("b_ada", "g_attn", "b_fgate", "g_out_fox", "g_out_sb", "g_mlp", "conv_b"):
            return a
        return a[None]

    outs = [loss, grad_x[None]]
    for k in range(4):
        outs += [shaped(n, res[n][k]) for n in names]
    return tuple(outs)
```

```python
import jax
import jax.numpy as jnp
from jax import lax
from jax.experimental import pallas as pl
from jax.experimental.pallas import tpu as pltpu

F32 = jnp.float32
BF16 = jnp.bfloat16
HIGHEST = lax.Precision.HIGHEST

N_DEV = 8
LANES = 128
HEAD_DIM = 64
EPS = 1e-6
CONV_W = 3
CONV_COLS = 1408
HALO = 16
ATT_BQ = 512
ATT_BK = 512
SCAN_BK = 128
VMEM_LIMIT = 56 * 1024 * 1024

ADAM_LR = 0.001
ADAM_B1 = 0.9
ADAM_B2 = 0.999
ADAM_EPS = 1e-08
ADAM_WD = 0.01
ADAM_STEP = 10


def _params(**kw):
    return pltpu.CompilerParams(vmem_limit_bytes=VMEM_LIMIT, **kw)


def _tile(n, cap):
    if n <= cap:
        return n
    best = None
    for t in range(LANES, cap + 1, LANES):
        if n % t == 0:
            best = t
    assert best is not None, (n, cap)
    return best


def _dot(a, b, **kw):
    return jnp.dot(a, b, preferred_element_type=F32, **kw)


def _exchange_copies(ins, outs, send_sems, recv_sems, loc_sems, scatter):
    n = len(ins)
    if n == 0:
        return []
    x, y, c = lax.axis_index("x"), lax.axis_index("y"), lax.axis_index("c")
    me = 4 * x + 2 * y + c
    copies = []
    for a in range(n):
        src = ins[a].at[me] if scatter else ins[a]
        copies.append(pltpu.make_async_copy(src, outs[a].at[me], loc_sems.at[a]))
    for k in range(1, N_DEV):
        px = 1 - x if k & 4 else x
        py = 1 - y if k & 2 else y
        pc = 1 - c if k & 1 else c
        peer = 4 * px + 2 * py + pc
        for a in range(n):
            src = ins[a].at[peer] if scatter else ins[a]
            copies.append(pltpu.make_async_remote_copy(
                src_ref=src, dst_ref=outs[a].at[me],
                send_sem=send_sems.at[a, k - 1], recv_sem=recv_sems.at[a, k - 1],
                device_id=(px, py, pc), device_id_type=pl.DeviceIdType.MESH))
    return copies


def _exchange_out_shapes(arrays, scatter):
    return [jax.ShapeDtypeStruct((N_DEV,) + tuple(a.shape[1:] if scatter else a.shape), a.dtype) for a in arrays]


def _exchange_sems(n):
    return [pltpu.SemaphoreType.DMA((n, N_DEV - 1)), pltpu.SemaphoreType.DMA((n, N_DEV - 1)),
            pltpu.SemaphoreType.DMA((n,))]


def _exchange(arrays, scatter, name):
    n = len(arrays)

    def body(*refs):
        copies = _exchange_copies(refs[:n], refs[n:2 * n], *refs[2 * n:], scatter)
        for cp in copies:
            cp.start()
        for cp in copies:
            cp.wait()

    any_spec = pl.BlockSpec(memory_space=pl.ANY)
    return pl.pallas_call(
        body, name=name, out_shape=tuple(_exchange_out_shapes(arrays, scatter)),
        in_specs=[any_spec] * n, out_specs=tuple([any_spec] * n),
        scratch_shapes=_exchange_sems(n),
        compiler_params=pltpu.CompilerParams(has_side_effects=True),
    )(*arrays)


def _gather_two_level(arrays, name):
    n = len(arrays)
    out_shape = [jax.ShapeDtypeStruct((N_DEV,) + tuple(a.shape), a.dtype) for a in arrays]

    def body(*refs):
        ins, outs = refs[:n], refs[n:2 * n]
        send_sems, recv_sems, loc_sems = refs[2 * n:]
        x, y, c = lax.axis_index("x"), lax.axis_index("y"), lax.axis_index("c")
        me, sibling = (x, y, c), (x, y, 1 - c)
        chips = [(1 - x, y), (x, 1 - y), (1 - x, 1 - y)]

        def slot(px, py, pc):
            return 4 * px + 2 * py + pc

        def copy(a, k, block, to, src=None):
            dst = outs[a].at[slot(*block)]
            return pltpu.make_async_remote_copy(
                src_ref=dst if src is None else src, dst_ref=dst,
                send_sem=send_sems.at[a, k], recv_sem=recv_sems.at[a, k],
                device_id=to, device_id_type=pl.DeviceIdType.MESH)

        local = [pltpu.make_async_copy(ins[a], outs[a].at[slot(*me)], loc_sems.at[a]) for a in range(n)]
        for cp in local:
            cp.start()
        first = []
        for a in range(n):
            first.append(copy(a, 0, me, sibling, src=ins[a]))
            first += [copy(a, 1 + j, me, (*chip, c), src=ins[a]) for j, chip in enumerate(chips)]
        for cp in first:
            cp.start()
        passed = []
        for j, chip in enumerate(chips):
            for a in range(n):
                copy(a, 1 + j, (*chip, c), me).wait_recv()
                cp = copy(a, 4 + j, (*chip, c), sibling)
                cp.start()
                passed.append(cp)
        for a in range(n):
            copy(a, 0, sibling, me).wait_recv()
            for j, chip in enumerate(chips):
                copy(a, 4 + j, (*chip, 1 - c), me).wait_recv()
        for cp in first + passed:
            cp.wait_send()
        for cp in local:
            cp.wait()

    any_spec = pl.BlockSpec(memory_space=pl.ANY)
    return pl.pallas_call(
        body, name=name, out_shape=tuple(out_shape),
        in_specs=[any_spec] * n, out_specs=tuple([any_spec] * n),
        scratch_shapes=[pltpu.SemaphoreType.DMA((n, N_DEV - 1)), pltpu.SemaphoreType.DMA((n, N_DEV - 1)),
                        pltpu.SemaphoreType.DMA((n,))],
        compiler_params=pltpu.CompilerParams(has_side_effects=True),
    )(*arrays)


N_CHIPS = 4


def _scatter_to_sibling(arrays, name):
    n = len(arrays)
    out_shape = [jax.ShapeDtypeStruct((N_CHIPS,) + tuple(a.shape[2:]), a.dtype) for a in arrays]

    def body(*refs):
        ins, outs = refs[:n], refs[n:2 * n]
        send_sems, recv_sems = refs[2 * n:]
        x, y, c = lax.axis_index("x"), lax.axis_index("y"), lax.axis_index("c")
        copies = []
        for a in range(n):
            for q in range(N_CHIPS):
                cp = pltpu.make_async_remote_copy(
                    src_ref=ins[a].at[q, 1 - c], dst_ref=outs[a].at[q],
                    send_sem=send_sems.at[a, q], recv_sem=recv_sems.at[a, q],
                    device_id=(x, y, 1 - c), device_id_type=pl.DeviceIdType.MESH)
                cp.start()
                copies.append(cp)
        for cp in copies:
            cp.wait()

    any_spec = pl.BlockSpec(memory_space=pl.ANY)
    return pl.pallas_call(
        body, name=name, out_shape=tuple(out_shape),
        in_specs=[any_spec] * n, out_specs=tuple([any_spec] * n),
        scratch_shapes=[pltpu.SemaphoreType.DMA((n, N_CHIPS)), pltpu.SemaphoreType.DMA((n, N_CHIPS))],
        compiler_params=pltpu.CompilerParams(has_side_effects=True),
    )(*arrays)


def _pair_add(mine, got, c_idx, name):
    _, _, R, C = mine.shape
    tr = 256 if (R % 256 == 0 and R > 256) else R

    def body(c_ref, m_ref, g_ref, o_ref):
        o_ref[...] = m_ref[...] + g_ref[...]

    grid_spec = pltpu.PrefetchScalarGridSpec(
        num_scalar_prefetch=1, grid=(N_CHIPS, R // tr),
        in_specs=[pl.BlockSpec((None, None, tr, C), lambda q, i, c_ref: (q, c_ref[0], i, 0)),
                  pl.BlockSpec((None, tr, C), lambda q, i, c_ref: (q, i, 0))],
        out_specs=pl.BlockSpec((None, tr, C), lambda q, i, c_ref: (q, i, 0)))
    return pl.pallas_call(
        body, name=name, grid_spec=grid_spec, out_shape=jax.ShapeDtypeStruct((N_CHIPS, R, C), mine.dtype),
        compiler_params=_params(dimension_semantics=("parallel", "parallel")),
    )(c_idx, mine, got)


def _scatter_to_chips(arrays, name):
    n = len(arrays)
    out_shape = [jax.ShapeDtypeStruct(a.shape, a.dtype) for a in arrays]

    def body(*refs):
        ins, outs = refs[:n], refs[n:2 * n]
        send_sems, recv_sems, loc_sems = refs[2 * n:]
        x, y, c = lax.axis_index("x"), lax.axis_index("y"), lax.axis_index("c")
        myq = 2 * x + y
        copies = []
        for a in range(n):
            cp = pltpu.make_async_copy(ins[a].at[myq], outs[a].at[myq], loc_sems.at[a])
            cp.start()
            copies.append(cp)
        for k in range(1, N_CHIPS):
            qx = 1 - x if k & 2 else x
            qy = 1 - y if k & 1 else y
            for a in range(n):
                cp = pltpu.make_async_remote_copy(
                    src_ref=ins[a].at[2 * qx + qy], dst_ref=outs[a].at[myq],
                    send_sem=send_sems.at[a, k - 1], recv_sem=recv_sems.at[a, k - 1],
                    device_id=(qx, qy, c), device_id_type=pl.DeviceIdType.MESH)
                cp.start()
                copies.append(cp)
        for cp in copies:
            cp.wait()

    any_spec = pl.BlockSpec(memory_space=pl.ANY)
    return pl.pallas_call(
        body, name=name, out_shape=tuple(out_shape),
        in_specs=[any_spec] * n, out_specs=tuple([any_spec] * n),
        scratch_shapes=[pltpu.SemaphoreType.DMA((n, N_CHIPS - 1)), pltpu.SemaphoreType.DMA((n, N_CHIPS - 1)),
                        pltpu.SemaphoreType.DMA((n,))],
        compiler_params=pltpu.CompilerParams(has_side_effects=True),
    )(*arrays)


def _dot_nt(a, b):
    return lax.dot_general(a, b, (((1,), (1,)), ((), ())), preferred_element_type=F32)


def _rhs_spec(b, tn, nt):
    if nt:
        return pl.BlockSpec((tn, b.shape[1]), lambda i, j: (j, 0))
    return pl.BlockSpec((b.shape[0], tn), lambda i, j: (0, j))


def _mm(a, b, out_dtype, name, tm=1024, tn=512, nt=False):
    M, K = a.shape
    N = b.shape[0] if nt else b.shape[1]
    tm, tn = _tile(M, tm), _tile(N, tn)
    dot = _dot_nt if nt else _dot

    def body(a_ref, b_ref, o_ref):
        o_ref[...] = dot(a_ref[...], b_ref[...]).astype(out_dtype)

    return pl.pallas_call(
        body, name=name, out_shape=jax.ShapeDtypeStruct((M, N), out_dtype),
        grid=(M // tm, N // tn),
        in_specs=[pl.BlockSpec((tm, K), lambda i, j: (i, 0)), _rhs_spec(b, tn, nt)],
        out_specs=pl.BlockSpec((tm, tn), lambda i, j: (i, j)),
        compiler_params=_params(dimension_semantics=("parallel", "parallel")),
    )(a, b)


def _mm2(a1, b1, a2, b2, out_dtype, name, tm=1024, tn=512, nt=False):
    M, K1 = a1.shape
    _, K2 = a2.shape
    N = b1.shape[0] if nt else b1.shape[1]
    tm, tn = _tile(M, tm), _tile(N, tn)
    dot = _dot_nt if nt else _dot

    def body(a1_ref, b1_ref, a2_ref, b2_ref, o_ref):
        o_ref[...] = (dot(a1_ref[...], b1_ref[...]) + dot(a2_ref[...], b2_ref[...])).astype(out_dtype)

    return pl.pallas_call(
        body, name=name, out_shape=jax.ShapeDtypeStruct((M, N), out_dtype),
        grid=(M // tm, N // tn),
        in_specs=[pl.BlockSpec((tm, K1), lambda i, j: (i, 0)), _rhs_spec(b1, tn, nt),
                  pl.BlockSpec((tm, K2), lambda i, j: (i, 0)), _rhs_spec(b2, tn, nt)],
        out_specs=pl.BlockSpec((tm, tn), lambda i, j: (i, j)),
        compiler_params=_params(dimension_semantics=("parallel", "parallel")),
    )(a1, b1, a2, b2)


def _mm_acc(a, b, name, tm=1408, tn=1408, tk=512):
    M, S = a.shape
    _, N = b.shape
    tm, tn, tk = _tile(M, tm), _tile(N, tn), _tile(S, tk)

    def body(a_ref, b_ref, o_ref):
        @pl.when(pl.program_id(2) == 0)
        def _():
            o_ref[...] = jnp.zeros_like(o_ref)

        o_ref[...] += _dot(a_ref[...], b_ref[...])

    return pl.pallas_call(
        body, name=name, out_shape=jax.ShapeDtypeStruct((M, N), F32),
        grid=(M // tm, N // tn, S // tk),
        in_specs=[pl.BlockSpec((tm, tk), lambda i, j, k: (i, k)), pl.BlockSpec((tk, tn), lambda i, j, k: (k, j))],
        out_specs=pl.BlockSpec((tm, tn), lambda i, j, k: (i, j)),
        compiler_params=_params(dimension_semantics=("parallel", "parallel", "arbitrary")),
    )(a, b)


def _mm_parts(parts, w, a2, w2, name, tm=1024, tn=512):
    P, S, K = parts.shape
    N, K2 = w2.shape
    tm, tn = _tile(S, tm), _tile(N, tn)

    def body(p_ref, w_ref, a2_ref, w2_ref, o_ref):
        acc = _dot_nt(a2_ref[...], w2_ref[...])
        for k in range(P):
            acc = acc + _dot_nt(p_ref[k], w_ref[:, k * K:(k + 1) * K])
        o_ref[...] = acc

    return pl.pallas_call(
        body, name=name, out_shape=jax.ShapeDtypeStruct((S, N), F32), grid=(S // tm, N // tn),
        in_specs=[pl.BlockSpec((P, tm, K), lambda i, j: (0, i, 0)), pl.BlockSpec((tn, P * K), lambda i, j: (j, 0)),
                  pl.BlockSpec((tm, K2), lambda i, j: (i, 0)), pl.BlockSpec((tn, K2), lambda i, j: (j, 0))],
        out_specs=pl.BlockSpec((tm, tn), lambda i, j: (i, j)),
        compiler_params=_params(dimension_semantics=("parallel", "parallel")),
    )(parts, w, a2, w2)


def _mm_acc_parts(a, parts, name, tm=1024, tk=512):
    M, S = a.shape
    P, _, K = parts.shape
    tm, tk = _tile(M, tm), _tile(S, tk)

    def body(a_ref, b_ref, o_ref):
        @pl.when(pl.program_id(2) == 0)
        def _():
            o_ref[...] = jnp.zeros_like(o_ref)

        o_ref[...] += _dot(a_ref[...], b_ref[...])

    return pl.pallas_call(
        body, name=name, out_shape=jax.ShapeDtypeStruct((P, M, K), F32), grid=(P, M // tm, S // tk),
        in_specs=[pl.BlockSpec((tm, tk), lambda k, i, s: (i, s)), pl.BlockSpec((None, tk, K), lambda k, i, s: (k, s, 0))],
        out_specs=pl.BlockSpec((None, tm, K), lambda k, i, s: (k, i, 0)),
        compiler_params=_params(dimension_semantics=("parallel", "parallel", "arbitrary")),
    )(a, parts)


def _silu(z):
    return z * (1.0 / (1.0 + jnp.exp(-z)))


def _ada_fwd(c_all, w_shard, b_shard):
    n = w_shard.shape[1]

    def body(c_ref, w_ref, b_ref, o_ref):
        o_ref[...] = _dot(_silu(c_ref[...]), w_ref[...], precision=HIGHEST) + b_ref[...]

    return pl.pallas_call(body, name="ada_fwd", out_shape=jax.ShapeDtypeStruct((N_DEV, n), F32),
                          compiler_params=_params())(c_all, w_shard, b_shard)


def _ada_bwd(c_all_t, dmod_cols):
    D = c_all_t.shape[0]
    n = dmod_cols.shape[1]

    def body(ct_ref, dm_ref, o_ref):
        sc = _silu(ct_ref[...])
        dm = dm_ref[...]
        acc = sc[:, 0:1] * dm[0:1, :]
        for b in range(1, N_DEV):
            acc = acc + sc[:, b:b + 1] * dm[b:b + 1, :]
        o_ref[...] = acc

    return pl.pallas_call(body, name="ada_bwd", out_shape=jax.ShapeDtypeStruct((D, n), F32),
                          compiler_params=_params())(c_all_t, dmod_cols)


def _row_specs(tm, widths):
    return [pl.BlockSpec((tm, w), lambda i: (i, 0)) for w in widths]


def _vec_spec(w):
    return pl.BlockSpec((1, w), lambda i: (0, 0))


def _col_spec(tm, w):
    return pl.BlockSpec((w, tm), lambda i: (0, i))


def _prenorm(x, g, scale, shift, name):
    S, D = x.shape
    tm = _tile(S, 512)

    def body(x_ref, g_ref, sc_ref, sh_ref, h_ref, ht_ref):
        xv = x_ref[...]
        r = lax.rsqrt(jnp.mean(xv * xv, axis=-1, keepdims=True) + EPS)
        h = (xv * r) * g_ref[...] * (1.0 + sc_ref[...]) + sh_ref[...]
        h_ref[...] = h.astype(BF16)
        ht_ref[...] = h.T.astype(BF16)

    return pl.pallas_call(
        body, name=name, grid=(S // tm,),
        out_shape=(jax.ShapeDtypeStruct((S, D), BF16), jax.ShapeDtypeStruct((D, S), BF16)),
        in_specs=_row_specs(tm, [D]) + [_vec_spec(D)] * 3,
        out_specs=(_row_specs(tm, [D])[0], _col_spec(tm, D)),
        compiler_params=_params(dimension_semantics=("parallel",)),
    )(x, g, scale, shift)


def _group_ones():
    r = lax.broadcasted_iota(jnp.int32, (LANES, LANES), 0) // HEAD_DIM
    c = lax.broadcasted_iota(jnp.int32, (LANES, LANES), 1) // HEAD_DIM
    return (r == c).astype(F32)


def _headnorm_fwd(o_f, o_s, g_f, g_s):
    S, dh = o_f.shape
    tm = _tile(S, 512)

    def body(of_ref, os_ref, gf_ref, gs_ref, mix_ref, mixt_ref):
        ones = _group_ones()
        for part, (o_ref, g_ref) in enumerate(((of_ref, gf_ref), (os_ref, gs_ref))):
            for t in range(dh // LANES):
                cols = slice(t * LANES, (t + 1) * LANES)
                out = slice(part * dh + t * LANES, part * dh + (t + 1) * LANES)
                o = o_ref[:, cols]
                ms = _dot(o * o, ones, precision=HIGHEST) * (1.0 / HEAD_DIM)
                mixn = o * lax.rsqrt(ms + EPS) * g_ref[:, cols]
                mix_ref[:, out] = mixn.astype(BF16)
                mixt_ref[out, :] = mixn.T.astype(BF16)

    return pl.pallas_call(
        body, name="headnorm_fwd", grid=(S // tm,),
        out_shape=(jax.ShapeDtypeStruct((S, 2 * dh), BF16), jax.ShapeDtypeStruct((2 * dh, S), BF16)),
        in_specs=_row_specs(tm, [dh, dh]) + [_vec_spec(dh)] * 2,
        out_specs=(_row_specs(tm, [2 * dh])[0], _col_spec(tm, 2 * dh)),
        compiler_params=_params(dimension_semantics=("parallel",)),
    )(o_f, o_s, g_f, g_s)


def _resid_prenorm(x, a_out, gate, g, scale, shift):
    S, D = x.shape
    tm = _tile(S, 512)

    def body(x_ref, a_ref, gt_ref, g_ref, sc_ref, sh_ref, x1_ref, h_ref, ht_ref):
        x1 = x_ref[...] + gt_ref[...] * a_ref[...]
        x1_ref[...] = x1
        r = lax.rsqrt(jnp.mean(x1 * x1, axis=-1, keepdims=True) + EPS)
        h = (x1 * r) * g_ref[...] * (1.0 + sc_ref[...]) + sh_ref[...]
        h_ref[...] = h.astype(BF16)
        ht_ref[...] = h.T.astype(BF16)

    return pl.pallas_call(
        body, name="resid_prenorm", grid=(S // tm,),
        out_shape=(jax.ShapeDtypeStruct((S, D), F32), jax.ShapeDtypeStruct((S, D), BF16),
                   jax.ShapeDtypeStruct((D, S), BF16)),
        in_specs=_row_specs(tm, [D, D]) + [_vec_spec(D)] * 4,
        out_specs=tuple(_row_specs(tm, [D, D]) + [_col_spec(tm, D)]),
        compiler_params=_params(dimension_semantics=("parallel",)),
    )(x, a_out, gate, g, scale, shift)


def _shift_down(main, halo, k):
    ext = jnp.concatenate([halo, main], axis=0)
    return pltpu.roll(ext, k, 0)[halo.shape[0]:]


def _shift_up(main, halo, k):
    ext = jnp.concatenate([main, halo], axis=0)
    n = ext.shape[0]
    return pltpu.roll(ext, n - k, 0)[:main.shape[0]]


def _conv(up, up_halo, w_ref, b_ref):
    return (w_ref[2:3, :] * up + w_ref[1:2, :] * _shift_down(up, up_halo, 1)
            + w_ref[0:1, :] * _shift_down(up, up_halo, 2) + b_ref[...])


def _prev_halo_map(tm):
    step = tm // HALO
    return lambda j, i: (jnp.maximum(i * step - 1, 0), j)


MLP_TM = 512
MLP_CT = 256
CARRY = 8


def _mlp_up(h, wg, wv, cwg, cwv, cbg, cbv):
    S, D = h.shape
    F = wg.shape[1]
    tm, ct = _tile(S, MLP_TM), _tile(F, MLP_CT)
    nct = F // ct

    def body(h_ref, wg_ref, wv_ref, cwg_ref, cwv_ref, cbg_ref, cbv_ref,
             upg_ref, upv_ref, act_ref, actt_ref, hg_scr, hv_scr):
        i, j = pl.program_id(0), pl.program_id(1)
        hv = h_ref[...]
        us = []
        for w_ref, cw_ref, cb_ref, up_ref, scr in ((wg_ref, cwg_ref, cbg_ref, upg_ref, hg_scr),
                                                   (wv_ref, cwv_ref, cbv_ref, upv_ref, hv_scr)):
            up = _dot(hv, w_ref[...]).astype(BF16)
            up_ref[...] = up
            upf = up.astype(F32)
            halo = jnp.where(i == 0, 0.0, scr[j])
            us.append(_conv(upf, halo, cw_ref, cb_ref))
            scr[j] = upf[tm - CARRY:, :]
        act = _silu(us[0]) * us[1]
        act_ref[...] = act.astype(BF16)
        actt_ref[...] = act.T.astype(BF16)

    blk = pl.BlockSpec((tm, ct), lambda i, j: (i, j))
    wspec = pl.BlockSpec((D, ct), lambda i, j: (0, j))
    cwspec = pl.BlockSpec((CONV_W, ct), lambda i, j: (0, j))
    cbspec = pl.BlockSpec((1, ct), lambda i, j: (0, j))
    sds = jax.ShapeDtypeStruct((S, F), BF16)
    return pl.pallas_call(
        body, name="mlp_up", grid=(S // tm, nct),
        out_shape=(sds, sds, sds, jax.ShapeDtypeStruct((F, S), BF16)),
        in_specs=[pl.BlockSpec((tm, D), lambda i, j: (i, 0)), wspec, wspec, cwspec, cwspec, cbspec, cbspec],
        out_specs=(blk, blk, blk, pl.BlockSpec((ct, tm), lambda i, j: (j, i))),
        scratch_shapes=[pltpu.VMEM((nct, CARRY, ct), F32), pltpu.VMEM((nct, CARRY, ct), F32)],
        compiler_params=_params(dimension_semantics=("arbitrary", "arbitrary")),
    )(h, wg, wv, cwg, cwv, cbg, cbv)


def _conv_act_bwd(dact, up_g, up_v, cwg, cwv, cbg, cbv):
    S, F = up_g.shape
    tm, ct = _tile(S, 256), _tile(F, CONV_COLS)
    nct = F // ct

    def body(da_ref, ug_ref, uv_ref, hg_ref, hv_ref, wg_ref, wv_ref, bg_ref, bv_ref,
             dug_ref, duv_ref, pg_ref, pv_ref):
        first = pl.program_id(1) == 0

        @pl.when(first)
        def _():
            pg_ref[...] = jnp.zeros_like(pg_ref)
            pv_ref[...] = jnp.zeros_like(pv_ref)

        da = da_ref[...].astype(F32)
        taps = []
        for u_ref, h_ref in ((ug_ref, hg_ref), (uv_ref, hv_ref)):
            h = jnp.where(first, 0.0, h_ref[...].astype(F32))
            uu = u_ref[...].astype(F32)
            taps.append((_shift_down(uu, h, 2), _shift_down(uu, h, 1), uu))
        u_g = wg_ref[0:1, :] * taps[0][0] + wg_ref[1:2, :] * taps[0][1] + wg_ref[2:3, :] * taps[0][2] + bg_ref[...]
        u_v = wv_ref[0:1, :] * taps[1][0] + wv_ref[1:2, :] * taps[1][1] + wv_ref[2:3, :] * taps[1][2] + bv_ref[...]
        sg = 1.0 / (1.0 + jnp.exp(-u_g))
        du_g = da * u_v * (sg * (1.0 + u_g * (1.0 - sg)))
        du_v = da * (u_g * sg)
        dug_ref[...] = du_g.astype(BF16)
        duv_ref[...] = du_v.astype(BF16)
        for du, tp, p_ref in ((du_g, taps[0], pg_ref), (du_v, taps[1], pv_ref)):
            for k in range(CONV_W):
                p_ref[k:k + 1, :] += jnp.sum(du * tp[k], axis=0, keepdims=True)
            p_ref[CONV_W:CONV_W + 1, :] += jnp.sum(du, axis=0, keepdims=True)

    main = pl.BlockSpec((tm, ct), lambda j, i: (i, j))
    halo = pl.BlockSpec((HALO, ct), _prev_halo_map(tm))
    wspec = pl.BlockSpec((CONV_W, ct), lambda j, i: (0, j))
    bspec = pl.BlockSpec((1, ct), lambda j, i: (0, j))
    pspec = pl.BlockSpec((8, ct), lambda j, i: (0, j))
    return pl.pallas_call(
        body, name="conv_act_bwd", grid=(nct, S // tm),
        out_shape=(jax.ShapeDtypeStruct((S, F), BF16), jax.ShapeDtypeStruct((S, F), BF16),
                   jax.ShapeDtypeStruct((8, F), F32), jax.ShapeDtypeStruct((8, F), F32)),
        in_specs=[main, main, main, halo, halo, wspec, wspec, bspec, bspec],
        out_specs=(main, main, pspec, pspec),
        compiler_params=_params(dimension_semantics=("parallel", "arbitrary")),
    )(dact, up_g, up_v, up_g, up_v, cwg, cwv, cbg, cbv)


def _conv_bwd_input(du, cw, name):
    S, C = du.shape
    tm, ct = _tile(S, 256), _tile(C, CONV_COLS)
    step = tm // HALO
    last_halo = S // HALO - 1

    def body(du_ref, h_ref, w_ref, o_ref):
        last = pl.program_id(1) == pl.num_programs(1) - 1
        d = du_ref[...].astype(F32)
        h = jnp.where(last, 0.0, h_ref[...].astype(F32))
        o_ref[...] = (w_ref[2:3, :] * d + w_ref[1:2, :] * _shift_up(d, h, 1)
                      + w_ref[0:1, :] * _shift_up(d, h, 2)).astype(BF16)

    return pl.pallas_call(
        body, name=name, out_shape=jax.ShapeDtypeStruct((S, C), BF16), grid=(C // ct, S // tm),
        in_specs=[pl.BlockSpec((tm, ct), lambda j, i: (i, j)),
                  pl.BlockSpec((HALO, ct), lambda j, i: (jnp.minimum((i + 1) * step, last_halo), j)),
                  pl.BlockSpec((CONV_W, ct), lambda j, i: (0, j))],
        out_specs=pl.BlockSpec((tm, ct), lambda j, i: (i, j)),
        compiler_params=_params(dimension_semantics=("parallel", "parallel")),
    )(du, du, cw)


def _scan_mats(R, nc, reverse):
    i = lax.broadcasted_iota(jnp.int32, (LANES, LANES), 0)
    j = lax.broadcasted_iota(jnp.int32, (LANES, LANES), 1)
    inner = ((i >= j) if reverse else (i <= j)).astype(F32)
    r = lax.broadcasted_iota(jnp.int32, (R, R), 0)
    c = lax.broadcasted_iota(jnp.int32, (R, R), 1)
    same = (r // nc) == (c // nc)
    outer = (same & ((c > r) if reverse else (c < r))).astype(F32)
    return inner, outer


def _chunk_scan(v, inner, outer, reverse):
    w = _dot(v, inner, precision=HIGHEST)
    col = 0 if reverse else LANES - 1
    carry = _dot(outer, w, precision=HIGHEST)[:, col:col + 1]
    return w + carry


def _fgate_fwd(z_rows, nc):
    R = z_rows.shape[0]

    def body(z_ref, f_ref):
        z = z_ref[...]
        logf = jnp.minimum(z, 0.0) - jnp.log(1.0 + jnp.exp(-jnp.abs(z)))
        inner, outer = _scan_mats(R, nc, False)
        f_ref[...] = _chunk_scan(logf, inner, outer, False)

    return pl.pallas_call(body, name="fgate_fwd", out_shape=jax.ShapeDtypeStruct((R, LANES), F32),
                          compiler_params=_params())(z_rows)


def _fgate_bwd(dfk_neg_rows, dfq_rows, z_rows, nc):
    R = z_rows.shape[0]
    nh = R // nc

    def body(dfk_ref, dfq_ref, z_ref, dz_ref, db_ref):
        inner, outer = _scan_mats(R, nc, True)
        dlogf = _chunk_scan(dfq_ref[...] - dfk_ref[...], inner, outer, True)
        dz = dlogf * (1.0 / (1.0 + jnp.exp(z_ref[...])))
        dz_ref[...] = dz
        hr = lax.broadcasted_iota(jnp.int32, (nh, R), 0)
        hc = lax.broadcasted_iota(jnp.int32, (nh, R), 1) // nc
        per_head = _dot((hr == hc).astype(F32), dz, precision=HIGHEST)
        db_ref[...] = jnp.sum(per_head, axis=1, keepdims=True)

    return pl.pallas_call(
        body, name="fgate_bwd",
        out_shape=(jax.ShapeDtypeStruct((R, LANES), F32), jax.ShapeDtypeStruct((nh, 1), F32)),
        compiler_params=_params())(dfk_neg_rows, dfq_rows, z_rows)


_NEG = -1e30
SKIP_BELOW = -106.0
_SCALE = HEAD_DIM ** -0.5
N_SCAN = ATT_BK // SCAN_BK
F_PARTS = 3
Q_F_LANE = HEAD_DIM
Q_ONE_LANE = HEAD_DIM + F_PARTS


def _kv_slice(j):
    return pl.ds(pl.multiple_of(j * ATT_BK, ATT_BK), ATT_BK)


def _mask_t(strict):
    s = lax.broadcasted_iota(jnp.int32, (ATT_BK, ATT_BQ), 0)
    t = lax.broadcasted_iota(jnp.int32, (ATT_BK, ATT_BQ), 1)
    return (s < t) if strict else (s <= t)


def _walk_down(i, step, alive, carry):
    carry = step(i, carry, True)

    def cond(st):
        n, go, _ = st
        return jnp.logical_and(n < i, go)

    def body(st):
        n, _, cr = st
        j = i - 1 - n
        cr = step(j, cr, False)
        return n + 1, alive(jnp.maximum(j - 1, 0), cr), cr

    return lax.while_loop(cond, body, (jnp.int32(0), alive(jnp.maximum(i - 1, 0), carry), carry))[2]


def _t_block(rows):
    return pl.BlockSpec((None, rows, ATT_BQ), lambda h, i, *_: (h, 0, i))


def _t_full(rows, S):
    return pl.BlockSpec((None, rows, S), lambda h, i, *_: (h, 0, 0))


def _n_block():
    return pl.BlockSpec((None, ATT_BQ, LANES), lambda h, i, *_: (h, i, 0))


def _n_full(S):
    return pl.BlockSpec((None, S, LANES), lambda h, i, *_: (h, 0, 0))


def _heads(t):
    S = t.shape[0]
    return jnp.transpose(t.reshape(S, -1, HEAD_DIM), (1, 0, 2))


def _unheads(t):
    return jnp.transpose(t, (1, 0, 2)).reshape(t.shape[1], -1)


def _tr(t):
    return jnp.transpose(t, (0, 2, 1))


def _skip_bounds(k_cols, f_rows):
    H, S = f_rows.shape
    f_end = f_rows.reshape(H, S // ATT_BK, ATT_BK)[:, :, -1]
    k_sq = jnp.sum(jnp.square(_heads(k_cols).astype(F32)), axis=-1).reshape(H, S // ATT_BK, ATT_BK)
    k_max = lax.cummax(jnp.sqrt(jnp.max(k_sq, axis=-1)), axis=1)
    return f_end, k_max


def _bf16_parts(f):
    hi = f.astype(BF16).astype(F32)
    mid = (f - hi).astype(BF16).astype(F32)
    return hi, mid, (f - hi - mid).astype(BF16).astype(F32)


def _att_prep(qkv, f_pairs):
    S = qkv.shape[0]
    n_pairs = qkv.shape[1] // (6 * LANES)
    H = 2 * n_pairs
    tm = _tile(S, 512)

    def body(qf_ref, kf_ref, vf_ref, qs_ref, ks_ref, vs_ref, f_ref,
             fqn, fqt, fkn, fkt, fvn, fvt, sqn, sqt, skn, skt, svn, svt):
        lane = lax.broadcasted_iota(jnp.int32, (1, LANES), 1)
        f = f_ref[...]

        def head(ref, e):
            t = ref[...].astype(F32)
            if e == 1:
                t = pltpu.roll(t, HEAD_DIM, 1)
            return jnp.where(lane < HEAD_DIM, t, 0.0)

        def at(first):
            return jnp.logical_and(lane >= first, lane < first + F_PARTS)

        for e in range(2):
            parts = _bf16_parts(f[:, e:e + 1])
            f_lanes = sum(jnp.where(lane == Q_F_LANE + k, parts[k], 0.0) for k in range(F_PARTS))
            nf_lanes = sum(jnp.where(lane == Q_ONE_LANE + k, parts[k], 0.0) for k in range(F_PARTS))
            vals = (
                (fqn, fqt, LANES, head(qf_ref, e) * _SCALE + f_lanes + jnp.where(at(Q_ONE_LANE), 1.0, 0.0)),
                (fkn, fkt, LANES, head(kf_ref, e) + jnp.where(at(Q_F_LANE), 1.0, 0.0) - nf_lanes),
                (fvn, fvt, HEAD_DIM, head(vf_ref, e)),
                (sqn, sqt, LANES, head(qs_ref, e) * _SCALE),
                (skn, skt, LANES, head(ks_ref, e)),
                (svn, svt, HEAD_DIM, head(vs_ref, e)),
            )
            for n_ref, t_ref, rows, val in vals:
                n_ref[e] = val.astype(BF16)
                t_ref[e] = val.T[:rows].astype(BF16)

    col = lambda base: pl.BlockSpec((tm, LANES), lambda i, p: (i, base + p))
    n_spec = pl.BlockSpec((2, tm, LANES), lambda i, p: (p, i, 0))
    t_spec = lambda rows: pl.BlockSpec((2, rows, tm), lambda i, p: (p, 0, i))
    n_sds = jax.ShapeDtypeStruct((H, S, LANES), BF16)
    t_sds = lambda rows: jax.ShapeDtypeStruct((H, rows, S), BF16)
    group = ([n_sds, t_sds(LANES), n_sds, t_sds(LANES), n_sds, t_sds(HEAD_DIM)],
             [n_spec, t_spec(LANES), n_spec, t_spec(LANES), n_spec, t_spec(HEAD_DIM)])
    res = pl.pallas_call(
        body, name="att_prep", grid=(S // tm, n_pairs),
        out_shape=tuple(group[0] * 2),
        in_specs=[col(k * n_pairs) for k in range(6)] + [pl.BlockSpec((None, tm, 2), lambda i, p: (p, i, 0))],
        out_specs=tuple(group[1] * 2),
        compiler_params=_params(dimension_semantics=("parallel", "parallel")),
    )(qkv, qkv, qkv, qkv, qkv, qkv, f_pairs)
    names = ("q_n", "q_t", "k_n", "k_t", "v_n", "v_t")
    return dict(zip(names, res[:6])), dict(zip(names, res[6:]))


def _fox_reach(qt, fend_ref, kmax_ref, h):
    qf = qt.astype(F32)
    q_norm = jnp.sqrt(jnp.sum(jnp.square(qf[:HEAD_DIM]), axis=0, keepdims=True))
    f_t = jnp.sum(qf[Q_F_LANE:Q_F_LANE + F_PARTS], axis=0, keepdims=True)
    return lambda j: q_norm * kmax_ref[h, j] + f_t - fend_ref[h, j]


def _fox_fwd(q_t, k_n, v_t, f_end, k_max, shards):
    H, _, S = q_t.shape
    n, nq = len(shards), S // ATT_BQ

    def body(fend_ref, kmax_ref, qt_ref, k_ref, vt_ref, *rest):
        ins, (ot_ref, lse_ref), outs, sems = rest[:n], rest[n:n + 2], rest[n + 2:2 * n + 2], rest[2 * n + 2:]
        h, i = pl.program_id(0), pl.program_id(1)

        @pl.when(jnp.logical_and(h == 0, i == 0))
        def _():
            for cp in _exchange_copies(ins, outs, *sems, False):
                cp.start()

        qt = qt_ref[...]
        reach = _fox_reach(qt, fend_ref, kmax_ref, h)

        def step(j, carry, masked):
            m, l, acc = carry
            ks = _kv_slice(j)
            s = _dot(k_ref[ks, :], qt)
            if masked:
                s = jnp.where(_mask_t(False), s, _NEG)
            mn = jnp.maximum(m, jnp.max(s, axis=0, keepdims=True))
            alpha = jnp.exp(m - mn)
            p = jnp.exp(s - mn)
            l = alpha * l + jnp.sum(p, axis=0, keepdims=True)
            acc = acc * alpha + _dot(vt_ref[:, ks], p.astype(BF16))
            return mn, l, acc

        def alive(j, carry):
            return jnp.max(reach(j) - carry[0]) > SKIP_BELOW

        row = jnp.zeros((1, ATT_BQ), F32)
        m, l, acc = _walk_down(i, step, alive, (row + _NEG, row, jnp.zeros((HEAD_DIM, ATT_BQ), F32)))
        ot_ref[...] = acc / l
        lse_ref[...] = m + jnp.log(l)

        @pl.when(jnp.logical_and(h == H - 1, i == nq - 1))
        def _():
            for cp in _exchange_copies(ins, outs, *sems, False):
                cp.wait()

    any_spec = pl.BlockSpec(memory_space=pl.ANY)
    grid_spec = pltpu.PrefetchScalarGridSpec(
        num_scalar_prefetch=2, grid=(H, nq),
        in_specs=[_t_block(LANES), _n_full(S), _t_full(HEAD_DIM, S)] + [any_spec] * n,
        out_specs=tuple([_t_block(HEAD_DIM), _t_block(1)] + [any_spec] * n),
        scratch_shapes=_exchange_sems(n))
    res = pl.pallas_call(
        body, name="fox_fwd", grid_spec=grid_spec,
        out_shape=tuple([jax.ShapeDtypeStruct((H, HEAD_DIM, S), F32), jax.ShapeDtypeStruct((H, 1, S), F32)]
                        + _exchange_out_shapes(shards, False)),
        compiler_params=_params(dimension_semantics=("arbitrary", "arbitrary"), has_side_effects=True),
    )(f_end, k_max, q_t, k_n, v_t, *shards)
    return res[0], res[1], res[2:]


def _fox_bwd(q_t, q_n, k_n, k_t, v_n, do_t, do_n, o_t, lse, f_end, k_max):
    H, _, S = q_t.shape

    def body(fend_ref, kmax_ref, qt_ref, qn_ref, k_ref, kt_ref, v_ref, dot_ref, don_ref, ot_ref, lse_ref,
             dqt_ref, dk_ref, dv_ref):
        h, i = pl.program_id(0), pl.program_id(1)

        @pl.when(i == 0)
        def _():
            dk_ref[...] = jnp.zeros_like(dk_ref)
            dv_ref[...] = jnp.zeros_like(dv_ref)

        qt, qn, dot, don = qt_ref[...], qn_ref[...], dot_ref[...], don_ref[...]
        lse = lse_ref[...]
        delta = jnp.sum(dot[:HEAD_DIM].astype(F32) * ot_ref[...], axis=0, keepdims=True)
        reach = _fox_reach(qt, fend_ref, kmax_ref, h)

        def alive(j, dq):
            return jnp.max(reach(j) - lse) > SKIP_BELOW

        def step(j, dq, masked):
            ks = _kv_slice(j)
            s = _dot(k_ref[ks, :], qt)
            if masked:
                s = jnp.where(_mask_t(False), s, _NEG)
            p = jnp.exp(s - lse)
            ds = (p * (_dot(v_ref[ks, :], dot) - delta)).astype(BF16)
            dk_ref[ks, :] += _dot(ds, qn)
            dv_ref[ks, :] += _dot(p.astype(BF16), don)
            return dq + _dot(kt_ref[:, ks], ds)

        dqt_ref[...] = _walk_down(i, step, alive, jnp.zeros((LANES, ATT_BQ), F32))

    grid_spec = pltpu.PrefetchScalarGridSpec(
        num_scalar_prefetch=2, grid=(H, S // ATT_BQ),
        in_specs=[_t_block(LANES), _n_block(), _n_full(S), _t_full(LANES, S), _n_full(S),
                  _t_block(LANES), _n_block(), _t_block(HEAD_DIM), _t_block(1)],
        out_specs=(_t_block(LANES), _n_full(S), _n_full(S)))
    return pl.pallas_call(
        body, name="fox_bwd", grid_spec=grid_spec,
        out_shape=(jax.ShapeDtypeStruct((H, LANES, S), F32), jax.ShapeDtypeStruct((H, S, LANES), F32),
                   jax.ShapeDtypeStruct((H, S, LANES), F32)),
        compiler_params=_params(dimension_semantics=("parallel", "arbitrary")),
    )(f_end, k_max, q_t, q_n, k_n, k_t, v_n, do_t, do_n, o_t, lse)


def _scan_lhs():
    r = lax.broadcasted_iota(jnp.int32, (SCAN_BK, 2 * SCAN_BK), 0)
    c = lax.broadcasted_iota(jnp.int32, (SCAN_BK, 2 * SCAN_BK), 1) % SCAN_BK
    return (c >= r).astype(BF16)


def _suffix_sum(t, lhs):
    hi = t.astype(BF16)
    lo = (t - hi.astype(F32)).astype(BF16)
    return _dot(lhs, jnp.concatenate([hi, lo], axis=0))


def _sb_scores(k, qt, mask):
    z = _dot(k, qt)
    e = jnp.exp(-jnp.abs(z))
    lb = -(jnp.maximum(z, 0.0) + jnp.log(1.0 + e))
    if mask is not None:
        lb = jnp.where(mask, lb, 0.0)
    return z, e, lb


def _scan_blocks():
    return [slice(u * SCAN_BK, (u + 1) * SCAN_BK) for u in reversed(range(N_SCAN))]


def _sb_fwd(q_t, k_n, v_t):
    H, _, S = q_t.shape

    def body(qt_ref, k_ref, vt_ref, ot_ref):
        i = pl.program_id(1)
        qt = qt_ref[...]
        lhs = _scan_lhs()

        def step(j, carry, masked):
            c, acc = carry
            ks = _kv_slice(j)
            mask = _mask_t(True) if masked else None
            z, _, lb = _sb_scores(k_ref[ks, :], qt, mask)
            parts = []
            for sl in _scan_blocks():
                rin = _suffix_sum(lb[sl], lhs)
                a = jnp.exp(z[sl] + rin + c)
                if masked:
                    a = jnp.where(mask[sl], a, 0.0)
                parts.append(a.astype(BF16))
                c = c + rin[0:1, :]
            a_all = jnp.concatenate(parts[::-1], axis=0)
            return c, acc + _dot(vt_ref[:, ks], a_all)

        carry = (jnp.zeros((1, ATT_BQ), F32), jnp.zeros((HEAD_DIM, ATT_BQ), F32))
        ot_ref[...] = _walk_down(i, step, lambda j, cr: jnp.max(cr[0]) > SKIP_BELOW, carry)[1]

    return pl.pallas_call(
        body, name="sb_fwd", grid=(H, S // ATT_BQ),
        out_shape=jax.ShapeDtypeStruct((H, HEAD_DIM, S), F32),
        in_specs=[_t_block(LANES), _n_full(S), _t_full(HEAD_DIM, S)],
        out_specs=_t_block(HEAD_DIM),
        compiler_params=_params(dimension_semantics=("parallel", "parallel")),
    )(q_t, k_n, v_t)


def _sb_bwd(q_t, q_n, k_n, k_t, v_n, do_t, do_n, o_t, bound):
    H, _, S = q_t.shape
    n, nq = len(bound), S // ATT_BQ

    def body(qt_ref, qn_ref, k_ref, kt_ref, v_ref, dot_ref, don_ref, ot_ref, *rest):
        ins, (dqt_ref, dk_ref, dv_ref) = rest[:n], rest[n:n + 3]
        outs, sems = rest[n + 3:2 * n + 3], rest[2 * n + 3:]
        h, i = pl.program_id(0), pl.program_id(1)

        @pl.when(jnp.logical_and(h == 0, i == 0))
        def _():
            for cp in _exchange_copies(ins, outs, *sems, True):
                cp.start()

        @pl.when(i == 0)
        def _():
            dk_ref[...] = jnp.zeros_like(dk_ref)
            dv_ref[...] = jnp.zeros_like(dv_ref)

        qt, qn, dot, don = qt_ref[...], qn_ref[...], dot_ref[...], don_ref[...]
        lhs = _scan_lhs()
        delta = jnp.sum(dot[:HEAD_DIM].astype(F32) * ot_ref[...], axis=0, keepdims=True)

        def step(j, carry, masked):
            c, g, dq = carry
            ks = _kv_slice(j)
            mask = _mask_t(True) if masked else None
            z, e, lb = _sb_scores(k_ref[ks, :], qt, mask)
            da = _dot(v_ref[ks, :], dot)
            a_parts, dz_parts = [], []
            for sl in _scan_blocks():
                rin = _suffix_sum(lb[sl], lhs)
                a = jnp.exp(z[sl] + rin + c)
                if masked:
                    a = jnp.where(mask[sl], a, 0.0)
                ab = a.astype(BF16)
                gg = ab.astype(F32) * da[sl]
                rgin = _suffix_sum(gg, lhs)
                rinv = 1.0 / (1.0 + e[sl])
                sig = jnp.where(z[sl] >= 0.0, rinv, e[sl] * rinv)
                dz = gg - sig * (delta - g - (rgin - gg))
                if masked:
                    dz = jnp.where(mask[sl], dz, 0.0)
                a_parts.append(ab)
                dz_parts.append(dz.astype(BF16))
                c = c + rin[0:1, :]
                g = g + rgin[0:1, :]
            ab_all = jnp.concatenate(a_parts[::-1], axis=0)
            dzb = jnp.concatenate(dz_parts[::-1], axis=0)
            dk_ref[ks, :] += _dot(dzb, qn)
            dv_ref[ks, :] += _dot(ab_all, don)
            return c, g, dq + _dot(kt_ref[:, ks], dzb)

        row = jnp.zeros((1, ATT_BQ), F32)
        carry = (row, row, jnp.zeros((LANES, ATT_BQ), F32))
        dqt_ref[...] = _walk_down(i, step, lambda j, cr: jnp.max(cr[0]) > SKIP_BELOW, carry)[2]

        @pl.when(jnp.logical_and(h == H - 1, i == nq - 1))
        def _():
            for cp in _exchange_copies(ins, outs, *sems, True):
                cp.wait()

    any_spec = pl.BlockSpec(memory_space=pl.ANY)
    res = pl.pallas_call(
        body, name="sb_bwd", grid=(H, nq),
        out_shape=tuple([jax.ShapeDtypeStruct((H, LANES, S), F32), jax.ShapeDtypeStruct((H, S, LANES), F32),
                         jax.ShapeDtypeStruct((H, S, LANES), F32)] + _exchange_out_shapes(bound, True)),
        in_specs=[_t_block(LANES), _n_block(), _n_full(S), _t_full(LANES, S), _n_full(S),
                  _t_block(LANES), _n_block(), _t_block(HEAD_DIM)] + [any_spec] * n,
        out_specs=tuple([_t_block(LANES), _n_full(S), _n_full(S)] + [any_spec] * n),
        scratch_shapes=_exchange_sems(n),
        compiler_params=_params(dimension_semantics=("arbitrary", "arbitrary"), has_side_effects=True),
    )(q_t, q_n, k_n, k_t, v_n, do_t, do_n, o_t, *bound)
    return res[0], res[1], res[2], res[3:]


def _dqkv_assemble(dqf_t, dkf, dvf, dqs_t, dks, dvs):
    H, _, S = dqf_t.shape
    n_pairs = H // 2
    tm = _tile(S, 512)

    def body(dqf_ref, dkf_ref, dvf_ref, dqs_ref, dks_ref, dvs_ref, out_ref, dfk_ref):
        lane = lax.broadcasted_iota(jnp.int32, (1, LANES), 1)
        slabs = ((dqf_ref, True), (dkf_ref, False), (dvf_ref, False),
                 (dqs_ref, True), (dks_ref, False), (dvs_ref, False))
        for k, (ref, transposed) in enumerate(slabs):
            if transposed:
                t0, t1 = ref[0].T * _SCALE, ref[1].T * _SCALE
            else:
                t0, t1 = ref[0], ref[1]
            out_ref[k] = jnp.where(lane < HEAD_DIM, t0, pltpu.roll(t1, HEAD_DIM, 1)).astype(BF16)
        for e in range(2):
            dfk_ref[e] = dkf_ref[e].T[Q_ONE_LANE:Q_ONE_LANE + 1, :]

    t_spec = pl.BlockSpec((2, LANES, tm), lambda i, p: (p, 0, i))
    n_spec = pl.BlockSpec((2, tm, LANES), lambda i, p: (p, i, 0))
    return pl.pallas_call(
        body, name="dqkv_assemble", grid=(S // tm, n_pairs),
        out_shape=(jax.ShapeDtypeStruct((6, S, n_pairs * LANES), BF16), jax.ShapeDtypeStruct((H, 1, S), F32)),
        in_specs=[t_spec, n_spec, n_spec, t_spec, n_spec, n_spec],
        out_specs=(pl.BlockSpec((6, tm, LANES), lambda i, p: (0, i, p)),
                   pl.BlockSpec((2, 1, tm), lambda i, p: (p, 0, i))),
        compiler_params=_params(dimension_semantics=("parallel", "parallel")),
    )(dqf_t, dkf, dvf, dqs_t, dks, dvs)


def _acc_spec(w):
    return pl.BlockSpec((1, w), lambda i: (0, 0))


def _loss_head(x1, act, w_down, gate_m, g_final, target):
    S, D = x1.shape
    F = act.shape[1]
    tm = _tile(S, 256)

    def body(x1_ref, act_ref, w_ref, gt_ref, gf_ref, tg_ref, dx2_ref, gm_ref, loss_ref, dgf_ref, dgt_ref):
        @pl.when(pl.program_id(0) == 0)
        def _():
            loss_ref[...] = jnp.zeros_like(loss_ref)
            dgf_ref[...] = jnp.zeros_like(dgf_ref)
            dgt_ref[...] = jnp.zeros_like(dgt_ref)

        mo = _dot(act_ref[...], w_ref[...])
        x2 = x1_ref[...] + gt_ref[...] * mo
        r = lax.rsqrt(jnp.mean(x2 * x2, axis=-1, keepdims=True) + EPS)
        xh = x2 * r
        diff = xh * gf_ref[...] - tg_ref[...]
        loss_ref[...] += (0.5 / D) * jnp.sum(diff * diff)
        dy = diff * (1.0 / D)
        dgf_ref[...] += jnp.sum(dy * xh, axis=0, keepdims=True)
        dxh = dy * gf_ref[...]
        dx2 = r * (dxh - xh * jnp.mean(dxh * xh, axis=-1, keepdims=True))
        dx2_ref[...] = dx2
        gm_ref[...] = (dx2 * gt_ref[...]).astype(BF16)
        dgt_ref[...] += jnp.sum(dx2 * mo, axis=0, keepdims=True)

    return pl.pallas_call(
        body, name="loss_head", grid=(S // tm,),
        out_shape=(jax.ShapeDtypeStruct((S, D), F32), jax.ShapeDtypeStruct((S, D), BF16),
                   jax.ShapeDtypeStruct((1, LANES), F32), jax.ShapeDtypeStruct((1, D), F32),
                   jax.ShapeDtypeStruct((1, D), F32)),
        in_specs=_row_specs(tm, [D, F]) + [pl.BlockSpec((F, D), lambda i: (0, 0))] + [_vec_spec(D)] * 2
        + _row_specs(tm, [D]),
        out_specs=tuple(_row_specs(tm, [D, D]) + [_acc_spec(LANES), _acc_spec(D), _acc_spec(D)]),
        compiler_params=_params(dimension_semantics=("arbitrary",)),
    )(x1, act, w_down, gate_m, g_final, target)


def _norm_bwd(lhs, rhs, xin, dres, g, scale, name, gate=None, branch=None, bound=()):
    S, D = xin.shape
    tm = _tile(S, 256)
    gated = gate is not None
    nl, nb, n_steps = len(lhs), len(bound), S // tm
    n_out = 6 if gated else 4

    def body(*refs):
        l_refs, r_refs, rest = refs[:nl], refs[nl:2 * nl], refs[2 * nl:]
        if gated:
            x_ref, dr_ref, g_ref, sc_ref, gt_ref, br_ref = rest[:6]
            rest = rest[6:]
        else:
            x_ref, dr_ref, g_ref, sc_ref = rest[:4]
            rest = rest[4:]
        ins, outs, ex_outs, sems = rest[:nb], rest[nb:nb + n_out], rest[nb + n_out:2 * nb + n_out], rest[2 * nb + n_out:]
        dx_ref, dsc_ref, dsh_ref, dg_ref = outs[:4]
        sums = (dsc_ref, dsh_ref, dg_ref) + ((outs[5],) if gated else ())
        i = pl.program_id(0)

        @pl.when(i == 0)
        def _():
            for s_ref in sums:
                s_ref[...] = jnp.zeros_like(s_ref)
            if nb:
                for cp in _exchange_copies(ins, ex_outs, *sems, True):
                    cp.start()

        dhv = None
        for l_ref, r_ref in zip(l_refs, r_refs):
            if len(l_ref.shape) == 3:
                K = l_ref.shape[2]
                terms = [_dot_nt(l_ref[k], r_ref[:, k * K:(k + 1) * K]) for k in range(l_ref.shape[0])]
            else:
                terms = [_dot_nt(l_ref[...], r_ref[...])]
            for t in terms:
                dhv = t if dhv is None else dhv + t
        xv = x_ref[...]
        r = lax.rsqrt(jnp.mean(xv * xv, axis=-1, keepdims=True) + EPS)
        xh = xv * r
        dsc_ref[...] += jnp.sum(dhv * (xh * g_ref[...]), axis=0, keepdims=True)
        dsh_ref[...] += jnp.sum(dhv, axis=0, keepdims=True)
        dn = dhv * (1.0 + sc_ref[...])
        dg_ref[...] += jnp.sum(dn * xh, axis=0, keepdims=True)
        dxh = dn * g_ref[...]
        dx = dr_ref[...] + r * (dxh - xh * jnp.mean(dxh * xh, axis=-1, keepdims=True))
        dx_ref[...] = dx
        if gated:
            outs[4][...] = (dx * gt_ref[...]).astype(BF16)
            outs[5][...] += jnp.sum(dx * br_ref[...], axis=0, keepdims=True)

        if nb:
            @pl.when(i == n_steps - 1)
            def _():
                for cp in _exchange_copies(ins, ex_outs, *sems, True):
                    cp.wait()

    def l_spec(a):
        if a.ndim == 3:
            return pl.BlockSpec((a.shape[0], tm, a.shape[2]), lambda i: (0, i, 0))
        return pl.BlockSpec((tm, a.shape[1]), lambda i: (i, 0))

    any_spec = pl.BlockSpec(memory_space=pl.ANY)
    vec = jax.ShapeDtypeStruct((1, D), F32)
    out_shape = [jax.ShapeDtypeStruct((S, D), F32), vec, vec, vec]
    out_specs = _row_specs(tm, [D]) + [_acc_spec(D)] * 3
    in_specs = [l_spec(a) for a in lhs] + [pl.BlockSpec(b.shape, lambda i: (0, 0)) for b in rhs]
    in_specs += _row_specs(tm, [D, D]) + [_vec_spec(D)] * 2
    args = list(lhs) + list(rhs) + [xin, dres, g, scale]
    if gated:
        out_shape += [jax.ShapeDtypeStruct((S, D), BF16), vec]
        out_specs += _row_specs(tm, [D]) + [_acc_spec(D)]
        in_specs += [_vec_spec(D)] + _row_specs(tm, [D])
        args += [gate, branch]
    res = pl.pallas_call(
        body, name=name, grid=(n_steps,), out_shape=tuple(out_shape + _exchange_out_shapes(bound, True)),
        in_specs=in_specs + [any_spec] * nb, out_specs=tuple(out_specs + [any_spec] * nb),
        scratch_shapes=_exchange_sems(nb) if nb else [],
        compiler_params=_params(dimension_semantics=("arbitrary",), has_side_effects=bool(nb)),
    )(*args, *bound)
    return tuple(res[:n_out]) + (tuple(res[n_out:]),)


def _headnorm_bwd(dmix, o_f, o_s, g_f, g_s):
    S, dh = o_f.shape
    H = dh // HEAD_DIM
    tm = _tile(S, 256)

    def body(dm_ref, of_ref, os_ref, gf_ref, gs_ref, fn_ref, ft_ref, sn_ref, st_ref, dgf_ref, dgs_ref):
        @pl.when(pl.program_id(0) == 0)
        def _():
            dgf_ref[...] = jnp.zeros_like(dgf_ref)
            dgs_ref[...] = jnp.zeros_like(dgs_ref)

        ones = _group_ones()
        lane = lax.broadcasted_iota(jnp.int32, (1, LANES), 1)
        parts = ((of_ref, gf_ref, fn_ref, ft_ref, dgf_ref), (os_ref, gs_ref, sn_ref, st_ref, dgs_ref))
        for part, (o_ref, g_ref, n_ref, t_ref, dg_ref) in enumerate(parts):
            for t in range(dh // LANES):
                cols = slice(t * LANES, (t + 1) * LANES)
                o = o_ref[:, cols]
                dm = dm_ref[:, part * dh + t * LANES: part * dh + (t + 1) * LANES]
                r = lax.rsqrt(_dot(o * o, ones, precision=HIGHEST) * (1.0 / HEAD_DIM) + EPS)
                oh = o * r
                dg_ref[:, cols] += jnp.sum(dm * oh, axis=0, keepdims=True)
                dn = dm * g_ref[:, cols]
                mean = _dot(dn * oh, ones, precision=HIGHEST) * (1.0 / HEAD_DIM)
                do = r * (dn - oh * mean)
                for e in range(2):
                    d = do if e == 0 else pltpu.roll(do, HEAD_DIM, 1)
                    d = jnp.where(lane < HEAD_DIM, d, 0.0)
                    n_ref[2 * t + e] = d.astype(BF16)
                    t_ref[2 * t + e] = d.T.astype(BF16)

    vec = jax.ShapeDtypeStruct((1, dh), F32)
    n_sds = jax.ShapeDtypeStruct((H, S, LANES), BF16)
    t_sds = jax.ShapeDtypeStruct((H, LANES, S), BF16)
    n_spec = pl.BlockSpec((H, tm, LANES), lambda i: (0, i, 0))
    t_spec = pl.BlockSpec((H, LANES, tm), lambda i: (0, 0, i))
    return pl.pallas_call(
        body, name="headnorm_bwd", grid=(S // tm,),
        out_shape=(n_sds, t_sds, n_sds, t_sds, vec, vec),
        in_specs=_row_specs(tm, [2 * dh, dh, dh]) + [_vec_spec(dh)] * 2,
        out_specs=(n_spec, t_spec, n_spec, t_spec, _acc_spec(dh), _acc_spec(dh)),
        compiler_params=_params(dimension_semantics=("arbitrary",)),
    )(dmix, o_f, o_s, g_f, g_s)


def _adamw(w, gslots, m, v, name):
    R, C = w.shape
    n = gslots.shape[0]
    tr = 256 if (R % 256 == 0 and R > 256) else R
    bc1 = 1.0 - ADAM_B1 ** ADAM_STEP
    bc2 = 1.0 - ADAM_B2 ** ADAM_STEP

    def body(w_ref, gs_ref, m_ref, v_ref, g_ref, d_ref, nm_ref, nv_ref):
        g = gs_ref[0]
        for s in range(1, n):
            g = g + gs_ref[s]
        nm = ADAM_B1 * m_ref[...] + (1.0 - ADAM_B1) * g
        nv = ADAM_B2 * v_ref[...] + (1.0 - ADAM_B2) * (g * g)
        g_ref[...] = g
        nm_ref[...] = nm
        nv_ref[...] = nv
        d_ref[...] = -ADAM_LR * ((nm / bc1) / (jnp.sqrt(nv / bc2) + ADAM_EPS) + ADAM_WD * w_ref[...])

    blk = pl.BlockSpec((tr, C), lambda i: (i, 0))
    sds = jax.ShapeDtypeStruct((R, C), F32)
    return pl.pallas_call(
        body, name=name, grid=(R // tr,), out_shape=(sds,) * 4,
        in_specs=[blk, pl.BlockSpec((n, tr, C), lambda i: (0, i, 0)), blk, blk], out_specs=(blk,) * 4,
        compiler_params=_params(dimension_semantics=("parallel",)),
    )(w, gslots, m, v)


def _slot_sum(slots, name):
    n, _, C = slots.shape

    def body(s_ref, o_ref):
        acc = s_ref[0]
        for s in range(1, n):
            acc = acc + s_ref[s]
        o_ref[...] = acc

    return pl.pallas_call(body, name=name, out_shape=jax.ShapeDtypeStruct((1, C), F32),
                          compiler_params=_params())(slots)


def _pad_cols(a, n):
    return jnp.pad(a, ((0, 0), (0, n - a.shape[1])))


def _ungather(g, axis):
    if axis == 0:
        return g.reshape(g.shape[0] * g.shape[1], g.shape[2])
    return jnp.transpose(g, (1, 0, 2)).reshape(g.shape[1], g.shape[0] * g.shape[2])


def _to_slots(full, axis):
    R, C = full.shape
    if axis == 0:
        return full.reshape(N_DEV, R // N_DEV, C)
    return jnp.transpose(full.reshape(R, N_DEV, C // N_DEV), (1, 0, 2))


def kernel(x, c, w_ada, b_ada, g_attn, w_in, b_fgate, g_out_fox, g_out_sb, w_out, g_mlp, w_up, conv_w, conv_b, w_down, g_final, loss_target, m_w_ada, m_b_ada, m_g_attn, m_w_in, m_b_fgate, m_g_out_fox, m_g_out_sb, m_w_out, m_g_mlp, m_w_up, m_conv_w, m_conv_b, m_w_down, m_g_final, v_w_ada, v_b_ada, v_g_attn, v_w_in, v_b_fgate, v_g_out_fox, v_g_out_sb, v_w_out, v_g_mlp, v_w_up, v_conv_w, v_conv_b, v_w_down, v_g_final):
    S, D = x.shape[1], x.shape[2]
    dh = D // 2
    n_heads = dh // HEAD_DIM
    n_qkv = 6 * dh
    ff = w_down.shape[1] * N_DEV
    ffp = -(-ff // (2 * LANES)) * (2 * LANES)
    nc = S // LANES
    me = 4 * lax.axis_index("x") + 2 * lax.axis_index("y") + lax.axis_index("c")
    xs, tgt = x[0], loss_target[0]

    c_all, win_g = _gather_two_level([c, w_in[0].astype(BF16)], name="gather_first")
    c_all = c_all.reshape(N_DEV, D)
    W_in = _ungather(win_g, 1)
    W_qkv, W_f = W_in[:, :n_qkv], _pad_cols(W_in[:, n_qkv:], LANES)
    cb_g, cb_v = _pad_cols(conv_b[:, :ff], ffp), _pad_cols(conv_b[:, ff:], ffp)

    n_ada = w_ada.shape[2]
    b_shard = lax.dynamic_slice(b_ada, (0, me * n_ada), (1, n_ada))
    mod_cols = _ada_fwd(c_all, w_ada[0], b_shard)
    (mod_g,) = _exchange([mod_cols], scatter=False, name="gather_mod")
    mod = lax.dynamic_index_in_dim(mod_g, me, axis=1, keepdims=False).reshape(6, 1, D)
    shift_a, scale_a, gate_a, shift_m, scale_m, gate_m = [mod[k] for k in range(6)]

    h1, h1_t = _prenorm(xs, g_attn, scale_a, shift_a, "prenorm_attn")
    qkv = _mm(h1, W_qkv, BF16, "proj_qkv")
    flog = _mm(h1, W_f, F32, "proj_fgate")
    zf = flog[:, :n_heads] + b_fgate
    z_rows = zf.T.reshape(n_heads * nc, LANES)
    f_rows = _fgate_fwd(z_rows, nc).reshape(n_heads, S)
    f_pairs = jnp.transpose(f_rows.reshape(n_heads // 2, 2, S), (0, 2, 1))
    fox, sb = _att_prep(qkv, f_pairs)
    f_end, k_max = _skip_bounds(qkv[:, dh:2 * dh], f_rows)
    of_t, lse, (wout_g, wup_g, wdown_g, convw_g) = _fox_fwd(
        fox["q_t"], fox["k_n"], fox["v_t"], f_end, k_max,
        [w_out[0].astype(BF16), w_up[0].astype(BF16), w_down[0].astype(BF16), conv_w[0]])
    W_out = _ungather(wout_g, 0)
    W_up = _ungather(wup_g, 1)
    W_g, W_v = _pad_cols(W_up[:, :ff], ffp), _pad_cols(W_up[:, ff:], ffp)
    W_down = jnp.pad(_ungather(wdown_g, 0), ((0, ffp - ff), (0, 0)))
    cw_full = _ungather(convw_g, 1)
    cw_g, cw_v = _pad_cols(cw_full[:, :ff], ffp), _pad_cols(cw_full[:, ff:], ffp)
    os_t = _sb_fwd(sb["q_t"], sb["k_n"], sb["v_t"])
    o_f, o_s = _unheads(_tr(of_t)), _unheads(_tr(os_t))
    mix, mix_t = _headnorm_fwd(o_f, o_s, g_out_fox, g_out_sb)
    a_out = _mm(mix, W_out, F32, "proj_out")
    x1, h2, h2_t = _resid_prenorm(xs, a_out, gate_a, g_mlp, scale_m, shift_m)
    up_g, up_v, act, act_t = _mlp_up(h2, W_g, W_v, cw_g, cw_v, cb_g, cb_v)

    dx2, gm, loss_p, dg_final, dgate_m = _loss_head(x1, act, W_down, gate_m, g_final.reshape(1, D), tgt)
    dact = _mm(gm, W_down, BF16, "bwd_down_act", tn=1408, nt=True)
    dW_down = _mm_acc(act_t, gm, "bwd_down_w")
    du_g, du_v, p_g, p_v = _conv_act_bwd(dact, up_g, up_v, cw_g, cw_v, cb_g, cb_v)
    dup_g = _conv_bwd_input(du_g, cw_g, "conv_bwd_input_g")
    dup_v = _conv_bwd_input(du_v, cw_v, "conv_bwd_input_v")
    dW_g = _mm_acc(h2_t, dup_g, "bwd_up_w_g")
    dW_v = _mm_acc(h2_t, dup_v, "bwd_up_w_v")
    dx1, dscale_m, dshift_m, dg_mlp, ga, dgate_a, _ = _norm_bwd(
        [dup_g, dup_v], [W_g, W_v], x1, dx2, g_mlp, scale_m, "norm_mlp_bwd", gate=gate_a, branch=a_out)
    dmix = _mm(ga, W_out, F32, "bwd_out_act", nt=True)
    dW_out = _mm_acc(mix_t, ga, "bwd_out_w")
    dof_n, dof_t, dos_n, dos_t, dg_fox, dg_sb = _headnorm_bwd(dmix, o_f, o_s, g_out_fox, g_out_sb)
    dqf_t, dkf, dvf = _fox_bwd(fox["q_t"], fox["q_n"], fox["k_n"], fox["k_t"], fox["v_n"], dof_t, dof_n, of_t, lse,
                              f_end, k_max)
    dW_upf = jnp.concatenate([dW_g[:, :ff], dW_v[:, :ff]], axis=1)
    dcw = jnp.concatenate([p_g[:CONV_W, :ff], p_v[:CONV_W, :ff]], axis=1)
    dqs_t, dks, dvs, (s_out, s_up, s_down, s_cw) = _sb_bwd(
        sb["q_t"], sb["q_n"], sb["k_n"], sb["k_t"], sb["v_n"], dos_t, dos_n, os_t,
        [_to_slots(dW_out, 0), _to_slots(dW_upf, 1), _to_slots(dW_down[:ff], 0), _to_slots(dcw, 1)])
    dparts, dfk = _dqkv_assemble(dqf_t, dkf, dvf, dqs_t, dks, dvs)
    dz_rows, db_fgate = _fgate_bwd(dfk.reshape(n_heads * nc, LANES),
                                   dqf_t[:, Q_F_LANE, :].reshape(n_heads * nc, LANES), z_rows, nc)
    dzf = _pad_cols(dz_rows.reshape(n_heads, S).T, LANES).astype(BF16)
    dW_qkv = _mm_acc_parts(h1_t, dparts, "bwd_in_w")
    dW_f = _mm_acc(h1_t, dzf, "bwd_in_w_fgate")
    dW_in = jnp.concatenate([jnp.transpose(dW_qkv, (1, 0, 2)).reshape(D, n_qkv), dW_f[:, :n_heads]], axis=1)
    grad_x, dscale_a, dshift_a, dg_attn, (s_in,) = _norm_bwd(
        [dparts, dzf], [W_qkv, W_f], xs, dx1, g_attn, scale_a, "norm_attn_bwd", bound=[_to_slots(dW_in, 1)])

    dconv_b = jnp.concatenate([p_g[CONV_W:CONV_W + 1, :ff], p_v[CONV_W:CONV_W + 1, :ff]], axis=1)
    parts = [dshift_a, dscale_a, dgate_a, dshift_m, dscale_m, dgate_m,
             dg_attn, db_fgate.reshape(1, n_heads), dg_fox, dg_sb, dg_mlp, dconv_b, dg_final,
             loss_p[:, :1]]
    sizes = [p.shape[1] for p in parts]
    vec = jnp.concatenate(parts, axis=1)
    n_vec = -(-vec.shape[1] // LANES) * LANES
    vec = _pad_cols(vec, n_vec)
    (vec_g,) = _exchange([vec], scatter=False, name="gather_small")
    offs = [0]
    for s in sizes:
        offs.append(offs[-1] + s)

    def small(k0, k1=None):
        k1 = k0 if k1 is None else k1
        return vec_g[:, :, offs[k0]:offs[k1 + 1]]

    dmod_all = small(0, 5).reshape(N_DEV, 6 * D)
    dmod_cols = lax.dynamic_slice(dmod_all, (0, me * n_ada), (N_DEV, n_ada))
    dW_ada = _ada_bwd(c_all.T, dmod_cols)


    res = {}
    res["w_ada"] = _adamw(w_ada[0], dW_ada[None], m_w_ada[0], v_w_ada[0], "adamw_w_ada")
    res["w_in"] = _adamw(w_in[0], s_in, m_w_in[0], v_w_in[0], "adamw_w_in")
    res["w_out"] = _adamw(w_out[0], s_out, m_w_out[0], v_w_out[0], "adamw_w_out")
    res["w_up"] = _adamw(w_up[0], s_up, m_w_up[0], v_w_up[0], "adamw_w_up")
    res["w_down"] = _adamw(w_down[0], s_down, m_w_down[0], v_w_down[0], "adamw_w_down")
    res["conv_w"] = _adamw(conv_w[0], s_cw, m_conv_w[0], v_conv_w[0], "adamw_conv_w")
    small_names = ["b_ada", "g_attn", "b_fgate", "g_out_fox", "g_out_sb", "g_mlp", "conv_b", "g_final"]
    small_w = [b_ada, g_attn, b_fgate, g_out_fox, g_out_sb, g_mlp, conv_b, g_final.reshape(1, D)]
    small_m = [m_b_ada, m_g_attn, m_b_fgate, m_g_out_fox, m_g_out_sb, m_g_mlp, m_conv_b, m_g_final.reshape(1, D)]
    small_v = [v_b_ada, v_g_attn, v_b_fgate, v_g_out_fox, v_g_out_sb, v_g_mlp, v_conv_b, v_g_final.reshape(1, D)]
    small_res = _adamw(jnp.concatenate(small_w, axis=1), small(0, 12), jnp.concatenate(small_m, axis=1),
                       jnp.concatenate(small_v, axis=1), "adamw_small")
    lo = 0
    for nm, wv in zip(small_names, small_w):
        res[nm] = tuple(r[:, lo:lo + wv.shape[1]] for r in small_res)
        lo += wv.shape[1]
    loss = _slot_sum(_pad_cols(small(13).reshape(N_DEV, 1), LANES).reshape(N_DEV, 1, LANES), "loss_sum")[0, 0]

    names = ["w_ada", "b_ada", "g_attn", "w_in", "b_fgate", "g_out_fox", "g_out_sb", "w_out", "g_mlp",
             "w_up", "conv_w", "conv_b", "w_down", "g_final"]

    def shaped(n, a):
        if n == "g_final":
            return a.reshape(D)
        if n in ("b_ada", "g_attn", "b_fgate", "g_out_fox", "g_out_sb", "g_mlp", "conv_b"):
            return a
        return a[None]

    outs = [loss, grad_x[None]]
    for k in range(4):
        outs += [shaped(n, res[n][k]) for n in names]
    return tuple(outs)
```

```python
import jax
import jax.numpy as jnp
from jax import lax
from jax.experimental import pallas as pl
from jax.experimental.pallas import tpu as pltpu

F32 = jnp.float32
BF16 = jnp.bfloat16
HIGHEST = lax.Precision.HIGHEST

N_DEV = 8
LANES = 128
HEAD_DIM = 64
EPS = 1e-6
CONV_W = 3
CONV_COLS = 1408
HALO = 16
ATT_BQ = 512
ATT_BK = 512
SCAN_BK = 128
VMEM_LIMIT = 56 * 1024 * 1024

ADAM_LR = 0.001
ADAM_B1 = 0.9
ADAM_B2 = 0.999
ADAM_EPS = 1e-08
ADAM_WD = 0.01
ADAM_STEP = 10


def _params(**kw):
    return pltpu.CompilerParams(vmem_limit_bytes=VMEM_LIMIT, **kw)


def _tile(n, cap):
    if n <= cap:
        return n
    best = None
    for t in range(LANES, cap + 1, LANES):
        if n % t == 0:
            best = t
    assert best is not None, (n, cap)
    return best


def _dot(a, b, **kw):
    return jnp.dot(a, b, preferred_element_type=F32, **kw)


def _exchange_copies(ins, outs, send_sems, recv_sems, loc_sems, scatter):
    n = len(ins)
    if n == 0:
        return []
    x, y, c = lax.axis_index("x"), lax.axis_index("y"), lax.axis_index("c")
    me = 4 * x + 2 * y + c
    copies = []
    for a in range(n):
        src = ins[a].at[me] if scatter else ins[a]
        copies.append(pltpu.make_async_copy(src, outs[a].at[me], loc_sems.at[a]))
    for k in range(1, N_DEV):
        px = 1 - x if k & 4 else x
        py = 1 - y if k & 2 else y
        pc = 1 - c if k & 1 else c
        peer = 4 * px + 2 * py + pc
        for a in range(n):
            src = ins[a].at[peer] if scatter else ins[a]
            copies.append(pltpu.make_async_remote_copy(
                src_ref=src, dst_ref=outs[a].at[me],
                send_sem=send_sems.at[a, k - 1], recv_sem=recv_sems.at[a, k - 1],
                device_id=(px, py, pc), device_id_type=pl.DeviceIdType.MESH))
    return copies


def _exchange_out_shapes(arrays, scatter):
    return [jax.ShapeDtypeStruct((N_DEV,) + tuple(a.shape[1:] if scatter else a.shape), a.dtype) for a in arrays]


def _exchange_sems(n):
    return [pltpu.SemaphoreType.DMA((n, N_DEV - 1)), pltpu.SemaphoreType.DMA((n, N_DEV - 1)),
            pltpu.SemaphoreType.DMA((n,))]


def _exchange(arrays, scatter, name):
    n = len(arrays)

    def body(*refs):
        copies = _exchange_copies(refs[:n], refs[n:2 * n], *refs[2 * n:], scatter)
        for cp in copies:
            cp.start()
        for cp in copies:
            cp.wait()

    any_spec = pl.BlockSpec(memory_space=pl.ANY)
    return pl.pallas_call(
        body, name=name, out_shape=tuple(_exchange_out_shapes(arrays, scatter)),
        in_specs=[any_spec] * n, out_specs=tuple([any_spec] * n),
        scratch_shapes=_exchange_sems(n),
        compiler_params=pltpu.CompilerParams(has_side_effects=True),
    )(*arrays)


def _gather_two_level(arrays, name):
    n = len(arrays)
    out_shape = [jax.ShapeDtypeStruct((N_DEV,) + tuple(a.shape), a.dtype) for a in arrays]

    def body(*refs):
        ins, outs = refs[:n], refs[n:2 * n]
        send_sems, recv_sems, loc_sems = refs[2 * n:]
        x, y, c = lax.axis_index("x"), lax.axis_index("y"), lax.axis_index("c")
        me, sibling = (x, y, c), (x, y, 1 - c)
        chips = [(1 - x, y), (x, 1 - y), (1 - x, 1 - y)]

        def slot(px, py, pc):
            return 4 * px + 2 * py + pc

        def copy(a, k, block, to, src=None):
            dst = outs[a].at[slot(*block)]
            return pltpu.make_async_remote_copy(
                src_ref=dst if src is None else src, dst_ref=dst,
                send_sem=send_sems.at[a, k], recv_sem=recv_sems.at[a, k],
                device_id=to, device_id_type=pl.DeviceIdType.MESH)

        local = [pltpu.make_async_copy(ins[a], outs[a].at[slot(*me)], loc_sems.at[a]) for a in range(n)]
        for cp in local:
            cp.start()
        first = []
        for a in range(n):
            first.append(copy(a, 0, me, sibling, src=ins[a]))
            first += [copy(a, 1 + j, me, (*chip, c), src=ins[a]) for j, chip in enumerate(chips)]
        for cp in first:
            cp.start()
        passed = []
        for j, chip in enumerate(chips):
            for a in range(n):
                copy(a, 1 + j, (*chip, c), me).wait_recv()
                cp = copy(a, 4 + j, (*chip, c), sibling)
                cp.start()
                passed.append(cp)
        for a in range(n):
            copy(a, 0, sibling, me).wait_recv()
            for j, chip in enumerate(chips):
                copy(a, 4 + j, (*chip, 1 - c), me).wait_recv()
        for cp in first + passed:
            cp.wait_send()
        for cp in local:
            cp.wait()

    any_spec = pl.BlockSpec(memory_space=pl.ANY)
    return pl.pallas_call(
        body, name=name, out_shape=tuple(out_shape),
        in_specs=[any_spec] * n, out_specs=tuple([any_spec] * n),
        scratch_shapes=[pltpu.SemaphoreType.DMA((n, N_DEV - 1)), pltpu.SemaphoreType.DMA((n, N_DEV - 1)),
                        pltpu.SemaphoreType.DMA((n,))],
        compiler_params=pltpu.CompilerParams(has_side_effects=True),
    )(*arrays)


N_CHIPS = 4


def _scatter_to_sibling(arrays, name):
    n = len(arrays)
    out_shape = [jax.ShapeDtypeStruct((N_CHIPS,) + tuple(a.shape[2:]), a.dtype) for a in arrays]

    def body(*refs):
        ins, outs = refs[:n], refs[n:2 * n]
        send_sems, recv_sems = refs[2 * n:]
        x, y, c = lax.axis_index("x"), lax.axis_index("y"), lax.axis_index("c")
        copies = []
        for a in range(n):
            for q in range(N_CHIPS):
                cp = pltpu.make_async_remote_copy(
                    src_ref=ins[a].at[q, 1 - c], dst_ref=outs[a].at[q],
                    send_sem=send_sems.at[a, q], recv_sem=recv_sems.at[a, q],
                    device_id=(x, y, 1 - c), device_id_type=pl.DeviceIdType.MESH)
                cp.start()
                copies.append(cp)
        for cp in copies:
            cp.wait()

    any_spec = pl.BlockSpec(memory_space=pl.ANY)
    return pl.pallas_call(
        body, name=name, out_shape=tuple(out_shape),
        in_specs=[any_spec] * n, out_specs=tuple([any_spec] * n),
        scratch_shapes=[pltpu.SemaphoreType.DMA((n, N_CHIPS)), pltpu.SemaphoreType.DMA((n, N_CHIPS))],
        compiler_params=pltpu.CompilerParams(has_side_effects=True),
    )(*arrays)


def _pair_add(mine, got, c_idx, name):
    _, _, R, C = mine.shape
    tr = 256 if (R % 256 == 0 and R > 256) else R

    def body(c_ref, m_ref, g_ref, o_ref):
        o_ref[...] = m_ref[...] + g_ref[...]

    grid_spec = pltpu.PrefetchScalarGridSpec(
        num_scalar_prefetch=1, grid=(N_CHIPS, R // tr),
        in_specs=[pl.BlockSpec((None, None, tr, C), lambda q, i, c_ref: (q, c_ref[0], i, 0)),
                  pl.BlockSpec((None, tr, C), lambda q, i, c_ref: (q, i, 0))],
        out_specs=pl.BlockSpec((None, tr, C), lambda q, i, c_ref: (q, i, 0)))
    return pl.pallas_call(
        body, name=name, grid_spec=grid_spec, out_shape=jax.ShapeDtypeStruct((N_CHIPS, R, C), mine.dtype),
        compiler_params=_params(dimension_semantics=("parallel", "parallel")),
    )(c_idx, mine, got)


def _chip_exchange_copies(ins, outs, send_sems, recv_sems, loc_sems):
    n = len(ins)
    x, y, c = lax.axis_index("x"), lax.axis_index("y"), lax.axis_index("c")
    myq = 2 * x + y
    copies = [pltpu.make_async_copy(ins[a].at[myq], outs[a].at[myq], loc_sems.at[a]) for a in range(n)]
    for k in range(1, N_CHIPS):
        qx = 1 - x if k & 2 else x
        qy = 1 - y if k & 1 else y
        for a in range(n):
            copies.append(pltpu.make_async_remote_copy(
                src_ref=ins[a].at[2 * qx + qy], dst_ref=outs[a].at[myq],
                send_sem=send_sems.at[a, k - 1], recv_sem=recv_sems.at[a, k - 1],
                device_id=(qx, qy, c), device_id_type=pl.DeviceIdType.MESH))
    return copies


def _chip_exchange_sems(n):
    return [pltpu.SemaphoreType.DMA((n, N_CHIPS - 1)), pltpu.SemaphoreType.DMA((n, N_CHIPS - 1)),
            pltpu.SemaphoreType.DMA((n,))]


def _dot_nt(a, b):
    return lax.dot_general(a, b, (((1,), (1,)), ((), ())), preferred_element_type=F32)


def _rhs_spec(b, tn, nt):
    if nt:
        return pl.BlockSpec((tn, b.shape[1]), lambda i, j: (j, 0))
    return pl.BlockSpec((b.shape[0], tn), lambda i, j: (0, j))


def _mm(a, b, out_dtype, name, tm=1024, tn=512, nt=False):
    M, K = a.shape
    N = b.shape[0] if nt else b.shape[1]
    tm, tn = _tile(M, tm), _tile(N, tn)
    dot = _dot_nt if nt else _dot

    def body(a_ref, b_ref, o_ref):
        o_ref[...] = dot(a_ref[...], b_ref[...]).astype(out_dtype)

    return pl.pallas_call(
        body, name=name, out_shape=jax.ShapeDtypeStruct((M, N), out_dtype),
        grid=(M // tm, N // tn),
        in_specs=[pl.BlockSpec((tm, K), lambda i, j: (i, 0)), _rhs_spec(b, tn, nt)],
        out_specs=pl.BlockSpec((tm, tn), lambda i, j: (i, j)),
        compiler_params=_params(dimension_semantics=("parallel", "parallel")),
    )(a, b)


def _mm_acc(a, b, name, tm=1408, tn=1408, tk=512):
    M, S = a.shape
    _, N = b.shape
    tm, tn, tk = _tile(M, tm), _tile(N, tn), _tile(S, tk)

    def body(a_ref, b_ref, o_ref):
        @pl.when(pl.program_id(2) == 0)
        def _():
            o_ref[...] = jnp.zeros_like(o_ref)

        o_ref[...] += _dot(a_ref[...], b_ref[...])

    return pl.pallas_call(
        body, name=name, out_shape=jax.ShapeDtypeStruct((M, N), F32),
        grid=(M // tm, N // tn, S // tk),
        in_specs=[pl.BlockSpec((tm, tk), lambda i, j, k: (i, k)), pl.BlockSpec((tk, tn), lambda i, j, k: (k, j))],
        out_specs=pl.BlockSpec((tm, tn), lambda i, j, k: (i, j)),
        compiler_params=_params(dimension_semantics=("parallel", "parallel", "arbitrary")),
    )(a, b)


def _mm_acc_parts(a, parts, name, tm=1024, tk=1024):
    M, S = a.shape
    P, _, K = parts.shape
    tm, tk = _tile(M, tm), _tile(S, tk)

    def body(a_ref, b_ref, o_ref):
        @pl.when(pl.program_id(2) == 0)
        def _():
            o_ref[...] = jnp.zeros_like(o_ref)

        o_ref[...] += _dot(a_ref[...], b_ref[...])

    return pl.pallas_call(
        body, name=name, out_shape=jax.ShapeDtypeStruct((P, M, K), F32), grid=(P, M // tm, S // tk),
        in_specs=[pl.BlockSpec((tm, tk), lambda k, i, s: (i, s)), pl.BlockSpec((None, tk, K), lambda k, i, s: (k, s, 0))],
        out_specs=pl.BlockSpec((None, tm, K), lambda k, i, s: (k, i, 0)),
        compiler_params=_params(dimension_semantics=("parallel", "parallel", "arbitrary")),
    )(a, parts)


def _silu(z):
    return z * (1.0 / (1.0 + jnp.exp(-z)))


def _ada_fwd(c_all, w_shard, b_shard):
    n = w_shard.shape[1]

    def body(c_ref, w_ref, b_ref, o_ref):
        o_ref[...] = _dot(_silu(c_ref[...]), w_ref[...], precision=HIGHEST) + b_ref[...]

    return pl.pallas_call(body, name="ada_fwd", out_shape=jax.ShapeDtypeStruct((N_DEV, n), F32),
                          compiler_params=_params())(c_all, w_shard, b_shard)


def _ada_bwd(c_all_t, dmod_cols):
    D = c_all_t.shape[0]
    n = dmod_cols.shape[1]

    def body(ct_ref, dm_ref, o_ref):
        sc = _silu(ct_ref[...])
        dm = dm_ref[...]
        acc = sc[:, 0:1] * dm[0:1, :]
        for b in range(1, N_DEV):
            acc = acc + sc[:, b:b + 1] * dm[b:b + 1, :]
        o_ref[...] = acc

    return pl.pallas_call(body, name="ada_bwd", out_shape=jax.ShapeDtypeStruct((D, n), F32),
                          compiler_params=_params())(c_all_t, dmod_cols)


def _row_specs(tm, widths):
    return [pl.BlockSpec((tm, w), lambda i: (i, 0)) for w in widths]


def _vec_spec(w):
    return pl.BlockSpec((1, w), lambda i: (0, 0))


def _col_spec(tm, w):
    return pl.BlockSpec((w, tm), lambda i: (0, i))


def _prenorm(x, g, scale, shift, name):
    S, D = x.shape
    tm = _tile(S, 512)

    def body(x_ref, g_ref, sc_ref, sh_ref, h_ref, ht_ref):
        xv = x_ref[...]
        r = lax.rsqrt(jnp.mean(xv * xv, axis=-1, keepdims=True) + EPS)
        h = (xv * r) * g_ref[...] * (1.0 + sc_ref[...]) + sh_ref[...]
        h_ref[...] = h.astype(BF16)
        ht_ref[...] = h.T.astype(BF16)

    return pl.pallas_call(
        body, name=name, grid=(S // tm,),
        out_shape=(jax.ShapeDtypeStruct((S, D), BF16), jax.ShapeDtypeStruct((D, S), BF16)),
        in_specs=_row_specs(tm, [D]) + [_vec_spec(D)] * 3,
        out_specs=(_row_specs(tm, [D])[0], _col_spec(tm, D)),
        compiler_params=_params(dimension_semantics=("parallel",)),
    )(x, g, scale, shift)


def _group_ones():
    r = lax.broadcasted_iota(jnp.int32, (LANES, LANES), 0) // HEAD_DIM
    c = lax.broadcasted_iota(jnp.int32, (LANES, LANES), 1) // HEAD_DIM
    return (r == c).astype(F32)


def _headnorm_fwd(o_f, o_s, g_f, g_s):
    S, dh = o_f.shape
    tm = _tile(S, 512)

    def body(of_ref, os_ref, gf_ref, gs_ref, mix_ref, mixt_ref):
        ones = _group_ones()
        for part, (o_ref, g_ref) in enumerate(((of_ref, gf_ref), (os_ref, gs_ref))):
            for t in range(dh // LANES):
                cols = slice(t * LANES, (t + 1) * LANES)
                out = slice(part * dh + t * LANES, part * dh + (t + 1) * LANES)
                o = o_ref[:, cols]
                ms = _dot(o * o, ones, precision=HIGHEST) * (1.0 / HEAD_DIM)
                mixn = o * lax.rsqrt(ms + EPS) * g_ref[:, cols]
                mix_ref[:, out] = mixn.astype(BF16)
                mixt_ref[out, :] = mixn.T.astype(BF16)

    return pl.pallas_call(
        body, name="headnorm_fwd", grid=(S // tm,),
        out_shape=(jax.ShapeDtypeStruct((S, 2 * dh), BF16), jax.ShapeDtypeStruct((2 * dh, S), BF16)),
        in_specs=_row_specs(tm, [dh, dh]) + [_vec_spec(dh)] * 2,
        out_specs=(_row_specs(tm, [2 * dh])[0], _col_spec(tm, 2 * dh)),
        compiler_params=_params(dimension_semantics=("parallel",)),
    )(o_f, o_s, g_f, g_s)


def _resid_prenorm(x, a_out, gate, g, scale, shift):
    S, D = x.shape
    tm = _tile(S, 512)

    def body(x_ref, a_ref, gt_ref, g_ref, sc_ref, sh_ref, x1_ref, h_ref, ht_ref):
        x1 = x_ref[...] + gt_ref[...] * a_ref[...]
        x1_ref[...] = x1
        r = lax.rsqrt(jnp.mean(x1 * x1, axis=-1, keepdims=True) + EPS)
        h = (x1 * r) * g_ref[...] * (1.0 + sc_ref[...]) + sh_ref[...]
        h_ref[...] = h.astype(BF16)
        ht_ref[...] = h.T.astype(BF16)

    return pl.pallas_call(
        body, name="resid_prenorm", grid=(S // tm,),
        out_shape=(jax.ShapeDtypeStruct((S, D), F32), jax.ShapeDtypeStruct((S, D), BF16),
                   jax.ShapeDtypeStruct((D, S), BF16)),
        in_specs=_row_specs(tm, [D, D]) + [_vec_spec(D)] * 4,
        out_specs=tuple(_row_specs(tm, [D, D]) + [_col_spec(tm, D)]),
        compiler_params=_params(dimension_semantics=("parallel",)),
    )(x, a_out, gate, g, scale, shift)


def _shift_down(main, halo, k):
    ext = jnp.concatenate([halo, main], axis=0)
    return pltpu.roll(ext, k, 0)[halo.shape[0]:]


def _shift_up(main, halo, k):
    ext = jnp.concatenate([main, halo], axis=0)
    n = ext.shape[0]
    return pltpu.roll(ext, n - k, 0)[:main.shape[0]]


def _conv(up, up_halo, w_ref, b_ref):
    return (w_ref[2:3, :] * up + w_ref[1:2, :] * _shift_down(up, up_halo, 1)
            + w_ref[0:1, :] * _shift_down(up, up_halo, 2) + b_ref[...])


def _prev_halo_map(tm):
    step = tm // HALO
    return lambda j, i: (jnp.maximum(i * step - 1, 0), j)


MLP_TM = 512
MLP_CT = 1408
CARRY = 8


def _mlp_up(h, wg, wv, cwg, cwv, cbg, cbv):
    S, D = h.shape
    F = wg.shape[1]
    tm, ct = _tile(S, MLP_TM), _tile(F, MLP_CT)
    nct = F // ct

    def body(h_ref, wg_ref, wv_ref, cwg_ref, cwv_ref, cbg_ref, cbv_ref,
             upg_ref, upv_ref, act_ref, actt_ref, hg_scr, hv_scr):
        i, j = pl.program_id(0), pl.program_id(1)
        hv = h_ref[...]
        us = []
        for w_ref, cw_ref, cb_ref, up_ref, scr in ((wg_ref, cwg_ref, cbg_ref, upg_ref, hg_scr),
                                                   (wv_ref, cwv_ref, cbv_ref, upv_ref, hv_scr)):
            up = _dot(hv, w_ref[...]).astype(BF16)
            up_ref[...] = up
            upf = up.astype(F32)
            halo = jnp.where(i == 0, 0.0, scr[j])
            us.append(_conv(upf, halo, cw_ref, cb_ref))
            scr[j] = upf[tm - CARRY:, :]
        act = _silu(us[0]) * us[1]
        act_ref[...] = act.astype(BF16)
        actt_ref[...] = act.T.astype(BF16)

    blk = pl.BlockSpec((tm, ct), lambda i, j: (i, j))
    wspec = pl.BlockSpec((D, ct), lambda i, j: (0, j))
    cwspec = pl.BlockSpec((CONV_W, ct), lambda i, j: (0, j))
    cbspec = pl.BlockSpec((1, ct), lambda i, j: (0, j))
    sds = jax.ShapeDtypeStruct((S, F), BF16)
    return pl.pallas_call(
        body, name="mlp_up", grid=(S // tm, nct),
        out_shape=(sds, sds, sds, jax.ShapeDtypeStruct((F, S), BF16)),
        in_specs=[pl.BlockSpec((tm, D), lambda i, j: (i, 0)), wspec, wspec, cwspec, cwspec, cbspec, cbspec],
        out_specs=(blk, blk, blk, pl.BlockSpec((ct, tm), lambda i, j: (j, i))),
        scratch_shapes=[pltpu.VMEM((nct, CARRY, ct), F32), pltpu.VMEM((nct, CARRY, ct), F32)],
        compiler_params=_params(dimension_semantics=("arbitrary", "arbitrary")),
    )(h, wg, wv, cwg, cwv, cbg, cbv)


def _conv_act_bwd(dact, up_g, up_v, cwg, cwv, cbg, cbv):
    S, F = up_g.shape
    tm, ct = _tile(S, 256), _tile(F, CONV_COLS)
    nct = F // ct

    def body(da_ref, ug_ref, uv_ref, hg_ref, hv_ref, wg_ref, wv_ref, bg_ref, bv_ref,
             dug_ref, duv_ref, pg_ref, pv_ref):
        first = pl.program_id(1) == 0

        @pl.when(first)
        def _():
            pg_ref[...] = jnp.zeros_like(pg_ref)
            pv_ref[...] = jnp.zeros_like(pv_ref)

        da = da_ref[...].astype(F32)
        taps = []
        for u_ref, h_ref in ((ug_ref, hg_ref), (uv_ref, hv_ref)):
            h = jnp.where(first, 0.0, h_ref[...].astype(F32))
            uu = u_ref[...].astype(F32)
            taps.append((_shift_down(uu, h, 2), _shift_down(uu, h, 1), uu))
        u_g = wg_ref[0:1, :] * taps[0][0] + wg_ref[1:2, :] * taps[0][1] + wg_ref[2:3, :] * taps[0][2] + bg_ref[...]
        u_v = wv_ref[0:1, :] * taps[1][0] + wv_ref[1:2, :] * taps[1][1] + wv_ref[2:3, :] * taps[1][2] + bv_ref[...]
        sg = 1.0 / (1.0 + jnp.exp(-u_g))
        du_g = da * u_v * (sg * (1.0 + u_g * (1.0 - sg)))
        du_v = da * (u_g * sg)
        dug_ref[...] = du_g.astype(BF16)
        duv_ref[...] = du_v.astype(BF16)
        for du, tp, p_ref in ((du_g, taps[0], pg_ref), (du_v, taps[1], pv_ref)):
            for k in range(CONV_W):
                p_ref[k:k + 1, :] += jnp.sum(du * tp[k], axis=0, keepdims=True)
            p_ref[CONV_W:CONV_W + 1, :] += jnp.sum(du, axis=0, keepdims=True)

    main = pl.BlockSpec((tm, ct), lambda j, i: (i, j))
    halo = pl.BlockSpec((HALO, ct), _prev_halo_map(tm))
    wspec = pl.BlockSpec((CONV_W, ct), lambda j, i: (0, j))
    bspec = pl.BlockSpec((1, ct), lambda j, i: (0, j))
    pspec = pl.BlockSpec((8, ct), lambda j, i: (0, j))
    return pl.pallas_call(
        body, name="conv_act_bwd", grid=(nct, S // tm),
        out_shape=(jax.ShapeDtypeStruct((S, F), BF16), jax.ShapeDtypeStruct((S, F), BF16),
                   jax.ShapeDtypeStruct((8, F), F32), jax.ShapeDtypeStruct((8, F), F32)),
        in_specs=[main, main, main, halo, halo, wspec, wspec, bspec, bspec],
        out_specs=(main, main, pspec, pspec),
        compiler_params=_params(dimension_semantics=("parallel", "arbitrary")),
    )(dact, up_g, up_v, up_g, up_v, cwg, cwv, cbg, cbv)


def _conv_bwd_input(du, cw, name):
    S, C = du.shape
    tm, ct = _tile(S, 256), _tile(C, CONV_COLS)
    step = tm // HALO
    last_halo = S // HALO - 1

    def body(du_ref, h_ref, w_ref, o_ref):
        last = pl.program_id(1) == pl.num_programs(1) - 1
        d = du_ref[...].astype(F32)
        h = jnp.where(last, 0.0, h_ref[...].astype(F32))
        o_ref[...] = (w_ref[2:3, :] * d + w_ref[1:2, :] * _shift_up(d, h, 1)
                      + w_ref[0:1, :] * _shift_up(d, h, 2)).astype(BF16)

    return pl.pallas_call(
        body, name=name, out_shape=jax.ShapeDtypeStruct((S, C), BF16), grid=(C // ct, S // tm),
        in_specs=[pl.BlockSpec((tm, ct), lambda j, i: (i, j)),
                  pl.BlockSpec((HALO, ct), lambda j, i: (jnp.minimum((i + 1) * step, last_halo), j)),
                  pl.BlockSpec((CONV_W, ct), lambda j, i: (0, j))],
        out_specs=pl.BlockSpec((tm, ct), lambda j, i: (i, j)),
        compiler_params=_params(dimension_semantics=("parallel", "parallel")),
    )(du, du, cw)


def _scan_mats(R, nc, reverse):
    i = lax.broadcasted_iota(jnp.int32, (LANES, LANES), 0)
    j = lax.broadcasted_iota(jnp.int32, (LANES, LANES), 1)
    inner = ((i >= j) if reverse else (i <= j)).astype(F32)
    r = lax.broadcasted_iota(jnp.int32, (R, R), 0)
    c = lax.broadcasted_iota(jnp.int32, (R, R), 1)
    same = (r // nc) == (c // nc)
    outer = (same & ((c > r) if reverse else (c < r))).astype(F32)
    return inner, outer


def _chunk_scan(v, inner, outer, reverse):
    w = _dot(v, inner, precision=HIGHEST)
    col = 0 if reverse else LANES - 1
    carry = _dot(outer, w, precision=HIGHEST)[:, col:col + 1]
    return w + carry


def _fgate_fwd(z_rows, nc):
    R = z_rows.shape[0]

    def body(z_ref, f_ref):
        z = z_ref[...]
        logf = jnp.minimum(z, 0.0) - jnp.log(1.0 + jnp.exp(-jnp.abs(z)))
        inner, outer = _scan_mats(R, nc, False)
        f_ref[...] = _chunk_scan(logf, inner, outer, False)

    return pl.pallas_call(body, name="fgate_fwd", out_shape=jax.ShapeDtypeStruct((R, LANES), F32),
                          compiler_params=_params())(z_rows)


def _fgate_bwd(dfk_neg_rows, dfq_rows, z_rows, nc):
    R = z_rows.shape[0]
    nh = R // nc

    def body(dfk_ref, dfq_ref, z_ref, dz_ref, db_ref):
        inner, outer = _scan_mats(R, nc, True)
        dlogf = _chunk_scan(dfq_ref[...] - dfk_ref[...], inner, outer, True)
        dz = dlogf * (1.0 / (1.0 + jnp.exp(z_ref[...])))
        dz_ref[...] = dz
        hr = lax.broadcasted_iota(jnp.int32, (nh, R), 0)
        hc = lax.broadcasted_iota(jnp.int32, (nh, R), 1) // nc
        per_head = _dot((hr == hc).astype(F32), dz, precision=HIGHEST)
        db_ref[...] = jnp.sum(per_head, axis=1, keepdims=True)

    return pl.pallas_call(
        body, name="fgate_bwd",
        out_shape=(jax.ShapeDtypeStruct((R, LANES), F32), jax.ShapeDtypeStruct((nh, 1), F32)),
        compiler_params=_params())(dfk_neg_rows, dfq_rows, z_rows)


_NEG = -1e30
SKIP_BELOW = -106.0
_SCALE = HEAD_DIM ** -0.5
N_SCAN = ATT_BK // SCAN_BK
F_PARTS = 3
Q_F_LANE = HEAD_DIM
Q_ONE_LANE = HEAD_DIM + F_PARTS


def _kv_slice(j):
    return pl.ds(pl.multiple_of(j * ATT_BK, ATT_BK), ATT_BK)


def _mask_t(strict):
    s = lax.broadcasted_iota(jnp.int32, (ATT_BK, ATT_BQ), 0)
    t = lax.broadcasted_iota(jnp.int32, (ATT_BK, ATT_BQ), 1)
    return (s < t) if strict else (s <= t)


def _walk_down(i, step, alive, carry):
    carry = step(i, carry, True)

    def cond(st):
        n, go, _ = st
        return jnp.logical_and(n < i, go)

    def body(st):
        n, _, cr = st
        j = i - 1 - n
        cr = step(j, cr, False)
        return n + 1, alive(jnp.maximum(j - 1, 0), cr), cr

    return lax.while_loop(cond, body, (jnp.int32(0), alive(jnp.maximum(i - 1, 0), carry), carry))[2]


def _t_block(rows):
    return pl.BlockSpec((None, rows, ATT_BQ), lambda h, i, *_: (h, 0, i))


def _t_full(rows, S):
    return pl.BlockSpec((None, rows, S), lambda h, i, *_: (h, 0, 0))


def _n_block():
    return pl.BlockSpec((None, ATT_BQ, LANES), lambda h, i, *_: (h, i, 0))


def _n_full(S):
    return pl.BlockSpec((None, S, LANES), lambda h, i, *_: (h, 0, 0))


def _heads(t):
    S = t.shape[0]
    return jnp.transpose(t.reshape(S, -1, HEAD_DIM), (1, 0, 2))


def _unheads(t):
    return jnp.transpose(t, (1, 0, 2)).reshape(t.shape[1], -1)


def _tr(t):
    return jnp.transpose(t, (0, 2, 1))


def _skip_bounds(k_cols, f_rows):
    H, S = f_rows.shape
    f_end = f_rows.reshape(H, S // ATT_BK, ATT_BK)[:, :, -1]
    k_sq = jnp.sum(jnp.square(_heads(k_cols).astype(F32)), axis=-1).reshape(H, S // ATT_BK, ATT_BK)
    k_max = lax.cummax(jnp.sqrt(jnp.max(k_sq, axis=-1)), axis=1)
    return f_end, k_max


def _bf16_parts(f):
    hi = f.astype(BF16).astype(F32)
    mid = (f - hi).astype(BF16).astype(F32)
    return hi, mid, (f - hi - mid).astype(BF16).astype(F32)


def _att_prep(qkv, f_pairs):
    S = qkv.shape[0]
    n_pairs = qkv.shape[1] // (6 * LANES)
    H = 2 * n_pairs
    tm = _tile(S, 512)

    def body(qf_ref, kf_ref, vf_ref, qs_ref, ks_ref, vs_ref, f_ref,
             fqn, fqt, fkn, fkt, fvn, fvt, sqn, sqt, skn, skt, svn, svt):
        lane = lax.broadcasted_iota(jnp.int32, (1, LANES), 1)
        f = f_ref[...]

        def head(ref, e):
            t = ref[...].astype(F32)
            if e == 1:
                t = pltpu.roll(t, HEAD_DIM, 1)
            return jnp.where(lane < HEAD_DIM, t, 0.0)

        def at(first):
            return jnp.logical_and(lane >= first, lane < first + F_PARTS)

        for e in range(2):
            parts = _bf16_parts(f[:, e:e + 1])
            f_lanes = sum(jnp.where(lane == Q_F_LANE + k, parts[k], 0.0) for k in range(F_PARTS))
            nf_lanes = sum(jnp.where(lane == Q_ONE_LANE + k, parts[k], 0.0) for k in range(F_PARTS))
            vals = (
                (fqn, fqt, LANES, head(qf_ref, e) * _SCALE + f_lanes + jnp.where(at(Q_ONE_LANE), 1.0, 0.0)),
                (fkn, fkt, LANES, head(kf_ref, e) + jnp.where(at(Q_F_LANE), 1.0, 0.0) - nf_lanes),
                (fvn, fvt, HEAD_DIM, head(vf_ref, e)),
                (sqn, sqt, LANES, head(qs_ref, e) * _SCALE),
                (skn, skt, LANES, head(ks_ref, e)),
                (svn, svt, HEAD_DIM, head(vs_ref, e)),
            )
            for n_ref, t_ref, rows, val in vals:
                n_ref[e] = val.astype(BF16)
                t_ref[e] = val.T[:rows].astype(BF16)

    col = lambda base: pl.BlockSpec((tm, LANES), lambda i, p: (i, base + p))
    n_spec = pl.BlockSpec((2, tm, LANES), lambda i, p: (p, i, 0))
    t_spec = lambda rows: pl.BlockSpec((2, rows, tm), lambda i, p: (p, 0, i))
    n_sds = jax.ShapeDtypeStruct((H, S, LANES), BF16)
    t_sds = lambda rows: jax.ShapeDtypeStruct((H, rows, S), BF16)
    group = ([n_sds, t_sds(LANES), n_sds, t_sds(LANES), n_sds, t_sds(HEAD_DIM)],
             [n_spec, t_spec(LANES), n_spec, t_spec(LANES), n_spec, t_spec(HEAD_DIM)])
    res = pl.pallas_call(
        body, name="att_prep", grid=(S // tm, n_pairs),
        out_shape=tuple(group[0] * 2),
        in_specs=[col(k * n_pairs) for k in range(6)] + [pl.BlockSpec((None, tm, 2), lambda i, p: (p, i, 0))],
        out_specs=tuple(group[1] * 2),
        compiler_params=_params(dimension_semantics=("parallel", "parallel")),
    )(qkv, qkv, qkv, qkv, qkv, qkv, f_pairs)
    names = ("q_n", "q_t", "k_n", "k_t", "v_n", "v_t")
    return dict(zip(names, res[:6])), dict(zip(names, res[6:]))


def _fox_reach(qt, fend_ref, kmax_ref, h):
    qf = qt.astype(F32)
    q_norm = jnp.sqrt(jnp.sum(jnp.square(qf[:HEAD_DIM]), axis=0, keepdims=True))
    f_t = jnp.sum(qf[Q_F_LANE:Q_F_LANE + F_PARTS], axis=0, keepdims=True)
    return lambda j: q_norm * kmax_ref[h, j] + f_t - fend_ref[h, j]


def _fox_fwd(q_t, k_n, v_t, f_end, k_max, shards):
    H, _, S = q_t.shape
    n, nq = len(shards), S // ATT_BQ

    def body(fend_ref, kmax_ref, qt_ref, k_ref, vt_ref, *rest):
        ins, (ot_ref, lse_ref), outs, sems = rest[:n], rest[n:n + 2], rest[n + 2:2 * n + 2], rest[2 * n + 2:]
        h, i = pl.program_id(0), pl.program_id(1)

        @pl.when(jnp.logical_and(h == 0, i == 0))
        def _():
            for cp in _exchange_copies(ins, outs, *sems, False):
                cp.start()

        qt = qt_ref[...]
        reach = _fox_reach(qt, fend_ref, kmax_ref, h)

        def step(j, carry, masked):
            m, l, acc = carry
            ks = _kv_slice(j)
            s = _dot(k_ref[ks, :], qt)
            if masked:
                s = jnp.where(_mask_t(False), s, _NEG)
            mn = jnp.maximum(m, jnp.max(s, axis=0, keepdims=True))
            alpha = jnp.exp(m - mn)
            p = jnp.exp(s - mn)
            l = alpha * l + jnp.sum(p, axis=0, keepdims=True)
            acc = acc * alpha + _dot(vt_ref[:, ks], p.astype(BF16))
            return mn, l, acc

        def alive(j, carry):
            return jnp.max(reach(j) - carry[0]) > SKIP_BELOW

        row = jnp.zeros((1, ATT_BQ), F32)
        m, l, acc = _walk_down(i, step, alive, (row + _NEG, row, jnp.zeros((HEAD_DIM, ATT_BQ), F32)))
        ot_ref[...] = acc / l
        lse_ref[...] = m + jnp.log(l)

        @pl.when(jnp.logical_and(h == H - 1, i == nq - 1))
        def _():
            for cp in _exchange_copies(ins, outs, *sems, False):
                cp.wait()

    any_spec = pl.BlockSpec(memory_space=pl.ANY)
    grid_spec = pltpu.PrefetchScalarGridSpec(
        num_scalar_prefetch=2, grid=(H, nq),
        in_specs=[_t_block(LANES), _n_full(S), _t_full(HEAD_DIM, S)] + [any_spec] * n,
        out_specs=tuple([_t_block(HEAD_DIM), _t_block(1)] + [any_spec] * n),
        scratch_shapes=_exchange_sems(n))
    res = pl.pallas_call(
        body, name="fox_fwd", grid_spec=grid_spec,
        out_shape=tuple([jax.ShapeDtypeStruct((H, HEAD_DIM, S), F32), jax.ShapeDtypeStruct((H, 1, S), F32)]
                        + _exchange_out_shapes(shards, False)),
        compiler_params=_params(dimension_semantics=("arbitrary", "arbitrary"), has_side_effects=True),
    )(f_end, k_max, q_t, k_n, v_t, *shards)
    return res[0], res[1], res[2:]


def _fox_bwd(q_t, q_n, k_n, k_t, v_n, do_t, do_n, o_t, lse, f_end, k_max):
    H, _, S = q_t.shape

    def body(fend_ref, kmax_ref, qt_ref, qn_ref, k_ref, kt_ref, v_ref, dot_ref, don_ref, ot_ref, lse_ref,
             dqt_ref, dk_ref, dv_ref):
        h, i = pl.program_id(0), pl.program_id(1)

        @pl.when(i == 0)
        def _():
            dk_ref[...] = jnp.zeros_like(dk_ref)
            dv_ref[...] = jnp.zeros_like(dv_ref)

        qt, qn, dot, don = qt_ref[...], qn_ref[...], dot_ref[...], don_ref[...]
        lse = lse_ref[...]
        delta = jnp.sum(dot[:HEAD_DIM].astype(F32) * ot_ref[...], axis=0, keepdims=True)
        reach = _fox_reach(qt, fend_ref, kmax_ref, h)

        def alive(j, dq):
            return jnp.max(reach(j) - lse) > SKIP_BELOW

        def step(j, dq, masked):
            ks = _kv_slice(j)
            s = _dot(k_ref[ks, :], qt)
            if masked:
                s = jnp.where(_mask_t(False), s, _NEG)
            p = jnp.exp(s - lse)
            ds = (p * (_dot(v_ref[ks, :], dot) - delta)).astype(BF16)
            dk_ref[ks, :] += _dot(ds, qn)
            dv_ref[ks, :] += _dot(p.astype(BF16), don)
            return dq + _dot(kt_ref[:, ks], ds)

        dqt_ref[...] = _walk_down(i, step, alive, jnp.zeros((LANES, ATT_BQ), F32))

    grid_spec = pltpu.PrefetchScalarGridSpec(
        num_scalar_prefetch=2, grid=(H, S // ATT_BQ),
        in_specs=[_t_block(LANES), _n_block(), _n_full(S), _t_full(LANES, S), _n_full(S),
                  _t_block(LANES), _n_block(), _t_block(HEAD_DIM), _t_block(1)],
        out_specs=(_t_block(LANES), _n_full(S), _n_full(S)))
    return pl.pallas_call(
        body, name="fox_bwd", grid_spec=grid_spec,
        out_shape=(jax.ShapeDtypeStruct((H, LANES, S), F32), jax.ShapeDtypeStruct((H, S, LANES), F32),
                   jax.ShapeDtypeStruct((H, S, LANES), F32)),
        compiler_params=_params(dimension_semantics=("parallel", "arbitrary")),
    )(f_end, k_max, q_t, q_n, k_n, k_t, v_n, do_t, do_n, o_t, lse)


def _scan_lhs():
    r = lax.broadcasted_iota(jnp.int32, (SCAN_BK, 2 * SCAN_BK), 0)
    c = lax.broadcasted_iota(jnp.int32, (SCAN_BK, 2 * SCAN_BK), 1) % SCAN_BK
    return (c >= r).astype(BF16)


def _suffix_sum(t, lhs):
    hi = t.astype(BF16)
    lo = (t - hi.astype(F32)).astype(BF16)
    return _dot(lhs, jnp.concatenate([hi, lo], axis=0))


def _sb_scores(k, qt, mask):
    z = _dot(k, qt)
    e = jnp.exp(-jnp.abs(z))
    lb = -(jnp.maximum(z, 0.0) + jnp.log(1.0 + e))
    if mask is not None:
        lb = jnp.where(mask, lb, 0.0)
    return z, e, lb


def _scan_blocks():
    return [slice(u * SCAN_BK, (u + 1) * SCAN_BK) for u in reversed(range(N_SCAN))]


def _sb_fwd(q_t, k_n, v_t):
    H, _, S = q_t.shape

    def body(qt_ref, k_ref, vt_ref, ot_ref):
        i = pl.program_id(1)
        qt = qt_ref[...]
        lhs = _scan_lhs()

        def step(j, carry, masked):
            c, acc = carry
            ks = _kv_slice(j)
            mask = _mask_t(True) if masked else None
            z, _, lb = _sb_scores(k_ref[ks, :], qt, mask)
            parts = []
            for sl in _scan_blocks():
                rin = _suffix_sum(lb[sl], lhs)
                a = jnp.exp(z[sl] + rin + c)
                if masked:
                    a = jnp.where(mask[sl], a, 0.0)
                parts.append(a.astype(BF16))
                c = c + rin[0:1, :]
            a_all = jnp.concatenate(parts[::-1], axis=0)
            return c, acc + _dot(vt_ref[:, ks], a_all)

        carry = (jnp.zeros((1, ATT_BQ), F32), jnp.zeros((HEAD_DIM, ATT_BQ), F32))
        ot_ref[...] = _walk_down(i, step, lambda j, cr: jnp.max(cr[0]) > SKIP_BELOW, carry)[1]

    return pl.pallas_call(
        body, name="sb_fwd", grid=(H, S // ATT_BQ),
        out_shape=jax.ShapeDtypeStruct((H, HEAD_DIM, S), F32),
        in_specs=[_t_block(LANES), _n_full(S), _t_full(HEAD_DIM, S)],
        out_specs=_t_block(HEAD_DIM),
        compiler_params=_params(dimension_semantics=("parallel", "parallel")),
    )(q_t, k_n, v_t)


def _sb_bwd(q_t, q_n, k_n, k_t, v_n, do_t, do_n, o_t, bound):
    H, _, S = q_t.shape
    n, nq = len(bound), S // ATT_BQ

    def body(qt_ref, qn_ref, k_ref, kt_ref, v_ref, dot_ref, don_ref, ot_ref, *rest):
        ins, (dqt_ref, dk_ref, dv_ref) = rest[:n], rest[n:n + 3]
        outs, sems = rest[n + 3:2 * n + 3], rest[2 * n + 3:]
        h, i = pl.program_id(0), pl.program_id(1)

        @pl.when(jnp.logical_and(h == 0, i == 0))
        def _():
            for cp in _exchange_copies(ins, outs, *sems, True):
                cp.start()

        @pl.when(i == 0)
        def _():
            dk_ref[...] = jnp.zeros_like(dk_ref)
            dv_ref[...] = jnp.zeros_like(dv_ref)

        qt, qn, dot, don = qt_ref[...], qn_ref[...], dot_ref[...], don_ref[...]
        lhs = _scan_lhs()
        delta = jnp.sum(dot[:HEAD_DIM].astype(F32) * ot_ref[...], axis=0, keepdims=True)

        def step(j, carry, masked):
            c, g, dq = carry
            ks = _kv_slice(j)
            mask = _mask_t(True) if masked else None
            z, e, lb = _sb_scores(k_ref[ks, :], qt, mask)
            da = _dot(v_ref[ks, :], dot)
            a_parts, dz_parts = [], []
            for sl in _scan_blocks():
                rin = _suffix_sum(lb[sl], lhs)
                a = jnp.exp(z[sl] + rin + c)
                if masked:
                    a = jnp.where(mask[sl], a, 0.0)
                ab = a.astype(BF16)
                gg = ab.astype(F32) * da[sl]
                rgin = _suffix_sum(gg, lhs)
                rinv = 1.0 / (1.0 + e[sl])
                sig = jnp.where(z[sl] >= 0.0, rinv, e[sl] * rinv)
                dz = gg - sig * (delta - g - (rgin - gg))
                if masked:
                    dz = jnp.where(mask[sl], dz, 0.0)
                a_parts.append(ab)
                dz_parts.append(dz.astype(BF16))
                c = c + rin[0:1, :]
                g = g + rgin[0:1, :]
            ab_all = jnp.concatenate(a_parts[::-1], axis=0)
            dzb = jnp.concatenate(dz_parts[::-1], axis=0)
            dk_ref[ks, :] += _dot(dzb, qn)
            dv_ref[ks, :] += _dot(ab_all, don)
            return c, g, dq + _dot(kt_ref[:, ks], dzb)

        row = jnp.zeros((1, ATT_BQ), F32)
        carry = (row, row, jnp.zeros((LANES, ATT_BQ), F32))
        dqt_ref[...] = _walk_down(i, step, lambda j, cr: jnp.max(cr[0]) > SKIP_BELOW, carry)[2]

        @pl.when(jnp.logical_and(h == H - 1, i == nq - 1))
        def _():
            for cp in _exchange_copies(ins, outs, *sems, True):
                cp.wait()

    any_spec = pl.BlockSpec(memory_space=pl.ANY)
    res = pl.pallas_call(
        body, name="sb_bwd", grid=(H, nq),
        out_shape=tuple([jax.ShapeDtypeStruct((H, LANES, S), F32), jax.ShapeDtypeStruct((H, S, LANES), F32),
                         jax.ShapeDtypeStruct((H, S, LANES), F32)] + _exchange_out_shapes(bound, True)),
        in_specs=[_t_block(LANES), _n_block(), _n_full(S), _t_full(LANES, S), _n_full(S),
                  _t_block(LANES), _n_block(), _t_block(HEAD_DIM)] + [any_spec] * n,
        out_specs=tuple([_t_block(LANES), _n_full(S), _n_full(S)] + [any_spec] * n),
        scratch_shapes=_exchange_sems(n),
        compiler_params=_params(dimension_semantics=("arbitrary", "arbitrary"), has_side_effects=True),
    )(q_t, q_n, k_n, k_t, v_n, do_t, do_n, o_t, *bound)
    return res[0], res[1], res[2], res[3:]


def _dqkv_assemble(dqf_t, dkf, dvf, dqs_t, dks, dvs):
    H, _, S = dqf_t.shape
    n_pairs = H // 2
    tm = _tile(S, 512)

    def body(dqf_ref, dkf_ref, dvf_ref, dqs_ref, dks_ref, dvs_ref, out_ref, dfk_ref):
        lane = lax.broadcasted_iota(jnp.int32, (1, LANES), 1)
        slabs = ((dqf_ref, True), (dkf_ref, False), (dvf_ref, False),
                 (dqs_ref, True), (dks_ref, False), (dvs_ref, False))
        for k, (ref, transposed) in enumerate(slabs):
            if transposed:
                t0, t1 = ref[0].T * _SCALE, ref[1].T * _SCALE
            else:
                t0, t1 = ref[0], ref[1]
            out_ref[k] = jnp.where(lane < HEAD_DIM, t0, pltpu.roll(t1, HEAD_DIM, 1)).astype(BF16)
        for e in range(2):
            dfk_ref[e] = dkf_ref[e].T[Q_ONE_LANE:Q_ONE_LANE + 1, :]

    t_spec = pl.BlockSpec((2, LANES, tm), lambda i, p: (p, 0, i))
    n_spec = pl.BlockSpec((2, tm, LANES), lambda i, p: (p, i, 0))
    return pl.pallas_call(
        body, name="dqkv_assemble", grid=(S // tm, n_pairs),
        out_shape=(jax.ShapeDtypeStruct((6, S, n_pairs * LANES), BF16), jax.ShapeDtypeStruct((H, 1, S), F32)),
        in_specs=[t_spec, n_spec, n_spec, t_spec, n_spec, n_spec],
        out_specs=(pl.BlockSpec((6, tm, LANES), lambda i, p: (0, i, p)),
                   pl.BlockSpec((2, 1, tm), lambda i, p: (p, 0, i))),
        compiler_params=_params(dimension_semantics=("parallel", "parallel")),
    )(dqf_t, dkf, dvf, dqs_t, dks, dvs)


def _acc_spec(w):
    return pl.BlockSpec((1, w), lambda i: (0, 0))


def _loss_head(x1, act, w_down, gate_m, g_final, target):
    S, D = x1.shape
    F = act.shape[1]
    tm = _tile(S, 256)

    def body(x1_ref, act_ref, w_ref, gt_ref, gf_ref, tg_ref, dx2_ref, gm_ref, loss_ref, dgf_ref, dgt_ref):
        @pl.when(pl.program_id(0) == 0)
        def _():
            loss_ref[...] = jnp.zeros_like(loss_ref)
            dgf_ref[...] = jnp.zeros_like(dgf_ref)
            dgt_ref[...] = jnp.zeros_like(dgt_ref)

        mo = _dot(act_ref[...], w_ref[...])
        x2 = x1_ref[...] + gt_ref[...] * mo
        r = lax.rsqrt(jnp.mean(x2 * x2, axis=-1, keepdims=True) + EPS)
        xh = x2 * r
        diff = xh * gf_ref[...] - tg_ref[...]
        loss_ref[...] += (0.5 / D) * jnp.sum(diff * diff)
        dy = diff * (1.0 / D)
        dgf_ref[...] += jnp.sum(dy * xh, axis=0, keepdims=True)
        dxh = dy * gf_ref[...]
        dx2 = r * (dxh - xh * jnp.mean(dxh * xh, axis=-1, keepdims=True))
        dx2_ref[...] = dx2
        gm_ref[...] = (dx2 * gt_ref[...]).astype(BF16)
        dgt_ref[...] += jnp.sum(dx2 * mo, axis=0, keepdims=True)

    return pl.pallas_call(
        body, name="loss_head", grid=(S // tm,),
        out_shape=(jax.ShapeDtypeStruct((S, D), F32), jax.ShapeDtypeStruct((S, D), BF16),
                   jax.ShapeDtypeStruct((1, LANES), F32), jax.ShapeDtypeStruct((1, D), F32),
                   jax.ShapeDtypeStruct((1, D), F32)),
        in_specs=_row_specs(tm, [D, F]) + [pl.BlockSpec((F, D), lambda i: (0, 0))] + [_vec_spec(D)] * 2
        + _row_specs(tm, [D]),
        out_specs=tuple(_row_specs(tm, [D, D]) + [_acc_spec(LANES), _acc_spec(D), _acc_spec(D)]),
        compiler_params=_params(dimension_semantics=("arbitrary",)),
    )(x1, act, w_down, gate_m, g_final, target)


def _norm_bwd(lhs, rhs, xin, dres, g, scale, name, gate=None, branch=None, bound=()):
    S, D = xin.shape
    tm = _tile(S, 256)
    gated = gate is not None
    nl, nb, n_steps = len(lhs), len(bound), S // tm
    n_out = 6 if gated else 4

    def body(*refs):
        l_refs, r_refs, rest = refs[:nl], refs[nl:2 * nl], refs[2 * nl:]
        if gated:
            x_ref, dr_ref, g_ref, sc_ref, gt_ref, br_ref = rest[:6]
            rest = rest[6:]
        else:
            x_ref, dr_ref, g_ref, sc_ref = rest[:4]
            rest = rest[4:]
        ins, outs, ex_outs, sems = rest[:nb], rest[nb:nb + n_out], rest[nb + n_out:2 * nb + n_out], rest[2 * nb + n_out:]
        dx_ref, dsc_ref, dsh_ref, dg_ref = outs[:4]
        sums = (dsc_ref, dsh_ref, dg_ref) + ((outs[5],) if gated else ())
        i = pl.program_id(0)

        @pl.when(i == 0)
        def _():
            for s_ref in sums:
                s_ref[...] = jnp.zeros_like(s_ref)
            if nb:
                for cp in _chip_exchange_copies(ins, ex_outs, *sems):
                    cp.start()

        dhv = None
        for l_ref, r_ref in zip(l_refs, r_refs):
            if len(l_ref.shape) == 3:
                K = l_ref.shape[2]
                terms = [_dot_nt(l_ref[k], r_ref[:, k * K:(k + 1) * K]) for k in range(l_ref.shape[0])]
            else:
                terms = [_dot_nt(l_ref[...], r_ref[...])]
            for t in terms:
                dhv = t if dhv is None else dhv + t
        xv = x_ref[...]
        r = lax.rsqrt(jnp.mean(xv * xv, axis=-1, keepdims=True) + EPS)
        xh = xv * r
        dsc_ref[...] += jnp.sum(dhv * (xh * g_ref[...]), axis=0, keepdims=True)
        dsh_ref[...] += jnp.sum(dhv, axis=0, keepdims=True)
        dn = dhv * (1.0 + sc_ref[...])
        dg_ref[...] += jnp.sum(dn * xh, axis=0, keepdims=True)
        dxh = dn * g_ref[...]
        dx = dr_ref[...] + r * (dxh - xh * jnp.mean(dxh * xh, axis=-1, keepdims=True))
        dx_ref[...] = dx
        if gated:
            outs[4][...] = (dx * gt_ref[...]).astype(BF16)
            outs[5][...] += jnp.sum(dx * br_ref[...], axis=0, keepdims=True)

        if nb:
            @pl.when(i == n_steps - 1)
            def _():
                for cp in _chip_exchange_copies(ins, ex_outs, *sems):
                    cp.wait()

    def l_spec(a):
        if a.ndim == 3:
            return pl.BlockSpec((a.shape[0], tm, a.shape[2]), lambda i: (0, i, 0))
        return pl.BlockSpec((tm, a.shape[1]), lambda i: (i, 0))

    any_spec = pl.BlockSpec(memory_space=pl.ANY)
    vec = jax.ShapeDtypeStruct((1, D), F32)
    out_shape = [jax.ShapeDtypeStruct((S, D), F32), vec, vec, vec]
    out_specs = _row_specs(tm, [D]) + [_acc_spec(D)] * 3
    in_specs = [l_spec(a) for a in lhs] + [pl.BlockSpec(b.shape, lambda i: (0, 0)) for b in rhs]
    in_specs += _row_specs(tm, [D, D]) + [_vec_spec(D)] * 2
    args = list(lhs) + list(rhs) + [xin, dres, g, scale]
    if gated:
        out_shape += [jax.ShapeDtypeStruct((S, D), BF16), vec]
        out_specs += _row_specs(tm, [D]) + [_acc_spec(D)]
        in_specs += [_vec_spec(D)] + _row_specs(tm, [D])
        args += [gate, branch]
    res = pl.pallas_call(
        body, name=name, grid=(n_steps,),
        out_shape=tuple(out_shape + [jax.ShapeDtypeStruct(b.shape, b.dtype) for b in bound]),
        in_specs=in_specs + [any_spec] * nb, out_specs=tuple(out_specs + [any_spec] * nb),
        scratch_shapes=_chip_exchange_sems(nb) if nb else [],
        compiler_params=_params(dimension_semantics=("arbitrary",), has_side_effects=bool(nb)),
    )(*args, *bound)
    return tuple(res[:n_out]) + (tuple(res[n_out:]),)


def _headnorm_bwd(dmix, o_f, o_s, g_f, g_s):
    S, dh = o_f.shape
    H = dh // HEAD_DIM
    tm = _tile(S, 256)

    def body(dm_ref, of_ref, os_ref, gf_ref, gs_ref, fn_ref, ft_ref, sn_ref, st_ref, dgf_ref, dgs_ref):
        @pl.when(pl.program_id(0) == 0)
        def _():
            dgf_ref[...] = jnp.zeros_like(dgf_ref)
            dgs_ref[...] = jnp.zeros_like(dgs_ref)

        ones = _group_ones()
        lane = lax.broadcasted_iota(jnp.int32, (1, LANES), 1)
        parts = ((of_ref, gf_ref, fn_ref, ft_ref, dgf_ref), (os_ref, gs_ref, sn_ref, st_ref, dgs_ref))
        for part, (o_ref, g_ref, n_ref, t_ref, dg_ref) in enumerate(parts):
            for t in range(dh // LANES):
                cols = slice(t * LANES, (t + 1) * LANES)
                o = o_ref[:, cols]
                dm = dm_ref[:, part * dh + t * LANES: part * dh + (t + 1) * LANES]
                r = lax.rsqrt(_dot(o * o, ones, precision=HIGHEST) * (1.0 / HEAD_DIM) + EPS)
                oh = o * r
                dg_ref[:, cols] += jnp.sum(dm * oh, axis=0, keepdims=True)
                dn = dm * g_ref[:, cols]
                mean = _dot(dn * oh, ones, precision=HIGHEST) * (1.0 / HEAD_DIM)
                do = r * (dn - oh * mean)
                for e in range(2):
                    d = do if e == 0 else pltpu.roll(do, HEAD_DIM, 1)
                    d = jnp.where(lane < HEAD_DIM, d, 0.0)
                    n_ref[2 * t + e] = d.astype(BF16)
                    t_ref[2 * t + e] = d.T.astype(BF16)

    vec = jax.ShapeDtypeStruct((1, dh), F32)
    n_sds = jax.ShapeDtypeStruct((H, S, LANES), BF16)
    t_sds = jax.ShapeDtypeStruct((H, LANES, S), BF16)
    n_spec = pl.BlockSpec((H, tm, LANES), lambda i: (0, i, 0))
    t_spec = pl.BlockSpec((H, LANES, tm), lambda i: (0, 0, i))
    return pl.pallas_call(
        body, name="headnorm_bwd", grid=(S // tm,),
        out_shape=(n_sds, t_sds, n_sds, t_sds, vec, vec),
        in_specs=_row_specs(tm, [2 * dh, dh, dh]) + [_vec_spec(dh)] * 2,
        out_specs=(n_spec, t_spec, n_spec, t_spec, _acc_spec(dh), _acc_spec(dh)),
        compiler_params=_params(dimension_semantics=("arbitrary",)),
    )(dmix, o_f, o_s, g_f, g_s)


def _adamw(w, gslots, m, v, name):
    R, C = w.shape
    n = gslots.shape[0]
    tr = 256 if (R % 256 == 0 and R > 256) else R
    bc1 = 1.0 - ADAM_B1 ** ADAM_STEP
    bc2 = 1.0 - ADAM_B2 ** ADAM_STEP

    def body(w_ref, gs_ref, m_ref, v_ref, g_ref, d_ref, nm_ref, nv_ref):
        g = gs_ref[0]
        for s in range(1, n):
            g = g + gs_ref[s]
        nm = ADAM_B1 * m_ref[...] + (1.0 - ADAM_B1) * g
        nv = ADAM_B2 * v_ref[...] + (1.0 - ADAM_B2) * (g * g)
        g_ref[...] = g
        nm_ref[...] = nm
        nv_ref[...] = nv
        d_ref[...] = -ADAM_LR * ((nm / bc1) / (jnp.sqrt(nv / bc2) + ADAM_EPS) + ADAM_WD * w_ref[...])

    blk = pl.BlockSpec((tr, C), lambda i: (i, 0))
    sds = jax.ShapeDtypeStruct((R, C), F32)
    return pl.pallas_call(
        body, name=name, grid=(R // tr,), out_shape=(sds,) * 4,
        in_specs=[blk, pl.BlockSpec((n, tr, C), lambda i: (0, i, 0)), blk, blk], out_specs=(blk,) * 4,
        compiler_params=_params(dimension_semantics=("parallel",)),
    )(w, gslots, m, v)


def _slot_sum(slots, name):
    n, _, C = slots.shape

    def body(s_ref, o_ref):
        acc = s_ref[0]
        for s in range(1, n):
            acc = acc + s_ref[s]
        o_ref[...] = acc

    return pl.pallas_call(body, name=name, out_shape=jax.ShapeDtypeStruct((1, C), F32),
                          compiler_params=_params())(slots)


def _pad_cols(a, n):
    return jnp.pad(a, ((0, 0), (0, n - a.shape[1])))


def _ungather(g, axis):
    if axis == 0:
        return g.reshape(g.shape[0] * g.shape[1], g.shape[2])
    return jnp.transpose(g, (1, 0, 2)).reshape(g.shape[1], g.shape[0] * g.shape[2])


def _to_slots(full, axis):
    R, C = full.shape
    if axis == 0:
        return full.reshape(N_DEV, R // N_DEV, C)
    return jnp.transpose(full.reshape(R, N_DEV, C // N_DEV), (1, 0, 2))


def kernel(x, c, w_ada, b_ada, g_attn, w_in, b_fgate, g_out_fox, g_out_sb, w_out, g_mlp, w_up, conv_w, conv_b, w_down, g_final, loss_target, m_w_ada, m_b_ada, m_g_attn, m_w_in, m_b_fgate, m_g_out_fox, m_g_out_sb, m_w_out, m_g_mlp, m_w_up, m_conv_w, m_conv_b, m_w_down, m_g_final, v_w_ada, v_b_ada, v_g_attn, v_w_in, v_b_fgate, v_g_out_fox, v_g_out_sb, v_w_out, v_g_mlp, v_w_up, v_conv_w, v_conv_b, v_w_down, v_g_final):
    S, D = x.shape[1], x.shape[2]
    dh = D // 2
    n_heads = dh // HEAD_DIM
    n_qkv = 6 * dh
    ff = w_down.shape[1] * N_DEV
    ffp = -(-ff // (2 * LANES)) * (2 * LANES)
    nc = S // LANES
    me = 4 * lax.axis_index("x") + 2 * lax.axis_index("y") + lax.axis_index("c")
    xs, tgt = x[0], loss_target[0]

    c_all, win_g = _gather_two_level([c, w_in[0].astype(BF16)], name="gather_first")
    c_all = c_all.reshape(N_DEV, D)
    W_in = _ungather(win_g, 1)
    W_qkv, W_f = W_in[:, :n_qkv], _pad_cols(W_in[:, n_qkv:], LANES)
    cb_g, cb_v = _pad_cols(conv_b[:, :ff], ffp), _pad_cols(conv_b[:, ff:], ffp)

    n_ada = w_ada.shape[2]
    b_shard = lax.dynamic_slice(b_ada, (0, me * n_ada), (1, n_ada))
    mod_cols = _ada_fwd(c_all, w_ada[0], b_shard)
    (mod_g,) = _exchange([mod_cols], scatter=False, name="gather_mod")
    mod = lax.dynamic_index_in_dim(mod_g, me, axis=1, keepdims=False).reshape(6, 1, D)
    shift_a, scale_a, gate_a, shift_m, scale_m, gate_m = [mod[k] for k in range(6)]

    h1, h1_t = _prenorm(xs, g_attn, scale_a, shift_a, "prenorm_attn")
    qkv = _mm(h1, W_qkv, BF16, "proj_qkv")
    flog = _mm(h1, W_f, F32, "proj_fgate")
    zf = flog[:, :n_heads] + b_fgate
    z_rows = zf.T.reshape(n_heads * nc, LANES)
    f_rows = _fgate_fwd(z_rows, nc).reshape(n_heads, S)
    f_pairs = jnp.transpose(f_rows.reshape(n_heads // 2, 2, S), (0, 2, 1))
    fox, sb = _att_prep(qkv, f_pairs)
    f_end, k_max = _skip_bounds(qkv[:, dh:2 * dh], f_rows)
    of_t, lse, (wout_g, wup_g, wdown_g, convw_g) = _fox_fwd(
        fox["q_t"], fox["k_n"], fox["v_t"], f_end, k_max,
        [w_out[0].astype(BF16), w_up[0].astype(BF16), w_down[0].astype(BF16), conv_w[0]])
    W_out = _ungather(wout_g, 0)
    W_up = _ungather(wup_g, 1)
    W_g, W_v = _pad_cols(W_up[:, :ff], ffp), _pad_cols(W_up[:, ff:], ffp)
    W_down = jnp.pad(_ungather(wdown_g, 0), ((0, ffp - ff), (0, 0)))
    cw_full = _ungather(convw_g, 1)
    cw_g, cw_v = _pad_cols(cw_full[:, :ff], ffp), _pad_cols(cw_full[:, ff:], ffp)
    os_t = _sb_fwd(sb["q_t"], sb["k_n"], sb["v_t"])
    o_f, o_s = _unheads(_tr(of_t)), _unheads(_tr(os_t))
    mix, mix_t = _headnorm_fwd(o_f, o_s, g_out_fox, g_out_sb)
    a_out = _mm(mix, W_out, F32, "proj_out")
    x1, h2, h2_t = _resid_prenorm(xs, a_out, gate_a, g_mlp, scale_m, shift_m)
    up_g, up_v, act, act_t = _mlp_up(h2, W_g, W_v, cw_g, cw_v, cb_g, cb_v)

    dx2, gm, loss_p, dg_final, dgate_m = _loss_head(x1, act, W_down, gate_m, g_final.reshape(1, D), tgt)
    dact = _mm(gm, W_down, BF16, "bwd_down_act", tn=1408, nt=True)
    dW_down = _mm_acc(act_t, gm, "bwd_down_w")
    du_g, du_v, p_g, p_v = _conv_act_bwd(dact, up_g, up_v, cw_g, cw_v, cb_g, cb_v)
    dup_g = _conv_bwd_input(du_g, cw_g, "conv_bwd_input_g")
    dup_v = _conv_bwd_input(du_v, cw_v, "conv_bwd_input_v")
    dW_g = _mm_acc(h2_t, dup_g, "bwd_up_w_g")
    dW_v = _mm_acc(h2_t, dup_v, "bwd_up_w_v")
    dx1, dscale_m, dshift_m, dg_mlp, ga, dgate_a, _ = _norm_bwd(
        [dup_g, dup_v], [W_g, W_v], x1, dx2, g_mlp, scale_m, "norm_mlp_bwd", gate=gate_a, branch=a_out)
    dmix = _mm(ga, W_out, F32, "bwd_out_act", nt=True)
    dW_out = _mm_acc(mix_t, ga, "bwd_out_w")
    dof_n, dof_t, dos_n, dos_t, dg_fox, dg_sb = _headnorm_bwd(dmix, o_f, o_s, g_out_fox, g_out_sb)
    dqf_t, dkf, dvf = _fox_bwd(fox["q_t"], fox["q_n"], fox["k_n"], fox["k_t"], fox["v_n"], dof_t, dof_n, of_t, lse,
                              f_end, k_max)
    dW_upf = jnp.concatenate([dW_g[:, :ff], dW_v[:, :ff]], axis=1)
    dcw = jnp.concatenate([p_g[:CONV_W, :ff], p_v[:CONV_W, :ff]], axis=1)
    dqs_t, dks, dvs, (s_out, s_up, s_down, s_cw) = _sb_bwd(
        sb["q_t"], sb["q_n"], sb["k_n"], sb["k_t"], sb["v_n"], dos_t, dos_n, os_t,
        [_to_slots(dW_out, 0), _to_slots(dW_upf, 1), _to_slots(dW_down[:ff], 0), _to_slots(dcw, 1)])
    dparts, dfk = _dqkv_assemble(dqf_t, dkf, dvf, dqs_t, dks, dvs)
    dz_rows, db_fgate = _fgate_bwd(dfk.reshape(n_heads * nc, LANES),
                                   dqf_t[:, Q_F_LANE, :].reshape(n_heads * nc, LANES), z_rows, nc)
    dzf = _pad_cols(dz_rows.reshape(n_heads, S).T, LANES).astype(BF16)
    dW_qkv = _mm_acc_parts(h1_t, dparts, "bwd_in_w")
    dW_f = _mm_acc(h1_t, dzf, "bwd_in_w_fgate")
    dW_in = jnp.concatenate([jnp.transpose(dW_qkv, (1, 0, 2)).reshape(D, n_qkv), dW_f[:, :n_heads]], axis=1)
    bound_in = _to_slots(dW_in, 1)
    bound_in = bound_in.reshape((N_CHIPS, 2) + bound_in.shape[1:])
    (got_in,) = _scatter_to_sibling([bound_in], "scatter_sibling")
    c_idx = lax.axis_index("c").astype(jnp.int32).reshape(1)
    grad_x, dscale_a, dshift_a, dg_attn, (s_in,) = _norm_bwd(
        [dparts, dzf], [W_qkv, W_f], xs, dx1, g_attn, scale_a, "norm_attn_bwd",
        bound=[_pair_add(bound_in, got_in, c_idx, "pair_add")])

    dconv_b = jnp.concatenate([p_g[CONV_W:CONV_W + 1, :ff], p_v[CONV_W:CONV_W + 1, :ff]], axis=1)
    parts = [dshift_a, dscale_a, dgate_a, dshift_m, dscale_m, dgate_m,
             dg_attn, db_fgate.reshape(1, n_heads), dg_fox, dg_sb, dg_mlp, dconv_b, dg_final,
             loss_p[:, :1]]
    sizes = [p.shape[1] for p in parts]
    vec = jnp.concatenate(parts, axis=1)
    n_vec = -(-vec.shape[1] // LANES) * LANES
    vec = _pad_cols(vec, n_vec)
    (vec_g,) = _exchange([vec], scatter=False, name="gather_small")
    offs = [0]
    for s in sizes:
        offs.append(offs[-1] + s)

    def small(k0, k1=None):
        k1 = k0 if k1 is None else k1
        return vec_g[:, :, offs[k0]:offs[k1 + 1]]

    dmod_all = small(0, 5).reshape(N_DEV, 6 * D)
    dmod_cols = lax.dynamic_slice(dmod_all, (0, me * n_ada), (N_DEV, n_ada))
    dW_ada = _ada_bwd(c_all.T, dmod_cols)


    res = {}
    res["w_ada"] = _adamw(w_ada[0], dW_ada[None], m_w_ada[0], v_w_ada[0], "adamw_w_ada")
    res["w_in"] = _adamw(w_in[0], s_in, m_w_in[0], v_w_in[0], "adamw_w_in")
    res["w_out"] = _adamw(w_out[0], s_out, m_w_out[0], v_w_out[0], "adamw_w_out")
    res["w_up"] = _adamw(w_up[0], s_up, m_w_up[0], v_w_up[0], "adamw_w_up")
    res["w_down"] = _adamw(w_down[0], s_down, m_w_down[0], v_w_down[0], "adamw_w_down")
    res["conv_w"] = _adamw(conv_w[0], s_cw, m_conv_w[0], v_conv_w[0], "adamw_conv_w")
    small_names = ["b_ada", "g_attn", "b_fgate", "g_out_fox", "g_out_sb", "g_mlp", "conv_b", "g_final"]
    small_w = [b_ada, g_attn, b_fgate, g_out_fox, g_out_sb, g_mlp, conv_b, g_final.reshape(1, D)]
    small_m = [m_b_ada, m_g_attn, m_b_fgate, m_g_out_fox, m_g_out_sb, m_g_mlp, m_conv_b, m_g_final.reshape(1, D)]
    small_v = [v_b_ada, v_g_attn, v_b_fgate, v_g_out_fox, v_g_out_sb, v_g_mlp, v_conv_b, v_g_final.reshape(1, D)]
    small_res = _adamw(jnp.concatenate(small_w, axis=1), small(0, 12), jnp.concatenate(small_m, axis=1),
                       jnp.concatenate(small_v, axis=1), "adamw_small")
    lo = 0
    for nm, wv in zip(small_names, small_w):
        res[nm] = tuple(r[:, lo:lo + wv.shape[1]] for r in small_res)
        lo += wv.shape[1]
    loss = _slot_sum(_pad_cols(small(13).reshape(N_DEV, 1), LANES).reshape(N_DEV, 1, LANES), "loss_sum")[0, 0]

    names = ["w_ada", "b_ada", "g_attn", "w_in", "b_fgate", "g_out_fox", "g_out_sb", "w_out", "g_mlp",
             "w_up", "conv_w", "conv_b", "w_down", "g_final"]

    def shaped(n, a):
        if n == "g_final":
            return a.reshape(D)
        if n in ("b_ada", "g_attn", "b_fgate", "g_out_fox", "g_out_sb", "g_mlp", "conv_b"):
            return a
        return a[None]

    outs = [loss, grad_x[None]]
    for k in range(4):
        outs += [shaped(n, res[n][k]) for n in names]
    return tuple(outs)
```

```python
import jax
import jax.numpy as jnp
from jax import lax
from jax.experimental import pallas as pl
from jax.experimental.pallas import tpu as pltpu

F32 = jnp.float32
BF16 = jnp.bfloat16
HIGHEST = lax.Precision.HIGHEST

N_DEV = 8
LANES = 128
HEAD_DIM = 64
EPS = 1e-6
CONV_W = 3
CONV_COLS = 1408
HALO = 16
ATT_BQ = 512
ATT_BK = 512
SCAN_BK = 128
VMEM_LIMIT = 56 * 1024 * 1024

ADAM_LR = 0.001
ADAM_B1 = 0.9
ADAM_B2 = 0.999
ADAM_EPS = 1e-08
ADAM_WD = 0.01
ADAM_STEP = 10


def _params(**kw):
    return pltpu.CompilerParams(vmem_limit_bytes=VMEM_LIMIT, **kw)


def _tile(n, cap):
    if n <= cap:
        return n
    best = None
    for t in range(LANES, cap + 1, LANES):
        if n % t == 0:
            best = t
    assert best is not None, (n, cap)
    return best


def _dot(a, b, **kw):
    return jnp.dot(a, b, preferred_element_type=F32, **kw)


def _exchange_copies(ins, outs, send_sems, recv_sems, loc_sems, scatter):
    n = len(ins)
    if n == 0:
        return []
    x, y, c = lax.axis_index("x"), lax.axis_index("y"), lax.axis_index("c")
    me = 4 * x + 2 * y + c
    copies = []
    for a in range(n):
        src = ins[a].at[me] if scatter else ins[a]
        copies.append(pltpu.make_async_copy(src, outs[a].at[me], loc_sems.at[a]))
    for k in range(1, N_DEV):
        px = 1 - x if k & 4 else x
        py = 1 - y if k & 2 else y
        pc = 1 - c if k & 1 else c
        peer = 4 * px + 2 * py + pc
        for a in range(n):
            src = ins[a].at[peer] if scatter else ins[a]
            copies.append(pltpu.make_async_remote_copy(
                src_ref=src, dst_ref=outs[a].at[me],
                send_sem=send_sems.at[a, k - 1], recv_sem=recv_sems.at[a, k - 1],
                device_id=(px, py, pc), device_id_type=pl.DeviceIdType.MESH))
    return copies


def _exchange_out_shapes(arrays, scatter):
    return [jax.ShapeDtypeStruct((N_DEV,) + tuple(a.shape[1:] if scatter else a.shape), a.dtype) for a in arrays]


def _exchange_sems(n):
    return [pltpu.SemaphoreType.DMA((n, N_DEV - 1)), pltpu.SemaphoreType.DMA((n, N_DEV - 1)),
            pltpu.SemaphoreType.DMA((n,))]


def _exchange(arrays, scatter, name):
    n = len(arrays)

    def body(*refs):
        copies = _exchange_copies(refs[:n], refs[n:2 * n], *refs[2 * n:], scatter)
        for cp in copies:
            cp.start()
        for cp in copies:
            cp.wait()

    any_spec = pl.BlockSpec(memory_space=pl.ANY)
    return pl.pallas_call(
        body, name=name, out_shape=tuple(_exchange_out_shapes(arrays, scatter)),
        in_specs=[any_spec] * n, out_specs=tuple([any_spec] * n),
        scratch_shapes=_exchange_sems(n),
        compiler_params=pltpu.CompilerParams(has_side_effects=True),
    )(*arrays)


def _gather_two_level(arrays, name):
    n = len(arrays)
    out_shape = [jax.ShapeDtypeStruct((N_DEV,) + tuple(a.shape), a.dtype) for a in arrays]

    def body(*refs):
        ins, outs = refs[:n], refs[n:2 * n]
        send_sems, recv_sems, loc_sems = refs[2 * n:]
        x, y, c = lax.axis_index("x"), lax.axis_index("y"), lax.axis_index("c")
        me, sibling = (x, y, c), (x, y, 1 - c)
        chips = [(1 - x, y), (x, 1 - y), (1 - x, 1 - y)]

        def slot(px, py, pc):
            return 4 * px + 2 * py + pc

        def copy(a, k, block, to, src=None):
            dst = outs[a].at[slot(*block)]
            return pltpu.make_async_remote_copy(
                src_ref=dst if src is None else src, dst_ref=dst,
                send_sem=send_sems.at[a, k], recv_sem=recv_sems.at[a, k],
                device_id=to, device_id_type=pl.DeviceIdType.MESH)

        local = [pltpu.make_async_copy(ins[a], outs[a].at[slot(*me)], loc_sems.at[a]) for a in range(n)]
        for cp in local:
            cp.start()
        first = []
        for a in range(n):
            first.append(copy(a, 0, me, sibling, src=ins[a]))
            first += [copy(a, 1 + j, me, (*chip, c), src=ins[a]) for j, chip in enumerate(chips)]
        for cp in first:
            cp.start()
        passed = []
        for j, chip in enumerate(chips):
            for a in range(n):
                copy(a, 1 + j, (*chip, c), me).wait_recv()
                cp = copy(a, 4 + j, (*chip, c), sibling)
                cp.start()
                passed.append(cp)
        for a in range(n):
            copy(a, 0, sibling, me).wait_recv()
            for j, chip in enumerate(chips):
                copy(a, 4 + j, (*chip, 1 - c), me).wait_recv()
        for cp in first + passed:
            cp.wait_send()
        for cp in local:
            cp.wait()

    any_spec = pl.BlockSpec(memory_space=pl.ANY)
    return pl.pallas_call(
        body, name=name, out_shape=tuple(out_shape),
        in_specs=[any_spec] * n, out_specs=tuple([any_spec] * n),
        scratch_shapes=[pltpu.SemaphoreType.DMA((n, N_DEV - 1)), pltpu.SemaphoreType.DMA((n, N_DEV - 1)),
                        pltpu.SemaphoreType.DMA((n,))],
        compiler_params=pltpu.CompilerParams(has_side_effects=True),
    )(*arrays)


N_CHIPS = 4


def _scatter_to_sibling(arrays, name):
    n = len(arrays)
    out_shape = [jax.ShapeDtypeStruct((N_CHIPS,) + tuple(a.shape[2:]), a.dtype) for a in arrays]

    def body(*refs):
        ins, outs = refs[:n], refs[n:2 * n]
        send_sems, recv_sems = refs[2 * n:]
        x, y, c = lax.axis_index("x"), lax.axis_index("y"), lax.axis_index("c")
        copies = []
        for a in range(n):
            for q in range(N_CHIPS):
                cp = pltpu.make_async_remote_copy(
                    src_ref=ins[a].at[q, 1 - c], dst_ref=outs[a].at[q],
                    send_sem=send_sems.at[a, q], recv_sem=recv_sems.at[a, q],
                    device_id=(x, y, 1 - c), device_id_type=pl.DeviceIdType.MESH)
                cp.start()
                copies.append(cp)
        for cp in copies:
            cp.wait()

    any_spec = pl.BlockSpec(memory_space=pl.ANY)
    return pl.pallas_call(
        body, name=name, out_shape=tuple(out_shape),
        in_specs=[any_spec] * n, out_specs=tuple([any_spec] * n),
        scratch_shapes=[pltpu.SemaphoreType.DMA((n, N_CHIPS)), pltpu.SemaphoreType.DMA((n, N_CHIPS))],
        compiler_params=pltpu.CompilerParams(has_side_effects=True),
    )(*arrays)


def _pair_add(mine, got, c_idx, name):
    _, _, R, C = mine.shape
    tr = 256 if (R % 256 == 0 and R > 256) else R

    def body(c_ref, m_ref, g_ref, o_ref):
        o_ref[...] = m_ref[...] + g_ref[...]

    grid_spec = pltpu.PrefetchScalarGridSpec(
        num_scalar_prefetch=1, grid=(N_CHIPS, R // tr),
        in_specs=[pl.BlockSpec((None, None, tr, C), lambda q, i, c_ref: (q, c_ref[0], i, 0)),
                  pl.BlockSpec((None, tr, C), lambda q, i, c_ref: (q, i, 0))],
        out_specs=pl.BlockSpec((None, tr, C), lambda q, i, c_ref: (q, i, 0)))
    return pl.pallas_call(
        body, name=name, grid_spec=grid_spec, out_shape=jax.ShapeDtypeStruct((N_CHIPS, R, C), mine.dtype),
        compiler_params=_params(dimension_semantics=("parallel", "parallel")),
    )(c_idx, mine, got)


def _chip_exchange_copies(ins, outs, send_sems, recv_sems, loc_sems):
    n = len(ins)
    x, y, c = lax.axis_index("x"), lax.axis_index("y"), lax.axis_index("c")
    myq = 2 * x + y
    copies = [pltpu.make_async_copy(ins[a].at[myq], outs[a].at[myq], loc_sems.at[a]) for a in range(n)]
    for k in range(1, N_CHIPS):
        qx = 1 - x if k & 2 else x
        qy = 1 - y if k & 1 else y
        for a in range(n):
            copies.append(pltpu.make_async_remote_copy(
                src_ref=ins[a].at[2 * qx + qy], dst_ref=outs[a].at[myq],
                send_sem=send_sems.at[a, k - 1], recv_sem=recv_sems.at[a, k - 1],
                device_id=(qx, qy, c), device_id_type=pl.DeviceIdType.MESH))
    return copies


def _chip_exchange_sems(n):
    return [pltpu.SemaphoreType.DMA((n, N_CHIPS - 1)), pltpu.SemaphoreType.DMA((n, N_CHIPS - 1)),
            pltpu.SemaphoreType.DMA((n,))]


def _dot_nt(a, b):
    return lax.dot_general(a, b, (((1,), (1,)), ((), ())), preferred_element_type=F32)


def _rhs_spec(b, tn, nt):
    if nt:
        return pl.BlockSpec((tn, b.shape[1]), lambda i, j: (j, 0))
    return pl.BlockSpec((b.shape[0], tn), lambda i, j: (0, j))


def _mm(a, b, out_dtype, name, tm=1024, tn=512, nt=False):
    M, K = a.shape
    N = b.shape[0] if nt else b.shape[1]
    tm, tn = _tile(M, tm), _tile(N, tn)
    dot = _dot_nt if nt else _dot

    def body(a_ref, b_ref, o_ref):
        o_ref[...] = dot(a_ref[...], b_ref[...]).astype(out_dtype)

    return pl.pallas_call(
        body, name=name, out_shape=jax.ShapeDtypeStruct((M, N), out_dtype),
        grid=(M // tm, N // tn),
        in_specs=[pl.BlockSpec((tm, K), lambda i, j: (i, 0)), _rhs_spec(b, tn, nt)],
        out_specs=pl.BlockSpec((tm, tn), lambda i, j: (i, j)),
        compiler_params=_params(dimension_semantics=("parallel", "parallel")),
    )(a, b)


def _mm_acc(a, b, name, tm=1408, tn=1408, tk=512):
    M, S = a.shape
    _, N = b.shape
    tm, tn, tk = _tile(M, tm), _tile(N, tn), _tile(S, tk)

    def body(a_ref, b_ref, o_ref):
        @pl.when(pl.program_id(2) == 0)
        def _():
            o_ref[...] = jnp.zeros_like(o_ref)

        o_ref[...] += _dot(a_ref[...], b_ref[...])

    return pl.pallas_call(
        body, name=name, out_shape=jax.ShapeDtypeStruct((M, N), F32),
        grid=(M // tm, N // tn, S // tk),
        in_specs=[pl.BlockSpec((tm, tk), lambda i, j, k: (i, k)), pl.BlockSpec((tk, tn), lambda i, j, k: (k, j))],
        out_specs=pl.BlockSpec((tm, tn), lambda i, j, k: (i, j)),
        compiler_params=_params(dimension_semantics=("parallel", "parallel", "arbitrary")),
    )(a, b)


def _mm_acc_parts(a, parts, name, tm=1024, tk=1024):
    M, S = a.shape
    P, _, K = parts.shape
    tm, tk = _tile(M, tm), _tile(S, tk)

    def body(a_ref, b_ref, o_ref):
        @pl.when(pl.program_id(2) == 0)
        def _():
            o_ref[...] = jnp.zeros_like(o_ref)

        o_ref[...] += _dot(a_ref[...], b_ref[...])

    return pl.pallas_call(
        body, name=name, out_shape=jax.ShapeDtypeStruct((P, M, K), F32), grid=(P, M // tm, S // tk),
        in_specs=[pl.BlockSpec((tm, tk), lambda k, i, s: (i, s)), pl.BlockSpec((None, tk, K), lambda k, i, s: (k, s, 0))],
        out_specs=pl.BlockSpec((None, tm, K), lambda k, i, s: (k, i, 0)),
        compiler_params=_params(dimension_semantics=("parallel", "parallel", "arbitrary")),
    )(a, parts)


def _silu(z):
    return z * (1.0 / (1.0 + jnp.exp(-z)))


def _ada_fwd(c_all, w_shard, b_shard):
    n = w_shard.shape[1]

    def body(c_ref, w_ref, b_ref, o_ref):
        o_ref[...] = _dot(_silu(c_ref[...]), w_ref[...], precision=HIGHEST) + b_ref[...]

    return pl.pallas_call(body, name="ada_fwd", out_shape=jax.ShapeDtypeStruct((N_DEV, n), F32),
                          compiler_params=_params())(c_all, w_shard, b_shard)


def _ada_bwd(c_all_t, dmod_cols):
    D = c_all_t.shape[0]
    n = dmod_cols.shape[1]

    def body(ct_ref, dm_ref, o_ref):
        sc = _silu(ct_ref[...])
        dm = dm_ref[...]
        acc = sc[:, 0:1] * dm[0:1, :]
        for b in range(1, N_DEV):
            acc = acc + sc[:, b:b + 1] * dm[b:b + 1, :]
        o_ref[...] = acc

    return pl.pallas_call(body, name="ada_bwd", out_shape=jax.ShapeDtypeStruct((D, n), F32),
                          compiler_params=_params())(c_all_t, dmod_cols)


def _row_specs(tm, widths):
    return [pl.BlockSpec((tm, w), lambda i: (i, 0)) for w in widths]


def _vec_spec(w):
    return pl.BlockSpec((1, w), lambda i: (0, 0))


def _col_spec(tm, w):
    return pl.BlockSpec((w, tm), lambda i: (0, i))


def _prenorm(x, g, scale, shift, name):
    S, D = x.shape
    tm = _tile(S, 512)

    def body(x_ref, g_ref, sc_ref, sh_ref, h_ref, ht_ref):
        xv = x_ref[...]
        r = lax.rsqrt(jnp.mean(xv * xv, axis=-1, keepdims=True) + EPS)
        h = (xv * r) * g_ref[...] * (1.0 + sc_ref[...]) + sh_ref[...]
        h_ref[...] = h.astype(BF16)
        ht_ref[...] = h.T.astype(BF16)

    return pl.pallas_call(
        body, name=name, grid=(S // tm,),
        out_shape=(jax.ShapeDtypeStruct((S, D), BF16), jax.ShapeDtypeStruct((D, S), BF16)),
        in_specs=_row_specs(tm, [D]) + [_vec_spec(D)] * 3,
        out_specs=(_row_specs(tm, [D])[0], _col_spec(tm, D)),
        compiler_params=_params(dimension_semantics=("parallel",)),
    )(x, g, scale, shift)


def _group_ones():
    r = lax.broadcasted_iota(jnp.int32, (LANES, LANES), 0) // HEAD_DIM
    c = lax.broadcasted_iota(jnp.int32, (LANES, LANES), 1) // HEAD_DIM
    return (r == c).astype(F32)


def _headnorm_fwd(o_f, o_s, g_f, g_s):
    S, dh = o_f.shape
    tm = _tile(S, 512)

    def body(of_ref, os_ref, gf_ref, gs_ref, mix_ref, mixt_ref):
        ones = _group_ones()
        for part, (o_ref, g_ref) in enumerate(((of_ref, gf_ref), (os_ref, gs_ref))):
            for t in range(dh // LANES):
                cols = slice(t * LANES, (t + 1) * LANES)
                out = slice(part * dh + t * LANES, part * dh + (t + 1) * LANES)
                o = o_ref[:, cols]
                ms = _dot(o * o, ones, precision=HIGHEST) * (1.0 / HEAD_DIM)
                mixn = o * lax.rsqrt(ms + EPS) * g_ref[:, cols]
                mix_ref[:, out] = mixn.astype(BF16)
                mixt_ref[out, :] = mixn.T.astype(BF16)

    return pl.pallas_call(
        body, name="headnorm_fwd", grid=(S // tm,),
        out_shape=(jax.ShapeDtypeStruct((S, 2 * dh), BF16), jax.ShapeDtypeStruct((2 * dh, S), BF16)),
        in_specs=_row_specs(tm, [dh, dh]) + [_vec_spec(dh)] * 2,
        out_specs=(_row_specs(tm, [2 * dh])[0], _col_spec(tm, 2 * dh)),
        compiler_params=_params(dimension_semantics=("parallel",)),
    )(o_f, o_s, g_f, g_s)


def _resid_prenorm(x, a_out, gate, g, scale, shift):
    S, D = x.shape
    tm = _tile(S, 512)

    def body(x_ref, a_ref, gt_ref, g_ref, sc_ref, sh_ref, x1_ref, h_ref, ht_ref):
        x1 = x_ref[...] + gt_ref[...] * a_ref[...]
        x1_ref[...] = x1
        r = lax.rsqrt(jnp.mean(x1 * x1, axis=-1, keepdims=True) + EPS)
        h = (x1 * r) * g_ref[...] * (1.0 + sc_ref[...]) + sh_ref[...]
        h_ref[...] = h.astype(BF16)
        ht_ref[...] = h.T.astype(BF16)

    return pl.pallas_call(
        body, name="resid_prenorm", grid=(S // tm,),
        out_shape=(jax.ShapeDtypeStruct((S, D), F32), jax.ShapeDtypeStruct((S, D), BF16),
                   jax.ShapeDtypeStruct((D, S), BF16)),
        in_specs=_row_specs(tm, [D, D]) + [_vec_spec(D)] * 4,
        out_specs=tuple(_row_specs(tm, [D, D]) + [_col_spec(tm, D)]),
        compiler_params=_params(dimension_semantics=("parallel",)),
    )(x, a_out, gate, g, scale, shift)


def _shift_down(main, halo, k):
    ext = jnp.concatenate([halo, main], axis=0)
    return pltpu.roll(ext, k, 0)[halo.shape[0]:]


def _shift_up(main, halo, k):
    ext = jnp.concatenate([main, halo], axis=0)
    n = ext.shape[0]
    return pltpu.roll(ext, n - k, 0)[:main.shape[0]]


def _conv(up, up_halo, w_ref, b_ref):
    return (w_ref[2:3, :] * up + w_ref[1:2, :] * _shift_down(up, up_halo, 1)
            + w_ref[0:1, :] * _shift_down(up, up_halo, 2) + b_ref[...])


def _prev_halo_map(tm):
    step = tm // HALO
    return lambda j, i: (jnp.maximum(i * step - 1, 0), j)


MLP_TM = 512
MLP_CT = 1408
CARRY = 8


def _mlp_up(h, wg, wv, cwg, cwv, cbg, cbv):
    S, D = h.shape
    F = wg.shape[1]
    tm, ct = _tile(S, MLP_TM), _tile(F, MLP_CT)
    nct = F // ct

    def body(h_ref, wg_ref, wv_ref, cwg_ref, cwv_ref, cbg_ref, cbv_ref,
             upg_ref, upv_ref, act_ref, actt_ref, hg_scr, hv_scr):
        i, j = pl.program_id(0), pl.program_id(1)
        hv = h_ref[...]
        us = []
        for w_ref, cw_ref, cb_ref, up_ref, scr in ((wg_ref, cwg_ref, cbg_ref, upg_ref, hg_scr),
                                                   (wv_ref, cwv_ref, cbv_ref, upv_ref, hv_scr)):
            up = _dot(hv, w_ref[...]).astype(BF16)
            up_ref[...] = up
            upf = up.astype(F32)
            halo = jnp.where(i == 0, 0.0, scr[j])
            us.append(_conv(upf, halo, cw_ref, cb_ref))
            scr[j] = upf[tm - CARRY:, :]
        act = _silu(us[0]) * us[1]
        act_ref[...] = act.astype(BF16)
        actt_ref[...] = act.T.astype(BF16)

    blk = pl.BlockSpec((tm, ct), lambda i, j: (i, j))
    wspec = pl.BlockSpec((D, ct), lambda i, j: (0, j))
    cwspec = pl.BlockSpec((CONV_W, ct), lambda i, j: (0, j))
    cbspec = pl.BlockSpec((1, ct), lambda i, j: (0, j))
    sds = jax.ShapeDtypeStruct((S, F), BF16)
    return pl.pallas_call(
        body, name="mlp_up", grid=(S // tm, nct),
        out_shape=(sds, sds, sds, jax.ShapeDtypeStruct((F, S), BF16)),
        in_specs=[pl.BlockSpec((tm, D), lambda i, j: (i, 0)), wspec, wspec, cwspec, cwspec, cbspec, cbspec],
        out_specs=(blk, blk, blk, pl.BlockSpec((ct, tm), lambda i, j: (j, i))),
        scratch_shapes=[pltpu.VMEM((nct, CARRY, ct), F32), pltpu.VMEM((nct, CARRY, ct), F32)],
        compiler_params=_params(dimension_semantics=("arbitrary", "arbitrary")),
    )(h, wg, wv, cwg, cwv, cbg, cbv)


def _conv_act_bwd(dact, up_g, up_v, cwg, cwv, cbg, cbv):
    S, F = up_g.shape
    tm, ct = _tile(S, 256), _tile(F, CONV_COLS)
    nct = F // ct

    def body(da_ref, ug_ref, uv_ref, hg_ref, hv_ref, wg_ref, wv_ref, bg_ref, bv_ref,
             dug_ref, duv_ref, pg_ref, pv_ref):
        first = pl.program_id(1) == 0

        @pl.when(first)
        def _():
            pg_ref[...] = jnp.zeros_like(pg_ref)
            pv_ref[...] = jnp.zeros_like(pv_ref)

        da = da_ref[...].astype(F32)
        taps = []
        for u_ref, h_ref in ((ug_ref, hg_ref), (uv_ref, hv_ref)):
            h = jnp.where(first, 0.0, h_ref[...].astype(F32))
            uu = u_ref[...].astype(F32)
            taps.append((_shift_down(uu, h, 2), _shift_down(uu, h, 1), uu))
        u_g = wg_ref[0:1, :] * taps[0][0] + wg_ref[1:2, :] * taps[0][1] + wg_ref[2:3, :] * taps[0][2] + bg_ref[...]
        u_v = wv_ref[0:1, :] * taps[1][0] + wv_ref[1:2, :] * taps[1][1] + wv_ref[2:3, :] * taps[1][2] + bv_ref[...]
        sg = 1.0 / (1.0 + jnp.exp(-u_g))
        du_g = da * u_v * (sg * (1.0 + u_g * (1.0 - sg)))
        du_v = da * (u_g * sg)
        dug_ref[...] = du_g.astype(BF16)
        duv_ref[...] = du_v.astype(BF16)
        for du, tp, p_ref in ((du_g, taps[0], pg_ref), (du_v, taps[1], pv_ref)):
            for k in range(CONV_W):
                p_ref[k:k + 1, :] += jnp.sum(du * tp[k], axis=0, keepdims=True)
            p_ref[CONV_W:CONV_W + 1, :] += jnp.sum(du, axis=0, keepdims=True)

    main = pl.BlockSpec((tm, ct), lambda j, i: (i, j))
    halo = pl.BlockSpec((HALO, ct), _prev_halo_map(tm))
    wspec = pl.BlockSpec((CONV_W, ct), lambda j, i: (0, j))
    bspec = pl.BlockSpec((1, ct), lambda j, i: (0, j))
    pspec = pl.BlockSpec((8, ct), lambda j, i: (0, j))
    return pl.pallas_call(
        body, name="conv_act_bwd", grid=(nct, S // tm),
        out_shape=(jax.ShapeDtypeStruct((S, F), BF16), jax.ShapeDtypeStruct((S, F), BF16),
                   jax.ShapeDtypeStruct((8, F), F32), jax.ShapeDtypeStruct((8, F), F32)),
        in_specs=[main, main, main, halo, halo, wspec, wspec, bspec, bspec],
        out_specs=(main, main, pspec, pspec),
        compiler_params=_params(dimension_semantics=("parallel", "arbitrary")),
    )(dact, up_g, up_v, up_g, up_v, cwg, cwv, cbg, cbv)


def _conv_bwd_taps(d, halo, w_ref):
    return w_ref[2:3, :] * d + w_ref[1:2, :] * _shift_up(d, halo, 1) + w_ref[0:1, :] * _shift_up(d, halo, 2)


def _scan_mats(R, nc, reverse):
    i = lax.broadcasted_iota(jnp.int32, (LANES, LANES), 0)
    j = lax.broadcasted_iota(jnp.int32, (LANES, LANES), 1)
    inner = ((i >= j) if reverse else (i <= j)).astype(F32)
    r = lax.broadcasted_iota(jnp.int32, (R, R), 0)
    c = lax.broadcasted_iota(jnp.int32, (R, R), 1)
    same = (r // nc) == (c // nc)
    outer = (same & ((c > r) if reverse else (c < r))).astype(F32)
    return inner, outer


def _chunk_scan(v, inner, outer, reverse):
    w = _dot(v, inner, precision=HIGHEST)
    col = 0 if reverse else LANES - 1
    carry = _dot(outer, w, precision=HIGHEST)[:, col:col + 1]
    return w + carry


def _fgate_fwd(z_rows, nc):
    R = z_rows.shape[0]

    def body(z_ref, f_ref):
        z = z_ref[...]
        logf = jnp.minimum(z, 0.0) - jnp.log(1.0 + jnp.exp(-jnp.abs(z)))
        inner, outer = _scan_mats(R, nc, False)
        f_ref[...] = _chunk_scan(logf, inner, outer, False)

    return pl.pallas_call(body, name="fgate_fwd", out_shape=jax.ShapeDtypeStruct((R, LANES), F32),
                          compiler_params=_params())(z_rows)


def _fgate_bwd(dfk_neg_rows, dfq_rows, z_rows, nc):
    R = z_rows.shape[0]
    nh = R // nc

    def body(dfk_ref, dfq_ref, z_ref, dz_ref, db_ref):
        inner, outer = _scan_mats(R, nc, True)
        dlogf = _chunk_scan(dfq_ref[...] - dfk_ref[...], inner, outer, True)
        dz = dlogf * (1.0 / (1.0 + jnp.exp(z_ref[...])))
        dz_ref[...] = dz
        hr = lax.broadcasted_iota(jnp.int32, (nh, R), 0)
        hc = lax.broadcasted_iota(jnp.int32, (nh, R), 1) // nc
        per_head = _dot((hr == hc).astype(F32), dz, precision=HIGHEST)
        db_ref[...] = jnp.sum(per_head, axis=1, keepdims=True)

    return pl.pallas_call(
        body, name="fgate_bwd",
        out_shape=(jax.ShapeDtypeStruct((R, LANES), F32), jax.ShapeDtypeStruct((nh, 1), F32)),
        compiler_params=_params())(dfk_neg_rows, dfq_rows, z_rows)


_NEG = -1e30
SKIP_BELOW = -106.0
_SCALE = HEAD_DIM ** -0.5
N_SCAN = ATT_BK // SCAN_BK
F_PARTS = 3
Q_F_LANE = HEAD_DIM
Q_ONE_LANE = HEAD_DIM + F_PARTS


def _kv_slice(j):
    return pl.ds(pl.multiple_of(j * ATT_BK, ATT_BK), ATT_BK)


def _mask_t(strict):
    s = lax.broadcasted_iota(jnp.int32, (ATT_BK, ATT_BQ), 0)
    t = lax.broadcasted_iota(jnp.int32, (ATT_BK, ATT_BQ), 1)
    return (s < t) if strict else (s <= t)


def _walk_down(i, step, alive, carry):
    carry = step(i, carry, True)

    def cond(st):
        n, go, _ = st
        return jnp.logical_and(n < i, go)

    def body(st):
        n, _, cr = st
        j = i - 1 - n
        cr = step(j, cr, False)
        return n + 1, alive(jnp.maximum(j - 1, 0), cr), cr

    return lax.while_loop(cond, body, (jnp.int32(0), alive(jnp.maximum(i - 1, 0), carry), carry))[2]


def _t_block(rows):
    return pl.BlockSpec((None, rows, ATT_BQ), lambda h, i, *_: (h, 0, i))


def _t_full(rows, S):
    return pl.BlockSpec((None, rows, S), lambda h, i, *_: (h, 0, 0))


def _n_block():
    return pl.BlockSpec((None, ATT_BQ, LANES), lambda h, i, *_: (h, i, 0))


def _n_full(S):
    return pl.BlockSpec((None, S, LANES), lambda h, i, *_: (h, 0, 0))


def _heads(t):
    S = t.shape[0]
    return jnp.transpose(t.reshape(S, -1, HEAD_DIM), (1, 0, 2))


def _unheads(t):
    return jnp.transpose(t, (1, 0, 2)).reshape(t.shape[1], -1)


def _tr(t):
    return jnp.transpose(t, (0, 2, 1))


def _skip_bounds(k_cols, f_rows):
    H, S = f_rows.shape
    f_end = f_rows.reshape(H, S // ATT_BK, ATT_BK)[:, :, -1]
    k_sq = jnp.sum(jnp.square(_heads(k_cols).astype(F32)), axis=-1).reshape(H, S // ATT_BK, ATT_BK)
    k_max = lax.cummax(jnp.sqrt(jnp.max(k_sq, axis=-1)), axis=1)
    return f_end, k_max


def _bf16_parts(f):
    hi = f.astype(BF16).astype(F32)
    mid = (f - hi).astype(BF16).astype(F32)
    return hi, mid, (f - hi - mid).astype(BF16).astype(F32)


def _att_prep(qkv, f_pairs):
    S = qkv.shape[0]
    n_pairs = qkv.shape[1] // (6 * LANES)
    H = 2 * n_pairs
    tm = _tile(S, 512)

    def body(qf_ref, kf_ref, vf_ref, qs_ref, ks_ref, vs_ref, f_ref,
             fqn, fqt, fkn, fkt, fvn, fvt, sqn, sqt, skn, skt, svn, svt):
        lane = lax.broadcasted_iota(jnp.int32, (1, LANES), 1)
        f = f_ref[...]

        def head(ref, e):
            t = ref[...].astype(F32)
            if e == 1:
                t = pltpu.roll(t, HEAD_DIM, 1)
            return jnp.where(lane < HEAD_DIM, t, 0.0)

        def at(first):
            return jnp.logical_and(lane >= first, lane < first + F_PARTS)

        for e in range(2):
            parts = _bf16_parts(f[:, e:e + 1])
            f_lanes = sum(jnp.where(lane == Q_F_LANE + k, parts[k], 0.0) for k in range(F_PARTS))
            nf_lanes = sum(jnp.where(lane == Q_ONE_LANE + k, parts[k], 0.0) for k in range(F_PARTS))
            vals = (
                (fqn, fqt, LANES, head(qf_ref, e) * _SCALE + f_lanes + jnp.where(at(Q_ONE_LANE), 1.0, 0.0)),
                (fkn, fkt, LANES, head(kf_ref, e) + jnp.where(at(Q_F_LANE), 1.0, 0.0) - nf_lanes),
                (fvn, fvt, HEAD_DIM, head(vf_ref, e)),
                (sqn, sqt, LANES, head(qs_ref, e) * _SCALE),
                (skn, skt, LANES, head(ks_ref, e)),
                (svn, svt, HEAD_DIM, head(vs_ref, e)),
            )
            for n_ref, t_ref, rows, val in vals:
                n_ref[e] = val.astype(BF16)
                t_ref[e] = val.T[:rows].astype(BF16)

    col = lambda base: pl.BlockSpec((tm, LANES), lambda i, p: (i, base + p))
    n_spec = pl.BlockSpec((2, tm, LANES), lambda i, p: (p, i, 0))
    t_spec = lambda rows: pl.BlockSpec((2, rows, tm), lambda i, p: (p, 0, i))
    n_sds = jax.ShapeDtypeStruct((H, S, LANES), BF16)
    t_sds = lambda rows: jax.ShapeDtypeStruct((H, rows, S), BF16)
    group = ([n_sds, t_sds(LANES), n_sds, t_sds(LANES), n_sds, t_sds(HEAD_DIM)],
             [n_spec, t_spec(LANES), n_spec, t_spec(LANES), n_spec, t_spec(HEAD_DIM)])
    res = pl.pallas_call(
        body, name="att_prep", grid=(S // tm, n_pairs),
        out_shape=tuple(group[0] * 2),
        in_specs=[col(k * n_pairs) for k in range(6)] + [pl.BlockSpec((None, tm, 2), lambda i, p: (p, i, 0))],
        out_specs=tuple(group[1] * 2),
        compiler_params=_params(dimension_semantics=("parallel", "parallel")),
    )(qkv, qkv, qkv, qkv, qkv, qkv, f_pairs)
    names = ("q_n", "q_t", "k_n", "k_t", "v_n", "v_t")
    return dict(zip(names, res[:6])), dict(zip(names, res[6:]))


def _fox_reach(qt, fend_ref, kmax_ref, h):
    qf = qt.astype(F32)
    q_norm = jnp.sqrt(jnp.sum(jnp.square(qf[:HEAD_DIM]), axis=0, keepdims=True))
    f_t = jnp.sum(qf[Q_F_LANE:Q_F_LANE + F_PARTS], axis=0, keepdims=True)
    return lambda j: q_norm * kmax_ref[h, j] + f_t - fend_ref[h, j]


def _fox_fwd(q_t, k_n, v_t, f_end, k_max, shards):
    H, _, S = q_t.shape
    n, nq = len(shards), S // ATT_BQ

    def body(fend_ref, kmax_ref, qt_ref, k_ref, vt_ref, *rest):
        ins, (ot_ref, lse_ref), outs, sems = rest[:n], rest[n:n + 2], rest[n + 2:2 * n + 2], rest[2 * n + 2:]
        h, i = pl.program_id(0), pl.program_id(1)

        @pl.when(jnp.logical_and(h == 0, i == 0))
        def _():
            for cp in _exchange_copies(ins, outs, *sems, False):
                cp.start()

        qt = qt_ref[...]
        reach = _fox_reach(qt, fend_ref, kmax_ref, h)

        def step(j, carry, masked):
            m, l, acc = carry
            ks = _kv_slice(j)
            s = _dot(k_ref[ks, :], qt)
            if masked:
                s = jnp.where(_mask_t(False), s, _NEG)
            mn = jnp.maximum(m, jnp.max(s, axis=0, keepdims=True))
            alpha = jnp.exp(m - mn)
            p = jnp.exp(s - mn)
            l = alpha * l + jnp.sum(p, axis=0, keepdims=True)
            acc = acc * alpha + _dot(vt_ref[:, ks], p.astype(BF16))
            return mn, l, acc

        def alive(j, carry):
            return jnp.max(reach(j) - carry[0]) > SKIP_BELOW

        row = jnp.zeros((1, ATT_BQ), F32)
        m, l, acc = _walk_down(i, step, alive, (row + _NEG, row, jnp.zeros((HEAD_DIM, ATT_BQ), F32)))
        ot_ref[...] = acc / l
        lse_ref[...] = m + jnp.log(l)

        @pl.when(jnp.logical_and(h == H - 1, i == nq - 1))
        def _():
            for cp in _exchange_copies(ins, outs, *sems, False):
                cp.wait()

    any_spec = pl.BlockSpec(memory_space=pl.ANY)
    grid_spec = pltpu.PrefetchScalarGridSpec(
        num_scalar_prefetch=2, grid=(H, nq),
        in_specs=[_t_block(LANES), _n_full(S), _t_full(HEAD_DIM, S)] + [any_spec] * n,
        out_specs=tuple([_t_block(HEAD_DIM), _t_block(1)] + [any_spec] * n),
        scratch_shapes=_exchange_sems(n))
    res = pl.pallas_call(
        body, name="fox_fwd", grid_spec=grid_spec,
        out_shape=tuple([jax.ShapeDtypeStruct((H, HEAD_DIM, S), F32), jax.ShapeDtypeStruct((H, 1, S), F32)]
                        + _exchange_out_shapes(shards, False)),
        compiler_params=_params(dimension_semantics=("arbitrary", "arbitrary"), has_side_effects=True),
    )(f_end, k_max, q_t, k_n, v_t, *shards)
    return res[0], res[1], res[2:]


def _fox_bwd(q_t, q_n, k_n, k_t, v_n, do_t, do_n, o_t, lse, f_end, k_max):
    H, _, S = q_t.shape

    def body(fend_ref, kmax_ref, qt_ref, qn_ref, k_ref, kt_ref, v_ref, dot_ref, don_ref, ot_ref, lse_ref,
             dqt_ref, dk_ref, dv_ref):
        h, i = pl.program_id(0), pl.program_id(1)

        @pl.when(i == 0)
        def _():
            dk_ref[...] = jnp.zeros_like(dk_ref)
            dv_ref[...] = jnp.zeros_like(dv_ref)

        qt, qn, dot, don = qt_ref[...], qn_ref[...], dot_ref[...], don_ref[...]
        lse = lse_ref[...]
        delta = jnp.sum(dot[:HEAD_DIM].astype(F32) * ot_ref[...], axis=0, keepdims=True)
        reach = _fox_reach(qt, fend_ref, kmax_ref, h)

        def alive(j, dq):
            return jnp.max(reach(j) - lse) > SKIP_BELOW

        def step(j, dq, masked):
            ks = _kv_slice(j)
            s = _dot(k_ref[ks, :], qt)
            if masked:
                s = jnp.where(_mask_t(False), s, _NEG)
            p = jnp.exp(s - lse)
            ds = (p * (_dot(v_ref[ks, :], dot) - delta)).astype(BF16)
            dk_ref[ks, :] += _dot(ds, qn)
            dv_ref[ks, :] += _dot(p.astype(BF16), don)
            return dq + _dot(kt_ref[:, ks], ds)

        dqt_ref[...] = _walk_down(i, step, alive, jnp.zeros((LANES, ATT_BQ), F32))

    grid_spec = pltpu.PrefetchScalarGridSpec(
        num_scalar_prefetch=2, grid=(H, S // ATT_BQ),
        in_specs=[_t_block(LANES), _n_block(), _n_full(S), _t_full(LANES, S), _n_full(S),
                  _t_block(LANES), _n_block(), _t_block(HEAD_DIM), _t_block(1)],
        out_specs=(_t_block(LANES), _n_full(S), _n_full(S)))
    return pl.pallas_call(
        body, name="fox_bwd", grid_spec=grid_spec,
        out_shape=(jax.ShapeDtypeStruct((H, LANES, S), F32), jax.ShapeDtypeStruct((H, S, LANES), F32),
                   jax.ShapeDtypeStruct((H, S, LANES), F32)),
        compiler_params=_params(dimension_semantics=("parallel", "arbitrary")),
    )(f_end, k_max, q_t, q_n, k_n, k_t, v_n, do_t, do_n, o_t, lse)


def _scan_lhs():
    r = lax.broadcasted_iota(jnp.int32, (SCAN_BK, 2 * SCAN_BK), 0)
    c = lax.broadcasted_iota(jnp.int32, (SCAN_BK, 2 * SCAN_BK), 1) % SCAN_BK
    return (c >= r).astype(BF16)


def _suffix_sum(t, lhs):
    hi = t.astype(BF16)
    lo = (t - hi.astype(F32)).astype(BF16)
    return _dot(lhs, jnp.concatenate([hi, lo], axis=0))


def _sb_scores(k, qt, mask):
    z = _dot(k, qt)
    e = jnp.exp(-jnp.abs(z))
    lb = -(jnp.maximum(z, 0.0) + jnp.log(1.0 + e))
    if mask is not None:
        lb = jnp.where(mask, lb, 0.0)
    return z, e, lb


def _scan_blocks():
    return [slice(u * SCAN_BK, (u + 1) * SCAN_BK) for u in reversed(range(N_SCAN))]


def _sb_fwd(q_t, k_n, v_t):
    H, _, S = q_t.shape

    def body(qt_ref, k_ref, vt_ref, ot_ref):
        i = pl.program_id(1)
        qt = qt_ref[...]
        lhs = _scan_lhs()

        def step(j, carry, masked):
            c, acc = carry
            ks = _kv_slice(j)
            mask = _mask_t(True) if masked else None
            z, _, lb = _sb_scores(k_ref[ks, :], qt, mask)
            parts = []
            for sl in _scan_blocks():
                rin = _suffix_sum(lb[sl], lhs)
                a = jnp.exp(z[sl] + rin + c)
                if masked:
                    a = jnp.where(mask[sl], a, 0.0)
                parts.append(a.astype(BF16))
                c = c + rin[0:1, :]
            a_all = jnp.concatenate(parts[::-1], axis=0)
            return c, acc + _dot(vt_ref[:, ks], a_all)

        carry = (jnp.zeros((1, ATT_BQ), F32), jnp.zeros((HEAD_DIM, ATT_BQ), F32))
        ot_ref[...] = _walk_down(i, step, lambda j, cr: jnp.max(cr[0]) > SKIP_BELOW, carry)[1]

    return pl.pallas_call(
        body, name="sb_fwd", grid=(H, S // ATT_BQ),
        out_shape=jax.ShapeDtypeStruct((H, HEAD_DIM, S), F32),
        in_specs=[_t_block(LANES), _n_full(S), _t_full(HEAD_DIM, S)],
        out_specs=_t_block(HEAD_DIM),
        compiler_params=_params(dimension_semantics=("parallel", "parallel")),
    )(q_t, k_n, v_t)


def _sb_bwd(q_t, q_n, k_n, k_t, v_n, do_t, do_n, o_t, bound):
    H, _, S = q_t.shape
    n, nq = len(bound), S // ATT_BQ

    def body(qt_ref, qn_ref, k_ref, kt_ref, v_ref, dot_ref, don_ref, ot_ref, *rest):
        ins, (dqt_ref, dk_ref, dv_ref) = rest[:n], rest[n:n + 3]
        outs, sems = rest[n + 3:2 * n + 3], rest[2 * n + 3:]
        h, i = pl.program_id(0), pl.program_id(1)

        @pl.when(jnp.logical_and(h == 0, i == 0))
        def _():
            for cp in _exchange_copies(ins, outs, *sems, True):
                cp.start()

        @pl.when(i == 0)
        def _():
            dk_ref[...] = jnp.zeros_like(dk_ref)
            dv_ref[...] = jnp.zeros_like(dv_ref)

        qt, qn, dot, don = qt_ref[...], qn_ref[...], dot_ref[...], don_ref[...]
        lhs = _scan_lhs()
        delta = jnp.sum(dot[:HEAD_DIM].astype(F32) * ot_ref[...], axis=0, keepdims=True)

        def step(j, carry, masked):
            c, g, dq = carry
            ks = _kv_slice(j)
            mask = _mask_t(True) if masked else None
            z, e, lb = _sb_scores(k_ref[ks, :], qt, mask)
            da = _dot(v_ref[ks, :], dot)
            a_parts, dz_parts = [], []
            for sl in _scan_blocks():
                rin = _suffix_sum(lb[sl], lhs)
                a = jnp.exp(z[sl] + rin + c)
                if masked:
                    a = jnp.where(mask[sl], a, 0.0)
                ab = a.astype(BF16)
                gg = ab.astype(F32) * da[sl]
                rgin = _suffix_sum(gg, lhs)
                rinv = 1.0 / (1.0 + e[sl])
                sig = jnp.where(z[sl] >= 0.0, rinv, e[sl] * rinv)
                dz = gg - sig * (delta - g - (rgin - gg))
                if masked:
                    dz = jnp.where(mask[sl], dz, 0.0)
                a_parts.append(ab)
                dz_parts.append(dz.astype(BF16))
                c = c + rin[0:1, :]
                g = g + rgin[0:1, :]
            ab_all = jnp.concatenate(a_parts[::-1], axis=0)
            dzb = jnp.concatenate(dz_parts[::-1], axis=0)
            dk_ref[ks, :] += _dot(dzb, qn)
            dv_ref[ks, :] += _dot(ab_all, don)
            return c, g, dq + _dot(kt_ref[:, ks], dzb)

        row = jnp.zeros((1, ATT_BQ), F32)
        carry = (row, row, jnp.zeros((LANES, ATT_BQ), F32))
        dqt_ref[...] = _walk_down(i, step, lambda j, cr: jnp.max(cr[0]) > SKIP_BELOW, carry)[2]

        @pl.when(jnp.logical_and(h == H - 1, i == nq - 1))
        def _():
            for cp in _exchange_copies(ins, outs, *sems, True):
                cp.wait()

    any_spec = pl.BlockSpec(memory_space=pl.ANY)
    res = pl.pallas_call(
        body, name="sb_bwd", grid=(H, nq),
        out_shape=tuple([jax.ShapeDtypeStruct((H, LANES, S), F32), jax.ShapeDtypeStruct((H, S, LANES), F32),
                         jax.ShapeDtypeStruct((H, S, LANES), F32)] + _exchange_out_shapes(bound, True)),
        in_specs=[_t_block(LANES), _n_block(), _n_full(S), _t_full(LANES, S), _n_full(S),
                  _t_block(LANES), _n_block(), _t_block(HEAD_DIM)] + [any_spec] * n,
        out_specs=tuple([_t_block(LANES), _n_full(S), _n_full(S)] + [any_spec] * n),
        scratch_shapes=_exchange_sems(n),
        compiler_params=_params(dimension_semantics=("arbitrary", "arbitrary"), has_side_effects=True),
    )(q_t, q_n, k_n, k_t, v_n, do_t, do_n, o_t, *bound)
    return res[0], res[1], res[2], res[3:]


def _dqkv_assemble(dqf_t, dkf, dvf, dqs_t, dks, dvs):
    H, _, S = dqf_t.shape
    n_pairs = H // 2
    tm = _tile(S, 512)

    def body(dqf_ref, dkf_ref, dvf_ref, dqs_ref, dks_ref, dvs_ref, out_ref, dfk_ref):
        lane = lax.broadcasted_iota(jnp.int32, (1, LANES), 1)
        slabs = ((dqf_ref, True), (dkf_ref, False), (dvf_ref, False),
                 (dqs_ref, True), (dks_ref, False), (dvs_ref, False))
        for k, (ref, transposed) in enumerate(slabs):
            if transposed:
                t0, t1 = ref[0].T * _SCALE, ref[1].T * _SCALE
            else:
                t0, t1 = ref[0], ref[1]
            out_ref[k] = jnp.where(lane < HEAD_DIM, t0, pltpu.roll(t1, HEAD_DIM, 1)).astype(BF16)
        for e in range(2):
            dfk_ref[e] = dkf_ref[e].T[Q_ONE_LANE:Q_ONE_LANE + 1, :]

    t_spec = pl.BlockSpec((2, LANES, tm), lambda i, p: (p, 0, i))
    n_spec = pl.BlockSpec((2, tm, LANES), lambda i, p: (p, i, 0))
    return pl.pallas_call(
        body, name="dqkv_assemble", grid=(S // tm, n_pairs),
        out_shape=(jax.ShapeDtypeStruct((6, S, n_pairs * LANES), BF16), jax.ShapeDtypeStruct((H, 1, S), F32)),
        in_specs=[t_spec, n_spec, n_spec, t_spec, n_spec, n_spec],
        out_specs=(pl.BlockSpec((6, tm, LANES), lambda i, p: (0, i, p)),
                   pl.BlockSpec((2, 1, tm), lambda i, p: (p, 0, i))),
        compiler_params=_params(dimension_semantics=("parallel", "parallel")),
    )(dqf_t, dkf, dvf, dqs_t, dks, dvs)


def _acc_spec(w):
    return pl.BlockSpec((1, w), lambda i: (0, 0))


def _loss_head(x1, act, w_down, gate_m, g_final, target):
    S, D = x1.shape
    F = act.shape[1]
    tm = _tile(S, 256)

    def body(x1_ref, act_ref, w_ref, gt_ref, gf_ref, tg_ref, dx2_ref, gm_ref, loss_ref, dgf_ref, dgt_ref):
        @pl.when(pl.program_id(0) == 0)
        def _():
            loss_ref[...] = jnp.zeros_like(loss_ref)
            dgf_ref[...] = jnp.zeros_like(dgf_ref)
            dgt_ref[...] = jnp.zeros_like(dgt_ref)

        mo = _dot(act_ref[...], w_ref[...])
        x2 = x1_ref[...] + gt_ref[...] * mo
        r = lax.rsqrt(jnp.mean(x2 * x2, axis=-1, keepdims=True) + EPS)
        xh = x2 * r
        diff = xh * gf_ref[...] - tg_ref[...]
        loss_ref[...] += (0.5 / D) * jnp.sum(diff * diff)
        dy = diff * (1.0 / D)
        dgf_ref[...] += jnp.sum(dy * xh, axis=0, keepdims=True)
        dxh = dy * gf_ref[...]
        dx2 = r * (dxh - xh * jnp.mean(dxh * xh, axis=-1, keepdims=True))
        dx2_ref[...] = dx2
        gm_ref[...] = (dx2 * gt_ref[...]).astype(BF16)
        dgt_ref[...] += jnp.sum(dx2 * mo, axis=0, keepdims=True)

    return pl.pallas_call(
        body, name="loss_head", grid=(S // tm,),
        out_shape=(jax.ShapeDtypeStruct((S, D), F32), jax.ShapeDtypeStruct((S, D), BF16),
                   jax.ShapeDtypeStruct((1, LANES), F32), jax.ShapeDtypeStruct((1, D), F32),
                   jax.ShapeDtypeStruct((1, D), F32)),
        in_specs=_row_specs(tm, [D, F]) + [pl.BlockSpec((F, D), lambda i: (0, 0))] + [_vec_spec(D)] * 2
        + _row_specs(tm, [D]),
        out_specs=tuple(_row_specs(tm, [D, D]) + [_acc_spec(LANES), _acc_spec(D), _acc_spec(D)]),
        compiler_params=_params(dimension_semantics=("arbitrary",)),
    )(x1, act, w_down, gate_m, g_final, target)


def _norm_bwd(lhs, rhs, xin, dres, g, scale, name, gate=None, branch=None, bound=(), conv=None):
    S, D = xin.shape
    tm = _tile(S, 256)
    gated = gate is not None
    nl, nb, n_steps = len(lhs), len(bound), S // tm
    nc_ = nl if conv else 0
    n_out = (6 if gated else 4) + nc_

    def body(*refs):
        l_refs, r_refs, rest = refs[:nl], refs[nl:2 * nl], refs[2 * nl:]
        halo_refs, cw_refs, rest = rest[:nc_], rest[nc_:2 * nc_], rest[2 * nc_:]
        if gated:
            x_ref, dr_ref, g_ref, sc_ref, gt_ref, br_ref = rest[:6]
            rest = rest[6:]
        else:
            x_ref, dr_ref, g_ref, sc_ref = rest[:4]
            rest = rest[4:]
        ins, outs, ex_outs, sems = rest[:nb], rest[nb:nb + n_out], rest[nb + n_out:2 * nb + n_out], rest[2 * nb + n_out:]
        dx_ref, dsc_ref, dsh_ref, dg_ref = outs[:4]
        sums = (dsc_ref, dsh_ref, dg_ref) + ((outs[5],) if gated else ())
        dup_refs = outs[n_out - nc_:]
        i = pl.program_id(0)

        @pl.when(i == 0)
        def _():
            for s_ref in sums:
                s_ref[...] = jnp.zeros_like(s_ref)
            if nb:
                for cp in _chip_exchange_copies(ins, ex_outs, *sems):
                    cp.start()

        dhv = None
        for k, (l_ref, r_ref) in enumerate(zip(l_refs, r_refs)):
            if conv:
                halo = jnp.where(i == n_steps - 1, 0.0, halo_refs[k][...].astype(F32))
                dup = _conv_bwd_taps(l_ref[...].astype(F32), halo, cw_refs[k]).astype(BF16)
                dup_refs[k][...] = dup
                terms = [_dot_nt(dup, r_ref[...])]
            elif len(l_ref.shape) == 3:
                K = l_ref.shape[2]
                terms = [_dot_nt(l_ref[k], r_ref[:, k * K:(k + 1) * K]) for k in range(l_ref.shape[0])]
            else:
                terms = [_dot_nt(l_ref[...], r_ref[...])]
            for t in terms:
                dhv = t if dhv is None else dhv + t
        xv = x_ref[...]
        r = lax.rsqrt(jnp.mean(xv * xv, axis=-1, keepdims=True) + EPS)
        xh = xv * r
        dsc_ref[...] += jnp.sum(dhv * (xh * g_ref[...]), axis=0, keepdims=True)
        dsh_ref[...] += jnp.sum(dhv, axis=0, keepdims=True)
        dn = dhv * (1.0 + sc_ref[...])
        dg_ref[...] += jnp.sum(dn * xh, axis=0, keepdims=True)
        dxh = dn * g_ref[...]
        dx = dr_ref[...] + r * (dxh - xh * jnp.mean(dxh * xh, axis=-1, keepdims=True))
        dx_ref[...] = dx
        if gated:
            outs[4][...] = (dx * gt_ref[...]).astype(BF16)
            outs[5][...] += jnp.sum(dx * br_ref[...], axis=0, keepdims=True)

        if nb:
            @pl.when(i == n_steps - 1)
            def _():
                for cp in _chip_exchange_copies(ins, ex_outs, *sems):
                    cp.wait()

    def l_spec(a):
        if a.ndim == 3:
            return pl.BlockSpec((a.shape[0], tm, a.shape[2]), lambda i: (0, i, 0))
        return pl.BlockSpec((tm, a.shape[1]), lambda i: (i, 0))

    any_spec = pl.BlockSpec(memory_space=pl.ANY)
    vec = jax.ShapeDtypeStruct((1, D), F32)
    out_shape = [jax.ShapeDtypeStruct((S, D), F32), vec, vec, vec]
    out_specs = _row_specs(tm, [D]) + [_acc_spec(D)] * 3
    in_specs = [l_spec(a) for a in lhs] + [pl.BlockSpec(b.shape, lambda i: (0, 0)) for b in rhs]
    args = list(lhs) + list(rhs)
    if conv:
        step, last_halo = tm // HALO, S // HALO - 1
        in_specs += [pl.BlockSpec((HALO, a.shape[1]), lambda i: (jnp.minimum((i + 1) * step, last_halo), 0))
                     for a in lhs]
        in_specs += [pl.BlockSpec(w.shape, lambda i: (0, 0)) for w in conv]
        args += list(lhs) + list(conv)
    in_specs += _row_specs(tm, [D, D]) + [_vec_spec(D)] * 2
    args += [xin, dres, g, scale]
    if gated:
        out_shape += [jax.ShapeDtypeStruct((S, D), BF16), vec]
        out_specs += _row_specs(tm, [D]) + [_acc_spec(D)]
        in_specs += [_vec_spec(D)] + _row_specs(tm, [D])
        args += [gate, branch]
    if conv:
        out_shape += [jax.ShapeDtypeStruct(a.shape, BF16) for a in lhs]
        out_specs += [l_spec(a) for a in lhs]
    res = pl.pallas_call(
        body, name=name, grid=(n_steps,),
        out_shape=tuple(out_shape + [jax.ShapeDtypeStruct(b.shape, b.dtype) for b in bound]),
        in_specs=in_specs + [any_spec] * nb, out_specs=tuple(out_specs + [any_spec] * nb),
        scratch_shapes=_chip_exchange_sems(nb) if nb else [],
        compiler_params=_params(dimension_semantics=("arbitrary",), has_side_effects=bool(nb)),
    )(*args, *bound)
    return tuple(res[:n_out]) + (tuple(res[n_out:]),)


def _headnorm_bwd(dmix, o_f, o_s, g_f, g_s):
    S, dh = o_f.shape
    H = dh // HEAD_DIM
    tm = _tile(S, 256)

    def body(dm_ref, of_ref, os_ref, gf_ref, gs_ref, fn_ref, ft_ref, sn_ref, st_ref, dgf_ref, dgs_ref):
        @pl.when(pl.program_id(0) == 0)
        def _():
            dgf_ref[...] = jnp.zeros_like(dgf_ref)
            dgs_ref[...] = jnp.zeros_like(dgs_ref)

        ones = _group_ones()
        lane = lax.broadcasted_iota(jnp.int32, (1, LANES), 1)
        parts = ((of_ref, gf_ref, fn_ref, ft_ref, dgf_ref), (os_ref, gs_ref, sn_ref, st_ref, dgs_ref))
        for part, (o_ref, g_ref, n_ref, t_ref, dg_ref) in enumerate(parts):
            for t in range(dh // LANES):
                cols = slice(t * LANES, (t + 1) * LANES)
                o = o_ref[:, cols]
                dm = dm_ref[:, part * dh + t * LANES: part * dh + (t + 1) * LANES]
                r = lax.rsqrt(_dot(o * o, ones, precision=HIGHEST) * (1.0 / HEAD_DIM) + EPS)
                oh = o * r
                dg_ref[:, cols] += jnp.sum(dm * oh, axis=0, keepdims=True)
                dn = dm * g_ref[:, cols]
                mean = _dot(dn * oh, ones, precision=HIGHEST) * (1.0 / HEAD_DIM)
                do = r * (dn - oh * mean)
                for e in range(2):
                    d = do if e == 0 else pltpu.roll(do, HEAD_DIM, 1)
                    d = jnp.where(lane < HEAD_DIM, d, 0.0)
                    n_ref[2 * t + e] = d.astype(BF16)
                    t_ref[2 * t + e] = d.T.astype(BF16)

    vec = jax.ShapeDtypeStruct((1, dh), F32)
    n_sds = jax.ShapeDtypeStruct((H, S, LANES), BF16)
    t_sds = jax.ShapeDtypeStruct((H, LANES, S), BF16)
    n_spec = pl.BlockSpec((H, tm, LANES), lambda i: (0, i, 0))
    t_spec = pl.BlockSpec((H, LANES, tm), lambda i: (0, 0, i))
    return pl.pallas_call(
        body, name="headnorm_bwd", grid=(S // tm,),
        out_shape=(n_sds, t_sds, n_sds, t_sds, vec, vec),
        in_specs=_row_specs(tm, [2 * dh, dh, dh]) + [_vec_spec(dh)] * 2,
        out_specs=(n_spec, t_spec, n_spec, t_spec, _acc_spec(dh), _acc_spec(dh)),
        compiler_params=_params(dimension_semantics=("arbitrary",)),
    )(dmix, o_f, o_s, g_f, g_s)


def _adamw(w, gslots, m, v, name):
    R, C = w.shape
    n = gslots.shape[0]
    tr = 256 if (R % 256 == 0 and R > 256) else R
    bc1 = 1.0 - ADAM_B1 ** ADAM_STEP
    bc2 = 1.0 - ADAM_B2 ** ADAM_STEP

    def body(w_ref, gs_ref, m_ref, v_ref, g_ref, d_ref, nm_ref, nv_ref):
        g = gs_ref[0]
        for s in range(1, n):
            g = g + gs_ref[s]
        nm = ADAM_B1 * m_ref[...] + (1.0 - ADAM_B1) * g
        nv = ADAM_B2 * v_ref[...] + (1.0 - ADAM_B2) * (g * g)
        g_ref[...] = g
        nm_ref[...] = nm
        nv_ref[...] = nv
        d_ref[...] = -ADAM_LR * ((nm / bc1) / (jnp.sqrt(nv / bc2) + ADAM_EPS) + ADAM_WD * w_ref[...])

    blk = pl.BlockSpec((tr, C), lambda i: (i, 0))
    sds = jax.ShapeDtypeStruct((R, C), F32)
    return pl.pallas_call(
        body, name=name, grid=(R // tr,), out_shape=(sds,) * 4,
        in_specs=[blk, pl.BlockSpec((n, tr, C), lambda i: (0, i, 0)), blk, blk], out_specs=(blk,) * 4,
        compiler_params=_params(dimension_semantics=("parallel",)),
    )(w, gslots, m, v)


def _slot_sum(slots, name):
    n, _, C = slots.shape

    def body(s_ref, o_ref):
        acc = s_ref[0]
        for s in range(1, n):
            acc = acc + s_ref[s]
        o_ref[...] = acc

    return pl.pallas_call(body, name=name, out_shape=jax.ShapeDtypeStruct((1, C), F32),
                          compiler_params=_params())(slots)


def _pad_cols(a, n):
    return jnp.pad(a, ((0, 0), (0, n - a.shape[1])))


def _ungather(g, axis):
    if axis == 0:
        return g.reshape(g.shape[0] * g.shape[1], g.shape[2])
    return jnp.transpose(g, (1, 0, 2)).reshape(g.shape[1], g.shape[0] * g.shape[2])


def _to_slots(full, axis):
    R, C = full.shape
    if axis == 0:
        return full.reshape(N_DEV, R // N_DEV, C)
    return jnp.transpose(full.reshape(R, N_DEV, C // N_DEV), (1, 0, 2))


def kernel(x, c, w_ada, b_ada, g_attn, w_in, b_fgate, g_out_fox, g_out_sb, w_out, g_mlp, w_up, conv_w, conv_b, w_down, g_final, loss_target, m_w_ada, m_b_ada, m_g_attn, m_w_in, m_b_fgate, m_g_out_fox, m_g_out_sb, m_w_out, m_g_mlp, m_w_up, m_conv_w, m_conv_b, m_w_down, m_g_final, v_w_ada, v_b_ada, v_g_attn, v_w_in, v_b_fgate, v_g_out_fox, v_g_out_sb, v_w_out, v_g_mlp, v_w_up, v_conv_w, v_conv_b, v_w_down, v_g_final):
    S, D = x.shape[1], x.shape[2]
    dh = D // 2
    n_heads = dh // HEAD_DIM
    n_qkv = 6 * dh
    ff = w_down.shape[1] * N_DEV
    ffp = -(-ff // (2 * LANES)) * (2 * LANES)
    nc = S // LANES
    me = 4 * lax.axis_index("x") + 2 * lax.axis_index("y") + lax.axis_index("c")
    xs, tgt = x[0], loss_target[0]

    c_all, win_g = _gather_two_level([c, w_in[0].astype(BF16)], name="gather_first")
    c_all = c_all.reshape(N_DEV, D)
    W_in = _ungather(win_g, 1)
    W_qkv, W_f = W_in[:, :n_qkv], _pad_cols(W_in[:, n_qkv:], LANES)
    cb_g, cb_v = _pad_cols(conv_b[:, :ff], ffp), _pad_cols(conv_b[:, ff:], ffp)

    n_ada = w_ada.shape[2]
    b_shard = lax.dynamic_slice(b_ada, (0, me * n_ada), (1, n_ada))
    mod_cols = _ada_fwd(c_all, w_ada[0], b_shard)
    (mod_g,) = _exchange([mod_cols], scatter=False, name="gather_mod")
    mod = lax.dynamic_index_in_dim(mod_g, me, axis=1, keepdims=False).reshape(6, 1, D)
    shift_a, scale_a, gate_a, shift_m, scale_m, gate_m = [mod[k] for k in range(6)]

    h1, h1_t = _prenorm(xs, g_attn, scale_a, shift_a, "prenorm_attn")
    qkv = _mm(h1, W_qkv, BF16, "proj_qkv")
    flog = _mm(h1, W_f, F32, "proj_fgate")
    zf = flog[:, :n_heads] + b_fgate
    z_rows = zf.T.reshape(n_heads * nc, LANES)
    f_rows = _fgate_fwd(z_rows, nc).reshape(n_heads, S)
    f_pairs = jnp.transpose(f_rows.reshape(n_heads // 2, 2, S), (0, 2, 1))
    fox, sb = _att_prep(qkv, f_pairs)
    f_end, k_max = _skip_bounds(qkv[:, dh:2 * dh], f_rows)
    of_t, lse, (wout_g, wup_g, wdown_g, convw_g) = _fox_fwd(
        fox["q_t"], fox["k_n"], fox["v_t"], f_end, k_max,
        [w_out[0].astype(BF16), w_up[0].astype(BF16), w_down[0].astype(BF16), conv_w[0]])
    W_out = _ungather(wout_g, 0)
    W_up = _ungather(wup_g, 1)
    W_g, W_v = _pad_cols(W_up[:, :ff], ffp), _pad_cols(W_up[:, ff:], ffp)
    W_down = jnp.pad(_ungather(wdown_g, 0), ((0, ffp - ff), (0, 0)))
    cw_full = _ungather(convw_g, 1)
    cw_g, cw_v = _pad_cols(cw_full[:, :ff], ffp), _pad_cols(cw_full[:, ff:], ffp)
    os_t = _sb_fwd(sb["q_t"], sb["k_n"], sb["v_t"])
    o_f, o_s = _unheads(_tr(of_t)), _unheads(_tr(os_t))
    mix, mix_t = _headnorm_fwd(o_f, o_s, g_out_fox, g_out_sb)
    a_out = _mm(mix, W_out, F32, "proj_out")
    x1, h2, h2_t = _resid_prenorm(xs, a_out, gate_a, g_mlp, scale_m, shift_m)
    up_g, up_v, act, act_t = _mlp_up(h2, W_g, W_v, cw_g, cw_v, cb_g, cb_v)

    dx2, gm, loss_p, dg_final, dgate_m = _loss_head(x1, act, W_down, gate_m, g_final.reshape(1, D), tgt)
    dact = _mm(gm, W_down, BF16, "bwd_down_act", tn=1408, nt=True)
    dW_down = _mm_acc(act_t, gm, "bwd_down_w")
    du_g, du_v, p_g, p_v = _conv_act_bwd(dact, up_g, up_v, cw_g, cw_v, cb_g, cb_v)
    dx1, dscale_m, dshift_m, dg_mlp, ga, dgate_a, dup_g, dup_v, _ = _norm_bwd(
        [du_g, du_v], [W_g, W_v], x1, dx2, g_mlp, scale_m, "norm_mlp_bwd", gate=gate_a, branch=a_out,
        conv=[cw_g, cw_v])
    dW_g = _mm_acc(h2_t, dup_g, "bwd_up_w_g")
    dW_v = _mm_acc(h2_t, dup_v, "bwd_up_w_v")
    dmix = _mm(ga, W_out, F32, "bwd_out_act", nt=True)
    dW_out = _mm_acc(mix_t, ga, "bwd_out_w")
    dof_n, dof_t, dos_n, dos_t, dg_fox, dg_sb = _headnorm_bwd(dmix, o_f, o_s, g_out_fox, g_out_sb)
    dqf_t, dkf, dvf = _fox_bwd(fox["q_t"], fox["q_n"], fox["k_n"], fox["k_t"], fox["v_n"], dof_t, dof_n, of_t, lse,
                              f_end, k_max)
    dW_upf = jnp.concatenate([dW_g[:, :ff], dW_v[:, :ff]], axis=1)
    dcw = jnp.concatenate([p_g[:CONV_W, :ff], p_v[:CONV_W, :ff]], axis=1)
    dqs_t, dks, dvs, (s_out, s_up, s_down, s_cw) = _sb_bwd(
        sb["q_t"], sb["q_n"], sb["k_n"], sb["k_t"], sb["v_n"], dos_t, dos_n, os_t,
        [_to_slots(dW_out, 0), _to_slots(dW_upf, 1), _to_slots(dW_down[:ff], 0), _to_slots(dcw, 1)])
    dparts, dfk = _dqkv_assemble(dqf_t, dkf, dvf, dqs_t, dks, dvs)
    dz_rows, db_fgate = _fgate_bwd(dfk.reshape(n_heads * nc, LANES),
                                   dqf_t[:, Q_F_LANE, :].reshape(n_heads * nc, LANES), z_rows, nc)
    dzf = _pad_cols(dz_rows.reshape(n_heads, S).T, LANES).astype(BF16)
    dW_qkv = _mm_acc_parts(h1_t, dparts, "bwd_in_w")
    dW_f = _mm_acc(h1_t, dzf, "bwd_in_w_fgate")
    dW_in = jnp.concatenate([jnp.transpose(dW_qkv, (1, 0, 2)).reshape(D, n_qkv), dW_f[:, :n_heads]], axis=1)
    bound_in = _to_slots(dW_in, 1)
    bound_in = bound_in.reshape((N_CHIPS, 2) + bound_in.shape[1:])
    (got_in,) = _scatter_to_sibling([bound_in], "scatter_sibling")
    c_idx = lax.axis_index("c").astype(jnp.int32).reshape(1)
    grad_x, dscale_a, dshift_a, dg_attn, (s_in,) = _norm_bwd(
        [dparts, dzf], [W_qkv, W_f], xs, dx1, g_attn, scale_a, "norm_attn_bwd",
        bound=[_pair_add(bound_in, got_in, c_idx, "pair_add")])

    dconv_b = jnp.concatenate([p_g[CONV_W:CONV_W + 1, :ff], p_v[CONV_W:CONV_W + 1, :ff]], axis=1)
    parts = [dshift_a, dscale_a, dgate_a, dshift_m, dscale_m, dgate_m,
             dg_attn, db_fgate.reshape(1, n_heads), dg_fox, dg_sb, dg_mlp, dconv_b, dg_final,
             loss_p[:, :1]]
    sizes = [p.shape[1] for p in parts]
    vec = jnp.concatenate(parts, axis=1)
    n_vec = -(-vec.shape[1] // LANES) * LANES
    vec = _pad_cols(vec, n_vec)
    (vec_g,) = _exchange([vec], scatter=False, name="gather_small")
    offs = [0]
    for s in sizes:
        offs.append(offs[-1] + s)

    def small(k0, k1=None):
        k1 = k0 if k1 is None else k1
        return vec_g[:, :, offs[k0]:offs[k1 + 1]]

    dmod_all = small(0, 5).reshape(N_DEV, 6 * D)
    dmod_cols = lax.dynamic_slice(dmod_all, (0, me * n_ada), (N_DEV, n_ada))
    dW_ada = _ada_bwd(c_all.T, dmod_cols)


    res = {}
    res["w_ada"] = _adamw(w_ada[0], dW_ada[None], m_w_ada[0], v_w_ada[0], "adamw_w_ada")
    res["w_in"] = _adamw(w_in[0], s_in, m_w_in[0], v_w_in[0], "adamw_w_in")
    res["w_out"] = _adamw(w_out[0], s_out, m_w_out[0], v_w_out[0], "adamw_w_out")
    res["w_up"] = _adamw(w_up[0], s_up, m_w_up[0], v_w_up[0], "adamw_w_up")
    res["w_down"] = _adamw(w_down[0], s_down, m_w_down[0], v_w_down[0], "adamw_w_down")
    res["conv_w"] = _adamw(conv_w[0], s_cw, m_conv_w[0], v_conv_w[0], "adamw_conv_w")
    small_names = ["b_ada", "g_attn", "b_fgate", "g_out_fox", "g_out_sb", "g_mlp", "conv_b", "g_final"]
    small_w = [b_ada, g_attn, b_fgate, g_out_fox, g_out_sb, g_mlp, conv_b, g_final.reshape(1, D)]
    small_m = [m_b_ada, m_g_attn, m_b_fgate, m_g_out_fox, m_g_out_sb, m_g_mlp, m_conv_b, m_g_final.reshape(1, D)]
    small_v = [v_b_ada, v_g_attn, v_b_fgate, v_g_out_fox, v_g_out_sb, v_g_mlp, v_conv_b, v_g_final.reshape(1, D)]
    small_res = _adamw(jnp.concatenate(small_w, axis=1), small(0, 12), jnp.concatenate(small_m, axis=1),
                       jnp.concatenate(small_v, axis=1), "adamw_small")
    lo = 0
    for nm, wv in zip(small_names, small_w):
        res[nm] = tuple(r[:, lo:lo + wv.shape[1]] for r in small_res)
        lo += wv.shape[1]
    loss = _slot_sum(_pad_cols(small(13).reshape(N_DEV, 1), LANES).reshape(N_DEV, 1, LANES), "loss_sum")[0, 0]

    names = ["w_ada", "b_ada", "g_attn", "w_in", "b_fgate", "g_out_fox", "g_out_sb", "w_out", "g_mlp",
             "w_up", "conv_w", "conv_b", "w_down", "g_final"]

    def shaped(n, a):
        if n == "g_final":
            return a.reshape(D)
        if n in ("b_ada", "g_attn", "b_fgate", "g_out_fox", "g_out_sb", "g_mlp", "conv_b"):
            return a
        return a[None]

    outs = [loss, grad_x[None]]
    for k in range(4):
        outs += [shaped(n, res[n][k]) for n in names]
    return tuple(outs)
```

```python
import jax
import jax.numpy as jnp
from jax import lax
from jax.experimental import pallas as pl
from jax.experimental.pallas import tpu as pltpu

F32 = jnp.float32
BF16 = jnp.bfloat16
HIGHEST = lax.Precision.HIGHEST

N_DEV = 8
LANES = 128
HEAD_DIM = 64
EPS = 1e-6
CONV_W = 3
CONV_COLS = 1408
HALO = 16
ATT_BQ = 512
ATT_BK = 512
SCAN_BK = 128
VMEM_LIMIT = 56 * 1024 * 1024

ADAM_LR = 0.001
ADAM_B1 = 0.9
ADAM_B2 = 0.999
ADAM_EPS = 1e-08
ADAM_WD = 0.01
ADAM_STEP = 10


def _params(**kw):
    return pltpu.CompilerParams(vmem_limit_bytes=VMEM_LIMIT, **kw)


def _tile(n, cap):
    if n <= cap:
        return n
    best = None
    for t in range(LANES, cap + 1, LANES):
        if n % t == 0:
            best = t
    assert best is not None, (n, cap)
    return best


def _dot(a, b, **kw):
    return jnp.dot(a, b, preferred_element_type=F32, **kw)


def _exchange_copies(ins, outs, send_sems, recv_sems, loc_sems, scatter):
    n = len(ins)
    if n == 0:
        return []
    x, y, c = lax.axis_index("x"), lax.axis_index("y"), lax.axis_index("c")
    me = 4 * x + 2 * y + c
    copies = []
    for a in range(n):
        src = ins[a].at[me] if scatter else ins[a]
        copies.append(pltpu.make_async_copy(src, outs[a].at[me], loc_sems.at[a]))
    for k in range(1, N_DEV):
        px = 1 - x if k & 4 else x
        py = 1 - y if k & 2 else y
        pc = 1 - c if k & 1 else c
        peer = 4 * px + 2 * py + pc
        for a in range(n):
            src = ins[a].at[peer] if scatter else ins[a]
            copies.append(pltpu.make_async_remote_copy(
                src_ref=src, dst_ref=outs[a].at[me],
                send_sem=send_sems.at[a, k - 1], recv_sem=recv_sems.at[a, k - 1],
                device_id=(px, py, pc), device_id_type=pl.DeviceIdType.MESH))
    return copies


def _exchange_out_shapes(arrays, scatter):
    return [jax.ShapeDtypeStruct((N_DEV,) + tuple(a.shape[1:] if scatter else a.shape), a.dtype) for a in arrays]


def _exchange_sems(n):
    return [pltpu.SemaphoreType.DMA((n, N_DEV - 1)), pltpu.SemaphoreType.DMA((n, N_DEV - 1)),
            pltpu.SemaphoreType.DMA((n,))]


def _exchange(arrays, scatter, name):
    n = len(arrays)

    def body(*refs):
        copies = _exchange_copies(refs[:n], refs[n:2 * n], *refs[2 * n:], scatter)
        for cp in copies:
            cp.start()
        for cp in copies:
            cp.wait()

    any_spec = pl.BlockSpec(memory_space=pl.ANY)
    return pl.pallas_call(
        body, name=name, out_shape=tuple(_exchange_out_shapes(arrays, scatter)),
        in_specs=[any_spec] * n, out_specs=tuple([any_spec] * n),
        scratch_shapes=_exchange_sems(n),
        compiler_params=pltpu.CompilerParams(has_side_effects=True),
    )(*arrays)


def _gather_two_level(arrays, name):
    n = len(arrays)
    out_shape = [jax.ShapeDtypeStruct((N_DEV,) + tuple(a.shape), a.dtype) for a in arrays]

    def body(*refs):
        ins, outs = refs[:n], refs[n:2 * n]
        send_sems, recv_sems, loc_sems = refs[2 * n:]
        x, y, c = lax.axis_index("x"), lax.axis_index("y"), lax.axis_index("c")
        me, sibling = (x, y, c), (x, y, 1 - c)
        chips = [(1 - x, y), (x, 1 - y), (1 - x, 1 - y)]

        def slot(px, py, pc):
            return 4 * px + 2 * py + pc

        def copy(a, k, block, to, src=None):
            dst = outs[a].at[slot(*block)]
            return pltpu.make_async_remote_copy(
                src_ref=dst if src is None else src, dst_ref=dst,
                send_sem=send_sems.at[a, k], recv_sem=recv_sems.at[a, k],
                device_id=to, device_id_type=pl.DeviceIdType.MESH)

        local = [pltpu.make_async_copy(ins[a], outs[a].at[slot(*me)], loc_sems.at[a]) for a in range(n)]
        for cp in local:
            cp.start()
        first = []
        for a in range(n):
            first.append(copy(a, 0, me, sibling, src=ins[a]))
            first += [copy(a, 1 + j, me, (*chip, c), src=ins[a]) for j, chip in enumerate(chips)]
        for cp in first:
            cp.start()
        passed = []
        for j, chip in enumerate(chips):
            for a in range(n):
                copy(a, 1 + j, (*chip, c), me).wait_recv()
                cp = copy(a, 4 + j, (*chip, c), sibling)
                cp.start()
                passed.append(cp)
        for a in range(n):
            copy(a, 0, sibling, me).wait_recv()
            for j, chip in enumerate(chips):
                copy(a, 4 + j, (*chip, 1 - c), me).wait_recv()
        for cp in first + passed:
            cp.wait_send()
        for cp in local:
            cp.wait()

    any_spec = pl.BlockSpec(memory_space=pl.ANY)
    return pl.pallas_call(
        body, name=name, out_shape=tuple(out_shape),
        in_specs=[any_spec] * n, out_specs=tuple([any_spec] * n),
        scratch_shapes=[pltpu.SemaphoreType.DMA((n, N_DEV - 1)), pltpu.SemaphoreType.DMA((n, N_DEV - 1)),
                        pltpu.SemaphoreType.DMA((n,))],
        compiler_params=pltpu.CompilerParams(has_side_effects=True),
    )(*arrays)


N_CHIPS = 4


def _scatter_to_sibling(arrays, name):
    n = len(arrays)
    out_shape = [jax.ShapeDtypeStruct((N_CHIPS,) + tuple(a.shape[2:]), a.dtype) for a in arrays]

    def body(*refs):
        ins, outs = refs[:n], refs[n:2 * n]
        send_sems, recv_sems = refs[2 * n:]
        x, y, c = lax.axis_index("x"), lax.axis_index("y"), lax.axis_index("c")
        copies = []
        for a in range(n):
            for q in range(N_CHIPS):
                cp = pltpu.make_async_remote_copy(
                    src_ref=ins[a].at[q, 1 - c], dst_ref=outs[a].at[q],
                    send_sem=send_sems.at[a, q], recv_sem=recv_sems.at[a, q],
                    device_id=(x, y, 1 - c), device_id_type=pl.DeviceIdType.MESH)
                cp.start()
                copies.append(cp)
        for cp in copies:
            cp.wait()

    any_spec = pl.BlockSpec(memory_space=pl.ANY)
    return pl.pallas_call(
        body, name=name, out_shape=tuple(out_shape),
        in_specs=[any_spec] * n, out_specs=tuple([any_spec] * n),
        scratch_shapes=[pltpu.SemaphoreType.DMA((n, N_CHIPS)), pltpu.SemaphoreType.DMA((n, N_CHIPS))],
        compiler_params=pltpu.CompilerParams(has_side_effects=True),
    )(*arrays)


def _pair_add(mine, got, c_idx, name):
    _, _, R, C = mine.shape
    tr = 256 if (R % 256 == 0 and R > 256) else R

    def body(c_ref, m_ref, g_ref, o_ref):
        o_ref[...] = m_ref[...] + g_ref[...]

    grid_spec = pltpu.PrefetchScalarGridSpec(
        num_scalar_prefetch=1, grid=(N_CHIPS, R // tr),
        in_specs=[pl.BlockSpec((None, None, tr, C), lambda q, i, c_ref: (q, c_ref[0], i, 0)),
                  pl.BlockSpec((None, tr, C), lambda q, i, c_ref: (q, i, 0))],
        out_specs=pl.BlockSpec((None, tr, C), lambda q, i, c_ref: (q, i, 0)))
    return pl.pallas_call(
        body, name=name, grid_spec=grid_spec, out_shape=jax.ShapeDtypeStruct((N_CHIPS, R, C), mine.dtype),
        compiler_params=_params(dimension_semantics=("parallel", "parallel")),
    )(c_idx, mine, got)


def _chip_exchange_copies(ins, outs, send_sems, recv_sems, loc_sems):
    n = len(ins)
    x, y, c = lax.axis_index("x"), lax.axis_index("y"), lax.axis_index("c")
    myq = 2 * x + y
    copies = [pltpu.make_async_copy(ins[a].at[myq], outs[a].at[myq], loc_sems.at[a]) for a in range(n)]
    for k in range(1, N_CHIPS):
        qx = 1 - x if k & 2 else x
        qy = 1 - y if k & 1 else y
        for a in range(n):
            copies.append(pltpu.make_async_remote_copy(
                src_ref=ins[a].at[2 * qx + qy], dst_ref=outs[a].at[myq],
                send_sem=send_sems.at[a, k - 1], recv_sem=recv_sems.at[a, k - 1],
                device_id=(qx, qy, c), device_id_type=pl.DeviceIdType.MESH))
    return copies


def _chip_exchange_sems(n):
    return [pltpu.SemaphoreType.DMA((n, N_CHIPS - 1)), pltpu.SemaphoreType.DMA((n, N_CHIPS - 1)),
            pltpu.SemaphoreType.DMA((n,))]


def _dot_nt(a, b):
    return lax.dot_general(a, b, (((1,), (1,)), ((), ())), preferred_element_type=F32)


def _rhs_spec(b, tn, nt):
    if nt:
        return pl.BlockSpec((tn, b.shape[1]), lambda i, j: (j, 0))
    return pl.BlockSpec((b.shape[0], tn), lambda i, j: (0, j))


def _mm(a, b, out_dtype, name, tm=1024, tn=512, nt=False):
    M, K = a.shape
    N = b.shape[0] if nt else b.shape[1]
    tm, tn = _tile(M, tm), _tile(N, tn)
    dot = _dot_nt if nt else _dot

    def body(a_ref, b_ref, o_ref):
        o_ref[...] = dot(a_ref[...], b_ref[...]).astype(out_dtype)

    return pl.pallas_call(
        body, name=name, out_shape=jax.ShapeDtypeStruct((M, N), out_dtype),
        grid=(M // tm, N // tn),
        in_specs=[pl.BlockSpec((tm, K), lambda i, j: (i, 0)), _rhs_spec(b, tn, nt)],
        out_specs=pl.BlockSpec((tm, tn), lambda i, j: (i, j)),
        compiler_params=_params(dimension_semantics=("parallel", "parallel")),
    )(a, b)


def _mm_acc(a, b, name, tm=1408, tn=1408, tk=512):
    M, S = a.shape
    _, N = b.shape
    tm, tn, tk = _tile(M, tm), _tile(N, tn), _tile(S, tk)

    def body(a_ref, b_ref, o_ref):
        @pl.when(pl.program_id(2) == 0)
        def _():
            o_ref[...] = jnp.zeros_like(o_ref)

        o_ref[...] += _dot(a_ref[...], b_ref[...])

    return pl.pallas_call(
        body, name=name, out_shape=jax.ShapeDtypeStruct((M, N), F32),
        grid=(M // tm, N // tn, S // tk),
        in_specs=[pl.BlockSpec((tm, tk), lambda i, j, k: (i, k)), pl.BlockSpec((tk, tn), lambda i, j, k: (k, j))],
        out_specs=pl.BlockSpec((tm, tn), lambda i, j, k: (i, j)),
        compiler_params=_params(dimension_semantics=("parallel", "parallel", "arbitrary")),
    )(a, b)


def _mm_acc_parts(a, parts, name, tm=1024, tk=1024):
    M, S = a.shape
    P, _, K = parts.shape
    tm, tk = _tile(M, tm), _tile(S, tk)

    def body(a_ref, b_ref, o_ref):
        @pl.when(pl.program_id(2) == 0)
        def _():
            o_ref[...] = jnp.zeros_like(o_ref)

        o_ref[...] += _dot(a_ref[...], b_ref[...])

    return pl.pallas_call(
        body, name=name, out_shape=jax.ShapeDtypeStruct((P, M, K), F32), grid=(P, M // tm, S // tk),
        in_specs=[pl.BlockSpec((tm, tk), lambda k, i, s: (i, s)), pl.BlockSpec((None, tk, K), lambda k, i, s: (k, s, 0))],
        out_specs=pl.BlockSpec((None, tm, K), lambda k, i, s: (k, i, 0)),
        compiler_params=_params(dimension_semantics=("parallel", "parallel", "arbitrary")),
    )(a, parts)


def _silu(z):
    return z * (1.0 / (1.0 + jnp.exp(-z)))


def _ada_fwd(c_all, w_shard, b_shard):
    n = w_shard.shape[1]

    def body(c_ref, w_ref, b_ref, o_ref):
        o_ref[...] = _dot(_silu(c_ref[...]), w_ref[...], precision=HIGHEST) + b_ref[...]

    return pl.pallas_call(body, name="ada_fwd", out_shape=jax.ShapeDtypeStruct((N_DEV, n), F32),
                          compiler_params=_params())(c_all, w_shard, b_shard)


def _ada_bwd(c_all_t, dmod_cols):
    D = c_all_t.shape[0]
    n = dmod_cols.shape[1]

    def body(ct_ref, dm_ref, o_ref):
        sc = _silu(ct_ref[...])
        dm = dm_ref[...]
        acc = sc[:, 0:1] * dm[0:1, :]
        for b in range(1, N_DEV):
            acc = acc + sc[:, b:b + 1] * dm[b:b + 1, :]
        o_ref[...] = acc

    return pl.pallas_call(body, name="ada_bwd", out_shape=jax.ShapeDtypeStruct((D, n), F32),
                          compiler_params=_params())(c_all_t, dmod_cols)


def _row_specs(tm, widths):
    return [pl.BlockSpec((tm, w), lambda i: (i, 0)) for w in widths]


def _vec_spec(w):
    return pl.BlockSpec((1, w), lambda i: (0, 0))


def _col_spec(tm, w):
    return pl.BlockSpec((w, tm), lambda i: (0, i))


def _prenorm(x, g, scale, shift, name):
    S, D = x.shape
    tm = _tile(S, 512)

    def body(x_ref, g_ref, sc_ref, sh_ref, h_ref, ht_ref):
        xv = x_ref[...]
        r = lax.rsqrt(jnp.mean(xv * xv, axis=-1, keepdims=True) + EPS)
        h = (xv * r) * g_ref[...] * (1.0 + sc_ref[...]) + sh_ref[...]
        h_ref[...] = h.astype(BF16)
        ht_ref[...] = h.T.astype(BF16)

    return pl.pallas_call(
        body, name=name, grid=(S // tm,),
        out_shape=(jax.ShapeDtypeStruct((S, D), BF16), jax.ShapeDtypeStruct((D, S), BF16)),
        in_specs=_row_specs(tm, [D]) + [_vec_spec(D)] * 3,
        out_specs=(_row_specs(tm, [D])[0], _col_spec(tm, D)),
        compiler_params=_params(dimension_semantics=("parallel",)),
    )(x, g, scale, shift)


def _group_ones():
    r = lax.broadcasted_iota(jnp.int32, (LANES, LANES), 0) // HEAD_DIM
    c = lax.broadcasted_iota(jnp.int32, (LANES, LANES), 1) // HEAD_DIM
    return (r == c).astype(F32)


def _headnorm_fwd(o_f, o_s, g_f, g_s):
    dh, S = o_f.shape
    tm = _tile(S, 512)

    def body(of_ref, os_ref, gf_ref, gs_ref, mix_ref, mixt_ref):
        ones = _group_ones()
        for part, (o_ref, g_ref) in enumerate(((of_ref, gf_ref), (os_ref, gs_ref))):
            for t in range(dh // LANES):
                cols = slice(t * LANES, (t + 1) * LANES)
                out = slice(part * dh + t * LANES, part * dh + (t + 1) * LANES)
                o = o_ref[cols, :].T
                ms = _dot(o * o, ones, precision=HIGHEST) * (1.0 / HEAD_DIM)
                mixn = o * lax.rsqrt(ms + EPS) * g_ref[:, cols]
                mix_ref[:, out] = mixn.astype(BF16)
                mixt_ref[out, :] = mixn.T.astype(BF16)

    return pl.pallas_call(
        body, name="headnorm_fwd", grid=(S // tm,),
        out_shape=(jax.ShapeDtypeStruct((S, 2 * dh), BF16), jax.ShapeDtypeStruct((2 * dh, S), BF16)),
        in_specs=[_col_spec(tm, dh)] * 2 + [_vec_spec(dh)] * 2,
        out_specs=(_row_specs(tm, [2 * dh])[0], _col_spec(tm, 2 * dh)),
        compiler_params=_params(dimension_semantics=("parallel",)),
    )(o_f, o_s, g_f, g_s)


def _resid_prenorm(x, a_out, gate, g, scale, shift):
    S, D = x.shape
    tm = _tile(S, 512)

    def body(x_ref, a_ref, gt_ref, g_ref, sc_ref, sh_ref, x1_ref, h_ref, ht_ref):
        x1 = x_ref[...] + gt_ref[...] * a_ref[...]
        x1_ref[...] = x1
        r = lax.rsqrt(jnp.mean(x1 * x1, axis=-1, keepdims=True) + EPS)
        h = (x1 * r) * g_ref[...] * (1.0 + sc_ref[...]) + sh_ref[...]
        h_ref[...] = h.astype(BF16)
        ht_ref[...] = h.T.astype(BF16)

    return pl.pallas_call(
        body, name="resid_prenorm", grid=(S // tm,),
        out_shape=(jax.ShapeDtypeStruct((S, D), F32), jax.ShapeDtypeStruct((S, D), BF16),
                   jax.ShapeDtypeStruct((D, S), BF16)),
        in_specs=_row_specs(tm, [D, D]) + [_vec_spec(D)] * 4,
        out_specs=tuple(_row_specs(tm, [D, D]) + [_col_spec(tm, D)]),
        compiler_params=_params(dimension_semantics=("parallel",)),
    )(x, a_out, gate, g, scale, shift)


def _shift_down(main, halo, k):
    ext = jnp.concatenate([halo, main], axis=0)
    return pltpu.roll(ext, k, 0)[halo.shape[0]:]


def _shift_up(main, halo, k):
    ext = jnp.concatenate([main, halo], axis=0)
    n = ext.shape[0]
    return pltpu.roll(ext, n - k, 0)[:main.shape[0]]


def _conv(up, up_halo, w_ref, b_ref):
    return (w_ref[2:3, :] * up + w_ref[1:2, :] * _shift_down(up, up_halo, 1)
            + w_ref[0:1, :] * _shift_down(up, up_halo, 2) + b_ref[...])


def _prev_halo_map(tm):
    step = tm // HALO
    return lambda j, i: (jnp.maximum(i * step - 1, 0), j)


MLP_TM = 512
MLP_CT = 1408
CARRY = 8


def _mlp_up(h, wg, wv, cwg, cwv, cbg, cbv):
    S, D = h.shape
    F = wg.shape[1]
    tm, ct = _tile(S, MLP_TM), _tile(F, MLP_CT)
    nct = F // ct

    def body(h_ref, wg_ref, wv_ref, cwg_ref, cwv_ref, cbg_ref, cbv_ref,
             upg_ref, upv_ref, act_ref, actt_ref, hg_scr, hv_scr):
        i, j = pl.program_id(0), pl.program_id(1)
        hv = h_ref[...]
        us = []
        for w_ref, cw_ref, cb_ref, up_ref, scr in ((wg_ref, cwg_ref, cbg_ref, upg_ref, hg_scr),
                                                   (wv_ref, cwv_ref, cbv_ref, upv_ref, hv_scr)):
            up = _dot(hv, w_ref[...]).astype(BF16)
            up_ref[...] = up
            upf = up.astype(F32)
            halo = jnp.where(i == 0, 0.0, scr[j])
            us.append(_conv(upf, halo, cw_ref, cb_ref))
            scr[j] = upf[tm - CARRY:, :]
        act = _silu(us[0]) * us[1]
        act_ref[...] = act.astype(BF16)
        actt_ref[...] = act.T.astype(BF16)

    blk = pl.BlockSpec((tm, ct), lambda i, j: (i, j))
    wspec = pl.BlockSpec((D, ct), lambda i, j: (0, j))
    cwspec = pl.BlockSpec((CONV_W, ct), lambda i, j: (0, j))
    cbspec = pl.BlockSpec((1, ct), lambda i, j: (0, j))
    sds = jax.ShapeDtypeStruct((S, F), BF16)
    return pl.pallas_call(
        body, name="mlp_up", grid=(S // tm, nct),
        out_shape=(sds, sds, sds, jax.ShapeDtypeStruct((F, S), BF16)),
        in_specs=[pl.BlockSpec((tm, D), lambda i, j: (i, 0)), wspec, wspec, cwspec, cwspec, cbspec, cbspec],
        out_specs=(blk, blk, blk, pl.BlockSpec((ct, tm), lambda i, j: (j, i))),
        scratch_shapes=[pltpu.VMEM((nct, CARRY, ct), F32), pltpu.VMEM((nct, CARRY, ct), F32)],
        compiler_params=_params(dimension_semantics=("arbitrary", "arbitrary")),
    )(h, wg, wv, cwg, cwv, cbg, cbv)


def _conv_act_bwd(gm, w_down, up_g, up_v, cwg, cwv, cbg, cbv):
    S, F = up_g.shape
    D = gm.shape[1]
    tm, ct = _tile(S, 256), _tile(F, CONV_COLS)
    nct = F // ct

    def body(gm_ref, wd_ref, ug_ref, uv_ref, hg_ref, hv_ref, wg_ref, wv_ref, bg_ref, bv_ref,
             dug_ref, duv_ref, pg_ref, pv_ref):
        first = pl.program_id(1) == 0

        @pl.when(first)
        def _():
            pg_ref[...] = jnp.zeros_like(pg_ref)
            pv_ref[...] = jnp.zeros_like(pv_ref)

        da = _dot_nt(gm_ref[...], wd_ref[...])
        taps = []
        for u_ref, h_ref in ((ug_ref, hg_ref), (uv_ref, hv_ref)):
            h = jnp.where(first, 0.0, h_ref[...].astype(F32))
            uu = u_ref[...].astype(F32)
            taps.append((_shift_down(uu, h, 2), _shift_down(uu, h, 1), uu))
        u_g = wg_ref[0:1, :] * taps[0][0] + wg_ref[1:2, :] * taps[0][1] + wg_ref[2:3, :] * taps[0][2] + bg_ref[...]
        u_v = wv_ref[0:1, :] * taps[1][0] + wv_ref[1:2, :] * taps[1][1] + wv_ref[2:3, :] * taps[1][2] + bv_ref[...]
        sg = 1.0 / (1.0 + jnp.exp(-u_g))
        du_g = da * u_v * (sg * (1.0 + u_g * (1.0 - sg)))
        du_v = da * (u_g * sg)
        dug_ref[...] = du_g.astype(BF16)
        duv_ref[...] = du_v.astype(BF16)
        for du, tp, p_ref in ((du_g, taps[0], pg_ref), (du_v, taps[1], pv_ref)):
            for k in range(CONV_W):
                p_ref[k:k + 1, :] += jnp.sum(du * tp[k], axis=0, keepdims=True)
            p_ref[CONV_W:CONV_W + 1, :] += jnp.sum(du, axis=0, keepdims=True)

    main = pl.BlockSpec((tm, ct), lambda j, i: (i, j))
    halo = pl.BlockSpec((HALO, ct), _prev_halo_map(tm))
    wspec = pl.BlockSpec((CONV_W, ct), lambda j, i: (0, j))
    bspec = pl.BlockSpec((1, ct), lambda j, i: (0, j))
    pspec = pl.BlockSpec((8, ct), lambda j, i: (0, j))
    return pl.pallas_call(
        body, name="conv_act_bwd", grid=(nct, S // tm),
        out_shape=(jax.ShapeDtypeStruct((S, F), BF16), jax.ShapeDtypeStruct((S, F), BF16),
                   jax.ShapeDtypeStruct((8, F), F32), jax.ShapeDtypeStruct((8, F), F32)),
        in_specs=[pl.BlockSpec((tm, D), lambda j, i: (i, 0)), pl.BlockSpec((ct, D), lambda j, i: (j, 0)),
                  main, main, halo, halo, wspec, wspec, bspec, bspec],
        out_specs=(main, main, pspec, pspec),
        compiler_params=_params(dimension_semantics=("parallel", "arbitrary")),
    )(gm, w_down, up_g, up_v, up_g, up_v, cwg, cwv, cbg, cbv)


def _conv_bwd_taps(d, halo, w_ref):
    return w_ref[2:3, :] * d + w_ref[1:2, :] * _shift_up(d, halo, 1) + w_ref[0:1, :] * _shift_up(d, halo, 2)


def _scan_mats(R, nc, reverse):
    i = lax.broadcasted_iota(jnp.int32, (LANES, LANES), 0)
    j = lax.broadcasted_iota(jnp.int32, (LANES, LANES), 1)
    inner = ((i >= j) if reverse else (i <= j)).astype(F32)
    r = lax.broadcasted_iota(jnp.int32, (R, R), 0)
    c = lax.broadcasted_iota(jnp.int32, (R, R), 1)
    same = (r // nc) == (c // nc)
    outer = (same & ((c > r) if reverse else (c < r))).astype(F32)
    return inner, outer


def _chunk_scan(v, inner, outer, reverse):
    w = _dot(v, inner, precision=HIGHEST)
    col = 0 if reverse else LANES - 1
    carry = _dot(outer, w, precision=HIGHEST)[:, col:col + 1]
    return w + carry


def _fgate_fwd(z_rows, nc):
    R = z_rows.shape[0]

    def body(z_ref, f_ref):
        z = z_ref[...]
        logf = jnp.minimum(z, 0.0) - jnp.log(1.0 + jnp.exp(-jnp.abs(z)))
        inner, outer = _scan_mats(R, nc, False)
        f_ref[...] = _chunk_scan(logf, inner, outer, False)

    return pl.pallas_call(body, name="fgate_fwd", out_shape=jax.ShapeDtypeStruct((R, LANES), F32),
                          compiler_params=_params())(z_rows)


def _fgate_bwd(dfk_neg_rows, dfq_rows, z_rows, nc):
    R = z_rows.shape[0]
    nh = R // nc

    def body(dfk_ref, dfq_ref, z_ref, dz_ref, db_ref):
        inner, outer = _scan_mats(R, nc, True)
        dlogf = _chunk_scan(dfq_ref[...] - dfk_ref[...], inner, outer, True)
        dz = dlogf * (1.0 / (1.0 + jnp.exp(z_ref[...])))
        dz_ref[...] = dz
        hr = lax.broadcasted_iota(jnp.int32, (nh, R), 0)
        hc = lax.broadcasted_iota(jnp.int32, (nh, R), 1) // nc
        per_head = _dot((hr == hc).astype(F32), dz, precision=HIGHEST)
        db_ref[...] = jnp.sum(per_head, axis=1, keepdims=True)

    return pl.pallas_call(
        body, name="fgate_bwd",
        out_shape=(jax.ShapeDtypeStruct((R, LANES), F32), jax.ShapeDtypeStruct((nh, 1), F32)),
        compiler_params=_params())(dfk_neg_rows, dfq_rows, z_rows)


_NEG = -1e30
SKIP_BELOW = -106.0
_SCALE = HEAD_DIM ** -0.5
N_SCAN = ATT_BK // SCAN_BK
F_PARTS = 3
Q_F_LANE = HEAD_DIM
Q_ONE_LANE = HEAD_DIM + F_PARTS


def _kv_slice(j):
    return pl.ds(pl.multiple_of(j * ATT_BK, ATT_BK), ATT_BK)


def _mask_t(strict):
    s = lax.broadcasted_iota(jnp.int32, (ATT_BK, ATT_BQ), 0)
    t = lax.broadcasted_iota(jnp.int32, (ATT_BK, ATT_BQ), 1)
    return (s < t) if strict else (s <= t)


def _walk_down(i, step, alive, carry):
    carry = step(i, carry, True)

    def cond(st):
        n, go, _ = st
        return jnp.logical_and(n < i, go)

    def body(st):
        n, _, cr = st
        j = i - 1 - n
        cr = step(j, cr, False)
        return n + 1, alive(jnp.maximum(j - 1, 0), cr), cr

    return lax.while_loop(cond, body, (jnp.int32(0), alive(jnp.maximum(i - 1, 0), carry), carry))[2]


def _t_block(rows):
    return pl.BlockSpec((None, rows, ATT_BQ), lambda h, i, *_: (h, 0, i))


def _t_full(rows, S):
    return pl.BlockSpec((None, rows, S), lambda h, i, *_: (h, 0, 0))


def _n_block():
    return pl.BlockSpec((None, ATT_BQ, LANES), lambda h, i, *_: (h, i, 0))


def _n_full(S):
    return pl.BlockSpec((None, S, LANES), lambda h, i, *_: (h, 0, 0))


def _heads(t):
    S = t.shape[0]
    return jnp.transpose(t.reshape(S, -1, HEAD_DIM), (1, 0, 2))


def _skip_bounds(k_cols, f_rows):
    H, S = f_rows.shape
    f_end = f_rows.reshape(H, S // ATT_BK, ATT_BK)[:, :, -1]
    k_sq = jnp.sum(jnp.square(_heads(k_cols).astype(F32)), axis=-1).reshape(H, S // ATT_BK, ATT_BK)
    k_max = lax.cummax(jnp.sqrt(jnp.max(k_sq, axis=-1)), axis=1)
    return f_end, k_max


def _bf16_parts(f):
    hi = f.astype(BF16).astype(F32)
    mid = (f - hi).astype(BF16).astype(F32)
    return hi, mid, (f - hi - mid).astype(BF16).astype(F32)


def _att_prep(qkv, f_pairs):
    S = qkv.shape[0]
    n_pairs = qkv.shape[1] // (6 * LANES)
    H = 2 * n_pairs
    tm = _tile(S, 512)

    def body(qf_ref, kf_ref, vf_ref, qs_ref, ks_ref, vs_ref, f_ref,
             fqn, fqt, fkn, fkt, fvn, fvt, sqn, sqt, skn, skt, svn, svt):
        lane = lax.broadcasted_iota(jnp.int32, (1, LANES), 1)
        f = f_ref[...]

        def head(ref, e):
            t = ref[...].astype(F32)
            if e == 1:
                t = pltpu.roll(t, HEAD_DIM, 1)
            return jnp.where(lane < HEAD_DIM, t, 0.0)

        def at(first):
            return jnp.logical_and(lane >= first, lane < first + F_PARTS)

        for e in range(2):
            parts = _bf16_parts(f[:, e:e + 1])
            f_lanes = sum(jnp.where(lane == Q_F_LANE + k, parts[k], 0.0) for k in range(F_PARTS))
            nf_lanes = sum(jnp.where(lane == Q_ONE_LANE + k, parts[k], 0.0) for k in range(F_PARTS))
            vals = (
                (fqn, fqt, LANES, head(qf_ref, e) * _SCALE + f_lanes + jnp.where(at(Q_ONE_LANE), 1.0, 0.0)),
                (fkn, fkt, LANES, head(kf_ref, e) + jnp.where(at(Q_F_LANE), 1.0, 0.0) - nf_lanes),
                (fvn, fvt, HEAD_DIM, head(vf_ref, e)),
                (sqn, sqt, LANES, head(qs_ref, e) * _SCALE),
                (skn, skt, LANES, head(ks_ref, e)),
                (svn, svt, HEAD_DIM, head(vs_ref, e)),
            )
            for n_ref, t_ref, rows, val in vals:
                n_ref[e] = val.astype(BF16)
                t_ref[e] = val.T[:rows].astype(BF16)

    col = lambda base: pl.BlockSpec((tm, LANES), lambda i, p: (i, base + p))
    n_spec = pl.BlockSpec((2, tm, LANES), lambda i, p: (p, i, 0))
    t_spec = lambda rows: pl.BlockSpec((2, rows, tm), lambda i, p: (p, 0, i))
    n_sds = jax.ShapeDtypeStruct((H, S, LANES), BF16)
    t_sds = lambda rows: jax.ShapeDtypeStruct((H, rows, S), BF16)
    group = ([n_sds, t_sds(LANES), n_sds, t_sds(LANES), n_sds, t_sds(HEAD_DIM)],
             [n_spec, t_spec(LANES), n_spec, t_spec(LANES), n_spec, t_spec(HEAD_DIM)])
    res = pl.pallas_call(
        body, name="att_prep", grid=(S // tm, n_pairs),
        out_shape=tuple(group[0] * 2),
        in_specs=[col(k * n_pairs) for k in range(6)] + [pl.BlockSpec((None, tm, 2), lambda i, p: (p, i, 0))],
        out_specs=tuple(group[1] * 2),
        compiler_params=_params(dimension_semantics=("parallel", "parallel")),
    )(qkv, qkv, qkv, qkv, qkv, qkv, f_pairs)
    names = ("q_n", "q_t", "k_n", "k_t", "v_n", "v_t")
    return dict(zip(names, res[:6])), dict(zip(names, res[6:]))


def _fox_reach(qt, fend_ref, kmax_ref, h):
    qf = qt.astype(F32)
    q_norm = jnp.sqrt(jnp.sum(jnp.square(qf[:HEAD_DIM]), axis=0, keepdims=True))
    f_t = jnp.sum(qf[Q_F_LANE:Q_F_LANE + F_PARTS], axis=0, keepdims=True)
    return lambda j: q_norm * kmax_ref[h, j] + f_t - fend_ref[h, j]


def _fox_fwd(q_t, k_n, v_t, f_end, k_max, shards):
    H, _, S = q_t.shape
    n, nq = len(shards), S // ATT_BQ

    def body(fend_ref, kmax_ref, qt_ref, k_ref, vt_ref, *rest):
        ins, (ot_ref, lse_ref), outs, sems = rest[:n], rest[n:n + 2], rest[n + 2:2 * n + 2], rest[2 * n + 2:]
        h, i = pl.program_id(0), pl.program_id(1)

        @pl.when(jnp.logical_and(h == 0, i == 0))
        def _():
            for cp in _exchange_copies(ins, outs, *sems, False):
                cp.start()

        qt = qt_ref[...]
        reach = _fox_reach(qt, fend_ref, kmax_ref, h)

        def step(j, carry, masked):
            m, l, acc = carry
            ks = _kv_slice(j)
            s = _dot(k_ref[ks, :], qt)
            if masked:
                s = jnp.where(_mask_t(False), s, _NEG)
            mn = jnp.maximum(m, jnp.max(s, axis=0, keepdims=True))
            alpha = jnp.exp(m - mn)
            p = jnp.exp(s - mn)
            l = alpha * l + jnp.sum(p, axis=0, keepdims=True)
            acc = acc * alpha + _dot(vt_ref[:, ks], p.astype(BF16))
            return mn, l, acc

        def alive(j, carry):
            return jnp.max(reach(j) - carry[0]) > SKIP_BELOW

        row = jnp.zeros((1, ATT_BQ), F32)
        m, l, acc = _walk_down(i, step, alive, (row + _NEG, row, jnp.zeros((HEAD_DIM, ATT_BQ), F32)))
        ot_ref[...] = acc / l
        lse_ref[...] = m + jnp.log(l)

        @pl.when(jnp.logical_and(h == H - 1, i == nq - 1))
        def _():
            for cp in _exchange_copies(ins, outs, *sems, False):
                cp.wait()

    any_spec = pl.BlockSpec(memory_space=pl.ANY)
    grid_spec = pltpu.PrefetchScalarGridSpec(
        num_scalar_prefetch=2, grid=(H, nq),
        in_specs=[_t_block(LANES), _n_full(S), _t_full(HEAD_DIM, S)] + [any_spec] * n,
        out_specs=tuple([_t_block(HEAD_DIM), _t_block(1)] + [any_spec] * n),
        scratch_shapes=_exchange_sems(n))
    res = pl.pallas_call(
        body, name="fox_fwd", grid_spec=grid_spec,
        out_shape=tuple([jax.ShapeDtypeStruct((H, HEAD_DIM, S), F32), jax.ShapeDtypeStruct((H, 1, S), F32)]
                        + _exchange_out_shapes(shards, False)),
        compiler_params=_params(dimension_semantics=("arbitrary", "arbitrary"), has_side_effects=True),
    )(f_end, k_max, q_t, k_n, v_t, *shards)
    return res[0], res[1], res[2:]


def _fox_bwd(q_t, q_n, k_n, k_t, v_n, do_t, do_n, o_t, lse, f_end, k_max):
    H, _, S = q_t.shape

    def body(fend_ref, kmax_ref, qt_ref, qn_ref, k_ref, kt_ref, v_ref, dot_ref, don_ref, ot_ref, lse_ref,
             dqt_ref, dk_ref, dv_ref):
        h, i = pl.program_id(0), pl.program_id(1)

        @pl.when(i == 0)
        def _():
            dk_ref[...] = jnp.zeros_like(dk_ref)
            dv_ref[...] = jnp.zeros_like(dv_ref)

        qt, qn, dot, don = qt_ref[...], qn_ref[...], dot_ref[...], don_ref[...]
        lse = lse_ref[...]
        delta = jnp.sum(dot[:HEAD_DIM].astype(F32) * ot_ref[...], axis=0, keepdims=True)
        reach = _fox_reach(qt, fend_ref, kmax_ref, h)

        def alive(j, dq):
            return jnp.max(reach(j) - lse) > SKIP_BELOW

        def step(j, dq, masked):
            ks = _kv_slice(j)
            s = _dot(k_ref[ks, :], qt)
            if masked:
                s = jnp.where(_mask_t(False), s, _NEG)
            p = jnp.exp(s - lse)
            ds = (p * (_dot(v_ref[ks, :], dot) - delta)).astype(BF16)
            dk_ref[ks, :] += _dot(ds, qn)
            dv_ref[ks, :] += _dot(p.astype(BF16), don)
            return dq + _dot(kt_ref[:, ks], ds)

        dqt_ref[...] = _walk_down(i, step, alive, jnp.zeros((LANES, ATT_BQ), F32))

    grid_spec = pltpu.PrefetchScalarGridSpec(
        num_scalar_prefetch=2, grid=(H, S // ATT_BQ),
        in_specs=[_t_block(LANES), _n_block(), _n_full(S), _t_full(LANES, S), _n_full(S),
                  _t_block(LANES), _n_block(), _t_block(HEAD_DIM), _t_block(1)],
        out_specs=(_t_block(LANES), _n_full(S), _n_full(S)))
    return pl.pallas_call(
        body, name="fox_bwd", grid_spec=grid_spec,
        out_shape=(jax.ShapeDtypeStruct((H, LANES, S), F32), jax.ShapeDtypeStruct((H, S, LANES), F32),
                   jax.ShapeDtypeStruct((H, S, LANES), F32)),
        compiler_params=_params(dimension_semantics=("parallel", "arbitrary")),
    )(f_end, k_max, q_t, q_n, k_n, k_t, v_n, do_t, do_n, o_t, lse)


def _scan_lhs():
    r = lax.broadcasted_iota(jnp.int32, (SCAN_BK, 2 * SCAN_BK), 0)
    c = lax.broadcasted_iota(jnp.int32, (SCAN_BK, 2 * SCAN_BK), 1) % SCAN_BK
    return (c >= r).astype(BF16)


def _suffix_sum(t, lhs):
    hi = t.astype(BF16)
    lo = (t - hi.astype(F32)).astype(BF16)
    return _dot(lhs, jnp.concatenate([hi, lo], axis=0))


def _sb_scores(k, qt, mask):
    z = _dot(k, qt)
    e = jnp.exp(-jnp.abs(z))
    lb = -(jnp.maximum(z, 0.0) + jnp.log(1.0 + e))
    if mask is not None:
        lb = jnp.where(mask, lb, 0.0)
    return z, e, lb


def _scan_blocks():
    return [slice(u * SCAN_BK, (u + 1) * SCAN_BK) for u in reversed(range(N_SCAN))]


def _sb_fwd(q_t, k_n, v_t):
    H, _, S = q_t.shape

    def body(qt_ref, k_ref, vt_ref, ot_ref):
        i = pl.program_id(1)
        qt = qt_ref[...]
        lhs = _scan_lhs()

        def step(j, carry, masked):
            c, acc = carry
            ks = _kv_slice(j)
            mask = _mask_t(True) if masked else None
            z, _, lb = _sb_scores(k_ref[ks, :], qt, mask)
            parts = []
            for sl in _scan_blocks():
                rin = _suffix_sum(lb[sl], lhs)
                a = jnp.exp(z[sl] + rin + c)
                if masked:
                    a = jnp.where(mask[sl], a, 0.0)
                parts.append(a.astype(BF16))
                c = c + rin[0:1, :]
            a_all = jnp.concatenate(parts[::-1], axis=0)
            return c, acc + _dot(vt_ref[:, ks], a_all)

        carry = (jnp.zeros((1, ATT_BQ), F32), jnp.zeros((HEAD_DIM, ATT_BQ), F32))
        ot_ref[...] = _walk_down(i, step, lambda j, cr: jnp.max(cr[0]) > SKIP_BELOW, carry)[1]

    return pl.pallas_call(
        body, name="sb_fwd", grid=(H, S // ATT_BQ),
        out_shape=jax.ShapeDtypeStruct((H, HEAD_DIM, S), F32),
        in_specs=[_t_block(LANES), _n_full(S), _t_full(HEAD_DIM, S)],
        out_specs=_t_block(HEAD_DIM),
        compiler_params=_params(dimension_semantics=("parallel", "parallel")),
    )(q_t, k_n, v_t)


def _sb_bwd(q_t, q_n, k_n, k_t, v_n, do_t, do_n, o_t, bound):
    H, _, S = q_t.shape
    n, nq = len(bound), S // ATT_BQ

    def body(qt_ref, qn_ref, k_ref, kt_ref, v_ref, dot_ref, don_ref, ot_ref, *rest):
        ins, (dqt_ref, dk_ref, dv_ref) = rest[:n], rest[n:n + 3]
        outs, sems = rest[n + 3:2 * n + 3], rest[2 * n + 3:]
        h, i = pl.program_id(0), pl.program_id(1)

        @pl.when(jnp.logical_and(h == 0, i == 0))
        def _():
            for cp in _exchange_copies(ins, outs, *sems, True):
                cp.start()

        @pl.when(i == 0)
        def _():
            dk_ref[...] = jnp.zeros_like(dk_ref)
            dv_ref[...] = jnp.zeros_like(dv_ref)

        qt, qn, dot, don = qt_ref[...], qn_ref[...], dot_ref[...], don_ref[...]
        lhs = _scan_lhs()
        delta = jnp.sum(dot[:HEAD_DIM].astype(F32) * ot_ref[...], axis=0, keepdims=True)

        def step(j, carry, masked):
            c, g, dq = carry
            ks = _kv_slice(j)
            mask = _mask_t(True) if masked else None
            z, e, lb = _sb_scores(k_ref[ks, :], qt, mask)
            da = _dot(v_ref[ks, :], dot)
            a_parts, dz_parts = [], []
            for sl in _scan_blocks():
                rin = _suffix_sum(lb[sl], lhs)
                a = jnp.exp(z[sl] + rin + c)
                if masked:
                    a = jnp.where(mask[sl], a, 0.0)
                ab = a.astype(BF16)
                gg = ab.astype(F32) * da[sl]
                rgin = _suffix_sum(gg, lhs)
                rinv = 1.0 / (1.0 + e[sl])
                sig = jnp.where(z[sl] >= 0.0, rinv, e[sl] * rinv)
                dz = gg - sig * (delta - g - (rgin - gg))
                if masked:
                    dz = jnp.where(mask[sl], dz, 0.0)
                a_parts.append(ab)
                dz_parts.append(dz.astype(BF16))
                c = c + rin[0:1, :]
                g = g + rgin[0:1, :]
            ab_all = jnp.concatenate(a_parts[::-1], axis=0)
            dzb = jnp.concatenate(dz_parts[::-1], axis=0)
            dk_ref[ks, :] += _dot(dzb, qn)
            dv_ref[ks, :] += _dot(ab_all, don)
            return c, g, dq + _dot(kt_ref[:, ks], dzb)

        row = jnp.zeros((1, ATT_BQ), F32)
        carry = (row, row, jnp.zeros((LANES, ATT_BQ), F32))
        dqt_ref[...] = _walk_down(i, step, lambda j, cr: jnp.max(cr[0]) > SKIP_BELOW, carry)[2]

        @pl.when(jnp.logical_and(h == H - 1, i == nq - 1))
        def _():
            for cp in _exchange_copies(ins, outs, *sems, True):
                cp.wait()

    any_spec = pl.BlockSpec(memory_space=pl.ANY)
    res = pl.pallas_call(
        body, name="sb_bwd", grid=(H, nq),
        out_shape=tuple([jax.ShapeDtypeStruct((H, LANES, S), F32), jax.ShapeDtypeStruct((H, S, LANES), F32),
                         jax.ShapeDtypeStruct((H, S, LANES), F32)] + _exchange_out_shapes(bound, True)),
        in_specs=[_t_block(LANES), _n_block(), _n_full(S), _t_full(LANES, S), _n_full(S),
                  _t_block(LANES), _n_block(), _t_block(HEAD_DIM)] + [any_spec] * n,
        out_specs=tuple([_t_block(LANES), _n_full(S), _n_full(S)] + [any_spec] * n),
        scratch_shapes=_exchange_sems(n),
        compiler_params=_params(dimension_semantics=("arbitrary", "arbitrary"), has_side_effects=True),
    )(q_t, q_n, k_n, k_t, v_n, do_t, do_n, o_t, *bound)
    return res[0], res[1], res[2], res[3:]


def _dqkv_assemble(dqf_t, dkf, dvf, dqs_t, dks, dvs):
    H, _, S = dqf_t.shape
    n_pairs = H // 2
    tm = _tile(S, 512)

    def body(dqf_ref, dkf_ref, dvf_ref, dqs_ref, dks_ref, dvs_ref, out_ref, dfk_ref):
        lane = lax.broadcasted_iota(jnp.int32, (1, LANES), 1)
        slabs = ((dqf_ref, True), (dkf_ref, False), (dvf_ref, False),
                 (dqs_ref, True), (dks_ref, False), (dvs_ref, False))
        for k, (ref, transposed) in enumerate(slabs):
            if transposed:
                t0, t1 = ref[0].T * _SCALE, ref[1].T * _SCALE
            else:
                t0, t1 = ref[0], ref[1]
            out_ref[k] = jnp.where(lane < HEAD_DIM, t0, pltpu.roll(t1, HEAD_DIM, 1)).astype(BF16)
        for e in range(2):
            dfk_ref[e] = dkf_ref[e].T[Q_ONE_LANE:Q_ONE_LANE + 1, :]

    t_spec = pl.BlockSpec((2, LANES, tm), lambda i, p: (p, 0, i))
    n_spec = pl.BlockSpec((2, tm, LANES), lambda i, p: (p, i, 0))
    return pl.pallas_call(
        body, name="dqkv_assemble", grid=(S // tm, n_pairs),
        out_shape=(jax.ShapeDtypeStruct((6, S, n_pairs * LANES), BF16), jax.ShapeDtypeStruct((H, 1, S), F32)),
        in_specs=[t_spec, n_spec, n_spec, t_spec, n_spec, n_spec],
        out_specs=(pl.BlockSpec((6, tm, LANES), lambda i, p: (0, i, p)),
                   pl.BlockSpec((2, 1, tm), lambda i, p: (p, 0, i))),
        compiler_params=_params(dimension_semantics=("parallel", "parallel")),
    )(dqf_t, dkf, dvf, dqs_t, dks, dvs)


def _acc_spec(w):
    return pl.BlockSpec((1, w), lambda i: (0, 0))


def _loss_head(x1, act, w_down, gate_m, g_final, target):
    S, D = x1.shape
    F = act.shape[1]
    tm = _tile(S, 256)

    def body(x1_ref, act_ref, w_ref, gt_ref, gf_ref, tg_ref, dx2_ref, gm_ref, loss_ref, dgf_ref, dgt_ref):
        @pl.when(pl.program_id(0) == 0)
        def _():
            loss_ref[...] = jnp.zeros_like(loss_ref)
            dgf_ref[...] = jnp.zeros_like(dgf_ref)
            dgt_ref[...] = jnp.zeros_like(dgt_ref)

        mo = _dot(act_ref[...], w_ref[...])
        x2 = x1_ref[...] + gt_ref[...] * mo
        r = lax.rsqrt(jnp.mean(x2 * x2, axis=-1, keepdims=True) + EPS)
        xh = x2 * r
        diff = xh * gf_ref[...] - tg_ref[...]
        loss_ref[...] += (0.5 / D) * jnp.sum(diff * diff)
        dy = diff * (1.0 / D)
        dgf_ref[...] += jnp.sum(dy * xh, axis=0, keepdims=True)
        dxh = dy * gf_ref[...]
        dx2 = r * (dxh - xh * jnp.mean(dxh * xh, axis=-1, keepdims=True))
        dx2_ref[...] = dx2
        gm_ref[...] = (dx2 * gt_ref[...]).astype(BF16)
        dgt_ref[...] += jnp.sum(dx2 * mo, axis=0, keepdims=True)

    return pl.pallas_call(
        body, name="loss_head", grid=(S // tm,),
        out_shape=(jax.ShapeDtypeStruct((S, D), F32), jax.ShapeDtypeStruct((S, D), BF16),
                   jax.ShapeDtypeStruct((1, LANES), F32), jax.ShapeDtypeStruct((1, D), F32),
                   jax.ShapeDtypeStruct((1, D), F32)),
        in_specs=_row_specs(tm, [D, F]) + [pl.BlockSpec((F, D), lambda i: (0, 0))] + [_vec_spec(D)] * 2
        + _row_specs(tm, [D]),
        out_specs=tuple(_row_specs(tm, [D, D]) + [_acc_spec(LANES), _acc_spec(D), _acc_spec(D)]),
        compiler_params=_params(dimension_semantics=("arbitrary",)),
    )(x1, act, w_down, gate_m, g_final, target)


def _norm_bwd(lhs, rhs, xin, dres, g, scale, name, gate=None, branch=None, bound=(), conv=None):
    S, D = xin.shape
    tm = _tile(S, 256)
    gated = gate is not None
    nl, nb, n_steps = len(lhs), len(bound), S // tm
    nc_ = nl if conv else 0
    n_out = (6 if gated else 4) + nc_

    def body(*refs):
        l_refs, r_refs, rest = refs[:nl], refs[nl:2 * nl], refs[2 * nl:]
        halo_refs, cw_refs, rest = rest[:nc_], rest[nc_:2 * nc_], rest[2 * nc_:]
        if gated:
            x_ref, dr_ref, g_ref, sc_ref, gt_ref, br_ref = rest[:6]
            rest = rest[6:]
        else:
            x_ref, dr_ref, g_ref, sc_ref = rest[:4]
            rest = rest[4:]
        ins, outs, ex_outs, sems = rest[:nb], rest[nb:nb + n_out], rest[nb + n_out:2 * nb + n_out], rest[2 * nb + n_out:]
        dx_ref, dsc_ref, dsh_ref, dg_ref = outs[:4]
        sums = (dsc_ref, dsh_ref, dg_ref) + ((outs[5],) if gated else ())
        dup_refs = outs[n_out - nc_:]
        i = pl.program_id(0)

        @pl.when(i == 0)
        def _():
            for s_ref in sums:
                s_ref[...] = jnp.zeros_like(s_ref)
            if nb:
                for cp in _chip_exchange_copies(ins, ex_outs, *sems):
                    cp.start()

        dhv = None
        for k, (l_ref, r_ref) in enumerate(zip(l_refs, r_refs)):
            if conv:
                halo = jnp.where(i == n_steps - 1, 0.0, halo_refs[k][...].astype(F32))
                dup = _conv_bwd_taps(l_ref[...].astype(F32), halo, cw_refs[k]).astype(BF16)
                dup_refs[k][...] = dup
                terms = [_dot_nt(dup, r_ref[...])]
            elif len(l_ref.shape) == 3:
                K = l_ref.shape[2]
                terms = [_dot_nt(l_ref[k], r_ref[:, k * K:(k + 1) * K]) for k in range(l_ref.shape[0])]
            else:
                terms = [_dot_nt(l_ref[...], r_ref[...])]
            for t in terms:
                dhv = t if dhv is None else dhv + t
        xv = x_ref[...]
        r = lax.rsqrt(jnp.mean(xv * xv, axis=-1, keepdims=True) + EPS)
        xh = xv * r
        dsc_ref[...] += jnp.sum(dhv * (xh * g_ref[...]), axis=0, keepdims=True)
        dsh_ref[...] += jnp.sum(dhv, axis=0, keepdims=True)
        dn = dhv * (1.0 + sc_ref[...])
        dg_ref[...] += jnp.sum(dn * xh, axis=0, keepdims=True)
        dxh = dn * g_ref[...]
        dx = dr_ref[...] + r * (dxh - xh * jnp.mean(dxh * xh, axis=-1, keepdims=True))
        dx_ref[...] = dx
        if gated:
            outs[4][...] = (dx * gt_ref[...]).astype(BF16)
            outs[5][...] += jnp.sum(dx * br_ref[...], axis=0, keepdims=True)

        if nb:
            @pl.when(i == n_steps - 1)
            def _():
                for cp in _chip_exchange_copies(ins, ex_outs, *sems):
                    cp.wait()

    def l_spec(a):
        if a.ndim == 3:
            return pl.BlockSpec((a.shape[0], tm, a.shape[2]), lambda i: (0, i, 0))
        return pl.BlockSpec((tm, a.shape[1]), lambda i: (i, 0))

    any_spec = pl.BlockSpec(memory_space=pl.ANY)
    vec = jax.ShapeDtypeStruct((1, D), F32)
    out_shape = [jax.ShapeDtypeStruct((S, D), F32), vec, vec, vec]
    out_specs = _row_specs(tm, [D]) + [_acc_spec(D)] * 3
    in_specs = [l_spec(a) for a in lhs] + [pl.BlockSpec(b.shape, lambda i: (0, 0)) for b in rhs]
    args = list(lhs) + list(rhs)
    if conv:
        step, last_halo = tm // HALO, S // HALO - 1
        in_specs += [pl.BlockSpec((HALO, a.shape[1]), lambda i: (jnp.minimum((i + 1) * step, last_halo), 0))
                     for a in lhs]
        in_specs += [pl.BlockSpec(w.shape, lambda i: (0, 0)) for w in conv]
        args += list(lhs) + list(conv)
    in_specs += _row_specs(tm, [D, D]) + [_vec_spec(D)] * 2
    args += [xin, dres, g, scale]
    if gated:
        out_shape += [jax.ShapeDtypeStruct((S, D), BF16), vec]
        out_specs += _row_specs(tm, [D]) + [_acc_spec(D)]
        in_specs += [_vec_spec(D)] + _row_specs(tm, [D])
        args += [gate, branch]
    if conv:
        out_shape += [jax.ShapeDtypeStruct(a.shape, BF16) for a in lhs]
        out_specs += [l_spec(a) for a in lhs]
    res = pl.pallas_call(
        body, name=name, grid=(n_steps,),
        out_shape=tuple(out_shape + [jax.ShapeDtypeStruct(b.shape, b.dtype) for b in bound]),
        in_specs=in_specs + [any_spec] * nb, out_specs=tuple(out_specs + [any_spec] * nb),
        scratch_shapes=_chip_exchange_sems(nb) if nb else [],
        compiler_params=_params(dimension_semantics=("arbitrary",), has_side_effects=bool(nb)),
    )(*args, *bound)
    return tuple(res[:n_out]) + (tuple(res[n_out:]),)


def _headnorm_bwd(dmix, o_f, o_s, g_f, g_s):
    dh, S = o_f.shape
    H = dh // HEAD_DIM
    tm = _tile(S, 256)

    def body(dm_ref, of_ref, os_ref, gf_ref, gs_ref, fn_ref, ft_ref, sn_ref, st_ref, dgf_ref, dgs_ref):
        @pl.when(pl.program_id(0) == 0)
        def _():
            dgf_ref[...] = jnp.zeros_like(dgf_ref)
            dgs_ref[...] = jnp.zeros_like(dgs_ref)

        ones = _group_ones()
        lane = lax.broadcasted_iota(jnp.int32, (1, LANES), 1)
        parts = ((of_ref, gf_ref, fn_ref, ft_ref, dgf_ref), (os_ref, gs_ref, sn_ref, st_ref, dgs_ref))
        for part, (o_ref, g_ref, n_ref, t_ref, dg_ref) in enumerate(parts):
            for t in range(dh // LANES):
                cols = slice(t * LANES, (t + 1) * LANES)
                o = o_ref[cols, :].T
                dm = dm_ref[:, part * dh + t * LANES: part * dh + (t + 1) * LANES]
                r = lax.rsqrt(_dot(o * o, ones, precision=HIGHEST) * (1.0 / HEAD_DIM) + EPS)
                oh = o * r
                dg_ref[:, cols] += jnp.sum(dm * oh, axis=0, keepdims=True)
                dn = dm * g_ref[:, cols]
                mean = _dot(dn * oh, ones, precision=HIGHEST) * (1.0 / HEAD_DIM)
                do = r * (dn - oh * mean)
                for e in range(2):
                    d = do if e == 0 else pltpu.roll(do, HEAD_DIM, 1)
                    d = jnp.where(lane < HEAD_DIM, d, 0.0)
                    n_ref[2 * t + e] = d.astype(BF16)
                    t_ref[2 * t + e] = d.T.astype(BF16)

    vec = jax.ShapeDtypeStruct((1, dh), F32)
    n_sds = jax.ShapeDtypeStruct((H, S, LANES), BF16)
    t_sds = jax.ShapeDtypeStruct((H, LANES, S), BF16)
    n_spec = pl.BlockSpec((H, tm, LANES), lambda i: (0, i, 0))
    t_spec = pl.BlockSpec((H, LANES, tm), lambda i: (0, 0, i))
    return pl.pallas_call(
        body, name="headnorm_bwd", grid=(S // tm,),
        out_shape=(n_sds, t_sds, n_sds, t_sds, vec, vec),
        in_specs=_row_specs(tm, [2 * dh]) + [_col_spec(tm, dh)] * 2 + [_vec_spec(dh)] * 2,
        out_specs=(n_spec, t_spec, n_spec, t_spec, _acc_spec(dh), _acc_spec(dh)),
        compiler_params=_params(dimension_semantics=("arbitrary",)),
    )(dmix, o_f, o_s, g_f, g_s)


def _adamw(w, gslots, m, v, name):
    R, C = w.shape
    n = gslots.shape[0]
    tr = 256 if (R % 256 == 0 and R > 256) else R
    bc1 = 1.0 - ADAM_B1 ** ADAM_STEP
    bc2 = 1.0 - ADAM_B2 ** ADAM_STEP

    def body(w_ref, gs_ref, m_ref, v_ref, g_ref, d_ref, nm_ref, nv_ref):
        g = gs_ref[0]
        for s in range(1, n):
            g = g + gs_ref[s]
        nm = ADAM_B1 * m_ref[...] + (1.0 - ADAM_B1) * g
        nv = ADAM_B2 * v_ref[...] + (1.0 - ADAM_B2) * (g * g)
        g_ref[...] = g
        nm_ref[...] = nm
        nv_ref[...] = nv
        d_ref[...] = -ADAM_LR * ((nm / bc1) / (jnp.sqrt(nv / bc2) + ADAM_EPS) + ADAM_WD * w_ref[...])

    blk = pl.BlockSpec((tr, C), lambda i: (i, 0))
    sds = jax.ShapeDtypeStruct((R, C), F32)
    return pl.pallas_call(
        body, name=name, grid=(R // tr,), out_shape=(sds,) * 4,
        in_specs=[blk, pl.BlockSpec((n, tr, C), lambda i: (0, i, 0)), blk, blk], out_specs=(blk,) * 4,
        compiler_params=_params(dimension_semantics=("parallel",)),
    )(w, gslots, m, v)


def _slot_sum(slots, name):
    n, _, C = slots.shape

    def body(s_ref, o_ref):
        acc = s_ref[0]
        for s in range(1, n):
            acc = acc + s_ref[s]
        o_ref[...] = acc

    return pl.pallas_call(body, name=name, out_shape=jax.ShapeDtypeStruct((1, C), F32),
                          compiler_params=_params())(slots)


def _pad_cols(a, n):
    return jnp.pad(a, ((0, 0), (0, n - a.shape[1])))


def _ungather(g, axis):
    if axis == 0:
        return g.reshape(g.shape[0] * g.shape[1], g.shape[2])
    return jnp.transpose(g, (1, 0, 2)).reshape(g.shape[1], g.shape[0] * g.shape[2])


def _to_slots(full, axis):
    R, C = full.shape
    if axis == 0:
        return full.reshape(N_DEV, R // N_DEV, C)
    return jnp.transpose(full.reshape(R, N_DEV, C // N_DEV), (1, 0, 2))


def kernel(x, c, w_ada, b_ada, g_attn, w_in, b_fgate, g_out_fox, g_out_sb, w_out, g_mlp, w_up, conv_w, conv_b, w_down, g_final, loss_target, m_w_ada, m_b_ada, m_g_attn, m_w_in, m_b_fgate, m_g_out_fox, m_g_out_sb, m_w_out, m_g_mlp, m_w_up, m_conv_w, m_conv_b, m_w_down, m_g_final, v_w_ada, v_b_ada, v_g_attn, v_w_in, v_b_fgate, v_g_out_fox, v_g_out_sb, v_w_out, v_g_mlp, v_w_up, v_conv_w, v_conv_b, v_w_down, v_g_final):
    S, D = x.shape[1], x.shape[2]
    dh = D // 2
    n_heads = dh // HEAD_DIM
    n_qkv = 6 * dh
    ff = w_down.shape[1] * N_DEV
    ffp = -(-ff // (2 * LANES)) * (2 * LANES)
    nc = S // LANES
    me = 4 * lax.axis_index("x") + 2 * lax.axis_index("y") + lax.axis_index("c")
    xs, tgt = x[0], loss_target[0]

    c_all, win_g = _gather_two_level([c, w_in[0].astype(BF16)], name="gather_first")
    c_all = c_all.reshape(N_DEV, D)
    W_in = _ungather(win_g, 1)
    W_qkv, W_f = W_in[:, :n_qkv], _pad_cols(W_in[:, n_qkv:], LANES)
    cb_g, cb_v = _pad_cols(conv_b[:, :ff], ffp), _pad_cols(conv_b[:, ff:], ffp)

    n_ada = w_ada.shape[2]
    b_shard = lax.dynamic_slice(b_ada, (0, me * n_ada), (1, n_ada))
    mod_cols = _ada_fwd(c_all, w_ada[0], b_shard)
    (mod_g,) = _exchange([mod_cols], scatter=False, name="gather_mod")
    mod = lax.dynamic_index_in_dim(mod_g, me, axis=1, keepdims=False).reshape(6, 1, D)
    shift_a, scale_a, gate_a, shift_m, scale_m, gate_m = [mod[k] for k in range(6)]

    h1, h1_t = _prenorm(xs, g_attn, scale_a, shift_a, "prenorm_attn")
    qkv = _mm(h1, W_qkv, BF16, "proj_qkv")
    flog = _mm(h1, W_f, F32, "proj_fgate")
    zf = flog[:, :n_heads] + b_fgate
    z_rows = zf.T.reshape(n_heads * nc, LANES)
    f_rows = _fgate_fwd(z_rows, nc).reshape(n_heads, S)
    f_pairs = jnp.transpose(f_rows.reshape(n_heads // 2, 2, S), (0, 2, 1))
    fox, sb = _att_prep(qkv, f_pairs)
    f_end, k_max = _skip_bounds(qkv[:, dh:2 * dh], f_rows)
    of_t, lse, (wout_g, wup_g, wdown_g, convw_g) = _fox_fwd(
        fox["q_t"], fox["k_n"], fox["v_t"], f_end, k_max,
        [w_out[0].astype(BF16), w_up[0].astype(BF16), w_down[0].astype(BF16), conv_w[0]])
    W_out = _ungather(wout_g, 0)
    W_up = _ungather(wup_g, 1)
    W_g, W_v = _pad_cols(W_up[:, :ff], ffp), _pad_cols(W_up[:, ff:], ffp)
    W_down = jnp.pad(_ungather(wdown_g, 0), ((0, ffp - ff), (0, 0)))
    cw_full = _ungather(convw_g, 1)
    cw_g, cw_v = _pad_cols(cw_full[:, :ff], ffp), _pad_cols(cw_full[:, ff:], ffp)
    os_t = _sb_fwd(sb["q_t"], sb["k_n"], sb["v_t"])
    o_f, o_s = of_t.reshape(dh, S), os_t.reshape(dh, S)
    mix, mix_t = _headnorm_fwd(o_f, o_s, g_out_fox, g_out_sb)
    a_out = _mm(mix, W_out, F32, "proj_out")
    x1, h2, h2_t = _resid_prenorm(xs, a_out, gate_a, g_mlp, scale_m, shift_m)
    up_g, up_v, act, act_t = _mlp_up(h2, W_g, W_v, cw_g, cw_v, cb_g, cb_v)

    dx2, gm, loss_p, dg_final, dgate_m = _loss_head(x1, act, W_down, gate_m, g_final.reshape(1, D), tgt)
    dW_down = _mm_acc(act_t, gm, "bwd_down_w")
    du_g, du_v, p_g, p_v = _conv_act_bwd(gm, W_down, up_g, up_v, cw_g, cw_v, cb_g, cb_v)
    dx1, dscale_m, dshift_m, dg_mlp, ga, dgate_a, dup_g, dup_v, _ = _norm_bwd(
        [du_g, du_v], [W_g, W_v], x1, dx2, g_mlp, scale_m, "norm_mlp_bwd", gate=gate_a, branch=a_out,
        conv=[cw_g, cw_v])
    dW_g = _mm_acc(h2_t, dup_g, "bwd_up_w_g")
    dW_v = _mm_acc(h2_t, dup_v, "bwd_up_w_v")
    dmix = _mm(ga, W_out, F32, "bwd_out_act", nt=True)
    dW_out = _mm_acc(mix_t, ga, "bwd_out_w")
    dof_n, dof_t, dos_n, dos_t, dg_fox, dg_sb = _headnorm_bwd(dmix, o_f, o_s, g_out_fox, g_out_sb)
    dqf_t, dkf, dvf = _fox_bwd(fox["q_t"], fox["q_n"], fox["k_n"], fox["k_t"], fox["v_n"], dof_t, dof_n, of_t, lse,
                              f_end, k_max)
    dW_upf = jnp.concatenate([dW_g[:, :ff], dW_v[:, :ff]], axis=1)
    dcw = jnp.concatenate([p_g[:CONV_W, :ff], p_v[:CONV_W, :ff]], axis=1)
    dqs_t, dks, dvs, (s_out, s_up, s_down, s_cw) = _sb_bwd(
        sb["q_t"], sb["q_n"], sb["k_n"], sb["k_t"], sb["v_n"], dos_t, dos_n, os_t,
        [_to_slots(dW_out, 0), _to_slots(dW_upf, 1), _to_slots(dW_down[:ff], 0), _to_slots(dcw, 1)])
    dparts, dfk = _dqkv_assemble(dqf_t, dkf, dvf, dqs_t, dks, dvs)
    dz_rows, db_fgate = _fgate_bwd(dfk.reshape(n_heads * nc, LANES),
                                   dqf_t[:, Q_F_LANE, :].reshape(n_heads * nc, LANES), z_rows, nc)
    dzf = _pad_cols(dz_rows.reshape(n_heads, S).T, LANES).astype(BF16)
    dW_qkv = _mm_acc_parts(h1_t, dparts, "bwd_in_w")
    dW_f = _mm_acc(h1_t, dzf, "bwd_in_w_fgate")
    dW_in = jnp.concatenate([jnp.transpose(dW_qkv, (1, 0, 2)).reshape(D, n_qkv), dW_f[:, :n_heads]], axis=1)
    bound_in = _to_slots(dW_in, 1)
    bound_in = bound_in.reshape((N_CHIPS, 2) + bound_in.shape[1:])
    (got_in,) = _scatter_to_sibling([bound_in], "scatter_sibling")
    c_idx = lax.axis_index("c").astype(jnp.int32).reshape(1)
    grad_x, dscale_a, dshift_a, dg_attn, (s_in,) = _norm_bwd(
        [dparts, dzf], [W_qkv, W_f], xs, dx1, g_attn, scale_a, "norm_attn_bwd",
        bound=[_pair_add(bound_in, got_in, c_idx, "pair_add")])

    dconv_b = jnp.concatenate([p_g[CONV_W:CONV_W + 1, :ff], p_v[CONV_W:CONV_W + 1, :ff]], axis=1)
    parts = [dshift_a, dscale_a, dgate_a, dshift_m, dscale_m, dgate_m,
             dg_attn, db_fgate.reshape(1, n_heads), dg_fox, dg_sb, dg_mlp, dconv_b, dg_final,
             loss_p[:, :1]]
    sizes = [p.shape[1] for p in parts]
    vec = jnp.concatenate(parts, axis=1)
    n_vec = -(-vec.shape[1] // LANES) * LANES
    vec = _pad_cols(vec, n_vec)
    (vec_g,) = _exchange([vec], scatter=False, name="gather_small")
    offs = [0]
    for s in sizes:
        offs.append(offs[-1] + s)

    def small(k0, k1=None):
        k1 = k0 if k1 is None else k1
        return vec_g[:, :, offs[k0]:offs[k1 + 1]]

    dmod_all = small(0, 5).reshape(N_DEV, 6 * D)
    dmod_cols = lax.dynamic_slice(dmod_all, (0, me * n_ada), (N_DEV, n_ada))
    dW_ada = _ada_bwd(c_all.T, dmod_cols)


    res = {}
    res["w_ada"] = _adamw(w_ada[0], dW_ada[None], m_w_ada[0], v_w_ada[0], "adamw_w_ada")
    res["w_in"] = _adamw(w_in[0], s_in, m_w_in[0], v_w_in[0], "adamw_w_in")
    res["w_out"] = _adamw(w_out[0], s_out, m_w_out[0], v_w_out[0], "adamw_w_out")
    res["w_up"] = _adamw(w_up[0], s_up, m_w_up[0], v_w_up[0], "adamw_w_up")
    res["w_down"] = _adamw(w_down[0], s_down, m_w_down[0], v_w_down[0], "adamw_w_down")
    res["conv_w"] = _adamw(conv_w[0], s_cw, m_conv_w[0], v_conv_w[0], "adamw_conv_w")
    small_names = ["b_ada", "g_attn", "b_fgate", "g_out_fox", "g_out_sb", "g_mlp", "conv_b", "g_final"]
    small_w = [b_ada, g_attn, b_fgate, g_out_fox, g_out_sb, g_mlp, conv_b, g_final.reshape(1, D)]
    small_m = [m_b_ada, m_g_attn, m_b_fgate, m_g_out_fox, m_g_out_sb, m_g_mlp, m_conv_b, m_g_final.reshape(1, D)]
    small_v = [v_b_ada, v_g_attn, v_b_fgate, v_g_out_fox, v_g_out_sb, v_g_mlp, v_conv_b, v_g_final.reshape(1, D)]
    small_res = _adamw(jnp.concatenate(small_w, axis=1), small(0, 12), jnp.concatenate(small_m, axis=1),
                       jnp.concatenate(small_v, axis=1), "adamw_small")
    lo = 0
    for nm, wv in zip(small_names, small_w):
        res[nm] = tuple(r[:, lo:lo + wv.shape[1]] for r in small_res)
        lo += wv.shape[1]
    loss = _slot_sum(_pad_cols(small(13).reshape(N_DEV, 1), LANES).reshape(N_DEV, 1, LANES), "loss_sum")[0, 0]

    names = ["w_ada", "b_ada", "g_attn", "w_in", "b_fgate", "g_out_fox", "g_out_sb", "w_out", "g_mlp",
             "w_up", "conv_w", "conv_b", "w_down", "g_final"]

    def shaped(n, a):
        if n == "g_final":
            return a.reshape(D)
        if n in ("b_ada", "g_attn", "b_fgate", "g_out_fox", "g_out_sb", "g_mlp", "conv_b"):
            return a
        return a[None]

    outs = [loss, grad_x[None]]
    for k in range(4):
        outs += [shaped(n, res[n][k]) for n in names]
    return tuple(outs)
```

```python
import jax
import jax.numpy as jnp
from jax import lax
from jax.experimental import pallas as pl
from jax.experimental.pallas import tpu as pltpu

F32 = jnp.float32
BF16 = jnp.bfloat16
HIGHEST = lax.Precision.HIGHEST

N_DEV = 8
LANES = 128
HEAD_DIM = 64
EPS = 1e-6
CONV_W = 3
CONV_COLS = 1408
HALO = 16
ATT_BQ = 512
ATT_BK = 256
SCAN_BK = 128
VMEM_LIMIT = 56 * 1024 * 1024

ADAM_LR = 0.001
ADAM_B1 = 0.9
ADAM_B2 = 0.999
ADAM_EPS = 1e-08
ADAM_WD = 0.01
ADAM_STEP = 10


def _params(**kw):
    return pltpu.CompilerParams(vmem_limit_bytes=VMEM_LIMIT, **kw)


def _tile(n, cap):
    if n <= cap:
        return n
    best = None
    for t in range(LANES, cap + 1, LANES):
        if n % t == 0:
            best = t
    assert best is not None, (n, cap)
    return best


def _dot(a, b, **kw):
    return jnp.dot(a, b, preferred_element_type=F32, **kw)


def _exchange_copies(ins, outs, send_sems, recv_sems, loc_sems, scatter):
    n = len(ins)
    if n == 0:
        return []
    x, y, c = lax.axis_index("x"), lax.axis_index("y"), lax.axis_index("c")
    me = 4 * x + 2 * y + c
    copies = []
    for a in range(n):
        src = ins[a].at[me] if scatter else ins[a]
        copies.append(pltpu.make_async_copy(src, outs[a].at[me], loc_sems.at[a]))
    for k in range(1, N_DEV):
        px = 1 - x if k & 4 else x
        py = 1 - y if k & 2 else y
        pc = 1 - c if k & 1 else c
        peer = 4 * px + 2 * py + pc
        for a in range(n):
            src = ins[a].at[peer] if scatter else ins[a]
            copies.append(pltpu.make_async_remote_copy(
                src_ref=src, dst_ref=outs[a].at[me],
                send_sem=send_sems.at[a, k - 1], recv_sem=recv_sems.at[a, k - 1],
                device_id=(px, py, pc), device_id_type=pl.DeviceIdType.MESH))
    return copies


def _exchange_out_shapes(arrays, scatter):
    return [jax.ShapeDtypeStruct((N_DEV,) + tuple(a.shape[1:] if scatter else a.shape), a.dtype) for a in arrays]


def _exchange_sems(n):
    return [pltpu.SemaphoreType.DMA((n, N_DEV - 1)), pltpu.SemaphoreType.DMA((n, N_DEV - 1)),
            pltpu.SemaphoreType.DMA((n,))]


def _exchange(arrays, scatter, name):
    n = len(arrays)

    def body(*refs):
        copies = _exchange_copies(refs[:n], refs[n:2 * n], *refs[2 * n:], scatter)
        for cp in copies:
            cp.start()
        for cp in copies:
            cp.wait()

    any_spec = pl.BlockSpec(memory_space=pl.ANY)
    return pl.pallas_call(
        body, name=name, out_shape=tuple(_exchange_out_shapes(arrays, scatter)),
        in_specs=[any_spec] * n, out_specs=tuple([any_spec] * n),
        scratch_shapes=_exchange_sems(n),
        compiler_params=pltpu.CompilerParams(has_side_effects=True),
    )(*arrays)


def _gather_two_level(arrays, name):
    n = len(arrays)
    out_shape = [jax.ShapeDtypeStruct((N_DEV,) + tuple(a.shape), a.dtype) for a in arrays]

    def body(*refs):
        ins, outs = refs[:n], refs[n:2 * n]
        send_sems, recv_sems, loc_sems = refs[2 * n:]
        x, y, c = lax.axis_index("x"), lax.axis_index("y"), lax.axis_index("c")
        me, sibling = (x, y, c), (x, y, 1 - c)
        chips = [(1 - x, y), (x, 1 - y), (1 - x, 1 - y)]

        def slot(px, py, pc):
            return 4 * px + 2 * py + pc

        def copy(a, k, block, to, src=None):
            dst = outs[a].at[slot(*block)]
            return pltpu.make_async_remote_copy(
                src_ref=dst if src is None else src, dst_ref=dst,
                send_sem=send_sems.at[a, k], recv_sem=recv_sems.at[a, k],
                device_id=to, device_id_type=pl.DeviceIdType.MESH)

        local = [pltpu.make_async_copy(ins[a], outs[a].at[slot(*me)], loc_sems.at[a]) for a in range(n)]
        for cp in local:
            cp.start()
        first = []
        for a in range(n):
            first.append(copy(a, 0, me, sibling, src=ins[a]))
            first += [copy(a, 1 + j, me, (*chip, c), src=ins[a]) for j, chip in enumerate(chips)]
        for cp in first:
            cp.start()
        passed = []
        for j, chip in enumerate(chips):
            for a in range(n):
                copy(a, 1 + j, (*chip, c), me).wait_recv()
                cp = copy(a, 4 + j, (*chip, c), sibling)
                cp.start()
                passed.append(cp)
        for a in range(n):
            copy(a, 0, sibling, me).wait_recv()
            for j, chip in enumerate(chips):
                copy(a, 4 + j, (*chip, 1 - c), me).wait_recv()
        for cp in first + passed:
            cp.wait_send()
        for cp in local:
            cp.wait()

    any_spec = pl.BlockSpec(memory_space=pl.ANY)
    return pl.pallas_call(
        body, name=name, out_shape=tuple(out_shape),
        in_specs=[any_spec] * n, out_specs=tuple([any_spec] * n),
        scratch_shapes=[pltpu.SemaphoreType.DMA((n, N_DEV - 1)), pltpu.SemaphoreType.DMA((n, N_DEV - 1)),
                        pltpu.SemaphoreType.DMA((n,))],
        compiler_params=pltpu.CompilerParams(has_side_effects=True),
    )(*arrays)


N_CHIPS = 4


def _scatter_to_sibling(arrays, name):
    n = len(arrays)
    out_shape = [jax.ShapeDtypeStruct((N_CHIPS,) + tuple(a.shape[2:]), a.dtype) for a in arrays]

    def body(*refs):
        ins, outs = refs[:n], refs[n:2 * n]
        send_sems, recv_sems = refs[2 * n:]
        x, y, c = lax.axis_index("x"), lax.axis_index("y"), lax.axis_index("c")
        copies = []
        for a in range(n):
            for q in range(N_CHIPS):
                cp = pltpu.make_async_remote_copy(
                    src_ref=ins[a].at[q, 1 - c], dst_ref=outs[a].at[q],
                    send_sem=send_sems.at[a, q], recv_sem=recv_sems.at[a, q],
                    device_id=(x, y, 1 - c), device_id_type=pl.DeviceIdType.MESH)
                cp.start()
                copies.append(cp)
        for cp in copies:
            cp.wait()

    any_spec = pl.BlockSpec(memory_space=pl.ANY)
    return pl.pallas_call(
        body, name=name, out_shape=tuple(out_shape),
        in_specs=[any_spec] * n, out_specs=tuple([any_spec] * n),
        scratch_shapes=[pltpu.SemaphoreType.DMA((n, N_CHIPS)), pltpu.SemaphoreType.DMA((n, N_CHIPS))],
        compiler_params=pltpu.CompilerParams(has_side_effects=True),
    )(*arrays)


def _pair_add(mine, got, c_idx, name):
    _, _, R, C = mine.shape
    tr = 256 if (R % 256 == 0 and R > 256) else R

    def body(c_ref, m_ref, g_ref, o_ref):
        o_ref[...] = m_ref[...] + g_ref[...]

    grid_spec = pltpu.PrefetchScalarGridSpec(
        num_scalar_prefetch=1, grid=(N_CHIPS, R // tr),
        in_specs=[pl.BlockSpec((None, None, tr, C), lambda q, i, c_ref: (q, c_ref[0], i, 0)),
                  pl.BlockSpec((None, tr, C), lambda q, i, c_ref: (q, i, 0))],
        out_specs=pl.BlockSpec((None, tr, C), lambda q, i, c_ref: (q, i, 0)))
    return pl.pallas_call(
        body, name=name, grid_spec=grid_spec, out_shape=jax.ShapeDtypeStruct((N_CHIPS, R, C), mine.dtype),
        compiler_params=_params(dimension_semantics=("parallel", "parallel")),
    )(c_idx, mine, got)


def _chip_exchange_copies(ins, outs, send_sems, recv_sems, loc_sems):
    n = len(ins)
    x, y, c = lax.axis_index("x"), lax.axis_index("y"), lax.axis_index("c")
    myq = 2 * x + y
    copies = [pltpu.make_async_copy(ins[a].at[myq], outs[a].at[myq], loc_sems.at[a]) for a in range(n)]
    for k in range(1, N_CHIPS):
        qx = 1 - x if k & 2 else x
        qy = 1 - y if k & 1 else y
        for a in range(n):
            copies.append(pltpu.make_async_remote_copy(
                src_ref=ins[a].at[2 * qx + qy], dst_ref=outs[a].at[myq],
                send_sem=send_sems.at[a, k - 1], recv_sem=recv_sems.at[a, k - 1],
                device_id=(qx, qy, c), device_id_type=pl.DeviceIdType.MESH))
    return copies


def _chip_exchange_sems(n):
    return [pltpu.SemaphoreType.DMA((n, N_CHIPS - 1)), pltpu.SemaphoreType.DMA((n, N_CHIPS - 1)),
            pltpu.SemaphoreType.DMA((n,))]


def _dot_nt(a, b):
    return lax.dot_general(a, b, (((1,), (1,)), ((), ())), preferred_element_type=F32)


def _rhs_spec(b, tn, nt):
    if nt:
        return pl.BlockSpec((tn, b.shape[1]), lambda i, j: (j, 0))
    return pl.BlockSpec((b.shape[0], tn), lambda i, j: (0, j))


def _mm(a, b, out_dtype, name, tm=1024, tn=512, nt=False):
    M, K = a.shape
    N = b.shape[0] if nt else b.shape[1]
    tm, tn = _tile(M, tm), _tile(N, tn)
    dot = _dot_nt if nt else _dot

    def body(a_ref, b_ref, o_ref):
        o_ref[...] = dot(a_ref[...], b_ref[...]).astype(out_dtype)

    return pl.pallas_call(
        body, name=name, out_shape=jax.ShapeDtypeStruct((M, N), out_dtype),
        grid=(M // tm, N // tn),
        in_specs=[pl.BlockSpec((tm, K), lambda i, j: (i, 0)), _rhs_spec(b, tn, nt)],
        out_specs=pl.BlockSpec((tm, tn), lambda i, j: (i, j)),
        compiler_params=_params(dimension_semantics=("parallel", "parallel")),
    )(a, b)


def _mm_acc(a, b, name, tm=1408, tn=1408, tk=512):
    M, S = a.shape
    _, N = b.shape
    tm, tn, tk = _tile(M, tm), _tile(N, tn), _tile(S, tk)

    def body(a_ref, b_ref, o_ref):
        @pl.when(pl.program_id(2) == 0)
        def _():
            o_ref[...] = jnp.zeros_like(o_ref)

        o_ref[...] += _dot(a_ref[...], b_ref[...])

    return pl.pallas_call(
        body, name=name, out_shape=jax.ShapeDtypeStruct((M, N), F32),
        grid=(M // tm, N // tn, S // tk),
        in_specs=[pl.BlockSpec((tm, tk), lambda i, j, k: (i, k)), pl.BlockSpec((tk, tn), lambda i, j, k: (k, j))],
        out_specs=pl.BlockSpec((tm, tn), lambda i, j, k: (i, j)),
        compiler_params=_params(dimension_semantics=("parallel", "parallel", "arbitrary")),
    )(a, b)


def _mm_acc_parts(a, parts, name, tm=1024, tk=1024):
    M, S = a.shape
    P, _, K = parts.shape
    tm, tk = _tile(M, tm), _tile(S, tk)

    def body(a_ref, b_ref, o_ref):
        @pl.when(pl.program_id(2) == 0)
        def _():
            o_ref[...] = jnp.zeros_like(o_ref)

        o_ref[...] += _dot(a_ref[...], b_ref[...])

    return pl.pallas_call(
        body, name=name, out_shape=jax.ShapeDtypeStruct((P, M, K), F32), grid=(P, M // tm, S // tk),
        in_specs=[pl.BlockSpec((tm, tk), lambda k, i, s: (i, s)), pl.BlockSpec((None, tk, K), lambda k, i, s: (k, s, 0))],
        out_specs=pl.BlockSpec((None, tm, K), lambda k, i, s: (k, i, 0)),
        compiler_params=_params(dimension_semantics=("parallel", "parallel", "arbitrary")),
    )(a, parts)


def _silu(z):
    return z * (1.0 / (1.0 + jnp.exp(-z)))


def _ada_fwd(c_all, w_shard, b_shard):
    n = w_shard.shape[1]

    def body(c_ref, w_ref, b_ref, o_ref):
        o_ref[...] = _dot(_silu(c_ref[...]), w_ref[...], precision=HIGHEST) + b_ref[...]

    return pl.pallas_call(body, name="ada_fwd", out_shape=jax.ShapeDtypeStruct((N_DEV, n), F32),
                          compiler_params=_params())(c_all, w_shard, b_shard)


def _ada_bwd(c_all_t, dmod_cols):
    D = c_all_t.shape[0]
    n = dmod_cols.shape[1]

    def body(ct_ref, dm_ref, o_ref):
        sc = _silu(ct_ref[...])
        dm = dm_ref[...]
        acc = sc[:, 0:1] * dm[0:1, :]
        for b in range(1, N_DEV):
            acc = acc + sc[:, b:b + 1] * dm[b:b + 1, :]
        o_ref[...] = acc

    return pl.pallas_call(body, name="ada_bwd", out_shape=jax.ShapeDtypeStruct((D, n), F32),
                          compiler_params=_params())(c_all_t, dmod_cols)


def _row_specs(tm, widths):
    return [pl.BlockSpec((tm, w), lambda i: (i, 0)) for w in widths]


def _vec_spec(w):
    return pl.BlockSpec((1, w), lambda i: (0, 0))


def _col_spec(tm, w):
    return pl.BlockSpec((w, tm), lambda i: (0, i))


def _prenorm(x, g, scale, shift, name):
    S, D = x.shape
    tm = _tile(S, 512)

    def body(x_ref, g_ref, sc_ref, sh_ref, h_ref, ht_ref):
        xv = x_ref[...]
        r = lax.rsqrt(jnp.mean(xv * xv, axis=-1, keepdims=True) + EPS)
        h = (xv * r) * g_ref[...] * (1.0 + sc_ref[...]) + sh_ref[...]
        h_ref[...] = h.astype(BF16)
        ht_ref[...] = h.T.astype(BF16)

    return pl.pallas_call(
        body, name=name, grid=(S // tm,),
        out_shape=(jax.ShapeDtypeStruct((S, D), BF16), jax.ShapeDtypeStruct((D, S), BF16)),
        in_specs=_row_specs(tm, [D]) + [_vec_spec(D)] * 3,
        out_specs=(_row_specs(tm, [D])[0], _col_spec(tm, D)),
        compiler_params=_params(dimension_semantics=("parallel",)),
    )(x, g, scale, shift)


def _group_ones():
    r = lax.broadcasted_iota(jnp.int32, (LANES, LANES), 0) // HEAD_DIM
    c = lax.broadcasted_iota(jnp.int32, (LANES, LANES), 1) // HEAD_DIM
    return (r == c).astype(F32)


def _headnorm_fwd(o_f, o_s, g_f, g_s):
    dh, S = o_f.shape
    tm = _tile(S, 512)

    def body(of_ref, os_ref, gf_ref, gs_ref, mix_ref, mixt_ref):
        ones = _group_ones()
        for part, (o_ref, g_ref) in enumerate(((of_ref, gf_ref), (os_ref, gs_ref))):
            for t in range(dh // LANES):
                cols = slice(t * LANES, (t + 1) * LANES)
                out = slice(part * dh + t * LANES, part * dh + (t + 1) * LANES)
                o = o_ref[cols, :].T
                ms = _dot(o * o, ones, precision=HIGHEST) * (1.0 / HEAD_DIM)
                mixn = o * lax.rsqrt(ms + EPS) * g_ref[:, cols]
                mix_ref[:, out] = mixn.astype(BF16)
                mixt_ref[out, :] = mixn.T.astype(BF16)

    return pl.pallas_call(
        body, name="headnorm_fwd", grid=(S // tm,),
        out_shape=(jax.ShapeDtypeStruct((S, 2 * dh), BF16), jax.ShapeDtypeStruct((2 * dh, S), BF16)),
        in_specs=[_col_spec(tm, dh)] * 2 + [_vec_spec(dh)] * 2,
        out_specs=(_row_specs(tm, [2 * dh])[0], _col_spec(tm, 2 * dh)),
        compiler_params=_params(dimension_semantics=("parallel",)),
    )(o_f, o_s, g_f, g_s)


def _resid_prenorm(x, a_out, gate, g, scale, shift):
    S, D = x.shape
    tm = _tile(S, 512)

    def body(x_ref, a_ref, gt_ref, g_ref, sc_ref, sh_ref, x1_ref, h_ref, ht_ref):
        x1 = x_ref[...] + gt_ref[...] * a_ref[...]
        x1_ref[...] = x1
        r = lax.rsqrt(jnp.mean(x1 * x1, axis=-1, keepdims=True) + EPS)
        h = (x1 * r) * g_ref[...] * (1.0 + sc_ref[...]) + sh_ref[...]
        h_ref[...] = h.astype(BF16)
        ht_ref[...] = h.T.astype(BF16)

    return pl.pallas_call(
        body, name="resid_prenorm", grid=(S // tm,),
        out_shape=(jax.ShapeDtypeStruct((S, D), F32), jax.ShapeDtypeStruct((S, D), BF16),
                   jax.ShapeDtypeStruct((D, S), BF16)),
        in_specs=_row_specs(tm, [D, D]) + [_vec_spec(D)] * 4,
        out_specs=tuple(_row_specs(tm, [D, D]) + [_col_spec(tm, D)]),
        compiler_params=_params(dimension_semantics=("parallel",)),
    )(x, a_out, gate, g, scale, shift)


def _shift_down(main, halo, k):
    ext = jnp.concatenate([halo, main], axis=0)
    return pltpu.roll(ext, k, 0)[halo.shape[0]:]


def _shift_up(main, halo, k):
    ext = jnp.concatenate([main, halo], axis=0)
    n = ext.shape[0]
    return pltpu.roll(ext, n - k, 0)[:main.shape[0]]


def _conv(up, up_halo, w_ref, b_ref):
    return (w_ref[2:3, :] * up + w_ref[1:2, :] * _shift_down(up, up_halo, 1)
            + w_ref[0:1, :] * _shift_down(up, up_halo, 2) + b_ref[...])


def _prev_halo_map(tm):
    step = tm // HALO
    return lambda j, i: (jnp.maximum(i * step - 1, 0), j)


MLP_TM = 512
MLP_CT = 1408
CARRY = 8


def _mlp_up(h, wg, wv, cwg, cwv, cbg, cbv):
    S, D = h.shape
    F = wg.shape[1]
    tm, ct = _tile(S, MLP_TM), _tile(F, MLP_CT)
    nct = F // ct

    def body(h_ref, wg_ref, wv_ref, cwg_ref, cwv_ref, cbg_ref, cbv_ref,
             upg_ref, upv_ref, act_ref, actt_ref, hg_scr, hv_scr):
        i, j = pl.program_id(0), pl.program_id(1)
        hv = h_ref[...]
        us = []
        for w_ref, cw_ref, cb_ref, up_ref, scr in ((wg_ref, cwg_ref, cbg_ref, upg_ref, hg_scr),
                                                   (wv_ref, cwv_ref, cbv_ref, upv_ref, hv_scr)):
            up = _dot(hv, w_ref[...]).astype(BF16)
            up_ref[...] = up
            upf = up.astype(F32)
            halo = jnp.where(i == 0, 0.0, scr[j])
            us.append(_conv(upf, halo, cw_ref, cb_ref))
            scr[j] = upf[tm - CARRY:, :]
        act = _silu(us[0]) * us[1]
        act_ref[...] = act.astype(BF16)
        actt_ref[...] = act.T.astype(BF16)

    blk = pl.BlockSpec((tm, ct), lambda i, j: (i, j))
    wspec = pl.BlockSpec((D, ct), lambda i, j: (0, j))
    cwspec = pl.BlockSpec((CONV_W, ct), lambda i, j: (0, j))
    cbspec = pl.BlockSpec((1, ct), lambda i, j: (0, j))
    sds = jax.ShapeDtypeStruct((S, F), BF16)
    return pl.pallas_call(
        body, name="mlp_up", grid=(S // tm, nct),
        out_shape=(sds, sds, sds, jax.ShapeDtypeStruct((F, S), BF16)),
        in_specs=[pl.BlockSpec((tm, D), lambda i, j: (i, 0)), wspec, wspec, cwspec, cwspec, cbspec, cbspec],
        out_specs=(blk, blk, blk, pl.BlockSpec((ct, tm), lambda i, j: (j, i))),
        scratch_shapes=[pltpu.VMEM((nct, CARRY, ct), F32), pltpu.VMEM((nct, CARRY, ct), F32)],
        compiler_params=_params(dimension_semantics=("arbitrary", "arbitrary")),
    )(h, wg, wv, cwg, cwv, cbg, cbv)


def _conv_act_bwd(gm, w_down, up_g, up_v, cwg, cwv, cbg, cbv):
    S, F = up_g.shape
    D = gm.shape[1]
    tm, ct = _tile(S, 256), _tile(F, CONV_COLS)
    nct = F // ct

    def body(gm_ref, wd_ref, ug_ref, uv_ref, hg_ref, hv_ref, wg_ref, wv_ref, bg_ref, bv_ref,
             dug_ref, duv_ref, pg_ref, pv_ref):
        first = pl.program_id(1) == 0

        @pl.when(first)
        def _():
            pg_ref[...] = jnp.zeros_like(pg_ref)
            pv_ref[...] = jnp.zeros_like(pv_ref)

        da = _dot_nt(gm_ref[...], wd_ref[...])
        taps = []
        for u_ref, h_ref in ((ug_ref, hg_ref), (uv_ref, hv_ref)):
            h = jnp.where(first, 0.0, h_ref[...].astype(F32))
            uu = u_ref[...].astype(F32)
            taps.append((_shift_down(uu, h, 2), _shift_down(uu, h, 1), uu))
        u_g = wg_ref[0:1, :] * taps[0][0] + wg_ref[1:2, :] * taps[0][1] + wg_ref[2:3, :] * taps[0][2] + bg_ref[...]
        u_v = wv_ref[0:1, :] * taps[1][0] + wv_ref[1:2, :] * taps[1][1] + wv_ref[2:3, :] * taps[1][2] + bv_ref[...]
        sg = 1.0 / (1.0 + jnp.exp(-u_g))
        du_g = da * u_v * (sg * (1.0 + u_g * (1.0 - sg)))
        du_v = da * (u_g * sg)
        dug_ref[...] = du_g.astype(BF16)
        duv_ref[...] = du_v.astype(BF16)
        for du, tp, p_ref in ((du_g, taps[0], pg_ref), (du_v, taps[1], pv_ref)):
            for k in range(CONV_W):
                p_ref[k:k + 1, :] += jnp.sum(du * tp[k], axis=0, keepdims=True)
            p_ref[CONV_W:CONV_W + 1, :] += jnp.sum(du, axis=0, keepdims=True)

    main = pl.BlockSpec((tm, ct), lambda j, i: (i, j))
    halo = pl.BlockSpec((HALO, ct), _prev_halo_map(tm))
    wspec = pl.BlockSpec((CONV_W, ct), lambda j, i: (0, j))
    bspec = pl.BlockSpec((1, ct), lambda j, i: (0, j))
    pspec = pl.BlockSpec((8, ct), lambda j, i: (0, j))
    return pl.pallas_call(
        body, name="conv_act_bwd", grid=(nct, S // tm),
        out_shape=(jax.ShapeDtypeStruct((S, F), BF16), jax.ShapeDtypeStruct((S, F), BF16),
                   jax.ShapeDtypeStruct((8, F), F32), jax.ShapeDtypeStruct((8, F), F32)),
        in_specs=[pl.BlockSpec((tm, D), lambda j, i: (i, 0)), pl.BlockSpec((ct, D), lambda j, i: (j, 0)),
                  main, main, halo, halo, wspec, wspec, bspec, bspec],
        out_specs=(main, main, pspec, pspec),
        compiler_params=_params(dimension_semantics=("parallel", "arbitrary")),
    )(gm, w_down, up_g, up_v, up_g, up_v, cwg, cwv, cbg, cbv)


def _conv_bwd_taps(d, halo, w_ref):
    return w_ref[2:3, :] * d + w_ref[1:2, :] * _shift_up(d, halo, 1) + w_ref[0:1, :] * _shift_up(d, halo, 2)


def _scan_mats(R, nc, reverse):
    i = lax.broadcasted_iota(jnp.int32, (LANES, LANES), 0)
    j = lax.broadcasted_iota(jnp.int32, (LANES, LANES), 1)
    inner = ((i >= j) if reverse else (i <= j)).astype(F32)
    r = lax.broadcasted_iota(jnp.int32, (R, R), 0)
    c = lax.broadcasted_iota(jnp.int32, (R, R), 1)
    same = (r // nc) == (c // nc)
    outer = (same & ((c > r) if reverse else (c < r))).astype(F32)
    return inner, outer


def _chunk_scan(v, inner, outer, reverse):
    w = _dot(v, inner, precision=HIGHEST)
    col = 0 if reverse else LANES - 1
    carry = _dot(outer, w, precision=HIGHEST)[:, col:col + 1]
    return w + carry


def _fgate_fwd(z_rows, nc):
    R = z_rows.shape[0]

    def body(z_ref, f_ref):
        z = z_ref[...]
        logf = jnp.minimum(z, 0.0) - jnp.log(1.0 + jnp.exp(-jnp.abs(z)))
        inner, outer = _scan_mats(R, nc, False)
        f_ref[...] = _chunk_scan(logf, inner, outer, False)

    return pl.pallas_call(body, name="fgate_fwd", out_shape=jax.ShapeDtypeStruct((R, LANES), F32),
                          compiler_params=_params())(z_rows)


def _fgate_bwd(dfk_neg_rows, dfq_rows, z_rows, nc):
    R = z_rows.shape[0]
    nh = R // nc

    def body(dfk_ref, dfq_ref, z_ref, dz_ref, db_ref):
        inner, outer = _scan_mats(R, nc, True)
        dlogf = _chunk_scan(dfq_ref[...] - dfk_ref[...], inner, outer, True)
        dz = dlogf * (1.0 / (1.0 + jnp.exp(z_ref[...])))
        dz_ref[...] = dz
        hr = lax.broadcasted_iota(jnp.int32, (nh, R), 0)
        hc = lax.broadcasted_iota(jnp.int32, (nh, R), 1) // nc
        per_head = _dot((hr == hc).astype(F32), dz, precision=HIGHEST)
        db_ref[...] = jnp.sum(per_head, axis=1, keepdims=True)

    return pl.pallas_call(
        body, name="fgate_bwd",
        out_shape=(jax.ShapeDtypeStruct((R, LANES), F32), jax.ShapeDtypeStruct((nh, 1), F32)),
        compiler_params=_params())(dfk_neg_rows, dfq_rows, z_rows)


_NEG = -1e30
SKIP_BELOW = -106.0
_SCALE = HEAD_DIM ** -0.5
F_PARTS = 3
Q_F_LANE = HEAD_DIM
Q_ONE_LANE = HEAD_DIM + F_PARTS


def _kv_slice(k0, nk):
    return pl.ds(pl.multiple_of(k0, ATT_BK), nk)


def _mask_t(strict):
    s = lax.broadcasted_iota(jnp.int32, (ATT_BQ, ATT_BQ), 0)
    t = lax.broadcasted_iota(jnp.int32, (ATT_BQ, ATT_BQ), 1)
    return (s < t) if strict else (s <= t)


def _walk_down(i, step, alive, carry):
    carry = step(i * ATT_BQ, ATT_BQ, carry, True)
    first = i * (ATT_BQ // ATT_BK)

    def cond(st):
        n, go, _ = st
        return jnp.logical_and(n < first, go)

    def body(st):
        n, _, cr = st
        j = first - 1 - n
        cr = step(j * ATT_BK, ATT_BK, cr, False)
        return n + 1, alive(jnp.maximum(j - 1, 0), cr), cr

    return lax.while_loop(cond, body, (jnp.int32(0), alive(jnp.maximum(first - 1, 0), carry), carry))[2]


def _t_block(rows):
    return pl.BlockSpec((None, rows, ATT_BQ), lambda h, i, *_: (h, 0, i))


def _t_full(rows, S):
    return pl.BlockSpec((None, rows, S), lambda h, i, *_: (h, 0, 0))


def _n_block():
    return pl.BlockSpec((None, ATT_BQ, LANES), lambda h, i, *_: (h, i, 0))


def _n_full(S):
    return pl.BlockSpec((None, S, LANES), lambda h, i, *_: (h, 0, 0))


def _heads(t):
    S = t.shape[0]
    return jnp.transpose(t.reshape(S, -1, HEAD_DIM), (1, 0, 2))


def _skip_bounds(k_cols, f_rows):
    H, S = f_rows.shape
    f_end = f_rows.reshape(H, S // ATT_BK, ATT_BK)[:, :, -1]
    k_sq = jnp.sum(jnp.square(_heads(k_cols).astype(F32)), axis=-1).reshape(H, S // ATT_BK, ATT_BK)
    k_max = lax.cummax(jnp.sqrt(jnp.max(k_sq, axis=-1)), axis=1)
    return f_end, k_max


def _bf16_parts(f):
    hi = f.astype(BF16).astype(F32)
    mid = (f - hi).astype(BF16).astype(F32)
    return hi, mid, (f - hi - mid).astype(BF16).astype(F32)


def _att_prep(qkv, f_pairs):
    S = qkv.shape[0]
    n_pairs = qkv.shape[1] // (6 * LANES)
    H = 2 * n_pairs
    tm = _tile(S, 512)

    def body(qf_ref, kf_ref, vf_ref, qs_ref, ks_ref, vs_ref, f_ref,
             fqn, fqt, fkn, fkt, fvn, fvt, sqn, sqt, skn, skt, svn, svt):
        lane = lax.broadcasted_iota(jnp.int32, (1, LANES), 1)
        f = f_ref[...]

        def head(ref, e):
            t = ref[...].astype(F32)
            if e == 1:
                t = pltpu.roll(t, HEAD_DIM, 1)
            return jnp.where(lane < HEAD_DIM, t, 0.0)

        def at(first):
            return jnp.logical_and(lane >= first, lane < first + F_PARTS)

        for e in range(2):
            parts = _bf16_parts(f[:, e:e + 1])
            f_lanes = sum(jnp.where(lane == Q_F_LANE + k, parts[k], 0.0) for k in range(F_PARTS))
            nf_lanes = sum(jnp.where(lane == Q_ONE_LANE + k, parts[k], 0.0) for k in range(F_PARTS))
            vals = (
                (fqn, fqt, LANES, head(qf_ref, e) * _SCALE + f_lanes + jnp.where(at(Q_ONE_LANE), 1.0, 0.0)),
                (fkn, fkt, LANES, head(kf_ref, e) + jnp.where(at(Q_F_LANE), 1.0, 0.0) - nf_lanes),
                (fvn, fvt, HEAD_DIM, head(vf_ref, e)),
                (sqn, sqt, LANES, head(qs_ref, e) * _SCALE),
                (skn, skt, LANES, head(ks_ref, e)),
                (svn, svt, HEAD_DIM, head(vs_ref, e)),
            )
            for n_ref, t_ref, rows, val in vals:
                n_ref[e] = val.astype(BF16)
                t_ref[e] = val.T[:rows].astype(BF16)

    col = lambda base: pl.BlockSpec((tm, LANES), lambda i, p: (i, base + p))
    n_spec = pl.BlockSpec((2, tm, LANES), lambda i, p: (p, i, 0))
    t_spec = lambda rows: pl.BlockSpec((2, rows, tm), lambda i, p: (p, 0, i))
    n_sds = jax.ShapeDtypeStruct((H, S, LANES), BF16)
    t_sds = lambda rows: jax.ShapeDtypeStruct((H, rows, S), BF16)
    group = ([n_sds, t_sds(LANES), n_sds, t_sds(LANES), n_sds, t_sds(HEAD_DIM)],
             [n_spec, t_spec(LANES), n_spec, t_spec(LANES), n_spec, t_spec(HEAD_DIM)])
    res = pl.pallas_call(
        body, name="att_prep", grid=(S // tm, n_pairs),
        out_shape=tuple(group[0] * 2),
        in_specs=[col(k * n_pairs) for k in range(6)] + [pl.BlockSpec((None, tm, 2), lambda i, p: (p, i, 0))],
        out_specs=tuple(group[1] * 2),
        compiler_params=_params(dimension_semantics=("parallel", "parallel")),
    )(qkv, qkv, qkv, qkv, qkv, qkv, f_pairs)
    names = ("q_n", "q_t", "k_n", "k_t", "v_n", "v_t")
    return dict(zip(names, res[:6])), dict(zip(names, res[6:]))


def _fox_reach(qt, fend_ref, kmax_ref, h):
    qf = qt.astype(F32)
    q_norm = jnp.sqrt(jnp.sum(jnp.square(qf[:HEAD_DIM]), axis=0, keepdims=True))
    f_t = jnp.sum(qf[Q_F_LANE:Q_F_LANE + F_PARTS], axis=0, keepdims=True)
    return lambda j: q_norm * kmax_ref[h, j] + f_t - fend_ref[h, j]


def _fox_fwd(q_t, k_n, v_t, f_end, k_max, shards):
    H, _, S = q_t.shape
    n, nq = len(shards), S // ATT_BQ

    def body(fend_ref, kmax_ref, qt_ref, k_ref, vt_ref, *rest):
        ins, (ot_ref, lse_ref), outs, sems = rest[:n], rest[n:n + 2], rest[n + 2:2 * n + 2], rest[2 * n + 2:]
        h, i = pl.program_id(0), pl.program_id(1)

        @pl.when(jnp.logical_and(h == 0, i == 0))
        def _():
            for cp in _exchange_copies(ins, outs, *sems, False):
                cp.start()

        qt = qt_ref[...]
        reach = _fox_reach(qt, fend_ref, kmax_ref, h)

        def step(k0, nk, carry, masked):
            m, l, acc = carry
            ks = _kv_slice(k0, nk)
            s = _dot(k_ref[ks, :], qt)
            if masked:
                s = jnp.where(_mask_t(False), s, _NEG)
            mn = jnp.maximum(m, jnp.max(s, axis=0, keepdims=True))
            alpha = jnp.exp(m - mn)
            p = jnp.exp(s - mn)
            l = alpha * l + jnp.sum(p, axis=0, keepdims=True)
            acc = acc * alpha + _dot(vt_ref[:, ks], p.astype(BF16))
            return mn, l, acc

        def alive(j, carry):
            return jnp.max(reach(j) - carry[0]) > SKIP_BELOW

        row = jnp.zeros((1, ATT_BQ), F32)
        m, l, acc = _walk_down(i, step, alive, (row + _NEG, row, jnp.zeros((HEAD_DIM, ATT_BQ), F32)))
        ot_ref[...] = acc / l
        lse_ref[...] = m + jnp.log(l)

        @pl.when(jnp.logical_and(h == H - 1, i == nq - 1))
        def _():
            for cp in _exchange_copies(ins, outs, *sems, False):
                cp.wait()

    any_spec = pl.BlockSpec(memory_space=pl.ANY)
    grid_spec = pltpu.PrefetchScalarGridSpec(
        num_scalar_prefetch=2, grid=(H, nq),
        in_specs=[_t_block(LANES), _n_full(S), _t_full(HEAD_DIM, S)] + [any_spec] * n,
        out_specs=tuple([_t_block(HEAD_DIM), _t_block(1)] + [any_spec] * n),
        scratch_shapes=_exchange_sems(n))
    res = pl.pallas_call(
        body, name="fox_fwd", grid_spec=grid_spec,
        out_shape=tuple([jax.ShapeDtypeStruct((H, HEAD_DIM, S), F32), jax.ShapeDtypeStruct((H, 1, S), F32)]
                        + _exchange_out_shapes(shards, False)),
        compiler_params=_params(dimension_semantics=("arbitrary", "arbitrary"), has_side_effects=True),
    )(f_end, k_max, q_t, k_n, v_t, *shards)
    return res[0], res[1], res[2:]


def _fox_bwd(q_t, q_n, k_n, k_t, v_n, do_t, do_n, o_t, lse, f_end, k_max):
    H, _, S = q_t.shape

    def body(fend_ref, kmax_ref, qt_ref, qn_ref, k_ref, kt_ref, v_ref, dot_ref, don_ref, ot_ref, lse_ref,
             dqt_ref, dk_ref, dv_ref):
        h, i = pl.program_id(0), pl.program_id(1)

        @pl.when(i == 0)
        def _():
            dk_ref[...] = jnp.zeros_like(dk_ref)
            dv_ref[...] = jnp.zeros_like(dv_ref)

        qt, qn, dot, don = qt_ref[...], qn_ref[...], dot_ref[...], don_ref[...]
        lse = lse_ref[...]
        delta = jnp.sum(dot[:HEAD_DIM].astype(F32) * ot_ref[...], axis=0, keepdims=True)
        reach = _fox_reach(qt, fend_ref, kmax_ref, h)

        def alive(j, dq):
            return jnp.max(reach(j) - lse) > SKIP_BELOW

        def step(k0, nk, dq, masked):
            ks = _kv_slice(k0, nk)
            s = _dot(k_ref[ks, :], qt)
            if masked:
                s = jnp.where(_mask_t(False), s, _NEG)
            p = jnp.exp(s - lse)
            ds = (p * (_dot(v_ref[ks, :], dot) - delta)).astype(BF16)
            dk_ref[ks, :] += _dot(ds, qn)
            dv_ref[ks, :] += _dot(p.astype(BF16), don)
            return dq + _dot(kt_ref[:, ks], ds)

        dqt_ref[...] = _walk_down(i, step, alive, jnp.zeros((LANES, ATT_BQ), F32))

    grid_spec = pltpu.PrefetchScalarGridSpec(
        num_scalar_prefetch=2, grid=(H, S // ATT_BQ),
        in_specs=[_t_block(LANES), _n_block(), _n_full(S), _t_full(LANES, S), _n_full(S),
                  _t_block(LANES), _n_block(), _t_block(HEAD_DIM), _t_block(1)],
        out_specs=(_t_block(LANES), _n_full(S), _n_full(S)))
    return pl.pallas_call(
        body, name="fox_bwd", grid_spec=grid_spec,
        out_shape=(jax.ShapeDtypeStruct((H, LANES, S), F32), jax.ShapeDtypeStruct((H, S, LANES), F32),
                   jax.ShapeDtypeStruct((H, S, LANES), F32)),
        compiler_params=_params(dimension_semantics=("parallel", "arbitrary")),
    )(f_end, k_max, q_t, q_n, k_n, k_t, v_n, do_t, do_n, o_t, lse)


def _scan_lhs():
    r = lax.broadcasted_iota(jnp.int32, (SCAN_BK, 2 * SCAN_BK), 0)
    c = lax.broadcasted_iota(jnp.int32, (SCAN_BK, 2 * SCAN_BK), 1) % SCAN_BK
    return (c >= r).astype(BF16)


def _suffix_sum(t, lhs):
    hi = t.astype(BF16)
    lo = (t - hi.astype(F32)).astype(BF16)
    return _dot(lhs, jnp.concatenate([hi, lo], axis=0))


def _sb_scores(k, qt, mask):
    z = _dot(k, qt)
    e = jnp.exp(-jnp.abs(z))
    lb = -(jnp.maximum(z, 0.0) + jnp.log(1.0 + e))
    if mask is not None:
        lb = jnp.where(mask, lb, 0.0)
    return z, e, lb


def _scan_blocks(nk):
    return [slice(u * SCAN_BK, (u + 1) * SCAN_BK) for u in reversed(range(nk // SCAN_BK))]


def _sb_fwd(q_t, k_n, v_t):
    H, _, S = q_t.shape

    def body(qt_ref, k_ref, vt_ref, ot_ref):
        i = pl.program_id(1)
        qt = qt_ref[...]
        lhs = _scan_lhs()

        def step(k0, nk, carry, masked):
            c, acc = carry
            ks = _kv_slice(k0, nk)
            mask = _mask_t(True) if masked else None
            z, _, lb = _sb_scores(k_ref[ks, :], qt, mask)
            parts = []
            for sl in _scan_blocks(nk):
                rin = _suffix_sum(lb[sl], lhs)
                a = jnp.exp(z[sl] + rin + c)
                if masked:
                    a = jnp.where(mask[sl], a, 0.0)
                parts.append(a.astype(BF16))
                c = c + rin[0:1, :]
            a_all = jnp.concatenate(parts[::-1], axis=0)
            return c, acc + _dot(vt_ref[:, ks], a_all)

        carry = (jnp.zeros((1, ATT_BQ), F32), jnp.zeros((HEAD_DIM, ATT_BQ), F32))
        ot_ref[...] = _walk_down(i, step, lambda j, cr: jnp.max(cr[0]) > SKIP_BELOW, carry)[1]

    return pl.pallas_call(
        body, name="sb_fwd", grid=(H, S // ATT_BQ),
        out_shape=jax.ShapeDtypeStruct((H, HEAD_DIM, S), F32),
        in_specs=[_t_block(LANES), _n_full(S), _t_full(HEAD_DIM, S)],
        out_specs=_t_block(HEAD_DIM),
        compiler_params=_params(dimension_semantics=("parallel", "parallel")),
    )(q_t, k_n, v_t)


def _sb_bwd(q_t, q_n, k_n, k_t, v_n, do_t, do_n, o_t, bound):
    H, _, S = q_t.shape
    n, nq = len(bound), S // ATT_BQ

    def body(qt_ref, qn_ref, k_ref, kt_ref, v_ref, dot_ref, don_ref, ot_ref, *rest):
        ins, (dqt_ref, dk_ref, dv_ref) = rest[:n], rest[n:n + 3]
        outs, sems = rest[n + 3:2 * n + 3], rest[2 * n + 3:]
        h, i = pl.program_id(0), pl.program_id(1)

        @pl.when(jnp.logical_and(h == 0, i == 0))
        def _():
            for cp in _exchange_copies(ins, outs, *sems, True):
                cp.start()

        @pl.when(i == 0)
        def _():
            dk_ref[...] = jnp.zeros_like(dk_ref)
            dv_ref[...] = jnp.zeros_like(dv_ref)

        qt, qn, dot, don = qt_ref[...], qn_ref[...], dot_ref[...], don_ref[...]
        lhs = _scan_lhs()
        delta = jnp.sum(dot[:HEAD_DIM].astype(F32) * ot_ref[...], axis=0, keepdims=True)

        def step(k0, nk, carry, masked):
            c, g, dq = carry
            ks = _kv_slice(k0, nk)
            mask = _mask_t(True) if masked else None
            z, e, lb = _sb_scores(k_ref[ks, :], qt, mask)
            da = _dot(v_ref[ks, :], dot)
            a_parts, dz_parts = [], []
            for sl in _scan_blocks(nk):
                rin = _suffix_sum(lb[sl], lhs)
                a = jnp.exp(z[sl] + rin + c)
                if masked:
                    a = jnp.where(mask[sl], a, 0.0)
                ab = a.astype(BF16)
                gg = ab.astype(F32) * da[sl]
                rgin = _suffix_sum(gg, lhs)
                rinv = 1.0 / (1.0 + e[sl])
                sig = jnp.where(z[sl] >= 0.0, rinv, e[sl] * rinv)
                dz = gg - sig * (delta - g - (rgin - gg))
                if masked:
                    dz = jnp.where(mask[sl], dz, 0.0)
                a_parts.append(ab)
                dz_parts.append(dz.astype(BF16))
                c = c + rin[0:1, :]
                g = g + rgin[0:1, :]
            ab_all = jnp.concatenate(a_parts[::-1], axis=0)
            dzb = jnp.concatenate(dz_parts[::-1], axis=0)
            dk_ref[ks, :] += _dot(dzb, qn)
            dv_ref[ks, :] += _dot(ab_all, don)
            return c, g, dq + _dot(kt_ref[:, ks], dzb)

        row = jnp.zeros((1, ATT_BQ), F32)
        carry = (row, row, jnp.zeros((LANES, ATT_BQ), F32))
        dqt_ref[...] = _walk_down(i, step, lambda j, cr: jnp.max(cr[0]) > SKIP_BELOW, carry)[2]

        @pl.when(jnp.logical_and(h == H - 1, i == nq - 1))
        def _():
            for cp in _exchange_copies(ins, outs, *sems, True):
                cp.wait()

    any_spec = pl.BlockSpec(memory_space=pl.ANY)
    res = pl.pallas_call(
        body, name="sb_bwd", grid=(H, nq),
        out_shape=tuple([jax.ShapeDtypeStruct((H, LANES, S), F32), jax.ShapeDtypeStruct((H, S, LANES), F32),
                         jax.ShapeDtypeStruct((H, S, LANES), F32)] + _exchange_out_shapes(bound, True)),
        in_specs=[_t_block(LANES), _n_block(), _n_full(S), _t_full(LANES, S), _n_full(S),
                  _t_block(LANES), _n_block(), _t_block(HEAD_DIM)] + [any_spec] * n,
        out_specs=tuple([_t_block(LANES), _n_full(S), _n_full(S)] + [any_spec] * n),
        scratch_shapes=_exchange_sems(n),
        compiler_params=_params(dimension_semantics=("arbitrary", "arbitrary"), has_side_effects=True),
    )(q_t, q_n, k_n, k_t, v_n, do_t, do_n, o_t, *bound)
    return res[0], res[1], res[2], res[3:]


def _dqkv_assemble(dqf_t, dkf, dvf, dqs_t, dks, dvs):
    H, _, S = dqf_t.shape
    n_pairs = H // 2
    tm = _tile(S, 512)

    def body(dqf_ref, dkf_ref, dvf_ref, dqs_ref, dks_ref, dvs_ref, out_ref, dfk_ref):
        lane = lax.broadcasted_iota(jnp.int32, (1, LANES), 1)
        slabs = ((dqf_ref, True), (dkf_ref, False), (dvf_ref, False),
                 (dqs_ref, True), (dks_ref, False), (dvs_ref, False))
        for k, (ref, transposed) in enumerate(slabs):
            if transposed:
                t0, t1 = ref[0].T * _SCALE, ref[1].T * _SCALE
            else:
                t0, t1 = ref[0], ref[1]
            out_ref[k] = jnp.where(lane < HEAD_DIM, t0, pltpu.roll(t1, HEAD_DIM, 1)).astype(BF16)
        for e in range(2):
            dfk_ref[e] = dkf_ref[e].T[Q_ONE_LANE:Q_ONE_LANE + 1, :]

    t_spec = pl.BlockSpec((2, LANES, tm), lambda i, p: (p, 0, i))
    n_spec = pl.BlockSpec((2, tm, LANES), lambda i, p: (p, i, 0))
    return pl.pallas_call(
        body, name="dqkv_assemble", grid=(S // tm, n_pairs),
        out_shape=(jax.ShapeDtypeStruct((6, S, n_pairs * LANES), BF16), jax.ShapeDtypeStruct((H, 1, S), F32)),
        in_specs=[t_spec, n_spec, n_spec, t_spec, n_spec, n_spec],
        out_specs=(pl.BlockSpec((6, tm, LANES), lambda i, p: (0, i, p)),
                   pl.BlockSpec((2, 1, tm), lambda i, p: (p, 0, i))),
        compiler_params=_params(dimension_semantics=("parallel", "parallel")),
    )(dqf_t, dkf, dvf, dqs_t, dks, dvs)


def _acc_spec(w):
    return pl.BlockSpec((1, w), lambda i: (0, 0))


def _loss_head(x1, act, w_down, gate_m, g_final, target):
    S, D = x1.shape
    F = act.shape[1]
    tm = _tile(S, 256)

    def body(x1_ref, act_ref, w_ref, gt_ref, gf_ref, tg_ref, dx2_ref, gm_ref, loss_ref, dgf_ref, dgt_ref):
        @pl.when(pl.program_id(0) == 0)
        def _():
            loss_ref[...] = jnp.zeros_like(loss_ref)
            dgf_ref[...] = jnp.zeros_like(dgf_ref)
            dgt_ref[...] = jnp.zeros_like(dgt_ref)

        mo = _dot(act_ref[...], w_ref[...])
        x2 = x1_ref[...] + gt_ref[...] * mo
        r = lax.rsqrt(jnp.mean(x2 * x2, axis=-1, keepdims=True) + EPS)
        xh = x2 * r
        diff = xh * gf_ref[...] - tg_ref[...]
        loss_ref[...] += (0.5 / D) * jnp.sum(diff * diff)
        dy = diff * (1.0 / D)
        dgf_ref[...] += jnp.sum(dy * xh, axis=0, keepdims=True)
        dxh = dy * gf_ref[...]
        dx2 = r * (dxh - xh * jnp.mean(dxh * xh, axis=-1, keepdims=True))
        dx2_ref[...] = dx2
        gm_ref[...] = (dx2 * gt_ref[...]).astype(BF16)
        dgt_ref[...] += jnp.sum(dx2 * mo, axis=0, keepdims=True)

    return pl.pallas_call(
        body, name="loss_head", grid=(S // tm,),
        out_shape=(jax.ShapeDtypeStruct((S, D), F32), jax.ShapeDtypeStruct((S, D), BF16),
                   jax.ShapeDtypeStruct((1, LANES), F32), jax.ShapeDtypeStruct((1, D), F32),
                   jax.ShapeDtypeStruct((1, D), F32)),
        in_specs=_row_specs(tm, [D, F]) + [pl.BlockSpec((F, D), lambda i: (0, 0))] + [_vec_spec(D)] * 2
        + _row_specs(tm, [D]),
        out_specs=tuple(_row_specs(tm, [D, D]) + [_acc_spec(LANES), _acc_spec(D), _acc_spec(D)]),
        compiler_params=_params(dimension_semantics=("arbitrary",)),
    )(x1, act, w_down, gate_m, g_final, target)


def _norm_bwd(lhs, rhs, xin, dres, g, scale, name, gate=None, branch=None, bound=(), conv=None):
    S, D = xin.shape
    tm = _tile(S, 256)
    gated = gate is not None
    nl, nb, n_steps = len(lhs), len(bound), S // tm
    nc_ = nl if conv else 0
    n_out = (6 if gated else 4) + nc_

    def body(*refs):
        l_refs, r_refs, rest = refs[:nl], refs[nl:2 * nl], refs[2 * nl:]
        halo_refs, cw_refs, rest = rest[:nc_], rest[nc_:2 * nc_], rest[2 * nc_:]
        if gated:
            x_ref, dr_ref, g_ref, sc_ref, gt_ref, br_ref = rest[:6]
            rest = rest[6:]
        else:
            x_ref, dr_ref, g_ref, sc_ref = rest[:4]
            rest = rest[4:]
        ins, outs, ex_outs, sems = rest[:nb], rest[nb:nb + n_out], rest[nb + n_out:2 * nb + n_out], rest[2 * nb + n_out:]
        dx_ref, dsc_ref, dsh_ref, dg_ref = outs[:4]
        sums = (dsc_ref, dsh_ref, dg_ref) + ((outs[5],) if gated else ())
        dup_refs = outs[n_out - nc_:]
        i = pl.program_id(0)

        @pl.when(i == 0)
        def _():
            for s_ref in sums:
                s_ref[...] = jnp.zeros_like(s_ref)
            if nb:
                for cp in _chip_exchange_copies(ins, ex_outs, *sems):
                    cp.start()

        dhv = None
        for k, (l_ref, r_ref) in enumerate(zip(l_refs, r_refs)):
            if conv:
                halo = jnp.where(i == n_steps - 1, 0.0, halo_refs[k][...].astype(F32))
                dup = _conv_bwd_taps(l_ref[...].astype(F32), halo, cw_refs[k]).astype(BF16)
                dup_refs[k][...] = dup
                terms = [_dot_nt(dup, r_ref[...])]
            elif len(l_ref.shape) == 3:
                K = l_ref.shape[2]
                terms = [_dot_nt(l_ref[k], r_ref[:, k * K:(k + 1) * K]) for k in range(l_ref.shape[0])]
            else:
                terms = [_dot_nt(l_ref[...], r_ref[...])]
            for t in terms:
                dhv = t if dhv is None else dhv + t
        xv = x_ref[...]
        r = lax.rsqrt(jnp.mean(xv * xv, axis=-1, keepdims=True) + EPS)
        xh = xv * r
        dsc_ref[...] += jnp.sum(dhv * (xh * g_ref[...]), axis=0, keepdims=True)
        dsh_ref[...] += jnp.sum(dhv, axis=0, keepdims=True)
        dn = dhv * (1.0 + sc_ref[...])
        dg_ref[...] += jnp.sum(dn * xh, axis=0, keepdims=True)
        dxh = dn * g_ref[...]
        dx = dr_ref[...] + r * (dxh - xh * jnp.mean(dxh * xh, axis=-1, keepdims=True))
        dx_ref[...] = dx
        if gated:
            outs[4][...] = (dx * gt_ref[...]).astype(BF16)
            outs[5][...] += jnp.sum(dx * br_ref[...], axis=0, keepdims=True)

        if nb:
            @pl.when(i == n_steps - 1)
            def _():
                for cp in _chip_exchange_copies(ins, ex_outs, *sems):
                    cp.wait()

    def l_spec(a):
        if a.ndim == 3:
            return pl.BlockSpec((a.shape[0], tm, a.shape[2]), lambda i: (0, i, 0))
        return pl.BlockSpec((tm, a.shape[1]), lambda i: (i, 0))

    any_spec = pl.BlockSpec(memory_space=pl.ANY)
    vec = jax.ShapeDtypeStruct((1, D), F32)
    out_shape = [jax.ShapeDtypeStruct((S, D), F32), vec, vec, vec]
    out_specs = _row_specs(tm, [D]) + [_acc_spec(D)] * 3
    in_specs = [l_spec(a) for a in lhs] + [pl.BlockSpec(b.shape, lambda i: (0, 0)) for b in rhs]
    args = list(lhs) + list(rhs)
    if conv:
        step, last_halo = tm // HALO, S // HALO - 1
        in_specs += [pl.BlockSpec((HALO, a.shape[1]), lambda i: (jnp.minimum((i + 1) * step, last_halo), 0))
                     for a in lhs]
        in_specs += [pl.BlockSpec(w.shape, lambda i: (0, 0)) for w in conv]
        args += list(lhs) + list(conv)
    in_specs += _row_specs(tm, [D, D]) + [_vec_spec(D)] * 2
    args += [xin, dres, g, scale]
    if gated:
        out_shape += [jax.ShapeDtypeStruct((S, D), BF16), vec]
        out_specs += _row_specs(tm, [D]) + [_acc_spec(D)]
        in_specs += [_vec_spec(D)] + _row_specs(tm, [D])
        args += [gate, branch]
    if conv:
        out_shape += [jax.ShapeDtypeStruct(a.shape, BF16) for a in lhs]
        out_specs += [l_spec(a) for a in lhs]
    res = pl.pallas_call(
        body, name=name, grid=(n_steps,),
        out_shape=tuple(out_shape + [jax.ShapeDtypeStruct(b.shape, b.dtype) for b in bound]),
        in_specs=in_specs + [any_spec] * nb, out_specs=tuple(out_specs + [any_spec] * nb),
        scratch_shapes=_chip_exchange_sems(nb) if nb else [],
        compiler_params=_params(dimension_semantics=("arbitrary",), has_side_effects=bool(nb)),
    )(*args, *bound)
    return tuple(res[:n_out]) + (tuple(res[n_out:]),)


def _headnorm_bwd(dmix, o_f, o_s, g_f, g_s):
    dh, S = o_f.shape
    H = dh // HEAD_DIM
    tm = _tile(S, 256)

    def body(dm_ref, of_ref, os_ref, gf_ref, gs_ref, fn_ref, ft_ref, sn_ref, st_ref, dgf_ref, dgs_ref):
        @pl.when(pl.program_id(0) == 0)
        def _():
            dgf_ref[...] = jnp.zeros_like(dgf_ref)
            dgs_ref[...] = jnp.zeros_like(dgs_ref)

        ones = _group_ones()
        lane = lax.broadcasted_iota(jnp.int32, (1, LANES), 1)
        parts = ((of_ref, gf_ref, fn_ref, ft_ref, dgf_ref), (os_ref, gs_ref, sn_ref, st_ref, dgs_ref))
        for part, (o_ref, g_ref, n_ref, t_ref, dg_ref) in enumerate(parts):
            for t in range(dh // LANES):
                cols = slice(t * LANES, (t + 1) * LANES)
                o = o_ref[cols, :].T
                dm = dm_ref[:, part * dh + t * LANES: part * dh + (t + 1) * LANES]
                r = lax.rsqrt(_dot(o * o, ones, precision=HIGHEST) * (1.0 / HEAD_DIM) + EPS)
                oh = o * r
                dg_ref[:, cols] += jnp.sum(dm * oh, axis=0, keepdims=True)
                dn = dm * g_ref[:, cols]
                mean = _dot(dn * oh, ones, precision=HIGHEST) * (1.0 / HEAD_DIM)
                do = r * (dn - oh * mean)
                for e in range(2):
                    d = do if e == 0 else pltpu.roll(do, HEAD_DIM, 1)
                    d = jnp.where(lane < HEAD_DIM, d, 0.0)
                    n_ref[2 * t + e] = d.astype(BF16)
                    t_ref[2 * t + e] = d.T.astype(BF16)

    vec = jax.ShapeDtypeStruct((1, dh), F32)
    n_sds = jax.ShapeDtypeStruct((H, S, LANES), BF16)
    t_sds = jax.ShapeDtypeStruct((H, LANES, S), BF16)
    n_spec = pl.BlockSpec((H, tm, LANES), lambda i: (0, i, 0))
    t_spec = pl.BlockSpec((H, LANES, tm), lambda i: (0, 0, i))
    return pl.pallas_call(
        body, name="headnorm_bwd", grid=(S // tm,),
        out_shape=(n_sds, t_sds, n_sds, t_sds, vec, vec),
        in_specs=_row_specs(tm, [2 * dh]) + [_col_spec(tm, dh)] * 2 + [_vec_spec(dh)] * 2,
        out_specs=(n_spec, t_spec, n_spec, t_spec, _acc_spec(dh), _acc_spec(dh)),
        compiler_params=_params(dimension_semantics=("arbitrary",)),
    )(dmix, o_f, o_s, g_f, g_s)


def _adamw(w, gslots, m, v, name):
    R, C = w.shape
    n = gslots.shape[0]
    tr = 256 if (R % 256 == 0 and R > 256) else R
    bc1 = 1.0 - ADAM_B1 ** ADAM_STEP
    bc2 = 1.0 - ADAM_B2 ** ADAM_STEP

    def body(w_ref, gs_ref, m_ref, v_ref, g_ref, d_ref, nm_ref, nv_ref):
        g = gs_ref[0]
        for s in range(1, n):
            g = g + gs_ref[s]
        nm = ADAM_B1 * m_ref[...] + (1.0 - ADAM_B1) * g
        nv = ADAM_B2 * v_ref[...] + (1.0 - ADAM_B2) * (g * g)
        g_ref[...] = g
        nm_ref[...] = nm
        nv_ref[...] = nv
        d_ref[...] = -ADAM_LR * ((nm / bc1) / (jnp.sqrt(nv / bc2) + ADAM_EPS) + ADAM_WD * w_ref[...])

    blk = pl.BlockSpec((tr, C), lambda i: (i, 0))
    sds = jax.ShapeDtypeStruct((R, C), F32)
    return pl.pallas_call(
        body, name=name, grid=(R // tr,), out_shape=(sds,) * 4,
        in_specs=[blk, pl.BlockSpec((n, tr, C), lambda i: (0, i, 0)), blk, blk], out_specs=(blk,) * 4,
        compiler_params=_params(dimension_semantics=("parallel",)),
    )(w, gslots, m, v)


def _slot_sum(slots, name):
    n, _, C = slots.shape

    def body(s_ref, o_ref):
        acc = s_ref[0]
        for s in range(1, n):
            acc = acc + s_ref[s]
        o_ref[...] = acc

    return pl.pallas_call(body, name=name, out_shape=jax.ShapeDtypeStruct((1, C), F32),
                          compiler_params=_params())(slots)


def _pad_cols(a, n):
    return jnp.pad(a, ((0, 0), (0, n - a.shape[1])))


def _ungather(g, axis):
    if axis == 0:
        return g.reshape(g.shape[0] * g.shape[1], g.shape[2])
    return jnp.transpose(g, (1, 0, 2)).reshape(g.shape[1], g.shape[0] * g.shape[2])


def _to_slots(full, axis):
    R, C = full.shape
    if axis == 0:
        return full.reshape(N_DEV, R // N_DEV, C)
    return jnp.transpose(full.reshape(R, N_DEV, C // N_DEV), (1, 0, 2))


def kernel(x, c, w_ada, b_ada, g_attn, w_in, b_fgate, g_out_fox, g_out_sb, w_out, g_mlp, w_up, conv_w, conv_b, w_down, g_final, loss_target, m_w_ada, m_b_ada, m_g_attn, m_w_in, m_b_fgate, m_g_out_fox, m_g_out_sb, m_w_out, m_g_mlp, m_w_up, m_conv_w, m_conv_b, m_w_down, m_g_final, v_w_ada, v_b_ada, v_g_attn, v_w_in, v_b_fgate, v_g_out_fox, v_g_out_sb, v_w_out, v_g_mlp, v_w_up, v_conv_w, v_conv_b, v_w_down, v_g_final):
    S, D = x.shape[1], x.shape[2]
    dh = D // 2
    n_heads = dh // HEAD_DIM
    n_qkv = 6 * dh
    ff = w_down.shape[1] * N_DEV
    ffp = -(-ff // (2 * LANES)) * (2 * LANES)
    nc = S // LANES
    me = 4 * lax.axis_index("x") + 2 * lax.axis_index("y") + lax.axis_index("c")
    xs, tgt = x[0], loss_target[0]

    c_all, win_g = _gather_two_level([c, w_in[0].astype(BF16)], name="gather_first")
    c_all = c_all.reshape(N_DEV, D)
    W_in = _ungather(win_g, 1)
    W_qkv, W_f = W_in[:, :n_qkv], _pad_cols(W_in[:, n_qkv:], LANES)
    cb_g, cb_v = _pad_cols(conv_b[:, :ff], ffp), _pad_cols(conv_b[:, ff:], ffp)

    n_ada = w_ada.shape[2]
    b_shard = lax.dynamic_slice(b_ada, (0, me * n_ada), (1, n_ada))
    mod_cols = _ada_fwd(c_all, w_ada[0], b_shard)
    (mod_g,) = _exchange([mod_cols], scatter=False, name="gather_mod")
    mod = lax.dynamic_index_in_dim(mod_g, me, axis=1, keepdims=False).reshape(6, 1, D)
    shift_a, scale_a, gate_a, shift_m, scale_m, gate_m = [mod[k] for k in range(6)]

    h1, h1_t = _prenorm(xs, g_attn, scale_a, shift_a, "prenorm_attn")
    qkv = _mm(h1, W_qkv, BF16, "proj_qkv")
    flog = _mm(h1, W_f, F32, "proj_fgate")
    zf = flog[:, :n_heads] + b_fgate
    z_rows = zf.T.reshape(n_heads * nc, LANES)
    f_rows = _fgate_fwd(z_rows, nc).reshape(n_heads, S)
    f_pairs = jnp.transpose(f_rows.reshape(n_heads // 2, 2, S), (0, 2, 1))
    fox, sb = _att_prep(qkv, f_pairs)
    f_end, k_max = _skip_bounds(qkv[:, dh:2 * dh], f_rows)
    of_t, lse, (wout_g, wup_g, wdown_g, convw_g) = _fox_fwd(
        fox["q_t"], fox["k_n"], fox["v_t"], f_end, k_max,
        [w_out[0].astype(BF16), w_up[0].astype(BF16), w_down[0].astype(BF16), conv_w[0]])
    W_out = _ungather(wout_g, 0)
    W_up = _ungather(wup_g, 1)
    W_g, W_v = _pad_cols(W_up[:, :ff], ffp), _pad_cols(W_up[:, ff:], ffp)
    W_down = jnp.pad(_ungather(wdown_g, 0), ((0, ffp - ff), (0, 0)))
    cw_full = _ungather(convw_g, 1)
    cw_g, cw_v = _pad_cols(cw_full[:, :ff], ffp), _pad_cols(cw_full[:, ff:], ffp)
    os_t = _sb_fwd(sb["q_t"], sb["k_n"], sb["v_t"])
    o_f, o_s = of_t.reshape(dh, S), os_t.reshape(dh, S)
    mix, mix_t = _headnorm_fwd(o_f, o_s, g_out_fox, g_out_sb)
    a_out = _mm(mix, W_out, F32, "proj_out")
    x1, h2, h2_t = _resid_prenorm(xs, a_out, gate_a, g_mlp, scale_m, shift_m)
    up_g, up_v, act, act_t = _mlp_up(h2, W_g, W_v, cw_g, cw_v, cb_g, cb_v)

    dx2, gm, loss_p, dg_final, dgate_m = _loss_head(x1, act, W_down, gate_m, g_final.reshape(1, D), tgt)
    dW_down = _mm_acc(act_t, gm, "bwd_down_w")
    du_g, du_v, p_g, p_v = _conv_act_bwd(gm, W_down, up_g, up_v, cw_g, cw_v, cb_g, cb_v)
    dx1, dscale_m, dshift_m, dg_mlp, ga, dgate_a, dup_g, dup_v, _ = _norm_bwd(
        [du_g, du_v], [W_g, W_v], x1, dx2, g_mlp, scale_m, "norm_mlp_bwd", gate=gate_a, branch=a_out,
        conv=[cw_g, cw_v])
    dW_g = _mm_acc(h2_t, dup_g, "bwd_up_w_g")
    dW_v = _mm_acc(h2_t, dup_v, "bwd_up_w_v")
    dmix = _mm(ga, W_out, F32, "bwd_out_act", nt=True)
    dW_out = _mm_acc(mix_t, ga, "bwd_out_w")
    dof_n, dof_t, dos_n, dos_t, dg_fox, dg_sb = _headnorm_bwd(dmix, o_f, o_s, g_out_fox, g_out_sb)
    dqf_t, dkf, dvf = _fox_bwd(fox["q_t"], fox["q_n"], fox["k_n"], fox["k_t"], fox["v_n"], dof_t, dof_n, of_t, lse,
                              f_end, k_max)
    dW_upf = jnp.concatenate([dW_g[:, :ff], dW_v[:, :ff]], axis=1)
    dcw = jnp.concatenate([p_g[:CONV_W, :ff], p_v[:CONV_W, :ff]], axis=1)
    dqs_t, dks, dvs, (s_out, s_up, s_down, s_cw) = _sb_bwd(
        sb["q_t"], sb["q_n"], sb["k_n"], sb["k_t"], sb["v_n"], dos_t, dos_n, os_t,
        [_to_slots(dW_out, 0), _to_slots(dW_upf, 1), _to_slots(dW_down[:ff], 0), _to_slots(dcw, 1)])
    dparts, dfk = _dqkv_assemble(dqf_t, dkf, dvf, dqs_t, dks, dvs)
    dz_rows, db_fgate = _fgate_bwd(dfk.reshape(n_heads * nc, LANES),
                                   dqf_t[:, Q_F_LANE, :].reshape(n_heads * nc, LANES), z_rows, nc)
    dzf = _pad_cols(dz_rows.reshape(n_heads, S).T, LANES).astype(BF16)
    dW_qkv = _mm_acc_parts(h1_t, dparts, "bwd_in_w")
    dW_f = _mm_acc(h1_t, dzf, "bwd_in_w_fgate")
    dW_in = jnp.concatenate([jnp.transpose(dW_qkv, (1, 0, 2)).reshape(D, n_qkv), dW_f[:, :n_heads]], axis=1)
    bound_in = _to_slots(dW_in, 1)
    bound_in = bound_in.reshape((N_CHIPS, 2) + bound_in.shape[1:])
    (got_in,) = _scatter_to_sibling([bound_in], "scatter_sibling")
    c_idx = lax.axis_index("c").astype(jnp.int32).reshape(1)
    grad_x, dscale_a, dshift_a, dg_attn, (s_in,) = _norm_bwd(
        [dparts, dzf], [W_qkv, W_f], xs, dx1, g_attn, scale_a, "norm_attn_bwd",
        bound=[_pair_add(bound_in, got_in, c_idx, "pair_add")])

    dconv_b = jnp.concatenate([p_g[CONV_W:CONV_W + 1, :ff], p_v[CONV_W:CONV_W + 1, :ff]], axis=1)
    parts = [dshift_a, dscale_a, dgate_a, dshift_m, dscale_m, dgate_m,
             dg_attn, db_fgate.reshape(1, n_heads), dg_fox, dg_sb, dg_mlp, dconv_b, dg_final,
             loss_p[:, :1]]
    sizes = [p.shape[1] for p in parts]
    vec = jnp.concatenate(parts, axis=1)
    n_vec = -(-vec.shape[1] // LANES) * LANES
    vec = _pad_cols(vec, n_vec)
    (vec_g,) = _exchange([vec], scatter=False, name="gather_small")
    offs = [0]
    for s in sizes:
        offs.append(offs[-1] + s)

    def small(k0, k1=None):
        k1 = k0 if k1 is None else k1
        return vec_g[:, :, offs[k0]:offs[k1 + 1]]

    dmod_all = small(0, 5).reshape(N_DEV, 6 * D)
    dmod_cols = lax.dynamic_slice(dmod_all, (0, me * n_ada), (N_DEV, n_ada))
    dW_ada = _ada_bwd(c_all.T, dmod_cols)


    res = {}
    res["w_ada"] = _adamw(w_ada[0], dW_ada[None], m_w_ada[0], v_w_ada[0], "adamw_w_ada")
    res["w_in"] = _adamw(w_in[0], s_in, m_w_in[0], v_w_in[0], "adamw_w_in")
    res["w_out"] = _adamw(w_out[0], s_out, m_w_out[0], v_w_out[0], "adamw_w_out")
    res["w_up"] = _adamw(w_up[0], s_up, m_w_up[0], v_w_up[0], "adamw_w_up")
    res["w_down"] = _adamw(w_down[0], s_down, m_w_down[0], v_w_down[0], "adamw_w_down")
    res["conv_w"] = _adamw(conv_w[0], s_cw, m_conv_w[0], v_conv_w[0], "adamw_conv_w")
    small_names = ["b_ada", "g_attn", "b_fgate", "g_out_fox", "g_out_sb", "g_mlp", "conv_b", "g_final"]
    small_w = [b_ada, g_attn, b_fgate, g_out_fox, g_out_sb, g_mlp, conv_b, g_final.reshape(1, D)]
    small_m = [m_b_ada, m_g_attn, m_b_fgate, m_g_out_fox, m_g_out_sb, m_g_mlp, m_conv_b, m_g_final.reshape(1, D)]
    small_v = [v_b_ada, v_g_attn, v_b_fgate, v_g_out_fox, v_g_out_sb, v_g_mlp, v_conv_b, v_g_final.reshape(1, D)]
    small_res = _adamw(jnp.concatenate(small_w, axis=1), small(0, 12), jnp.concatenate(small_m, axis=1),
                       jnp.concatenate(small_v, axis=1), "adamw_small")
    lo = 0
    for nm, wv in zip(small_names, small_w):
        res[nm] = tuple(r[:, lo:lo + wv.shape[1]] for r in small_res)
        lo += wv.shape[1]
    loss = _slot_sum(_pad_cols(small(13).reshape(N_DEV, 1), LANES).reshape(N_DEV, 1, LANES), "loss_sum")[0, 0]

    names = ["w_ada", "b_ada", "g_attn", "w_in", "b_fgate", "g_out_fox", "g_out_sb", "w_out", "g_mlp",
             "w_up", "conv_w", "conv_b", "w_down", "g_final"]

    def shaped(n, a):
        if n == "g_final":
            return a.reshape(D)
        if n in ("b_ada", "g_attn", "b_fgate", "g_out_fox", "g_out_sb", "g_mlp", "conv_b"):
            return a
        return a[None]

    outs = [loss, grad_x[None]]
    for k in range(4):
        outs += [shaped(n, res[n][k]) for n in names]
    return tuple(outs)
```

```python
import jax
import jax.numpy as jnp
from jax import lax
from jax.experimental import pallas as pl
from jax.experimental.pallas import tpu as pltpu

F32 = jnp.float32
BF16 = jnp.bfloat16
HIGHEST = lax.Precision.HIGHEST

N_DEV = 8
LANES = 128
HEAD_DIM = 64
EPS = 1e-6
CONV_W = 3
CONV_COLS = 1408
HALO = 16
ATT_BQ = 512
FOX_BK = 512
SB_BK = 256
SCAN_BK = 128
VMEM_LIMIT = 56 * 1024 * 1024

ADAM_LR = 0.001
ADAM_B1 = 0.9
ADAM_B2 = 0.999
ADAM_EPS = 1e-08
ADAM_WD = 0.01
ADAM_STEP = 10


def _params(**kw):
    return pltpu.CompilerParams(vmem_limit_bytes=VMEM_LIMIT, **kw)


def _tile(n, cap):
    if n <= cap:
        return n
    best = None
    for t in range(LANES, cap + 1, LANES):
        if n % t == 0:
            best = t
    assert best is not None, (n, cap)
    return best


def _dot(a, b, **kw):
    return jnp.dot(a, b, preferred_element_type=F32, **kw)


def _exchange_copies(ins, outs, send_sems, recv_sems, loc_sems, scatter):
    n = len(ins)
    if n == 0:
        return []
    x, y, c = lax.axis_index("x"), lax.axis_index("y"), lax.axis_index("c")
    me = 4 * x + 2 * y + c
    copies = []
    for a in range(n):
        src = ins[a].at[me] if scatter else ins[a]
        copies.append(pltpu.make_async_copy(src, outs[a].at[me], loc_sems.at[a]))
    for k in range(1, N_DEV):
        px = 1 - x if k & 4 else x
        py = 1 - y if k & 2 else y
        pc = 1 - c if k & 1 else c
        peer = 4 * px + 2 * py + pc
        for a in range(n):
            src = ins[a].at[peer] if scatter else ins[a]
            copies.append(pltpu.make_async_remote_copy(
                src_ref=src, dst_ref=outs[a].at[me],
                send_sem=send_sems.at[a, k - 1], recv_sem=recv_sems.at[a, k - 1],
                device_id=(px, py, pc), device_id_type=pl.DeviceIdType.MESH))
    return copies


def _exchange_out_shapes(arrays, scatter):
    return [jax.ShapeDtypeStruct((N_DEV,) + tuple(a.shape[1:] if scatter else a.shape), a.dtype) for a in arrays]


def _exchange_sems(n):
    return [pltpu.SemaphoreType.DMA((n, N_DEV - 1)), pltpu.SemaphoreType.DMA((n, N_DEV - 1)),
            pltpu.SemaphoreType.DMA((n,))]


def _exchange(arrays, scatter, name):
    n = len(arrays)

    def body(*refs):
        copies = _exchange_copies(refs[:n], refs[n:2 * n], *refs[2 * n:], scatter)
        for cp in copies:
            cp.start()
        for cp in copies:
            cp.wait()

    any_spec = pl.BlockSpec(memory_space=pl.ANY)
    return pl.pallas_call(
        body, name=name, out_shape=tuple(_exchange_out_shapes(arrays, scatter)),
        in_specs=[any_spec] * n, out_specs=tuple([any_spec] * n),
        scratch_shapes=_exchange_sems(n),
        compiler_params=pltpu.CompilerParams(has_side_effects=True),
    )(*arrays)


N_CHIPS = 4


def _scatter_to_sibling(arrays, name):
    n = len(arrays)
    out_shape = [jax.ShapeDtypeStruct((N_CHIPS,) + tuple(a.shape[2:]), a.dtype) for a in arrays]

    def body(*refs):
        ins, outs = refs[:n], refs[n:2 * n]
        send_sems, recv_sems = refs[2 * n:]
        x, y, c = lax.axis_index("x"), lax.axis_index("y"), lax.axis_index("c")
        copies = []
        for a in range(n):
            for q in range(N_CHIPS):
                cp = pltpu.make_async_remote_copy(
                    src_ref=ins[a].at[q, 1 - c], dst_ref=outs[a].at[q],
                    send_sem=send_sems.at[a, q], recv_sem=recv_sems.at[a, q],
                    device_id=(x, y, 1 - c), device_id_type=pl.DeviceIdType.MESH)
                cp.start()
                copies.append(cp)
        for cp in copies:
            cp.wait()

    any_spec = pl.BlockSpec(memory_space=pl.ANY)
    return pl.pallas_call(
        body, name=name, out_shape=tuple(out_shape),
        in_specs=[any_spec] * n, out_specs=tuple([any_spec] * n),
        scratch_shapes=[pltpu.SemaphoreType.DMA((n, N_CHIPS)), pltpu.SemaphoreType.DMA((n, N_CHIPS))],
        compiler_params=pltpu.CompilerParams(has_side_effects=True),
    )(*arrays)


def _pair_add(mine, got, c_idx, name):
    _, _, R, C = mine.shape
    tr = 256 if (R % 256 == 0 and R > 256) else R

    def body(c_ref, m_ref, g_ref, o_ref):
        o_ref[...] = m_ref[...] + g_ref[...]

    grid_spec = pltpu.PrefetchScalarGridSpec(
        num_scalar_prefetch=1, grid=(N_CHIPS, R // tr),
        in_specs=[pl.BlockSpec((None, None, tr, C), lambda q, i, c_ref: (q, c_ref[0], i, 0)),
                  pl.BlockSpec((None, tr, C), lambda q, i, c_ref: (q, i, 0))],
        out_specs=pl.BlockSpec((None, tr, C), lambda q, i, c_ref: (q, i, 0)))
    return pl.pallas_call(
        body, name=name, grid_spec=grid_spec, out_shape=jax.ShapeDtypeStruct((N_CHIPS, R, C), mine.dtype),
        compiler_params=_params(dimension_semantics=("parallel", "parallel")),
    )(c_idx, mine, got)


def _chip_exchange_copies(ins, outs, send_sems, recv_sems, loc_sems):
    n = len(ins)
    x, y, c = lax.axis_index("x"), lax.axis_index("y"), lax.axis_index("c")
    myq = 2 * x + y
    copies = [pltpu.make_async_copy(ins[a].at[myq], outs[a].at[myq], loc_sems.at[a]) for a in range(n)]
    for k in range(1, N_CHIPS):
        qx = 1 - x if k & 2 else x
        qy = 1 - y if k & 1 else y
        for a in range(n):
            copies.append(pltpu.make_async_remote_copy(
                src_ref=ins[a].at[2 * qx + qy], dst_ref=outs[a].at[myq],
                send_sem=send_sems.at[a, k - 1], recv_sem=recv_sems.at[a, k - 1],
                device_id=(qx, qy, c), device_id_type=pl.DeviceIdType.MESH))
    return copies


def _chip_exchange_sems(n):
    return [pltpu.SemaphoreType.DMA((n, N_CHIPS - 1)), pltpu.SemaphoreType.DMA((n, N_CHIPS - 1)),
            pltpu.SemaphoreType.DMA((n,))]


def _dot_nt(a, b):
    return lax.dot_general(a, b, (((1,), (1,)), ((), ())), preferred_element_type=F32)


def _rhs_spec(b, tn, nt):
    if nt:
        return pl.BlockSpec((tn, b.shape[1]), lambda i, j: (j, 0))
    return pl.BlockSpec((b.shape[0], tn), lambda i, j: (0, j))


def _mm(a, b, out_dtype, name, tm=1024, tn=512, nt=False):
    M, K = a.shape
    N = b.shape[0] if nt else b.shape[1]
    tm, tn = _tile(M, tm), _tile(N, tn)
    dot = _dot_nt if nt else _dot

    def body(a_ref, b_ref, o_ref):
        o_ref[...] = dot(a_ref[...], b_ref[...]).astype(out_dtype)

    return pl.pallas_call(
        body, name=name, out_shape=jax.ShapeDtypeStruct((M, N), out_dtype),
        grid=(M // tm, N // tn),
        in_specs=[pl.BlockSpec((tm, K), lambda i, j: (i, 0)), _rhs_spec(b, tn, nt)],
        out_specs=pl.BlockSpec((tm, tn), lambda i, j: (i, j)),
        compiler_params=_params(dimension_semantics=("parallel", "parallel")),
    )(a, b)


def _mm_acc(a, b, name, tm=1408, tn=1408, tk=512):
    M, S = a.shape
    _, N = b.shape
    tm, tn, tk = _tile(M, tm), _tile(N, tn), _tile(S, tk)

    def body(a_ref, b_ref, o_ref):
        @pl.when(pl.program_id(2) == 0)
        def _():
            o_ref[...] = jnp.zeros_like(o_ref)

        o_ref[...] += _dot(a_ref[...], b_ref[...])

    return pl.pallas_call(
        body, name=name, out_shape=jax.ShapeDtypeStruct((M, N), F32),
        grid=(M // tm, N // tn, S // tk),
        in_specs=[pl.BlockSpec((tm, tk), lambda i, j, k: (i, k)), pl.BlockSpec((tk, tn), lambda i, j, k: (k, j))],
        out_specs=pl.BlockSpec((tm, tn), lambda i, j, k: (i, j)),
        compiler_params=_params(dimension_semantics=("parallel", "parallel", "arbitrary")),
    )(a, b)


def _mm_acc_parts(a, parts, name, tm=1024, tk=1024):
    M, S = a.shape
    P, _, K = parts.shape
    tm, tk = _tile(M, tm), _tile(S, tk)

    def body(a_ref, b_ref, o_ref):
        @pl.when(pl.program_id(2) == 0)
        def _():
            o_ref[...] = jnp.zeros_like(o_ref)

        o_ref[...] += _dot(a_ref[...], b_ref[...])

    return pl.pallas_call(
        body, name=name, out_shape=jax.ShapeDtypeStruct((P, M, K), F32), grid=(P, M // tm, S // tk),
        in_specs=[pl.BlockSpec((tm, tk), lambda k, i, s: (i, s)), pl.BlockSpec((None, tk, K), lambda k, i, s: (k, s, 0))],
        out_specs=pl.BlockSpec((None, tm, K), lambda k, i, s: (k, i, 0)),
        compiler_params=_params(dimension_semantics=("parallel", "parallel", "arbitrary")),
    )(a, parts)


def _silu(z):
    return z * (1.0 / (1.0 + jnp.exp(-z)))


def _ada_fwd(c_all, w_shard, b_shard):
    n = w_shard.shape[1]

    def body(c_ref, w_ref, b_ref, o_ref):
        o_ref[...] = _dot(_silu(c_ref[...]), w_ref[...], precision=HIGHEST) + b_ref[...]

    return pl.pallas_call(body, name="ada_fwd", out_shape=jax.ShapeDtypeStruct((N_DEV, n), F32),
                          compiler_params=_params())(c_all, w_shard, b_shard)


def _ada_bwd(c_all_t, dmod_cols):
    D = c_all_t.shape[0]
    n = dmod_cols.shape[1]

    def body(ct_ref, dm_ref, o_ref):
        sc = _silu(ct_ref[...])
        dm = dm_ref[...]
        acc = sc[:, 0:1] * dm[0:1, :]
        for b in range(1, N_DEV):
            acc = acc + sc[:, b:b + 1] * dm[b:b + 1, :]
        o_ref[...] = acc

    return pl.pallas_call(body, name="ada_bwd", out_shape=jax.ShapeDtypeStruct((D, n), F32),
                          compiler_params=_params())(c_all_t, dmod_cols)


def _row_specs(tm, widths):
    return [pl.BlockSpec((tm, w), lambda i: (i, 0)) for w in widths]


def _vec_spec(w):
    return pl.BlockSpec((1, w), lambda i: (0, 0))


def _col_spec(tm, w):
    return pl.BlockSpec((w, tm), lambda i: (0, i))


def _prenorm(x, g, scale, shift, name, shards=()):
    S, D = x.shape
    tm = _tile(S, 512)
    n, n_steps = len(shards), S // tm

    def body(x_ref, g_ref, sc_ref, sh_ref, *rest):
        ins, (h_ref, ht_ref), outs, sems = rest[:n], rest[n:n + 2], rest[n + 2:2 * n + 2], rest[2 * n + 2:]
        i = pl.program_id(0)

        @pl.when(i == 0)
        def _():
            for cp in _exchange_copies(ins, outs, *sems, False):
                cp.start()

        xv = x_ref[...]
        r = lax.rsqrt(jnp.mean(xv * xv, axis=-1, keepdims=True) + EPS)
        h = (xv * r) * g_ref[...] * (1.0 + sc_ref[...]) + sh_ref[...]
        h_ref[...] = h.astype(BF16)
        ht_ref[...] = h.T.astype(BF16)

        @pl.when(i == n_steps - 1)
        def _():
            for cp in _exchange_copies(ins, outs, *sems, False):
                cp.wait()

    any_spec = pl.BlockSpec(memory_space=pl.ANY)
    res = pl.pallas_call(
        body, name=name, grid=(n_steps,),
        out_shape=tuple([jax.ShapeDtypeStruct((S, D), BF16), jax.ShapeDtypeStruct((D, S), BF16)]
                        + _exchange_out_shapes(shards, False)),
        in_specs=_row_specs(tm, [D]) + [_vec_spec(D)] * 3 + [any_spec] * n,
        out_specs=tuple([_row_specs(tm, [D])[0], _col_spec(tm, D)] + [any_spec] * n),
        scratch_shapes=_exchange_sems(n),
        compiler_params=_params(dimension_semantics=("arbitrary",), has_side_effects=True),
    )(x, g, scale, shift, *shards)
    return res[0], res[1], res[2:]


def _group_ones():
    r = lax.broadcasted_iota(jnp.int32, (LANES, LANES), 0) // HEAD_DIM
    c = lax.broadcasted_iota(jnp.int32, (LANES, LANES), 1) // HEAD_DIM
    return (r == c).astype(F32)


def _headnorm_fwd(o_f, o_s, g_f, g_s):
    dh, S = o_f.shape
    tm = _tile(S, 512)

    def body(of_ref, os_ref, gf_ref, gs_ref, mix_ref, mixt_ref):
        ones = _group_ones()
        for part, (o_ref, g_ref) in enumerate(((of_ref, gf_ref), (os_ref, gs_ref))):
            for t in range(dh // LANES):
                cols = slice(t * LANES, (t + 1) * LANES)
                out = slice(part * dh + t * LANES, part * dh + (t + 1) * LANES)
                o = o_ref[cols, :].T
                ms = _dot(o * o, ones, precision=HIGHEST) * (1.0 / HEAD_DIM)
                mixn = o * lax.rsqrt(ms + EPS) * g_ref[:, cols]
                mix_ref[:, out] = mixn.astype(BF16)
                mixt_ref[out, :] = mixn.T.astype(BF16)

    return pl.pallas_call(
        body, name="headnorm_fwd", grid=(S // tm,),
        out_shape=(jax.ShapeDtypeStruct((S, 2 * dh), BF16), jax.ShapeDtypeStruct((2 * dh, S), BF16)),
        in_specs=[_col_spec(tm, dh)] * 2 + [_vec_spec(dh)] * 2,
        out_specs=(_row_specs(tm, [2 * dh])[0], _col_spec(tm, 2 * dh)),
        compiler_params=_params(dimension_semantics=("parallel",)),
    )(o_f, o_s, g_f, g_s)


def _resid_prenorm(x, a_out, gate, g, scale, shift):
    S, D = x.shape
    tm = _tile(S, 512)

    def body(x_ref, a_ref, gt_ref, g_ref, sc_ref, sh_ref, x1_ref, h_ref, ht_ref):
        x1 = x_ref[...] + gt_ref[...] * a_ref[...]
        x1_ref[...] = x1
        r = lax.rsqrt(jnp.mean(x1 * x1, axis=-1, keepdims=True) + EPS)
        h = (x1 * r) * g_ref[...] * (1.0 + sc_ref[...]) + sh_ref[...]
        h_ref[...] = h.astype(BF16)
        ht_ref[...] = h.T.astype(BF16)

    return pl.pallas_call(
        body, name="resid_prenorm", grid=(S // tm,),
        out_shape=(jax.ShapeDtypeStruct((S, D), F32), jax.ShapeDtypeStruct((S, D), BF16),
                   jax.ShapeDtypeStruct((D, S), BF16)),
        in_specs=_row_specs(tm, [D, D]) + [_vec_spec(D)] * 4,
        out_specs=tuple(_row_specs(tm, [D, D]) + [_col_spec(tm, D)]),
        compiler_params=_params(dimension_semantics=("parallel",)),
    )(x, a_out, gate, g, scale, shift)


def _shift_down(main, halo, k):
    ext = jnp.concatenate([halo, main], axis=0)
    return pltpu.roll(ext, k, 0)[halo.shape[0]:]


def _shift_up(main, halo, k):
    ext = jnp.concatenate([main, halo], axis=0)
    n = ext.shape[0]
    return pltpu.roll(ext, n - k, 0)[:main.shape[0]]


def _conv(up, up_halo, w_ref, b_ref):
    return (w_ref[2:3, :] * up + w_ref[1:2, :] * _shift_down(up, up_halo, 1)
            + w_ref[0:1, :] * _shift_down(up, up_halo, 2) + b_ref[...])


def _prev_halo_map(tm):
    step = tm // HALO
    return lambda j, i: (jnp.maximum(i * step - 1, 0), j)


MLP_TM = 512
MLP_CT = 1408
CARRY = 8


def _mlp_up(h, wg, wv, cwg, cwv, cbg, cbv):
    S, D = h.shape
    F = wg.shape[1]
    tm, ct = _tile(S, MLP_TM), _tile(F, MLP_CT)
    nct = F // ct

    def body(h_ref, wg_ref, wv_ref, cwg_ref, cwv_ref, cbg_ref, cbv_ref,
             upg_ref, upv_ref, act_ref, actt_ref, hg_scr, hv_scr):
        i, j = pl.program_id(0), pl.program_id(1)
        hv = h_ref[...]
        us = []
        for w_ref, cw_ref, cb_ref, up_ref, scr in ((wg_ref, cwg_ref, cbg_ref, upg_ref, hg_scr),
                                                   (wv_ref, cwv_ref, cbv_ref, upv_ref, hv_scr)):
            up = _dot(hv, w_ref[...]).astype(BF16)
            up_ref[...] = up
            upf = up.astype(F32)
            halo = jnp.where(i == 0, 0.0, scr[j])
            us.append(_conv(upf, halo, cw_ref, cb_ref))
            scr[j] = upf[tm - CARRY:, :]
        act = _silu(us[0]) * us[1]
        act_ref[...] = act.astype(BF16)
        actt_ref[...] = act.T.astype(BF16)

    blk = pl.BlockSpec((tm, ct), lambda i, j: (i, j))
    wspec = pl.BlockSpec((D, ct), lambda i, j: (0, j))
    cwspec = pl.BlockSpec((CONV_W, ct), lambda i, j: (0, j))
    cbspec = pl.BlockSpec((1, ct), lambda i, j: (0, j))
    sds = jax.ShapeDtypeStruct((S, F), BF16)
    return pl.pallas_call(
        body, name="mlp_up", grid=(S // tm, nct),
        out_shape=(sds, sds, sds, jax.ShapeDtypeStruct((F, S), BF16)),
        in_specs=[pl.BlockSpec((tm, D), lambda i, j: (i, 0)), wspec, wspec, cwspec, cwspec, cbspec, cbspec],
        out_specs=(blk, blk, blk, pl.BlockSpec((ct, tm), lambda i, j: (j, i))),
        scratch_shapes=[pltpu.VMEM((nct, CARRY, ct), F32), pltpu.VMEM((nct, CARRY, ct), F32)],
        compiler_params=_params(dimension_semantics=("arbitrary", "arbitrary")),
    )(h, wg, wv, cwg, cwv, cbg, cbv)


def _conv_act_bwd(gm, w_down, up_g, up_v, cwg, cwv, cbg, cbv):
    S, F = up_g.shape
    D = gm.shape[1]
    tm, ct = _tile(S, 256), _tile(F, CONV_COLS)
    nct = F // ct

    def body(gm_ref, wd_ref, ug_ref, uv_ref, hg_ref, hv_ref, wg_ref, wv_ref, bg_ref, bv_ref,
             dug_ref, duv_ref, pg_ref, pv_ref):
        first = pl.program_id(1) == 0

        @pl.when(first)
        def _():
            pg_ref[...] = jnp.zeros_like(pg_ref)
            pv_ref[...] = jnp.zeros_like(pv_ref)

        da = _dot_nt(gm_ref[...], wd_ref[...])
        taps = []
        for u_ref, h_ref in ((ug_ref, hg_ref), (uv_ref, hv_ref)):
            h = jnp.where(first, 0.0, h_ref[...].astype(F32))
            uu = u_ref[...].astype(F32)
            taps.append((_shift_down(uu, h, 2), _shift_down(uu, h, 1), uu))
        u_g = wg_ref[0:1, :] * taps[0][0] + wg_ref[1:2, :] * taps[0][1] + wg_ref[2:3, :] * taps[0][2] + bg_ref[...]
        u_v = wv_ref[0:1, :] * taps[1][0] + wv_ref[1:2, :] * taps[1][1] + wv_ref[2:3, :] * taps[1][2] + bv_ref[...]
        sg = 1.0 / (1.0 + jnp.exp(-u_g))
        du_g = da * u_v * (sg * (1.0 + u_g * (1.0 - sg)))
        du_v = da * (u_g * sg)
        dug_ref[...] = du_g.astype(BF16)
        duv_ref[...] = du_v.astype(BF16)
        for du, tp, p_ref in ((du_g, taps[0], pg_ref), (du_v, taps[1], pv_ref)):
            for k in range(CONV_W):
                p_ref[k:k + 1, :] += jnp.sum(du * tp[k], axis=0, keepdims=True)
            p_ref[CONV_W:CONV_W + 1, :] += jnp.sum(du, axis=0, keepdims=True)

    main = pl.BlockSpec((tm, ct), lambda j, i: (i, j))
    halo = pl.BlockSpec((HALO, ct), _prev_halo_map(tm))
    wspec = pl.BlockSpec((CONV_W, ct), lambda j, i: (0, j))
    bspec = pl.BlockSpec((1, ct), lambda j, i: (0, j))
    pspec = pl.BlockSpec((8, ct), lambda j, i: (0, j))
    return pl.pallas_call(
        body, name="conv_act_bwd", grid=(nct, S // tm),
        out_shape=(jax.ShapeDtypeStruct((S, F), BF16), jax.ShapeDtypeStruct((S, F), BF16),
                   jax.ShapeDtypeStruct((8, F), F32), jax.ShapeDtypeStruct((8, F), F32)),
        in_specs=[pl.BlockSpec((tm, D), lambda j, i: (i, 0)), pl.BlockSpec((ct, D), lambda j, i: (j, 0)),
                  main, main, halo, halo, wspec, wspec, bspec, bspec],
        out_specs=(main, main, pspec, pspec),
        compiler_params=_params(dimension_semantics=("parallel", "arbitrary")),
    )(gm, w_down, up_g, up_v, up_g, up_v, cwg, cwv, cbg, cbv)


def _conv_bwd_taps(d, halo, w_ref):
    return w_ref[2:3, :] * d + w_ref[1:2, :] * _shift_up(d, halo, 1) + w_ref[0:1, :] * _shift_up(d, halo, 2)


def _scan_mats(R, nc, reverse):
    i = lax.broadcasted_iota(jnp.int32, (LANES, LANES), 0)
    j = lax.broadcasted_iota(jnp.int32, (LANES, LANES), 1)
    inner = ((i >= j) if reverse else (i <= j)).astype(F32)
    r = lax.broadcasted_iota(jnp.int32, (R, R), 0)
    c = lax.broadcasted_iota(jnp.int32, (R, R), 1)
    same = (r // nc) == (c // nc)
    outer = (same & ((c > r) if reverse else (c < r))).astype(F32)
    return inner, outer


def _chunk_scan(v, inner, outer, reverse):
    w = _dot(v, inner, precision=HIGHEST)
    col = 0 if reverse else LANES - 1
    carry = _dot(outer, w, precision=HIGHEST)[:, col:col + 1]
    return w + carry


def _fgate_fwd(z_rows, nc):
    R = z_rows.shape[0]

    def body(z_ref, f_ref):
        z = z_ref[...]
        logf = jnp.minimum(z, 0.0) - jnp.log(1.0 + jnp.exp(-jnp.abs(z)))
        inner, outer = _scan_mats(R, nc, False)
        f_ref[...] = _chunk_scan(logf, inner, outer, False)

    return pl.pallas_call(body, name="fgate_fwd", out_shape=jax.ShapeDtypeStruct((R, LANES), F32),
                          compiler_params=_params())(z_rows)


def _fgate_bwd(dfk_neg_rows, dfq_rows, z_rows, nc):
    R = z_rows.shape[0]
    nh = R // nc

    def body(dfk_ref, dfq_ref, z_ref, dz_ref, db_ref):
        inner, outer = _scan_mats(R, nc, True)
        dlogf = _chunk_scan(dfq_ref[...] - dfk_ref[...], inner, outer, True)
        dz = dlogf * (1.0 / (1.0 + jnp.exp(z_ref[...])))
        dz_ref[...] = dz
        hr = lax.broadcasted_iota(jnp.int32, (nh, R), 0)
        hc = lax.broadcasted_iota(jnp.int32, (nh, R), 1) // nc
        per_head = _dot((hr == hc).astype(F32), dz, precision=HIGHEST)
        db_ref[...] = jnp.sum(per_head, axis=1, keepdims=True)

    return pl.pallas_call(
        body, name="fgate_bwd",
        out_shape=(jax.ShapeDtypeStruct((R, LANES), F32), jax.ShapeDtypeStruct((nh, 1), F32)),
        compiler_params=_params())(dfk_neg_rows, dfq_rows, z_rows)


_NEG = -1e30
SKIP_BELOW = -106.0
_SCALE = HEAD_DIM ** -0.5
F_PARTS = 3
Q_F_LANE = HEAD_DIM
Q_ONE_LANE = HEAD_DIM + F_PARTS


def _kv_slice(k0, nk):
    return pl.ds(pl.multiple_of(k0, SB_BK), nk)


def _mask_t(strict):
    s = lax.broadcasted_iota(jnp.int32, (ATT_BQ, ATT_BQ), 0)
    t = lax.broadcasted_iota(jnp.int32, (ATT_BQ, ATT_BQ), 1)
    return (s < t) if strict else (s <= t)


def _walk_down(i, step, alive, carry, bk):
    carry = step(i * ATT_BQ, ATT_BQ, carry, True)
    first = i * (ATT_BQ // bk)

    def cond(st):
        n, go, _ = st
        return jnp.logical_and(n < first, go)

    def body(st):
        n, _, cr = st
        j = first - 1 - n
        cr = step(j * bk, bk, cr, False)
        return n + 1, alive(jnp.maximum(j - 1, 0), cr), cr

    return lax.while_loop(cond, body, (jnp.int32(0), alive(jnp.maximum(first - 1, 0), carry), carry))[2]


def _t_block(rows):
    return pl.BlockSpec((None, rows, ATT_BQ), lambda h, i, *_: (h, 0, i))


def _t_full(rows, S):
    return pl.BlockSpec((None, rows, S), lambda h, i, *_: (h, 0, 0))


def _n_block():
    return pl.BlockSpec((None, ATT_BQ, LANES), lambda h, i, *_: (h, i, 0))


def _n_full(S):
    return pl.BlockSpec((None, S, LANES), lambda h, i, *_: (h, 0, 0))


def _heads(t):
    S = t.shape[0]
    return jnp.transpose(t.reshape(S, -1, HEAD_DIM), (1, 0, 2))


def _skip_bounds(k_cols, f_rows):
    H, S = f_rows.shape
    f_end = f_rows.reshape(H, S // FOX_BK, FOX_BK)[:, :, -1]
    k_sq = jnp.sum(jnp.square(_heads(k_cols).astype(F32)), axis=-1).reshape(H, S // FOX_BK, FOX_BK)
    k_max = lax.cummax(jnp.sqrt(jnp.max(k_sq, axis=-1)), axis=1)
    return f_end, k_max


def _bf16_parts(f):
    hi = f.astype(BF16).astype(F32)
    mid = (f - hi).astype(BF16).astype(F32)
    return hi, mid, (f - hi - mid).astype(BF16).astype(F32)


def _att_prep(qkv, f_pairs):
    S = qkv.shape[0]
    n_pairs = qkv.shape[1] // (6 * LANES)
    H = 2 * n_pairs
    tm = _tile(S, 512)

    def body(qf_ref, kf_ref, vf_ref, qs_ref, ks_ref, vs_ref, f_ref,
             fqn, fqt, fkn, fkt, fvn, fvt, sqn, sqt, skn, skt, svn, svt):
        lane = lax.broadcasted_iota(jnp.int32, (1, LANES), 1)
        f = f_ref[...]

        def head(ref, e):
            t = ref[...].astype(F32)
            if e == 1:
                t = pltpu.roll(t, HEAD_DIM, 1)
            return jnp.where(lane < HEAD_DIM, t, 0.0)

        def at(first):
            return jnp.logical_and(lane >= first, lane < first + F_PARTS)

        for e in range(2):
            parts = _bf16_parts(f[:, e:e + 1])
            f_lanes = sum(jnp.where(lane == Q_F_LANE + k, parts[k], 0.0) for k in range(F_PARTS))
            nf_lanes = sum(jnp.where(lane == Q_ONE_LANE + k, parts[k], 0.0) for k in range(F_PARTS))
            vals = (
                (fqn, fqt, LANES, head(qf_ref, e) * _SCALE + f_lanes + jnp.where(at(Q_ONE_LANE), 1.0, 0.0)),
                (fkn, fkt, LANES, head(kf_ref, e) + jnp.where(at(Q_F_LANE), 1.0, 0.0) - nf_lanes),
                (fvn, fvt, HEAD_DIM, head(vf_ref, e)),
                (sqn, sqt, LANES, head(qs_ref, e) * _SCALE),
                (skn, skt, LANES, head(ks_ref, e)),
                (svn, svt, HEAD_DIM, head(vs_ref, e)),
            )
            for n_ref, t_ref, rows, val in vals:
                n_ref[e] = val.astype(BF16)
                t_ref[e] = val.T[:rows].astype(BF16)

    col = lambda base: pl.BlockSpec((tm, LANES), lambda i, p: (i, base + p))
    n_spec = pl.BlockSpec((2, tm, LANES), lambda i, p: (p, i, 0))
    t_spec = lambda rows: pl.BlockSpec((2, rows, tm), lambda i, p: (p, 0, i))
    n_sds = jax.ShapeDtypeStruct((H, S, LANES), BF16)
    t_sds = lambda rows: jax.ShapeDtypeStruct((H, rows, S), BF16)
    group = ([n_sds, t_sds(LANES), n_sds, t_sds(LANES), n_sds, t_sds(HEAD_DIM)],
             [n_spec, t_spec(LANES), n_spec, t_spec(LANES), n_spec, t_spec(HEAD_DIM)])
    res = pl.pallas_call(
        body, name="att_prep", grid=(S // tm, n_pairs),
        out_shape=tuple(group[0] * 2),
        in_specs=[col(k * n_pairs) for k in range(6)] + [pl.BlockSpec((None, tm, 2), lambda i, p: (p, i, 0))],
        out_specs=tuple(group[1] * 2),
        compiler_params=_params(dimension_semantics=("parallel", "parallel")),
    )(qkv, qkv, qkv, qkv, qkv, qkv, f_pairs)
    names = ("q_n", "q_t", "k_n", "k_t", "v_n", "v_t")
    return dict(zip(names, res[:6])), dict(zip(names, res[6:]))


def _fox_reach(qt, fend_ref, kmax_ref, h):
    qf = qt.astype(F32)
    q_norm = jnp.sqrt(jnp.sum(jnp.square(qf[:HEAD_DIM]), axis=0, keepdims=True))
    f_t = jnp.sum(qf[Q_F_LANE:Q_F_LANE + F_PARTS], axis=0, keepdims=True)
    return lambda j: q_norm * kmax_ref[h, j] + f_t - fend_ref[h, j]


def _fox_fwd(q_t, k_n, v_t, f_end, k_max, shards):
    H, _, S = q_t.shape
    n, nq = len(shards), S // ATT_BQ

    def body(fend_ref, kmax_ref, qt_ref, k_ref, vt_ref, *rest):
        ins, (ot_ref, lse_ref), outs, sems = rest[:n], rest[n:n + 2], rest[n + 2:2 * n + 2], rest[2 * n + 2:]
        h, i = pl.program_id(0), pl.program_id(1)

        @pl.when(jnp.logical_and(h == 0, i == 0))
        def _():
            for cp in _exchange_copies(ins, outs, *sems, False):
                cp.start()

        qt = qt_ref[...]
        reach = _fox_reach(qt, fend_ref, kmax_ref, h)

        def step(k0, nk, carry, masked):
            m, l, acc = carry
            ks = _kv_slice(k0, nk)
            s = _dot(k_ref[ks, :], qt)
            if masked:
                s = jnp.where(_mask_t(False), s, _NEG)
            mn = jnp.maximum(m, jnp.max(s, axis=0, keepdims=True))
            alpha = jnp.exp(m - mn)
            p = jnp.exp(s - mn)
            l = alpha * l + jnp.sum(p, axis=0, keepdims=True)
            acc = acc * alpha + _dot(vt_ref[:, ks], p.astype(BF16))
            return mn, l, acc

        def alive(j, carry):
            return jnp.max(reach(j) - carry[0]) > SKIP_BELOW

        row = jnp.zeros((1, ATT_BQ), F32)
        m, l, acc = _walk_down(i, step, alive, (row + _NEG, row, jnp.zeros((HEAD_DIM, ATT_BQ), F32)), FOX_BK)
        ot_ref[...] = acc / l
        lse_ref[...] = m + jnp.log(l)

        @pl.when(jnp.logical_and(h == H - 1, i == nq - 1))
        def _():
            for cp in _exchange_copies(ins, outs, *sems, False):
                cp.wait()

    any_spec = pl.BlockSpec(memory_space=pl.ANY)
    grid_spec = pltpu.PrefetchScalarGridSpec(
        num_scalar_prefetch=2, grid=(H, nq),
        in_specs=[_t_block(LANES), _n_full(S), _t_full(HEAD_DIM, S)] + [any_spec] * n,
        out_specs=tuple([_t_block(HEAD_DIM), _t_block(1)] + [any_spec] * n),
        scratch_shapes=_exchange_sems(n))
    res = pl.pallas_call(
        body, name="fox_fwd", grid_spec=grid_spec,
        out_shape=tuple([jax.ShapeDtypeStruct((H, HEAD_DIM, S), F32), jax.ShapeDtypeStruct((H, 1, S), F32)]
                        + _exchange_out_shapes(shards, False)),
        compiler_params=_params(dimension_semantics=("arbitrary", "arbitrary"), has_side_effects=True),
    )(f_end, k_max, q_t, k_n, v_t, *shards)
    return res[0], res[1], res[2:]


def _fox_bwd(q_t, q_n, k_n, k_t, v_n, do_t, do_n, o_t, lse, f_end, k_max):
    H, _, S = q_t.shape

    def body(fend_ref, kmax_ref, qt_ref, qn_ref, k_ref, kt_ref, v_ref, dot_ref, don_ref, ot_ref, lse_ref,
             dqt_ref, dk_ref, dv_ref):
        h, i = pl.program_id(0), pl.program_id(1)

        @pl.when(i == 0)
        def _():
            dk_ref[...] = jnp.zeros_like(dk_ref)
            dv_ref[...] = jnp.zeros_like(dv_ref)

        qt, qn, dot, don = qt_ref[...], qn_ref[...], dot_ref[...], don_ref[...]
        lse = lse_ref[...]
        delta = jnp.sum(dot[:HEAD_DIM].astype(F32) * ot_ref[...], axis=0, keepdims=True)
        reach = _fox_reach(qt, fend_ref, kmax_ref, h)

        def alive(j, dq):
            return jnp.max(reach(j) - lse) > SKIP_BELOW

        def step(k0, nk, dq, masked):
            ks = _kv_slice(k0, nk)
            s = _dot(k_ref[ks, :], qt)
            if masked:
                s = jnp.where(_mask_t(False), s, _NEG)
            p = jnp.exp(s - lse)
            ds = (p * (_dot(v_ref[ks, :], dot) - delta)).astype(BF16)
            dk_ref[ks, :] += _dot(ds, qn)
            dv_ref[ks, :] += _dot(p.astype(BF16), don)
            return dq + _dot(kt_ref[:, ks], ds)

        dqt_ref[...] = _walk_down(i, step, alive, jnp.zeros((LANES, ATT_BQ), F32), FOX_BK)

    grid_spec = pltpu.PrefetchScalarGridSpec(
        num_scalar_prefetch=2, grid=(H, S // ATT_BQ),
        in_specs=[_t_block(LANES), _n_block(), _n_full(S), _t_full(LANES, S), _n_full(S),
                  _t_block(LANES), _n_block(), _t_block(HEAD_DIM), _t_block(1)],
        out_specs=(_t_block(LANES), _n_full(S), _n_full(S)))
    return pl.pallas_call(
        body, name="fox_bwd", grid_spec=grid_spec,
        out_shape=(jax.ShapeDtypeStruct((H, LANES, S), F32), jax.ShapeDtypeStruct((H, S, LANES), F32),
                   jax.ShapeDtypeStruct((H, S, LANES), F32)),
        compiler_params=_params(dimension_semantics=("parallel", "arbitrary")),
    )(f_end, k_max, q_t, q_n, k_n, k_t, v_n, do_t, do_n, o_t, lse)


def _scan_lhs():
    r = lax.broadcasted_iota(jnp.int32, (SCAN_BK, 2 * SCAN_BK), 0)
    c = lax.broadcasted_iota(jnp.int32, (SCAN_BK, 2 * SCAN_BK), 1) % SCAN_BK
    return (c >= r).astype(BF16)


def _suffix_sum(t, lhs):
    hi = t.astype(BF16)
    lo = (t - hi.astype(F32)).astype(BF16)
    return _dot(lhs, jnp.concatenate([hi, lo], axis=0))


def _sb_scores(k, qt, mask):
    z = _dot(k, qt)
    e = jnp.exp(-jnp.abs(z))
    lb = -(jnp.maximum(z, 0.0) + jnp.log(1.0 + e))
    if mask is not None:
        lb = jnp.where(mask, lb, 0.0)
    return z, e, lb


def _scan_blocks(nk):
    return [slice(u * SCAN_BK, (u + 1) * SCAN_BK) for u in reversed(range(nk // SCAN_BK))]


def _sb_fwd(q_t, k_n, v_t):
    H, _, S = q_t.shape

    def body(qt_ref, k_ref, vt_ref, ot_ref):
        i = pl.program_id(1)
        qt = qt_ref[...]
        lhs = _scan_lhs()

        def step(k0, nk, carry, masked):
            c, acc = carry
            ks = _kv_slice(k0, nk)
            mask = _mask_t(True) if masked else None
            z, _, lb = _sb_scores(k_ref[ks, :], qt, mask)
            parts = []
            for sl in _scan_blocks(nk):
                rin = _suffix_sum(lb[sl], lhs)
                a = jnp.exp(z[sl] + rin + c)
                if masked:
                    a = jnp.where(mask[sl], a, 0.0)
                parts.append(a.astype(BF16))
                c = c + rin[0:1, :]
            a_all = jnp.concatenate(parts[::-1], axis=0)
            return c, acc + _dot(vt_ref[:, ks], a_all)

        carry = (jnp.zeros((1, ATT_BQ), F32), jnp.zeros((HEAD_DIM, ATT_BQ), F32))
        ot_ref[...] = _walk_down(i, step, lambda j, cr: jnp.max(cr[0]) > SKIP_BELOW, carry, SB_BK)[1]

    return pl.pallas_call(
        body, name="sb_fwd", grid=(H, S // ATT_BQ),
        out_shape=jax.ShapeDtypeStruct((H, HEAD_DIM, S), F32),
        in_specs=[_t_block(LANES), _n_full(S), _t_full(HEAD_DIM, S)],
        out_specs=_t_block(HEAD_DIM),
        compiler_params=_params(dimension_semantics=("parallel", "parallel")),
    )(q_t, k_n, v_t)


def _sb_bwd(q_t, q_n, k_n, k_t, v_n, do_t, do_n, o_t, bound):
    H, _, S = q_t.shape
    n, nq = len(bound), S // ATT_BQ

    def body(qt_ref, qn_ref, k_ref, kt_ref, v_ref, dot_ref, don_ref, ot_ref, *rest):
        ins, (dqt_ref, dk_ref, dv_ref) = rest[:n], rest[n:n + 3]
        outs, sems = rest[n + 3:2 * n + 3], rest[2 * n + 3:]
        h, i = pl.program_id(0), pl.program_id(1)

        @pl.when(jnp.logical_and(h == 0, i == 0))
        def _():
            for cp in _exchange_copies(ins, outs, *sems, True):
                cp.start()

        @pl.when(i == 0)
        def _():
            dk_ref[...] = jnp.zeros_like(dk_ref)
            dv_ref[...] = jnp.zeros_like(dv_ref)

        qt, qn, dot, don = qt_ref[...], qn_ref[...], dot_ref[...], don_ref[...]
        lhs = _scan_lhs()
        delta = jnp.sum(dot[:HEAD_DIM].astype(F32) * ot_ref[...], axis=0, keepdims=True)

        def step(k0, nk, carry, masked):
            c, g, dq = carry
            ks = _kv_slice(k0, nk)
            mask = _mask_t(True) if masked else None
            z, e, lb = _sb_scores(k_ref[ks, :], qt, mask)
            da = _dot(v_ref[ks, :], dot)
            a_parts, dz_parts = [], []
            for sl in _scan_blocks(nk):
                rin = _suffix_sum(lb[sl], lhs)
                a = jnp.exp(z[sl] + rin + c)
                if masked:
                    a = jnp.where(mask[sl], a, 0.0)
                ab = a.astype(BF16)
                gg = ab.astype(F32) * da[sl]
                rgin = _suffix_sum(gg, lhs)
                rinv = 1.0 / (1.0 + e[sl])
                sig = jnp.where(z[sl] >= 0.0, rinv, e[sl] * rinv)
                dz = gg - sig * (delta - g - (rgin - gg))
                if masked:
                    dz = jnp.where(mask[sl], dz, 0.0)
                a_parts.append(ab)
                dz_parts.append(dz.astype(BF16))
                c = c + rin[0:1, :]
                g = g + rgin[0:1, :]
            ab_all = jnp.concatenate(a_parts[::-1], axis=0)
            dzb = jnp.concatenate(dz_parts[::-1], axis=0)
            dk_ref[ks, :] += _dot(dzb, qn)
            dv_ref[ks, :] += _dot(ab_all, don)
            return c, g, dq + _dot(kt_ref[:, ks], dzb)

        row = jnp.zeros((1, ATT_BQ), F32)
        carry = (row, row, jnp.zeros((LANES, ATT_BQ), F32))
        dqt_ref[...] = _walk_down(i, step, lambda j, cr: jnp.max(cr[0]) > SKIP_BELOW, carry, SB_BK)[2]

        @pl.when(jnp.logical_and(h == H - 1, i == nq - 1))
        def _():
            for cp in _exchange_copies(ins, outs, *sems, True):
                cp.wait()

    any_spec = pl.BlockSpec(memory_space=pl.ANY)
    res = pl.pallas_call(
        body, name="sb_bwd", grid=(H, nq),
        out_shape=tuple([jax.ShapeDtypeStruct((H, LANES, S), F32), jax.ShapeDtypeStruct((H, S, LANES), F32),
                         jax.ShapeDtypeStruct((H, S, LANES), F32)] + _exchange_out_shapes(bound, True)),
        in_specs=[_t_block(LANES), _n_block(), _n_full(S), _t_full(LANES, S), _n_full(S),
                  _t_block(LANES), _n_block(), _t_block(HEAD_DIM)] + [any_spec] * n,
        out_specs=tuple([_t_block(LANES), _n_full(S), _n_full(S)] + [any_spec] * n),
        scratch_shapes=_exchange_sems(n),
        compiler_params=_params(dimension_semantics=("arbitrary", "arbitrary"), has_side_effects=True),
    )(q_t, q_n, k_n, k_t, v_n, do_t, do_n, o_t, *bound)
    return res[0], res[1], res[2], res[3:]


def _dqkv_assemble(dqf_t, dkf, dvf, dqs_t, dks, dvs):
    H, _, S = dqf_t.shape
    n_pairs = H // 2
    tm = _tile(S, 512)

    def body(dqf_ref, dkf_ref, dvf_ref, dqs_ref, dks_ref, dvs_ref, out_ref, dfk_ref):
        lane = lax.broadcasted_iota(jnp.int32, (1, LANES), 1)
        slabs = ((dqf_ref, True), (dkf_ref, False), (dvf_ref, False),
                 (dqs_ref, True), (dks_ref, False), (dvs_ref, False))
        for k, (ref, transposed) in enumerate(slabs):
            if transposed:
                t0, t1 = ref[0].T * _SCALE, ref[1].T * _SCALE
            else:
                t0, t1 = ref[0], ref[1]
            out_ref[k] = jnp.where(lane < HEAD_DIM, t0, pltpu.roll(t1, HEAD_DIM, 1)).astype(BF16)
        for e in range(2):
            dfk_ref[e] = dkf_ref[e].T[Q_ONE_LANE:Q_ONE_LANE + 1, :]

    t_spec = pl.BlockSpec((2, LANES, tm), lambda i, p: (p, 0, i))
    n_spec = pl.BlockSpec((2, tm, LANES), lambda i, p: (p, i, 0))
    return pl.pallas_call(
        body, name="dqkv_assemble", grid=(S // tm, n_pairs),
        out_shape=(jax.ShapeDtypeStruct((6, S, n_pairs * LANES), BF16), jax.ShapeDtypeStruct((H, 1, S), F32)),
        in_specs=[t_spec, n_spec, n_spec, t_spec, n_spec, n_spec],
        out_specs=(pl.BlockSpec((6, tm, LANES), lambda i, p: (0, i, p)),
                   pl.BlockSpec((2, 1, tm), lambda i, p: (p, 0, i))),
        compiler_params=_params(dimension_semantics=("parallel", "parallel")),
    )(dqf_t, dkf, dvf, dqs_t, dks, dvs)


def _acc_spec(w):
    return pl.BlockSpec((1, w), lambda i: (0, 0))


def _loss_head(x1, act, w_down, gate_m, g_final, target):
    S, D = x1.shape
    F = act.shape[1]
    tm = _tile(S, 256)

    def body(x1_ref, act_ref, w_ref, gt_ref, gf_ref, tg_ref, dx2_ref, gm_ref, loss_ref, dgf_ref, dgt_ref):
        @pl.when(pl.program_id(0) == 0)
        def _():
            loss_ref[...] = jnp.zeros_like(loss_ref)
            dgf_ref[...] = jnp.zeros_like(dgf_ref)
            dgt_ref[...] = jnp.zeros_like(dgt_ref)

        mo = _dot(act_ref[...], w_ref[...])
        x2 = x1_ref[...] + gt_ref[...] * mo
        r = lax.rsqrt(jnp.mean(x2 * x2, axis=-1, keepdims=True) + EPS)
        xh = x2 * r
        diff = xh * gf_ref[...] - tg_ref[...]
        loss_ref[...] += (0.5 / D) * jnp.sum(diff * diff)
        dy = diff * (1.0 / D)
        dgf_ref[...] += jnp.sum(dy * xh, axis=0, keepdims=True)
        dxh = dy * gf_ref[...]
        dx2 = r * (dxh - xh * jnp.mean(dxh * xh, axis=-1, keepdims=True))
        dx2_ref[...] = dx2
        gm_ref[...] = (dx2 * gt_ref[...]).astype(BF16)
        dgt_ref[...] += jnp.sum(dx2 * mo, axis=0, keepdims=True)

    return pl.pallas_call(
        body, name="loss_head", grid=(S // tm,),
        out_shape=(jax.ShapeDtypeStruct((S, D), F32), jax.ShapeDtypeStruct((S, D), BF16),
                   jax.ShapeDtypeStruct((1, LANES), F32), jax.ShapeDtypeStruct((1, D), F32),
                   jax.ShapeDtypeStruct((1, D), F32)),
        in_specs=_row_specs(tm, [D, F]) + [pl.BlockSpec((F, D), lambda i: (0, 0))] + [_vec_spec(D)] * 2
        + _row_specs(tm, [D]),
        out_specs=tuple(_row_specs(tm, [D, D]) + [_acc_spec(LANES), _acc_spec(D), _acc_spec(D)]),
        compiler_params=_params(dimension_semantics=("arbitrary",)),
    )(x1, act, w_down, gate_m, g_final, target)


def _norm_bwd(lhs, rhs, xin, dres, g, scale, name, gate=None, branch=None, bound=(), conv=None):
    S, D = xin.shape
    tm = _tile(S, 256)
    gated = gate is not None
    nl, nb, n_steps = len(lhs), len(bound), S // tm
    nc_ = nl if conv else 0
    n_out = (6 if gated else 4) + nc_

    def body(*refs):
        l_refs, r_refs, rest = refs[:nl], refs[nl:2 * nl], refs[2 * nl:]
        halo_refs, cw_refs, rest = rest[:nc_], rest[nc_:2 * nc_], rest[2 * nc_:]
        if gated:
            x_ref, dr_ref, g_ref, sc_ref, gt_ref, br_ref = rest[:6]
            rest = rest[6:]
        else:
            x_ref, dr_ref, g_ref, sc_ref = rest[:4]
            rest = rest[4:]
        ins, outs, ex_outs, sems = rest[:nb], rest[nb:nb + n_out], rest[nb + n_out:2 * nb + n_out], rest[2 * nb + n_out:]
        dx_ref, dsc_ref, dsh_ref, dg_ref = outs[:4]
        sums = (dsc_ref, dsh_ref, dg_ref) + ((outs[5],) if gated else ())
        dup_refs = outs[n_out - nc_:]
        i = pl.program_id(0)

        @pl.when(i == 0)
        def _():
            for s_ref in sums:
                s_ref[...] = jnp.zeros_like(s_ref)
            if nb:
                for cp in _chip_exchange_copies(ins, ex_outs, *sems):
                    cp.start()

        dhv = None
        for k, (l_ref, r_ref) in enumerate(zip(l_refs, r_refs)):
            if conv:
                halo = jnp.where(i == n_steps - 1, 0.0, halo_refs[k][...].astype(F32))
                dup = _conv_bwd_taps(l_ref[...].astype(F32), halo, cw_refs[k]).astype(BF16)
                dup_refs[k][...] = dup
                terms = [_dot_nt(dup, r_ref[...])]
            elif len(l_ref.shape) == 3:
                K = l_ref.shape[2]
                terms = [_dot_nt(l_ref[k], r_ref[:, k * K:(k + 1) * K]) for k in range(l_ref.shape[0])]
            else:
                terms = [_dot_nt(l_ref[...], r_ref[...])]
            for t in terms:
                dhv = t if dhv is None else dhv + t
        xv = x_ref[...]
        r = lax.rsqrt(jnp.mean(xv * xv, axis=-1, keepdims=True) + EPS)
        xh = xv * r
        dsc_ref[...] += jnp.sum(dhv * (xh * g_ref[...]), axis=0, keepdims=True)
        dsh_ref[...] += jnp.sum(dhv, axis=0, keepdims=True)
        dn = dhv * (1.0 + sc_ref[...])
        dg_ref[...] += jnp.sum(dn * xh, axis=0, keepdims=True)
        dxh = dn * g_ref[...]
        dx = dr_ref[...] + r * (dxh - xh * jnp.mean(dxh * xh, axis=-1, keepdims=True))
        dx_ref[...] = dx
        if gated:
            outs[4][...] = (dx * gt_ref[...]).astype(BF16)
            outs[5][...] += jnp.sum(dx * br_ref[...], axis=0, keepdims=True)

        if nb:
            @pl.when(i == n_steps - 1)
            def _():
                for cp in _chip_exchange_copies(ins, ex_outs, *sems):
                    cp.wait()

    def l_spec(a):
        if a.ndim == 3:
            return pl.BlockSpec((a.shape[0], tm, a.shape[2]), lambda i: (0, i, 0))
        return pl.BlockSpec((tm, a.shape[1]), lambda i: (i, 0))

    any_spec = pl.BlockSpec(memory_space=pl.ANY)
    vec = jax.ShapeDtypeStruct((1, D), F32)
    out_shape = [jax.ShapeDtypeStruct((S, D), F32), vec, vec, vec]
    out_specs = _row_specs(tm, [D]) + [_acc_spec(D)] * 3
    in_specs = [l_spec(a) for a in lhs] + [pl.BlockSpec(b.shape, lambda i: (0, 0)) for b in rhs]
    args = list(lhs) + list(rhs)
    if conv:
        step, last_halo = tm // HALO, S // HALO - 1
        in_specs += [pl.BlockSpec((HALO, a.shape[1]), lambda i: (jnp.minimum((i + 1) * step, last_halo), 0))
                     for a in lhs]
        in_specs += [pl.BlockSpec(w.shape, lambda i: (0, 0)) for w in conv]
        args += list(lhs) + list(conv)
    in_specs += _row_specs(tm, [D, D]) + [_vec_spec(D)] * 2
    args += [xin, dres, g, scale]
    if gated:
        out_shape += [jax.ShapeDtypeStruct((S, D), BF16), vec]
        out_specs += _row_specs(tm, [D]) + [_acc_spec(D)]
        in_specs += [_vec_spec(D)] + _row_specs(tm, [D])
        args += [gate, branch]
    if conv:
        out_shape += [jax.ShapeDtypeStruct(a.shape, BF16) for a in lhs]
        out_specs += [l_spec(a) for a in lhs]
    res = pl.pallas_call(
        body, name=name, grid=(n_steps,),
        out_shape=tuple(out_shape + [jax.ShapeDtypeStruct(b.shape, b.dtype) for b in bound]),
        in_specs=in_specs + [any_spec] * nb, out_specs=tuple(out_specs + [any_spec] * nb),
        scratch_shapes=_chip_exchange_sems(nb) if nb else [],
        compiler_params=_params(dimension_semantics=("arbitrary",), has_side_effects=bool(nb)),
    )(*args, *bound)
    return tuple(res[:n_out]) + (tuple(res[n_out:]),)


def _headnorm_bwd(dmix, o_f, o_s, g_f, g_s):
    dh, S = o_f.shape
    H = dh // HEAD_DIM
    tm = _tile(S, 256)

    def body(dm_ref, of_ref, os_ref, gf_ref, gs_ref, fn_ref, ft_ref, sn_ref, st_ref, dgf_ref, dgs_ref):
        @pl.when(pl.program_id(0) == 0)
        def _():
            dgf_ref[...] = jnp.zeros_like(dgf_ref)
            dgs_ref[...] = jnp.zeros_like(dgs_ref)

        ones = _group_ones()
        lane = lax.broadcasted_iota(jnp.int32, (1, LANES), 1)
        parts = ((of_ref, gf_ref, fn_ref, ft_ref, dgf_ref), (os_ref, gs_ref, sn_ref, st_ref, dgs_ref))
        for part, (o_ref, g_ref, n_ref, t_ref, dg_ref) in enumerate(parts):
            for t in range(dh // LANES):
                cols = slice(t * LANES, (t + 1) * LANES)
                o = o_ref[cols, :].T
                dm = dm_ref[:, part * dh + t * LANES: part * dh + (t + 1) * LANES]
                r = lax.rsqrt(_dot(o * o, ones, precision=HIGHEST) * (1.0 / HEAD_DIM) + EPS)
                oh = o * r
                dg_ref[:, cols] += jnp.sum(dm * oh, axis=0, keepdims=True)
                dn = dm * g_ref[:, cols]
                mean = _dot(dn * oh, ones, precision=HIGHEST) * (1.0 / HEAD_DIM)
                do = r * (dn - oh * mean)
                for e in range(2):
                    d = do if e == 0 else pltpu.roll(do, HEAD_DIM, 1)
                    d = jnp.where(lane < HEAD_DIM, d, 0.0)
                    n_ref[2 * t + e] = d.astype(BF16)
                    t_ref[2 * t + e] = d.T.astype(BF16)

    vec = jax.ShapeDtypeStruct((1, dh), F32)
    n_sds = jax.ShapeDtypeStruct((H, S, LANES), BF16)
    t_sds = jax.ShapeDtypeStruct((H, LANES, S), BF16)
    n_spec = pl.BlockSpec((H, tm, LANES), lambda i: (0, i, 0))
    t_spec = pl.BlockSpec((H, LANES, tm), lambda i: (0, 0, i))
    return pl.pallas_call(
        body, name="headnorm_bwd", grid=(S // tm,),
        out_shape=(n_sds, t_sds, n_sds, t_sds, vec, vec),
        in_specs=_row_specs(tm, [2 * dh]) + [_col_spec(tm, dh)] * 2 + [_vec_spec(dh)] * 2,
        out_specs=(n_spec, t_spec, n_spec, t_spec, _acc_spec(dh), _acc_spec(dh)),
        compiler_params=_params(dimension_semantics=("arbitrary",)),
    )(dmix, o_f, o_s, g_f, g_s)


def _adamw(w, gslots, m, v, name):
    R, C = w.shape
    n = gslots.shape[0]
    tr = 256 if (R % 256 == 0 and R > 256) else R
    bc1 = 1.0 - ADAM_B1 ** ADAM_STEP
    bc2 = 1.0 - ADAM_B2 ** ADAM_STEP

    def body(w_ref, gs_ref, m_ref, v_ref, g_ref, d_ref, nm_ref, nv_ref):
        g = gs_ref[0]
        for s in range(1, n):
            g = g + gs_ref[s]
        nm = ADAM_B1 * m_ref[...] + (1.0 - ADAM_B1) * g
        nv = ADAM_B2 * v_ref[...] + (1.0 - ADAM_B2) * (g * g)
        g_ref[...] = g
        nm_ref[...] = nm
        nv_ref[...] = nv
        d_ref[...] = -ADAM_LR * ((nm / bc1) / (jnp.sqrt(nv / bc2) + ADAM_EPS) + ADAM_WD * w_ref[...])

    blk = pl.BlockSpec((tr, C), lambda i: (i, 0))
    sds = jax.ShapeDtypeStruct((R, C), F32)
    return pl.pallas_call(
        body, name=name, grid=(R // tr,), out_shape=(sds,) * 4,
        in_specs=[blk, pl.BlockSpec((n, tr, C), lambda i: (0, i, 0)), blk, blk], out_specs=(blk,) * 4,
        compiler_params=_params(dimension_semantics=("parallel",)),
    )(w, gslots, m, v)


def _slot_sum(slots, name):
    n, _, C = slots.shape

    def body(s_ref, o_ref):
        acc = s_ref[0]
        for s in range(1, n):
            acc = acc + s_ref[s]
        o_ref[...] = acc

    return pl.pallas_call(body, name=name, out_shape=jax.ShapeDtypeStruct((1, C), F32),
                          compiler_params=_params())(slots)


def _pad_cols(a, n):
    return jnp.pad(a, ((0, 0), (0, n - a.shape[1])))


def _ungather(g, axis):
    if axis == 0:
        return g.reshape(g.shape[0] * g.shape[1], g.shape[2])
    return jnp.transpose(g, (1, 0, 2)).reshape(g.shape[1], g.shape[0] * g.shape[2])


def _to_slots(full, axis):
    R, C = full.shape
    if axis == 0:
        return full.reshape(N_DEV, R // N_DEV, C)
    return jnp.transpose(full.reshape(R, N_DEV, C // N_DEV), (1, 0, 2))


def kernel(x, c, w_ada, b_ada, g_attn, w_in, b_fgate, g_out_fox, g_out_sb, w_out, g_mlp, w_up, conv_w, conv_b, w_down, g_final, loss_target, m_w_ada, m_b_ada, m_g_attn, m_w_in, m_b_fgate, m_g_out_fox, m_g_out_sb, m_w_out, m_g_mlp, m_w_up, m_conv_w, m_conv_b, m_w_down, m_g_final, v_w_ada, v_b_ada, v_g_attn, v_w_in, v_b_fgate, v_g_out_fox, v_g_out_sb, v_w_out, v_g_mlp, v_w_up, v_conv_w, v_conv_b, v_w_down, v_g_final):
    S, D = x.shape[1], x.shape[2]
    dh = D // 2
    n_heads = dh // HEAD_DIM
    n_qkv = 6 * dh
    ff = w_down.shape[1] * N_DEV
    ffp = -(-ff // (2 * LANES)) * (2 * LANES)
    nc = S // LANES
    me = 4 * lax.axis_index("x") + 2 * lax.axis_index("y") + lax.axis_index("c")
    xs, tgt = x[0], loss_target[0]

    (c_all,) = _exchange([c], scatter=False, name="gather_c")
    c_all = c_all.reshape(N_DEV, D)
    cb_g, cb_v = _pad_cols(conv_b[:, :ff], ffp), _pad_cols(conv_b[:, ff:], ffp)

    n_ada = w_ada.shape[2]
    b_shard = lax.dynamic_slice(b_ada, (0, me * n_ada), (1, n_ada))
    mod_cols = _ada_fwd(c_all, w_ada[0], b_shard)
    (mod_g,) = _exchange([mod_cols], scatter=False, name="gather_mod")
    mod = lax.dynamic_index_in_dim(mod_g, me, axis=1, keepdims=False).reshape(6, 1, D)
    shift_a, scale_a, gate_a, shift_m, scale_m, gate_m = [mod[k] for k in range(6)]

    h1, h1_t, (win_g,) = _prenorm(xs, g_attn, scale_a, shift_a, "prenorm_attn", shards=[w_in[0].astype(BF16)])
    W_in = _ungather(win_g, 1)
    W_qkv, W_f = W_in[:, :n_qkv], _pad_cols(W_in[:, n_qkv:], LANES)
    qkv = _mm(h1, W_qkv, BF16, "proj_qkv")
    flog = _mm(h1, W_f, F32, "proj_fgate")
    zf = flog[:, :n_heads] + b_fgate
    z_rows = zf.T.reshape(n_heads * nc, LANES)
    f_rows = _fgate_fwd(z_rows, nc).reshape(n_heads, S)
    f_pairs = jnp.transpose(f_rows.reshape(n_heads // 2, 2, S), (0, 2, 1))
    fox, sb = _att_prep(qkv, f_pairs)
    f_end, k_max = _skip_bounds(qkv[:, dh:2 * dh], f_rows)
    of_t, lse, (wout_g, wup_g, wdown_g, convw_g) = _fox_fwd(
        fox["q_t"], fox["k_n"], fox["v_t"], f_end, k_max,
        [w_out[0].astype(BF16), w_up[0].astype(BF16), w_down[0].astype(BF16), conv_w[0]])
    W_out = _ungather(wout_g, 0)
    W_up = _ungather(wup_g, 1)
    W_g, W_v = _pad_cols(W_up[:, :ff], ffp), _pad_cols(W_up[:, ff:], ffp)
    W_down = jnp.pad(_ungather(wdown_g, 0), ((0, ffp - ff), (0, 0)))
    cw_full = _ungather(convw_g, 1)
    cw_g, cw_v = _pad_cols(cw_full[:, :ff], ffp), _pad_cols(cw_full[:, ff:], ffp)
    os_t = _sb_fwd(sb["q_t"], sb["k_n"], sb["v_t"])
    o_f, o_s = of_t.reshape(dh, S), os_t.reshape(dh, S)
    mix, mix_t = _headnorm_fwd(o_f, o_s, g_out_fox, g_out_sb)
    a_out = _mm(mix, W_out, F32, "proj_out")
    x1, h2, h2_t = _resid_prenorm(xs, a_out, gate_a, g_mlp, scale_m, shift_m)
    up_g, up_v, act, act_t = _mlp_up(h2, W_g, W_v, cw_g, cw_v, cb_g, cb_v)

    dx2, gm, loss_p, dg_final, dgate_m = _loss_head(x1, act, W_down, gate_m, g_final.reshape(1, D), tgt)
    dW_down = _mm_acc(act_t, gm, "bwd_down_w")
    du_g, du_v, p_g, p_v = _conv_act_bwd(gm, W_down, up_g, up_v, cw_g, cw_v, cb_g, cb_v)
    dx1, dscale_m, dshift_m, dg_mlp, ga, dgate_a, dup_g, dup_v, _ = _norm_bwd(
        [du_g, du_v], [W_g, W_v], x1, dx2, g_mlp, scale_m, "norm_mlp_bwd", gate=gate_a, branch=a_out,
        conv=[cw_g, cw_v])
    dW_g = _mm_acc(h2_t, dup_g, "bwd_up_w_g")
    dW_v = _mm_acc(h2_t, dup_v, "bwd_up_w_v")
    dmix = _mm(ga, W_out, F32, "bwd_out_act", nt=True)
    dW_out = _mm_acc(mix_t, ga, "bwd_out_w")
    dof_n, dof_t, dos_n, dos_t, dg_fox, dg_sb = _headnorm_bwd(dmix, o_f, o_s, g_out_fox, g_out_sb)
    dqf_t, dkf, dvf = _fox_bwd(fox["q_t"], fox["q_n"], fox["k_n"], fox["k_t"], fox["v_n"], dof_t, dof_n, of_t, lse,
                              f_end, k_max)
    dW_upf = jnp.concatenate([dW_g[:, :ff], dW_v[:, :ff]], axis=1)
    dcw = jnp.concatenate([p_g[:CONV_W, :ff], p_v[:CONV_W, :ff]], axis=1)
    dqs_t, dks, dvs, (s_out, s_up, s_down, s_cw) = _sb_bwd(
        sb["q_t"], sb["q_n"], sb["k_n"], sb["k_t"], sb["v_n"], dos_t, dos_n, os_t,
        [_to_slots(dW_out, 0), _to_slots(dW_upf, 1), _to_slots(dW_down[:ff], 0), _to_slots(dcw, 1)])
    dparts, dfk = _dqkv_assemble(dqf_t, dkf, dvf, dqs_t, dks, dvs)
    dz_rows, db_fgate = _fgate_bwd(dfk.reshape(n_heads * nc, LANES),
                                   dqf_t[:, Q_F_LANE, :].reshape(n_heads * nc, LANES), z_rows, nc)
    dzf = _pad_cols(dz_rows.reshape(n_heads, S).T, LANES).astype(BF16)
    dW_qkv = _mm_acc_parts(h1_t, dparts, "bwd_in_w")
    dW_f = _mm_acc(h1_t, dzf, "bwd_in_w_fgate")
    dW_in = jnp.concatenate([jnp.transpose(dW_qkv, (1, 0, 2)).reshape(D, n_qkv), dW_f[:, :n_heads]], axis=1)
    bound_in = _to_slots(dW_in, 1)
    bound_in = bound_in.reshape((N_CHIPS, 2) + bound_in.shape[1:])
    (got_in,) = _scatter_to_sibling([bound_in], "scatter_sibling")
    c_idx = lax.axis_index("c").astype(jnp.int32).reshape(1)
    grad_x, dscale_a, dshift_a, dg_attn, (s_in,) = _norm_bwd(
        [dparts, dzf], [W_qkv, W_f], xs, dx1, g_attn, scale_a, "norm_attn_bwd",
        bound=[_pair_add(bound_in, got_in, c_idx, "pair_add")])

    dconv_b = jnp.concatenate([p_g[CONV_W:CONV_W + 1, :ff], p_v[CONV_W:CONV_W + 1, :ff]], axis=1)
    parts = [dshift_a, dscale_a, dgate_a, dshift_m, dscale_m, dgate_m,
             dg_attn, db_fgate.reshape(1, n_heads), dg_fox, dg_sb, dg_mlp, dconv_b, dg_final,
             loss_p[:, :1]]
    sizes = [p.shape[1] for p in parts]
    vec = jnp.concatenate(parts, axis=1)
    n_vec = -(-vec.shape[1] // LANES) * LANES
    vec = _pad_cols(vec, n_vec)
    (vec_g,) = _exchange([vec], scatter=False, name="gather_small")
    offs = [0]
    for s in sizes:
        offs.append(offs[-1] + s)

    def small(k0, k1=None):
        k1 = k0 if k1 is None else k1
        return vec_g[:, :, offs[k0]:offs[k1 + 1]]

    dmod_all = small(0, 5).reshape(N_DEV, 6 * D)
    dmod_cols = lax.dynamic_slice(dmod_all, (0, me * n_ada), (N_DEV, n_ada))
    dW_ada = _ada_bwd(c_all.T, dmod_cols)


    res = {}
    res["w_ada"] = _adamw(w_ada[0], dW_ada[None], m_w_ada[0], v_w_ada[0], "adamw_w_ada")
    res["w_in"] = _adamw(w_in[0], s_in, m_w_in[0], v_w_in[0], "adamw_w_in")
    res["w_out"] = _adamw(w_out[0], s_out, m_w_out[0], v_w_out[0], "adamw_w_out")
    res["w_up"] = _adamw(w_up[0], s_up, m_w_up[0], v_w_up[0], "adamw_w_up")
    res["w_down"] = _adamw(w_down[0], s_down, m_w_down[0], v_w_down[0], "adamw_w_down")
    res["conv_w"] = _adamw(conv_w[0], s_cw, m_conv_w[0], v_conv_w[0], "adamw_conv_w")
    small_names = ["b_ada", "g_attn", "b_fgate", "g_out_fox", "g_out_sb", "g_mlp", "conv_b", "g_final"]
    small_w = [b_ada, g_attn, b_fgate, g_out_fox, g_out_sb, g_mlp, conv_b, g_final.reshape(1, D)]
    small_m = [m_b_ada, m_g_attn, m_b_fgate, m_g_out_fox, m_g_out_sb, m_g_mlp, m_conv_b, m_g_final.reshape(1, D)]
    small_v = [v_b_ada, v_g_attn, v_b_fgate, v_g_out_fox, v_g_out_sb, v_g_mlp, v_conv_b, v_g_final.reshape(1, D)]
    small_res = _adamw(jnp.concatenate(small_w, axis=1), small(0, 12), jnp.concatenate(small_m, axis=1),
                       jnp.concatenate(small_v, axis=1), "adamw_small")
    lo = 0
    for nm, wv in zip(small_names, small_w):
        res[nm] = tuple(r[:, lo:lo + wv.shape[1]] for r in small_res)
        lo += wv.shape[1]
    loss = _slot_sum(_pad_cols(small(13).reshape(N_DEV, 1), LANES).reshape(N_DEV, 1, LANES), "loss_sum")[0, 0]

    names = ["w_ada", "b_ada", "g_attn", "w_in", "b_fgate", "g_out_fox", "g_out_sb", "w_out", "g_mlp",
             "w_up", "conv_w", "conv_b", "w_down", "g_final"]

    def shaped(n, a):
        if n == "g_final":
            return a.reshape(D)
        if n in ("b_ada", "g_attn", "b_fgate", "g_out_fox", "g_out_sb", "g_mlp", "conv_b"):
            return a
        return a[None]

    outs = [loss, grad_x[None]]
    for k in range(4):
        outs += [shaped(n, res[n][k]) for n in names]
    return tuple(outs)
```

```python
import jax
import jax.numpy as jnp
from jax import lax
from jax.experimental import pallas as pl
from jax.experimental.pallas import tpu as pltpu

F32 = jnp.float32
BF16 = jnp.bfloat16
HIGHEST = lax.Precision.HIGHEST

N_DEV = 8
LANES = 128
HEAD_DIM = 64
EPS = 1e-6
CONV_W = 3
CONV_COLS = 1408
HALO = 16
ATT_BQ = 512
FOX_BK = 512
SB_BK = 256
SCAN_BK = 128
VMEM_LIMIT = 56 * 1024 * 1024

ADAM_LR = 0.001
ADAM_B1 = 0.9
ADAM_B2 = 0.999
ADAM_EPS = 1e-08
ADAM_WD = 0.01
ADAM_STEP = 10


def _params(**kw):
    return pltpu.CompilerParams(vmem_limit_bytes=VMEM_LIMIT, **kw)


def _tile(n, cap):
    if n <= cap:
        return n
    best = None
    for t in range(LANES, cap + 1, LANES):
        if n % t == 0:
            best = t
    assert best is not None, (n, cap)
    return best


def _dot(a, b, **kw):
    return jnp.dot(a, b, preferred_element_type=F32, **kw)


def _exchange_copies(ins, outs, send_sems, recv_sems, loc_sems, scatter):
    n = len(ins)
    if n == 0:
        return []
    x, y, c = lax.axis_index("x"), lax.axis_index("y"), lax.axis_index("c")
    me = 4 * x + 2 * y + c
    copies = []
    for a in range(n):
        src = ins[a].at[me] if scatter else ins[a]
        copies.append(pltpu.make_async_copy(src, outs[a].at[me], loc_sems.at[a]))
    for k in range(1, N_DEV):
        px = 1 - x if k & 4 else x
        py = 1 - y if k & 2 else y
        pc = 1 - c if k & 1 else c
        peer = 4 * px + 2 * py + pc
        for a in range(n):
            src = ins[a].at[peer] if scatter else ins[a]
            copies.append(pltpu.make_async_remote_copy(
                src_ref=src, dst_ref=outs[a].at[me],
                send_sem=send_sems.at[a, k - 1], recv_sem=recv_sems.at[a, k - 1],
                device_id=(px, py, pc), device_id_type=pl.DeviceIdType.MESH))
    return copies


def _exchange_out_shapes(arrays, scatter):
    return [jax.ShapeDtypeStruct((N_DEV,) + tuple(a.shape[1:] if scatter else a.shape), a.dtype) for a in arrays]


def _exchange_sems(n):
    return [pltpu.SemaphoreType.DMA((n, N_DEV - 1)), pltpu.SemaphoreType.DMA((n, N_DEV - 1)),
            pltpu.SemaphoreType.DMA((n,))]


def _exchange(arrays, scatter, name):
    n = len(arrays)

    def body(*refs):
        copies = _exchange_copies(refs[:n], refs[n:2 * n], *refs[2 * n:], scatter)
        for cp in copies:
            cp.start()
        for cp in copies:
            cp.wait()

    any_spec = pl.BlockSpec(memory_space=pl.ANY)
    return pl.pallas_call(
        body, name=name, out_shape=tuple(_exchange_out_shapes(arrays, scatter)),
        in_specs=[any_spec] * n, out_specs=tuple([any_spec] * n),
        scratch_shapes=_exchange_sems(n),
        compiler_params=pltpu.CompilerParams(has_side_effects=True),
    )(*arrays)


def _gather_two_level(arrays, name):
    n = len(arrays)
    out_shape = [jax.ShapeDtypeStruct((N_DEV,) + tuple(a.shape), a.dtype) for a in arrays]

    def body(*refs):
        ins, outs = refs[:n], refs[n:2 * n]
        send_sems, recv_sems, loc_sems = refs[2 * n:]
        x, y, c = lax.axis_index("x"), lax.axis_index("y"), lax.axis_index("c")
        me, sibling = (x, y, c), (x, y, 1 - c)
        chips = [(1 - x, y), (x, 1 - y), (1 - x, 1 - y)]

        def slot(px, py, pc):
            return 4 * px + 2 * py + pc

        def copy(a, k, block, to, src=None):
            dst = outs[a].at[slot(*block)]
            return pltpu.make_async_remote_copy(
                src_ref=dst if src is None else src, dst_ref=dst,
                send_sem=send_sems.at[a, k], recv_sem=recv_sems.at[a, k],
                device_id=to, device_id_type=pl.DeviceIdType.MESH)

        local = [pltpu.make_async_copy(ins[a], outs[a].at[slot(*me)], loc_sems.at[a]) for a in range(n)]
        for cp in local:
            cp.start()
        first = []
        for a in range(n):
            first.append(copy(a, 0, me, sibling, src=ins[a]))
            first += [copy(a, 1 + j, me, (*chip, c), src=ins[a]) for j, chip in enumerate(chips)]
        for cp in first:
            cp.start()
        passed = []
        for j, chip in enumerate(chips):
            for a in range(n):
                copy(a, 1 + j, (*chip, c), me).wait_recv()
                cp = copy(a, 4 + j, (*chip, c), sibling)
                cp.start()
                passed.append(cp)
        for a in range(n):
            copy(a, 0, sibling, me).wait_recv()
            for j, chip in enumerate(chips):
                copy(a, 4 + j, (*chip, 1 - c), me).wait_recv()
        for cp in first + passed:
            cp.wait_send()
        for cp in local:
            cp.wait()

    any_spec = pl.BlockSpec(memory_space=pl.ANY)
    return pl.pallas_call(
        body, name=name, out_shape=tuple(out_shape),
        in_specs=[any_spec] * n, out_specs=tuple([any_spec] * n),
        scratch_shapes=[pltpu.SemaphoreType.DMA((n, N_DEV - 1)), pltpu.SemaphoreType.DMA((n, N_DEV - 1)),
                        pltpu.SemaphoreType.DMA((n,))],
        compiler_params=pltpu.CompilerParams(has_side_effects=True),
    )(*arrays)


N_CHIPS = 4


def _scatter_to_sibling(arrays, name):
    n = len(arrays)
    out_shape = [jax.ShapeDtypeStruct((N_CHIPS,) + tuple(a.shape[2:]), a.dtype) for a in arrays]

    def body(*refs):
        ins, outs = refs[:n], refs[n:2 * n]
        send_sems, recv_sems = refs[2 * n:]
        x, y, c = lax.axis_index("x"), lax.axis_index("y"), lax.axis_index("c")
        copies = []
        for a in range(n):
            for q in range(N_CHIPS):
                cp = pltpu.make_async_remote_copy(
                    src_ref=ins[a].at[q, 1 - c], dst_ref=outs[a].at[q],
                    send_sem=send_sems.at[a, q], recv_sem=recv_sems.at[a, q],
                    device_id=(x, y, 1 - c), device_id_type=pl.DeviceIdType.MESH)
                cp.start()
                copies.append(cp)
        for cp in copies:
            cp.wait()

    any_spec = pl.BlockSpec(memory_space=pl.ANY)
    return pl.pallas_call(
        body, name=name, out_shape=tuple(out_shape),
        in_specs=[any_spec] * n, out_specs=tuple([any_spec] * n),
        scratch_shapes=[pltpu.SemaphoreType.DMA((n, N_CHIPS)), pltpu.SemaphoreType.DMA((n, N_CHIPS))],
        compiler_params=pltpu.CompilerParams(has_side_effects=True),
    )(*arrays)


def _pair_add(mine, got, c_idx, name):
    _, _, R, C = mine.shape
    tr = 256 if (R % 256 == 0 and R > 256) else R

    def body(c_ref, m_ref, g_ref, o_ref):
        o_ref[...] = m_ref[...] + g_ref[...]

    grid_spec = pltpu.PrefetchScalarGridSpec(
        num_scalar_prefetch=1, grid=(N_CHIPS, R // tr),
        in_specs=[pl.BlockSpec((None, None, tr, C), lambda q, i, c_ref: (q, c_ref[0], i, 0)),
                  pl.BlockSpec((None, tr, C), lambda q, i, c_ref: (q, i, 0))],
        out_specs=pl.BlockSpec((None, tr, C), lambda q, i, c_ref: (q, i, 0)))
    return pl.pallas_call(
        body, name=name, grid_spec=grid_spec, out_shape=jax.ShapeDtypeStruct((N_CHIPS, R, C), mine.dtype),
        compiler_params=_params(dimension_semantics=("parallel", "parallel")),
    )(c_idx, mine, got)


def _chip_exchange_copies(ins, outs, send_sems, recv_sems, loc_sems):
    n = len(ins)
    x, y, c = lax.axis_index("x"), lax.axis_index("y"), lax.axis_index("c")
    myq = 2 * x + y
    copies = [pltpu.make_async_copy(ins[a].at[myq], outs[a].at[myq], loc_sems.at[a]) for a in range(n)]
    for k in range(1, N_CHIPS):
        qx = 1 - x if k & 2 else x
        qy = 1 - y if k & 1 else y
        for a in range(n):
            copies.append(pltpu.make_async_remote_copy(
                src_ref=ins[a].at[2 * qx + qy], dst_ref=outs[a].at[myq],
                send_sem=send_sems.at[a, k - 1], recv_sem=recv_sems.at[a, k - 1],
                device_id=(qx, qy, c), device_id_type=pl.DeviceIdType.MESH))
    return copies


def _chip_exchange_sems(n):
    return [pltpu.SemaphoreType.DMA((n, N_CHIPS - 1)), pltpu.SemaphoreType.DMA((n, N_CHIPS - 1)),
            pltpu.SemaphoreType.DMA((n,))]


def _dot_nt(a, b):
    return lax.dot_general(a, b, (((1,), (1,)), ((), ())), preferred_element_type=F32)


def _rhs_spec(b, tn, nt):
    if nt:
        return pl.BlockSpec((tn, b.shape[1]), lambda i, j: (j, 0))
    return pl.BlockSpec((b.shape[0], tn), lambda i, j: (0, j))


def _mm(a, b, out_dtype, name, tm=1024, tn=512, nt=False):
    M, K = a.shape
    N = b.shape[0] if nt else b.shape[1]
    tm, tn = _tile(M, tm), _tile(N, tn)
    dot = _dot_nt if nt else _dot

    def body(a_ref, b_ref, o_ref):
        o_ref[...] = dot(a_ref[...], b_ref[...]).astype(out_dtype)

    return pl.pallas_call(
        body, name=name, out_shape=jax.ShapeDtypeStruct((M, N), out_dtype),
        grid=(M // tm, N // tn),
        in_specs=[pl.BlockSpec((tm, K), lambda i, j: (i, 0)), _rhs_spec(b, tn, nt)],
        out_specs=pl.BlockSpec((tm, tn), lambda i, j: (i, j)),
        compiler_params=_params(dimension_semantics=("parallel", "parallel")),
    )(a, b)


def _mm_acc(a, b, name, tm=1408, tn=1408, tk=512):
    M, S = a.shape
    _, N = b.shape
    tm, tn, tk = _tile(M, tm), _tile(N, tn), _tile(S, tk)

    def body(a_ref, b_ref, o_ref):
        @pl.when(pl.program_id(2) == 0)
        def _():
            o_ref[...] = jnp.zeros_like(o_ref)

        o_ref[...] += _dot(a_ref[...], b_ref[...])

    return pl.pallas_call(
        body, name=name, out_shape=jax.ShapeDtypeStruct((M, N), F32),
        grid=(M // tm, N // tn, S // tk),
        in_specs=[pl.BlockSpec((tm, tk), lambda i, j, k: (i, k)), pl.BlockSpec((tk, tn), lambda i, j, k: (k, j))],
        out_specs=pl.BlockSpec((tm, tn), lambda i, j, k: (i, j)),
        compiler_params=_params(dimension_semantics=("parallel", "parallel", "arbitrary")),
    )(a, b)


def _mm_acc_parts(a, parts, name, tm=1024, tk=1024):
    M, S = a.shape
    P, _, K = parts.shape
    tm, tk = _tile(M, tm), _tile(S, tk)

    def body(a_ref, b_ref, o_ref):
        @pl.when(pl.program_id(2) == 0)
        def _():
            o_ref[...] = jnp.zeros_like(o_ref)

        o_ref[...] += _dot(a_ref[...], b_ref[...])

    return pl.pallas_call(
        body, name=name, out_shape=jax.ShapeDtypeStruct((P, M, K), F32), grid=(P, M // tm, S // tk),
        in_specs=[pl.BlockSpec((tm, tk), lambda k, i, s: (i, s)), pl.BlockSpec((None, tk, K), lambda k, i, s: (k, s, 0))],
        out_specs=pl.BlockSpec((None, tm, K), lambda k, i, s: (k, i, 0)),
        compiler_params=_params(dimension_semantics=("parallel", "parallel", "arbitrary")),
    )(a, parts)


def _silu(z):
    return z * (1.0 / (1.0 + jnp.exp(-z)))


def _ada_fwd(c_all, w_shard, b_shard):
    n = w_shard.shape[1]

    def body(c_ref, w_ref, b_ref, o_ref):
        o_ref[...] = _dot(_silu(c_ref[...]), w_ref[...], precision=HIGHEST) + b_ref[...]

    return pl.pallas_call(body, name="ada_fwd", out_shape=jax.ShapeDtypeStruct((N_DEV, n), F32),
                          compiler_params=_params())(c_all, w_shard, b_shard)


def _ada_bwd(c_all_t, dmod_cols):
    D = c_all_t.shape[0]
    n = dmod_cols.shape[1]

    def body(ct_ref, dm_ref, o_ref):
        sc = _silu(ct_ref[...])
        dm = dm_ref[...]
        acc = sc[:, 0:1] * dm[0:1, :]
        for b in range(1, N_DEV):
            acc = acc + sc[:, b:b + 1] * dm[b:b + 1, :]
        o_ref[...] = acc

    return pl.pallas_call(body, name="ada_bwd", out_shape=jax.ShapeDtypeStruct((D, n), F32),
                          compiler_params=_params())(c_all_t, dmod_cols)


def _row_specs(tm, widths):
    return [pl.BlockSpec((tm, w), lambda i: (i, 0)) for w in widths]


def _vec_spec(w):
    return pl.BlockSpec((1, w), lambda i: (0, 0))


def _col_spec(tm, w):
    return pl.BlockSpec((w, tm), lambda i: (0, i))


def _prenorm(x, g, scale, shift, name):
    S, D = x.shape
    tm = _tile(S, 512)

    def body(x_ref, g_ref, sc_ref, sh_ref, h_ref, ht_ref):
        xv = x_ref[...]
        r = lax.rsqrt(jnp.mean(xv * xv, axis=-1, keepdims=True) + EPS)
        h = (xv * r) * g_ref[...] * (1.0 + sc_ref[...]) + sh_ref[...]
        h_ref[...] = h.astype(BF16)
        ht_ref[...] = h.T.astype(BF16)

    return pl.pallas_call(
        body, name=name, grid=(S // tm,),
        out_shape=(jax.ShapeDtypeStruct((S, D), BF16), jax.ShapeDtypeStruct((D, S), BF16)),
        in_specs=_row_specs(tm, [D]) + [_vec_spec(D)] * 3,
        out_specs=(_row_specs(tm, [D])[0], _col_spec(tm, D)),
        compiler_params=_params(dimension_semantics=("parallel",)),
    )(x, g, scale, shift)


def _group_ones():
    r = lax.broadcasted_iota(jnp.int32, (LANES, LANES), 0) // HEAD_DIM
    c = lax.broadcasted_iota(jnp.int32, (LANES, LANES), 1) // HEAD_DIM
    return (r == c).astype(F32)


def _headnorm_fwd(o_f, o_s, g_f, g_s):
    dh, S = o_f.shape
    tm = _tile(S, 512)

    def body(of_ref, os_ref, gf_ref, gs_ref, mix_ref, mixt_ref):
        ones = _group_ones()
        for part, (o_ref, g_ref) in enumerate(((of_ref, gf_ref), (os_ref, gs_ref))):
            for t in range(dh // LANES):
                cols = slice(t * LANES, (t + 1) * LANES)
                out = slice(part * dh + t * LANES, part * dh + (t + 1) * LANES)
                o = o_ref[cols, :].T
                ms = _dot(o * o, ones, precision=HIGHEST) * (1.0 / HEAD_DIM)
                mixn = o * lax.rsqrt(ms + EPS) * g_ref[:, cols]
                mix_ref[:, out] = mixn.astype(BF16)
                mixt_ref[out, :] = mixn.T.astype(BF16)

    return pl.pallas_call(
        body, name="headnorm_fwd", grid=(S // tm,),
        out_shape=(jax.ShapeDtypeStruct((S, 2 * dh), BF16), jax.ShapeDtypeStruct((2 * dh, S), BF16)),
        in_specs=[_col_spec(tm, dh)] * 2 + [_vec_spec(dh)] * 2,
        out_specs=(_row_specs(tm, [2 * dh])[0], _col_spec(tm, 2 * dh)),
        compiler_params=_params(dimension_semantics=("parallel",)),
    )(o_f, o_s, g_f, g_s)


def _resid_prenorm(x, a_out, gate, g, scale, shift):
    S, D = x.shape
    tm = _tile(S, 512)

    def body(x_ref, a_ref, gt_ref, g_ref, sc_ref, sh_ref, x1_ref, h_ref, ht_ref):
        x1 = x_ref[...] + gt_ref[...] * a_ref[...]
        x1_ref[...] = x1
        r = lax.rsqrt(jnp.mean(x1 * x1, axis=-1, keepdims=True) + EPS)
        h = (x1 * r) * g_ref[...] * (1.0 + sc_ref[...]) + sh_ref[...]
        h_ref[...] = h.astype(BF16)
        ht_ref[...] = h.T.astype(BF16)

    return pl.pallas_call(
        body, name="resid_prenorm", grid=(S // tm,),
        out_shape=(jax.ShapeDtypeStruct((S, D), F32), jax.ShapeDtypeStruct((S, D), BF16),
                   jax.ShapeDtypeStruct((D, S), BF16)),
        in_specs=_row_specs(tm, [D, D]) + [_vec_spec(D)] * 4,
        out_specs=tuple(_row_specs(tm, [D, D]) + [_col_spec(tm, D)]),
        compiler_params=_params(dimension_semantics=("parallel",)),
    )(x, a_out, gate, g, scale, shift)


def _shift_down(main, halo, k):
    ext = jnp.concatenate([halo, main], axis=0)
    return pltpu.roll(ext, k, 0)[halo.shape[0]:]


def _shift_up(main, halo, k):
    ext = jnp.concatenate([main, halo], axis=0)
    n = ext.shape[0]
    return pltpu.roll(ext, n - k, 0)[:main.shape[0]]


def _conv(up, up_halo, w_ref, b_ref):
    return (w_ref[2:3, :] * up + w_ref[1:2, :] * _shift_down(up, up_halo, 1)
            + w_ref[0:1, :] * _shift_down(up, up_halo, 2) + b_ref[...])


def _prev_halo_map(tm):
    step = tm // HALO
    return lambda j, i: (jnp.maximum(i * step - 1, 0), j)


MLP_TM = 512
MLP_CT = 1408
CARRY = 8


def _mlp_up(h, wg, wv, cwg, cwv, cbg, cbv):
    S, D = h.shape
    F = wg.shape[1]
    tm, ct = _tile(S, MLP_TM), _tile(F, MLP_CT)
    nct = F // ct

    def body(h_ref, wg_ref, wv_ref, cwg_ref, cwv_ref, cbg_ref, cbv_ref,
             upg_ref, upv_ref, act_ref, actt_ref, hg_scr, hv_scr):
        i, j = pl.program_id(0), pl.program_id(1)
        hv = h_ref[...]
        us = []
        for w_ref, cw_ref, cb_ref, up_ref, scr in ((wg_ref, cwg_ref, cbg_ref, upg_ref, hg_scr),
                                                   (wv_ref, cwv_ref, cbv_ref, upv_ref, hv_scr)):
            up = _dot(hv, w_ref[...]).astype(BF16)
            up_ref[...] = up
            upf = up.astype(F32)
            halo = jnp.where(i == 0, 0.0, scr[j])
            us.append(_conv(upf, halo, cw_ref, cb_ref))
            scr[j] = upf[tm - CARRY:, :]
        act = _silu(us[0]) * us[1]
        act_ref[...] = act.astype(BF16)
        actt_ref[...] = act.T.astype(BF16)

    blk = pl.BlockSpec((tm, ct), lambda i, j: (i, j))
    wspec = pl.BlockSpec((D, ct), lambda i, j: (0, j))
    cwspec = pl.BlockSpec((CONV_W, ct), lambda i, j: (0, j))
    cbspec = pl.BlockSpec((1, ct), lambda i, j: (0, j))
    sds = jax.ShapeDtypeStruct((S, F), BF16)
    return pl.pallas_call(
        body, name="mlp_up", grid=(S // tm, nct),
        out_shape=(sds, sds, sds, jax.ShapeDtypeStruct((F, S), BF16)),
        in_specs=[pl.BlockSpec((tm, D), lambda i, j: (i, 0)), wspec, wspec, cwspec, cwspec, cbspec, cbspec],
        out_specs=(blk, blk, blk, pl.BlockSpec((ct, tm), lambda i, j: (j, i))),
        scratch_shapes=[pltpu.VMEM((nct, CARRY, ct), F32), pltpu.VMEM((nct, CARRY, ct), F32)],
        compiler_params=_params(dimension_semantics=("arbitrary", "arbitrary")),
    )(h, wg, wv, cwg, cwv, cbg, cbv)


def _conv_act_bwd(gm, w_down, up_g, up_v, cwg, cwv, cbg, cbv):
    S, F = up_g.shape
    D = gm.shape[1]
    tm, ct = _tile(S, 256), _tile(F, CONV_COLS)
    nct = F // ct

    def body(gm_ref, wd_ref, ug_ref, uv_ref, hg_ref, hv_ref, wg_ref, wv_ref, bg_ref, bv_ref,
             dug_ref, duv_ref, pg_ref, pv_ref):
        first = pl.program_id(1) == 0

        @pl.when(first)
        def _():
            pg_ref[...] = jnp.zeros_like(pg_ref)
            pv_ref[...] = jnp.zeros_like(pv_ref)

        da = _dot_nt(gm_ref[...], wd_ref[...])
        taps = []
        for u_ref, h_ref in ((ug_ref, hg_ref), (uv_ref, hv_ref)):
            h = jnp.where(first, 0.0, h_ref[...].astype(F32))
            uu = u_ref[...].astype(F32)
            taps.append((_shift_down(uu, h, 2), _shift_down(uu, h, 1), uu))
        u_g = wg_ref[0:1, :] * taps[0][0] + wg_ref[1:2, :] * taps[0][1] + wg_ref[2:3, :] * taps[0][2] + bg_ref[...]
        u_v = wv_ref[0:1, :] * taps[1][0] + wv_ref[1:2, :] * taps[1][1] + wv_ref[2:3, :] * taps[1][2] + bv_ref[...]
        sg = 1.0 / (1.0 + jnp.exp(-u_g))
        du_g = da * u_v * (sg * (1.0 + u_g * (1.0 - sg)))
        du_v = da * (u_g * sg)
        dug_ref[...] = du_g.astype(BF16)
        duv_ref[...] = du_v.astype(BF16)
        for du, tp, p_ref in ((du_g, taps[0], pg_ref), (du_v, taps[1], pv_ref)):
            for k in range(CONV_W):
                p_ref[k:k + 1, :] += jnp.sum(du * tp[k], axis=0, keepdims=True)
            p_ref[CONV_W:CONV_W + 1, :] += jnp.sum(du, axis=0, keepdims=True)

    main = pl.BlockSpec((tm, ct), lambda j, i: (i, j))
    halo = pl.BlockSpec((HALO, ct), _prev_halo_map(tm))
    wspec = pl.BlockSpec((CONV_W, ct), lambda j, i: (0, j))
    bspec = pl.BlockSpec((1, ct), lambda j, i: (0, j))
    pspec = pl.BlockSpec((8, ct), lambda j, i: (0, j))
    return pl.pallas_call(
        body, name="conv_act_bwd", grid=(nct, S // tm),
        out_shape=(jax.ShapeDtypeStruct((S, F), BF16), jax.ShapeDtypeStruct((S, F), BF16),
                   jax.ShapeDtypeStruct((8, F), F32), jax.ShapeDtypeStruct((8, F), F32)),
        in_specs=[pl.BlockSpec((tm, D), lambda j, i: (i, 0)), pl.BlockSpec((ct, D), lambda j, i: (j, 0)),
                  main, main, halo, halo, wspec, wspec, bspec, bspec],
        out_specs=(main, main, pspec, pspec),
        compiler_params=_params(dimension_semantics=("parallel", "arbitrary")),
    )(gm, w_down, up_g, up_v, up_g, up_v, cwg, cwv, cbg, cbv)


def _conv_bwd_taps(d, halo, w_ref):
    return w_ref[2:3, :] * d + w_ref[1:2, :] * _shift_up(d, halo, 1) + w_ref[0:1, :] * _shift_up(d, halo, 2)


def _scan_mats(R, nc, reverse):
    i = lax.broadcasted_iota(jnp.int32, (LANES, LANES), 0)
    j = lax.broadcasted_iota(jnp.int32, (LANES, LANES), 1)
    inner = ((i >= j) if reverse else (i <= j)).astype(F32)
    r = lax.broadcasted_iota(jnp.int32, (R, R), 0)
    c = lax.broadcasted_iota(jnp.int32, (R, R), 1)
    same = (r // nc) == (c // nc)
    outer = (same & ((c > r) if reverse else (c < r))).astype(F32)
    return inner, outer


def _chunk_scan(v, inner, outer, reverse):
    w = _dot(v, inner, precision=HIGHEST)
    col = 0 if reverse else LANES - 1
    carry = _dot(outer, w, precision=HIGHEST)[:, col:col + 1]
    return w + carry


def _fgate_fwd(z_rows, nc):
    R = z_rows.shape[0]

    def body(z_ref, f_ref):
        z = z_ref[...]
        logf = jnp.minimum(z, 0.0) - jnp.log(1.0 + jnp.exp(-jnp.abs(z)))
        inner, outer = _scan_mats(R, nc, False)
        f_ref[...] = _chunk_scan(logf, inner, outer, False)

    return pl.pallas_call(body, name="fgate_fwd", out_shape=jax.ShapeDtypeStruct((R, LANES), F32),
                          compiler_params=_params())(z_rows)


def _fgate_bwd(dfk_neg_rows, dfq_rows, z_rows, nc):
    R = z_rows.shape[0]
    nh = R // nc

    def body(dfk_ref, dfq_ref, z_ref, dz_ref, db_ref):
        inner, outer = _scan_mats(R, nc, True)
        dlogf = _chunk_scan(dfq_ref[...] - dfk_ref[...], inner, outer, True)
        dz = dlogf * (1.0 / (1.0 + jnp.exp(z_ref[...])))
        dz_ref[...] = dz
        hr = lax.broadcasted_iota(jnp.int32, (nh, R), 0)
        hc = lax.broadcasted_iota(jnp.int32, (nh, R), 1) // nc
        per_head = _dot((hr == hc).astype(F32), dz, precision=HIGHEST)
        db_ref[...] = jnp.sum(per_head, axis=1, keepdims=True)

    return pl.pallas_call(
        body, name="fgate_bwd",
        out_shape=(jax.ShapeDtypeStruct((R, LANES), F32), jax.ShapeDtypeStruct((nh, 1), F32)),
        compiler_params=_params())(dfk_neg_rows, dfq_rows, z_rows)


_NEG = -1e30
SKIP_BELOW = -106.0
_SCALE = HEAD_DIM ** -0.5
F_PARTS = 3
Q_F_LANE = HEAD_DIM
Q_ONE_LANE = HEAD_DIM + F_PARTS


def _kv_slice(k0, nk):
    return pl.ds(pl.multiple_of(k0, SB_BK), nk)


def _mask_t(strict):
    s = lax.broadcasted_iota(jnp.int32, (ATT_BQ, ATT_BQ), 0)
    t = lax.broadcasted_iota(jnp.int32, (ATT_BQ, ATT_BQ), 1)
    return (s < t) if strict else (s <= t)


def _walk_down(i, step, alive, carry, bk):
    carry = step(i * ATT_BQ, ATT_BQ, carry, True)
    first = i * (ATT_BQ // bk)

    def cond(st):
        n, go, _ = st
        return jnp.logical_and(n < first, go)

    def body(st):
        n, _, cr = st
        j = first - 1 - n
        cr = step(j * bk, bk, cr, False)
        return n + 1, alive(jnp.maximum(j - 1, 0), cr), cr

    return lax.while_loop(cond, body, (jnp.int32(0), alive(jnp.maximum(first - 1, 0), carry), carry))[2]


def _t_block(rows):
    return pl.BlockSpec((None, rows, ATT_BQ), lambda h, i, *_: (h, 0, i))


def _t_full(rows, S):
    return pl.BlockSpec((None, rows, S), lambda h, i, *_: (h, 0, 0))


def _n_block():
    return pl.BlockSpec((None, ATT_BQ, LANES), lambda h, i, *_: (h, i, 0))


def _n_full(S):
    return pl.BlockSpec((None, S, LANES), lambda h, i, *_: (h, 0, 0))


def _heads(t):
    S = t.shape[0]
    return jnp.transpose(t.reshape(S, -1, HEAD_DIM), (1, 0, 2))


def _skip_bounds(k_cols, f_rows):
    H, S = f_rows.shape
    f_end = f_rows.reshape(H, S // FOX_BK, FOX_BK)[:, :, -1]
    k_sq = jnp.sum(jnp.square(_heads(k_cols).astype(F32)), axis=-1).reshape(H, S // FOX_BK, FOX_BK)
    k_max = lax.cummax(jnp.sqrt(jnp.max(k_sq, axis=-1)), axis=1)
    return f_end, k_max


def _bf16_parts(f):
    hi = f.astype(BF16).astype(F32)
    mid = (f - hi).astype(BF16).astype(F32)
    return hi, mid, (f - hi - mid).astype(BF16).astype(F32)


def _att_prep(qkv, f_pairs):
    S = qkv.shape[0]
    n_pairs = qkv.shape[1] // (6 * LANES)
    H = 2 * n_pairs
    tm = _tile(S, 512)

    def body(qf_ref, kf_ref, vf_ref, qs_ref, ks_ref, vs_ref, f_ref,
             fqn, fqt, fkn, fkt, fvn, fvt, sqn, sqt, skn, skt, svn, svt):
        lane = lax.broadcasted_iota(jnp.int32, (1, LANES), 1)
        f = f_ref[...]

        def head(ref, e):
            t = ref[...].astype(F32)
            if e == 1:
                t = pltpu.roll(t, HEAD_DIM, 1)
            return jnp.where(lane < HEAD_DIM, t, 0.0)

        def at(first):
            return jnp.logical_and(lane >= first, lane < first + F_PARTS)

        for e in range(2):
            parts = _bf16_parts(f[:, e:e + 1])
            f_lanes = sum(jnp.where(lane == Q_F_LANE + k, parts[k], 0.0) for k in range(F_PARTS))
            nf_lanes = sum(jnp.where(lane == Q_ONE_LANE + k, parts[k], 0.0) for k in range(F_PARTS))
            vals = (
                (fqn, fqt, LANES, head(qf_ref, e) * _SCALE + f_lanes + jnp.where(at(Q_ONE_LANE), 1.0, 0.0)),
                (fkn, fkt, LANES, head(kf_ref, e) + jnp.where(at(Q_F_LANE), 1.0, 0.0) - nf_lanes),
                (fvn, fvt, HEAD_DIM, head(vf_ref, e)),
                (sqn, sqt, LANES, head(qs_ref, e) * _SCALE),
                (skn, skt, LANES, head(ks_ref, e)),
                (svn, svt, HEAD_DIM, head(vs_ref, e)),
            )
            for n_ref, t_ref, rows, val in vals:
                n_ref[e] = val.astype(BF16)
                t_ref[e] = val.T[:rows].astype(BF16)

    col = lambda base: pl.BlockSpec((tm, LANES), lambda i, p: (i, base + p))
    n_spec = pl.BlockSpec((2, tm, LANES), lambda i, p: (p, i, 0))
    t_spec = lambda rows: pl.BlockSpec((2, rows, tm), lambda i, p: (p, 0, i))
    n_sds = jax.ShapeDtypeStruct((H, S, LANES), BF16)
    t_sds = lambda rows: jax.ShapeDtypeStruct((H, rows, S), BF16)
    group = ([n_sds, t_sds(LANES), n_sds, t_sds(LANES), n_sds, t_sds(HEAD_DIM)],
             [n_spec, t_spec(LANES), n_spec, t_spec(LANES), n_spec, t_spec(HEAD_DIM)])
    res = pl.pallas_call(
        body, name="att_prep", grid=(S // tm, n_pairs),
        out_shape=tuple(group[0] * 2),
        in_specs=[col(k * n_pairs) for k in range(6)] + [pl.BlockSpec((None, tm, 2), lambda i, p: (p, i, 0))],
        out_specs=tuple(group[1] * 2),
        compiler_params=_params(dimension_semantics=("parallel", "parallel")),
    )(qkv, qkv, qkv, qkv, qkv, qkv, f_pairs)
    names = ("q_n", "q_t", "k_n", "k_t", "v_n", "v_t")
    return dict(zip(names, res[:6])), dict(zip(names, res[6:]))


def _fox_reach(qt, fend_ref, kmax_ref, h):
    qf = qt.astype(F32)
    q_norm = jnp.sqrt(jnp.sum(jnp.square(qf[:HEAD_DIM]), axis=0, keepdims=True))
    f_t = jnp.sum(qf[Q_F_LANE:Q_F_LANE + F_PARTS], axis=0, keepdims=True)
    return lambda j: q_norm * kmax_ref[h, j] + f_t - fend_ref[h, j]


def _fox_fwd(q_t, k_n, v_t, f_end, k_max, shards):
    H, _, S = q_t.shape
    n, nq = len(shards), S // ATT_BQ

    def body(fend_ref, kmax_ref, qt_ref, k_ref, vt_ref, *rest):
        ins, (ot_ref, lse_ref), outs, sems = rest[:n], rest[n:n + 2], rest[n + 2:2 * n + 2], rest[2 * n + 2:]
        h, i = pl.program_id(0), pl.program_id(1)

        @pl.when(jnp.logical_and(h == 0, i == 0))
        def _():
            for cp in _exchange_copies(ins, outs, *sems, False):
                cp.start()

        qt = qt_ref[...]
        reach = _fox_reach(qt, fend_ref, kmax_ref, h)

        def step(k0, nk, carry, masked):
            m, l, acc = carry
            ks = _kv_slice(k0, nk)
            s = _dot(k_ref[ks, :], qt)
            if masked:
                s = jnp.where(_mask_t(False), s, _NEG)
            mn = jnp.maximum(m, jnp.max(s, axis=0, keepdims=True))
            alpha = jnp.exp(m - mn)
            p = jnp.exp(s - mn)
            l = alpha * l + jnp.sum(p, axis=0, keepdims=True)
            acc = acc * alpha + _dot(vt_ref[:, ks], p.astype(BF16))
            return mn, l, acc

        def alive(j, carry):
            return jnp.max(reach(j) - carry[0]) > SKIP_BELOW

        row = jnp.zeros((1, ATT_BQ), F32)
        m, l, acc = _walk_down(i, step, alive, (row + _NEG, row, jnp.zeros((HEAD_DIM, ATT_BQ), F32)), FOX_BK)
        ot_ref[...] = acc / l
        lse_ref[...] = m + jnp.log(l)

        @pl.when(jnp.logical_and(h == H - 1, i == nq - 1))
        def _():
            for cp in _exchange_copies(ins, outs, *sems, False):
                cp.wait()

    any_spec = pl.BlockSpec(memory_space=pl.ANY)
    grid_spec = pltpu.PrefetchScalarGridSpec(
        num_scalar_prefetch=2, grid=(H, nq),
        in_specs=[_t_block(LANES), _n_full(S), _t_full(HEAD_DIM, S)] + [any_spec] * n,
        out_specs=tuple([_t_block(HEAD_DIM), _t_block(1)] + [any_spec] * n),
        scratch_shapes=_exchange_sems(n))
    res = pl.pallas_call(
        body, name="fox_fwd", grid_spec=grid_spec,
        out_shape=tuple([jax.ShapeDtypeStruct((H, HEAD_DIM, S), F32), jax.ShapeDtypeStruct((H, 1, S), F32)]
                        + _exchange_out_shapes(shards, False)),
        compiler_params=_params(dimension_semantics=("arbitrary", "arbitrary"), has_side_effects=True),
    )(f_end, k_max, q_t, k_n, v_t, *shards)
    return res[0], res[1], res[2:]


def _fox_bwd(q_t, q_n, k_n, k_t, v_n, do_t, do_n, o_t, lse, f_end, k_max):
    H, _, S = q_t.shape

    def body(fend_ref, kmax_ref, qt_ref, qn_ref, k_ref, kt_ref, v_ref, dot_ref, don_ref, ot_ref, lse_ref,
             dqt_ref, dk_ref, dv_ref):
        h, i = pl.program_id(0), pl.program_id(1)

        @pl.when(i == 0)
        def _():
            dk_ref[...] = jnp.zeros_like(dk_ref)
            dv_ref[...] = jnp.zeros_like(dv_ref)

        qt, qn, dot, don = qt_ref[...], qn_ref[...], dot_ref[...], don_ref[...]
        lse = lse_ref[...]
        delta = jnp.sum(dot[:HEAD_DIM].astype(F32) * ot_ref[...], axis=0, keepdims=True)
        reach = _fox_reach(qt, fend_ref, kmax_ref, h)

        def alive(j, dq):
            return jnp.max(reach(j) - lse) > SKIP_BELOW

        def step(k0, nk, dq, masked):
            ks = _kv_slice(k0, nk)
            s = _dot(k_ref[ks, :], qt)
            if masked:
                s = jnp.where(_mask_t(False), s, _NEG)
            p = jnp.exp(s - lse)
            ds = (p * (_dot(v_ref[ks, :], dot) - delta)).astype(BF16)
            dk_ref[ks, :] += _dot(ds, qn)
            dv_ref[ks, :] += _dot(p.astype(BF16), don)
            return dq + _dot(kt_ref[:, ks], ds)

        dqt_ref[...] = _walk_down(i, step, alive, jnp.zeros((LANES, ATT_BQ), F32), FOX_BK)

    grid_spec = pltpu.PrefetchScalarGridSpec(
        num_scalar_prefetch=2, grid=(H, S // ATT_BQ),
        in_specs=[_t_block(LANES), _n_block(), _n_full(S), _t_full(LANES, S), _n_full(S),
                  _t_block(LANES), _n_block(), _t_block(HEAD_DIM), _t_block(1)],
        out_specs=(_t_block(LANES), _n_full(S), _n_full(S)))
    return pl.pallas_call(
        body, name="fox_bwd", grid_spec=grid_spec,
        out_shape=(jax.ShapeDtypeStruct((H, LANES, S), F32), jax.ShapeDtypeStruct((H, S, LANES), F32),
                   jax.ShapeDtypeStruct((H, S, LANES), F32)),
        compiler_params=_params(dimension_semantics=("parallel", "arbitrary")),
    )(f_end, k_max, q_t, q_n, k_n, k_t, v_n, do_t, do_n, o_t, lse)


def _scan_lhs():
    r = lax.broadcasted_iota(jnp.int32, (SCAN_BK, 2 * SCAN_BK), 0)
    c = lax.broadcasted_iota(jnp.int32, (SCAN_BK, 2 * SCAN_BK), 1) % SCAN_BK
    return (c >= r).astype(BF16)


def _suffix_sum(t, lhs):
    hi = t.astype(BF16)
    lo = (t - hi.astype(F32)).astype(BF16)
    return _dot(lhs, jnp.concatenate([hi, lo], axis=0))


def _sb_scores(k, qt, mask):
    z = _dot(k, qt)
    e = jnp.exp(-jnp.abs(z))
    lb = -(jnp.maximum(z, 0.0) + jnp.log(1.0 + e))
    if mask is not None:
        lb = jnp.where(mask, lb, 0.0)
    return z, e, lb


def _scan_blocks(nk):
    return [slice(u * SCAN_BK, (u + 1) * SCAN_BK) for u in reversed(range(nk // SCAN_BK))]


def _sb_fwd(q_t, k_n, v_t):
    H, _, S = q_t.shape

    def body(qt_ref, k_ref, vt_ref, ot_ref):
        i = pl.program_id(1)
        qt = qt_ref[...]
        lhs = _scan_lhs()

        def step(k0, nk, carry, masked):
            c, acc = carry
            ks = _kv_slice(k0, nk)
            mask = _mask_t(True) if masked else None
            z, _, lb = _sb_scores(k_ref[ks, :], qt, mask)
            parts = []
            for sl in _scan_blocks(nk):
                rin = _suffix_sum(lb[sl], lhs)
                a = jnp.exp(z[sl] + rin + c)
                if masked:
                    a = jnp.where(mask[sl], a, 0.0)
                parts.append(a.astype(BF16))
                c = c + rin[0:1, :]
            a_all = jnp.concatenate(parts[::-1], axis=0)
            return c, acc + _dot(vt_ref[:, ks], a_all)

        carry = (jnp.zeros((1, ATT_BQ), F32), jnp.zeros((HEAD_DIM, ATT_BQ), F32))
        ot_ref[...] = _walk_down(i, step, lambda j, cr: jnp.max(cr[0]) > SKIP_BELOW, carry, SB_BK)[1]

    return pl.pallas_call(
        body, name="sb_fwd", grid=(H, S // ATT_BQ),
        out_shape=jax.ShapeDtypeStruct((H, HEAD_DIM, S), F32),
        in_specs=[_t_block(LANES), _n_full(S), _t_full(HEAD_DIM, S)],
        out_specs=_t_block(HEAD_DIM),
        compiler_params=_params(dimension_semantics=("parallel", "parallel")),
    )(q_t, k_n, v_t)


def _sb_bwd(q_t, q_n, k_n, k_t, v_n, do_t, do_n, o_t, bound):
    H, _, S = q_t.shape
    n, nq = len(bound), S // ATT_BQ

    def body(qt_ref, qn_ref, k_ref, kt_ref, v_ref, dot_ref, don_ref, ot_ref, *rest):
        ins, (dqt_ref, dk_ref, dv_ref) = rest[:n], rest[n:n + 3]
        outs, sems = rest[n + 3:2 * n + 3], rest[2 * n + 3:]
        h, i = pl.program_id(0), pl.program_id(1)

        @pl.when(jnp.logical_and(h == 0, i == 0))
        def _():
            for cp in _exchange_copies(ins, outs, *sems, True):
                cp.start()

        @pl.when(i == 0)
        def _():
            dk_ref[...] = jnp.zeros_like(dk_ref)
            dv_ref[...] = jnp.zeros_like(dv_ref)

        qt, qn, dot, don = qt_ref[...], qn_ref[...], dot_ref[...], don_ref[...]
        lhs = _scan_lhs()
        delta = jnp.sum(dot[:HEAD_DIM].astype(F32) * ot_ref[...], axis=0, keepdims=True)

        def step(k0, nk, carry, masked):
            c, g, dq = carry
            ks = _kv_slice(k0, nk)
            mask = _mask_t(True) if masked else None
            z, e, lb = _sb_scores(k_ref[ks, :], qt, mask)
            da = _dot(v_ref[ks, :], dot)
            a_parts, dz_parts = [], []
            for sl in _scan_blocks(nk):
                rin = _suffix_sum(lb[sl], lhs)
                a = jnp.exp(z[sl] + rin + c)
                if masked:
                    a = jnp.where(mask[sl], a, 0.0)
                ab = a.astype(BF16)
                gg = ab.astype(F32) * da[sl]
                rgin = _suffix_sum(gg, lhs)
                rinv = 1.0 / (1.0 + e[sl])
                sig = jnp.where(z[sl] >= 0.0, rinv, e[sl] * rinv)
                dz = gg - sig * (delta - g - (rgin - gg))
                if masked:
                    dz = jnp.where(mask[sl], dz, 0.0)
                a_parts.append(ab)
                dz_parts.append(dz.astype(BF16))
                c = c + rin[0:1, :]
                g = g + rgin[0:1, :]
            ab_all = jnp.concatenate(a_parts[::-1], axis=0)
            dzb = jnp.concatenate(dz_parts[::-1], axis=0)
            dk_ref[ks, :] += _dot(dzb, qn)
            dv_ref[ks, :] += _dot(ab_all, don)
            return c, g, dq + _dot(kt_ref[:, ks], dzb)

        row = jnp.zeros((1, ATT_BQ), F32)
        carry = (row, row, jnp.zeros((LANES, ATT_BQ), F32))
        dqt_ref[...] = _walk_down(i, step, lambda j, cr: jnp.max(cr[0]) > SKIP_BELOW, carry, SB_BK)[2]

        @pl.when(jnp.logical_and(h == H - 1, i == nq - 1))
        def _():
            for cp in _exchange_copies(ins, outs, *sems, True):
                cp.wait()

    any_spec = pl.BlockSpec(memory_space=pl.ANY)
    res = pl.pallas_call(
        body, name="sb_bwd", grid=(H, nq),
        out_shape=tuple([jax.ShapeDtypeStruct((H, LANES, S), F32), jax.ShapeDtypeStruct((H, S, LANES), F32),
                         jax.ShapeDtypeStruct((H, S, LANES), F32)] + _exchange_out_shapes(bound, True)),
        in_specs=[_t_block(LANES), _n_block(), _n_full(S), _t_full(LANES, S), _n_full(S),
                  _t_block(LANES), _n_block(), _t_block(HEAD_DIM)] + [any_spec] * n,
        out_specs=tuple([_t_block(LANES), _n_full(S), _n_full(S)] + [any_spec] * n),
        scratch_shapes=_exchange_sems(n),
        compiler_params=_params(dimension_semantics=("arbitrary", "arbitrary"), has_side_effects=True),
    )(q_t, q_n, k_n, k_t, v_n, do_t, do_n, o_t, *bound)
    return res[0], res[1], res[2], res[3:]


def _dqkv_assemble(dqf_t, dkf, dvf, dqs_t, dks, dvs):
    H, _, S = dqf_t.shape
    n_pairs = H // 2
    tm = _tile(S, 512)

    def body(dqf_ref, dkf_ref, dvf_ref, dqs_ref, dks_ref, dvs_ref, out_ref, dfk_ref):
        lane = lax.broadcasted_iota(jnp.int32, (1, LANES), 1)
        slabs = ((dqf_ref, True), (dkf_ref, False), (dvf_ref, False),
                 (dqs_ref, True), (dks_ref, False), (dvs_ref, False))
        for k, (ref, transposed) in enumerate(slabs):
            if transposed:
                t0, t1 = ref[0].T * _SCALE, ref[1].T * _SCALE
            else:
                t0, t1 = ref[0], ref[1]
            out_ref[k] = jnp.where(lane < HEAD_DIM, t0, pltpu.roll(t1, HEAD_DIM, 1)).astype(BF16)
        for e in range(2):
            dfk_ref[e] = dkf_ref[e].T[Q_ONE_LANE:Q_ONE_LANE + 1, :]

    t_spec = pl.BlockSpec((2, LANES, tm), lambda i, p: (p, 0, i))
    n_spec = pl.BlockSpec((2, tm, LANES), lambda i, p: (p, i, 0))
    return pl.pallas_call(
        body, name="dqkv_assemble", grid=(S // tm, n_pairs),
        out_shape=(jax.ShapeDtypeStruct((6, S, n_pairs * LANES), BF16), jax.ShapeDtypeStruct((H, 1, S), F32)),
        in_specs=[t_spec, n_spec, n_spec, t_spec, n_spec, n_spec],
        out_specs=(pl.BlockSpec((6, tm, LANES), lambda i, p: (0, i, p)),
                   pl.BlockSpec((2, 1, tm), lambda i, p: (p, 0, i))),
        compiler_params=_params(dimension_semantics=("parallel", "parallel")),
    )(dqf_t, dkf, dvf, dqs_t, dks, dvs)


def _acc_spec(w):
    return pl.BlockSpec((1, w), lambda i: (0, 0))


def _loss_head(x1, act, w_down, gate_m, g_final, target):
    S, D = x1.shape
    F = act.shape[1]
    tm = _tile(S, 256)

    def body(x1_ref, act_ref, w_ref, gt_ref, gf_ref, tg_ref, dx2_ref, gm_ref, loss_ref, dgf_ref, dgt_ref):
        @pl.when(pl.program_id(0) == 0)
        def _():
            loss_ref[...] = jnp.zeros_like(loss_ref)
            dgf_ref[...] = jnp.zeros_like(dgf_ref)
            dgt_ref[...] = jnp.zeros_like(dgt_ref)

        mo = _dot(act_ref[...], w_ref[...])
        x2 = x1_ref[...] + gt_ref[...] * mo
        r = lax.rsqrt(jnp.mean(x2 * x2, axis=-1, keepdims=True) + EPS)
        xh = x2 * r
        diff = xh * gf_ref[...] - tg_ref[...]
        loss_ref[...] += (0.5 / D) * jnp.sum(diff * diff)
        dy = diff * (1.0 / D)
        dgf_ref[...] += jnp.sum(dy * xh, axis=0, keepdims=True)
        dxh = dy * gf_ref[...]
        dx2 = r * (dxh - xh * jnp.mean(dxh * xh, axis=-1, keepdims=True))
        dx2_ref[...] = dx2
        gm_ref[...] = (dx2 * gt_ref[...]).astype(BF16)
        dgt_ref[...] += jnp.sum(dx2 * mo, axis=0, keepdims=True)

    return pl.pallas_call(
        body, name="loss_head", grid=(S // tm,),
        out_shape=(jax.ShapeDtypeStruct((S, D), F32), jax.ShapeDtypeStruct((S, D), BF16),
                   jax.ShapeDtypeStruct((1, LANES), F32), jax.ShapeDtypeStruct((1, D), F32),
                   jax.ShapeDtypeStruct((1, D), F32)),
        in_specs=_row_specs(tm, [D, F]) + [pl.BlockSpec((F, D), lambda i: (0, 0))] + [_vec_spec(D)] * 2
        + _row_specs(tm, [D]),
        out_specs=tuple(_row_specs(tm, [D, D]) + [_acc_spec(LANES), _acc_spec(D), _acc_spec(D)]),
        compiler_params=_params(dimension_semantics=("arbitrary",)),
    )(x1, act, w_down, gate_m, g_final, target)


def _norm_bwd(lhs, rhs, xin, dres, g, scale, name, gate=None, branch=None, bound=(), conv=None):
    S, D = xin.shape
    tm = _tile(S, 256)
    gated = gate is not None
    nl, nb, n_steps = len(lhs), len(bound), S // tm
    nc_ = nl if conv else 0
    n_out = (6 if gated else 4) + nc_

    def body(*refs):
        l_refs, r_refs, rest = refs[:nl], refs[nl:2 * nl], refs[2 * nl:]
        halo_refs, cw_refs, rest = rest[:nc_], rest[nc_:2 * nc_], rest[2 * nc_:]
        if gated:
            x_ref, dr_ref, g_ref, sc_ref, gt_ref, br_ref = rest[:6]
            rest = rest[6:]
        else:
            x_ref, dr_ref, g_ref, sc_ref = rest[:4]
            rest = rest[4:]
        ins, outs, ex_outs, sems = rest[:nb], rest[nb:nb + n_out], rest[nb + n_out:2 * nb + n_out], rest[2 * nb + n_out:]
        dx_ref, dsc_ref, dsh_ref, dg_ref = outs[:4]
        sums = (dsc_ref, dsh_ref, dg_ref) + ((outs[5],) if gated else ())
        dup_refs = outs[n_out - nc_:]
        i = pl.program_id(0)

        @pl.when(i == 0)
        def _():
            for s_ref in sums:
                s_ref[...] = jnp.zeros_like(s_ref)
            if nb:
                for cp in _chip_exchange_copies(ins, ex_outs, *sems):
                    cp.start()

        dhv = None
        for k, (l_ref, r_ref) in enumerate(zip(l_refs, r_refs)):
            if conv:
                halo = jnp.where(i == n_steps - 1, 0.0, halo_refs[k][...].astype(F32))
                dup = _conv_bwd_taps(l_ref[...].astype(F32), halo, cw_refs[k]).astype(BF16)
                dup_refs[k][...] = dup
                terms = [_dot_nt(dup, r_ref[...])]
            elif len(l_ref.shape) == 3:
                K = l_ref.shape[2]
                terms = [_dot_nt(l_ref[k], r_ref[:, k * K:(k + 1) * K]) for k in range(l_ref.shape[0])]
            else:
                terms = [_dot_nt(l_ref[...], r_ref[...])]
            for t in terms:
                dhv = t if dhv is None else dhv + t
        xv = x_ref[...]
        r = lax.rsqrt(jnp.mean(xv * xv, axis=-1, keepdims=True) + EPS)
        xh = xv * r
        dsc_ref[...] += jnp.sum(dhv * (xh * g_ref[...]), axis=0, keepdims=True)
        dsh_ref[...] += jnp.sum(dhv, axis=0, keepdims=True)
        dn = dhv * (1.0 + sc_ref[...])
        dg_ref[...] += jnp.sum(dn * xh, axis=0, keepdims=True)
        dxh = dn * g_ref[...]
        dx = dr_ref[...] + r * (dxh - xh * jnp.mean(dxh * xh, axis=-1, keepdims=True))
        dx_ref[...] = dx
        if gated:
            outs[4][...] = (dx * gt_ref[...]).astype(BF16)
            outs[5][...] += jnp.sum(dx * br_ref[...], axis=0, keepdims=True)

        if nb:
            @pl.when(i == n_steps - 1)
            def _():
                for cp in _chip_exchange_copies(ins, ex_outs, *sems):
                    cp.wait()

    def l_spec(a):
        if a.ndim == 3:
            return pl.BlockSpec((a.shape[0], tm, a.shape[2]), lambda i: (0, i, 0))
        return pl.BlockSpec((tm, a.shape[1]), lambda i: (i, 0))

    any_spec = pl.BlockSpec(memory_space=pl.ANY)
    vec = jax.ShapeDtypeStruct((1, D), F32)
    out_shape = [jax.ShapeDtypeStruct((S, D), F32), vec, vec, vec]
    out_specs = _row_specs(tm, [D]) + [_acc_spec(D)] * 3
    in_specs = [l_spec(a) for a in lhs] + [pl.BlockSpec(b.shape, lambda i: (0, 0)) for b in rhs]
    args = list(lhs) + list(rhs)
    if conv:
        step, last_halo = tm // HALO, S // HALO - 1
        in_specs += [pl.BlockSpec((HALO, a.shape[1]), lambda i: (jnp.minimum((i + 1) * step, last_halo), 0))
                     for a in lhs]
        in_specs += [pl.BlockSpec(w.shape, lambda i: (0, 0)) for w in conv]
        args += list(lhs) + list(conv)
    in_specs += _row_specs(tm, [D, D]) + [_vec_spec(D)] * 2
    args += [xin, dres, g, scale]
    if gated:
        out_shape += [jax.ShapeDtypeStruct((S, D), BF16), vec]
        out_specs += _row_specs(tm, [D]) + [_acc_spec(D)]
        in_specs += [_vec_spec(D)] + _row_specs(tm, [D])
        args += [gate, branch]
    if conv:
        out_shape += [jax.ShapeDtypeStruct(a.shape, BF16) for a in lhs]
        out_specs += [l_spec(a) for a in lhs]
    res = pl.pallas_call(
        body, name=name, grid=(n_steps,),
        out_shape=tuple(out_shape + [jax.ShapeDtypeStruct(b.shape, b.dtype) for b in bound]),
        in_specs=in_specs + [any_spec] * nb, out_specs=tuple(out_specs + [any_spec] * nb),
        scratch_shapes=_chip_exchange_sems(nb) if nb else [],
        compiler_params=_params(dimension_semantics=("arbitrary",), has_side_effects=bool(nb)),
    )(*args, *bound)
    return tuple(res[:n_out]) + (tuple(res[n_out:]),)


def _headnorm_bwd(dmix, o_f, o_s, g_f, g_s):
    dh, S = o_f.shape
    H = dh // HEAD_DIM
    tm = _tile(S, 256)

    def body(dm_ref, of_ref, os_ref, gf_ref, gs_ref, fn_ref, ft_ref, sn_ref, st_ref, dgf_ref, dgs_ref):
        @pl.when(pl.program_id(0) == 0)
        def _():
            dgf_ref[...] = jnp.zeros_like(dgf_ref)
            dgs_ref[...] = jnp.zeros_like(dgs_ref)

        ones = _group_ones()
        lane = lax.broadcasted_iota(jnp.int32, (1, LANES), 1)
        parts = ((of_ref, gf_ref, fn_ref, ft_ref, dgf_ref), (os_ref, gs_ref, sn_ref, st_ref, dgs_ref))
        for part, (o_ref, g_ref, n_ref, t_ref, dg_ref) in enumerate(parts):
            for t in range(dh // LANES):
                cols = slice(t * LANES, (t + 1) * LANES)
                o = o_ref[cols, :].T
                dm = dm_ref[:, part * dh + t * LANES: part * dh + (t + 1) * LANES]
                r = lax.rsqrt(_dot(o * o, ones, precision=HIGHEST) * (1.0 / HEAD_DIM) + EPS)
                oh = o * r
                dg_ref[:, cols] += jnp.sum(dm * oh, axis=0, keepdims=True)
                dn = dm * g_ref[:, cols]
                mean = _dot(dn * oh, ones, precision=HIGHEST) * (1.0 / HEAD_DIM)
                do = r * (dn - oh * mean)
                for e in range(2):
                    d = do if e == 0 else pltpu.roll(do, HEAD_DIM, 1)
                    d = jnp.where(lane < HEAD_DIM, d, 0.0)
                    n_ref[2 * t + e] = d.astype(BF16)
                    t_ref[2 * t + e] = d.T.astype(BF16)

    vec = jax.ShapeDtypeStruct((1, dh), F32)
    n_sds = jax.ShapeDtypeStruct((H, S, LANES), BF16)
    t_sds = jax.ShapeDtypeStruct((H, LANES, S), BF16)
    n_spec = pl.BlockSpec((H, tm, LANES), lambda i: (0, i, 0))
    t_spec = pl.BlockSpec((H, LANES, tm), lambda i: (0, 0, i))
    return pl.pallas_call(
        body, name="headnorm_bwd", grid=(S // tm,),
        out_shape=(n_sds, t_sds, n_sds, t_sds, vec, vec),
        in_specs=_row_specs(tm, [2 * dh]) + [_col_spec(tm, dh)] * 2 + [_vec_spec(dh)] * 2,
        out_specs=(n_spec, t_spec, n_spec, t_spec, _acc_spec(dh), _acc_spec(dh)),
        compiler_params=_params(dimension_semantics=("arbitrary",)),
    )(dmix, o_f, o_s, g_f, g_s)


def _adamw(w, gslots, m, v, name):
    R, C = w.shape
    n = gslots.shape[0]
    tr = 256 if (R % 256 == 0 and R > 256) else R
    bc1 = 1.0 - ADAM_B1 ** ADAM_STEP
    bc2 = 1.0 - ADAM_B2 ** ADAM_STEP

    def body(w_ref, gs_ref, m_ref, v_ref, g_ref, d_ref, nm_ref, nv_ref):
        g = gs_ref[0]
        for s in range(1, n):
            g = g + gs_ref[s]
        nm = ADAM_B1 * m_ref[...] + (1.0 - ADAM_B1) * g
        nv = ADAM_B2 * v_ref[...] + (1.0 - ADAM_B2) * (g * g)
        g_ref[...] = g
        nm_ref[...] = nm
        nv_ref[...] = nv
        d_ref[...] = -ADAM_LR * ((nm / bc1) / (jnp.sqrt(nv / bc2) + ADAM_EPS) + ADAM_WD * w_ref[...])

    blk = pl.BlockSpec((tr, C), lambda i: (i, 0))
    sds = jax.ShapeDtypeStruct((R, C), F32)
    return pl.pallas_call(
        body, name=name, grid=(R // tr,), out_shape=(sds,) * 4,
        in_specs=[blk, pl.BlockSpec((n, tr, C), lambda i: (0, i, 0)), blk, blk], out_specs=(blk,) * 4,
        compiler_params=_params(dimension_semantics=("parallel",)),
    )(w, gslots, m, v)


def _slot_sum(slots, name):
    n, _, C = slots.shape

    def body(s_ref, o_ref):
        acc = s_ref[0]
        for s in range(1, n):
            acc = acc + s_ref[s]
        o_ref[...] = acc

    return pl.pallas_call(body, name=name, out_shape=jax.ShapeDtypeStruct((1, C), F32),
                          compiler_params=_params())(slots)


def _pad_cols(a, n):
    return jnp.pad(a, ((0, 0), (0, n - a.shape[1])))


def _ungather(g, axis):
    if axis == 0:
        return g.reshape(g.shape[0] * g.shape[1], g.shape[2])
    return jnp.transpose(g, (1, 0, 2)).reshape(g.shape[1], g.shape[0] * g.shape[2])


def _to_slots(full, axis):
    R, C = full.shape
    if axis == 0:
        return full.reshape(N_DEV, R // N_DEV, C)
    return jnp.transpose(full.reshape(R, N_DEV, C // N_DEV), (1, 0, 2))


def kernel(x, c, w_ada, b_ada, g_attn, w_in, b_fgate, g_out_fox, g_out_sb, w_out, g_mlp, w_up, conv_w, conv_b, w_down, g_final, loss_target, m_w_ada, m_b_ada, m_g_attn, m_w_in, m_b_fgate, m_g_out_fox, m_g_out_sb, m_w_out, m_g_mlp, m_w_up, m_conv_w, m_conv_b, m_w_down, m_g_final, v_w_ada, v_b_ada, v_g_attn, v_w_in, v_b_fgate, v_g_out_fox, v_g_out_sb, v_w_out, v_g_mlp, v_w_up, v_conv_w, v_conv_b, v_w_down, v_g_final):
    S, D = x.shape[1], x.shape[2]
    dh = D // 2
    n_heads = dh // HEAD_DIM
    n_qkv = 6 * dh
    ff = w_down.shape[1] * N_DEV
    ffp = -(-ff // (2 * LANES)) * (2 * LANES)
    nc = S // LANES
    me = 4 * lax.axis_index("x") + 2 * lax.axis_index("y") + lax.axis_index("c")
    xs, tgt = x[0], loss_target[0]

    c_all, win_g = _gather_two_level([c, w_in[0].astype(BF16)], name="gather_first")
    c_all = c_all.reshape(N_DEV, D)
    W_in = _ungather(win_g, 1)
    W_qkv, W_f = W_in[:, :n_qkv], _pad_cols(W_in[:, n_qkv:], LANES)
    cb_g, cb_v = _pad_cols(conv_b[:, :ff], ffp), _pad_cols(conv_b[:, ff:], ffp)

    n_ada = w_ada.shape[2]
    b_shard = lax.dynamic_slice(b_ada, (0, me * n_ada), (1, n_ada))
    mod_cols = _ada_fwd(c_all, w_ada[0], b_shard)
    (mod_g,) = _exchange([mod_cols], scatter=False, name="gather_mod")
    mod = lax.dynamic_index_in_dim(mod_g, me, axis=1, keepdims=False).reshape(6, 1, D)
    shift_a, scale_a, gate_a, shift_m, scale_m, gate_m = [mod[k] for k in range(6)]

    h1, h1_t = _prenorm(xs, g_attn, scale_a, shift_a, "prenorm_attn")
    qkv = _mm(h1, W_qkv, BF16, "proj_qkv")
    flog = _mm(h1, W_f, F32, "proj_fgate")
    zf = flog[:, :n_heads] + b_fgate
    z_rows = zf.T.reshape(n_heads * nc, LANES)
    f_rows = _fgate_fwd(z_rows, nc).reshape(n_heads, S)
    f_pairs = jnp.transpose(f_rows.reshape(n_heads // 2, 2, S), (0, 2, 1))
    fox, sb = _att_prep(qkv, f_pairs)
    f_end, k_max = _skip_bounds(qkv[:, dh:2 * dh], f_rows)
    of_t, lse, (wout_g, wup_g, wdown_g, convw_g) = _fox_fwd(
        fox["q_t"], fox["k_n"], fox["v_t"], f_end, k_max,
        [w_out[0].astype(BF16), w_up[0].astype(BF16), w_down[0].astype(BF16), conv_w[0]])
    W_out = _ungather(wout_g, 0)
    W_up = _ungather(wup_g, 1)
    W_g, W_v = _pad_cols(W_up[:, :ff], ffp), _pad_cols(W_up[:, ff:], ffp)
    W_down = jnp.pad(_ungather(wdown_g, 0), ((0, ffp - ff), (0, 0)))
    cw_full = _ungather(convw_g, 1)
    cw_g, cw_v = _pad_cols(cw_full[:, :ff], ffp), _pad_cols(cw_full[:, ff:], ffp)
    os_t = _sb_fwd(sb["q_t"], sb["k_n"], sb["v_t"])
    o_f, o_s = of_t.reshape(dh, S), os_t.reshape(dh, S)
    mix, mix_t = _headnorm_fwd(o_f, o_s, g_out_fox, g_out_sb)
    a_out = _mm(mix, W_out, F32, "proj_out")
    x1, h2, h2_t = _resid_prenorm(xs, a_out, gate_a, g_mlp, scale_m, shift_m)
    up_g, up_v, act, act_t = _mlp_up(h2, W_g, W_v, cw_g, cw_v, cb_g, cb_v)

    dx2, gm, loss_p, dg_final, dgate_m = _loss_head(x1, act, W_down, gate_m, g_final.reshape(1, D), tgt)
    dW_down = _mm_acc(act_t, gm, "bwd_down_w")
    du_g, du_v, p_g, p_v = _conv_act_bwd(gm, W_down, up_g, up_v, cw_g, cw_v, cb_g, cb_v)
    dx1, dscale_m, dshift_m, dg_mlp, ga, dgate_a, dup_g, dup_v, _ = _norm_bwd(
        [du_g, du_v], [W_g, W_v], x1, dx2, g_mlp, scale_m, "norm_mlp_bwd", gate=gate_a, branch=a_out,
        conv=[cw_g, cw_v])
    dW_g = _mm_acc(h2_t, dup_g, "bwd_up_w_g")
    dW_v = _mm_acc(h2_t, dup_v, "bwd_up_w_v")
    dmix = _mm(ga, W_out, F32, "bwd_out_act", nt=True)
    dW_out = _mm_acc(mix_t, ga, "bwd_out_w")
    dof_n, dof_t, dos_n, dos_t, dg_fox, dg_sb = _headnorm_bwd(dmix, o_f, o_s, g_out_fox, g_out_sb)
    dqf_t, dkf, dvf = _fox_bwd(fox["q_t"], fox["q_n"], fox["k_n"], fox["k_t"], fox["v_n"], dof_t, dof_n, of_t, lse,
                              f_end, k_max)
    dW_upf = jnp.concatenate([dW_g[:, :ff], dW_v[:, :ff]], axis=1)
    dcw = jnp.concatenate([p_g[:CONV_W, :ff], p_v[:CONV_W, :ff]], axis=1)
    dqs_t, dks, dvs, (s_out, s_up, s_down, s_cw) = _sb_bwd(
        sb["q_t"], sb["q_n"], sb["k_n"], sb["k_t"], sb["v_n"], dos_t, dos_n, os_t,
        [_to_slots(dW_out, 0), _to_slots(dW_upf, 1), _to_slots(dW_down[:ff], 0), _to_slots(dcw, 1)])
    dparts, dfk = _dqkv_assemble(dqf_t, dkf, dvf, dqs_t, dks, dvs)
    dz_rows, db_fgate = _fgate_bwd(dfk.reshape(n_heads * nc, LANES),
                                   dqf_t[:, Q_F_LANE, :].reshape(n_heads * nc, LANES), z_rows, nc)
    dzf = _pad_cols(dz_rows.reshape(n_heads, S).T, LANES).astype(BF16)
    dW_qkv = _mm_acc_parts(h1_t, dparts, "bwd_in_w")
    dW_f = _mm_acc(h1_t, dzf, "bwd_in_w_fgate")
    dW_in = jnp.concatenate([jnp.transpose(dW_qkv, (1, 0, 2)).reshape(D, n_qkv), dW_f[:, :n_heads]], axis=1)
    bound_in = _to_slots(dW_in, 1)
    bound_in = bound_in.reshape((N_CHIPS, 2) + bound_in.shape[1:])
    (got_in,) = _scatter_to_sibling([bound_in], "scatter_sibling")
    c_idx = lax.axis_index("c").astype(jnp.int32).reshape(1)
    grad_x, dscale_a, dshift_a, dg_attn, (s_in,) = _norm_bwd(
        [dparts, dzf], [W_qkv, W_f], xs, dx1, g_attn, scale_a, "norm_attn_bwd",
        bound=[_pair_add(bound_in, got_in, c_idx, "pair_add")])

    dconv_b = jnp.concatenate([p_g[CONV_W:CONV_W + 1, :ff], p_v[CONV_W:CONV_W + 1, :ff]], axis=1)
    parts = [dshift_a, dscale_a, dgate_a, dshift_m, dscale_m, dgate_m,
             dg_attn, db_fgate.reshape(1, n_heads), dg_fox, dg_sb, dg_mlp, dconv_b, dg_final,
             loss_p[:, :1]]
    sizes = [p.shape[1] for p in parts]
    vec = jnp.concatenate(parts, axis=1)
    n_vec = -(-vec.shape[1] // LANES) * LANES
    vec = _pad_cols(vec, n_vec)
    (vec_g,) = _exchange([vec], scatter=False, name="gather_small")
    offs = [0]
    for s in sizes:
        offs.append(offs[-1] + s)

    def small(k0, k1=None):
        k1 = k0 if k1 is None else k1
        return vec_g[:, :, offs[k0]:offs[k1 + 1]]

    dmod_all = small(0, 5).reshape(N_DEV, 6 * D)
    dmod_cols = lax.dynamic_slice(dmod_all, (0, me * n_ada), (N_DEV, n_ada))
    dW_ada = _ada_bwd(c_all.T, dmod_cols)


    res = {}
    res["w_ada"] = _adamw(w_ada[0], dW_ada[None], m_w_ada[0], v_w_ada[0], "adamw_w_ada")
    res["w_in"] = _adamw(w_in[0], s_in, m_w_in[0], v_w_in[0], "adamw_w_in")
    res["w_out"] = _adamw(w_out[0], s_out, m_w_out[0], v_w_out[0], "adamw_w_out")
    res["w_up"] = _adamw(w_up[0], s_up, m_w_up[0], v_w_up[0], "adamw_w_up")
    res["w_down"] = _adamw(w_down[0], s_down, m_w_down[0], v_w_down[0], "adamw_w_down")
    res["conv_w"] = _adamw(conv_w[0], s_cw, m_conv_w[0], v_conv_w[0], "adamw_conv_w")
    small_names = ["b_ada", "g_attn", "b_fgate", "g_out_fox", "g_out_sb", "g_mlp", "conv_b", "g_final"]
    small_w = [b_ada, g_attn, b_fgate, g_out_fox, g_out_sb, g_mlp, conv_b, g_final.reshape(1, D)]
    small_m = [m_b_ada, m_g_attn, m_b_fgate, m_g_out_fox, m_g_out_sb, m_g_mlp, m_conv_b, m_g_final.reshape(1, D)]
    small_v = [v_b_ada, v_g_attn, v_b_fgate, v_g_out_fox, v_g_out_sb, v_g_mlp, v_conv_b, v_g_final.reshape(1, D)]
    small_res = _adamw(jnp.concatenate(small_w, axis=1), small(0, 12), jnp.concatenate(small_m, axis=1),
                       jnp.concatenate(small_v, axis=1), "adamw_small")
    lo = 0
    for nm, wv in zip(small_names, small_w):
        res[nm] = tuple(r[:, lo:lo + wv.shape[1]] for r in small_res)
        lo += wv.shape[1]
    loss = _slot_sum(_pad_cols(small(13).reshape(N_DEV, 1), LANES).reshape(N_DEV, 1, LANES), "loss_sum")[0, 0]

    names = ["w_ada", "b_ada", "g_attn", "w_in", "b_fgate", "g_out_fox", "g_out_sb", "w_out", "g_mlp",
             "w_up", "conv_w", "conv_b", "w_down", "g_final"]

    def shaped(n, a):
        if n == "g_final":
            return a.reshape(D)
        if n in ("b_ada", "g_attn", "b_fgate", "g_out_fox", "g_out_sb", "g_mlp", "conv_b"):
            return a
        return a[None]

    outs = [loss, grad_x[None]]
    for k in range(4):
        outs += [shaped(n, res[n][k]) for n in names]
    return tuple(outs)
```

```python
import jax
import jax.numpy as jnp
from jax import lax
from jax.experimental import pallas as pl
from jax.experimental.pallas import tpu as pltpu

F32 = jnp.float32
BF16 = jnp.bfloat16
HIGHEST = lax.Precision.HIGHEST

N_DEV = 8
LANES = 128
HEAD_DIM = 64
EPS = 1e-6
CONV_W = 3
CONV_COLS = 1408
HALO = 16
ATT_BQ = 512
ATT_BK = 512
SCAN_BK = 128
VMEM_LIMIT = 56 * 1024 * 1024

ADAM_LR = 0.001
ADAM_B1 = 0.9
ADAM_B2 = 0.999
ADAM_EPS = 1e-08
ADAM_WD = 0.01
ADAM_STEP = 10


def _params(**kw):
    return pltpu.CompilerParams(vmem_limit_bytes=VMEM_LIMIT, **kw)


def _tile(n, cap):
    if n <= cap:
        return n
    best = None
    for t in range(LANES, cap + 1, LANES):
        if n % t == 0:
            best = t
    assert best is not None, (n, cap)
    return best


def _dot(a, b, **kw):
    return jnp.dot(a, b, preferred_element_type=F32, **kw)


def _exchange_copies(ins, outs, send_sems, recv_sems, loc_sems, scatter):
    n = len(ins)
    if n == 0:
        return []
    x, y, c = lax.axis_index("x"), lax.axis_index("y"), lax.axis_index("c")
    me = 4 * x + 2 * y + c
    copies = []
    for a in range(n):
        src = ins[a].at[me] if scatter else ins[a]
        copies.append(pltpu.make_async_copy(src, outs[a].at[me], loc_sems.at[a]))
    for k in range(1, N_DEV):
        px = 1 - x if k & 4 else x
        py = 1 - y if k & 2 else y
        pc = 1 - c if k & 1 else c
        peer = 4 * px + 2 * py + pc
        for a in range(n):
            src = ins[a].at[peer] if scatter else ins[a]
            copies.append(pltpu.make_async_remote_copy(
                src_ref=src, dst_ref=outs[a].at[me],
                send_sem=send_sems.at[a, k - 1], recv_sem=recv_sems.at[a, k - 1],
                device_id=(px, py, pc), device_id_type=pl.DeviceIdType.MESH))
    return copies


def _exchange_out_shapes(arrays, scatter):
    return [jax.ShapeDtypeStruct((N_DEV,) + tuple(a.shape[1:] if scatter else a.shape), a.dtype) for a in arrays]


def _exchange_sems(n):
    return [pltpu.SemaphoreType.DMA((n, N_DEV - 1)), pltpu.SemaphoreType.DMA((n, N_DEV - 1)),
            pltpu.SemaphoreType.DMA((n,))]


def _exchange(arrays, scatter, name):
    n = len(arrays)

    def body(*refs):
        copies = _exchange_copies(refs[:n], refs[n:2 * n], *refs[2 * n:], scatter)
        for cp in copies:
            cp.start()
        for cp in copies:
            cp.wait()

    any_spec = pl.BlockSpec(memory_space=pl.ANY)
    return pl.pallas_call(
        body, name=name, out_shape=tuple(_exchange_out_shapes(arrays, scatter)),
        in_specs=[any_spec] * n, out_specs=tuple([any_spec] * n),
        scratch_shapes=_exchange_sems(n),
        compiler_params=pltpu.CompilerParams(has_side_effects=True),
    )(*arrays)


def _gather_two_level(arrays, name):
    n = len(arrays)
    out_shape = [jax.ShapeDtypeStruct((N_DEV,) + tuple(a.shape), a.dtype) for a in arrays]

    def body(*refs):
        ins, outs = refs[:n], refs[n:2 * n]
        send_sems, recv_sems, loc_sems = refs[2 * n:]
        x, y, c = lax.axis_index("x"), lax.axis_index("y"), lax.axis_index("c")
        me, sibling = (x, y, c), (x, y, 1 - c)
        chips = [(1 - x, y), (x, 1 - y), (1 - x, 1 - y)]

        def slot(px, py, pc):
            return 4 * px + 2 * py + pc

        def copy(a, k, block, to, src=None):
            dst = outs[a].at[slot(*block)]
            return pltpu.make_async_remote_copy(
                src_ref=dst if src is None else src, dst_ref=dst,
                send_sem=send_sems.at[a, k], recv_sem=recv_sems.at[a, k],
                device_id=to, device_id_type=pl.DeviceIdType.MESH)

        local = [pltpu.make_async_copy(ins[a], outs[a].at[slot(*me)], loc_sems.at[a]) for a in range(n)]
        for cp in local:
            cp.start()
        first = []
        for a in range(n):
            first.append(copy(a, 0, me, sibling, src=ins[a]))
            first += [copy(a, 1 + j, me, (*chip, c), src=ins[a]) for j, chip in enumerate(chips)]
        for cp in first:
            cp.start()
        passed = []
        for j, chip in enumerate(chips):
            for a in range(n):
                copy(a, 1 + j, (*chip, c), me).wait_recv()
                cp = copy(a, 4 + j, (*chip, c), sibling)
                cp.start()
                passed.append(cp)
        for a in range(n):
            copy(a, 0, sibling, me).wait_recv()
            for j, chip in enumerate(chips):
                copy(a, 4 + j, (*chip, 1 - c), me).wait_recv()
        for cp in first + passed:
            cp.wait_send()
        for cp in local:
            cp.wait()

    any_spec = pl.BlockSpec(memory_space=pl.ANY)
    return pl.pallas_call(
        body, name=name, out_shape=tuple(out_shape),
        in_specs=[any_spec] * n, out_specs=tuple([any_spec] * n),
        scratch_shapes=[pltpu.SemaphoreType.DMA((n, N_DEV - 1)), pltpu.SemaphoreType.DMA((n, N_DEV - 1)),
                        pltpu.SemaphoreType.DMA((n,))],
        compiler_params=pltpu.CompilerParams(has_side_effects=True),
    )(*arrays)


N_CHIPS = 4


def _scatter_to_sibling(arrays, name):
    n = len(arrays)
    out_shape = [jax.ShapeDtypeStruct((N_CHIPS,) + tuple(a.shape[2:]), a.dtype) for a in arrays]

    def body(*refs):
        ins, outs = refs[:n], refs[n:2 * n]
        send_sems, recv_sems = refs[2 * n:]
        x, y, c = lax.axis_index("x"), lax.axis_index("y"), lax.axis_index("c")
        copies = []
        for a in range(n):
            for q in range(N_CHIPS):
                cp = pltpu.make_async_remote_copy(
                    src_ref=ins[a].at[q, 1 - c], dst_ref=outs[a].at[q],
                    send_sem=send_sems.at[a, q], recv_sem=recv_sems.at[a, q],
                    device_id=(x, y, 1 - c), device_id_type=pl.DeviceIdType.MESH)
                cp.start()
                copies.append(cp)
        for cp in copies:
            cp.wait()

    any_spec = pl.BlockSpec(memory_space=pl.ANY)
    return pl.pallas_call(
        body, name=name, out_shape=tuple(out_shape),
        in_specs=[any_spec] * n, out_specs=tuple([any_spec] * n),
        scratch_shapes=[pltpu.SemaphoreType.DMA((n, N_CHIPS)), pltpu.SemaphoreType.DMA((n, N_CHIPS))],
        compiler_params=pltpu.CompilerParams(has_side_effects=True),
    )(*arrays)


def _pair_add(mine, got, c_idx, name):
    _, _, R, C = mine.shape
    tr = 256 if (R % 256 == 0 and R > 256) else R

    def body(c_ref, m_ref, g_ref, o_ref):
        o_ref[...] = m_ref[...] + g_ref[...]

    grid_spec = pltpu.PrefetchScalarGridSpec(
        num_scalar_prefetch=1, grid=(N_CHIPS, R // tr),
        in_specs=[pl.BlockSpec((None, None, tr, C), lambda q, i, c_ref: (q, c_ref[0], i, 0)),
                  pl.BlockSpec((None, tr, C), lambda q, i, c_ref: (q, i, 0))],
        out_specs=pl.BlockSpec((None, tr, C), lambda q, i, c_ref: (q, i, 0)))
    return pl.pallas_call(
        body, name=name, grid_spec=grid_spec, out_shape=jax.ShapeDtypeStruct((N_CHIPS, R, C), mine.dtype),
        compiler_params=_params(dimension_semantics=("parallel", "parallel")),
    )(c_idx, mine, got)


def _chip_exchange_copies(ins, outs, send_sems, recv_sems, loc_sems):
    n = len(ins)
    x, y, c = lax.axis_index("x"), lax.axis_index("y"), lax.axis_index("c")
    myq = 2 * x + y
    copies = [pltpu.make_async_copy(ins[a].at[myq], outs[a].at[myq], loc_sems.at[a]) for a in range(n)]
    for k in range(1, N_CHIPS):
        qx = 1 - x if k & 2 else x
        qy = 1 - y if k & 1 else y
        for a in range(n):
            copies.append(pltpu.make_async_remote_copy(
                src_ref=ins[a].at[2 * qx + qy], dst_ref=outs[a].at[myq],
                send_sem=send_sems.at[a, k - 1], recv_sem=recv_sems.at[a, k - 1],
                device_id=(qx, qy, c), device_id_type=pl.DeviceIdType.MESH))
    return copies


def _chip_exchange_sems(n):
    return [pltpu.SemaphoreType.DMA((n, N_CHIPS - 1)), pltpu.SemaphoreType.DMA((n, N_CHIPS - 1)),
            pltpu.SemaphoreType.DMA((n,))]


def _dot_nt(a, b):
    return lax.dot_general(a, b, (((1,), (1,)), ((), ())), preferred_element_type=F32)


def _rhs_spec(b, tn, nt):
    if nt:
        return pl.BlockSpec((tn, b.shape[1]), lambda i, j: (j, 0))
    return pl.BlockSpec((b.shape[0], tn), lambda i, j: (0, j))


def _mm(a, b, out_dtype, name, tm=1024, tn=512, nt=False):
    M, K = a.shape
    N = b.shape[0] if nt else b.shape[1]
    tm, tn = _tile(M, tm), _tile(N, tn)
    dot = _dot_nt if nt else _dot

    def body(a_ref, b_ref, o_ref):
        o_ref[...] = dot(a_ref[...], b_ref[...]).astype(out_dtype)

    return pl.pallas_call(
        body, name=name, out_shape=jax.ShapeDtypeStruct((M, N), out_dtype),
        grid=(M // tm, N // tn),
        in_specs=[pl.BlockSpec((tm, K), lambda i, j: (i, 0)), _rhs_spec(b, tn, nt)],
        out_specs=pl.BlockSpec((tm, tn), lambda i, j: (i, j)),
        compiler_params=_params(dimension_semantics=("parallel", "parallel")),
    )(a, b)


def _mm_acc(a, b, name, tm=1408, tn=1408, tk=1024):
    M, S = a.shape
    _, N = b.shape
    tm, tn, tk = _tile(M, tm), _tile(N, tn), _tile(S, tk)

    def body(a_ref, b_ref, o_ref):
        @pl.when(pl.program_id(2) == 0)
        def _():
            o_ref[...] = jnp.zeros_like(o_ref)

        o_ref[...] += _dot(a_ref[...], b_ref[...])

    return pl.pallas_call(
        body, name=name, out_shape=jax.ShapeDtypeStruct((M, N), F32),
        grid=(M // tm, N // tn, S // tk),
        in_specs=[pl.BlockSpec((tm, tk), lambda i, j, k: (i, k)), pl.BlockSpec((tk, tn), lambda i, j, k: (k, j))],
        out_specs=pl.BlockSpec((tm, tn), lambda i, j, k: (i, j)),
        compiler_params=_params(dimension_semantics=("parallel", "parallel", "arbitrary")),
    )(a, b)


def _mm_acc_parts(a, parts, name, tm=1024, tk=1024):
    M, S = a.shape
    P, _, K = parts.shape
    tm, tk = _tile(M, tm), _tile(S, tk)

    def body(a_ref, b_ref, o_ref):
        @pl.when(pl.program_id(2) == 0)
        def _():
            o_ref[...] = jnp.zeros_like(o_ref)

        o_ref[...] += _dot(a_ref[...], b_ref[...])

    return pl.pallas_call(
        body, name=name, out_shape=jax.ShapeDtypeStruct((P, M, K), F32), grid=(P, M // tm, S // tk),
        in_specs=[pl.BlockSpec((tm, tk), lambda k, i, s: (i, s)), pl.BlockSpec((None, tk, K), lambda k, i, s: (k, s, 0))],
        out_specs=pl.BlockSpec((None, tm, K), lambda k, i, s: (k, i, 0)),
        compiler_params=_params(dimension_semantics=("parallel", "parallel", "arbitrary")),
    )(a, parts)


def _silu(z):
    return z * (1.0 / (1.0 + jnp.exp(-z)))


def _ada_fwd(c_all, w_shard, b_shard):
    n = w_shard.shape[1]

    def body(c_ref, w_ref, b_ref, o_ref):
        o_ref[...] = _dot(_silu(c_ref[...]), w_ref[...], precision=HIGHEST) + b_ref[...]

    return pl.pallas_call(body, name="ada_fwd", out_shape=jax.ShapeDtypeStruct((N_DEV, n), F32),
                          compiler_params=_params())(c_all, w_shard, b_shard)


def _ada_bwd(c_all_t, dmod_cols):
    D = c_all_t.shape[0]
    n = dmod_cols.shape[1]

    def body(ct_ref, dm_ref, o_ref):
        sc = _silu(ct_ref[...])
        dm = dm_ref[...]
        acc = sc[:, 0:1] * dm[0:1, :]
        for b in range(1, N_DEV):
            acc = acc + sc[:, b:b + 1] * dm[b:b + 1, :]
        o_ref[...] = acc

    return pl.pallas_call(body, name="ada_bwd", out_shape=jax.ShapeDtypeStruct((D, n), F32),
                          compiler_params=_params())(c_all_t, dmod_cols)


def _row_specs(tm, widths):
    return [pl.BlockSpec((tm, w), lambda i: (i, 0)) for w in widths]


def _vec_spec(w):
    return pl.BlockSpec((1, w), lambda i: (0, 0))


def _col_spec(tm, w):
    return pl.BlockSpec((w, tm), lambda i: (0, i))


def _prenorm(x, g, scale, shift, name):
    S, D = x.shape
    tm = _tile(S, 512)

    def body(x_ref, g_ref, sc_ref, sh_ref, h_ref, ht_ref):
        xv = x_ref[...]
        r = lax.rsqrt(jnp.mean(xv * xv, axis=-1, keepdims=True) + EPS)
        h = (xv * r) * g_ref[...] * (1.0 + sc_ref[...]) + sh_ref[...]
        h_ref[...] = h.astype(BF16)
        ht_ref[...] = h.T.astype(BF16)

    return pl.pallas_call(
        body, name=name, grid=(S // tm,),
        out_shape=(jax.ShapeDtypeStruct((S, D), BF16), jax.ShapeDtypeStruct((D, S), BF16)),
        in_specs=_row_specs(tm, [D]) + [_vec_spec(D)] * 3,
        out_specs=(_row_specs(tm, [D])[0], _col_spec(tm, D)),
        compiler_params=_params(dimension_semantics=("parallel",)),
    )(x, g, scale, shift)


def _group_ones():
    r = lax.broadcasted_iota(jnp.int32, (LANES, LANES), 0) // HEAD_DIM
    c = lax.broadcasted_iota(jnp.int32, (LANES, LANES), 1) // HEAD_DIM
    return (r == c).astype(F32)


def _headnorm_fwd(o_f, o_s, g_f, g_s):
    dh, S = o_f.shape
    tm = _tile(S, 512)

    def body(of_ref, os_ref, gf_ref, gs_ref, mix_ref, mixt_ref):
        ones = _group_ones()
        for part, (o_ref, g_ref) in enumerate(((of_ref, gf_ref), (os_ref, gs_ref))):
            for t in range(dh // LANES):
                cols = slice(t * LANES, (t + 1) * LANES)
                out = slice(part * dh + t * LANES, part * dh + (t + 1) * LANES)
                o = o_ref[cols, :].T
                ms = _dot(o * o, ones, precision=HIGHEST) * (1.0 / HEAD_DIM)
                mixn = o * lax.rsqrt(ms + EPS) * g_ref[:, cols]
                mix_ref[:, out] = mixn.astype(BF16)
                mixt_ref[out, :] = mixn.T.astype(BF16)

    return pl.pallas_call(
        body, name="headnorm_fwd", grid=(S // tm,),
        out_shape=(jax.ShapeDtypeStruct((S, 2 * dh), BF16), jax.ShapeDtypeStruct((2 * dh, S), BF16)),
        in_specs=[_col_spec(tm, dh)] * 2 + [_vec_spec(dh)] * 2,
        out_specs=(_row_specs(tm, [2 * dh])[0], _col_spec(tm, 2 * dh)),
        compiler_params=_params(dimension_semantics=("parallel",)),
    )(o_f, o_s, g_f, g_s)


def _resid_prenorm(x, a_out, gate, g, scale, shift):
    S, D = x.shape
    tm = _tile(S, 512)

    def body(x_ref, a_ref, gt_ref, g_ref, sc_ref, sh_ref, x1_ref, h_ref, ht_ref):
        x1 = x_ref[...] + gt_ref[...] * a_ref[...]
        x1_ref[...] = x1
        r = lax.rsqrt(jnp.mean(x1 * x1, axis=-1, keepdims=True) + EPS)
        h = (x1 * r) * g_ref[...] * (1.0 + sc_ref[...]) + sh_ref[...]
        h_ref[...] = h.astype(BF16)
        ht_ref[...] = h.T.astype(BF16)

    return pl.pallas_call(
        body, name="resid_prenorm", grid=(S // tm,),
        out_shape=(jax.ShapeDtypeStruct((S, D), F32), jax.ShapeDtypeStruct((S, D), BF16),
                   jax.ShapeDtypeStruct((D, S), BF16)),
        in_specs=_row_specs(tm, [D, D]) + [_vec_spec(D)] * 4,
        out_specs=tuple(_row_specs(tm, [D, D]) + [_col_spec(tm, D)]),
        compiler_params=_params(dimension_semantics=("parallel",)),
    )(x, a_out, gate, g, scale, shift)


def _shift_down(main, halo, k):
    ext = jnp.concatenate([halo, main], axis=0)
    return pltpu.roll(ext, k, 0)[halo.shape[0]:]


def _shift_up(main, halo, k):
    ext = jnp.concatenate([main, halo], axis=0)
    n = ext.shape[0]
    return pltpu.roll(ext, n - k, 0)[:main.shape[0]]


def _conv(up, up_halo, w_ref, b_ref):
    return (w_ref[2:3, :] * up + w_ref[1:2, :] * _shift_down(up, up_halo, 1)
            + w_ref[0:1, :] * _shift_down(up, up_halo, 2) + b_ref[...])


def _prev_halo_map(tm):
    step = tm // HALO
    return lambda j, i: (jnp.maximum(i * step - 1, 0), j)


MLP_TM = 512
MLP_CT = 1408
CARRY = 8


def _mlp_up(h, wg, wv, cwg, cwv, cbg, cbv):
    S, D = h.shape
    F = wg.shape[1]
    tm, ct = _tile(S, MLP_TM), _tile(F, MLP_CT)
    nct = F // ct

    def body(h_ref, wg_ref, wv_ref, cwg_ref, cwv_ref, cbg_ref, cbv_ref,
             upg_ref, upv_ref, act_ref, actt_ref, hg_scr, hv_scr):
        i, j = pl.program_id(0), pl.program_id(1)
        hv = h_ref[...]
        us = []
        for w_ref, cw_ref, cb_ref, up_ref, scr in ((wg_ref, cwg_ref, cbg_ref, upg_ref, hg_scr),
                                                   (wv_ref, cwv_ref, cbv_ref, upv_ref, hv_scr)):
            up = _dot(hv, w_ref[...]).astype(BF16)
            up_ref[...] = up
            upf = up.astype(F32)
            halo = jnp.where(i == 0, 0.0, scr[j])
            us.append(_conv(upf, halo, cw_ref, cb_ref))
            scr[j] = upf[tm - CARRY:, :]
        act = _silu(us[0]) * us[1]
        act_ref[...] = act.astype(BF16)
        actt_ref[...] = act.T.astype(BF16)

    blk = pl.BlockSpec((tm, ct), lambda i, j: (i, j))
    wspec = pl.BlockSpec((D, ct), lambda i, j: (0, j))
    cwspec = pl.BlockSpec((CONV_W, ct), lambda i, j: (0, j))
    cbspec = pl.BlockSpec((1, ct), lambda i, j: (0, j))
    sds = jax.ShapeDtypeStruct((S, F), BF16)
    return pl.pallas_call(
        body, name="mlp_up", grid=(S // tm, nct),
        out_shape=(sds, sds, sds, jax.ShapeDtypeStruct((F, S), BF16)),
        in_specs=[pl.BlockSpec((tm, D), lambda i, j: (i, 0)), wspec, wspec, cwspec, cwspec, cbspec, cbspec],
        out_specs=(blk, blk, blk, pl.BlockSpec((ct, tm), lambda i, j: (j, i))),
        scratch_shapes=[pltpu.VMEM((nct, CARRY, ct), F32), pltpu.VMEM((nct, CARRY, ct), F32)],
        compiler_params=_params(dimension_semantics=("arbitrary", "arbitrary")),
    )(h, wg, wv, cwg, cwv, cbg, cbv)


def _conv_act_bwd(gm, w_down, up_g, up_v, cwg, cwv, cbg, cbv):
    S, F = up_g.shape
    D = gm.shape[1]
    tm, ct = _tile(S, 256), _tile(F, CONV_COLS)
    nct = F // ct

    def body(gm_ref, wd_ref, ug_ref, uv_ref, hg_ref, hv_ref, wg_ref, wv_ref, bg_ref, bv_ref,
             dug_ref, duv_ref, pg_ref, pv_ref):
        first = pl.program_id(1) == 0

        @pl.when(first)
        def _():
            pg_ref[...] = jnp.zeros_like(pg_ref)
            pv_ref[...] = jnp.zeros_like(pv_ref)

        da = _dot_nt(gm_ref[...], wd_ref[...])
        taps = []
        for u_ref, h_ref in ((ug_ref, hg_ref), (uv_ref, hv_ref)):
            h = jnp.where(first, 0.0, h_ref[...].astype(F32))
            uu = u_ref[...].astype(F32)
            taps.append((_shift_down(uu, h, 2), _shift_down(uu, h, 1), uu))
        u_g = wg_ref[0:1, :] * taps[0][0] + wg_ref[1:2, :] * taps[0][1] + wg_ref[2:3, :] * taps[0][2] + bg_ref[...]
        u_v = wv_ref[0:1, :] * taps[1][0] + wv_ref[1:2, :] * taps[1][1] + wv_ref[2:3, :] * taps[1][2] + bv_ref[...]
        sg = 1.0 / (1.0 + jnp.exp(-u_g))
        du_g = da * u_v * (sg * (1.0 + u_g * (1.0 - sg)))
        du_v = da * (u_g * sg)
        dug_ref[...] = du_g.astype(BF16)
        duv_ref[...] = du_v.astype(BF16)
        for du, tp, p_ref in ((du_g, taps[0], pg_ref), (du_v, taps[1], pv_ref)):
            for k in range(CONV_W):
                p_ref[k:k + 1, :] += jnp.sum(du * tp[k], axis=0, keepdims=True)
            p_ref[CONV_W:CONV_W + 1, :] += jnp.sum(du, axis=0, keepdims=True)

    main = pl.BlockSpec((tm, ct), lambda j, i: (i, j))
    halo = pl.BlockSpec((HALO, ct), _prev_halo_map(tm))
    wspec = pl.BlockSpec((CONV_W, ct), lambda j, i: (0, j))
    bspec = pl.BlockSpec((1, ct), lambda j, i: (0, j))
    pspec = pl.BlockSpec((8, ct), lambda j, i: (0, j))
    return pl.pallas_call(
        body, name="conv_act_bwd", grid=(nct, S // tm),
        out_shape=(jax.ShapeDtypeStruct((S, F), BF16), jax.ShapeDtypeStruct((S, F), BF16),
                   jax.ShapeDtypeStruct((8, F), F32), jax.ShapeDtypeStruct((8, F), F32)),
        in_specs=[pl.BlockSpec((tm, D), lambda j, i: (i, 0)), pl.BlockSpec((ct, D), lambda j, i: (j, 0)),
                  main, main, halo, halo, wspec, wspec, bspec, bspec],
        out_specs=(main, main, pspec, pspec),
        compiler_params=_params(dimension_semantics=("parallel", "arbitrary")),
    )(gm, w_down, up_g, up_v, up_g, up_v, cwg, cwv, cbg, cbv)


def _conv_bwd_taps(d, halo, w_ref):
    return w_ref[2:3, :] * d + w_ref[1:2, :] * _shift_up(d, halo, 1) + w_ref[0:1, :] * _shift_up(d, halo, 2)


def _scan_mats(R, nc, reverse):
    i = lax.broadcasted_iota(jnp.int32, (LANES, LANES), 0)
    j = lax.broadcasted_iota(jnp.int32, (LANES, LANES), 1)
    inner = ((i >= j) if reverse else (i <= j)).astype(F32)
    r = lax.broadcasted_iota(jnp.int32, (R, R), 0)
    c = lax.broadcasted_iota(jnp.int32, (R, R), 1)
    same = (r // nc) == (c // nc)
    outer = (same & ((c > r) if reverse else (c < r))).astype(F32)
    return inner, outer


def _chunk_scan(v, inner, outer, reverse):
    w = _dot(v, inner, precision=HIGHEST)
    col = 0 if reverse else LANES - 1
    carry = _dot(outer, w, precision=HIGHEST)[:, col:col + 1]
    return w + carry


def _fgate_fwd(z_rows, nc):
    R = z_rows.shape[0]

    def body(z_ref, f_ref):
        z = z_ref[...]
        logf = jnp.minimum(z, 0.0) - jnp.log(1.0 + jnp.exp(-jnp.abs(z)))
        inner, outer = _scan_mats(R, nc, False)
        f_ref[...] = _chunk_scan(logf, inner, outer, False)

    return pl.pallas_call(body, name="fgate_fwd", out_shape=jax.ShapeDtypeStruct((R, LANES), F32),
                          compiler_params=_params())(z_rows)


def _fgate_bwd(dfk_neg_rows, dfq_rows, z_rows, nc):
    R = z_rows.shape[0]
    nh = R // nc

    def body(dfk_ref, dfq_ref, z_ref, dz_ref, db_ref):
        inner, outer = _scan_mats(R, nc, True)
        dlogf = _chunk_scan(dfq_ref[...] - dfk_ref[...], inner, outer, True)
        dz = dlogf * (1.0 / (1.0 + jnp.exp(z_ref[...])))
        dz_ref[...] = dz
        hr = lax.broadcasted_iota(jnp.int32, (nh, R), 0)
        hc = lax.broadcasted_iota(jnp.int32, (nh, R), 1) // nc
        per_head = _dot((hr == hc).astype(F32), dz, precision=HIGHEST)
        db_ref[...] = jnp.sum(per_head, axis=1, keepdims=True)

    return pl.pallas_call(
        body, name="fgate_bwd",
        out_shape=(jax.ShapeDtypeStruct((R, LANES), F32), jax.ShapeDtypeStruct((nh, 1), F32)),
        compiler_params=_params())(dfk_neg_rows, dfq_rows, z_rows)


_NEG = -1e30
SKIP_BELOW = -106.0
_SCALE = HEAD_DIM ** -0.5
N_SCAN = ATT_BK // SCAN_BK
F_PARTS = 3
Q_F_LANE = HEAD_DIM
Q_ONE_LANE = HEAD_DIM + F_PARTS


def _kv_slice(j):
    return pl.ds(pl.multiple_of(j * ATT_BK, ATT_BK), ATT_BK)


def _mask_t(strict):
    s = lax.broadcasted_iota(jnp.int32, (ATT_BK, ATT_BQ), 0)
    t = lax.broadcasted_iota(jnp.int32, (ATT_BK, ATT_BQ), 1)
    return (s < t) if strict else (s <= t)


def _walk_down(i, step, alive, carry):
    carry = step(i, carry, True)

    def cond(st):
        n, go, _ = st
        return jnp.logical_and(n < i, go)

    def body(st):
        n, _, cr = st
        j = i - 1 - n
        cr = step(j, cr, False)
        return n + 1, alive(jnp.maximum(j - 1, 0), cr), cr

    return lax.while_loop(cond, body, (jnp.int32(0), alive(jnp.maximum(i - 1, 0), carry), carry))[2]


def _t_block(rows):
    return pl.BlockSpec((None, rows, ATT_BQ), lambda h, i, *_: (h, 0, i))


def _t_full(rows, S):
    return pl.BlockSpec((None, rows, S), lambda h, i, *_: (h, 0, 0))


def _n_block():
    return pl.BlockSpec((None, ATT_BQ, LANES), lambda h, i, *_: (h, i, 0))


def _n_full(S):
    return pl.BlockSpec((None, S, LANES), lambda h, i, *_: (h, 0, 0))


def _heads(t):
    S = t.shape[0]
    return jnp.transpose(t.reshape(S, -1, HEAD_DIM), (1, 0, 2))


def _skip_bounds(k_cols, f_rows):
    H, S = f_rows.shape
    f_end = f_rows.reshape(H, S // ATT_BK, ATT_BK)[:, :, -1]
    k_sq = jnp.sum(jnp.square(_heads(k_cols).astype(F32)), axis=-1).reshape(H, S // ATT_BK, ATT_BK)
    k_max = lax.cummax(jnp.sqrt(jnp.max(k_sq, axis=-1)), axis=1)
    return f_end, k_max


def _bf16_parts(f):
    hi = f.astype(BF16).astype(F32)
    mid = (f - hi).astype(BF16).astype(F32)
    return hi, mid, (f - hi - mid).astype(BF16).astype(F32)


def _att_prep(qkv, f_pairs):
    S = qkv.shape[0]
    n_pairs = qkv.shape[1] // (6 * LANES)
    H = 2 * n_pairs
    tm = _tile(S, 512)

    def body(qf_ref, kf_ref, vf_ref, qs_ref, ks_ref, vs_ref, f_ref,
             fqn, fqt, fkn, fkt, fvn, fvt, sqn, sqt, skn, skt, svn, svt):
        lane = lax.broadcasted_iota(jnp.int32, (1, LANES), 1)
        f = f_ref[...]

        def head(ref, e):
            t = ref[...].astype(F32)
            if e == 1:
                t = pltpu.roll(t, HEAD_DIM, 1)
            return jnp.where(lane < HEAD_DIM, t, 0.0)

        def at(first):
            return jnp.logical_and(lane >= first, lane < first + F_PARTS)

        for e in range(2):
            parts = _bf16_parts(f[:, e:e + 1])
            f_lanes = sum(jnp.where(lane == Q_F_LANE + k, parts[k], 0.0) for k in range(F_PARTS))
            nf_lanes = sum(jnp.where(lane == Q_ONE_LANE + k, parts[k], 0.0) for k in range(F_PARTS))
            vals = (
                (fqn, fqt, LANES, head(qf_ref, e) * _SCALE + f_lanes + jnp.where(at(Q_ONE_LANE), 1.0, 0.0)),
                (fkn, fkt, LANES, head(kf_ref, e) + jnp.where(at(Q_F_LANE), 1.0, 0.0) - nf_lanes),
                (fvn, fvt, HEAD_DIM, head(vf_ref, e)),
                (sqn, sqt, LANES, head(qs_ref, e) * _SCALE),
                (skn, skt, LANES, head(ks_ref, e)),
                (svn, svt, HEAD_DIM, head(vs_ref, e)),
            )
            for n_ref, t_ref, rows, val in vals:
                n_ref[e] = val.astype(BF16)
                t_ref[e] = val.T[:rows].astype(BF16)

    col = lambda base: pl.BlockSpec((tm, LANES), lambda i, p: (i, base + p))
    n_spec = pl.BlockSpec((2, tm, LANES), lambda i, p: (p, i, 0))
    t_spec = lambda rows: pl.BlockSpec((2, rows, tm), lambda i, p: (p, 0, i))
    n_sds = jax.ShapeDtypeStruct((H, S, LANES), BF16)
    t_sds = lambda rows: jax.ShapeDtypeStruct((H, rows, S), BF16)
    group = ([n_sds, t_sds(LANES), n_sds, t_sds(LANES), n_sds, t_sds(HEAD_DIM)],
             [n_spec, t_spec(LANES), n_spec, t_spec(LANES), n_spec, t_spec(HEAD_DIM)])
    res = pl.pallas_call(
        body, name="att_prep", grid=(S // tm, n_pairs),
        out_shape=tuple(group[0] * 2),
        in_specs=[col(k * n_pairs) for k in range(6)] + [pl.BlockSpec((None, tm, 2), lambda i, p: (p, i, 0))],
        out_specs=tuple(group[1] * 2),
        compiler_params=_params(dimension_semantics=("parallel", "parallel")),
    )(qkv, qkv, qkv, qkv, qkv, qkv, f_pairs)
    names = ("q_n", "q_t", "k_n", "k_t", "v_n", "v_t")
    return dict(zip(names, res[:6])), dict(zip(names, res[6:]))


def _fox_reach(qt, fend_ref, kmax_ref, h):
    qf = qt.astype(F32)
    q_norm = jnp.sqrt(jnp.sum(jnp.square(qf[:HEAD_DIM]), axis=0, keepdims=True))
    f_t = jnp.sum(qf[Q_F_LANE:Q_F_LANE + F_PARTS], axis=0, keepdims=True)
    return lambda j: q_norm * kmax_ref[h, j] + f_t - fend_ref[h, j]


def _fox_fwd(q_t, k_n, v_t, f_end, k_max, shards):
    H, _, S = q_t.shape
    n, nq = len(shards), S // ATT_BQ

    def body(fend_ref, kmax_ref, qt_ref, k_ref, vt_ref, *rest):
        ins, (ot_ref, lse_ref), outs, sems = rest[:n], rest[n:n + 2], rest[n + 2:2 * n + 2], rest[2 * n + 2:]
        h, i = pl.program_id(0), pl.program_id(1)

        @pl.when(jnp.logical_and(h == 0, i == 0))
        def _():
            for cp in _exchange_copies(ins, outs, *sems, False):
                cp.start()

        qt = qt_ref[...]
        reach = _fox_reach(qt, fend_ref, kmax_ref, h)

        def step(j, carry, masked):
            m, l, acc = carry
            ks = _kv_slice(j)
            s = _dot(k_ref[ks, :], qt)
            if masked:
                s = jnp.where(_mask_t(False), s, _NEG)
            mn = jnp.maximum(m, jnp.max(s, axis=0, keepdims=True))
            alpha = jnp.exp(m - mn)
            p = jnp.exp(s - mn)
            l = alpha * l + jnp.sum(p, axis=0, keepdims=True)
            acc = acc * alpha + _dot(vt_ref[:, ks], p.astype(BF16))
            return mn, l, acc

        def alive(j, carry):
            return jnp.max(reach(j) - carry[0]) > SKIP_BELOW

        row = jnp.zeros((1, ATT_BQ), F32)
        m, l, acc = _walk_down(i, step, alive, (row + _NEG, row, jnp.zeros((HEAD_DIM, ATT_BQ), F32)))
        ot_ref[...] = acc / l
        lse_ref[...] = m + jnp.log(l)

        @pl.when(jnp.logical_and(h == H - 1, i == nq - 1))
        def _():
            for cp in _exchange_copies(ins, outs, *sems, False):
                cp.wait()

    any_spec = pl.BlockSpec(memory_space=pl.ANY)
    grid_spec = pltpu.PrefetchScalarGridSpec(
        num_scalar_prefetch=2, grid=(H, nq),
        in_specs=[_t_block(LANES), _n_full(S), _t_full(HEAD_DIM, S)] + [any_spec] * n,
        out_specs=tuple([_t_block(HEAD_DIM), _t_block(1)] + [any_spec] * n),
        scratch_shapes=_exchange_sems(n))
    res = pl.pallas_call(
        body, name="fox_fwd", grid_spec=grid_spec,
        out_shape=tuple([jax.ShapeDtypeStruct((H, HEAD_DIM, S), F32), jax.ShapeDtypeStruct((H, 1, S), F32)]
                        + _exchange_out_shapes(shards, False)),
        compiler_params=_params(dimension_semantics=("arbitrary", "arbitrary"), has_side_effects=True),
    )(f_end, k_max, q_t, k_n, v_t, *shards)
    return res[0], res[1], res[2:]


def _fox_bwd(q_t, q_n, k_n, k_t, v_n, do_t, do_n, o_t, lse, f_end, k_max):
    H, _, S = q_t.shape

    def body(fend_ref, kmax_ref, qt_ref, qn_ref, k_ref, kt_ref, v_ref, dot_ref, don_ref, ot_ref, lse_ref,
             dqt_ref, dk_ref, dv_ref):
        h, i = pl.program_id(0), pl.program_id(1)

        @pl.when(i == 0)
        def _():
            dk_ref[...] = jnp.zeros_like(dk_ref)
            dv_ref[...] = jnp.zeros_like(dv_ref)

        qt, qn, dot, don = qt_ref[...], qn_ref[...], dot_ref[...], don_ref[...]
        lse = lse_ref[...]
        delta = jnp.sum(dot[:HEAD_DIM].astype(F32) * ot_ref[...], axis=0, keepdims=True)
        reach = _fox_reach(qt, fend_ref, kmax_ref, h)

        def alive(j, dq):
            return jnp.max(reach(j) - lse) > SKIP_BELOW

        def step(j, dq, masked):
            ks = _kv_slice(j)
            s = _dot(k_ref[ks, :], qt)
            if masked:
                s = jnp.where(_mask_t(False), s, _NEG)
            p = jnp.exp(s - lse)
            ds = (p * (_dot(v_ref[ks, :], dot) - delta)).astype(BF16)
            dk_ref[ks, :] += _dot(ds, qn)
            dv_ref[ks, :] += _dot(p.astype(BF16), don)
            return dq + _dot(kt_ref[:, ks], ds)

        dqt_ref[...] = _walk_down(i, step, alive, jnp.zeros((LANES, ATT_BQ), F32))

    grid_spec = pltpu.PrefetchScalarGridSpec(
        num_scalar_prefetch=2, grid=(H, S // ATT_BQ),
        in_specs=[_t_block(LANES), _n_block(), _n_full(S), _t_full(LANES, S), _n_full(S),
                  _t_block(LANES), _n_block(), _t_block(HEAD_DIM), _t_block(1)],
        out_specs=(_t_block(LANES), _n_full(S), _n_full(S)))
    return pl.pallas_call(
        body, name="fox_bwd", grid_spec=grid_spec,
        out_shape=(jax.ShapeDtypeStruct((H, LANES, S), F32), jax.ShapeDtypeStruct((H, S, LANES), F32),
                   jax.ShapeDtypeStruct((H, S, LANES), F32)),
        compiler_params=_params(dimension_semantics=("parallel", "arbitrary")),
    )(f_end, k_max, q_t, q_n, k_n, k_t, v_n, do_t, do_n, o_t, lse)


def _scan_lhs():
    r = lax.broadcasted_iota(jnp.int32, (SCAN_BK, 2 * SCAN_BK), 0)
    c = lax.broadcasted_iota(jnp.int32, (SCAN_BK, 2 * SCAN_BK), 1) % SCAN_BK
    return (c >= r).astype(BF16)


def _suffix_sum(t, lhs):
    hi = t.astype(BF16)
    lo = (t - hi.astype(F32)).astype(BF16)
    return _dot(lhs, jnp.concatenate([hi, lo], axis=0))


def _sb_scores(k, qt, mask):
    z = _dot(k, qt)
    e = jnp.exp(-jnp.abs(z))
    lb = -(jnp.maximum(z, 0.0) + jnp.log(1.0 + e))
    if mask is not None:
        lb = jnp.where(mask, lb, 0.0)
    return z, e, lb


def _scan_blocks():
    return [slice(u * SCAN_BK, (u + 1) * SCAN_BK) for u in reversed(range(N_SCAN))]


def _sb_fwd(q_t, k_n, v_t):
    H, _, S = q_t.shape

    def body(qt_ref, k_ref, vt_ref, ot_ref):
        i = pl.program_id(1)
        qt = qt_ref[...]
        lhs = _scan_lhs()

        def step(j, carry, masked):
            c, acc = carry
            ks = _kv_slice(j)
            mask = _mask_t(True) if masked else None
            z, _, lb = _sb_scores(k_ref[ks, :], qt, mask)
            parts = []
            for sl in _scan_blocks():
                rin = _suffix_sum(lb[sl], lhs)
                a = jnp.exp(z[sl] + rin + c)
                if masked:
                    a = jnp.where(mask[sl], a, 0.0)
                parts.append(a.astype(BF16))
                c = c + rin[0:1, :]
            a_all = jnp.concatenate(parts[::-1], axis=0)
            return c, acc + _dot(vt_ref[:, ks], a_all)

        carry = (jnp.zeros((1, ATT_BQ), F32), jnp.zeros((HEAD_DIM, ATT_BQ), F32))
        ot_ref[...] = _walk_down(i, step, lambda j, cr: jnp.max(cr[0]) > SKIP_BELOW, carry)[1]

    return pl.pallas_call(
        body, name="sb_fwd", grid=(H, S // ATT_BQ),
        out_shape=jax.ShapeDtypeStruct((H, HEAD_DIM, S), F32),
        in_specs=[_t_block(LANES), _n_full(S), _t_full(HEAD_DIM, S)],
        out_specs=_t_block(HEAD_DIM),
        compiler_params=_params(dimension_semantics=("parallel", "parallel")),
    )(q_t, k_n, v_t)


def _sb_bwd(q_t, q_n, k_n, k_t, v_n, do_t, do_n, o_t, bound):
    H, _, S = q_t.shape
    n, nq = len(bound), S // ATT_BQ

    def body(qt_ref, qn_ref, k_ref, kt_ref, v_ref, dot_ref, don_ref, ot_ref, *rest):
        ins, (dqt_ref, dk_ref, dv_ref) = rest[:n], rest[n:n + 3]
        outs, sems = rest[n + 3:2 * n + 3], rest[2 * n + 3:]
        h, i = pl.program_id(0), pl.program_id(1)

        @pl.when(jnp.logical_and(h == 0, i == 0))
        def _():
            for cp in _exchange_copies(ins, outs, *sems, True):
                cp.start()

        @pl.when(i == 0)
        def _():
            dk_ref[...] = jnp.zeros_like(dk_ref)
            dv_ref[...] = jnp.zeros_like(dv_ref)

        qt, qn, dot, don = qt_ref[...], qn_ref[...], dot_ref[...], don_ref[...]
        lhs = _scan_lhs()
        delta = jnp.sum(dot[:HEAD_DIM].astype(F32) * ot_ref[...], axis=0, keepdims=True)

        def step(j, carry, masked):
            c, g, dq = carry
            ks = _kv_slice(j)
            mask = _mask_t(True) if masked else None
            z, e, lb = _sb_scores(k_ref[ks, :], qt, mask)
            da = _dot(v_ref[ks, :], dot)
            a_parts, dz_parts = [], []
            for sl in _scan_blocks():
                rin = _suffix_sum(lb[sl], lhs)
                a = jnp.exp(z[sl] + rin + c)
                if masked:
                    a = jnp.where(mask[sl], a, 0.0)
                ab = a.astype(BF16)
                gg = ab.astype(F32) * da[sl]
                rgin = _suffix_sum(gg, lhs)
                rinv = 1.0 / (1.0 + e[sl])
                sig = jnp.where(z[sl] >= 0.0, rinv, e[sl] * rinv)
                dz = gg - sig * (delta - g - (rgin - gg))
                if masked:
                    dz = jnp.where(mask[sl], dz, 0.0)
                a_parts.append(ab)
                dz_parts.append(dz.astype(BF16))
                c = c + rin[0:1, :]
                g = g + rgin[0:1, :]
            ab_all = jnp.concatenate(a_parts[::-1], axis=0)
            dzb = jnp.concatenate(dz_parts[::-1], axis=0)
            dk_ref[ks, :] += _dot(dzb, qn)
            dv_ref[ks, :] += _dot(ab_all, don)
            return c, g, dq + _dot(kt_ref[:, ks], dzb)

        row = jnp.zeros((1, ATT_BQ), F32)
        carry = (row, row, jnp.zeros((LANES, ATT_BQ), F32))
        dqt_ref[...] = _walk_down(i, step, lambda j, cr: jnp.max(cr[0]) > SKIP_BELOW, carry)[2]

        @pl.when(jnp.logical_and(h == H - 1, i == nq - 1))
        def _():
            for cp in _exchange_copies(ins, outs, *sems, True):
                cp.wait()

    any_spec = pl.BlockSpec(memory_space=pl.ANY)
    res = pl.pallas_call(
        body, name="sb_bwd", grid=(H, nq),
        out_shape=tuple([jax.ShapeDtypeStruct((H, LANES, S), F32), jax.ShapeDtypeStruct((H, S, LANES), F32),
                         jax.ShapeDtypeStruct((H, S, LANES), F32)] + _exchange_out_shapes(bound, True)),
        in_specs=[_t_block(LANES), _n_block(), _n_full(S), _t_full(LANES, S), _n_full(S),
                  _t_block(LANES), _n_block(), _t_block(HEAD_DIM)] + [any_spec] * n,
        out_specs=tuple([_t_block(LANES), _n_full(S), _n_full(S)] + [any_spec] * n),
        scratch_shapes=_exchange_sems(n),
        compiler_params=_params(dimension_semantics=("arbitrary", "arbitrary"), has_side_effects=True),
    )(q_t, q_n, k_n, k_t, v_n, do_t, do_n, o_t, *bound)
    return res[0], res[1], res[2], res[3:]


def _dqkv_assemble(dqf_t, dkf, dvf, dqs_t, dks, dvs):
    H, _, S = dqf_t.shape
    n_pairs = H // 2
    tm = _tile(S, 512)

    def body(dqf_ref, dkf_ref, dvf_ref, dqs_ref, dks_ref, dvs_ref, out_ref, dfk_ref):
        lane = lax.broadcasted_iota(jnp.int32, (1, LANES), 1)
        slabs = ((dqf_ref, True), (dkf_ref, False), (dvf_ref, False),
                 (dqs_ref, True), (dks_ref, False), (dvs_ref, False))
        for k, (ref, transposed) in enumerate(slabs):
            if transposed:
                t0, t1 = ref[0].T * _SCALE, ref[1].T * _SCALE
            else:
                t0, t1 = ref[0], ref[1]
            out_ref[k] = jnp.where(lane < HEAD_DIM, t0, pltpu.roll(t1, HEAD_DIM, 1)).astype(BF16)
        for e in range(2):
            dfk_ref[e] = dkf_ref[e].T[Q_ONE_LANE:Q_ONE_LANE + 1, :]

    t_spec = pl.BlockSpec((2, LANES, tm), lambda i, p: (p, 0, i))
    n_spec = pl.BlockSpec((2, tm, LANES), lambda i, p: (p, i, 0))
    return pl.pallas_call(
        body, name="dqkv_assemble", grid=(S // tm, n_pairs),
        out_shape=(jax.ShapeDtypeStruct((6, S, n_pairs * LANES), BF16), jax.ShapeDtypeStruct((H, 1, S), F32)),
        in_specs=[t_spec, n_spec, n_spec, t_spec, n_spec, n_spec],
        out_specs=(pl.BlockSpec((6, tm, LANES), lambda i, p: (0, i, p)),
                   pl.BlockSpec((2, 1, tm), lambda i, p: (p, 0, i))),
        compiler_params=_params(dimension_semantics=("parallel", "parallel")),
    )(dqf_t, dkf, dvf, dqs_t, dks, dvs)


def _acc_spec(w):
    return pl.BlockSpec((1, w), lambda i: (0, 0))


def _loss_head(x1, act, w_down, gate_m, g_final, target):
    S, D = x1.shape
    F = act.shape[1]
    tm = _tile(S, 512)

    def body(x1_ref, act_ref, w_ref, gt_ref, gf_ref, tg_ref, dx2_ref, gm_ref, loss_ref, dgf_ref, dgt_ref):
        @pl.when(pl.program_id(0) == 0)
        def _():
            loss_ref[...] = jnp.zeros_like(loss_ref)
            dgf_ref[...] = jnp.zeros_like(dgf_ref)
            dgt_ref[...] = jnp.zeros_like(dgt_ref)

        mo = _dot(act_ref[...], w_ref[...])
        x2 = x1_ref[...] + gt_ref[...] * mo
        r = lax.rsqrt(jnp.mean(x2 * x2, axis=-1, keepdims=True) + EPS)
        xh = x2 * r
        diff = xh * gf_ref[...] - tg_ref[...]
        loss_ref[...] += (0.5 / D) * jnp.sum(diff * diff)
        dy = diff * (1.0 / D)
        dgf_ref[...] += jnp.sum(dy * xh, axis=0, keepdims=True)
        dxh = dy * gf_ref[...]
        dx2 = r * (dxh - xh * jnp.mean(dxh * xh, axis=-1, keepdims=True))
        dx2_ref[...] = dx2
        gm_ref[...] = (dx2 * gt_ref[...]).astype(BF16)
        dgt_ref[...] += jnp.sum(dx2 * mo, axis=0, keepdims=True)

    return pl.pallas_call(
        body, name="loss_head", grid=(S // tm,),
        out_shape=(jax.ShapeDtypeStruct((S, D), F32), jax.ShapeDtypeStruct((S, D), BF16),
                   jax.ShapeDtypeStruct((1, LANES), F32), jax.ShapeDtypeStruct((1, D), F32),
                   jax.ShapeDtypeStruct((1, D), F32)),
        in_specs=_row_specs(tm, [D, F]) + [pl.BlockSpec((F, D), lambda i: (0, 0))] + [_vec_spec(D)] * 2
        + _row_specs(tm, [D]),
        out_specs=tuple(_row_specs(tm, [D, D]) + [_acc_spec(LANES), _acc_spec(D), _acc_spec(D)]),
        compiler_params=_params(dimension_semantics=("arbitrary",)),
    )(x1, act, w_down, gate_m, g_final, target)


def _norm_bwd(lhs, rhs, xin, dres, g, scale, name, gate=None, branch=None, bound=(), conv=None, tm=256):
    S, D = xin.shape
    tm = _tile(S, tm)
    gated = gate is not None
    nl, nb, n_steps = len(lhs), len(bound), S // tm
    nc_ = nl if conv else 0
    n_out = (6 if gated else 4) + nc_

    def body(*refs):
        l_refs, r_refs, rest = refs[:nl], refs[nl:2 * nl], refs[2 * nl:]
        halo_refs, cw_refs, rest = rest[:nc_], rest[nc_:2 * nc_], rest[2 * nc_:]
        if gated:
            x_ref, dr_ref, g_ref, sc_ref, gt_ref, br_ref = rest[:6]
            rest = rest[6:]
        else:
            x_ref, dr_ref, g_ref, sc_ref = rest[:4]
            rest = rest[4:]
        ins, outs, ex_outs, sems = rest[:nb], rest[nb:nb + n_out], rest[nb + n_out:2 * nb + n_out], rest[2 * nb + n_out:]
        dx_ref, dsc_ref, dsh_ref, dg_ref = outs[:4]
        sums = (dsc_ref, dsh_ref, dg_ref) + ((outs[5],) if gated else ())
        dup_refs = outs[n_out - nc_:]
        i = pl.program_id(0)

        @pl.when(i == 0)
        def _():
            for s_ref in sums:
                s_ref[...] = jnp.zeros_like(s_ref)
            if nb:
                for cp in _chip_exchange_copies(ins, ex_outs, *sems):
                    cp.start()

        dhv = None
        for k, (l_ref, r_ref) in enumerate(zip(l_refs, r_refs)):
            if conv:
                halo = jnp.where(i == n_steps - 1, 0.0, halo_refs[k][...].astype(F32))
                dup = _conv_bwd_taps(l_ref[...].astype(F32), halo, cw_refs[k]).astype(BF16)
                dup_refs[k][...] = dup
                terms = [_dot_nt(dup, r_ref[...])]
            elif len(l_ref.shape) == 3:
                K = l_ref.shape[2]
                terms = [_dot_nt(l_ref[k], r_ref[:, k * K:(k + 1) * K]) for k in range(l_ref.shape[0])]
            else:
                terms = [_dot_nt(l_ref[...], r_ref[...])]
            for t in terms:
                dhv = t if dhv is None else dhv + t
        xv = x_ref[...]
        r = lax.rsqrt(jnp.mean(xv * xv, axis=-1, keepdims=True) + EPS)
        xh = xv * r
        dsc_ref[...] += jnp.sum(dhv * (xh * g_ref[...]), axis=0, keepdims=True)
        dsh_ref[...] += jnp.sum(dhv, axis=0, keepdims=True)
        dn = dhv * (1.0 + sc_ref[...])
        dg_ref[...] += jnp.sum(dn * xh, axis=0, keepdims=True)
        dxh = dn * g_ref[...]
        dx = dr_ref[...] + r * (dxh - xh * jnp.mean(dxh * xh, axis=-1, keepdims=True))
        dx_ref[...] = dx
        if gated:
            outs[4][...] = (dx * gt_ref[...]).astype(BF16)
            outs[5][...] += jnp.sum(dx * br_ref[...], axis=0, keepdims=True)

        if nb:
            @pl.when(i == n_steps - 1)
            def _():
                for cp in _chip_exchange_copies(ins, ex_outs, *sems):
                    cp.wait()

    def l_spec(a):
        if a.ndim == 3:
            return pl.BlockSpec((a.shape[0], tm, a.shape[2]), lambda i: (0, i, 0))
        return pl.BlockSpec((tm, a.shape[1]), lambda i: (i, 0))

    any_spec = pl.BlockSpec(memory_space=pl.ANY)
    vec = jax.ShapeDtypeStruct((1, D), F32)
    out_shape = [jax.ShapeDtypeStruct((S, D), F32), vec, vec, vec]
    out_specs = _row_specs(tm, [D]) + [_acc_spec(D)] * 3
    in_specs = [l_spec(a) for a in lhs] + [pl.BlockSpec(b.shape, lambda i: (0, 0)) for b in rhs]
    args = list(lhs) + list(rhs)
    if conv:
        step, last_halo = tm // HALO, S // HALO - 1
        in_specs += [pl.BlockSpec((HALO, a.shape[1]), lambda i: (jnp.minimum((i + 1) * step, last_halo), 0))
                     for a in lhs]
        in_specs += [pl.BlockSpec(w.shape, lambda i: (0, 0)) for w in conv]
        args += list(lhs) + list(conv)
    in_specs += _row_specs(tm, [D, D]) + [_vec_spec(D)] * 2
    args += [xin, dres, g, scale]
    if gated:
        out_shape += [jax.ShapeDtypeStruct((S, D), BF16), vec]
        out_specs += _row_specs(tm, [D]) + [_acc_spec(D)]
        in_specs += [_vec_spec(D)] + _row_specs(tm, [D])
        args += [gate, branch]
    if conv:
        out_shape += [jax.ShapeDtypeStruct(a.shape, BF16) for a in lhs]
        out_specs += [l_spec(a) for a in lhs]
    res = pl.pallas_call(
        body, name=name, grid=(n_steps,),
        out_shape=tuple(out_shape + [jax.ShapeDtypeStruct(b.shape, b.dtype) for b in bound]),
        in_specs=in_specs + [any_spec] * nb, out_specs=tuple(out_specs + [any_spec] * nb),
        scratch_shapes=_chip_exchange_sems(nb) if nb else [],
        compiler_params=_params(dimension_semantics=("arbitrary",), has_side_effects=bool(nb)),
    )(*args, *bound)
    return tuple(res[:n_out]) + (tuple(res[n_out:]),)


def _headnorm_bwd(dmix, o_f, o_s, g_f, g_s):
    dh, S = o_f.shape
    H = dh // HEAD_DIM
    tm = _tile(S, 256)

    def body(dm_ref, of_ref, os_ref, gf_ref, gs_ref, fn_ref, ft_ref, sn_ref, st_ref, dgf_ref, dgs_ref):
        @pl.when(pl.program_id(0) == 0)
        def _():
            dgf_ref[...] = jnp.zeros_like(dgf_ref)
            dgs_ref[...] = jnp.zeros_like(dgs_ref)

        ones = _group_ones()
        lane = lax.broadcasted_iota(jnp.int32, (1, LANES), 1)
        parts = ((of_ref, gf_ref, fn_ref, ft_ref, dgf_ref), (os_ref, gs_ref, sn_ref, st_ref, dgs_ref))
        for part, (o_ref, g_ref, n_ref, t_ref, dg_ref) in enumerate(parts):
            for t in range(dh // LANES):
                cols = slice(t * LANES, (t + 1) * LANES)
                o = o_ref[cols, :].T
                dm = dm_ref[:, part * dh + t * LANES: part * dh + (t + 1) * LANES]
                r = lax.rsqrt(_dot(o * o, ones, precision=HIGHEST) * (1.0 / HEAD_DIM) + EPS)
                oh = o * r
                dg_ref[:, cols] += jnp.sum(dm * oh, axis=0, keepdims=True)
                dn = dm * g_ref[:, cols]
                mean = _dot(dn * oh, ones, precision=HIGHEST) * (1.0 / HEAD_DIM)
                do = r * (dn - oh * mean)
                for e in range(2):
                    d = do if e == 0 else pltpu.roll(do, HEAD_DIM, 1)
                    d = jnp.where(lane < HEAD_DIM, d, 0.0)
                    n_ref[2 * t + e] = d.astype(BF16)
                    t_ref[2 * t + e] = d.T.astype(BF16)

    vec = jax.ShapeDtypeStruct((1, dh), F32)
    n_sds = jax.ShapeDtypeStruct((H, S, LANES), BF16)
    t_sds = jax.ShapeDtypeStruct((H, LANES, S), BF16)
    n_spec = pl.BlockSpec((H, tm, LANES), lambda i: (0, i, 0))
    t_spec = pl.BlockSpec((H, LANES, tm), lambda i: (0, 0, i))
    return pl.pallas_call(
        body, name="headnorm_bwd", grid=(S // tm,),
        out_shape=(n_sds, t_sds, n_sds, t_sds, vec, vec),
        in_specs=_row_specs(tm, [2 * dh]) + [_col_spec(tm, dh)] * 2 + [_vec_spec(dh)] * 2,
        out_specs=(n_spec, t_spec, n_spec, t_spec, _acc_spec(dh), _acc_spec(dh)),
        compiler_params=_params(dimension_semantics=("arbitrary",)),
    )(dmix, o_f, o_s, g_f, g_s)


def _adamw(w, gslots, m, v, name):
    R, C = w.shape
    n = gslots.shape[0]
    tr = 256 if (R % 256 == 0 and R > 256) else R
    bc1 = 1.0 - ADAM_B1 ** ADAM_STEP
    bc2 = 1.0 - ADAM_B2 ** ADAM_STEP

    def body(w_ref, gs_ref, m_ref, v_ref, g_ref, d_ref, nm_ref, nv_ref):
        g = gs_ref[0]
        for s in range(1, n):
            g = g + gs_ref[s]
        nm = ADAM_B1 * m_ref[...] + (1.0 - ADAM_B1) * g
        nv = ADAM_B2 * v_ref[...] + (1.0 - ADAM_B2) * (g * g)
        g_ref[...] = g
        nm_ref[...] = nm
        nv_ref[...] = nv
        d_ref[...] = -ADAM_LR * ((nm / bc1) / (jnp.sqrt(nv / bc2) + ADAM_EPS) + ADAM_WD * w_ref[...])

    blk = pl.BlockSpec((tr, C), lambda i: (i, 0))
    sds = jax.ShapeDtypeStruct((R, C), F32)
    return pl.pallas_call(
        body, name=name, grid=(R // tr,), out_shape=(sds,) * 4,
        in_specs=[blk, pl.BlockSpec((n, tr, C), lambda i: (0, i, 0)), blk, blk], out_specs=(blk,) * 4,
        compiler_params=_params(dimension_semantics=("parallel",)),
    )(w, gslots, m, v)


def _slot_sum(slots, name):
    n, _, C = slots.shape

    def body(s_ref, o_ref):
        acc = s_ref[0]
        for s in range(1, n):
            acc = acc + s_ref[s]
        o_ref[...] = acc

    return pl.pallas_call(body, name=name, out_shape=jax.ShapeDtypeStruct((1, C), F32),
                          compiler_params=_params())(slots)


def _pad_cols(a, n):
    return jnp.pad(a, ((0, 0), (0, n - a.shape[1])))


def _ungather(g, axis):
    if axis == 0:
        return g.reshape(g.shape[0] * g.shape[1], g.shape[2])
    return jnp.transpose(g, (1, 0, 2)).reshape(g.shape[1], g.shape[0] * g.shape[2])


def _to_slots(full, axis):
    R, C = full.shape
    if axis == 0:
        return full.reshape(N_DEV, R // N_DEV, C)
    return jnp.transpose(full.reshape(R, N_DEV, C // N_DEV), (1, 0, 2))


def kernel(x, c, w_ada, b_ada, g_attn, w_in, b_fgate, g_out_fox, g_out_sb, w_out, g_mlp, w_up, conv_w, conv_b, w_down, g_final, loss_target, m_w_ada, m_b_ada, m_g_attn, m_w_in, m_b_fgate, m_g_out_fox, m_g_out_sb, m_w_out, m_g_mlp, m_w_up, m_conv_w, m_conv_b, m_w_down, m_g_final, v_w_ada, v_b_ada, v_g_attn, v_w_in, v_b_fgate, v_g_out_fox, v_g_out_sb, v_w_out, v_g_mlp, v_w_up, v_conv_w, v_conv_b, v_w_down, v_g_final):
    S, D = x.shape[1], x.shape[2]
    dh = D // 2
    n_heads = dh // HEAD_DIM
    n_qkv = 6 * dh
    ff = w_down.shape[1] * N_DEV
    ffp = -(-ff // (2 * LANES)) * (2 * LANES)
    nc = S // LANES
    me = 4 * lax.axis_index("x") + 2 * lax.axis_index("y") + lax.axis_index("c")
    xs, tgt = x[0], loss_target[0]

    c_all, win_g = _gather_two_level([c, w_in[0].astype(BF16)], name="gather_first")
    c_all = c_all.reshape(N_DEV, D)
    W_in = _ungather(win_g, 1)
    W_qkv, W_f = W_in[:, :n_qkv], _pad_cols(W_in[:, n_qkv:], LANES)
    cb_g, cb_v = _pad_cols(conv_b[:, :ff], ffp), _pad_cols(conv_b[:, ff:], ffp)

    n_ada = w_ada.shape[2]
    b_shard = lax.dynamic_slice(b_ada, (0, me * n_ada), (1, n_ada))
    mod_cols = _ada_fwd(c_all, w_ada[0], b_shard)
    (mod_g,) = _exchange([mod_cols], scatter=False, name="gather_mod")
    mod = lax.dynamic_index_in_dim(mod_g, me, axis=1, keepdims=False).reshape(6, 1, D)
    shift_a, scale_a, gate_a, shift_m, scale_m, gate_m = [mod[k] for k in range(6)]

    h1, h1_t = _prenorm(xs, g_attn, scale_a, shift_a, "prenorm_attn")
    qkv = _mm(h1, W_qkv, BF16, "proj_qkv")
    flog = _mm(h1, W_f, F32, "proj_fgate")
    zf = flog[:, :n_heads] + b_fgate
    z_rows = zf.T.reshape(n_heads * nc, LANES)
    f_rows = _fgate_fwd(z_rows, nc).reshape(n_heads, S)
    f_pairs = jnp.transpose(f_rows.reshape(n_heads // 2, 2, S), (0, 2, 1))
    fox, sb = _att_prep(qkv, f_pairs)
    f_end, k_max = _skip_bounds(qkv[:, dh:2 * dh], f_rows)
    of_t, lse, (wout_g, wup_g, wdown_g, convw_g) = _fox_fwd(
        fox["q_t"], fox["k_n"], fox["v_t"], f_end, k_max,
        [w_out[0].astype(BF16), w_up[0].astype(BF16), w_down[0].astype(BF16), conv_w[0]])
    W_out = _ungather(wout_g, 0)
    W_up = _ungather(wup_g, 1)
    W_g, W_v = _pad_cols(W_up[:, :ff], ffp), _pad_cols(W_up[:, ff:], ffp)
    W_down = jnp.pad(_ungather(wdown_g, 0), ((0, ffp - ff), (0, 0)))
    cw_full = _ungather(convw_g, 1)
    cw_g, cw_v = _pad_cols(cw_full[:, :ff], ffp), _pad_cols(cw_full[:, ff:], ffp)
    os_t = _sb_fwd(sb["q_t"], sb["k_n"], sb["v_t"])
    o_f, o_s = of_t.reshape(dh, S), os_t.reshape(dh, S)
    mix, mix_t = _headnorm_fwd(o_f, o_s, g_out_fox, g_out_sb)
    a_out = _mm(mix, W_out, F32, "proj_out")
    x1, h2, h2_t = _resid_prenorm(xs, a_out, gate_a, g_mlp, scale_m, shift_m)
    up_g, up_v, act, act_t = _mlp_up(h2, W_g, W_v, cw_g, cw_v, cb_g, cb_v)

    dx2, gm, loss_p, dg_final, dgate_m = _loss_head(x1, act, W_down, gate_m, g_final.reshape(1, D), tgt)
    dW_down = _mm_acc(act_t, gm, "bwd_down_w")
    du_g, du_v, p_g, p_v = _conv_act_bwd(gm, W_down, up_g, up_v, cw_g, cw_v, cb_g, cb_v)
    dx1, dscale_m, dshift_m, dg_mlp, ga, dgate_a, dup_g, dup_v, _ = _norm_bwd(
        [du_g, du_v], [W_g, W_v], x1, dx2, g_mlp, scale_m, "norm_mlp_bwd", gate=gate_a, branch=a_out,
        conv=[cw_g, cw_v])
    dW_g = _mm_acc(h2_t, dup_g, "bwd_up_w_g")
    dW_v = _mm_acc(h2_t, dup_v, "bwd_up_w_v")
    dmix = _mm(ga, W_out, F32, "bwd_out_act", nt=True)
    dW_out = _mm_acc(mix_t, ga, "bwd_out_w")
    dof_n, dof_t, dos_n, dos_t, dg_fox, dg_sb = _headnorm_bwd(dmix, o_f, o_s, g_out_fox, g_out_sb)
    dqf_t, dkf, dvf = _fox_bwd(fox["q_t"], fox["q_n"], fox["k_n"], fox["k_t"], fox["v_n"], dof_t, dof_n, of_t, lse,
                              f_end, k_max)
    dW_upf = jnp.concatenate([dW_g[:, :ff], dW_v[:, :ff]], axis=1)
    dcw = jnp.concatenate([p_g[:CONV_W, :ff], p_v[:CONV_W, :ff]], axis=1)
    dqs_t, dks, dvs, (s_out, s_up, s_down, s_cw) = _sb_bwd(
        sb["q_t"], sb["q_n"], sb["k_n"], sb["k_t"], sb["v_n"], dos_t, dos_n, os_t,
        [_to_slots(dW_out, 0), _to_slots(dW_upf, 1), _to_slots(dW_down[:ff], 0), _to_slots(dcw, 1)])
    dparts, dfk = _dqkv_assemble(dqf_t, dkf, dvf, dqs_t, dks, dvs)
    dz_rows, db_fgate = _fgate_bwd(dfk.reshape(n_heads * nc, LANES),
                                   dqf_t[:, Q_F_LANE, :].reshape(n_heads * nc, LANES), z_rows, nc)
    dzf = _pad_cols(dz_rows.reshape(n_heads, S).T, LANES).astype(BF16)
    dW_qkv = _mm_acc_parts(h1_t, dparts, "bwd_in_w")
    dW_f = _mm_acc(h1_t, dzf, "bwd_in_w_fgate")
    dW_in = jnp.concatenate([jnp.transpose(dW_qkv, (1, 0, 2)).reshape(D, n_qkv), dW_f[:, :n_heads]], axis=1)
    bound_in = _to_slots(dW_in, 1)
    bound_in = bound_in.reshape((N_CHIPS, 2) + bound_in.shape[1:])
    (got_in,) = _scatter_to_sibling([bound_in], "scatter_sibling")
    c_idx = lax.axis_index("c").astype(jnp.int32).reshape(1)
    grad_x, dscale_a, dshift_a, dg_attn, (s_in,) = _norm_bwd(
        [dparts, dzf], [W_qkv, W_f], xs, dx1, g_attn, scale_a, "norm_attn_bwd",
        bound=[_pair_add(bound_in, got_in, c_idx, "pair_add")], tm=512)

    dconv_b = jnp.concatenate([p_g[CONV_W:CONV_W + 1, :ff], p_v[CONV_W:CONV_W + 1, :ff]], axis=1)
    parts = [dshift_a, dscale_a, dgate_a, dshift_m, dscale_m, dgate_m,
             dg_attn, db_fgate.reshape(1, n_heads), dg_fox, dg_sb, dg_mlp, dconv_b, dg_final,
             loss_p[:, :1]]
    sizes = [p.shape[1] for p in parts]
    vec = jnp.concatenate(parts, axis=1)
    n_vec = -(-vec.shape[1] // LANES) * LANES
    vec = _pad_cols(vec, n_vec)
    (vec_g,) = _exchange([vec], scatter=False, name="gather_small")
    offs = [0]
    for s in sizes:
        offs.append(offs[-1] + s)

    def small(k0, k1=None):
        k1 = k0 if k1 is None else k1
        return vec_g[:, :, offs[k0]:offs[k1 + 1]]

    dmod_all = small(0, 5).reshape(N_DEV, 6 * D)
    dmod_cols = lax.dynamic_slice(dmod_all, (0, me * n_ada), (N_DEV, n_ada))
    dW_ada = _ada_bwd(c_all.T, dmod_cols)


    res = {}
    res["w_ada"] = _adamw(w_ada[0], dW_ada[None], m_w_ada[0], v_w_ada[0], "adamw_w_ada")
    res["w_in"] = _adamw(w_in[0], s_in, m_w_in[0], v_w_in[0], "adamw_w_in")
    res["w_out"] = _adamw(w_out[0], s_out, m_w_out[0], v_w_out[0], "adamw_w_out")
    res["w_up"] = _adamw(w_up[0], s_up, m_w_up[0], v_w_up[0], "adamw_w_up")
    res["w_down"] = _adamw(w_down[0], s_down, m_w_down[0], v_w_down[0], "adamw_w_down")
    res["conv_w"] = _adamw(conv_w[0], s_cw, m_conv_w[0], v_conv_w[0], "adamw_conv_w")
    small_names = ["b_ada", "g_attn", "b_fgate", "g_out_fox", "g_out_sb", "g_mlp", "conv_b", "g_final"]
    small_w = [b_ada, g_attn, b_fgate, g_out_fox, g_out_sb, g_mlp, conv_b, g_final.reshape(1, D)]
    small_m = [m_b_ada, m_g_attn, m_b_fgate, m_g_out_fox, m_g_out_sb, m_g_mlp, m_conv_b, m_g_final.reshape(1, D)]
    small_v = [v_b_ada, v_g_attn, v_b_fgate, v_g_out_fox, v_g_out_sb, v_g_mlp, v_conv_b, v_g_final.reshape(1, D)]
    small_res = _adamw(jnp.concatenate(small_w, axis=1), small(0, 12), jnp.concatenate(small_m, axis=1),
                       jnp.concatenate(small_v, axis=1), "adamw_small")
    lo = 0
    for nm, wv in zip(small_names, small_w):
        res[nm] = tuple(r[:, lo:lo + wv.shape[1]] for r in small_res)
        lo += wv.shape[1]
    loss = _slot_sum(_pad_cols(small(13).reshape(N_DEV, 1), LANES).reshape(N_DEV, 1, LANES), "loss_sum")[0, 0]

    names = ["w_ada", "b_ada", "g_attn", "w_in", "b_fgate", "g_out_fox", "g_out_sb", "w_out", "g_mlp",
             "w_up", "conv_w", "conv_b", "w_down", "g_final"]

    def shaped(n, a):
        if n == "g_final":
            return a.reshape(D)
        if n in ("b_ada", "g_attn", "b_fgate", "g_out_fox", "g_out_sb", "g_mlp", "conv_b"):
            return a
        return a[None]

    outs = [loss, grad_x[None]]
    for k in range(4):
        outs += [shaped(n, res[n][k]) for n in names]
    return tuple(outs)
```

```python
import jax
import jax.numpy as jnp
from jax import lax
from jax.experimental import pallas as pl
from jax.experimental.pallas import tpu as pltpu

F32 = jnp.float32
BF16 = jnp.bfloat16
HIGHEST = lax.Precision.HIGHEST

N_DEV = 8
LANES = 128
HEAD_DIM = 64
EPS = 1e-6
CONV_W = 3
CONV_COLS = 1408
HALO = 16
ATT_BQ = 512
ATT_BK = 512
SCAN_BK = 128
VMEM_LIMIT = 56 * 1024 * 1024

ADAM_LR = 0.001
ADAM_B1 = 0.9
ADAM_B2 = 0.999
ADAM_EPS = 1e-08
ADAM_WD = 0.01
ADAM_STEP = 10


def _params(**kw):
    return pltpu.CompilerParams(vmem_limit_bytes=VMEM_LIMIT, **kw)


def _tile(n, cap):
    if n <= cap:
        return n
    best = None
    for t in range(LANES, cap + 1, LANES):
        if n % t == 0:
            best = t
    assert best is not None, (n, cap)
    return best


def _dot(a, b, **kw):
    return jnp.dot(a, b, preferred_element_type=F32, **kw)


def _exchange_copies(ins, outs, send_sems, recv_sems, loc_sems, scatter):
    n = len(ins)
    if n == 0:
        return []
    x, y, c = lax.axis_index("x"), lax.axis_index("y"), lax.axis_index("c")
    me = 4 * x + 2 * y + c
    copies = []
    for a in range(n):
        src = ins[a].at[me] if scatter else ins[a]
        copies.append(pltpu.make_async_copy(src, outs[a].at[me], loc_sems.at[a]))
    for k in range(1, N_DEV):
        px = 1 - x if k & 4 else x
        py = 1 - y if k & 2 else y
        pc = 1 - c if k & 1 else c
        peer = 4 * px + 2 * py + pc
        for a in range(n):
            src = ins[a].at[peer] if scatter else ins[a]
            copies.append(pltpu.make_async_remote_copy(
                src_ref=src, dst_ref=outs[a].at[me],
                send_sem=send_sems.at[a, k - 1], recv_sem=recv_sems.at[a, k - 1],
                device_id=(px, py, pc), device_id_type=pl.DeviceIdType.MESH))
    return copies


def _exchange_out_shapes(arrays, scatter):
    return [jax.ShapeDtypeStruct((N_DEV,) + tuple(a.shape[1:] if scatter else a.shape), a.dtype) for a in arrays]


def _exchange_sems(n):
    return [pltpu.SemaphoreType.DMA((n, N_DEV - 1)), pltpu.SemaphoreType.DMA((n, N_DEV - 1)),
            pltpu.SemaphoreType.DMA((n,))]


def _exchange(arrays, scatter, name):
    n = len(arrays)

    def body(*refs):
        copies = _exchange_copies(refs[:n], refs[n:2 * n], *refs[2 * n:], scatter)
        for cp in copies:
            cp.start()
        for cp in copies:
            cp.wait()

    any_spec = pl.BlockSpec(memory_space=pl.ANY)
    return pl.pallas_call(
        body, name=name, out_shape=tuple(_exchange_out_shapes(arrays, scatter)),
        in_specs=[any_spec] * n, out_specs=tuple([any_spec] * n),
        scratch_shapes=_exchange_sems(n),
        compiler_params=pltpu.CompilerParams(has_side_effects=True),
    )(*arrays)


def _gather_two_level(arrays, name):
    n = len(arrays)
    out_shape = [jax.ShapeDtypeStruct((N_DEV,) + tuple(a.shape), a.dtype) for a in arrays]

    def body(*refs):
        ins, outs = refs[:n], refs[n:2 * n]
        send_sems, recv_sems, loc_sems = refs[2 * n:]
        x, y, c = lax.axis_index("x"), lax.axis_index("y"), lax.axis_index("c")
        me, sibling = (x, y, c), (x, y, 1 - c)
        chips = [(1 - x, y), (x, 1 - y), (1 - x, 1 - y)]

        def slot(px, py, pc):
            return 4 * px + 2 * py + pc

        def copy(a, k, block, to, src=None):
            dst = outs[a].at[slot(*block)]
            return pltpu.make_async_remote_copy(
                src_ref=dst if src is None else src, dst_ref=dst,
                send_sem=send_sems.at[a, k], recv_sem=recv_sems.at[a, k],
                device_id=to, device_id_type=pl.DeviceIdType.MESH)

        local = [pltpu.make_async_copy(ins[a], outs[a].at[slot(*me)], loc_sems.at[a]) for a in range(n)]
        for cp in local:
            cp.start()
        first = []
        for a in range(n):
            first.append(copy(a, 0, me, sibling, src=ins[a]))
            first += [copy(a, 1 + j, me, (*chip, c), src=ins[a]) for j, chip in enumerate(chips)]
        for cp in first:
            cp.start()
        passed = []
        for j, chip in enumerate(chips):
            for a in range(n):
                copy(a, 1 + j, (*chip, c), me).wait_recv()
                cp = copy(a, 4 + j, (*chip, c), sibling)
                cp.start()
                passed.append(cp)
        for a in range(n):
            copy(a, 0, sibling, me).wait_recv()
            for j, chip in enumerate(chips):
                copy(a, 4 + j, (*chip, 1 - c), me).wait_recv()
        for cp in first + passed:
            cp.wait_send()
        for cp in local:
            cp.wait()

    any_spec = pl.BlockSpec(memory_space=pl.ANY)
    return pl.pallas_call(
        body, name=name, out_shape=tuple(out_shape),
        in_specs=[any_spec] * n, out_specs=tuple([any_spec] * n),
        scratch_shapes=[pltpu.SemaphoreType.DMA((n, N_DEV - 1)), pltpu.SemaphoreType.DMA((n, N_DEV - 1)),
                        pltpu.SemaphoreType.DMA((n,))],
        compiler_params=pltpu.CompilerParams(has_side_effects=True),
    )(*arrays)


N_CHIPS = 4


def _scatter_to_sibling(arrays, name):
    n = len(arrays)
    out_shape = [jax.ShapeDtypeStruct((N_CHIPS,) + tuple(a.shape[2:]), a.dtype) for a in arrays]

    def body(*refs):
        ins, outs = refs[:n], refs[n:2 * n]
        send_sems, recv_sems = refs[2 * n:]
        x, y, c = lax.axis_index("x"), lax.axis_index("y"), lax.axis_index("c")
        copies = []
        for a in range(n):
            for q in range(N_CHIPS):
                cp = pltpu.make_async_remote_copy(
                    src_ref=ins[a].at[q, 1 - c], dst_ref=outs[a].at[q],
                    send_sem=send_sems.at[a, q], recv_sem=recv_sems.at[a, q],
                    device_id=(x, y, 1 - c), device_id_type=pl.DeviceIdType.MESH)
                cp.start()
                copies.append(cp)
        for cp in copies:
            cp.wait()

    any_spec = pl.BlockSpec(memory_space=pl.ANY)
    return pl.pallas_call(
        body, name=name, out_shape=tuple(out_shape),
        in_specs=[any_spec] * n, out_specs=tuple([any_spec] * n),
        scratch_shapes=[pltpu.SemaphoreType.DMA((n, N_CHIPS)), pltpu.SemaphoreType.DMA((n, N_CHIPS))],
        compiler_params=pltpu.CompilerParams(has_side_effects=True),
    )(*arrays)


def _pair_add(mine, got, c_idx, name):
    _, _, R, C = mine.shape
    tr = 256 if (R % 256 == 0 and R > 256) else R

    def body(c_ref, m_ref, g_ref, o_ref):
        o_ref[...] = m_ref[...] + g_ref[...]

    grid_spec = pltpu.PrefetchScalarGridSpec(
        num_scalar_prefetch=1, grid=(N_CHIPS, R // tr),
        in_specs=[pl.BlockSpec((None, None, tr, C), lambda q, i, c_ref: (q, c_ref[0], i, 0)),
                  pl.BlockSpec((None, tr, C), lambda q, i, c_ref: (q, i, 0))],
        out_specs=pl.BlockSpec((None, tr, C), lambda q, i, c_ref: (q, i, 0)))
    return pl.pallas_call(
        body, name=name, grid_spec=grid_spec, out_shape=jax.ShapeDtypeStruct((N_CHIPS, R, C), mine.dtype),
        compiler_params=_params(dimension_semantics=("parallel", "parallel")),
    )(c_idx, mine, got)


def _chip_exchange_copies(ins, outs, send_sems, recv_sems, loc_sems):
    n = len(ins)
    x, y, c = lax.axis_index("x"), lax.axis_index("y"), lax.axis_index("c")
    myq = 2 * x + y
    copies = [pltpu.make_async_copy(ins[a].at[myq], outs[a].at[myq], loc_sems.at[a]) for a in range(n)]
    for k in range(1, N_CHIPS):
        qx = 1 - x if k & 2 else x
        qy = 1 - y if k & 1 else y
        for a in range(n):
            copies.append(pltpu.make_async_remote_copy(
                src_ref=ins[a].at[2 * qx + qy], dst_ref=outs[a].at[myq],
                send_sem=send_sems.at[a, k - 1], recv_sem=recv_sems.at[a, k - 1],
                device_id=(qx, qy, c), device_id_type=pl.DeviceIdType.MESH))
    return copies


def _chip_exchange_sems(n):
    return [pltpu.SemaphoreType.DMA((n, N_CHIPS - 1)), pltpu.SemaphoreType.DMA((n, N_CHIPS - 1)),
            pltpu.SemaphoreType.DMA((n,))]


def _dot_nt(a, b):
    return lax.dot_general(a, b, (((1,), (1,)), ((), ())), preferred_element_type=F32)


def _rhs_spec(b, tn, nt):
    if nt:
        return pl.BlockSpec((tn, b.shape[1]), lambda i, j: (j, 0))
    return pl.BlockSpec((b.shape[0], tn), lambda i, j: (0, j))


def _mm(a, b, out_dtype, name, tm=1024, tn=512, nt=False):
    M, K = a.shape
    N = b.shape[0] if nt else b.shape[1]
    tm, tn = _tile(M, tm), _tile(N, tn)
    dot = _dot_nt if nt else _dot

    def body(a_ref, b_ref, o_ref):
        o_ref[...] = dot(a_ref[...], b_ref[...]).astype(out_dtype)

    return pl.pallas_call(
        body, name=name, out_shape=jax.ShapeDtypeStruct((M, N), out_dtype),
        grid=(M // tm, N // tn),
        in_specs=[pl.BlockSpec((tm, K), lambda i, j: (i, 0)), _rhs_spec(b, tn, nt)],
        out_specs=pl.BlockSpec((tm, tn), lambda i, j: (i, j)),
        compiler_params=_params(dimension_semantics=("parallel", "parallel")),
    )(a, b)


def _mm_acc(a, b, name, tm=1408, tn=1408, tk=1024):
    M, S = a.shape
    _, N = b.shape
    tm, tn, tk = _tile(M, tm), _tile(N, tn), _tile(S, tk)

    def body(a_ref, b_ref, o_ref):
        @pl.when(pl.program_id(2) == 0)
        def _():
            o_ref[...] = jnp.zeros_like(o_ref)

        o_ref[...] += _dot(a_ref[...], b_ref[...])

    return pl.pallas_call(
        body, name=name, out_shape=jax.ShapeDtypeStruct((M, N), F32),
        grid=(M // tm, N // tn, S // tk),
        in_specs=[pl.BlockSpec((tm, tk), lambda i, j, k: (i, k)), pl.BlockSpec((tk, tn), lambda i, j, k: (k, j))],
        out_specs=pl.BlockSpec((tm, tn), lambda i, j, k: (i, j)),
        compiler_params=_params(dimension_semantics=("parallel", "parallel", "arbitrary")),
    )(a, b)


def _mm_acc_parts(a, parts, name, tm=1024, tk=1024):
    M, S = a.shape
    P, _, K = parts.shape
    tm, tk = _tile(M, tm), _tile(S, tk)

    def body(a_ref, b_ref, o_ref):
        @pl.when(pl.program_id(2) == 0)
        def _():
            o_ref[...] = jnp.zeros_like(o_ref)

        o_ref[...] += _dot(a_ref[...], b_ref[...])

    return pl.pallas_call(
        body, name=name, out_shape=jax.ShapeDtypeStruct((P, M, K), F32), grid=(P, M // tm, S // tk),
        in_specs=[pl.BlockSpec((tm, tk), lambda k, i, s: (i, s)), pl.BlockSpec((None, tk, K), lambda k, i, s: (k, s, 0))],
        out_specs=pl.BlockSpec((None, tm, K), lambda k, i, s: (k, i, 0)),
        compiler_params=_params(dimension_semantics=("parallel", "parallel", "arbitrary")),
    )(a, parts)


def _silu(z):
    return z * (1.0 / (1.0 + jnp.exp(-z)))


def _ada_fwd(c_all, w_shard, b_shard):
    n = w_shard.shape[1]

    def body(c_ref, w_ref, b_ref, o_ref):
        o_ref[...] = _dot(_silu(c_ref[...]), w_ref[...], precision=HIGHEST) + b_ref[...]

    return pl.pallas_call(body, name="ada_fwd", out_shape=jax.ShapeDtypeStruct((N_DEV, n), F32),
                          compiler_params=_params())(c_all, w_shard, b_shard)


def _ada_bwd(c_all_t, dmod_cols):
    D = c_all_t.shape[0]
    n = dmod_cols.shape[1]

    def body(ct_ref, dm_ref, o_ref):
        sc = _silu(ct_ref[...])
        dm = dm_ref[...]
        acc = sc[:, 0:1] * dm[0:1, :]
        for b in range(1, N_DEV):
            acc = acc + sc[:, b:b + 1] * dm[b:b + 1, :]
        o_ref[...] = acc

    return pl.pallas_call(body, name="ada_bwd", out_shape=jax.ShapeDtypeStruct((D, n), F32),
                          compiler_params=_params())(c_all_t, dmod_cols)


def _row_specs(tm, widths):
    return [pl.BlockSpec((tm, w), lambda i: (i, 0)) for w in widths]


def _vec_spec(w):
    return pl.BlockSpec((1, w), lambda i: (0, 0))


def _col_spec(tm, w):
    return pl.BlockSpec((w, tm), lambda i: (0, i))


def _prenorm(x, g, scale, shift, name):
    S, D = x.shape
    tm = _tile(S, 512)

    def body(x_ref, g_ref, sc_ref, sh_ref, h_ref, ht_ref):
        xv = x_ref[...]
        r = lax.rsqrt(jnp.mean(xv * xv, axis=-1, keepdims=True) + EPS)
        h = (xv * r) * g_ref[...] * (1.0 + sc_ref[...]) + sh_ref[...]
        h_ref[...] = h.astype(BF16)
        ht_ref[...] = h.T.astype(BF16)

    return pl.pallas_call(
        body, name=name, grid=(S // tm,),
        out_shape=(jax.ShapeDtypeStruct((S, D), BF16), jax.ShapeDtypeStruct((D, S), BF16)),
        in_specs=_row_specs(tm, [D]) + [_vec_spec(D)] * 3,
        out_specs=(_row_specs(tm, [D])[0], _col_spec(tm, D)),
        compiler_params=_params(dimension_semantics=("parallel",)),
    )(x, g, scale, shift)


def _group_ones():
    r = (lax.broadcasted_iota(jnp.int32, (2 * LANES, LANES), 0) % LANES) // HEAD_DIM
    c = lax.broadcasted_iota(jnp.int32, (2 * LANES, LANES), 1) // HEAD_DIM
    return (r == c).astype(BF16)


def _group_sum(t, ones):
    hi = t.astype(BF16)
    lo = (t - hi.astype(F32)).astype(BF16)
    return _dot(jnp.concatenate([hi, lo], axis=1), ones)


def _headnorm_fwd(o_f, o_s, g_f, g_s):
    dh, S = o_f.shape
    tm = _tile(S, 512)

    def body(of_ref, os_ref, gf_ref, gs_ref, mix_ref, mixt_ref):
        ones = _group_ones()
        for part, (o_ref, g_ref) in enumerate(((of_ref, gf_ref), (os_ref, gs_ref))):
            for t in range(dh // LANES):
                cols = slice(t * LANES, (t + 1) * LANES)
                out = slice(part * dh + t * LANES, part * dh + (t + 1) * LANES)
                o = o_ref[cols, :].T
                ms = _group_sum(o * o, ones) * (1.0 / HEAD_DIM)
                mixn = o * lax.rsqrt(ms + EPS) * g_ref[:, cols]
                mix_ref[:, out] = mixn.astype(BF16)
                mixt_ref[out, :] = mixn.T.astype(BF16)

    return pl.pallas_call(
        body, name="headnorm_fwd", grid=(S // tm,),
        out_shape=(jax.ShapeDtypeStruct((S, 2 * dh), BF16), jax.ShapeDtypeStruct((2 * dh, S), BF16)),
        in_specs=[_col_spec(tm, dh)] * 2 + [_vec_spec(dh)] * 2,
        out_specs=(_row_specs(tm, [2 * dh])[0], _col_spec(tm, 2 * dh)),
        compiler_params=_params(dimension_semantics=("parallel",)),
    )(o_f, o_s, g_f, g_s)


def _resid_prenorm(x, a_out, gate, g, scale, shift):
    S, D = x.shape
    tm = _tile(S, 512)

    def body(x_ref, a_ref, gt_ref, g_ref, sc_ref, sh_ref, x1_ref, h_ref, ht_ref):
        x1 = x_ref[...] + gt_ref[...] * a_ref[...]
        x1_ref[...] = x1
        r = lax.rsqrt(jnp.mean(x1 * x1, axis=-1, keepdims=True) + EPS)
        h = (x1 * r) * g_ref[...] * (1.0 + sc_ref[...]) + sh_ref[...]
        h_ref[...] = h.astype(BF16)
        ht_ref[...] = h.T.astype(BF16)

    return pl.pallas_call(
        body, name="resid_prenorm", grid=(S // tm,),
        out_shape=(jax.ShapeDtypeStruct((S, D), F32), jax.ShapeDtypeStruct((S, D), BF16),
                   jax.ShapeDtypeStruct((D, S), BF16)),
        in_specs=_row_specs(tm, [D, D]) + [_vec_spec(D)] * 4,
        out_specs=tuple(_row_specs(tm, [D, D]) + [_col_spec(tm, D)]),
        compiler_params=_params(dimension_semantics=("parallel",)),
    )(x, a_out, gate, g, scale, shift)


def _shift_down(main, halo, k):
    ext = jnp.concatenate([halo, main], axis=0)
    return pltpu.roll(ext, k, 0)[halo.shape[0]:]


def _shift_up(main, halo, k):
    ext = jnp.concatenate([main, halo], axis=0)
    n = ext.shape[0]
    return pltpu.roll(ext, n - k, 0)[:main.shape[0]]


def _conv(up, up_halo, w_ref, b_ref):
    return (w_ref[2:3, :] * up + w_ref[1:2, :] * _shift_down(up, up_halo, 1)
            + w_ref[0:1, :] * _shift_down(up, up_halo, 2) + b_ref[...])


def _prev_halo_map(tm):
    step = tm // HALO
    return lambda j, i: (jnp.maximum(i * step - 1, 0), j)


MLP_TM = 512
MLP_CT = 1408
CARRY = 8


def _mlp_up(h, wg, wv, cwg, cwv, cbg, cbv):
    S, D = h.shape
    F = wg.shape[1]
    tm, ct = _tile(S, MLP_TM), _tile(F, MLP_CT)
    nct = F // ct

    def body(h_ref, wg_ref, wv_ref, cwg_ref, cwv_ref, cbg_ref, cbv_ref,
             upg_ref, upv_ref, act_ref, actt_ref, hg_scr, hv_scr):
        i, j = pl.program_id(0), pl.program_id(1)
        hv = h_ref[...]
        us = []
        for w_ref, cw_ref, cb_ref, up_ref, scr in ((wg_ref, cwg_ref, cbg_ref, upg_ref, hg_scr),
                                                   (wv_ref, cwv_ref, cbv_ref, upv_ref, hv_scr)):
            up = _dot(hv, w_ref[...]).astype(BF16)
            up_ref[...] = up
            upf = up.astype(F32)
            halo = jnp.where(i == 0, 0.0, scr[j])
            us.append(_conv(upf, halo, cw_ref, cb_ref))
            scr[j] = upf[tm - CARRY:, :]
        act = _silu(us[0]) * us[1]
        act_ref[...] = act.astype(BF16)
        actt_ref[...] = act.T.astype(BF16)

    blk = pl.BlockSpec((tm, ct), lambda i, j: (i, j))
    wspec = pl.BlockSpec((D, ct), lambda i, j: (0, j))
    cwspec = pl.BlockSpec((CONV_W, ct), lambda i, j: (0, j))
    cbspec = pl.BlockSpec((1, ct), lambda i, j: (0, j))
    sds = jax.ShapeDtypeStruct((S, F), BF16)
    return pl.pallas_call(
        body, name="mlp_up", grid=(S // tm, nct),
        out_shape=(sds, sds, sds, jax.ShapeDtypeStruct((F, S), BF16)),
        in_specs=[pl.BlockSpec((tm, D), lambda i, j: (i, 0)), wspec, wspec, cwspec, cwspec, cbspec, cbspec],
        out_specs=(blk, blk, blk, pl.BlockSpec((ct, tm), lambda i, j: (j, i))),
        scratch_shapes=[pltpu.VMEM((nct, CARRY, ct), F32), pltpu.VMEM((nct, CARRY, ct), F32)],
        compiler_params=_params(dimension_semantics=("arbitrary", "arbitrary")),
    )(h, wg, wv, cwg, cwv, cbg, cbv)


def _conv_act_bwd(gm, w_down, up_g, up_v, cwg, cwv, cbg, cbv):
    S, F = up_g.shape
    D = gm.shape[1]
    tm, ct = _tile(S, 256), _tile(F, CONV_COLS)
    nct = F // ct

    def body(gm_ref, wd_ref, ug_ref, uv_ref, hg_ref, hv_ref, wg_ref, wv_ref, bg_ref, bv_ref,
             dug_ref, duv_ref, pg_ref, pv_ref):
        first = pl.program_id(1) == 0

        @pl.when(first)
        def _():
            pg_ref[...] = jnp.zeros_like(pg_ref)
            pv_ref[...] = jnp.zeros_like(pv_ref)

        da = _dot_nt(gm_ref[...], wd_ref[...])
        taps = []
        for u_ref, h_ref in ((ug_ref, hg_ref), (uv_ref, hv_ref)):
            h = jnp.where(first, 0.0, h_ref[...].astype(F32))
            uu = u_ref[...].astype(F32)
            taps.append((_shift_down(uu, h, 2), _shift_down(uu, h, 1), uu))
        u_g = wg_ref[0:1, :] * taps[0][0] + wg_ref[1:2, :] * taps[0][1] + wg_ref[2:3, :] * taps[0][2] + bg_ref[...]
        u_v = wv_ref[0:1, :] * taps[1][0] + wv_ref[1:2, :] * taps[1][1] + wv_ref[2:3, :] * taps[1][2] + bv_ref[...]
        sg = 1.0 / (1.0 + jnp.exp(-u_g))
        du_g = da * u_v * (sg * (1.0 + u_g * (1.0 - sg)))
        du_v = da * (u_g * sg)
        dug_ref[...] = du_g.astype(BF16)
        duv_ref[...] = du_v.astype(BF16)
        for du, tp, p_ref in ((du_g, taps[0], pg_ref), (du_v, taps[1], pv_ref)):
            for k in range(CONV_W):
                p_ref[k:k + 1, :] += jnp.sum(du * tp[k], axis=0, keepdims=True)
            p_ref[CONV_W:CONV_W + 1, :] += jnp.sum(du, axis=0, keepdims=True)

    main = pl.BlockSpec((tm, ct), lambda j, i: (i, j))
    halo = pl.BlockSpec((HALO, ct), _prev_halo_map(tm))
    wspec = pl.BlockSpec((CONV_W, ct), lambda j, i: (0, j))
    bspec = pl.BlockSpec((1, ct), lambda j, i: (0, j))
    pspec = pl.BlockSpec((8, ct), lambda j, i: (0, j))
    return pl.pallas_call(
        body, name="conv_act_bwd", grid=(nct, S // tm),
        out_shape=(jax.ShapeDtypeStruct((S, F), BF16), jax.ShapeDtypeStruct((S, F), BF16),
                   jax.ShapeDtypeStruct((8, F), F32), jax.ShapeDtypeStruct((8, F), F32)),
        in_specs=[pl.BlockSpec((tm, D), lambda j, i: (i, 0)), pl.BlockSpec((ct, D), lambda j, i: (j, 0)),
                  main, main, halo, halo, wspec, wspec, bspec, bspec],
        out_specs=(main, main, pspec, pspec),
        compiler_params=_params(dimension_semantics=("parallel", "arbitrary")),
    )(gm, w_down, up_g, up_v, up_g, up_v, cwg, cwv, cbg, cbv)


def _conv_bwd_taps(d, halo, w_ref):
    return w_ref[2:3, :] * d + w_ref[1:2, :] * _shift_up(d, halo, 1) + w_ref[0:1, :] * _shift_up(d, halo, 2)


def _scan_mats(R, nc, reverse):
    i = lax.broadcasted_iota(jnp.int32, (LANES, LANES), 0)
    j = lax.broadcasted_iota(jnp.int32, (LANES, LANES), 1)
    inner = ((i >= j) if reverse else (i <= j)).astype(F32)
    r = lax.broadcasted_iota(jnp.int32, (R, R), 0)
    c = lax.broadcasted_iota(jnp.int32, (R, R), 1)
    same = (r // nc) == (c // nc)
    outer = (same & ((c > r) if reverse else (c < r))).astype(F32)
    return inner, outer


def _chunk_scan(v, inner, outer, reverse):
    w = _dot(v, inner, precision=HIGHEST)
    col = 0 if reverse else LANES - 1
    carry = _dot(outer, w, precision=HIGHEST)[:, col:col + 1]
    return w + carry


def _fgate_fwd(z_rows, nc):
    R = z_rows.shape[0]

    def body(z_ref, f_ref):
        z = z_ref[...]
        logf = jnp.minimum(z, 0.0) - jnp.log(1.0 + jnp.exp(-jnp.abs(z)))
        inner, outer = _scan_mats(R, nc, False)
        f_ref[...] = _chunk_scan(logf, inner, outer, False)

    return pl.pallas_call(body, name="fgate_fwd", out_shape=jax.ShapeDtypeStruct((R, LANES), F32),
                          compiler_params=_params())(z_rows)


def _fgate_bwd(dfk_neg_rows, dfq_rows, z_rows, nc):
    R = z_rows.shape[0]
    nh = R // nc

    def body(dfk_ref, dfq_ref, z_ref, dz_ref, db_ref):
        inner, outer = _scan_mats(R, nc, True)
        dlogf = _chunk_scan(dfq_ref[...] - dfk_ref[...], inner, outer, True)
        dz = dlogf * (1.0 / (1.0 + jnp.exp(z_ref[...])))
        dz_ref[...] = dz
        hr = lax.broadcasted_iota(jnp.int32, (nh, R), 0)
        hc = lax.broadcasted_iota(jnp.int32, (nh, R), 1) // nc
        per_head = _dot((hr == hc).astype(F32), dz, precision=HIGHEST)
        db_ref[...] = jnp.sum(per_head, axis=1, keepdims=True)

    return pl.pallas_call(
        body, name="fgate_bwd",
        out_shape=(jax.ShapeDtypeStruct((R, LANES), F32), jax.ShapeDtypeStruct((nh, 1), F32)),
        compiler_params=_params())(dfk_neg_rows, dfq_rows, z_rows)


_NEG = -1e30
SKIP_BELOW = -106.0
_SCALE = HEAD_DIM ** -0.5
N_SCAN = ATT_BK // SCAN_BK
F_PARTS = 3
Q_F_LANE = HEAD_DIM
Q_ONE_LANE = HEAD_DIM + F_PARTS


def _kv_slice(j):
    return pl.ds(pl.multiple_of(j * ATT_BK, ATT_BK), ATT_BK)


def _mask_t(strict):
    s = lax.broadcasted_iota(jnp.int32, (ATT_BK, ATT_BQ), 0)
    t = lax.broadcasted_iota(jnp.int32, (ATT_BK, ATT_BQ), 1)
    return (s < t) if strict else (s <= t)


def _walk_down(i, step, alive, carry):
    carry = step(i, carry, True)

    def cond(st):
        n, go, _ = st
        return jnp.logical_and(n < i, go)

    def body(st):
        n, _, cr = st
        j = i - 1 - n
        cr = step(j, cr, False)
        return n + 1, alive(jnp.maximum(j - 1, 0), cr), cr

    return lax.while_loop(cond, body, (jnp.int32(0), alive(jnp.maximum(i - 1, 0), carry), carry))[2]


def _t_block(rows):
    return pl.BlockSpec((None, rows, ATT_BQ), lambda h, i, *_: (h, 0, i))


def _t_full(rows, S):
    return pl.BlockSpec((None, rows, S), lambda h, i, *_: (h, 0, 0))


def _n_block():
    return pl.BlockSpec((None, ATT_BQ, LANES), lambda h, i, *_: (h, i, 0))


def _n_full(S):
    return pl.BlockSpec((None, S, LANES), lambda h, i, *_: (h, 0, 0))


def _heads(t):
    S = t.shape[0]
    return jnp.transpose(t.reshape(S, -1, HEAD_DIM), (1, 0, 2))


def _skip_bounds(k_cols, f_rows):
    H, S = f_rows.shape
    f_end = f_rows.reshape(H, S // ATT_BK, ATT_BK)[:, :, -1]
    k_sq = jnp.sum(jnp.square(_heads(k_cols).astype(F32)), axis=-1).reshape(H, S // ATT_BK, ATT_BK)
    k_max = lax.cummax(jnp.sqrt(jnp.max(k_sq, axis=-1)), axis=1)
    return f_end, k_max


def _bf16_parts(f):
    hi = f.astype(BF16).astype(F32)
    mid = (f - hi).astype(BF16).astype(F32)
    return hi, mid, (f - hi - mid).astype(BF16).astype(F32)


def _att_prep(qkv, f_pairs):
    S = qkv.shape[0]
    n_pairs = qkv.shape[1] // (6 * LANES)
    H = 2 * n_pairs
    tm = _tile(S, 512)

    def body(qf_ref, kf_ref, vf_ref, qs_ref, ks_ref, vs_ref, f_ref,
             fqn, fqt, fkn, fkt, fvn, fvt, sqn, sqt, skn, skt, svn, svt):
        lane = lax.broadcasted_iota(jnp.int32, (1, LANES), 1)
        f = f_ref[...]

        def head(ref, e):
            t = ref[...].astype(F32)
            if e == 1:
                t = pltpu.roll(t, HEAD_DIM, 1)
            return jnp.where(lane < HEAD_DIM, t, 0.0)

        def at(first):
            return jnp.logical_and(lane >= first, lane < first + F_PARTS)

        for e in range(2):
            parts = _bf16_parts(f[:, e:e + 1])
            f_lanes = sum(jnp.where(lane == Q_F_LANE + k, parts[k], 0.0) for k in range(F_PARTS))
            nf_lanes = sum(jnp.where(lane == Q_ONE_LANE + k, parts[k], 0.0) for k in range(F_PARTS))
            vals = (
                (fqn, fqt, LANES, head(qf_ref, e) * _SCALE + f_lanes + jnp.where(at(Q_ONE_LANE), 1.0, 0.0)),
                (fkn, fkt, LANES, head(kf_ref, e) + jnp.where(at(Q_F_LANE), 1.0, 0.0) - nf_lanes),
                (fvn, fvt, HEAD_DIM, head(vf_ref, e)),
                (sqn, sqt, LANES, head(qs_ref, e) * _SCALE),
                (skn, skt, LANES, head(ks_ref, e)),
                (svn, svt, HEAD_DIM, head(vs_ref, e)),
            )
            for n_ref, t_ref, rows, val in vals:
                n_ref[e] = val.astype(BF16)
                t_ref[e] = val.T[:rows].astype(BF16)

    col = lambda base: pl.BlockSpec((tm, LANES), lambda i, p: (i, base + p))
    n_spec = pl.BlockSpec((2, tm, LANES), lambda i, p: (p, i, 0))
    t_spec = lambda rows: pl.BlockSpec((2, rows, tm), lambda i, p: (p, 0, i))
    n_sds = jax.ShapeDtypeStruct((H, S, LANES), BF16)
    t_sds = lambda rows: jax.ShapeDtypeStruct((H, rows, S), BF16)
    group = ([n_sds, t_sds(LANES), n_sds, t_sds(LANES), n_sds, t_sds(HEAD_DIM)],
             [n_spec, t_spec(LANES), n_spec, t_spec(LANES), n_spec, t_spec(HEAD_DIM)])
    res = pl.pallas_call(
        body, name="att_prep", grid=(S // tm, n_pairs),
        out_shape=tuple(group[0] * 2),
        in_specs=[col(k * n_pairs) for k in range(6)] + [pl.BlockSpec((None, tm, 2), lambda i, p: (p, i, 0))],
        out_specs=tuple(group[1] * 2),
        compiler_params=_params(dimension_semantics=("parallel", "parallel")),
    )(qkv, qkv, qkv, qkv, qkv, qkv, f_pairs)
    names = ("q_n", "q_t", "k_n", "k_t", "v_n", "v_t")
    return dict(zip(names, res[:6])), dict(zip(names, res[6:]))


def _fox_reach(qt, fend_ref, kmax_ref, h):
    qf = qt.astype(F32)
    q_norm = jnp.sqrt(jnp.sum(jnp.square(qf[:HEAD_DIM]), axis=0, keepdims=True))
    f_t = jnp.sum(qf[Q_F_LANE:Q_F_LANE + F_PARTS], axis=0, keepdims=True)
    return lambda j: q_norm * kmax_ref[h, j] + f_t - fend_ref[h, j]


def _fox_fwd(q_t, k_n, v_t, f_end, k_max, shards):
    H, _, S = q_t.shape
    n, nq = len(shards), S // ATT_BQ

    def body(fend_ref, kmax_ref, qt_ref, k_ref, vt_ref, *rest):
        ins, (ot_ref, lse_ref), outs, sems = rest[:n], rest[n:n + 2], rest[n + 2:2 * n + 2], rest[2 * n + 2:]
        h, i = pl.program_id(0), pl.program_id(1)

        @pl.when(jnp.logical_and(h == 0, i == 0))
        def _():
            for cp in _exchange_copies(ins, outs, *sems, False):
                cp.start()

        qt = qt_ref[...]
        reach = _fox_reach(qt, fend_ref, kmax_ref, h)

        def step(j, carry, masked):
            m, l, acc = carry
            ks = _kv_slice(j)
            s = _dot(k_ref[ks, :], qt)
            if masked:
                s = jnp.where(_mask_t(False), s, _NEG)
            mn = jnp.maximum(m, jnp.max(s, axis=0, keepdims=True))
            alpha = jnp.exp(m - mn)
            p = jnp.exp(s - mn)
            l = alpha * l + jnp.sum(p, axis=0, keepdims=True)
            acc = acc * alpha + _dot(vt_ref[:, ks], p.astype(BF16))
            return mn, l, acc

        def alive(j, carry):
            return jnp.max(reach(j) - carry[0]) > SKIP_BELOW

        row = jnp.zeros((1, ATT_BQ), F32)
        m, l, acc = _walk_down(i, step, alive, (row + _NEG, row, jnp.zeros((HEAD_DIM, ATT_BQ), F32)))
        ot_ref[...] = acc / l
        lse_ref[...] = m + jnp.log(l)

        @pl.when(jnp.logical_and(h == H - 1, i == nq - 1))
        def _():
            for cp in _exchange_copies(ins, outs, *sems, False):
                cp.wait()

    any_spec = pl.BlockSpec(memory_space=pl.ANY)
    grid_spec = pltpu.PrefetchScalarGridSpec(
        num_scalar_prefetch=2, grid=(H, nq),
        in_specs=[_t_block(LANES), _n_full(S), _t_full(HEAD_DIM, S)] + [any_spec] * n,
        out_specs=tuple([_t_block(HEAD_DIM), _t_block(1)] + [any_spec] * n),
        scratch_shapes=_exchange_sems(n))
    res = pl.pallas_call(
        body, name="fox_fwd", grid_spec=grid_spec,
        out_shape=tuple([jax.ShapeDtypeStruct((H, HEAD_DIM, S), F32), jax.ShapeDtypeStruct((H, 1, S), F32)]
                        + _exchange_out_shapes(shards, False)),
        compiler_params=_params(dimension_semantics=("arbitrary", "arbitrary"), has_side_effects=True),
    )(f_end, k_max, q_t, k_n, v_t, *shards)
    return res[0], res[1], res[2:]


def _fox_bwd(q_t, q_n, k_n, k_t, v_n, do_t, do_n, o_t, lse, f_end, k_max):
    H, _, S = q_t.shape

    def body(fend_ref, kmax_ref, qt_ref, qn_ref, k_ref, kt_ref, v_ref, dot_ref, don_ref, ot_ref, lse_ref,
             dqt_ref, dk_ref, dv_ref):
        h, i = pl.program_id(0), pl.program_id(1)

        @pl.when(i == 0)
        def _():
            dk_ref[...] = jnp.zeros_like(dk_ref)
            dv_ref[...] = jnp.zeros_like(dv_ref)

        qt, qn, dot, don = qt_ref[...], qn_ref[...], dot_ref[...], don_ref[...]
        lse = lse_ref[...]
        delta = jnp.sum(dot[:HEAD_DIM].astype(F32) * ot_ref[...], axis=0, keepdims=True)
        reach = _fox_reach(qt, fend_ref, kmax_ref, h)

        def alive(j, dq):
            return jnp.max(reach(j) - lse) > SKIP_BELOW

        def step(j, dq, masked):
            ks = _kv_slice(j)
            s = _dot(k_ref[ks, :], qt)
            if masked:
                s = jnp.where(_mask_t(False), s, _NEG)
            p = jnp.exp(s - lse)
            ds = (p * (_dot(v_ref[ks, :], dot) - delta)).astype(BF16)
            dk_ref[ks, :] += _dot(ds, qn)
            dv_ref[ks, :] += _dot(p.astype(BF16), don)
            return dq + _dot(kt_ref[:, ks], ds)

        dqt_ref[...] = _walk_down(i, step, alive, jnp.zeros((LANES, ATT_BQ), F32))

    grid_spec = pltpu.PrefetchScalarGridSpec(
        num_scalar_prefetch=2, grid=(H, S // ATT_BQ),
        in_specs=[_t_block(LANES), _n_block(), _n_full(S), _t_full(LANES, S), _n_full(S),
                  _t_block(LANES), _n_block(), _t_block(HEAD_DIM), _t_block(1)],
        out_specs=(_t_block(LANES), _n_full(S), _n_full(S)))
    return pl.pallas_call(
        body, name="fox_bwd", grid_spec=grid_spec,
        out_shape=(jax.ShapeDtypeStruct((H, LANES, S), F32), jax.ShapeDtypeStruct((H, S, LANES), F32),
                   jax.ShapeDtypeStruct((H, S, LANES), F32)),
        compiler_params=_params(dimension_semantics=("parallel", "arbitrary")),
    )(f_end, k_max, q_t, q_n, k_n, k_t, v_n, do_t, do_n, o_t, lse)


def _scan_lhs():
    r = lax.broadcasted_iota(jnp.int32, (SCAN_BK, 2 * SCAN_BK), 0)
    c = lax.broadcasted_iota(jnp.int32, (SCAN_BK, 2 * SCAN_BK), 1) % SCAN_BK
    return (c >= r).astype(BF16)


def _suffix_sum(t, lhs):
    hi = t.astype(BF16)
    lo = (t - hi.astype(F32)).astype(BF16)
    return _dot(lhs, jnp.concatenate([hi, lo], axis=0))


def _sb_scores(k, qt, mask):
    z = _dot(k, qt)
    e = jnp.exp(-jnp.abs(z))
    lb = -(jnp.maximum(z, 0.0) + jnp.log(1.0 + e))
    if mask is not None:
        lb = jnp.where(mask, lb, 0.0)
    return z, e, lb


def _scan_blocks():
    return [slice(u * SCAN_BK, (u + 1) * SCAN_BK) for u in reversed(range(N_SCAN))]


def _sb_fwd(q_t, k_n, v_t):
    H, _, S = q_t.shape

    def body(qt_ref, k_ref, vt_ref, ot_ref):
        i = pl.program_id(1)
        qt = qt_ref[...]
        lhs = _scan_lhs()

        def step(j, carry, masked):
            c, acc = carry
            ks = _kv_slice(j)
            mask = _mask_t(True) if masked else None
            z, _, lb = _sb_scores(k_ref[ks, :], qt, mask)
            parts = []
            for sl in _scan_blocks():
                rin = _suffix_sum(lb[sl], lhs)
                a = jnp.exp(z[sl] + rin + c)
                if masked:
                    a = jnp.where(mask[sl], a, 0.0)
                parts.append(a.astype(BF16))
                c = c + rin[0:1, :]
            a_all = jnp.concatenate(parts[::-1], axis=0)
            return c, acc + _dot(vt_ref[:, ks], a_all)

        carry = (jnp.zeros((1, ATT_BQ), F32), jnp.zeros((HEAD_DIM, ATT_BQ), F32))
        ot_ref[...] = _walk_down(i, step, lambda j, cr: jnp.max(cr[0]) > SKIP_BELOW, carry)[1]

    return pl.pallas_call(
        body, name="sb_fwd", grid=(H, S // ATT_BQ),
        out_shape=jax.ShapeDtypeStruct((H, HEAD_DIM, S), F32),
        in_specs=[_t_block(LANES), _n_full(S), _t_full(HEAD_DIM, S)],
        out_specs=_t_block(HEAD_DIM),
        compiler_params=_params(dimension_semantics=("parallel", "parallel")),
    )(q_t, k_n, v_t)


def _sb_bwd(q_t, q_n, k_n, k_t, v_n, do_t, do_n, o_t, bound):
    H, _, S = q_t.shape
    n, nq = len(bound), S // ATT_BQ

    def body(qt_ref, qn_ref, k_ref, kt_ref, v_ref, dot_ref, don_ref, ot_ref, *rest):
        ins, (dqt_ref, dk_ref, dv_ref) = rest[:n], rest[n:n + 3]
        outs, sems = rest[n + 3:2 * n + 3], rest[2 * n + 3:]
        h, i = pl.program_id(0), pl.program_id(1)

        @pl.when(jnp.logical_and(h == 0, i == 0))
        def _():
            for cp in _exchange_copies(ins, outs, *sems, True):
                cp.start()

        @pl.when(i == 0)
        def _():
            dk_ref[...] = jnp.zeros_like(dk_ref)
            dv_ref[...] = jnp.zeros_like(dv_ref)

        qt, qn, dot, don = qt_ref[...], qn_ref[...], dot_ref[...], don_ref[...]
        lhs = _scan_lhs()
        delta = jnp.sum(dot[:HEAD_DIM].astype(F32) * ot_ref[...], axis=0, keepdims=True)

        def step(j, carry, masked):
            c, g, dq = carry
            ks = _kv_slice(j)
            mask = _mask_t(True) if masked else None
            z, e, lb = _sb_scores(k_ref[ks, :], qt, mask)
            da = _dot(v_ref[ks, :], dot)
            a_parts, dz_parts = [], []
            for sl in _scan_blocks():
                rin = _suffix_sum(lb[sl], lhs)
                a = jnp.exp(z[sl] + rin + c)
                if masked:
                    a = jnp.where(mask[sl], a, 0.0)
                ab = a.astype(BF16)
                gg = ab.astype(F32) * da[sl]
                rgin = _suffix_sum(gg, lhs)
                rinv = 1.0 / (1.0 + e[sl])
                sig = jnp.where(z[sl] >= 0.0, rinv, e[sl] * rinv)
                dz = gg - sig * (delta - g - (rgin - gg))
                if masked:
                    dz = jnp.where(mask[sl], dz, 0.0)
                a_parts.append(ab)
                dz_parts.append(dz.astype(BF16))
                c = c + rin[0:1, :]
                g = g + rgin[0:1, :]
            ab_all = jnp.concatenate(a_parts[::-1], axis=0)
            dzb = jnp.concatenate(dz_parts[::-1], axis=0)
            dk_ref[ks, :] += _dot(dzb, qn)
            dv_ref[ks, :] += _dot(ab_all, don)
            return c, g, dq + _dot(kt_ref[:, ks], dzb)

        row = jnp.zeros((1, ATT_BQ), F32)
        carry = (row, row, jnp.zeros((LANES, ATT_BQ), F32))
        dqt_ref[...] = _walk_down(i, step, lambda j, cr: jnp.max(cr[0]) > SKIP_BELOW, carry)[2]

        @pl.when(jnp.logical_and(h == H - 1, i == nq - 1))
        def _():
            for cp in _exchange_copies(ins, outs, *sems, True):
                cp.wait()

    any_spec = pl.BlockSpec(memory_space=pl.ANY)
    res = pl.pallas_call(
        body, name="sb_bwd", grid=(H, nq),
        out_shape=tuple([jax.ShapeDtypeStruct((H, LANES, S), F32), jax.ShapeDtypeStruct((H, S, LANES), F32),
                         jax.ShapeDtypeStruct((H, S, LANES), F32)] + _exchange_out_shapes(bound, True)),
        in_specs=[_t_block(LANES), _n_block(), _n_full(S), _t_full(LANES, S), _n_full(S),
                  _t_block(LANES), _n_block(), _t_block(HEAD_DIM)] + [any_spec] * n,
        out_specs=tuple([_t_block(LANES), _n_full(S), _n_full(S)] + [any_spec] * n),
        scratch_shapes=_exchange_sems(n),
        compiler_params=_params(dimension_semantics=("arbitrary", "arbitrary"), has_side_effects=True),
    )(q_t, q_n, k_n, k_t, v_n, do_t, do_n, o_t, *bound)
    return res[0], res[1], res[2], res[3:]


def _dqkv_assemble(dqf_t, dkf, dvf, dqs_t, dks, dvs):
    H, _, S = dqf_t.shape
    n_pairs = H // 2
    tm = _tile(S, 512)

    def body(dqf_ref, dkf_ref, dvf_ref, dqs_ref, dks_ref, dvs_ref, out_ref, dfk_ref):
        lane = lax.broadcasted_iota(jnp.int32, (1, LANES), 1)
        slabs = ((dqf_ref, True), (dkf_ref, False), (dvf_ref, False),
                 (dqs_ref, True), (dks_ref, False), (dvs_ref, False))
        for k, (ref, transposed) in enumerate(slabs):
            if transposed:
                t0, t1 = ref[0].T * _SCALE, ref[1].T * _SCALE
            else:
                t0, t1 = ref[0], ref[1]
            out_ref[k] = jnp.where(lane < HEAD_DIM, t0, pltpu.roll(t1, HEAD_DIM, 1)).astype(BF16)
        for e in range(2):
            dfk_ref[e] = dkf_ref[e].T[Q_ONE_LANE:Q_ONE_LANE + 1, :]

    t_spec = pl.BlockSpec((2, LANES, tm), lambda i, p: (p, 0, i))
    n_spec = pl.BlockSpec((2, tm, LANES), lambda i, p: (p, i, 0))
    return pl.pallas_call(
        body, name="dqkv_assemble", grid=(S // tm, n_pairs),
        out_shape=(jax.ShapeDtypeStruct((6, S, n_pairs * LANES), BF16), jax.ShapeDtypeStruct((H, 1, S), F32)),
        in_specs=[t_spec, n_spec, n_spec, t_spec, n_spec, n_spec],
        out_specs=(pl.BlockSpec((6, tm, LANES), lambda i, p: (0, i, p)),
                   pl.BlockSpec((2, 1, tm), lambda i, p: (p, 0, i))),
        compiler_params=_params(dimension_semantics=("parallel", "parallel")),
    )(dqf_t, dkf, dvf, dqs_t, dks, dvs)


def _acc_spec(w):
    return pl.BlockSpec((1, w), lambda i: (0, 0))


def _loss_head(x1, act, w_down, gate_m, g_final, target):
    S, D = x1.shape
    F = act.shape[1]
    tm = _tile(S, 512)

    def body(x1_ref, act_ref, w_ref, gt_ref, gf_ref, tg_ref, dx2_ref, gm_ref, loss_ref, dgf_ref, dgt_ref):
        @pl.when(pl.program_id(0) == 0)
        def _():
            loss_ref[...] = jnp.zeros_like(loss_ref)
            dgf_ref[...] = jnp.zeros_like(dgf_ref)
            dgt_ref[...] = jnp.zeros_like(dgt_ref)

        mo = _dot(act_ref[...], w_ref[...])
        x2 = x1_ref[...] + gt_ref[...] * mo
        r = lax.rsqrt(jnp.mean(x2 * x2, axis=-1, keepdims=True) + EPS)
        xh = x2 * r
        diff = xh * gf_ref[...] - tg_ref[...]
        loss_ref[...] += (0.5 / D) * jnp.sum(diff * diff)
        dy = diff * (1.0 / D)
        dgf_ref[...] += jnp.sum(dy * xh, axis=0, keepdims=True)
        dxh = dy * gf_ref[...]
        dx2 = r * (dxh - xh * jnp.mean(dxh * xh, axis=-1, keepdims=True))
        dx2_ref[...] = dx2
        gm_ref[...] = (dx2 * gt_ref[...]).astype(BF16)
        dgt_ref[...] += jnp.sum(dx2 * mo, axis=0, keepdims=True)

    return pl.pallas_call(
        body, name="loss_head", grid=(S // tm,),
        out_shape=(jax.ShapeDtypeStruct((S, D), F32), jax.ShapeDtypeStruct((S, D), BF16),
                   jax.ShapeDtypeStruct((1, LANES), F32), jax.ShapeDtypeStruct((1, D), F32),
                   jax.ShapeDtypeStruct((1, D), F32)),
        in_specs=_row_specs(tm, [D, F]) + [pl.BlockSpec((F, D), lambda i: (0, 0))] + [_vec_spec(D)] * 2
        + _row_specs(tm, [D]),
        out_specs=tuple(_row_specs(tm, [D, D]) + [_acc_spec(LANES), _acc_spec(D), _acc_spec(D)]),
        compiler_params=_params(dimension_semantics=("arbitrary",)),
    )(x1, act, w_down, gate_m, g_final, target)


def _norm_bwd(lhs, rhs, xin, dres, g, scale, name, gate=None, branch=None, bound=(), conv=None, tm=256):
    S, D = xin.shape
    tm = _tile(S, tm)
    gated = gate is not None
    nl, nb, n_steps = len(lhs), len(bound), S // tm
    nc_ = nl if conv else 0
    n_out = (6 if gated else 4) + nc_

    def body(*refs):
        l_refs, r_refs, rest = refs[:nl], refs[nl:2 * nl], refs[2 * nl:]
        halo_refs, cw_refs, rest = rest[:nc_], rest[nc_:2 * nc_], rest[2 * nc_:]
        if gated:
            x_ref, dr_ref, g_ref, sc_ref, gt_ref, br_ref = rest[:6]
            rest = rest[6:]
        else:
            x_ref, dr_ref, g_ref, sc_ref = rest[:4]
            rest = rest[4:]
        ins, outs, ex_outs, sems = rest[:nb], rest[nb:nb + n_out], rest[nb + n_out:2 * nb + n_out], rest[2 * nb + n_out:]
        dx_ref, dsc_ref, dsh_ref, dg_ref = outs[:4]
        sums = (dsc_ref, dsh_ref, dg_ref) + ((outs[5],) if gated else ())
        dup_refs = outs[n_out - nc_:]
        i = pl.program_id(0)

        @pl.when(i == 0)
        def _():
            for s_ref in sums:
                s_ref[...] = jnp.zeros_like(s_ref)
            if nb:
                for cp in _chip_exchange_copies(ins, ex_outs, *sems):
                    cp.start()

        dhv = None
        for k, (l_ref, r_ref) in enumerate(zip(l_refs, r_refs)):
            if conv:
                halo = jnp.where(i == n_steps - 1, 0.0, halo_refs[k][...].astype(F32))
                dup = _conv_bwd_taps(l_ref[...].astype(F32), halo, cw_refs[k]).astype(BF16)
                dup_refs[k][...] = dup
                terms = [_dot_nt(dup, r_ref[...])]
            elif len(l_ref.shape) == 3:
                K = l_ref.shape[2]
                terms = [_dot_nt(l_ref[k], r_ref[:, k * K:(k + 1) * K]) for k in range(l_ref.shape[0])]
            else:
                terms = [_dot_nt(l_ref[...], r_ref[...])]
            for t in terms:
                dhv = t if dhv is None else dhv + t
        xv = x_ref[...]
        r = lax.rsqrt(jnp.mean(xv * xv, axis=-1, keepdims=True) + EPS)
        xh = xv * r
        dsc_ref[...] += jnp.sum(dhv * (xh * g_ref[...]), axis=0, keepdims=True)
        dsh_ref[...] += jnp.sum(dhv, axis=0, keepdims=True)
        dn = dhv * (1.0 + sc_ref[...])
        dg_ref[...] += jnp.sum(dn * xh, axis=0, keepdims=True)
        dxh = dn * g_ref[...]
        dx = dr_ref[...] + r * (dxh - xh * jnp.mean(dxh * xh, axis=-1, keepdims=True))
        dx_ref[...] = dx
        if gated:
            outs[4][...] = (dx * gt_ref[...]).astype(BF16)
            outs[5][...] += jnp.sum(dx * br_ref[...], axis=0, keepdims=True)

        if nb:
            @pl.when(i == n_steps - 1)
            def _():
                for cp in _chip_exchange_copies(ins, ex_outs, *sems):
                    cp.wait()

    def l_spec(a):
        if a.ndim == 3:
            return pl.BlockSpec((a.shape[0], tm, a.shape[2]), lambda i: (0, i, 0))
        return pl.BlockSpec((tm, a.shape[1]), lambda i: (i, 0))

    any_spec = pl.BlockSpec(memory_space=pl.ANY)
    vec = jax.ShapeDtypeStruct((1, D), F32)
    out_shape = [jax.ShapeDtypeStruct((S, D), F32), vec, vec, vec]
    out_specs = _row_specs(tm, [D]) + [_acc_spec(D)] * 3
    in_specs = [l_spec(a) for a in lhs] + [pl.BlockSpec(b.shape, lambda i: (0, 0)) for b in rhs]
    args = list(lhs) + list(rhs)
    if conv:
        step, last_halo = tm // HALO, S // HALO - 1
        in_specs += [pl.BlockSpec((HALO, a.shape[1]), lambda i: (jnp.minimum((i + 1) * step, last_halo), 0))
                     for a in lhs]
        in_specs += [pl.BlockSpec(w.shape, lambda i: (0, 0)) for w in conv]
        args += list(lhs) + list(conv)
    in_specs += _row_specs(tm, [D, D]) + [_vec_spec(D)] * 2
    args += [xin, dres, g, scale]
    if gated:
        out_shape += [jax.ShapeDtypeStruct((S, D), BF16), vec]
        out_specs += _row_specs(tm, [D]) + [_acc_spec(D)]
        in_specs += [_vec_spec(D)] + _row_specs(tm, [D])
        args += [gate, branch]
    if conv:
        out_shape += [jax.ShapeDtypeStruct(a.shape, BF16) for a in lhs]
        out_specs += [l_spec(a) for a in lhs]
    res = pl.pallas_call(
        body, name=name, grid=(n_steps,),
        out_shape=tuple(out_shape + [jax.ShapeDtypeStruct(b.shape, b.dtype) for b in bound]),
        in_specs=in_specs + [any_spec] * nb, out_specs=tuple(out_specs + [any_spec] * nb),
        scratch_shapes=_chip_exchange_sems(nb) if nb else [],
        compiler_params=_params(dimension_semantics=("arbitrary",), has_side_effects=bool(nb)),
    )(*args, *bound)
    return tuple(res[:n_out]) + (tuple(res[n_out:]),)


def _headnorm_bwd(dmix, o_f, o_s, g_f, g_s):
    dh, S = o_f.shape
    H = dh // HEAD_DIM
    tm = _tile(S, 256)

    def body(dm_ref, of_ref, os_ref, gf_ref, gs_ref, fn_ref, ft_ref, sn_ref, st_ref, dgf_ref, dgs_ref):
        @pl.when(pl.program_id(0) == 0)
        def _():
            dgf_ref[...] = jnp.zeros_like(dgf_ref)
            dgs_ref[...] = jnp.zeros_like(dgs_ref)

        ones = _group_ones()
        lane = lax.broadcasted_iota(jnp.int32, (1, LANES), 1)
        parts = ((of_ref, gf_ref, fn_ref, ft_ref, dgf_ref), (os_ref, gs_ref, sn_ref, st_ref, dgs_ref))
        for part, (o_ref, g_ref, n_ref, t_ref, dg_ref) in enumerate(parts):
            for t in range(dh // LANES):
                cols = slice(t * LANES, (t + 1) * LANES)
                o = o_ref[cols, :].T
                dm = dm_ref[:, part * dh + t * LANES: part * dh + (t + 1) * LANES]
                r = lax.rsqrt(_group_sum(o * o, ones) * (1.0 / HEAD_DIM) + EPS)
                oh = o * r
                dg_ref[:, cols] += jnp.sum(dm * oh, axis=0, keepdims=True)
                dn = dm * g_ref[:, cols]
                mean = _group_sum(dn * oh, ones) * (1.0 / HEAD_DIM)
                do = r * (dn - oh * mean)
                for e in range(2):
                    d = do if e == 0 else pltpu.roll(do, HEAD_DIM, 1)
                    d = jnp.where(lane < HEAD_DIM, d, 0.0)
                    n_ref[2 * t + e] = d.astype(BF16)
                    t_ref[2 * t + e] = d.T.astype(BF16)

    vec = jax.ShapeDtypeStruct((1, dh), F32)
    n_sds = jax.ShapeDtypeStruct((H, S, LANES), BF16)
    t_sds = jax.ShapeDtypeStruct((H, LANES, S), BF16)
    n_spec = pl.BlockSpec((H, tm, LANES), lambda i: (0, i, 0))
    t_spec = pl.BlockSpec((H, LANES, tm), lambda i: (0, 0, i))
    return pl.pallas_call(
        body, name="headnorm_bwd", grid=(S // tm,),
        out_shape=(n_sds, t_sds, n_sds, t_sds, vec, vec),
        in_specs=_row_specs(tm, [2 * dh]) + [_col_spec(tm, dh)] * 2 + [_vec_spec(dh)] * 2,
        out_specs=(n_spec, t_spec, n_spec, t_spec, _acc_spec(dh), _acc_spec(dh)),
        compiler_params=_params(dimension_semantics=("arbitrary",)),
    )(dmix, o_f, o_s, g_f, g_s)


def _adamw(w, gslots, m, v, name):
    R, C = w.shape
    n = gslots.shape[0]
    tr = 256 if (R % 256 == 0 and R > 256) else R
    bc1 = 1.0 - ADAM_B1 ** ADAM_STEP
    bc2 = 1.0 - ADAM_B2 ** ADAM_STEP

    def body(w_ref, gs_ref, m_ref, v_ref, g_ref, d_ref, nm_ref, nv_ref):
        g = gs_ref[0]
        for s in range(1, n):
            g = g + gs_ref[s]
        nm = ADAM_B1 * m_ref[...] + (1.0 - ADAM_B1) * g
        nv = ADAM_B2 * v_ref[...] + (1.0 - ADAM_B2) * (g * g)
        g_ref[...] = g
        nm_ref[...] = nm
        nv_ref[...] = nv
        d_ref[...] = -ADAM_LR * ((nm / bc1) / (jnp.sqrt(nv / bc2) + ADAM_EPS) + ADAM_WD * w_ref[...])

    blk = pl.BlockSpec((tr, C), lambda i: (i, 0))
    sds = jax.ShapeDtypeStruct((R, C), F32)
    return pl.pallas_call(
        body, name=name, grid=(R // tr,), out_shape=(sds,) * 4,
        in_specs=[blk, pl.BlockSpec((n, tr, C), lambda i: (0, i, 0)), blk, blk], out_specs=(blk,) * 4,
        compiler_params=_params(dimension_semantics=("parallel",)),
    )(w, gslots, m, v)


def _slot_sum(slots, name):
    n, _, C = slots.shape

    def body(s_ref, o_ref):
        acc = s_ref[0]
        for s in range(1, n):
            acc = acc + s_ref[s]
        o_ref[...] = acc

    return pl.pallas_call(body, name=name, out_shape=jax.ShapeDtypeStruct((1, C), F32),
                          compiler_params=_params())(slots)


def _pad_cols(a, n):
    return jnp.pad(a, ((0, 0), (0, n - a.shape[1])))


def _ungather(g, axis):
    if axis == 0:
        return g.reshape(g.shape[0] * g.shape[1], g.shape[2])
    return jnp.transpose(g, (1, 0, 2)).reshape(g.shape[1], g.shape[0] * g.shape[2])


def _to_slots(full, axis):
    R, C = full.shape
    if axis == 0:
        return full.reshape(N_DEV, R // N_DEV, C)
    return jnp.transpose(full.reshape(R, N_DEV, C // N_DEV), (1, 0, 2))


def kernel(x, c, w_ada, b_ada, g_attn, w_in, b_fgate, g_out_fox, g_out_sb, w_out, g_mlp, w_up, conv_w, conv_b, w_down, g_final, loss_target, m_w_ada, m_b_ada, m_g_attn, m_w_in, m_b_fgate, m_g_out_fox, m_g_out_sb, m_w_out, m_g_mlp, m_w_up, m_conv_w, m_conv_b, m_w_down, m_g_final, v_w_ada, v_b_ada, v_g_attn, v_w_in, v_b_fgate, v_g_out_fox, v_g_out_sb, v_w_out, v_g_mlp, v_w_up, v_conv_w, v_conv_b, v_w_down, v_g_final):
    S, D = x.shape[1], x.shape[2]
    dh = D // 2
    n_heads = dh // HEAD_DIM
    n_qkv = 6 * dh
    ff = w_down.shape[1] * N_DEV
    ffp = -(-ff // (2 * LANES)) * (2 * LANES)
    nc = S // LANES
    me = 4 * lax.axis_index("x") + 2 * lax.axis_index("y") + lax.axis_index("c")
    xs, tgt = x[0], loss_target[0]

    c_all, win_g = _gather_two_level([c, w_in[0].astype(BF16)], name="gather_first")
    c_all = c_all.reshape(N_DEV, D)
    W_in = _ungather(win_g, 1)
    W_qkv, W_f = W_in[:, :n_qkv], _pad_cols(W_in[:, n_qkv:], LANES)
    cb_g, cb_v = _pad_cols(conv_b[:, :ff], ffp), _pad_cols(conv_b[:, ff:], ffp)

    n_ada = w_ada.shape[2]
    b_shard = lax.dynamic_slice(b_ada, (0, me * n_ada), (1, n_ada))
    mod_cols = _ada_fwd(c_all, w_ada[0], b_shard)
    (mod_g,) = _exchange([mod_cols], scatter=False, name="gather_mod")
    mod = lax.dynamic_index_in_dim(mod_g, me, axis=1, keepdims=False).reshape(6, 1, D)
    shift_a, scale_a, gate_a, shift_m, scale_m, gate_m = [mod[k] for k in range(6)]

    h1, h1_t = _prenorm(xs, g_attn, scale_a, shift_a, "prenorm_attn")
    qkv = _mm(h1, W_qkv, BF16, "proj_qkv")
    flog = _mm(h1, W_f, F32, "proj_fgate")
    zf = flog[:, :n_heads] + b_fgate
    z_rows = zf.T.reshape(n_heads * nc, LANES)
    f_rows = _fgate_fwd(z_rows, nc).reshape(n_heads, S)
    f_pairs = jnp.transpose(f_rows.reshape(n_heads // 2, 2, S), (0, 2, 1))
    fox, sb = _att_prep(qkv, f_pairs)
    f_end, k_max = _skip_bounds(qkv[:, dh:2 * dh], f_rows)
    of_t, lse, (wout_g, wup_g, wdown_g, convw_g) = _fox_fwd(
        fox["q_t"], fox["k_n"], fox["v_t"], f_end, k_max,
        [w_out[0].astype(BF16), w_up[0].astype(BF16), w_down[0].astype(BF16), conv_w[0]])
    W_out = _ungather(wout_g, 0)
    W_up = _ungather(wup_g, 1)
    W_g, W_v = _pad_cols(W_up[:, :ff], ffp), _pad_cols(W_up[:, ff:], ffp)
    W_down = jnp.pad(_ungather(wdown_g, 0), ((0, ffp - ff), (0, 0)))
    cw_full = _ungather(convw_g, 1)
    cw_g, cw_v = _pad_cols(cw_full[:, :ff], ffp), _pad_cols(cw_full[:, ff:], ffp)
    os_t = _sb_fwd(sb["q_t"], sb["k_n"], sb["v_t"])
    o_f, o_s = of_t.reshape(dh, S), os_t.reshape(dh, S)
    mix, mix_t = _headnorm_fwd(o_f, o_s, g_out_fox, g_out_sb)
    a_out = _mm(mix, W_out, F32, "proj_out")
    x1, h2, h2_t = _resid_prenorm(xs, a_out, gate_a, g_mlp, scale_m, shift_m)
    up_g, up_v, act, act_t = _mlp_up(h2, W_g, W_v, cw_g, cw_v, cb_g, cb_v)

    dx2, gm, loss_p, dg_final, dgate_m = _loss_head(x1, act, W_down, gate_m, g_final.reshape(1, D), tgt)
    dW_down = _mm_acc(act_t, gm, "bwd_down_w")
    du_g, du_v, p_g, p_v = _conv_act_bwd(gm, W_down, up_g, up_v, cw_g, cw_v, cb_g, cb_v)
    dx1, dscale_m, dshift_m, dg_mlp, ga, dgate_a, dup_g, dup_v, _ = _norm_bwd(
        [du_g, du_v], [W_g, W_v], x1, dx2, g_mlp, scale_m, "norm_mlp_bwd", gate=gate_a, branch=a_out,
        conv=[cw_g, cw_v])
    dW_g = _mm_acc(h2_t, dup_g, "bwd_up_w_g")
    dW_v = _mm_acc(h2_t, dup_v, "bwd_up_w_v")
    dmix = _mm(ga, W_out, F32, "bwd_out_act", nt=True)
    dW_out = _mm_acc(mix_t, ga, "bwd_out_w")
    dof_n, dof_t, dos_n, dos_t, dg_fox, dg_sb = _headnorm_bwd(dmix, o_f, o_s, g_out_fox, g_out_sb)
    dqf_t, dkf, dvf = _fox_bwd(fox["q_t"], fox["q_n"], fox["k_n"], fox["k_t"], fox["v_n"], dof_t, dof_n, of_t, lse,
                              f_end, k_max)
    dW_upf = jnp.concatenate([dW_g[:, :ff], dW_v[:, :ff]], axis=1)
    dcw = jnp.concatenate([p_g[:CONV_W, :ff], p_v[:CONV_W, :ff]], axis=1)
    dqs_t, dks, dvs, (s_out, s_up, s_down, s_cw) = _sb_bwd(
        sb["q_t"], sb["q_n"], sb["k_n"], sb["k_t"], sb["v_n"], dos_t, dos_n, os_t,
        [_to_slots(dW_out, 0), _to_slots(dW_upf, 1), _to_slots(dW_down[:ff], 0), _to_slots(dcw, 1)])
    dparts, dfk = _dqkv_assemble(dqf_t, dkf, dvf, dqs_t, dks, dvs)
    dz_rows, db_fgate = _fgate_bwd(dfk.reshape(n_heads * nc, LANES),
                                   dqf_t[:, Q_F_LANE, :].reshape(n_heads * nc, LANES), z_rows, nc)
    dzf = _pad_cols(dz_rows.reshape(n_heads, S).T, LANES).astype(BF16)
    dW_qkv = _mm_acc_parts(h1_t, dparts, "bwd_in_w")
    dW_f = _mm_acc(h1_t, dzf, "bwd_in_w_fgate")
    dW_in = jnp.concatenate([jnp.transpose(dW_qkv, (1, 0, 2)).reshape(D, n_qkv), dW_f[:, :n_heads]], axis=1)
    bound_in = _to_slots(dW_in, 1)
    bound_in = bound_in.reshape((N_CHIPS, 2) + bound_in.shape[1:])
    (got_in,) = _scatter_to_sibling([bound_in], "scatter_sibling")
    c_idx = lax.axis_index("c").astype(jnp.int32).reshape(1)
    grad_x, dscale_a, dshift_a, dg_attn, (s_in,) = _norm_bwd(
        [dparts, dzf], [W_qkv, W_f], xs, dx1, g_attn, scale_a, "norm_attn_bwd",
        bound=[_pair_add(bound_in, got_in, c_idx, "pair_add")], tm=512)

    dconv_b = jnp.concatenate([p_g[CONV_W:CONV_W + 1, :ff], p_v[CONV_W:CONV_W + 1, :ff]], axis=1)
    parts = [dshift_a, dscale_a, dgate_a, dshift_m, dscale_m, dgate_m,
             dg_attn, db_fgate.reshape(1, n_heads), dg_fox, dg_sb, dg_mlp, dconv_b, dg_final,
             loss_p[:, :1]]
    sizes = [p.shape[1] for p in parts]
    vec = jnp.concatenate(parts, axis=1)
    n_vec = -(-vec.shape[1] // LANES) * LANES
    vec = _pad_cols(vec, n_vec)
    (vec_g,) = _exchange([vec], scatter=False, name="gather_small")
    offs = [0]
    for s in sizes:
        offs.append(offs[-1] + s)

    def small(k0, k1=None):
        k1 = k0 if k1 is None else k1
        return vec_g[:, :, offs[k0]:offs[k1 + 1]]

    dmod_all = small(0, 5).reshape(N_DEV, 6 * D)
    dmod_cols = lax.dynamic_slice(dmod_all, (0, me * n_ada), (N_DEV, n_ada))
    dW_ada = _ada_bwd(c_all.T, dmod_cols)


    res = {}
    res["w_ada"] = _adamw(w_ada[0], dW_ada[None], m_w_ada[0], v_w_ada[0], "adamw_w_ada")
    res["w_in"] = _adamw(w_in[0], s_in, m_w_in[0], v_w_in[0], "adamw_w_in")
    res["w_out"] = _adamw(w_out[0], s_out, m_w_out[0], v_w_out[0], "adamw_w_out")
    res["w_up"] = _adamw(w_up[0], s_up, m_w_up[0], v_w_up[0], "adamw_w_up")
    res["w_down"] = _adamw(w_down[0], s_down, m_w_down[0], v_w_down[0], "adamw_w_down")
    res["conv_w"] = _adamw(conv_w[0], s_cw, m_conv_w[0], v_conv_w[0], "adamw_conv_w")
    small_names = ["b_ada", "g_attn", "b_fgate", "g_out_fox", "g_out_sb", "g_mlp", "conv_b", "g_final"]
    small_w = [b_ada, g_attn, b_fgate, g_out_fox, g_out_sb, g_mlp, conv_b, g_final.reshape(1, D)]
    small_m = [m_b_ada, m_g_attn, m_b_fgate, m_g_out_fox, m_g_out_sb, m_g_mlp, m_conv_b, m_g_final.reshape(1, D)]
    small_v = [v_b_ada, v_g_attn, v_b_fgate, v_g_out_fox, v_g_out_sb, v_g_mlp, v_conv_b, v_g_final.reshape(1, D)]
    small_res = _adamw(jnp.concatenate(small_w, axis=1), small(0, 12), jnp.concatenate(small_m, axis=1),
                       jnp.concatenate(small_v, axis=1), "adamw_small")
    lo = 0
    for nm, wv in zip(small_names, small_w):
        res[nm] = tuple(r[:, lo:lo + wv.shape[1]] for r in small_res)
        lo += wv.shape[1]
    loss = _slot_sum(_pad_cols(small(13).reshape(N_DEV, 1), LANES).reshape(N_DEV, 1, LANES), "loss_sum")[0, 0]

    names = ["w_ada", "b_ada", "g_attn", "w_in", "b_fgate", "g_out_fox", "g_out_sb", "w_out", "g_mlp",
             "w_up", "conv_w", "conv_b", "w_down", "g_final"]

    def shaped(n, a):
        if n == "g_final":
            return a.reshape(D)
        if n in ("b_ada", "g_attn", "b_fgate", "g_out_fox", "g_out_sb", "g_mlp", "conv_b"):
            return a
        return a[None]

    outs = [loss, grad_x[None]]
    for k in range(4):
        outs += [shaped(n, res[n][k]) for n in names]
    return tuple(outs)
```

```python
import jax
import jax.numpy as jnp
from jax import lax
from jax.experimental import pallas as pl
from jax.experimental.pallas import tpu as pltpu

F32 = jnp.float32
BF16 = jnp.bfloat16
HIGHEST = lax.Precision.HIGHEST

N_DEV = 8
LANES = 128
HEAD_DIM = 64
EPS = 1e-6
CONV_W = 3
CONV_COLS = 2816
HALO = 16
ATT_BQ = 512
ATT_BK = 512
SCAN_BK = 128
VMEM_LIMIT = 56 * 1024 * 1024

ADAM_LR = 0.001
ADAM_B1 = 0.9
ADAM_B2 = 0.999
ADAM_EPS = 1e-08
ADAM_WD = 0.01
ADAM_STEP = 10


def _params(**kw):
    return pltpu.CompilerParams(vmem_limit_bytes=VMEM_LIMIT, **kw)


def _tile(n, cap):
    if n <= cap:
        return n
    best = None
    for t in range(LANES, cap + 1, LANES):
        if n % t == 0:
            best = t
    assert best is not None, (n, cap)
    return best


def _dot(a, b, **kw):
    return jnp.dot(a, b, preferred_element_type=F32, **kw)


def _exchange_copies(ins, outs, send_sems, recv_sems, loc_sems, scatter):
    n = len(ins)
    if n == 0:
        return []
    x, y, c = lax.axis_index("x"), lax.axis_index("y"), lax.axis_index("c")
    me = 4 * x + 2 * y + c
    copies = []
    for a in range(n):
        src = ins[a].at[me] if scatter else ins[a]
        copies.append(pltpu.make_async_copy(src, outs[a].at[me], loc_sems.at[a]))
    for k in range(1, N_DEV):
        px = 1 - x if k & 4 else x
        py = 1 - y if k & 2 else y
        pc = 1 - c if k & 1 else c
        peer = 4 * px + 2 * py + pc
        for a in range(n):
            src = ins[a].at[peer] if scatter else ins[a]
            copies.append(pltpu.make_async_remote_copy(
                src_ref=src, dst_ref=outs[a].at[me],
                send_sem=send_sems.at[a, k - 1], recv_sem=recv_sems.at[a, k - 1],
                device_id=(px, py, pc), device_id_type=pl.DeviceIdType.MESH))
    return copies


def _exchange_out_shapes(arrays, scatter):
    return [jax.ShapeDtypeStruct((N_DEV,) + tuple(a.shape[1:] if scatter else a.shape), a.dtype) for a in arrays]


def _exchange_sems(n):
    return [pltpu.SemaphoreType.DMA((n, N_DEV - 1)), pltpu.SemaphoreType.DMA((n, N_DEV - 1)),
            pltpu.SemaphoreType.DMA((n,))]


def _exchange(arrays, scatter, name):
    n = len(arrays)

    def body(*refs):
        copies = _exchange_copies(refs[:n], refs[n:2 * n], *refs[2 * n:], scatter)
        for cp in copies:
            cp.start()
        for cp in copies:
            cp.wait()

    any_spec = pl.BlockSpec(memory_space=pl.ANY)
    return pl.pallas_call(
        body, name=name, out_shape=tuple(_exchange_out_shapes(arrays, scatter)),
        in_specs=[any_spec] * n, out_specs=tuple([any_spec] * n),
        scratch_shapes=_exchange_sems(n),
        compiler_params=pltpu.CompilerParams(has_side_effects=True),
    )(*arrays)


def _gather_two_level(arrays, name):
    n = len(arrays)
    out_shape = [jax.ShapeDtypeStruct((N_DEV,) + tuple(a.shape), a.dtype) for a in arrays]

    def body(*refs):
        ins, outs = refs[:n], refs[n:2 * n]
        send_sems, recv_sems, loc_sems = refs[2 * n:]
        x, y, c = lax.axis_index("x"), lax.axis_index("y"), lax.axis_index("c")
        me, sibling = (x, y, c), (x, y, 1 - c)
        chips = [(1 - x, y), (x, 1 - y), (1 - x, 1 - y)]

        def slot(px, py, pc):
            return 4 * px + 2 * py + pc

        def copy(a, k, block, to, src=None):
            dst = outs[a].at[slot(*block)]
            return pltpu.make_async_remote_copy(
                src_ref=dst if src is None else src, dst_ref=dst,
                send_sem=send_sems.at[a, k], recv_sem=recv_sems.at[a, k],
                device_id=to, device_id_type=pl.DeviceIdType.MESH)

        local = [pltpu.make_async_copy(ins[a], outs[a].at[slot(*me)], loc_sems.at[a]) for a in range(n)]
        for cp in local:
            cp.start()
        first = []
        for a in range(n):
            first.append(copy(a, 0, me, sibling, src=ins[a]))
            first += [copy(a, 1 + j, me, (*chip, c), src=ins[a]) for j, chip in enumerate(chips)]
        for cp in first:
            cp.start()
        passed = []
        for j, chip in enumerate(chips):
            for a in range(n):
                copy(a, 1 + j, (*chip, c), me).wait_recv()
                cp = copy(a, 4 + j, (*chip, c), sibling)
                cp.start()
                passed.append(cp)
        for a in range(n):
            copy(a, 0, sibling, me).wait_recv()
            for j, chip in enumerate(chips):
                copy(a, 4 + j, (*chip, 1 - c), me).wait_recv()
        for cp in first + passed:
            cp.wait_send()
        for cp in local:
            cp.wait()

    any_spec = pl.BlockSpec(memory_space=pl.ANY)
    return pl.pallas_call(
        body, name=name, out_shape=tuple(out_shape),
        in_specs=[any_spec] * n, out_specs=tuple([any_spec] * n),
        scratch_shapes=[pltpu.SemaphoreType.DMA((n, N_DEV - 1)), pltpu.SemaphoreType.DMA((n, N_DEV - 1)),
                        pltpu.SemaphoreType.DMA((n,))],
        compiler_params=pltpu.CompilerParams(has_side_effects=True),
    )(*arrays)


N_CHIPS = 4


def _scatter_to_sibling(arrays, name):
    n = len(arrays)
    out_shape = [jax.ShapeDtypeStruct((N_CHIPS,) + tuple(a.shape[2:]), a.dtype) for a in arrays]

    def body(*refs):
        ins, outs = refs[:n], refs[n:2 * n]
        send_sems, recv_sems = refs[2 * n:]
        x, y, c = lax.axis_index("x"), lax.axis_index("y"), lax.axis_index("c")
        copies = []
        for a in range(n):
            for q in range(N_CHIPS):
                cp = pltpu.make_async_remote_copy(
                    src_ref=ins[a].at[q, 1 - c], dst_ref=outs[a].at[q],
                    send_sem=send_sems.at[a, q], recv_sem=recv_sems.at[a, q],
                    device_id=(x, y, 1 - c), device_id_type=pl.DeviceIdType.MESH)
                cp.start()
                copies.append(cp)
        for cp in copies:
            cp.wait()

    any_spec = pl.BlockSpec(memory_space=pl.ANY)
    return pl.pallas_call(
        body, name=name, out_shape=tuple(out_shape),
        in_specs=[any_spec] * n, out_specs=tuple([any_spec] * n),
        scratch_shapes=[pltpu.SemaphoreType.DMA((n, N_CHIPS)), pltpu.SemaphoreType.DMA((n, N_CHIPS))],
        compiler_params=pltpu.CompilerParams(has_side_effects=True),
    )(*arrays)


def _pair_add(mine, got, c_idx, name):
    _, _, R, C = mine.shape
    tr = 256 if (R % 256 == 0 and R > 256) else R

    def body(c_ref, m_ref, g_ref, o_ref):
        o_ref[...] = m_ref[...] + g_ref[...]

    grid_spec = pltpu.PrefetchScalarGridSpec(
        num_scalar_prefetch=1, grid=(N_CHIPS, R // tr),
        in_specs=[pl.BlockSpec((None, None, tr, C), lambda q, i, c_ref: (q, c_ref[0], i, 0)),
                  pl.BlockSpec((None, tr, C), lambda q, i, c_ref: (q, i, 0))],
        out_specs=pl.BlockSpec((None, tr, C), lambda q, i, c_ref: (q, i, 0)))
    return pl.pallas_call(
        body, name=name, grid_spec=grid_spec, out_shape=jax.ShapeDtypeStruct((N_CHIPS, R, C), mine.dtype),
        compiler_params=_params(dimension_semantics=("parallel", "parallel")),
    )(c_idx, mine, got)


def _chip_exchange_copies(ins, outs, send_sems, recv_sems, loc_sems):
    n = len(ins)
    x, y, c = lax.axis_index("x"), lax.axis_index("y"), lax.axis_index("c")
    myq = 2 * x + y
    copies = [pltpu.make_async_copy(ins[a].at[myq], outs[a].at[myq], loc_sems.at[a]) for a in range(n)]
    for k in range(1, N_CHIPS):
        qx = 1 - x if k & 2 else x
        qy = 1 - y if k & 1 else y
        for a in range(n):
            copies.append(pltpu.make_async_remote_copy(
                src_ref=ins[a].at[2 * qx + qy], dst_ref=outs[a].at[myq],
                send_sem=send_sems.at[a, k - 1], recv_sem=recv_sems.at[a, k - 1],
                device_id=(qx, qy, c), device_id_type=pl.DeviceIdType.MESH))
    return copies


def _chip_exchange_sems(n):
    return [pltpu.SemaphoreType.DMA((n, N_CHIPS - 1)), pltpu.SemaphoreType.DMA((n, N_CHIPS - 1)),
            pltpu.SemaphoreType.DMA((n,))]


def _dot_nt(a, b):
    return lax.dot_general(a, b, (((1,), (1,)), ((), ())), preferred_element_type=F32)


def _rhs_spec(b, tn, nt):
    if nt:
        return pl.BlockSpec((tn, b.shape[1]), lambda i, j: (j, 0))
    return pl.BlockSpec((b.shape[0], tn), lambda i, j: (0, j))


def _mm(a, b, out_dtype, name, tm=1024, tn=512, nt=False):
    M, K = a.shape
    N = b.shape[0] if nt else b.shape[1]
    tm, tn = _tile(M, tm), _tile(N, tn)
    dot = _dot_nt if nt else _dot

    def body(a_ref, b_ref, o_ref):
        o_ref[...] = dot(a_ref[...], b_ref[...]).astype(out_dtype)

    return pl.pallas_call(
        body, name=name, out_shape=jax.ShapeDtypeStruct((M, N), out_dtype),
        grid=(M // tm, N // tn),
        in_specs=[pl.BlockSpec((tm, K), lambda i, j: (i, 0)), _rhs_spec(b, tn, nt)],
        out_specs=pl.BlockSpec((tm, tn), lambda i, j: (i, j)),
        compiler_params=_params(dimension_semantics=("parallel", "parallel")),
    )(a, b)


def _mm_acc(a, b, name, tm=1408, tn=1408, tk=1024):
    M, S = a.shape
    _, N = b.shape
    tm, tn, tk = _tile(M, tm), _tile(N, tn), _tile(S, tk)

    def body(a_ref, b_ref, o_ref):
        @pl.when(pl.program_id(2) == 0)
        def _():
            o_ref[...] = jnp.zeros_like(o_ref)

        o_ref[...] += _dot(a_ref[...], b_ref[...])

    return pl.pallas_call(
        body, name=name, out_shape=jax.ShapeDtypeStruct((M, N), F32),
        grid=(M // tm, N // tn, S // tk),
        in_specs=[pl.BlockSpec((tm, tk), lambda i, j, k: (i, k)), pl.BlockSpec((tk, tn), lambda i, j, k: (k, j))],
        out_specs=pl.BlockSpec((tm, tn), lambda i, j, k: (i, j)),
        compiler_params=_params(dimension_semantics=("parallel", "parallel", "arbitrary")),
    )(a, b)


def _mm_acc_parts(a, parts, name, tm=1024, tk=1024):
    M, S = a.shape
    P, _, K = parts.shape
    tm, tk = _tile(M, tm), _tile(S, tk)

    def body(a_ref, b_ref, o_ref):
        @pl.when(pl.program_id(2) == 0)
        def _():
            o_ref[...] = jnp.zeros_like(o_ref)

        o_ref[...] += _dot(a_ref[...], b_ref[...])

    return pl.pallas_call(
        body, name=name, out_shape=jax.ShapeDtypeStruct((P, M, K), F32), grid=(P, M // tm, S // tk),
        in_specs=[pl.BlockSpec((tm, tk), lambda k, i, s: (i, s)), pl.BlockSpec((None, tk, K), lambda k, i, s: (k, s, 0))],
        out_specs=pl.BlockSpec((None, tm, K), lambda k, i, s: (k, i, 0)),
        compiler_params=_params(dimension_semantics=("parallel", "parallel", "arbitrary")),
    )(a, parts)


def _silu(z):
    return z * (1.0 / (1.0 + jnp.exp(-z)))


def _ada_fwd(c_all, w_shard, b_shard):
    n = w_shard.shape[1]

    def body(c_ref, w_ref, b_ref, o_ref):
        o_ref[...] = _dot(_silu(c_ref[...]), w_ref[...], precision=HIGHEST) + b_ref[...]

    return pl.pallas_call(body, name="ada_fwd", out_shape=jax.ShapeDtypeStruct((N_DEV, n), F32),
                          compiler_params=_params())(c_all, w_shard, b_shard)


def _ada_bwd(c_all_t, dmod_cols):
    D = c_all_t.shape[0]
    n = dmod_cols.shape[1]

    def body(ct_ref, dm_ref, o_ref):
        sc = _silu(ct_ref[...])
        dm = dm_ref[...]
        acc = sc[:, 0:1] * dm[0:1, :]
        for b in range(1, N_DEV):
            acc = acc + sc[:, b:b + 1] * dm[b:b + 1, :]
        o_ref[...] = acc

    return pl.pallas_call(body, name="ada_bwd", out_shape=jax.ShapeDtypeStruct((D, n), F32),
                          compiler_params=_params())(c_all_t, dmod_cols)


def _row_specs(tm, widths):
    return [pl.BlockSpec((tm, w), lambda i: (i, 0)) for w in widths]


def _vec_spec(w):
    return pl.BlockSpec((1, w), lambda i: (0, 0))


def _col_spec(tm, w):
    return pl.BlockSpec((w, tm), lambda i: (0, i))


def _prenorm(x, g, scale, shift, name):
    S, D = x.shape
    tm = _tile(S, 512)

    def body(x_ref, g_ref, sc_ref, sh_ref, h_ref, ht_ref):
        xv = x_ref[...]
        r = lax.rsqrt(jnp.mean(xv * xv, axis=-1, keepdims=True) + EPS)
        h = (xv * r) * g_ref[...] * (1.0 + sc_ref[...]) + sh_ref[...]
        h_ref[...] = h.astype(BF16)
        ht_ref[...] = h.T.astype(BF16)

    return pl.pallas_call(
        body, name=name, grid=(S // tm,),
        out_shape=(jax.ShapeDtypeStruct((S, D), BF16), jax.ShapeDtypeStruct((D, S), BF16)),
        in_specs=_row_specs(tm, [D]) + [_vec_spec(D)] * 3,
        out_specs=(_row_specs(tm, [D])[0], _col_spec(tm, D)),
        compiler_params=_params(dimension_semantics=("parallel",)),
    )(x, g, scale, shift)


def _group_ones():
    r = (lax.broadcasted_iota(jnp.int32, (2 * LANES, LANES), 0) % LANES) // HEAD_DIM
    c = lax.broadcasted_iota(jnp.int32, (2 * LANES, LANES), 1) // HEAD_DIM
    return (r == c).astype(BF16)


def _group_sum(t, ones):
    hi = t.astype(BF16)
    lo = (t - hi.astype(F32)).astype(BF16)
    return _dot(jnp.concatenate([hi, lo], axis=1), ones)


def _headnorm_fwd(o_f, o_s, g_f, g_s):
    dh, S = o_f.shape
    tm = _tile(S, 512)

    def body(of_ref, os_ref, gf_ref, gs_ref, mix_ref, mixt_ref):
        ones = _group_ones()
        for part, (o_ref, g_ref) in enumerate(((of_ref, gf_ref), (os_ref, gs_ref))):
            for t in range(dh // LANES):
                cols = slice(t * LANES, (t + 1) * LANES)
                out = slice(part * dh + t * LANES, part * dh + (t + 1) * LANES)
                o = o_ref[cols, :].T
                ms = _group_sum(o * o, ones) * (1.0 / HEAD_DIM)
                mixn = o * lax.rsqrt(ms + EPS) * g_ref[:, cols]
                mix_ref[:, out] = mixn.astype(BF16)
                mixt_ref[out, :] = mixn.T.astype(BF16)

    return pl.pallas_call(
        body, name="headnorm_fwd", grid=(S // tm,),
        out_shape=(jax.ShapeDtypeStruct((S, 2 * dh), BF16), jax.ShapeDtypeStruct((2 * dh, S), BF16)),
        in_specs=[_col_spec(tm, dh)] * 2 + [_vec_spec(dh)] * 2,
        out_specs=(_row_specs(tm, [2 * dh])[0], _col_spec(tm, 2 * dh)),
        compiler_params=_params(dimension_semantics=("parallel",)),
    )(o_f, o_s, g_f, g_s)


def _resid_prenorm(x, a_out, gate, g, scale, shift):
    S, D = x.shape
    tm = _tile(S, 512)

    def body(x_ref, a_ref, gt_ref, g_ref, sc_ref, sh_ref, x1_ref, h_ref, ht_ref):
        x1 = x_ref[...] + gt_ref[...] * a_ref[...]
        x1_ref[...] = x1
        r = lax.rsqrt(jnp.mean(x1 * x1, axis=-1, keepdims=True) + EPS)
        h = (x1 * r) * g_ref[...] * (1.0 + sc_ref[...]) + sh_ref[...]
        h_ref[...] = h.astype(BF16)
        ht_ref[...] = h.T.astype(BF16)

    return pl.pallas_call(
        body, name="resid_prenorm", grid=(S // tm,),
        out_shape=(jax.ShapeDtypeStruct((S, D), F32), jax.ShapeDtypeStruct((S, D), BF16),
                   jax.ShapeDtypeStruct((D, S), BF16)),
        in_specs=_row_specs(tm, [D, D]) + [_vec_spec(D)] * 4,
        out_specs=tuple(_row_specs(tm, [D, D]) + [_col_spec(tm, D)]),
        compiler_params=_params(dimension_semantics=("parallel",)),
    )(x, a_out, gate, g, scale, shift)


def _shift_down(main, halo, k):
    ext = jnp.concatenate([halo, main], axis=0)
    return pltpu.roll(ext, k, 0)[halo.shape[0]:]


def _shift_up(main, halo, k):
    ext = jnp.concatenate([main, halo], axis=0)
    n = ext.shape[0]
    return pltpu.roll(ext, n - k, 0)[:main.shape[0]]


def _conv(up, up_halo, w_ref, b_ref):
    return (w_ref[2:3, :] * up + w_ref[1:2, :] * _shift_down(up, up_halo, 1)
            + w_ref[0:1, :] * _shift_down(up, up_halo, 2) + b_ref[...])


def _prev_halo_map(tm):
    step = tm // HALO
    return lambda j, i: (jnp.maximum(i * step - 1, 0), j)


MLP_TM = 256
MLP_CT = 2816
CARRY = 8


def _mlp_up(h, wg, wv, cwg, cwv, cbg, cbv):
    S, D = h.shape
    F = wg.shape[1]
    tm, ct = _tile(S, MLP_TM), _tile(F, MLP_CT)
    nct = F // ct

    def body(h_ref, wg_ref, wv_ref, cwg_ref, cwv_ref, cbg_ref, cbv_ref,
             upg_ref, upv_ref, act_ref, actt_ref, hg_scr, hv_scr):
        i, j = pl.program_id(0), pl.program_id(1)
        hv = h_ref[...]
        us = []
        for w_ref, cw_ref, cb_ref, up_ref, scr in ((wg_ref, cwg_ref, cbg_ref, upg_ref, hg_scr),
                                                   (wv_ref, cwv_ref, cbv_ref, upv_ref, hv_scr)):
            up = _dot(hv, w_ref[...]).astype(BF16)
            up_ref[...] = up
            upf = up.astype(F32)
            halo = jnp.where(i == 0, 0.0, scr[j])
            us.append(_conv(upf, halo, cw_ref, cb_ref))
            scr[j] = upf[tm - CARRY:, :]
        act = _silu(us[0]) * us[1]
        act_ref[...] = act.astype(BF16)
        actt_ref[...] = act.T.astype(BF16)

    blk = pl.BlockSpec((tm, ct), lambda i, j: (i, j))
    wspec = pl.BlockSpec((D, ct), lambda i, j: (0, j))
    cwspec = pl.BlockSpec((CONV_W, ct), lambda i, j: (0, j))
    cbspec = pl.BlockSpec((1, ct), lambda i, j: (0, j))
    sds = jax.ShapeDtypeStruct((S, F), BF16)
    return pl.pallas_call(
        body, name="mlp_up", grid=(S // tm, nct),
        out_shape=(sds, sds, sds, jax.ShapeDtypeStruct((F, S), BF16)),
        in_specs=[pl.BlockSpec((tm, D), lambda i, j: (i, 0)), wspec, wspec, cwspec, cwspec, cbspec, cbspec],
        out_specs=(blk, blk, blk, pl.BlockSpec((ct, tm), lambda i, j: (j, i))),
        scratch_shapes=[pltpu.VMEM((nct, CARRY, ct), F32), pltpu.VMEM((nct, CARRY, ct), F32)],
        compiler_params=_params(dimension_semantics=("arbitrary", "arbitrary")),
    )(h, wg, wv, cwg, cwv, cbg, cbv)


def _conv_act_bwd(gm, w_down, up_g, up_v, cwg, cwv, cbg, cbv):
    S, F = up_g.shape
    D = gm.shape[1]
    tm, ct = _tile(S, 256), _tile(F, CONV_COLS)
    nct = F // ct

    def body(gm_ref, wd_ref, ug_ref, uv_ref, hg_ref, hv_ref, wg_ref, wv_ref, bg_ref, bv_ref,
             dug_ref, duv_ref, pg_ref, pv_ref):
        first = pl.program_id(1) == 0

        @pl.when(first)
        def _():
            pg_ref[...] = jnp.zeros_like(pg_ref)
            pv_ref[...] = jnp.zeros_like(pv_ref)

        da = _dot_nt(gm_ref[...], wd_ref[...])
        taps = []
        for u_ref, h_ref in ((ug_ref, hg_ref), (uv_ref, hv_ref)):
            h = jnp.where(first, 0.0, h_ref[...].astype(F32))
            uu = u_ref[...].astype(F32)
            taps.append((_shift_down(uu, h, 2), _shift_down(uu, h, 1), uu))
        u_g = wg_ref[0:1, :] * taps[0][0] + wg_ref[1:2, :] * taps[0][1] + wg_ref[2:3, :] * taps[0][2] + bg_ref[...]
        u_v = wv_ref[0:1, :] * taps[1][0] + wv_ref[1:2, :] * taps[1][1] + wv_ref[2:3, :] * taps[1][2] + bv_ref[...]
        sg = 1.0 / (1.0 + jnp.exp(-u_g))
        du_g = da * u_v * (sg * (1.0 + u_g * (1.0 - sg)))
        du_v = da * (u_g * sg)
        dug_ref[...] = du_g.astype(BF16)
        duv_ref[...] = du_v.astype(BF16)
        for du, tp, p_ref in ((du_g, taps[0], pg_ref), (du_v, taps[1], pv_ref)):
            for k in range(CONV_W):
                p_ref[k:k + 1, :] += jnp.sum(du * tp[k], axis=0, keepdims=True)
            p_ref[CONV_W:CONV_W + 1, :] += jnp.sum(du, axis=0, keepdims=True)

    main = pl.BlockSpec((tm, ct), lambda j, i: (i, j))
    halo = pl.BlockSpec((HALO, ct), _prev_halo_map(tm))
    wspec = pl.BlockSpec((CONV_W, ct), lambda j, i: (0, j))
    bspec = pl.BlockSpec((1, ct), lambda j, i: (0, j))
    pspec = pl.BlockSpec((8, ct), lambda j, i: (0, j))
    return pl.pallas_call(
        body, name="conv_act_bwd", grid=(nct, S // tm),
        out_shape=(jax.ShapeDtypeStruct((S, F), BF16), jax.ShapeDtypeStruct((S, F), BF16),
                   jax.ShapeDtypeStruct((8, F), F32), jax.ShapeDtypeStruct((8, F), F32)),
        in_specs=[pl.BlockSpec((tm, D), lambda j, i: (i, 0)), pl.BlockSpec((ct, D), lambda j, i: (j, 0)),
                  main, main, halo, halo, wspec, wspec, bspec, bspec],
        out_specs=(main, main, pspec, pspec),
        compiler_params=_params(dimension_semantics=("parallel", "arbitrary")),
    )(gm, w_down, up_g, up_v, up_g, up_v, cwg, cwv, cbg, cbv)


def _conv_bwd_taps(d, halo, w_ref):
    return w_ref[2:3, :] * d + w_ref[1:2, :] * _shift_up(d, halo, 1) + w_ref[0:1, :] * _shift_up(d, halo, 2)


def _scan_mats(R, nc, reverse):
    i = lax.broadcasted_iota(jnp.int32, (LANES, LANES), 0)
    j = lax.broadcasted_iota(jnp.int32, (LANES, LANES), 1)
    inner = ((i >= j) if reverse else (i <= j)).astype(F32)
    r = lax.broadcasted_iota(jnp.int32, (R, R), 0)
    c = lax.broadcasted_iota(jnp.int32, (R, R), 1)
    same = (r // nc) == (c // nc)
    outer = (same & ((c > r) if reverse else (c < r))).astype(F32)
    return inner, outer


def _chunk_scan(v, inner, outer, reverse):
    w = _dot(v, inner, precision=HIGHEST)
    col = 0 if reverse else LANES - 1
    carry = _dot(outer, w, precision=HIGHEST)[:, col:col + 1]
    return w + carry


def _fgate_fwd(z_rows, nc):
    R = z_rows.shape[0]

    def body(z_ref, f_ref):
        z = z_ref[...]
        logf = jnp.minimum(z, 0.0) - jnp.log(1.0 + jnp.exp(-jnp.abs(z)))
        inner, outer = _scan_mats(R, nc, False)
        f_ref[...] = _chunk_scan(logf, inner, outer, False)

    return pl.pallas_call(body, name="fgate_fwd", out_shape=jax.ShapeDtypeStruct((R, LANES), F32),
                          compiler_params=_params())(z_rows)


def _fgate_bwd(dfk_neg_rows, dfq_rows, z_rows, nc):
    R = z_rows.shape[0]
    nh = R // nc

    def body(dfk_ref, dfq_ref, z_ref, dz_ref, db_ref):
        inner, outer = _scan_mats(R, nc, True)
        dlogf = _chunk_scan(dfq_ref[...] - dfk_ref[...], inner, outer, True)
        dz = dlogf * (1.0 / (1.0 + jnp.exp(z_ref[...])))
        dz_ref[...] = dz
        hr = lax.broadcasted_iota(jnp.int32, (nh, R), 0)
        hc = lax.broadcasted_iota(jnp.int32, (nh, R), 1) // nc
        per_head = _dot((hr == hc).astype(F32), dz, precision=HIGHEST)
        db_ref[...] = jnp.sum(per_head, axis=1, keepdims=True)

    return pl.pallas_call(
        body, name="fgate_bwd",
        out_shape=(jax.ShapeDtypeStruct((R, LANES), F32), jax.ShapeDtypeStruct((nh, 1), F32)),
        compiler_params=_params())(dfk_neg_rows, dfq_rows, z_rows)


_NEG = -1e30
SKIP_BELOW = -106.0
_SCALE = HEAD_DIM ** -0.5
N_SCAN = ATT_BK // SCAN_BK
F_PARTS = 3
Q_F_LANE = HEAD_DIM
Q_ONE_LANE = HEAD_DIM + F_PARTS


def _kv_slice(j):
    return pl.ds(pl.multiple_of(j * ATT_BK, ATT_BK), ATT_BK)


def _mask_t(strict):
    s = lax.broadcasted_iota(jnp.int32, (ATT_BK, ATT_BQ), 0)
    t = lax.broadcasted_iota(jnp.int32, (ATT_BK, ATT_BQ), 1)
    return (s < t) if strict else (s <= t)


def _walk_down(i, step, alive, carry):
    carry = step(i, carry, True)

    def cond(st):
        n, go, _ = st
        return jnp.logical_and(n < i, go)

    def body(st):
        n, _, cr = st
        j = i - 1 - n
        cr = step(j, cr, False)
        return n + 1, alive(jnp.maximum(j - 1, 0), cr), cr

    return lax.while_loop(cond, body, (jnp.int32(0), alive(jnp.maximum(i - 1, 0), carry), carry))[2]


def _t_block(rows):
    return pl.BlockSpec((None, rows, ATT_BQ), lambda h, i, *_: (h, 0, i))


def _t_full(rows, S):
    return pl.BlockSpec((None, rows, S), lambda h, i, *_: (h, 0, 0))


def _n_block():
    return pl.BlockSpec((None, ATT_BQ, LANES), lambda h, i, *_: (h, i, 0))


def _n_full(S):
    return pl.BlockSpec((None, S, LANES), lambda h, i, *_: (h, 0, 0))


def _heads(t):
    S = t.shape[0]
    return jnp.transpose(t.reshape(S, -1, HEAD_DIM), (1, 0, 2))


def _skip_bounds(k_cols, f_rows):
    H, S = f_rows.shape
    f_end = f_rows.reshape(H, S // ATT_BK, ATT_BK)[:, :, -1]
    k_sq = jnp.sum(jnp.square(_heads(k_cols).astype(F32)), axis=-1).reshape(H, S // ATT_BK, ATT_BK)
    k_max = lax.cummax(jnp.sqrt(jnp.max(k_sq, axis=-1)), axis=1)
    return f_end, k_max


def _bf16_parts(f):
    hi = f.astype(BF16).astype(F32)
    mid = (f - hi).astype(BF16).astype(F32)
    return hi, mid, (f - hi - mid).astype(BF16).astype(F32)


def _att_prep(qkv, f_pairs):
    S = qkv.shape[0]
    n_pairs = qkv.shape[1] // (6 * LANES)
    H = 2 * n_pairs
    tm = _tile(S, 512)

    def body(qf_ref, kf_ref, vf_ref, qs_ref, ks_ref, vs_ref, f_ref,
             fqn, fqt, fkn, fkt, fvn, fvt, sqn, sqt, skn, skt, svn, svt):
        lane = lax.broadcasted_iota(jnp.int32, (1, LANES), 1)
        f = f_ref[...]

        def head(ref, e):
            t = ref[...].astype(F32)
            if e == 1:
                t = pltpu.roll(t, HEAD_DIM, 1)
            return jnp.where(lane < HEAD_DIM, t, 0.0)

        def at(first):
            return jnp.logical_and(lane >= first, lane < first + F_PARTS)

        for e in range(2):
            parts = _bf16_parts(f[:, e:e + 1])
            f_lanes = sum(jnp.where(lane == Q_F_LANE + k, parts[k], 0.0) for k in range(F_PARTS))
            nf_lanes = sum(jnp.where(lane == Q_ONE_LANE + k, parts[k], 0.0) for k in range(F_PARTS))
            vals = (
                (fqn, fqt, LANES, head(qf_ref, e) * _SCALE + f_lanes + jnp.where(at(Q_ONE_LANE), 1.0, 0.0)),
                (fkn, fkt, LANES, head(kf_ref, e) + jnp.where(at(Q_F_LANE), 1.0, 0.0) - nf_lanes),
                (fvn, fvt, HEAD_DIM, head(vf_ref, e)),
                (sqn, sqt, LANES, head(qs_ref, e) * _SCALE),
                (skn, skt, LANES, head(ks_ref, e)),
                (svn, svt, HEAD_DIM, head(vs_ref, e)),
            )
            for n_ref, t_ref, rows, val in vals:
                n_ref[e] = val.astype(BF16)
                t_ref[e] = val.T[:rows].astype(BF16)

    col = lambda base: pl.BlockSpec((tm, LANES), lambda i, p: (i, base + p))
    n_spec = pl.BlockSpec((2, tm, LANES), lambda i, p: (p, i, 0))
    t_spec = lambda rows: pl.BlockSpec((2, rows, tm), lambda i, p: (p, 0, i))
    n_sds = jax.ShapeDtypeStruct((H, S, LANES), BF16)
    t_sds = lambda rows: jax.ShapeDtypeStruct((H, rows, S), BF16)
    group = ([n_sds, t_sds(LANES), n_sds, t_sds(LANES), n_sds, t_sds(HEAD_DIM)],
             [n_spec, t_spec(LANES), n_spec, t_spec(LANES), n_spec, t_spec(HEAD_DIM)])
    res = pl.pallas_call(
        body, name="att_prep", grid=(S // tm, n_pairs),
        out_shape=tuple(group[0] * 2),
        in_specs=[col(k * n_pairs) for k in range(6)] + [pl.BlockSpec((None, tm, 2), lambda i, p: (p, i, 0))],
        out_specs=tuple(group[1] * 2),
        compiler_params=_params(dimension_semantics=("parallel", "parallel")),
    )(qkv, qkv, qkv, qkv, qkv, qkv, f_pairs)
    names = ("q_n", "q_t", "k_n", "k_t", "v_n", "v_t")
    return dict(zip(names, res[:6])), dict(zip(names, res[6:]))


def _fox_reach(qt, fend_ref, kmax_ref, h):
    qf = qt.astype(F32)
    q_norm = jnp.sqrt(jnp.sum(jnp.square(qf[:HEAD_DIM]), axis=0, keepdims=True))
    f_t = jnp.sum(qf[Q_F_LANE:Q_F_LANE + F_PARTS], axis=0, keepdims=True)
    return lambda j: q_norm * kmax_ref[h, j] + f_t - fend_ref[h, j]


def _fox_fwd(q_t, k_n, v_t, f_end, k_max, shards):
    H, _, S = q_t.shape
    n, nq = len(shards), S // ATT_BQ

    def body(fend_ref, kmax_ref, qt_ref, k_ref, vt_ref, *rest):
        ins, (ot_ref, lse_ref), outs, sems = rest[:n], rest[n:n + 2], rest[n + 2:2 * n + 2], rest[2 * n + 2:]
        h, i = pl.program_id(0), pl.program_id(1)

        @pl.when(jnp.logical_and(h == 0, i == 0))
        def _():
            for cp in _exchange_copies(ins, outs, *sems, False):
                cp.start()

        qt = qt_ref[...]
        reach = _fox_reach(qt, fend_ref, kmax_ref, h)

        def step(j, carry, masked):
            m, l, acc = carry
            ks = _kv_slice(j)
            s = _dot(k_ref[ks, :], qt)
            if masked:
                s = jnp.where(_mask_t(False), s, _NEG)
            mn = jnp.maximum(m, jnp.max(s, axis=0, keepdims=True))
            alpha = jnp.exp(m - mn)
            p = jnp.exp(s - mn)
            l = alpha * l + jnp.sum(p, axis=0, keepdims=True)
            acc = acc * alpha + _dot(vt_ref[:, ks], p.astype(BF16))
            return mn, l, acc

        def alive(j, carry):
            return jnp.max(reach(j) - carry[0]) > SKIP_BELOW

        row = jnp.zeros((1, ATT_BQ), F32)
        m, l, acc = _walk_down(i, step, alive, (row + _NEG, row, jnp.zeros((HEAD_DIM, ATT_BQ), F32)))
        ot_ref[...] = acc / l
        lse_ref[...] = m + jnp.log(l)

        @pl.when(jnp.logical_and(h == H - 1, i == nq - 1))
        def _():
            for cp in _exchange_copies(ins, outs, *sems, False):
                cp.wait()

    any_spec = pl.BlockSpec(memory_space=pl.ANY)
    grid_spec = pltpu.PrefetchScalarGridSpec(
        num_scalar_prefetch=2, grid=(H, nq),
        in_specs=[_t_block(LANES), _n_full(S), _t_full(HEAD_DIM, S)] + [any_spec] * n,
        out_specs=tuple([_t_block(HEAD_DIM), _t_block(1)] + [any_spec] * n),
        scratch_shapes=_exchange_sems(n))
    res = pl.pallas_call(
        body, name="fox_fwd", grid_spec=grid_spec,
        out_shape=tuple([jax.ShapeDtypeStruct((H, HEAD_DIM, S), F32), jax.ShapeDtypeStruct((H, 1, S), F32)]
                        + _exchange_out_shapes(shards, False)),
        compiler_params=_params(dimension_semantics=("arbitrary", "arbitrary"), has_side_effects=True),
    )(f_end, k_max, q_t, k_n, v_t, *shards)
    return res[0], res[1], res[2:]


def _fox_bwd(q_t, q_n, k_n, k_t, v_n, do_t, do_n, o_t, lse, f_end, k_max):
    H, _, S = q_t.shape

    def body(fend_ref, kmax_ref, qt_ref, qn_ref, k_ref, kt_ref, v_ref, dot_ref, don_ref, ot_ref, lse_ref,
             dqt_ref, dk_ref, dv_ref):
        h, i = pl.program_id(0), pl.program_id(1)

        @pl.when(i == 0)
        def _():
            dk_ref[...] = jnp.zeros_like(dk_ref)
            dv_ref[...] = jnp.zeros_like(dv_ref)

        qt, qn, dot, don = qt_ref[...], qn_ref[...], dot_ref[...], don_ref[...]
        lse = lse_ref[...]
        delta = jnp.sum(dot[:HEAD_DIM].astype(F32) * ot_ref[...], axis=0, keepdims=True)
        reach = _fox_reach(qt, fend_ref, kmax_ref, h)

        def alive(j, dq):
            return jnp.max(reach(j) - lse) > SKIP_BELOW

        def step(j, dq, masked):
            ks = _kv_slice(j)
            s = _dot(k_ref[ks, :], qt)
            if masked:
                s = jnp.where(_mask_t(False), s, _NEG)
            p = jnp.exp(s - lse)
            ds = (p * (_dot(v_ref[ks, :], dot) - delta)).astype(BF16)
            dk_ref[ks, :] += _dot(ds, qn)
            dv_ref[ks, :] += _dot(p.astype(BF16), don)
            return dq + _dot(kt_ref[:, ks], ds)

        dqt_ref[...] = _walk_down(i, step, alive, jnp.zeros((LANES, ATT_BQ), F32))

    grid_spec = pltpu.PrefetchScalarGridSpec(
        num_scalar_prefetch=2, grid=(H, S // ATT_BQ),
        in_specs=[_t_block(LANES), _n_block(), _n_full(S), _t_full(LANES, S), _n_full(S),
                  _t_block(LANES), _n_block(), _t_block(HEAD_DIM), _t_block(1)],
        out_specs=(_t_block(LANES), _n_full(S), _n_full(S)))
    return pl.pallas_call(
        body, name="fox_bwd", grid_spec=grid_spec,
        out_shape=(jax.ShapeDtypeStruct((H, LANES, S), F32), jax.ShapeDtypeStruct((H, S, LANES), F32),
                   jax.ShapeDtypeStruct((H, S, LANES), F32)),
        compiler_params=_params(dimension_semantics=("parallel", "arbitrary")),
    )(f_end, k_max, q_t, q_n, k_n, k_t, v_n, do_t, do_n, o_t, lse)


def _scan_lhs():
    r = lax.broadcasted_iota(jnp.int32, (SCAN_BK, 2 * SCAN_BK), 0)
    c = lax.broadcasted_iota(jnp.int32, (SCAN_BK, 2 * SCAN_BK), 1) % SCAN_BK
    return (c >= r).astype(BF16)


def _suffix_sum(t, lhs):
    hi = t.astype(BF16)
    lo = (t - hi.astype(F32)).astype(BF16)
    return _dot(lhs, jnp.concatenate([hi, lo], axis=0))


def _sb_scores(k, qt, mask):
    z = _dot(k, qt)
    e = jnp.exp(-jnp.abs(z))
    lb = -(jnp.maximum(z, 0.0) + jnp.log(1.0 + e))
    if mask is not None:
        lb = jnp.where(mask, lb, 0.0)
    return z, e, lb


def _scan_blocks():
    return [slice(u * SCAN_BK, (u + 1) * SCAN_BK) for u in reversed(range(N_SCAN))]


def _sb_fwd(q_t, k_n, v_t):
    H, _, S = q_t.shape

    def body(qt_ref, k_ref, vt_ref, ot_ref):
        i = pl.program_id(1)
        qt = qt_ref[...]
        lhs = _scan_lhs()

        def step(j, carry, masked):
            c, acc = carry
            ks = _kv_slice(j)
            mask = _mask_t(True) if masked else None
            z, _, lb = _sb_scores(k_ref[ks, :], qt, mask)
            parts = []
            for sl in _scan_blocks():
                rin = _suffix_sum(lb[sl], lhs)
                a = jnp.exp(z[sl] + rin + c)
                if masked:
                    a = jnp.where(mask[sl], a, 0.0)
                parts.append(a.astype(BF16))
                c = c + rin[0:1, :]
            a_all = jnp.concatenate(parts[::-1], axis=0)
            return c, acc + _dot(vt_ref[:, ks], a_all)

        carry = (jnp.zeros((1, ATT_BQ), F32), jnp.zeros((HEAD_DIM, ATT_BQ), F32))
        ot_ref[...] = _walk_down(i, step, lambda j, cr: jnp.max(cr[0]) > SKIP_BELOW, carry)[1]

    return pl.pallas_call(
        body, name="sb_fwd", grid=(H, S // ATT_BQ),
        out_shape=jax.ShapeDtypeStruct((H, HEAD_DIM, S), F32),
        in_specs=[_t_block(LANES), _n_full(S), _t_full(HEAD_DIM, S)],
        out_specs=_t_block(HEAD_DIM),
        compiler_params=_params(dimension_semantics=("parallel", "parallel")),
    )(q_t, k_n, v_t)


def _sb_bwd(q_t, q_n, k_n, k_t, v_n, do_t, do_n, o_t, bound):
    H, _, S = q_t.shape
    n, nq = len(bound), S // ATT_BQ

    def body(qt_ref, qn_ref, k_ref, kt_ref, v_ref, dot_ref, don_ref, ot_ref, *rest):
        ins, (dqt_ref, dk_ref, dv_ref) = rest[:n], rest[n:n + 3]
        outs, sems = rest[n + 3:2 * n + 3], rest[2 * n + 3:]
        h, i = pl.program_id(0), pl.program_id(1)

        @pl.when(jnp.logical_and(h == 0, i == 0))
        def _():
            for cp in _exchange_copies(ins, outs, *sems, True):
                cp.start()

        @pl.when(i == 0)
        def _():
            dk_ref[...] = jnp.zeros_like(dk_ref)
            dv_ref[...] = jnp.zeros_like(dv_ref)

        qt, qn, dot, don = qt_ref[...], qn_ref[...], dot_ref[...], don_ref[...]
        lhs = _scan_lhs()
        delta = jnp.sum(dot[:HEAD_DIM].astype(F32) * ot_ref[...], axis=0, keepdims=True)

        def step(j, carry, masked):
            c, g, dq = carry
            ks = _kv_slice(j)
            mask = _mask_t(True) if masked else None
            z, e, lb = _sb_scores(k_ref[ks, :], qt, mask)
            da = _dot(v_ref[ks, :], dot)
            a_parts, dz_parts = [], []
            for sl in _scan_blocks():
                rin = _suffix_sum(lb[sl], lhs)
                a = jnp.exp(z[sl] + rin + c)
                if masked:
                    a = jnp.where(mask[sl], a, 0.0)
                ab = a.astype(BF16)
                gg = ab.astype(F32) * da[sl]
                rgin = _suffix_sum(gg, lhs)
                rinv = 1.0 / (1.0 + e[sl])
                sig = jnp.where(z[sl] >= 0.0, rinv, e[sl] * rinv)
                dz = gg - sig * (delta - g - (rgin - gg))
                if masked:
                    dz = jnp.where(mask[sl], dz, 0.0)
                a_parts.append(ab)
                dz_parts.append(dz.astype(BF16))
                c = c + rin[0:1, :]
                g = g + rgin[0:1, :]
            ab_all = jnp.concatenate(a_parts[::-1], axis=0)
            dzb = jnp.concatenate(dz_parts[::-1], axis=0)
            dk_ref[ks, :] += _dot(dzb, qn)
            dv_ref[ks, :] += _dot(ab_all, don)
            return c, g, dq + _dot(kt_ref[:, ks], dzb)

        row = jnp.zeros((1, ATT_BQ), F32)
        carry = (row, row, jnp.zeros((LANES, ATT_BQ), F32))
        dqt_ref[...] = _walk_down(i, step, lambda j, cr: jnp.max(cr[0]) > SKIP_BELOW, carry)[2]

        @pl.when(jnp.logical_and(h == H - 1, i == nq - 1))
        def _():
            for cp in _exchange_copies(ins, outs, *sems, True):
                cp.wait()

    any_spec = pl.BlockSpec(memory_space=pl.ANY)
    res = pl.pallas_call(
        body, name="sb_bwd", grid=(H, nq),
        out_shape=tuple([jax.ShapeDtypeStruct((H, LANES, S), F32), jax.ShapeDtypeStruct((H, S, LANES), F32),
                         jax.ShapeDtypeStruct((H, S, LANES), F32)] + _exchange_out_shapes(bound, True)),
        in_specs=[_t_block(LANES), _n_block(), _n_full(S), _t_full(LANES, S), _n_full(S),
                  _t_block(LANES), _n_block(), _t_block(HEAD_DIM)] + [any_spec] * n,
        out_specs=tuple([_t_block(LANES), _n_full(S), _n_full(S)] + [any_spec] * n),
        scratch_shapes=_exchange_sems(n),
        compiler_params=_params(dimension_semantics=("arbitrary", "arbitrary"), has_side_effects=True),
    )(q_t, q_n, k_n, k_t, v_n, do_t, do_n, o_t, *bound)
    return res[0], res[1], res[2], res[3:]


def _dqkv_assemble(dqf_t, dkf, dvf, dqs_t, dks, dvs):
    H, _, S = dqf_t.shape
    n_pairs = H // 2
    tm = _tile(S, 512)

    def body(dqf_ref, dkf_ref, dvf_ref, dqs_ref, dks_ref, dvs_ref, out_ref, dfk_ref):
        lane = lax.broadcasted_iota(jnp.int32, (1, LANES), 1)
        slabs = ((dqf_ref, True), (dkf_ref, False), (dvf_ref, False),
                 (dqs_ref, True), (dks_ref, False), (dvs_ref, False))
        for k, (ref, transposed) in enumerate(slabs):
            if transposed:
                t0, t1 = ref[0].T * _SCALE, ref[1].T * _SCALE
            else:
                t0, t1 = ref[0], ref[1]
            out_ref[k] = jnp.where(lane < HEAD_DIM, t0, pltpu.roll(t1, HEAD_DIM, 1)).astype(BF16)
        for e in range(2):
            dfk_ref[e] = dkf_ref[e].T[Q_ONE_LANE:Q_ONE_LANE + 1, :]

    t_spec = pl.BlockSpec((2, LANES, tm), lambda i, p: (p, 0, i))
    n_spec = pl.BlockSpec((2, tm, LANES), lambda i, p: (p, i, 0))
    return pl.pallas_call(
        body, name="dqkv_assemble", grid=(S // tm, n_pairs),
        out_shape=(jax.ShapeDtypeStruct((6, S, n_pairs * LANES), BF16), jax.ShapeDtypeStruct((H, 1, S), F32)),
        in_specs=[t_spec, n_spec, n_spec, t_spec, n_spec, n_spec],
        out_specs=(pl.BlockSpec((6, tm, LANES), lambda i, p: (0, i, p)),
                   pl.BlockSpec((2, 1, tm), lambda i, p: (p, 0, i))),
        compiler_params=_params(dimension_semantics=("parallel", "parallel")),
    )(dqf_t, dkf, dvf, dqs_t, dks, dvs)


def _acc_spec(w):
    return pl.BlockSpec((1, w), lambda i: (0, 0))


def _loss_head(x1, act, w_down, gate_m, g_final, target):
    S, D = x1.shape
    F = act.shape[1]
    tm = _tile(S, 512)

    def body(x1_ref, act_ref, w_ref, gt_ref, gf_ref, tg_ref, dx2_ref, gm_ref, loss_ref, dgf_ref, dgt_ref):
        @pl.when(pl.program_id(0) == 0)
        def _():
            loss_ref[...] = jnp.zeros_like(loss_ref)
            dgf_ref[...] = jnp.zeros_like(dgf_ref)
            dgt_ref[...] = jnp.zeros_like(dgt_ref)

        mo = _dot(act_ref[...], w_ref[...])
        x2 = x1_ref[...] + gt_ref[...] * mo
        r = lax.rsqrt(jnp.mean(x2 * x2, axis=-1, keepdims=True) + EPS)
        xh = x2 * r
        diff = xh * gf_ref[...] - tg_ref[...]
        loss_ref[...] += (0.5 / D) * jnp.sum(diff * diff)
        dy = diff * (1.0 / D)
        dgf_ref[...] += jnp.sum(dy * xh, axis=0, keepdims=True)
        dxh = dy * gf_ref[...]
        dx2 = r * (dxh - xh * jnp.mean(dxh * xh, axis=-1, keepdims=True))
        dx2_ref[...] = dx2
        gm_ref[...] = (dx2 * gt_ref[...]).astype(BF16)
        dgt_ref[...] += jnp.sum(dx2 * mo, axis=0, keepdims=True)

    return pl.pallas_call(
        body, name="loss_head", grid=(S // tm,),
        out_shape=(jax.ShapeDtypeStruct((S, D), F32), jax.ShapeDtypeStruct((S, D), BF16),
                   jax.ShapeDtypeStruct((1, LANES), F32), jax.ShapeDtypeStruct((1, D), F32),
                   jax.ShapeDtypeStruct((1, D), F32)),
        in_specs=_row_specs(tm, [D, F]) + [pl.BlockSpec((F, D), lambda i: (0, 0))] + [_vec_spec(D)] * 2
        + _row_specs(tm, [D]),
        out_specs=tuple(_row_specs(tm, [D, D]) + [_acc_spec(LANES), _acc_spec(D), _acc_spec(D)]),
        compiler_params=_params(dimension_semantics=("arbitrary",)),
    )(x1, act, w_down, gate_m, g_final, target)


def _norm_bwd(lhs, rhs, xin, dres, g, scale, name, gate=None, branch=None, bound=(), conv=None, tm=256):
    S, D = xin.shape
    tm = _tile(S, tm)
    gated = gate is not None
    nl, nb, n_steps = len(lhs), len(bound), S // tm
    nc_ = nl if conv else 0
    n_out = (6 if gated else 4) + nc_

    def body(*refs):
        l_refs, r_refs, rest = refs[:nl], refs[nl:2 * nl], refs[2 * nl:]
        halo_refs, cw_refs, rest = rest[:nc_], rest[nc_:2 * nc_], rest[2 * nc_:]
        if gated:
            x_ref, dr_ref, g_ref, sc_ref, gt_ref, br_ref = rest[:6]
            rest = rest[6:]
        else:
            x_ref, dr_ref, g_ref, sc_ref = rest[:4]
            rest = rest[4:]
        ins, outs, ex_outs, sems = rest[:nb], rest[nb:nb + n_out], rest[nb + n_out:2 * nb + n_out], rest[2 * nb + n_out:]
        dx_ref, dsc_ref, dsh_ref, dg_ref = outs[:4]
        sums = (dsc_ref, dsh_ref, dg_ref) + ((outs[5],) if gated else ())
        dup_refs = outs[n_out - nc_:]
        i = pl.program_id(0)

        @pl.when(i == 0)
        def _():
            for s_ref in sums:
                s_ref[...] = jnp.zeros_like(s_ref)
            if nb:
                for cp in _chip_exchange_copies(ins, ex_outs, *sems):
                    cp.start()

        dhv = None
        for k, (l_ref, r_ref) in enumerate(zip(l_refs, r_refs)):
            if conv:
                halo = jnp.where(i == n_steps - 1, 0.0, halo_refs[k][...].astype(F32))
                dup = _conv_bwd_taps(l_ref[...].astype(F32), halo, cw_refs[k]).astype(BF16)
                dup_refs[k][...] = dup
                terms = [_dot_nt(dup, r_ref[...])]
            elif len(l_ref.shape) == 3:
                K = l_ref.shape[2]
                terms = [_dot_nt(l_ref[k], r_ref[:, k * K:(k + 1) * K]) for k in range(l_ref.shape[0])]
            else:
                terms = [_dot_nt(l_ref[...], r_ref[...])]
            for t in terms:
                dhv = t if dhv is None else dhv + t
        xv = x_ref[...]
        r = lax.rsqrt(jnp.mean(xv * xv, axis=-1, keepdims=True) + EPS)
        xh = xv * r
        dsc_ref[...] += jnp.sum(dhv * (xh * g_ref[...]), axis=0, keepdims=True)
        dsh_ref[...] += jnp.sum(dhv, axis=0, keepdims=True)
        dn = dhv * (1.0 + sc_ref[...])
        dg_ref[...] += jnp.sum(dn * xh, axis=0, keepdims=True)
        dxh = dn * g_ref[...]
        dx = dr_ref[...] + r * (dxh - xh * jnp.mean(dxh * xh, axis=-1, keepdims=True))
        dx_ref[...] = dx
        if gated:
            outs[4][...] = (dx * gt_ref[...]).astype(BF16)
            outs[5][...] += jnp.sum(dx * br_ref[...], axis=0, keepdims=True)

        if nb:
            @pl.when(i == n_steps - 1)
            def _():
                for cp in _chip_exchange_copies(ins, ex_outs, *sems):
                    cp.wait()

    def l_spec(a):
        if a.ndim == 3:
            return pl.BlockSpec((a.shape[0], tm, a.shape[2]), lambda i: (0, i, 0))
        return pl.BlockSpec((tm, a.shape[1]), lambda i: (i, 0))

    any_spec = pl.BlockSpec(memory_space=pl.ANY)
    vec = jax.ShapeDtypeStruct((1, D), F32)
    out_shape = [jax.ShapeDtypeStruct((S, D), F32), vec, vec, vec]
    out_specs = _row_specs(tm, [D]) + [_acc_spec(D)] * 3
    in_specs = [l_spec(a) for a in lhs] + [pl.BlockSpec(b.shape, lambda i: (0, 0)) for b in rhs]
    args = list(lhs) + list(rhs)
    if conv:
        step, last_halo = tm // HALO, S // HALO - 1
        in_specs += [pl.BlockSpec((HALO, a.shape[1]), lambda i: (jnp.minimum((i + 1) * step, last_halo), 0))
                     for a in lhs]
        in_specs += [pl.BlockSpec(w.shape, lambda i: (0, 0)) for w in conv]
        args += list(lhs) + list(conv)
    in_specs += _row_specs(tm, [D, D]) + [_vec_spec(D)] * 2
    args += [xin, dres, g, scale]
    if gated:
        out_shape += [jax.ShapeDtypeStruct((S, D), BF16), vec]
        out_specs += _row_specs(tm, [D]) + [_acc_spec(D)]
        in_specs += [_vec_spec(D)] + _row_specs(tm, [D])
        args += [gate, branch]
    if conv:
        out_shape += [jax.ShapeDtypeStruct(a.shape, BF16) for a in lhs]
        out_specs += [l_spec(a) for a in lhs]
    res = pl.pallas_call(
        body, name=name, grid=(n_steps,),
        out_shape=tuple(out_shape + [jax.ShapeDtypeStruct(b.shape, b.dtype) for b in bound]),
        in_specs=in_specs + [any_spec] * nb, out_specs=tuple(out_specs + [any_spec] * nb),
        scratch_shapes=_chip_exchange_sems(nb) if nb else [],
        compiler_params=_params(dimension_semantics=("arbitrary",), has_side_effects=bool(nb)),
    )(*args, *bound)
    return tuple(res[:n_out]) + (tuple(res[n_out:]),)


def _headnorm_bwd(dmix, o_f, o_s, g_f, g_s):
    dh, S = o_f.shape
    H = dh // HEAD_DIM
    tm = _tile(S, 256)

    def body(dm_ref, of_ref, os_ref, gf_ref, gs_ref, fn_ref, ft_ref, sn_ref, st_ref, dgf_ref, dgs_ref):
        @pl.when(pl.program_id(0) == 0)
        def _():
            dgf_ref[...] = jnp.zeros_like(dgf_ref)
            dgs_ref[...] = jnp.zeros_like(dgs_ref)

        ones = _group_ones()
        lane = lax.broadcasted_iota(jnp.int32, (1, LANES), 1)
        parts = ((of_ref, gf_ref, fn_ref, ft_ref, dgf_ref), (os_ref, gs_ref, sn_ref, st_ref, dgs_ref))
        for part, (o_ref, g_ref, n_ref, t_ref, dg_ref) in enumerate(parts):
            for t in range(dh // LANES):
                cols = slice(t * LANES, (t + 1) * LANES)
                o = o_ref[cols, :].T
                dm = dm_ref[:, part * dh + t * LANES: part * dh + (t + 1) * LANES]
                r = lax.rsqrt(_group_sum(o * o, ones) * (1.0 / HEAD_DIM) + EPS)
                oh = o * r
                dg_ref[:, cols] += jnp.sum(dm * oh, axis=0, keepdims=True)
                dn = dm * g_ref[:, cols]
                mean = _group_sum(dn * oh, ones) * (1.0 / HEAD_DIM)
                do = r * (dn - oh * mean)
                for e in range(2):
                    d = do if e == 0 else pltpu.roll(do, HEAD_DIM, 1)
                    d = jnp.where(lane < HEAD_DIM, d, 0.0)
                    n_ref[2 * t + e] = d.astype(BF16)
                    t_ref[2 * t + e] = d.T.astype(BF16)

    vec = jax.ShapeDtypeStruct((1, dh), F32)
    n_sds = jax.ShapeDtypeStruct((H, S, LANES), BF16)
    t_sds = jax.ShapeDtypeStruct((H, LANES, S), BF16)
    n_spec = pl.BlockSpec((H, tm, LANES), lambda i: (0, i, 0))
    t_spec = pl.BlockSpec((H, LANES, tm), lambda i: (0, 0, i))
    return pl.pallas_call(
        body, name="headnorm_bwd", grid=(S // tm,),
        out_shape=(n_sds, t_sds, n_sds, t_sds, vec, vec),
        in_specs=_row_specs(tm, [2 * dh]) + [_col_spec(tm, dh)] * 2 + [_vec_spec(dh)] * 2,
        out_specs=(n_spec, t_spec, n_spec, t_spec, _acc_spec(dh), _acc_spec(dh)),
        compiler_params=_params(dimension_semantics=("arbitrary",)),
    )(dmix, o_f, o_s, g_f, g_s)


def _adamw(w, gslots, m, v, name):
    R, C = w.shape
    n = gslots.shape[0]
    tr = 256 if (R % 256 == 0 and R > 256) else R
    bc1 = 1.0 - ADAM_B1 ** ADAM_STEP
    bc2 = 1.0 - ADAM_B2 ** ADAM_STEP

    def body(w_ref, gs_ref, m_ref, v_ref, g_ref, d_ref, nm_ref, nv_ref):
        g = gs_ref[0]
        for s in range(1, n):
            g = g + gs_ref[s]
        nm = ADAM_B1 * m_ref[...] + (1.0 - ADAM_B1) * g
        nv = ADAM_B2 * v_ref[...] + (1.0 - ADAM_B2) * (g * g)
        g_ref[...] = g
        nm_ref[...] = nm
        nv_ref[...] = nv
        d_ref[...] = -ADAM_LR * ((nm / bc1) / (jnp.sqrt(nv / bc2) + ADAM_EPS) + ADAM_WD * w_ref[...])

    blk = pl.BlockSpec((tr, C), lambda i: (i, 0))
    sds = jax.ShapeDtypeStruct((R, C), F32)
    return pl.pallas_call(
        body, name=name, grid=(R // tr,), out_shape=(sds,) * 4,
        in_specs=[blk, pl.BlockSpec((n, tr, C), lambda i: (0, i, 0)), blk, blk], out_specs=(blk,) * 4,
        compiler_params=_params(dimension_semantics=("parallel",)),
    )(w, gslots, m, v)


def _slot_sum(slots, name):
    n, _, C = slots.shape

    def body(s_ref, o_ref):
        acc = s_ref[0]
        for s in range(1, n):
            acc = acc + s_ref[s]
        o_ref[...] = acc

    return pl.pallas_call(body, name=name, out_shape=jax.ShapeDtypeStruct((1, C), F32),
                          compiler_params=_params())(slots)


def _pad_cols(a, n):
    return jnp.pad(a, ((0, 0), (0, n - a.shape[1])))


def _ungather(g, axis):
    if axis == 0:
        return g.reshape(g.shape[0] * g.shape[1], g.shape[2])
    return jnp.transpose(g, (1, 0, 2)).reshape(g.shape[1], g.shape[0] * g.shape[2])


def _to_slots(full, axis):
    R, C = full.shape
    if axis == 0:
        return full.reshape(N_DEV, R // N_DEV, C)
    return jnp.transpose(full.reshape(R, N_DEV, C // N_DEV), (1, 0, 2))


def kernel(x, c, w_ada, b_ada, g_attn, w_in, b_fgate, g_out_fox, g_out_sb, w_out, g_mlp, w_up, conv_w, conv_b, w_down, g_final, loss_target, m_w_ada, m_b_ada, m_g_attn, m_w_in, m_b_fgate, m_g_out_fox, m_g_out_sb, m_w_out, m_g_mlp, m_w_up, m_conv_w, m_conv_b, m_w_down, m_g_final, v_w_ada, v_b_ada, v_g_attn, v_w_in, v_b_fgate, v_g_out_fox, v_g_out_sb, v_w_out, v_g_mlp, v_w_up, v_conv_w, v_conv_b, v_w_down, v_g_final):
    S, D = x.shape[1], x.shape[2]
    dh = D // 2
    n_heads = dh // HEAD_DIM
    n_qkv = 6 * dh
    ff = w_down.shape[1] * N_DEV
    ffp = -(-ff // (2 * LANES)) * (2 * LANES)
    nc = S // LANES
    me = 4 * lax.axis_index("x") + 2 * lax.axis_index("y") + lax.axis_index("c")
    xs, tgt = x[0], loss_target[0]

    c_all, win_g = _gather_two_level([c, w_in[0].astype(BF16)], name="gather_first")
    c_all = c_all.reshape(N_DEV, D)
    W_in = _ungather(win_g, 1)
    W_qkv, W_f = W_in[:, :n_qkv], _pad_cols(W_in[:, n_qkv:], LANES)
    cb_g, cb_v = _pad_cols(conv_b[:, :ff], ffp), _pad_cols(conv_b[:, ff:], ffp)

    n_ada = w_ada.shape[2]
    b_shard = lax.dynamic_slice(b_ada, (0, me * n_ada), (1, n_ada))
    mod_cols = _ada_fwd(c_all, w_ada[0], b_shard)
    (mod_g,) = _exchange([mod_cols], scatter=False, name="gather_mod")
    mod = lax.dynamic_index_in_dim(mod_g, me, axis=1, keepdims=False).reshape(6, 1, D)
    shift_a, scale_a, gate_a, shift_m, scale_m, gate_m = [mod[k] for k in range(6)]

    h1, h1_t = _prenorm(xs, g_attn, scale_a, shift_a, "prenorm_attn")
    qkv = _mm(h1, W_qkv, BF16, "proj_qkv")
    flog = _mm(h1, W_f, F32, "proj_fgate")
    zf = flog[:, :n_heads] + b_fgate
    z_rows = zf.T.reshape(n_heads * nc, LANES)
    f_rows = _fgate_fwd(z_rows, nc).reshape(n_heads, S)
    f_pairs = jnp.transpose(f_rows.reshape(n_heads // 2, 2, S), (0, 2, 1))
    fox, sb = _att_prep(qkv, f_pairs)
    f_end, k_max = _skip_bounds(qkv[:, dh:2 * dh], f_rows)
    of_t, lse, (wout_g, wup_g, wdown_g, convw_g) = _fox_fwd(
        fox["q_t"], fox["k_n"], fox["v_t"], f_end, k_max,
        [w_out[0].astype(BF16), w_up[0].astype(BF16), w_down[0].astype(BF16), conv_w[0]])
    W_out = _ungather(wout_g, 0)
    W_up = _ungather(wup_g, 1)
    W_g, W_v = _pad_cols(W_up[:, :ff], ffp), _pad_cols(W_up[:, ff:], ffp)
    W_down = jnp.pad(_ungather(wdown_g, 0), ((0, ffp - ff), (0, 0)))
    cw_full = _ungather(convw_g, 1)
    cw_g, cw_v = _pad_cols(cw_full[:, :ff], ffp), _pad_cols(cw_full[:, ff:], ffp)
    os_t = _sb_fwd(sb["q_t"], sb["k_n"], sb["v_t"])
    o_f, o_s = of_t.reshape(dh, S), os_t.reshape(dh, S)
    mix, mix_t = _headnorm_fwd(o_f, o_s, g_out_fox, g_out_sb)
    a_out = _mm(mix, W_out, F32, "proj_out")
    x1, h2, h2_t = _resid_prenorm(xs, a_out, gate_a, g_mlp, scale_m, shift_m)
    up_g, up_v, act, act_t = _mlp_up(h2, W_g, W_v, cw_g, cw_v, cb_g, cb_v)

    dx2, gm, loss_p, dg_final, dgate_m = _loss_head(x1, act, W_down, gate_m, g_final.reshape(1, D), tgt)
    dW_down = _mm_acc(act_t, gm, "bwd_down_w")
    du_g, du_v, p_g, p_v = _conv_act_bwd(gm, W_down, up_g, up_v, cw_g, cw_v, cb_g, cb_v)
    dx1, dscale_m, dshift_m, dg_mlp, ga, dgate_a, dup_g, dup_v, _ = _norm_bwd(
        [du_g, du_v], [W_g, W_v], x1, dx2, g_mlp, scale_m, "norm_mlp_bwd", gate=gate_a, branch=a_out,
        conv=[cw_g, cw_v])
    dW_g = _mm_acc(h2_t, dup_g, "bwd_up_w_g")
    dW_v = _mm_acc(h2_t, dup_v, "bwd_up_w_v")
    dmix = _mm(ga, W_out, F32, "bwd_out_act", nt=True)
    dW_out = _mm_acc(mix_t, ga, "bwd_out_w")
    dof_n, dof_t, dos_n, dos_t, dg_fox, dg_sb = _headnorm_bwd(dmix, o_f, o_s, g_out_fox, g_out_sb)
    dqf_t, dkf, dvf = _fox_bwd(fox["q_t"], fox["q_n"], fox["k_n"], fox["k_t"], fox["v_n"], dof_t, dof_n, of_t, lse,
                              f_end, k_max)
    dW_upf = jnp.concatenate([dW_g[:, :ff], dW_v[:, :ff]], axis=1)
    dcw = jnp.concatenate([p_g[:CONV_W, :ff], p_v[:CONV_W, :ff]], axis=1)
    dqs_t, dks, dvs, (s_out, s_up, s_down, s_cw) = _sb_bwd(
        sb["q_t"], sb["q_n"], sb["k_n"], sb["k_t"], sb["v_n"], dos_t, dos_n, os_t,
        [_to_slots(dW_out, 0), _to_slots(dW_upf, 1), _to_slots(dW_down[:ff], 0), _to_slots(dcw, 1)])
    dparts, dfk = _dqkv_assemble(dqf_t, dkf, dvf, dqs_t, dks, dvs)
    dz_rows, db_fgate = _fgate_bwd(dfk.reshape(n_heads * nc, LANES),
                                   dqf_t[:, Q_F_LANE, :].reshape(n_heads * nc, LANES), z_rows, nc)
    dzf = _pad_cols(dz_rows.reshape(n_heads, S).T, LANES).astype(BF16)
    dW_qkv = _mm_acc_parts(h1_t, dparts, "bwd_in_w")
    dW_f = _mm_acc(h1_t, dzf, "bwd_in_w_fgate")
    dW_in = jnp.concatenate([jnp.transpose(dW_qkv, (1, 0, 2)).reshape(D, n_qkv), dW_f[:, :n_heads]], axis=1)
    bound_in = _to_slots(dW_in, 1)
    bound_in = bound_in.reshape((N_CHIPS, 2) + bound_in.shape[1:])
    (got_in,) = _scatter_to_sibling([bound_in], "scatter_sibling")
    c_idx = lax.axis_index("c").astype(jnp.int32).reshape(1)
    grad_x, dscale_a, dshift_a, dg_attn, (s_in,) = _norm_bwd(
        [dparts, dzf], [W_qkv, W_f], xs, dx1, g_attn, scale_a, "norm_attn_bwd",
        bound=[_pair_add(bound_in, got_in, c_idx, "pair_add")], tm=512)

    dconv_b = jnp.concatenate([p_g[CONV_W:CONV_W + 1, :ff], p_v[CONV_W:CONV_W + 1, :ff]], axis=1)
    parts = [dshift_a, dscale_a, dgate_a, dshift_m, dscale_m, dgate_m,
             dg_attn, db_fgate.reshape(1, n_heads), dg_fox, dg_sb, dg_mlp, dconv_b, dg_final,
             loss_p[:, :1]]
    sizes = [p.shape[1] for p in parts]
    vec = jnp.concatenate(parts, axis=1)
    n_vec = -(-vec.shape[1] // LANES) * LANES
    vec = _pad_cols(vec, n_vec)
    (vec_g,) = _exchange([vec], scatter=False, name="gather_small")
    offs = [0]
    for s in sizes:
        offs.append(offs[-1] + s)

    def small(k0, k1=None):
        k1 = k0 if k1 is None else k1
        return vec_g[:, :, offs[k0]:offs[k1 + 1]]

    dmod_all = small(0, 5).reshape(N_DEV, 6 * D)
    dmod_cols = lax.dynamic_slice(dmod_all, (0, me * n_ada), (N_DEV, n_ada))
    dW_ada = _ada_bwd(c_all.T, dmod_cols)


    res = {}
    res["w_ada"] = _adamw(w_ada[0], dW_ada[None], m_w_ada[0], v_w_ada[0], "adamw_w_ada")
    res["w_in"] = _adamw(w_in[0], s_in, m_w_in[0], v_w_in[0], "adamw_w_in")
    res["w_out"] = _adamw(w_out[0], s_out, m_w_out[0], v_w_out[0], "adamw_w_out")
    res["w_up"] = _adamw(w_up[0], s_up, m_w_up[0], v_w_up[0], "adamw_w_up")
    res["w_down"] = _adamw(w_down[0], s_down, m_w_down[0], v_w_down[0], "adamw_w_down")
    res["conv_w"] = _adamw(conv_w[0], s_cw, m_conv_w[0], v_conv_w[0], "adamw_conv_w")
    small_names = ["b_ada", "g_attn", "b_fgate", "g_out_fox", "g_out_sb", "g_mlp", "conv_b", "g_final"]
    small_w = [b_ada, g_attn, b_fgate, g_out_fox, g_out_sb, g_mlp, conv_b, g_final.reshape(1, D)]
    small_m = [m_b_ada, m_g_attn, m_b_fgate, m_g_out_fox, m_g_out_sb, m_g_mlp, m_conv_b, m_g_final.reshape(1, D)]
    small_v = [v_b_ada, v_g_attn, v_b_fgate, v_g_out_fox, v_g_out_sb, v_g_mlp, v_conv_b, v_g_final.reshape(1, D)]
    small_res = _adamw(jnp.concatenate(small_w, axis=1), small(0, 12), jnp.concatenate(small_m, axis=1),
                       jnp.concatenate(small_v, axis=1), "adamw_small")
    lo = 0
    for nm, wv in zip(small_names, small_w):
        res[nm] = tuple(r[:, lo:lo + wv.shape[1]] for r in small_res)
        lo += wv.shape[1]
    loss = _slot_sum(_pad_cols(small(13).reshape(N_DEV, 1), LANES).reshape(N_DEV, 1, LANES), "loss_sum")[0, 0]

    names = ["w_ada", "b_ada", "g_attn", "w_in", "b_fgate", "g_out_fox", "g_out_sb", "w_out", "g_mlp",
             "w_up", "conv_w", "conv_b", "w_down", "g_final"]

    def shaped(n, a):
        if n == "g_final":
            return a.reshape(D)
        if n in ("b_ada", "g_attn", "b_fgate", "g_out_fox", "g_out_sb", "g_mlp", "conv_b"):
            return a
        return a[None]

    outs = [loss, grad_x[None]]
    for k in range(4):
        outs += [shaped(n, res[n][k]) for n in names]
    return tuple(outs)
```

```python
import jax
import jax.numpy as jnp
from jax import lax
from jax.experimental import pallas as pl
from jax.experimental.pallas import tpu as pltpu

F32 = jnp.float32
BF16 = jnp.bfloat16
HIGHEST = lax.Precision.HIGHEST

N_DEV = 8
LANES = 128
HEAD_DIM = 64
EPS = 1e-6
CONV_W = 3
CONV_COLS = 1408
HALO = 16
ATT_BQ = 512
ATT_BK = 512
SCAN_BK = 128
VMEM_LIMIT = 56 * 1024 * 1024

ADAM_LR = 0.001
ADAM_B1 = 0.9
ADAM_B2 = 0.999
ADAM_EPS = 1e-08
ADAM_WD = 0.01
ADAM_STEP = 10


def _params(**kw):
    return pltpu.CompilerParams(vmem_limit_bytes=VMEM_LIMIT, **kw)


def _tile(n, cap):
    if n <= cap:
        return n
    best = None
    for t in range(LANES, cap + 1, LANES):
        if n % t == 0:
            best = t
    assert best is not None, (n, cap)
    return best


def _dot(a, b, **kw):
    return jnp.dot(a, b, preferred_element_type=F32, **kw)


def _exchange_copies(ins, outs, send_sems, recv_sems, loc_sems, scatter):
    n = len(ins)
    if n == 0:
        return []
    x, y, c = lax.axis_index("x"), lax.axis_index("y"), lax.axis_index("c")
    me = 4 * x + 2 * y + c
    copies = []
    for a in range(n):
        src = ins[a].at[me] if scatter else ins[a]
        copies.append(pltpu.make_async_copy(src, outs[a].at[me], loc_sems.at[a]))
    for k in range(1, N_DEV):
        px = 1 - x if k & 4 else x
        py = 1 - y if k & 2 else y
        pc = 1 - c if k & 1 else c
        peer = 4 * px + 2 * py + pc
        for a in range(n):
            src = ins[a].at[peer] if scatter else ins[a]
            copies.append(pltpu.make_async_remote_copy(
                src_ref=src, dst_ref=outs[a].at[me],
                send_sem=send_sems.at[a, k - 1], recv_sem=recv_sems.at[a, k - 1],
                device_id=(px, py, pc), device_id_type=pl.DeviceIdType.MESH))
    return copies


def _exchange_out_shapes(arrays, scatter):
    return [jax.ShapeDtypeStruct((N_DEV,) + tuple(a.shape[1:] if scatter else a.shape), a.dtype) for a in arrays]


def _exchange_sems(n):
    return [pltpu.SemaphoreType.DMA((n, N_DEV - 1)), pltpu.SemaphoreType.DMA((n, N_DEV - 1)),
            pltpu.SemaphoreType.DMA((n,))]


def _exchange(arrays, scatter, name):
    n = len(arrays)

    def body(*refs):
        copies = _exchange_copies(refs[:n], refs[n:2 * n], *refs[2 * n:], scatter)
        for cp in copies:
            cp.start()
        for cp in copies:
            cp.wait()

    any_spec = pl.BlockSpec(memory_space=pl.ANY)
    return pl.pallas_call(
        body, name=name, out_shape=tuple(_exchange_out_shapes(arrays, scatter)),
        in_specs=[any_spec] * n, out_specs=tuple([any_spec] * n),
        scratch_shapes=_exchange_sems(n),
        compiler_params=pltpu.CompilerParams(has_side_effects=True),
    )(*arrays)


def _gather_two_level(arrays, name):
    n = len(arrays)
    out_shape = [jax.ShapeDtypeStruct((N_DEV,) + tuple(a.shape), a.dtype) for a in arrays]

    def body(*refs):
        ins, outs = refs[:n], refs[n:2 * n]
        send_sems, recv_sems, loc_sems = refs[2 * n:]
        x, y, c = lax.axis_index("x"), lax.axis_index("y"), lax.axis_index("c")
        me, sibling = (x, y, c), (x, y, 1 - c)
        chips = [(1 - x, y), (x, 1 - y), (1 - x, 1 - y)]

        def slot(px, py, pc):
            return 4 * px + 2 * py + pc

        def copy(a, k, block, to, src=None):
            dst = outs[a].at[slot(*block)]
            return pltpu.make_async_remote_copy(
                src_ref=dst if src is None else src, dst_ref=dst,
                send_sem=send_sems.at[a, k], recv_sem=recv_sems.at[a, k],
                device_id=to, device_id_type=pl.DeviceIdType.MESH)

        local = [pltpu.make_async_copy(ins[a], outs[a].at[slot(*me)], loc_sems.at[a]) for a in range(n)]
        for cp in local:
            cp.start()
        first = []
        for a in range(n):
            first.append(copy(a, 0, me, sibling, src=ins[a]))
            first += [copy(a, 1 + j, me, (*chip, c), src=ins[a]) for j, chip in enumerate(chips)]
        for cp in first:
            cp.start()
        passed = []
        for j, chip in enumerate(chips):
            for a in range(n):
                copy(a, 1 + j, (*chip, c), me).wait_recv()
                cp = copy(a, 4 + j, (*chip, c), sibling)
                cp.start()
                passed.append(cp)
        for a in range(n):
            copy(a, 0, sibling, me).wait_recv()
            for j, chip in enumerate(chips):
                copy(a, 4 + j, (*chip, 1 - c), me).wait_recv()
        for cp in first + passed:
            cp.wait_send()
        for cp in local:
            cp.wait()

    any_spec = pl.BlockSpec(memory_space=pl.ANY)
    return pl.pallas_call(
        body, name=name, out_shape=tuple(out_shape),
        in_specs=[any_spec] * n, out_specs=tuple([any_spec] * n),
        scratch_shapes=[pltpu.SemaphoreType.DMA((n, N_DEV - 1)), pltpu.SemaphoreType.DMA((n, N_DEV - 1)),
                        pltpu.SemaphoreType.DMA((n,))],
        compiler_params=pltpu.CompilerParams(has_side_effects=True),
    )(*arrays)


N_CHIPS = 4


def _scatter_to_sibling(arrays, name):
    n = len(arrays)
    out_shape = [jax.ShapeDtypeStruct((N_CHIPS,) + tuple(a.shape[2:]), a.dtype) for a in arrays]

    def body(*refs):
        ins, outs = refs[:n], refs[n:2 * n]
        send_sems, recv_sems = refs[2 * n:]
        x, y, c = lax.axis_index("x"), lax.axis_index("y"), lax.axis_index("c")
        copies = []
        for a in range(n):
            for q in range(N_CHIPS):
                cp = pltpu.make_async_remote_copy(
                    src_ref=ins[a].at[q, 1 - c], dst_ref=outs[a].at[q],
                    send_sem=send_sems.at[a, q], recv_sem=recv_sems.at[a, q],
                    device_id=(x, y, 1 - c), device_id_type=pl.DeviceIdType.MESH)
                cp.start()
                copies.append(cp)
        for cp in copies:
            cp.wait()

    any_spec = pl.BlockSpec(memory_space=pl.ANY)
    return pl.pallas_call(
        body, name=name, out_shape=tuple(out_shape),
        in_specs=[any_spec] * n, out_specs=tuple([any_spec] * n),
        scratch_shapes=[pltpu.SemaphoreType.DMA((n, N_CHIPS)), pltpu.SemaphoreType.DMA((n, N_CHIPS))],
        compiler_params=pltpu.CompilerParams(has_side_effects=True),
    )(*arrays)


def _pair_add(mine, got, c_idx, name):
    _, _, R, C = mine.shape
    tr = 256 if (R % 256 == 0 and R > 256) else R

    def body(c_ref, m_ref, g_ref, o_ref):
        o_ref[...] = m_ref[...] + g_ref[...]

    grid_spec = pltpu.PrefetchScalarGridSpec(
        num_scalar_prefetch=1, grid=(N_CHIPS, R // tr),
        in_specs=[pl.BlockSpec((None, None, tr, C), lambda q, i, c_ref: (q, c_ref[0], i, 0)),
                  pl.BlockSpec((None, tr, C), lambda q, i, c_ref: (q, i, 0))],
        out_specs=pl.BlockSpec((None, tr, C), lambda q, i, c_ref: (q, i, 0)))
    return pl.pallas_call(
        body, name=name, grid_spec=grid_spec, out_shape=jax.ShapeDtypeStruct((N_CHIPS, R, C), mine.dtype),
        compiler_params=_params(dimension_semantics=("parallel", "parallel")),
    )(c_idx, mine, got)


def _chip_exchange_copies(ins, outs, send_sems, recv_sems, loc_sems):
    n = len(ins)
    x, y, c = lax.axis_index("x"), lax.axis_index("y"), lax.axis_index("c")
    myq = 2 * x + y
    copies = [pltpu.make_async_copy(ins[a].at[myq], outs[a].at[myq], loc_sems.at[a]) for a in range(n)]
    for k in range(1, N_CHIPS):
        qx = 1 - x if k & 2 else x
        qy = 1 - y if k & 1 else y
        for a in range(n):
            copies.append(pltpu.make_async_remote_copy(
                src_ref=ins[a].at[2 * qx + qy], dst_ref=outs[a].at[myq],
                send_sem=send_sems.at[a, k - 1], recv_sem=recv_sems.at[a, k - 1],
                device_id=(qx, qy, c), device_id_type=pl.DeviceIdType.MESH))
    return copies


def _chip_exchange_sems(n):
    return [pltpu.SemaphoreType.DMA((n, N_CHIPS - 1)), pltpu.SemaphoreType.DMA((n, N_CHIPS - 1)),
            pltpu.SemaphoreType.DMA((n,))]


def _dot_nt(a, b):
    return lax.dot_general(a, b, (((1,), (1,)), ((), ())), preferred_element_type=F32)


def _rhs_spec(b, tn, nt):
    if nt:
        return pl.BlockSpec((tn, b.shape[1]), lambda i, j: (j, 0))
    return pl.BlockSpec((b.shape[0], tn), lambda i, j: (0, j))


def _mm(a, b, out_dtype, name, tm=1024, tn=512, nt=False):
    M, K = a.shape
    N = b.shape[0] if nt else b.shape[1]
    tm, tn = _tile(M, tm), _tile(N, tn)
    dot = _dot_nt if nt else _dot

    def body(a_ref, b_ref, o_ref):
        o_ref[...] = dot(a_ref[...], b_ref[...]).astype(out_dtype)

    return pl.pallas_call(
        body, name=name, out_shape=jax.ShapeDtypeStruct((M, N), out_dtype),
        grid=(M // tm, N // tn),
        in_specs=[pl.BlockSpec((tm, K), lambda i, j: (i, 0)), _rhs_spec(b, tn, nt)],
        out_specs=pl.BlockSpec((tm, tn), lambda i, j: (i, j)),
        compiler_params=_params(dimension_semantics=("parallel", "parallel")),
    )(a, b)


def _mm_acc(a, b, name, tm=1408, tn=1408, tk=1024):
    M, S = a.shape
    _, N = b.shape
    tm, tn, tk = _tile(M, tm), _tile(N, tn), _tile(S, tk)

    def body(a_ref, b_ref, o_ref):
        @pl.when(pl.program_id(2) == 0)
        def _():
            o_ref[...] = jnp.zeros_like(o_ref)

        o_ref[...] += _dot(a_ref[...], b_ref[...])

    return pl.pallas_call(
        body, name=name, out_shape=jax.ShapeDtypeStruct((M, N), F32),
        grid=(M // tm, N // tn, S // tk),
        in_specs=[pl.BlockSpec((tm, tk), lambda i, j, k: (i, k)), pl.BlockSpec((tk, tn), lambda i, j, k: (k, j))],
        out_specs=pl.BlockSpec((tm, tn), lambda i, j, k: (i, j)),
        compiler_params=_params(dimension_semantics=("parallel", "parallel", "arbitrary")),
    )(a, b)


def _mm_acc_parts(a, parts, name, tm=1024, tk=1024):
    M, S = a.shape
    P, _, K = parts.shape
    tm, tk = _tile(M, tm), _tile(S, tk)

    def body(a_ref, b_ref, o_ref):
        @pl.when(pl.program_id(2) == 0)
        def _():
            o_ref[...] = jnp.zeros_like(o_ref)

        o_ref[...] += _dot(a_ref[...], b_ref[...])

    return pl.pallas_call(
        body, name=name, out_shape=jax.ShapeDtypeStruct((P, M, K), F32), grid=(P, M // tm, S // tk),
        in_specs=[pl.BlockSpec((tm, tk), lambda k, i, s: (i, s)), pl.BlockSpec((None, tk, K), lambda k, i, s: (k, s, 0))],
        out_specs=pl.BlockSpec((None, tm, K), lambda k, i, s: (k, i, 0)),
        compiler_params=_params(dimension_semantics=("parallel", "parallel", "arbitrary")),
    )(a, parts)


def _silu(z):
    return z * (1.0 / (1.0 + jnp.exp(-z)))


def _ada_fwd(c_all, w_shard, b_shard):
    n = w_shard.shape[1]

    def body(c_ref, w_ref, b_ref, o_ref):
        o_ref[...] = _dot(_silu(c_ref[...]), w_ref[...], precision=HIGHEST) + b_ref[...]

    return pl.pallas_call(body, name="ada_fwd", out_shape=jax.ShapeDtypeStruct((N_DEV, n), F32),
                          compiler_params=_params())(c_all, w_shard, b_shard)


def _ada_bwd(c_all_t, dmod_cols):
    D = c_all_t.shape[0]
    n = dmod_cols.shape[1]

    def body(ct_ref, dm_ref, o_ref):
        sc = _silu(ct_ref[...])
        dm = dm_ref[...]
        acc = sc[:, 0:1] * dm[0:1, :]
        for b in range(1, N_DEV):
            acc = acc + sc[:, b:b + 1] * dm[b:b + 1, :]
        o_ref[...] = acc

    return pl.pallas_call(body, name="ada_bwd", out_shape=jax.ShapeDtypeStruct((D, n), F32),
                          compiler_params=_params())(c_all_t, dmod_cols)


def _row_specs(tm, widths):
    return [pl.BlockSpec((tm, w), lambda i: (i, 0)) for w in widths]


def _vec_spec(w):
    return pl.BlockSpec((1, w), lambda i: (0, 0))


def _col_spec(tm, w):
    return pl.BlockSpec((w, tm), lambda i: (0, i))


def _prenorm(x, g, scale, shift, name):
    S, D = x.shape
    tm = _tile(S, 512)

    def body(x_ref, g_ref, sc_ref, sh_ref, h_ref, ht_ref):
        xv = x_ref[...]
        r = lax.rsqrt(jnp.mean(xv * xv, axis=-1, keepdims=True) + EPS)
        h = (xv * r) * g_ref[...] * (1.0 + sc_ref[...]) + sh_ref[...]
        h_ref[...] = h.astype(BF16)
        ht_ref[...] = h.T.astype(BF16)

    return pl.pallas_call(
        body, name=name, grid=(S // tm,),
        out_shape=(jax.ShapeDtypeStruct((S, D), BF16), jax.ShapeDtypeStruct((D, S), BF16)),
        in_specs=_row_specs(tm, [D]) + [_vec_spec(D)] * 3,
        out_specs=(_row_specs(tm, [D])[0], _col_spec(tm, D)),
        compiler_params=_params(dimension_semantics=("parallel",)),
    )(x, g, scale, shift)


def _group_ones():
    r = (lax.broadcasted_iota(jnp.int32, (2 * LANES, LANES), 0) % LANES) // HEAD_DIM
    c = lax.broadcasted_iota(jnp.int32, (2 * LANES, LANES), 1) // HEAD_DIM
    return (r == c).astype(BF16)


def _group_sum(t, ones):
    hi = t.astype(BF16)
    lo = (t - hi.astype(F32)).astype(BF16)
    return _dot(jnp.concatenate([hi, lo], axis=1), ones)


def _headnorm_fwd(o_f, o_s, g_f, g_s):
    dh, S = o_f.shape
    tm = _tile(S, 512)

    def body(of_ref, os_ref, gf_ref, gs_ref, mix_ref, mixt_ref):
        ones = _group_ones()
        for part, (o_ref, g_ref) in enumerate(((of_ref, gf_ref), (os_ref, gs_ref))):
            for t in range(dh // LANES):
                cols = slice(t * LANES, (t + 1) * LANES)
                out = slice(part * dh + t * LANES, part * dh + (t + 1) * LANES)
                o = o_ref[cols, :].T
                ms = _group_sum(o * o, ones) * (1.0 / HEAD_DIM)
                mixn = o * lax.rsqrt(ms + EPS) * g_ref[:, cols]
                mix_ref[:, out] = mixn.astype(BF16)
                mixt_ref[out, :] = mixn.T.astype(BF16)

    return pl.pallas_call(
        body, name="headnorm_fwd", grid=(S // tm,),
        out_shape=(jax.ShapeDtypeStruct((S, 2 * dh), BF16), jax.ShapeDtypeStruct((2 * dh, S), BF16)),
        in_specs=[_col_spec(tm, dh)] * 2 + [_vec_spec(dh)] * 2,
        out_specs=(_row_specs(tm, [2 * dh])[0], _col_spec(tm, 2 * dh)),
        compiler_params=_params(dimension_semantics=("parallel",)),
    )(o_f, o_s, g_f, g_s)


def _resid_prenorm(x, a_out, gate, g, scale, shift):
    S, D = x.shape
    tm = _tile(S, 512)

    def body(x_ref, a_ref, gt_ref, g_ref, sc_ref, sh_ref, x1_ref, h_ref, ht_ref):
        x1 = x_ref[...] + gt_ref[...] * a_ref[...]
        x1_ref[...] = x1
        r = lax.rsqrt(jnp.mean(x1 * x1, axis=-1, keepdims=True) + EPS)
        h = (x1 * r) * g_ref[...] * (1.0 + sc_ref[...]) + sh_ref[...]
        h_ref[...] = h.astype(BF16)
        ht_ref[...] = h.T.astype(BF16)

    return pl.pallas_call(
        body, name="resid_prenorm", grid=(S // tm,),
        out_shape=(jax.ShapeDtypeStruct((S, D), F32), jax.ShapeDtypeStruct((S, D), BF16),
                   jax.ShapeDtypeStruct((D, S), BF16)),
        in_specs=_row_specs(tm, [D, D]) + [_vec_spec(D)] * 4,
        out_specs=tuple(_row_specs(tm, [D, D]) + [_col_spec(tm, D)]),
        compiler_params=_params(dimension_semantics=("parallel",)),
    )(x, a_out, gate, g, scale, shift)


def _shift_down(main, halo, k):
    ext = jnp.concatenate([halo, main], axis=0)
    return pltpu.roll(ext, k, 0)[halo.shape[0]:]


def _shift_up(main, halo, k):
    ext = jnp.concatenate([main, halo], axis=0)
    n = ext.shape[0]
    return pltpu.roll(ext, n - k, 0)[:main.shape[0]]


def _conv(up, up_halo, w_ref, b_ref):
    return (w_ref[2:3, :] * up + w_ref[1:2, :] * _shift_down(up, up_halo, 1)
            + w_ref[0:1, :] * _shift_down(up, up_halo, 2) + b_ref[...])


def _prev_halo_map(tm):
    step = tm // HALO
    return lambda j, i: (jnp.maximum(i * step - 1, 0), j)


MLP_TM = 512
MLP_CT = 1408
CARRY = 8


def _mlp_up(h, wg, wv, cwg, cwv, cbg, cbv):
    S, D = h.shape
    F = wg.shape[1]
    tm, ct = _tile(S, MLP_TM), _tile(F, MLP_CT)
    nct = F // ct

    def body(h_ref, wg_ref, wv_ref, cwg_ref, cwv_ref, cbg_ref, cbv_ref,
             upg_ref, upv_ref, act_ref, actt_ref, hg_scr, hv_scr):
        i, j = pl.program_id(0), pl.program_id(1)
        hv = h_ref[...]
        us = []
        for w_ref, cw_ref, cb_ref, up_ref, scr in ((wg_ref, cwg_ref, cbg_ref, upg_ref, hg_scr),
                                                   (wv_ref, cwv_ref, cbv_ref, upv_ref, hv_scr)):
            up = _dot(hv, w_ref[...]).astype(BF16)
            up_ref[...] = up
            upf = up.astype(F32)
            halo = jnp.where(i == 0, 0.0, scr[j])
            us.append(_conv(upf, halo, cw_ref, cb_ref))
            scr[j] = upf[tm - CARRY:, :]
        act = _silu(us[0]) * us[1]
        act_ref[...] = act.astype(BF16)
        actt_ref[...] = act.T.astype(BF16)

    blk = pl.BlockSpec((tm, ct), lambda i, j: (i, j))
    wspec = pl.BlockSpec((D, ct), lambda i, j: (0, j))
    cwspec = pl.BlockSpec((CONV_W, ct), lambda i, j: (0, j))
    cbspec = pl.BlockSpec((1, ct), lambda i, j: (0, j))
    sds = jax.ShapeDtypeStruct((S, F), BF16)
    return pl.pallas_call(
        body, name="mlp_up", grid=(S // tm, nct),
        out_shape=(sds, sds, sds, jax.ShapeDtypeStruct((F, S), BF16)),
        in_specs=[pl.BlockSpec((tm, D), lambda i, j: (i, 0)), wspec, wspec, cwspec, cwspec, cbspec, cbspec],
        out_specs=(blk, blk, blk, pl.BlockSpec((ct, tm), lambda i, j: (j, i))),
        scratch_shapes=[pltpu.VMEM((nct, CARRY, ct), F32), pltpu.VMEM((nct, CARRY, ct), F32)],
        compiler_params=_params(dimension_semantics=("arbitrary", "arbitrary")),
    )(h, wg, wv, cwg, cwv, cbg, cbv)


def _conv_act_bwd(gm, w_down, up_g, up_v, cwg, cwv, cbg, cbv):
    S, F = up_g.shape
    D = gm.shape[1]
    tm, ct = _tile(S, 256), _tile(F, CONV_COLS)
    nct = F // ct

    def body(gm_ref, wd_ref, ug_ref, uv_ref, hg_ref, hv_ref, wg_ref, wv_ref, bg_ref, bv_ref,
             dug_ref, duv_ref, pg_ref, pv_ref):
        first = pl.program_id(1) == 0

        @pl.when(first)
        def _():
            pg_ref[...] = jnp.zeros_like(pg_ref)
            pv_ref[...] = jnp.zeros_like(pv_ref)

        da = _dot_nt(gm_ref[...], wd_ref[...])
        taps = []
        for u_ref, h_ref in ((ug_ref, hg_ref), (uv_ref, hv_ref)):
            h = jnp.where(first, 0.0, h_ref[...].astype(F32))
            uu = u_ref[...].astype(F32)
            taps.append((_shift_down(uu, h, 2), _shift_down(uu, h, 1), uu))
        u_g = wg_ref[0:1, :] * taps[0][0] + wg_ref[1:2, :] * taps[0][1] + wg_ref[2:3, :] * taps[0][2] + bg_ref[...]
        u_v = wv_ref[0:1, :] * taps[1][0] + wv_ref[1:2, :] * taps[1][1] + wv_ref[2:3, :] * taps[1][2] + bv_ref[...]
        sg = 1.0 / (1.0 + jnp.exp(-u_g))
        du_g = da * u_v * (sg * (1.0 + u_g * (1.0 - sg)))
        du_v = da * (u_g * sg)
        dug_ref[...] = du_g.astype(BF16)
        duv_ref[...] = du_v.astype(BF16)
        for du, tp, p_ref in ((du_g, taps[0], pg_ref), (du_v, taps[1], pv_ref)):
            for k in range(CONV_W):
                p_ref[k:k + 1, :] += jnp.sum(du * tp[k], axis=0, keepdims=True)
            p_ref[CONV_W:CONV_W + 1, :] += jnp.sum(du, axis=0, keepdims=True)

    main = pl.BlockSpec((tm, ct), lambda j, i: (i, j))
    halo = pl.BlockSpec((HALO, ct), _prev_halo_map(tm))
    wspec = pl.BlockSpec((CONV_W, ct), lambda j, i: (0, j))
    bspec = pl.BlockSpec((1, ct), lambda j, i: (0, j))
    pspec = pl.BlockSpec((8, ct), lambda j, i: (0, j))
    return pl.pallas_call(
        body, name="conv_act_bwd", grid=(nct, S // tm),
        out_shape=(jax.ShapeDtypeStruct((S, F), BF16), jax.ShapeDtypeStruct((S, F), BF16),
                   jax.ShapeDtypeStruct((8, F), F32), jax.ShapeDtypeStruct((8, F), F32)),
        in_specs=[pl.BlockSpec((tm, D), lambda j, i: (i, 0)), pl.BlockSpec((ct, D), lambda j, i: (j, 0)),
                  main, main, halo, halo, wspec, wspec, bspec, bspec],
        out_specs=(main, main, pspec, pspec),
        compiler_params=_params(dimension_semantics=("parallel", "arbitrary")),
    )(gm, w_down, up_g, up_v, up_g, up_v, cwg, cwv, cbg, cbv)


def _conv_bwd_taps(d, halo, w_ref):
    return w_ref[2:3, :] * d + w_ref[1:2, :] * _shift_up(d, halo, 1) + w_ref[0:1, :] * _shift_up(d, halo, 2)


def _scan_mats(R, nc, reverse):
    i = lax.broadcasted_iota(jnp.int32, (LANES, LANES), 0)
    j = lax.broadcasted_iota(jnp.int32, (LANES, LANES), 1)
    inner = ((i >= j) if reverse else (i <= j)).astype(F32)
    r = lax.broadcasted_iota(jnp.int32, (R, R), 0)
    c = lax.broadcasted_iota(jnp.int32, (R, R), 1)
    same = (r // nc) == (c // nc)
    outer = (same & ((c > r) if reverse else (c < r))).astype(F32)
    return inner, outer


def _chunk_scan(v, inner, outer, reverse):
    w = _dot(v, inner, precision=HIGHEST)
    col = 0 if reverse else LANES - 1
    carry = _dot(outer, w, precision=HIGHEST)[:, col:col + 1]
    return w + carry


def _fgate_fwd(z_rows, nc):
    R = z_rows.shape[0]

    def body(z_ref, f_ref):
        z = z_ref[...]
        logf = jnp.minimum(z, 0.0) - jnp.log(1.0 + jnp.exp(-jnp.abs(z)))
        inner, outer = _scan_mats(R, nc, False)
        f_ref[...] = _chunk_scan(logf, inner, outer, False)

    return pl.pallas_call(body, name="fgate_fwd", out_shape=jax.ShapeDtypeStruct((R, LANES), F32),
                          compiler_params=_params())(z_rows)


def _fgate_bwd(dfk_neg_rows, dfq_rows, z_rows, nc):
    R = z_rows.shape[0]
    nh = R // nc

    def body(dfk_ref, dfq_ref, z_ref, dz_ref, db_ref):
        inner, outer = _scan_mats(R, nc, True)
        dlogf = _chunk_scan(dfq_ref[...] - dfk_ref[...], inner, outer, True)
        dz = dlogf * (1.0 / (1.0 + jnp.exp(z_ref[...])))
        dz_ref[...] = dz
        hr = lax.broadcasted_iota(jnp.int32, (nh, R), 0)
        hc = lax.broadcasted_iota(jnp.int32, (nh, R), 1) // nc
        per_head = _dot((hr == hc).astype(F32), dz, precision=HIGHEST)
        db_ref[...] = jnp.sum(per_head, axis=1, keepdims=True)

    return pl.pallas_call(
        body, name="fgate_bwd",
        out_shape=(jax.ShapeDtypeStruct((R, LANES), F32), jax.ShapeDtypeStruct((nh, 1), F32)),
        compiler_params=_params())(dfk_neg_rows, dfq_rows, z_rows)


_NEG = -1e30
SKIP_BELOW = -106.0
_SCALE = HEAD_DIM ** -0.5
N_SCAN = ATT_BK // SCAN_BK
F_PARTS = 3
Q_F_LANE = HEAD_DIM
Q_ONE_LANE = HEAD_DIM + F_PARTS


def _kv_slice(j):
    return pl.ds(pl.multiple_of(j * ATT_BK, ATT_BK), ATT_BK)


def _mask_t(strict):
    s = lax.broadcasted_iota(jnp.int32, (ATT_BK, ATT_BQ), 0)
    t = lax.broadcasted_iota(jnp.int32, (ATT_BK, ATT_BQ), 1)
    return (s < t) if strict else (s <= t)


def _walk_down(i, step, alive, carry):
    carry = step(i, carry, True)

    def cond(st):
        n, go, _ = st
        return jnp.logical_and(n < i, go)

    def body(st):
        n, _, cr = st
        j = i - 1 - n
        cr = step(j, cr, False)
        return n + 1, alive(jnp.maximum(j - 1, 0), cr), cr

    return lax.while_loop(cond, body, (jnp.int32(0), alive(jnp.maximum(i - 1, 0), carry), carry))[2]


def _t_block(rows):
    return pl.BlockSpec((None, rows, ATT_BQ), lambda h, i, *_: (h, 0, i))


def _t_full(rows, S):
    return pl.BlockSpec((None, rows, S), lambda h, i, *_: (h, 0, 0))


def _n_block():
    return pl.BlockSpec((None, ATT_BQ, LANES), lambda h, i, *_: (h, i, 0))


def _n_full(S):
    return pl.BlockSpec((None, S, LANES), lambda h, i, *_: (h, 0, 0))


def _heads(t):
    S = t.shape[0]
    return jnp.transpose(t.reshape(S, -1, HEAD_DIM), (1, 0, 2))


def _skip_bounds(k_cols, f_rows):
    H, S = f_rows.shape
    f_end = f_rows.reshape(H, S // ATT_BK, ATT_BK)[:, :, -1]
    k_sq = jnp.sum(jnp.square(_heads(k_cols).astype(F32)), axis=-1).reshape(H, S // ATT_BK, ATT_BK)
    k_max = lax.cummax(jnp.sqrt(jnp.max(k_sq, axis=-1)), axis=1)
    return f_end, k_max


def _bf16_parts(f):
    hi = f.astype(BF16).astype(F32)
    mid = (f - hi).astype(BF16).astype(F32)
    return hi, mid, (f - hi - mid).astype(BF16).astype(F32)


def _att_prep(qkv, f_pairs):
    S = qkv.shape[0]
    n_pairs = qkv.shape[1] // (6 * LANES)
    H = 2 * n_pairs
    tm = _tile(S, 512)

    def body(qf_ref, kf_ref, vf_ref, qs_ref, ks_ref, vs_ref, f_ref,
             fqn, fqt, fkn, fkt, fvn, fvt, sqn, sqt, skn, skt, svn, svt):
        lane = lax.broadcasted_iota(jnp.int32, (1, LANES), 1)
        f = f_ref[...]

        def head(ref, e):
            t = ref[...].astype(F32)
            if e == 1:
                t = pltpu.roll(t, HEAD_DIM, 1)
            return jnp.where(lane < HEAD_DIM, t, 0.0)

        def at(first):
            return jnp.logical_and(lane >= first, lane < first + F_PARTS)

        for e in range(2):
            parts = _bf16_parts(f[:, e:e + 1])
            f_lanes = sum(jnp.where(lane == Q_F_LANE + k, parts[k], 0.0) for k in range(F_PARTS))
            nf_lanes = sum(jnp.where(lane == Q_ONE_LANE + k, parts[k], 0.0) for k in range(F_PARTS))
            vals = (
                (fqn, fqt, LANES, head(qf_ref, e) * _SCALE + f_lanes + jnp.where(at(Q_ONE_LANE), 1.0, 0.0)),
                (fkn, fkt, LANES, head(kf_ref, e) + jnp.where(at(Q_F_LANE), 1.0, 0.0) - nf_lanes),
                (fvn, fvt, HEAD_DIM, head(vf_ref, e)),
                (sqn, sqt, LANES, head(qs_ref, e) * _SCALE),
                (skn, skt, LANES, head(ks_ref, e)),
                (svn, svt, HEAD_DIM, head(vs_ref, e)),
            )
            for n_ref, t_ref, rows, val in vals:
                n_ref[e] = val.astype(BF16)
                t_ref[e] = val.T[:rows].astype(BF16)

    col = lambda base: pl.BlockSpec((tm, LANES), lambda i, p: (i, base + p))
    n_spec = pl.BlockSpec((2, tm, LANES), lambda i, p: (p, i, 0))
    t_spec = lambda rows: pl.BlockSpec((2, rows, tm), lambda i, p: (p, 0, i))
    n_sds = jax.ShapeDtypeStruct((H, S, LANES), BF16)
    t_sds = lambda rows: jax.ShapeDtypeStruct((H, rows, S), BF16)
    group = ([n_sds, t_sds(LANES), n_sds, t_sds(LANES), n_sds, t_sds(HEAD_DIM)],
             [n_spec, t_spec(LANES), n_spec, t_spec(LANES), n_spec, t_spec(HEAD_DIM)])
    res = pl.pallas_call(
        body, name="att_prep", grid=(S // tm, n_pairs),
        out_shape=tuple(group[0] * 2),
        in_specs=[col(k * n_pairs) for k in range(6)] + [pl.BlockSpec((None, tm, 2), lambda i, p: (p, i, 0))],
        out_specs=tuple(group[1] * 2),
        compiler_params=_params(dimension_semantics=("parallel", "parallel")),
    )(qkv, qkv, qkv, qkv, qkv, qkv, f_pairs)
    names = ("q_n", "q_t", "k_n", "k_t", "v_n", "v_t")
    return dict(zip(names, res[:6])), dict(zip(names, res[6:]))


def _fox_reach(qt, fend_ref, kmax_ref, h):
    qf = qt.astype(F32)
    q_norm = jnp.sqrt(jnp.sum(jnp.square(qf[:HEAD_DIM]), axis=0, keepdims=True))
    f_t = jnp.sum(qf[Q_F_LANE:Q_F_LANE + F_PARTS], axis=0, keepdims=True)
    return lambda j: q_norm * kmax_ref[h, j] + f_t - fend_ref[h, j]


def _fox_fwd(q_t, k_n, v_t, f_end, k_max, shards):
    H, _, S = q_t.shape
    n, nq = len(shards), S // ATT_BQ

    def body(fend_ref, kmax_ref, qt_ref, k_ref, vt_ref, *rest):
        ins, (ot_ref, lse_ref), outs, sems = rest[:n], rest[n:n + 2], rest[n + 2:2 * n + 2], rest[2 * n + 2:]
        h, i = pl.program_id(0), pl.program_id(1)

        @pl.when(jnp.logical_and(h == 0, i == 0))
        def _():
            for cp in _exchange_copies(ins, outs, *sems, False):
                cp.start()

        qt = qt_ref[...]
        reach = _fox_reach(qt, fend_ref, kmax_ref, h)

        def step(j, carry, masked):
            m, l, acc = carry
            ks = _kv_slice(j)
            s = _dot(k_ref[ks, :], qt)
            if masked:
                s = jnp.where(_mask_t(False), s, _NEG)
            mn = jnp.maximum(m, jnp.max(s, axis=0, keepdims=True))
            alpha = jnp.exp(m - mn)
            p = jnp.exp(s - mn)
            l = alpha * l + jnp.sum(p, axis=0, keepdims=True)
            acc = acc * alpha + _dot(vt_ref[:, ks], p.astype(BF16))
            return mn, l, acc

        def alive(j, carry):
            return jnp.max(reach(j) - carry[0]) > SKIP_BELOW

        row = jnp.zeros((1, ATT_BQ), F32)
        m, l, acc = _walk_down(i, step, alive, (row + _NEG, row, jnp.zeros((HEAD_DIM, ATT_BQ), F32)))
        ot_ref[...] = acc / l
        lse_ref[...] = m + jnp.log(l)

        @pl.when(jnp.logical_and(h == H - 1, i == nq - 1))
        def _():
            for cp in _exchange_copies(ins, outs, *sems, False):
                cp.wait()

    any_spec = pl.BlockSpec(memory_space=pl.ANY)
    grid_spec = pltpu.PrefetchScalarGridSpec(
        num_scalar_prefetch=2, grid=(H, nq),
        in_specs=[_t_block(LANES), _n_full(S), _t_full(HEAD_DIM, S)] + [any_spec] * n,
        out_specs=tuple([_t_block(HEAD_DIM), _t_block(1)] + [any_spec] * n),
        scratch_shapes=_exchange_sems(n))
    res = pl.pallas_call(
        body, name="fox_fwd", grid_spec=grid_spec,
        out_shape=tuple([jax.ShapeDtypeStruct((H, HEAD_DIM, S), F32), jax.ShapeDtypeStruct((H, 1, S), F32)]
                        + _exchange_out_shapes(shards, False)),
        compiler_params=_params(dimension_semantics=("arbitrary", "arbitrary"), has_side_effects=True),
    )(f_end, k_max, q_t, k_n, v_t, *shards)
    return res[0], res[1], res[2:]


def _fox_bwd(q_t, q_n, k_n, k_t, v_n, do_t, do_n, o_t, lse, f_end, k_max):
    H, _, S = q_t.shape

    def body(fend_ref, kmax_ref, qt_ref, qn_ref, k_ref, kt_ref, v_ref, dot_ref, don_ref, ot_ref, lse_ref,
             dqt_ref, dk_ref, dv_ref):
        h, i = pl.program_id(0), pl.program_id(1)

        @pl.when(i == 0)
        def _():
            dk_ref[...] = jnp.zeros_like(dk_ref)
            dv_ref[...] = jnp.zeros_like(dv_ref)

        qt, qn, dot, don = qt_ref[...], qn_ref[...], dot_ref[...], don_ref[...]
        lse = lse_ref[...]
        delta = jnp.sum(dot[:HEAD_DIM].astype(F32) * ot_ref[...], axis=0, keepdims=True)
        reach = _fox_reach(qt, fend_ref, kmax_ref, h)

        def alive(j, dq):
            return jnp.max(reach(j) - lse) > SKIP_BELOW

        def step(j, dq, masked):
            ks = _kv_slice(j)
            s = _dot(k_ref[ks, :], qt)
            if masked:
                s = jnp.where(_mask_t(False), s, _NEG)
            p = jnp.exp(s - lse)
            ds = (p * (_dot(v_ref[ks, :], dot) - delta)).astype(BF16)
            dk_ref[ks, :] += _dot(ds, qn)
            dv_ref[ks, :] += _dot(p.astype(BF16), don)
            return dq + _dot(kt_ref[:, ks], ds)

        dqt_ref[...] = _walk_down(i, step, alive, jnp.zeros((LANES, ATT_BQ), F32))

    grid_spec = pltpu.PrefetchScalarGridSpec(
        num_scalar_prefetch=2, grid=(H, S // ATT_BQ),
        in_specs=[_t_block(LANES), _n_block(), _n_full(S), _t_full(LANES, S), _n_full(S),
                  _t_block(LANES), _n_block(), _t_block(HEAD_DIM), _t_block(1)],
        out_specs=(_t_block(LANES), _n_full(S), _n_full(S)))
    return pl.pallas_call(
        body, name="fox_bwd", grid_spec=grid_spec,
        out_shape=(jax.ShapeDtypeStruct((H, LANES, S), F32), jax.ShapeDtypeStruct((H, S, LANES), F32),
                   jax.ShapeDtypeStruct((H, S, LANES), F32)),
        compiler_params=_params(dimension_semantics=("parallel", "arbitrary")),
    )(f_end, k_max, q_t, q_n, k_n, k_t, v_n, do_t, do_n, o_t, lse)


def _scan_lhs():
    r = lax.broadcasted_iota(jnp.int32, (SCAN_BK, 2 * SCAN_BK), 0)
    c = lax.broadcasted_iota(jnp.int32, (SCAN_BK, 2 * SCAN_BK), 1) % SCAN_BK
    return (c >= r).astype(BF16)


def _suffix_sum(t, lhs):
    hi = t.astype(BF16)
    lo = (t - hi.astype(F32)).astype(BF16)
    return _dot(lhs, jnp.concatenate([hi, lo], axis=0))


def _sb_scores(k, qt, mask):
    z = _dot(k, qt)
    e = jnp.exp(-jnp.abs(z))
    lb = -(jnp.maximum(z, 0.0) + jnp.log(1.0 + e))
    if mask is not None:
        lb = jnp.where(mask, lb, 0.0)
    return z, e, lb


def _scan_blocks():
    return [slice(u * SCAN_BK, (u + 1) * SCAN_BK) for u in reversed(range(N_SCAN))]


def _sb_fwd(q_t, k_n, v_t):
    H, _, S = q_t.shape

    def body(qt_ref, k_ref, vt_ref, ot_ref):
        i = pl.program_id(1)
        qt = qt_ref[...]
        lhs = _scan_lhs()

        def step(j, carry, masked):
            c, acc = carry
            ks = _kv_slice(j)
            mask = _mask_t(True) if masked else None
            z, _, lb = _sb_scores(k_ref[ks, :], qt, mask)
            parts = []
            for sl in _scan_blocks():
                rin = _suffix_sum(lb[sl], lhs)
                a = jnp.exp(z[sl] + rin + c)
                if masked:
                    a = jnp.where(mask[sl], a, 0.0)
                parts.append(a.astype(BF16))
                c = c + rin[0:1, :]
            a_all = jnp.concatenate(parts[::-1], axis=0)
            return c, acc + _dot(vt_ref[:, ks], a_all)

        carry = (jnp.zeros((1, ATT_BQ), F32), jnp.zeros((HEAD_DIM, ATT_BQ), F32))
        ot_ref[...] = _walk_down(i, step, lambda j, cr: jnp.max(cr[0]) > SKIP_BELOW, carry)[1]

    return pl.pallas_call(
        body, name="sb_fwd", grid=(H, S // ATT_BQ),
        out_shape=jax.ShapeDtypeStruct((H, HEAD_DIM, S), F32),
        in_specs=[_t_block(LANES), _n_full(S), _t_full(HEAD_DIM, S)],
        out_specs=_t_block(HEAD_DIM),
        compiler_params=_params(dimension_semantics=("parallel", "parallel")),
    )(q_t, k_n, v_t)


def _sb_bwd(q_t, q_n, k_n, k_t, v_n, do_t, do_n, o_t, bound):
    H, _, S = q_t.shape
    n, nq = len(bound), S // ATT_BQ

    def body(qt_ref, qn_ref, k_ref, kt_ref, v_ref, dot_ref, don_ref, ot_ref, *rest):
        ins, (dqt_ref, dko_ref, dvo_ref) = rest[:n], rest[n:n + 3]
        outs, sems, (dk_ref, dv_ref) = rest[n + 3:2 * n + 3], rest[2 * n + 3:2 * n + 6], rest[2 * n + 6:]
        h, i = pl.program_id(0), pl.program_id(1)

        @pl.when(jnp.logical_and(h == 0, i == 0))
        def _():
            for cp in _exchange_copies(ins, outs, *sems, True):
                cp.start()

        @pl.when(i == 0)
        def _():
            dk_ref[...] = jnp.zeros_like(dk_ref)
            dv_ref[...] = jnp.zeros_like(dv_ref)

        qt, qn, dot, don = qt_ref[...], qn_ref[...], dot_ref[...], don_ref[...]
        lhs = _scan_lhs()
        delta = jnp.sum(dot[:HEAD_DIM].astype(F32) * ot_ref[...], axis=0, keepdims=True)

        def step(j, carry, masked):
            c, g, dq = carry
            ks = _kv_slice(j)
            mask = _mask_t(True) if masked else None
            z, e, lb = _sb_scores(k_ref[ks, :], qt, mask)
            da = _dot(v_ref[ks, :], dot)
            a_parts, dz_parts = [], []
            for sl in _scan_blocks():
                rin = _suffix_sum(lb[sl], lhs)
                a = jnp.exp(z[sl] + rin + c)
                if masked:
                    a = jnp.where(mask[sl], a, 0.0)
                ab = a.astype(BF16)
                gg = ab.astype(F32) * da[sl]
                rgin = _suffix_sum(gg, lhs)
                rinv = 1.0 / (1.0 + e[sl])
                sig = jnp.where(z[sl] >= 0.0, rinv, e[sl] * rinv)
                dz = gg - sig * (delta - g - (rgin - gg))
                if masked:
                    dz = jnp.where(mask[sl], dz, 0.0)
                a_parts.append(ab)
                dz_parts.append(dz.astype(BF16))
                c = c + rin[0:1, :]
                g = g + rgin[0:1, :]
            ab_all = jnp.concatenate(a_parts[::-1], axis=0)
            dzb = jnp.concatenate(dz_parts[::-1], axis=0)
            dk_ref[ks, :] += _dot(dzb, qn)
            dv_ref[ks, :] += _dot(ab_all, don)
            return c, g, dq + _dot(kt_ref[:, ks], dzb)

        row = jnp.zeros((1, ATT_BQ), F32)
        carry = (row, row, jnp.zeros((LANES, ATT_BQ), F32))
        dqt_ref[...] = _walk_down(i, step, lambda j, cr: jnp.max(cr[0]) > SKIP_BELOW, carry)[2]

        @pl.when(i == nq - 1)
        def _():
            def cast(r, _):
                rows = pl.ds(pl.multiple_of(r * ATT_BQ, ATT_BQ), ATT_BQ)
                dko_ref[rows, :] = dk_ref[rows, :].astype(BF16)
                dvo_ref[rows, :] = dv_ref[rows, :].astype(BF16)
                return 0

            lax.fori_loop(0, nq, cast, 0)

        @pl.when(jnp.logical_and(h == H - 1, i == nq - 1))
        def _():
            for cp in _exchange_copies(ins, outs, *sems, True):
                cp.wait()

    any_spec = pl.BlockSpec(memory_space=pl.ANY)
    res = pl.pallas_call(
        body, name="sb_bwd", grid=(H, nq),
        out_shape=tuple([jax.ShapeDtypeStruct((H, LANES, S), F32), jax.ShapeDtypeStruct((H, S, LANES), BF16),
                         jax.ShapeDtypeStruct((H, S, LANES), BF16)] + _exchange_out_shapes(bound, True)),
        in_specs=[_t_block(LANES), _n_block(), _n_full(S), _t_full(LANES, S), _n_full(S),
                  _t_block(LANES), _n_block(), _t_block(HEAD_DIM)] + [any_spec] * n,
        out_specs=tuple([_t_block(LANES), _n_full(S), _n_full(S)] + [any_spec] * n),
        scratch_shapes=_exchange_sems(n) + [pltpu.VMEM((S, LANES), F32), pltpu.VMEM((S, LANES), F32)],
        compiler_params=_params(dimension_semantics=("arbitrary", "arbitrary"), has_side_effects=True),
    )(q_t, q_n, k_n, k_t, v_n, do_t, do_n, o_t, *bound)
    return res[0], res[1], res[2], res[3:]


def _dqkv_assemble(dqf_t, dkf, dvf, dqs_t, dks, dvs):
    H, _, S = dqf_t.shape
    n_pairs = H // 2
    tm = _tile(S, 512)

    def body(dqf_ref, dkf_ref, dvf_ref, dqs_ref, dks_ref, dvs_ref, out_ref, dfk_ref):
        lane = lax.broadcasted_iota(jnp.int32, (1, LANES), 1)
        slabs = ((dqf_ref, True), (dkf_ref, False), (dvf_ref, False),
                 (dqs_ref, True), (dks_ref, False), (dvs_ref, False))
        for k, (ref, transposed) in enumerate(slabs):
            if transposed:
                t0, t1 = ref[0].T * _SCALE, ref[1].T * _SCALE
            else:
                t0, t1 = ref[0].astype(F32), ref[1].astype(F32)
            out_ref[k] = jnp.where(lane < HEAD_DIM, t0, pltpu.roll(t1, HEAD_DIM, 1)).astype(BF16)
        for e in range(2):
            dfk_ref[e] = dkf_ref[e].T[Q_ONE_LANE:Q_ONE_LANE + 1, :]

    t_spec = pl.BlockSpec((2, LANES, tm), lambda i, p: (p, 0, i))
    n_spec = pl.BlockSpec((2, tm, LANES), lambda i, p: (p, i, 0))
    return pl.pallas_call(
        body, name="dqkv_assemble", grid=(S // tm, n_pairs),
        out_shape=(jax.ShapeDtypeStruct((6, S, n_pairs * LANES), BF16), jax.ShapeDtypeStruct((H, 1, S), F32)),
        in_specs=[t_spec, n_spec, n_spec, t_spec, n_spec, n_spec],
        out_specs=(pl.BlockSpec((6, tm, LANES), lambda i, p: (0, i, p)),
                   pl.BlockSpec((2, 1, tm), lambda i, p: (p, 0, i))),
        compiler_params=_params(dimension_semantics=("parallel", "parallel")),
    )(dqf_t, dkf, dvf, dqs_t, dks, dvs)


def _acc_spec(w):
    return pl.BlockSpec((1, w), lambda i: (0, 0))


def _loss_head(x1, act, w_down, gate_m, g_final, target):
    S, D = x1.shape
    F = act.shape[1]
    tm = _tile(S, 512)

    def body(x1_ref, act_ref, w_ref, gt_ref, gf_ref, tg_ref, dx2_ref, gm_ref, loss_ref, dgf_ref, dgt_ref):
        @pl.when(pl.program_id(0) == 0)
        def _():
            loss_ref[...] = jnp.zeros_like(loss_ref)
            dgf_ref[...] = jnp.zeros_like(dgf_ref)
            dgt_ref[...] = jnp.zeros_like(dgt_ref)

        mo = _dot(act_ref[...], w_ref[...])
        x2 = x1_ref[...] + gt_ref[...] * mo
        r = lax.rsqrt(jnp.mean(x2 * x2, axis=-1, keepdims=True) + EPS)
        xh = x2 * r
        diff = xh * gf_ref[...] - tg_ref[...]
        loss_ref[...] += (0.5 / D) * jnp.sum(diff * diff)
        dy = diff * (1.0 / D)
        dgf_ref[...] += jnp.sum(dy * xh, axis=0, keepdims=True)
        dxh = dy * gf_ref[...]
        dx2 = r * (dxh - xh * jnp.mean(dxh * xh, axis=-1, keepdims=True))
        dx2_ref[...] = dx2
        gm_ref[...] = (dx2 * gt_ref[...]).astype(BF16)
        dgt_ref[...] += jnp.sum(dx2 * mo, axis=0, keepdims=True)

    return pl.pallas_call(
        body, name="loss_head", grid=(S // tm,),
        out_shape=(jax.ShapeDtypeStruct((S, D), F32), jax.ShapeDtypeStruct((S, D), BF16),
                   jax.ShapeDtypeStruct((1, LANES), F32), jax.ShapeDtypeStruct((1, D), F32),
                   jax.ShapeDtypeStruct((1, D), F32)),
        in_specs=_row_specs(tm, [D, F]) + [pl.BlockSpec((F, D), lambda i: (0, 0))] + [_vec_spec(D)] * 2
        + _row_specs(tm, [D]),
        out_specs=tuple(_row_specs(tm, [D, D]) + [_acc_spec(LANES), _acc_spec(D), _acc_spec(D)]),
        compiler_params=_params(dimension_semantics=("arbitrary",)),
    )(x1, act, w_down, gate_m, g_final, target)


def _norm_bwd(lhs, rhs, xin, dres, g, scale, name, gate=None, branch=None, bound=(), conv=None, tm=256):
    S, D = xin.shape
    tm = _tile(S, tm)
    gated = gate is not None
    nl, nb, n_steps = len(lhs), len(bound), S // tm
    nc_ = nl if conv else 0
    n_out = (6 if gated else 4) + nc_

    def body(*refs):
        l_refs, r_refs, rest = refs[:nl], refs[nl:2 * nl], refs[2 * nl:]
        halo_refs, cw_refs, rest = rest[:nc_], rest[nc_:2 * nc_], rest[2 * nc_:]
        if gated:
            x_ref, dr_ref, g_ref, sc_ref, gt_ref, br_ref = rest[:6]
            rest = rest[6:]
        else:
            x_ref, dr_ref, g_ref, sc_ref = rest[:4]
            rest = rest[4:]
        ins, outs, ex_outs, sems = rest[:nb], rest[nb:nb + n_out], rest[nb + n_out:2 * nb + n_out], rest[2 * nb + n_out:]
        dx_ref, dsc_ref, dsh_ref, dg_ref = outs[:4]
        sums = (dsc_ref, dsh_ref, dg_ref) + ((outs[5],) if gated else ())
        dup_refs = outs[n_out - nc_:]
        i = pl.program_id(0)

        @pl.when(i == 0)
        def _():
            for s_ref in sums:
                s_ref[...] = jnp.zeros_like(s_ref)
            if nb:
                for cp in _chip_exchange_copies(ins, ex_outs, *sems):
                    cp.start()

        dhv = None
        for k, (l_ref, r_ref) in enumerate(zip(l_refs, r_refs)):
            if conv:
                halo = jnp.where(i == n_steps - 1, 0.0, halo_refs[k][...].astype(F32))
                dup = _conv_bwd_taps(l_ref[...].astype(F32), halo, cw_refs[k]).astype(BF16)
                dup_refs[k][...] = dup
                terms = [_dot_nt(dup, r_ref[...])]
            elif len(l_ref.shape) == 3:
                K = l_ref.shape[2]
                terms = [_dot_nt(l_ref[k], r_ref[:, k * K:(k + 1) * K]) for k in range(l_ref.shape[0])]
            else:
                terms = [_dot_nt(l_ref[...], r_ref[...])]
            for t in terms:
                dhv = t if dhv is None else dhv + t
        xv = x_ref[...]
        r = lax.rsqrt(jnp.mean(xv * xv, axis=-1, keepdims=True) + EPS)
        xh = xv * r
        dsc_ref[...] += jnp.sum(dhv * (xh * g_ref[...]), axis=0, keepdims=True)
        dsh_ref[...] += jnp.sum(dhv, axis=0, keepdims=True)
        dn = dhv * (1.0 + sc_ref[...])
        dg_ref[...] += jnp.sum(dn * xh, axis=0, keepdims=True)
        dxh = dn * g_ref[...]
        dx = dr_ref[...] + r * (dxh - xh * jnp.mean(dxh * xh, axis=-1, keepdims=True))
        dx_ref[...] = dx
        if gated:
            outs[4][...] = (dx * gt_ref[...]).astype(BF16)
            outs[5][...] += jnp.sum(dx * br_ref[...], axis=0, keepdims=True)

        if nb:
            @pl.when(i == n_steps - 1)
            def _():
                for cp in _chip_exchange_copies(ins, ex_outs, *sems):
                    cp.wait()

    def l_spec(a):
        if a.ndim == 3:
            return pl.BlockSpec((a.shape[0], tm, a.shape[2]), lambda i: (0, i, 0))
        return pl.BlockSpec((tm, a.shape[1]), lambda i: (i, 0))

    any_spec = pl.BlockSpec(memory_space=pl.ANY)
    vec = jax.ShapeDtypeStruct((1, D), F32)
    out_shape = [jax.ShapeDtypeStruct((S, D), F32), vec, vec, vec]
    out_specs = _row_specs(tm, [D]) + [_acc_spec(D)] * 3
    in_specs = [l_spec(a) for a in lhs] + [pl.BlockSpec(b.shape, lambda i: (0, 0)) for b in rhs]
    args = list(lhs) + list(rhs)
    if conv:
        step, last_halo = tm // HALO, S // HALO - 1
        in_specs += [pl.BlockSpec((HALO, a.shape[1]), lambda i: (jnp.minimum((i + 1) * step, last_halo), 0))
                     for a in lhs]
        in_specs += [pl.BlockSpec(w.shape, lambda i: (0, 0)) for w in conv]
        args += list(lhs) + list(conv)
    in_specs += _row_specs(tm, [D, D]) + [_vec_spec(D)] * 2
    args += [xin, dres, g, scale]
    if gated:
        out_shape += [jax.ShapeDtypeStruct((S, D), BF16), vec]
        out_specs += _row_specs(tm, [D]) + [_acc_spec(D)]
        in_specs += [_vec_spec(D)] + _row_specs(tm, [D])
        args += [gate, branch]
    if conv:
        out_shape += [jax.ShapeDtypeStruct(a.shape, BF16) for a in lhs]
        out_specs += [l_spec(a) for a in lhs]
    res = pl.pallas_call(
        body, name=name, grid=(n_steps,),
        out_shape=tuple(out_shape + [jax.ShapeDtypeStruct(b.shape, b.dtype) for b in bound]),
        in_specs=in_specs + [any_spec] * nb, out_specs=tuple(out_specs + [any_spec] * nb),
        scratch_shapes=_chip_exchange_sems(nb) if nb else [],
        compiler_params=_params(dimension_semantics=("arbitrary",), has_side_effects=bool(nb)),
    )(*args, *bound)
    return tuple(res[:n_out]) + (tuple(res[n_out:]),)


def _headnorm_bwd(dmix, o_f, o_s, g_f, g_s):
    dh, S = o_f.shape
    H = dh // HEAD_DIM
    tm = _tile(S, 256)

    def body(dm_ref, of_ref, os_ref, gf_ref, gs_ref, fn_ref, ft_ref, sn_ref, st_ref, dgf_ref, dgs_ref):
        @pl.when(pl.program_id(0) == 0)
        def _():
            dgf_ref[...] = jnp.zeros_like(dgf_ref)
            dgs_ref[...] = jnp.zeros_like(dgs_ref)

        ones = _group_ones()
        lane = lax.broadcasted_iota(jnp.int32, (1, LANES), 1)
        parts = ((of_ref, gf_ref, fn_ref, ft_ref, dgf_ref), (os_ref, gs_ref, sn_ref, st_ref, dgs_ref))
        for part, (o_ref, g_ref, n_ref, t_ref, dg_ref) in enumerate(parts):
            for t in range(dh // LANES):
                cols = slice(t * LANES, (t + 1) * LANES)
                o = o_ref[cols, :].T
                dm = dm_ref[:, part * dh + t * LANES: part * dh + (t + 1) * LANES]
                r = lax.rsqrt(_group_sum(o * o, ones) * (1.0 / HEAD_DIM) + EPS)
                oh = o * r
                dg_ref[:, cols] += jnp.sum(dm * oh, axis=0, keepdims=True)
                dn = dm * g_ref[:, cols]
                mean = _group_sum(dn * oh, ones) * (1.0 / HEAD_DIM)
                do = r * (dn - oh * mean)
                for e in range(2):
                    d = do if e == 0 else pltpu.roll(do, HEAD_DIM, 1)
                    d = jnp.where(lane < HEAD_DIM, d, 0.0)
                    n_ref[2 * t + e] = d.astype(BF16)
                    t_ref[2 * t + e] = d.T.astype(BF16)

    vec = jax.ShapeDtypeStruct((1, dh), F32)
    n_sds = jax.ShapeDtypeStruct((H, S, LANES), BF16)
    t_sds = jax.ShapeDtypeStruct((H, LANES, S), BF16)
    n_spec = pl.BlockSpec((H, tm, LANES), lambda i: (0, i, 0))
    t_spec = pl.BlockSpec((H, LANES, tm), lambda i: (0, 0, i))
    return pl.pallas_call(
        body, name="headnorm_bwd", grid=(S // tm,),
        out_shape=(n_sds, t_sds, n_sds, t_sds, vec, vec),
        in_specs=_row_specs(tm, [2 * dh]) + [_col_spec(tm, dh)] * 2 + [_vec_spec(dh)] * 2,
        out_specs=(n_spec, t_spec, n_spec, t_spec, _acc_spec(dh), _acc_spec(dh)),
        compiler_params=_params(dimension_semantics=("arbitrary",)),
    )(dmix, o_f, o_s, g_f, g_s)


def _adamw(w, gslots, m, v, name):
    R, C = w.shape
    n = gslots.shape[0]
    tr = 256 if (R % 256 == 0 and R > 256) else R
    bc1 = 1.0 - ADAM_B1 ** ADAM_STEP
    bc2 = 1.0 - ADAM_B2 ** ADAM_STEP

    def body(w_ref, gs_ref, m_ref, v_ref, g_ref, d_ref, nm_ref, nv_ref):
        g = gs_ref[0]
        for s in range(1, n):
            g = g + gs_ref[s]
        nm = ADAM_B1 * m_ref[...] + (1.0 - ADAM_B1) * g
        nv = ADAM_B2 * v_ref[...] + (1.0 - ADAM_B2) * (g * g)
        g_ref[...] = g
        nm_ref[...] = nm
        nv_ref[...] = nv
        d_ref[...] = -ADAM_LR * ((nm / bc1) / (jnp.sqrt(nv / bc2) + ADAM_EPS) + ADAM_WD * w_ref[...])

    blk = pl.BlockSpec((tr, C), lambda i: (i, 0))
    sds = jax.ShapeDtypeStruct((R, C), F32)
    return pl.pallas_call(
        body, name=name, grid=(R // tr,), out_shape=(sds,) * 4,
        in_specs=[blk, pl.BlockSpec((n, tr, C), lambda i: (0, i, 0)), blk, blk], out_specs=(blk,) * 4,
        compiler_params=_params(dimension_semantics=("parallel",)),
    )(w, gslots, m, v)


def _slot_sum(slots, name):
    n, _, C = slots.shape

    def body(s_ref, o_ref):
        acc = s_ref[0]
        for s in range(1, n):
            acc = acc + s_ref[s]
        o_ref[...] = acc

    return pl.pallas_call(body, name=name, out_shape=jax.ShapeDtypeStruct((1, C), F32),
                          compiler_params=_params())(slots)


def _pad_cols(a, n):
    return jnp.pad(a, ((0, 0), (0, n - a.shape[1])))


def _ungather(g, axis):
    if axis == 0:
        return g.reshape(g.shape[0] * g.shape[1], g.shape[2])
    return jnp.transpose(g, (1, 0, 2)).reshape(g.shape[1], g.shape[0] * g.shape[2])


def _to_slots(full, axis):
    R, C = full.shape
    if axis == 0:
        return full.reshape(N_DEV, R // N_DEV, C)
    return jnp.transpose(full.reshape(R, N_DEV, C // N_DEV), (1, 0, 2))


def kernel(x, c, w_ada, b_ada, g_attn, w_in, b_fgate, g_out_fox, g_out_sb, w_out, g_mlp, w_up, conv_w, conv_b, w_down, g_final, loss_target, m_w_ada, m_b_ada, m_g_attn, m_w_in, m_b_fgate, m_g_out_fox, m_g_out_sb, m_w_out, m_g_mlp, m_w_up, m_conv_w, m_conv_b, m_w_down, m_g_final, v_w_ada, v_b_ada, v_g_attn, v_w_in, v_b_fgate, v_g_out_fox, v_g_out_sb, v_w_out, v_g_mlp, v_w_up, v_conv_w, v_conv_b, v_w_down, v_g_final):
    S, D = x.shape[1], x.shape[2]
    dh = D // 2
    n_heads = dh // HEAD_DIM
    n_qkv = 6 * dh
    ff = w_down.shape[1] * N_DEV
    ffp = -(-ff // (2 * LANES)) * (2 * LANES)
    nc = S // LANES
    me = 4 * lax.axis_index("x") + 2 * lax.axis_index("y") + lax.axis_index("c")
    xs, tgt = x[0], loss_target[0]

    c_all, win_g = _gather_two_level([c, w_in[0].astype(BF16)], name="gather_first")
    c_all = c_all.reshape(N_DEV, D)
    W_in = _ungather(win_g, 1)
    W_qkv, W_f = W_in[:, :n_qkv], _pad_cols(W_in[:, n_qkv:], LANES)
    cb_g, cb_v = _pad_cols(conv_b[:, :ff], ffp), _pad_cols(conv_b[:, ff:], ffp)

    n_ada = w_ada.shape[2]
    b_shard = lax.dynamic_slice(b_ada, (0, me * n_ada), (1, n_ada))
    mod_cols = _ada_fwd(c_all, w_ada[0], b_shard)
    (mod_g,) = _exchange([mod_cols], scatter=False, name="gather_mod")
    mod = lax.dynamic_index_in_dim(mod_g, me, axis=1, keepdims=False).reshape(6, 1, D)
    shift_a, scale_a, gate_a, shift_m, scale_m, gate_m = [mod[k] for k in range(6)]

    h1, h1_t = _prenorm(xs, g_attn, scale_a, shift_a, "prenorm_attn")
    qkv = _mm(h1, W_qkv, BF16, "proj_qkv")
    flog = _mm(h1, W_f, F32, "proj_fgate")
    zf = flog[:, :n_heads] + b_fgate
    z_rows = zf.T.reshape(n_heads * nc, LANES)
    f_rows = _fgate_fwd(z_rows, nc).reshape(n_heads, S)
    f_pairs = jnp.transpose(f_rows.reshape(n_heads // 2, 2, S), (0, 2, 1))
    fox, sb = _att_prep(qkv, f_pairs)
    f_end, k_max = _skip_bounds(qkv[:, dh:2 * dh], f_rows)
    of_t, lse, (wout_g, wup_g, wdown_g, convw_g) = _fox_fwd(
        fox["q_t"], fox["k_n"], fox["v_t"], f_end, k_max,
        [w_out[0].astype(BF16), w_up[0].astype(BF16), w_down[0].astype(BF16), conv_w[0]])
    W_out = _ungather(wout_g, 0)
    W_up = _ungather(wup_g, 1)
    W_g, W_v = _pad_cols(W_up[:, :ff], ffp), _pad_cols(W_up[:, ff:], ffp)
    W_down = jnp.pad(_ungather(wdown_g, 0), ((0, ffp - ff), (0, 0)))
    cw_full = _ungather(convw_g, 1)
    cw_g, cw_v = _pad_cols(cw_full[:, :ff], ffp), _pad_cols(cw_full[:, ff:], ffp)
    os_t = _sb_fwd(sb["q_t"], sb["k_n"], sb["v_t"])
    o_f, o_s = of_t.reshape(dh, S), os_t.reshape(dh, S)
    mix, mix_t = _headnorm_fwd(o_f, o_s, g_out_fox, g_out_sb)
    a_out = _mm(mix, W_out, F32, "proj_out")
    x1, h2, h2_t = _resid_prenorm(xs, a_out, gate_a, g_mlp, scale_m, shift_m)
    up_g, up_v, act, act_t = _mlp_up(h2, W_g, W_v, cw_g, cw_v, cb_g, cb_v)

    dx2, gm, loss_p, dg_final, dgate_m = _loss_head(x1, act, W_down, gate_m, g_final.reshape(1, D), tgt)
    dW_down = _mm_acc(act_t, gm, "bwd_down_w")
    du_g, du_v, p_g, p_v = _conv_act_bwd(gm, W_down, up_g, up_v, cw_g, cw_v, cb_g, cb_v)
    dx1, dscale_m, dshift_m, dg_mlp, ga, dgate_a, dup_g, dup_v, _ = _norm_bwd(
        [du_g, du_v], [W_g, W_v], x1, dx2, g_mlp, scale_m, "norm_mlp_bwd", gate=gate_a, branch=a_out,
        conv=[cw_g, cw_v])
    dW_g = _mm_acc(h2_t, dup_g, "bwd_up_w_g")
    dW_v = _mm_acc(h2_t, dup_v, "bwd_up_w_v")
    dmix = _mm(ga, W_out, F32, "bwd_out_act", nt=True)
    dW_out = _mm_acc(mix_t, ga, "bwd_out_w")
    dof_n, dof_t, dos_n, dos_t, dg_fox, dg_sb = _headnorm_bwd(dmix, o_f, o_s, g_out_fox, g_out_sb)
    dqf_t, dkf, dvf = _fox_bwd(fox["q_t"], fox["q_n"], fox["k_n"], fox["k_t"], fox["v_n"], dof_t, dof_n, of_t, lse,
                              f_end, k_max)
    dW_upf = jnp.concatenate([dW_g[:, :ff], dW_v[:, :ff]], axis=1)
    dcw = jnp.concatenate([p_g[:CONV_W, :ff], p_v[:CONV_W, :ff]], axis=1)
    dqs_t, dks, dvs, (s_out, s_up, s_down, s_cw) = _sb_bwd(
        sb["q_t"], sb["q_n"], sb["k_n"], sb["k_t"], sb["v_n"], dos_t, dos_n, os_t,
        [_to_slots(dW_out, 0), _to_slots(dW_upf, 1), _to_slots(dW_down[:ff], 0), _to_slots(dcw, 1)])
    dparts, dfk = _dqkv_assemble(dqf_t, dkf, dvf, dqs_t, dks, dvs)
    dz_rows, db_fgate = _fgate_bwd(dfk.reshape(n_heads * nc, LANES),
                                   dqf_t[:, Q_F_LANE, :].reshape(n_heads * nc, LANES), z_rows, nc)
    dzf = _pad_cols(dz_rows.reshape(n_heads, S).T, LANES).astype(BF16)
    dW_qkv = _mm_acc_parts(h1_t, dparts, "bwd_in_w")
    dW_f = _mm_acc(h1_t, dzf, "bwd_in_w_fgate")
    dW_in = jnp.concatenate([jnp.transpose(dW_qkv, (1, 0, 2)).reshape(D, n_qkv), dW_f[:, :n_heads]], axis=1)
    bound_in = _to_slots(dW_in, 1)
    bound_in = bound_in.reshape((N_CHIPS, 2) + bound_in.shape[1:])
    (got_in,) = _scatter_to_sibling([bound_in], "scatter_sibling")
    c_idx = lax.axis_index("c").astype(jnp.int32).reshape(1)
    grad_x, dscale_a, dshift_a, dg_attn, (s_in,) = _norm_bwd(
        [dparts, dzf], [W_qkv, W_f], xs, dx1, g_attn, scale_a, "norm_attn_bwd",
        bound=[_pair_add(bound_in, got_in, c_idx, "pair_add")], tm=512)

    dconv_b = jnp.concatenate([p_g[CONV_W:CONV_W + 1, :ff], p_v[CONV_W:CONV_W + 1, :ff]], axis=1)
    parts = [dshift_a, dscale_a, dgate_a, dshift_m, dscale_m, dgate_m,
             dg_attn, db_fgate.reshape(1, n_heads), dg_fox, dg_sb, dg_mlp, dconv_b, dg_final,
             loss_p[:, :1]]
    sizes = [p.shape[1] for p in parts]
    vec = jnp.concatenate(parts, axis=1)
    n_vec = -(-vec.shape[1] // LANES) * LANES
    vec = _pad_cols(vec, n_vec)
    (vec_g,) = _exchange([vec], scatter=False, name="gather_small")
    offs = [0]
    for s in sizes:
        offs.append(offs[-1] + s)

    def small(k0, k1=None):
        k1 = k0 if k1 is None else k1
        return vec_g[:, :, offs[k0]:offs[k1 + 1]]

    dmod_all = small(0, 5).reshape(N_DEV, 6 * D)
    dmod_cols = lax.dynamic_slice(dmod_all, (0, me * n_ada), (N_DEV, n_ada))
    dW_ada = _ada_bwd(c_all.T, dmod_cols)


    res = {}
    res["w_ada"] = _adamw(w_ada[0], dW_ada[None], m_w_ada[0], v_w_ada[0], "adamw_w_ada")
    res["w_in"] = _adamw(w_in[0], s_in, m_w_in[0], v_w_in[0], "adamw_w_in")
    res["w_out"] = _adamw(w_out[0], s_out, m_w_out[0], v_w_out[0], "adamw_w_out")
    res["w_up"] = _adamw(w_up[0], s_up, m_w_up[0], v_w_up[0], "adamw_w_up")
    res["w_down"] = _adamw(w_down[0], s_down, m_w_down[0], v_w_down[0], "adamw_w_down")
    res["conv_w"] = _adamw(conv_w[0], s_cw, m_conv_w[0], v_conv_w[0], "adamw_conv_w")
    small_names = ["b_ada", "g_attn", "b_fgate", "g_out_fox", "g_out_sb", "g_mlp", "conv_b", "g_final"]
    small_w = [b_ada, g_attn, b_fgate, g_out_fox, g_out_sb, g_mlp, conv_b, g_final.reshape(1, D)]
    small_m = [m_b_ada, m_g_attn, m_b_fgate, m_g_out_fox, m_g_out_sb, m_g_mlp, m_conv_b, m_g_final.reshape(1, D)]
    small_v = [v_b_ada, v_g_attn, v_b_fgate, v_g_out_fox, v_g_out_sb, v_g_mlp, v_conv_b, v_g_final.reshape(1, D)]
    small_res = _adamw(jnp.concatenate(small_w, axis=1), small(0, 12), jnp.concatenate(small_m, axis=1),
                       jnp.concatenate(small_v, axis=1), "adamw_small")
    lo = 0
    for nm, wv in zip(small_names, small_w):
        res[nm] = tuple(r[:, lo:lo + wv.shape[1]] for r in small_res)
        lo += wv.shape[1]
    loss = _slot_sum(_pad_cols(small(13).reshape(N_DEV, 1), LANES).reshape(N_DEV, 1, LANES), "loss_sum")[0, 0]

    names = ["w_ada", "b_ada", "g_attn", "w_in", "b_fgate", "g_out_fox", "g_out_sb", "w_out", "g_mlp",
             "w_up", "conv_w", "conv_b", "w_down", "g_final"]

    def shaped(n, a):
        if n == "g_final":
            return a.reshape(D)
        if n in ("b_ada", "g_attn", "b_fgate", "g_out_fox", "g_out_sb", "g_mlp", "conv_b"):
            return a
        return a[None]

    outs = [loss, grad_x[None]]
    for k in range(4):
        outs += [shaped(n, res[n][k]) for n in names]
    return tuple(outs)
```

```python
import jax
import jax.numpy as jnp
from jax import lax
from jax.experimental import pallas as pl
from jax.experimental.pallas import tpu as pltpu

F32 = jnp.float32
BF16 = jnp.bfloat16
HIGHEST = lax.Precision.HIGHEST

N_DEV = 8
LANES = 128
HEAD_DIM = 64
EPS = 1e-6
CONV_W = 3
CONV_COLS = 1408
HALO = 16
ATT_BQ = 512
ATT_BK = 512
SCAN_BK = 128
VMEM_LIMIT = 56 * 1024 * 1024

ADAM_LR = 0.001
ADAM_B1 = 0.9
ADAM_B2 = 0.999
ADAM_EPS = 1e-08
ADAM_WD = 0.01
ADAM_STEP = 10


def _params(**kw):
    return pltpu.CompilerParams(vmem_limit_bytes=VMEM_LIMIT, **kw)


def _tile(n, cap):
    if n <= cap:
        return n
    best = None
    for t in range(LANES, cap + 1, LANES):
        if n % t == 0:
            best = t
    assert best is not None, (n, cap)
    return best


def _dot(a, b, **kw):
    return jnp.dot(a, b, preferred_element_type=F32, **kw)


def _exchange_copies(ins, outs, send_sems, recv_sems, loc_sems, scatter):
    n = len(ins)
    if n == 0:
        return []
    x, y, c = lax.axis_index("x"), lax.axis_index("y"), lax.axis_index("c")
    me = 4 * x + 2 * y + c
    copies = []
    for a in range(n):
        src = ins[a].at[me] if scatter else ins[a]
        copies.append(pltpu.make_async_copy(src, outs[a].at[me], loc_sems.at[a]))
    for k in range(1, N_DEV):
        px = 1 - x if k & 4 else x
        py = 1 - y if k & 2 else y
        pc = 1 - c if k & 1 else c
        peer = 4 * px + 2 * py + pc
        for a in range(n):
            src = ins[a].at[peer] if scatter else ins[a]
            copies.append(pltpu.make_async_remote_copy(
                src_ref=src, dst_ref=outs[a].at[me],
                send_sem=send_sems.at[a, k - 1], recv_sem=recv_sems.at[a, k - 1],
                device_id=(px, py, pc), device_id_type=pl.DeviceIdType.MESH))
    return copies


def _exchange_out_shapes(arrays, scatter):
    return [jax.ShapeDtypeStruct((N_DEV,) + tuple(a.shape[1:] if scatter else a.shape), a.dtype) for a in arrays]


def _exchange_sems(n):
    return [pltpu.SemaphoreType.DMA((n, N_DEV - 1)), pltpu.SemaphoreType.DMA((n, N_DEV - 1)),
            pltpu.SemaphoreType.DMA((n,))]


def _exchange(arrays, scatter, name):
    n = len(arrays)

    def body(*refs):
        copies = _exchange_copies(refs[:n], refs[n:2 * n], *refs[2 * n:], scatter)
        for cp in copies:
            cp.start()
        for cp in copies:
            cp.wait()

    any_spec = pl.BlockSpec(memory_space=pl.ANY)
    return pl.pallas_call(
        body, name=name, out_shape=tuple(_exchange_out_shapes(arrays, scatter)),
        in_specs=[any_spec] * n, out_specs=tuple([any_spec] * n),
        scratch_shapes=_exchange_sems(n),
        compiler_params=pltpu.CompilerParams(has_side_effects=True),
    )(*arrays)


def _gather_two_level(arrays, name):
    n = len(arrays)
    out_shape = [jax.ShapeDtypeStruct((N_DEV,) + tuple(a.shape), a.dtype) for a in arrays]

    def body(*refs):
        ins, outs = refs[:n], refs[n:2 * n]
        send_sems, recv_sems, loc_sems = refs[2 * n:]
        x, y, c = lax.axis_index("x"), lax.axis_index("y"), lax.axis_index("c")
        me, sibling = (x, y, c), (x, y, 1 - c)
        chips = [(1 - x, y), (x, 1 - y), (1 - x, 1 - y)]

        def slot(px, py, pc):
            return 4 * px + 2 * py + pc

        def copy(a, k, block, to, src=None):
            dst = outs[a].at[slot(*block)]
            return pltpu.make_async_remote_copy(
                src_ref=dst if src is None else src, dst_ref=dst,
                send_sem=send_sems.at[a, k], recv_sem=recv_sems.at[a, k],
                device_id=to, device_id_type=pl.DeviceIdType.MESH)

        local = [pltpu.make_async_copy(ins[a], outs[a].at[slot(*me)], loc_sems.at[a]) for a in range(n)]
        for cp in local:
            cp.start()
        first = []
        for a in range(n):
            first.append(copy(a, 0, me, sibling, src=ins[a]))
            first += [copy(a, 1 + j, me, (*chip, c), src=ins[a]) for j, chip in enumerate(chips)]
        for cp in first:
            cp.start()
        passed = []
        for j, chip in enumerate(chips):
            for a in range(n):
                copy(a, 1 + j, (*chip, c), me).wait_recv()
                cp = copy(a, 4 + j, (*chip, c), sibling)
                cp.start()
                passed.append(cp)
        for a in range(n):
            copy(a, 0, sibling, me).wait_recv()
            for j, chip in enumerate(chips):
                copy(a, 4 + j, (*chip, 1 - c), me).wait_recv()
        for cp in first + passed:
            cp.wait_send()
        for cp in local:
            cp.wait()

    any_spec = pl.BlockSpec(memory_space=pl.ANY)
    return pl.pallas_call(
        body, name=name, out_shape=tuple(out_shape),
        in_specs=[any_spec] * n, out_specs=tuple([any_spec] * n),
        scratch_shapes=[pltpu.SemaphoreType.DMA((n, N_DEV - 1)), pltpu.SemaphoreType.DMA((n, N_DEV - 1)),
                        pltpu.SemaphoreType.DMA((n,))],
        compiler_params=pltpu.CompilerParams(has_side_effects=True),
    )(*arrays)


N_CHIPS = 4


def _scatter_to_sibling(arrays, name):
    n = len(arrays)
    out_shape = [jax.ShapeDtypeStruct((N_CHIPS,) + tuple(a.shape[2:]), a.dtype) for a in arrays]

    def body(*refs):
        ins, outs = refs[:n], refs[n:2 * n]
        send_sems, recv_sems = refs[2 * n:]
        x, y, c = lax.axis_index("x"), lax.axis_index("y"), lax.axis_index("c")
        copies = []
        for a in range(n):
            for q in range(N_CHIPS):
                cp = pltpu.make_async_remote_copy(
                    src_ref=ins[a].at[q, 1 - c], dst_ref=outs[a].at[q],
                    send_sem=send_sems.at[a, q], recv_sem=recv_sems.at[a, q],
                    device_id=(x, y, 1 - c), device_id_type=pl.DeviceIdType.MESH)
                cp.start()
                copies.append(cp)
        for cp in copies:
            cp.wait()

    any_spec = pl.BlockSpec(memory_space=pl.ANY)
    return pl.pallas_call(
        body, name=name, out_shape=tuple(out_shape),
        in_specs=[any_spec] * n, out_specs=tuple([any_spec] * n),
        scratch_shapes=[pltpu.SemaphoreType.DMA((n, N_CHIPS)), pltpu.SemaphoreType.DMA((n, N_CHIPS))],
        compiler_params=pltpu.CompilerParams(has_side_effects=True),
    )(*arrays)


def _pair_add(mine, got, c_idx, name):
    _, _, R, C = mine.shape
    tr = 256 if (R % 256 == 0 and R > 256) else R

    def body(c_ref, m_ref, g_ref, o_ref):
        o_ref[...] = m_ref[...] + g_ref[...]

    grid_spec = pltpu.PrefetchScalarGridSpec(
        num_scalar_prefetch=1, grid=(N_CHIPS, R // tr),
        in_specs=[pl.BlockSpec((None, None, tr, C), lambda q, i, c_ref: (q, c_ref[0], i, 0)),
                  pl.BlockSpec((None, tr, C), lambda q, i, c_ref: (q, i, 0))],
        out_specs=pl.BlockSpec((None, tr, C), lambda q, i, c_ref: (q, i, 0)))
    return pl.pallas_call(
        body, name=name, grid_spec=grid_spec, out_shape=jax.ShapeDtypeStruct((N_CHIPS, R, C), mine.dtype),
        compiler_params=_params(dimension_semantics=("parallel", "parallel")),
    )(c_idx, mine, got)


def _chip_exchange_copies(ins, outs, send_sems, recv_sems, loc_sems):
    n = len(ins)
    x, y, c = lax.axis_index("x"), lax.axis_index("y"), lax.axis_index("c")
    myq = 2 * x + y
    copies = [pltpu.make_async_copy(ins[a].at[myq], outs[a].at[myq], loc_sems.at[a]) for a in range(n)]
    for k in range(1, N_CHIPS):
        qx = 1 - x if k & 2 else x
        qy = 1 - y if k & 1 else y
        for a in range(n):
            copies.append(pltpu.make_async_remote_copy(
                src_ref=ins[a].at[2 * qx + qy], dst_ref=outs[a].at[myq],
                send_sem=send_sems.at[a, k - 1], recv_sem=recv_sems.at[a, k - 1],
                device_id=(qx, qy, c), device_id_type=pl.DeviceIdType.MESH))
    return copies


def _chip_exchange_sems(n):
    return [pltpu.SemaphoreType.DMA((n, N_CHIPS - 1)), pltpu.SemaphoreType.DMA((n, N_CHIPS - 1)),
            pltpu.SemaphoreType.DMA((n,))]


def _dot_nt(a, b):
    return lax.dot_general(a, b, (((1,), (1,)), ((), ())), preferred_element_type=F32)


def _rhs_spec(b, tn, nt):
    if nt:
        return pl.BlockSpec((tn, b.shape[1]), lambda i, j: (j, 0))
    return pl.BlockSpec((b.shape[0], tn), lambda i, j: (0, j))


def _mm(a, b, out_dtype, name, tm=1024, tn=512, nt=False):
    M, K = a.shape
    N = b.shape[0] if nt else b.shape[1]
    tm, tn = _tile(M, tm), _tile(N, tn)
    dot = _dot_nt if nt else _dot

    def body(a_ref, b_ref, o_ref):
        o_ref[...] = dot(a_ref[...], b_ref[...]).astype(out_dtype)

    return pl.pallas_call(
        body, name=name, out_shape=jax.ShapeDtypeStruct((M, N), out_dtype),
        grid=(M // tm, N // tn),
        in_specs=[pl.BlockSpec((tm, K), lambda i, j: (i, 0)), _rhs_spec(b, tn, nt)],
        out_specs=pl.BlockSpec((tm, tn), lambda i, j: (i, j)),
        compiler_params=_params(dimension_semantics=("parallel", "parallel")),
    )(a, b)


def _mm_acc(a, b, name, tm=1408, tn=1408, tk=1024):
    M, S = a.shape
    _, N = b.shape
    tm, tn, tk = _tile(M, tm), _tile(N, tn), _tile(S, tk)

    def body(a_ref, b_ref, o_ref):
        @pl.when(pl.program_id(2) == 0)
        def _():
            o_ref[...] = jnp.zeros_like(o_ref)

        o_ref[...] += _dot(a_ref[...], b_ref[...])

    return pl.pallas_call(
        body, name=name, out_shape=jax.ShapeDtypeStruct((M, N), F32),
        grid=(M // tm, N // tn, S // tk),
        in_specs=[pl.BlockSpec((tm, tk), lambda i, j, k: (i, k)), pl.BlockSpec((tk, tn), lambda i, j, k: (k, j))],
        out_specs=pl.BlockSpec((tm, tn), lambda i, j, k: (i, j)),
        compiler_params=_params(dimension_semantics=("parallel", "parallel", "arbitrary")),
    )(a, b)


def _mm_acc_parts(a, parts, name, tm=1024, tk=1024):
    M, S = a.shape
    P, _, K = parts.shape
    tm, tk = _tile(M, tm), _tile(S, tk)

    def body(a_ref, b_ref, o_ref):
        @pl.when(pl.program_id(2) == 0)
        def _():
            o_ref[...] = jnp.zeros_like(o_ref)

        o_ref[...] += _dot(a_ref[...], b_ref[...])

    return pl.pallas_call(
        body, name=name, out_shape=jax.ShapeDtypeStruct((P, M, K), F32), grid=(P, M // tm, S // tk),
        in_specs=[pl.BlockSpec((tm, tk), lambda k, i, s: (i, s)), pl.BlockSpec((None, tk, K), lambda k, i, s: (k, s, 0))],
        out_specs=pl.BlockSpec((None, tm, K), lambda k, i, s: (k, i, 0)),
        compiler_params=_params(dimension_semantics=("parallel", "parallel", "arbitrary")),
    )(a, parts)


def _silu(z):
    return z * (1.0 / (1.0 + jnp.exp(-z)))


def _ada_fwd(c_all, w_shard, b_shard):
    n = w_shard.shape[1]

    def body(c_ref, w_ref, b_ref, o_ref):
        o_ref[...] = _dot(_silu(c_ref[...]), w_ref[...], precision=HIGHEST) + b_ref[...]

    return pl.pallas_call(body, name="ada_fwd", out_shape=jax.ShapeDtypeStruct((N_DEV, n), F32),
                          compiler_params=_params())(c_all, w_shard, b_shard)


def _ada_bwd(c_all_t, dmod_cols):
    D = c_all_t.shape[0]
    n = dmod_cols.shape[1]

    def body(ct_ref, dm_ref, o_ref):
        sc = _silu(ct_ref[...])
        dm = dm_ref[...]
        acc = sc[:, 0:1] * dm[0:1, :]
        for b in range(1, N_DEV):
            acc = acc + sc[:, b:b + 1] * dm[b:b + 1, :]
        o_ref[...] = acc

    return pl.pallas_call(body, name="ada_bwd", out_shape=jax.ShapeDtypeStruct((D, n), F32),
                          compiler_params=_params())(c_all_t, dmod_cols)


def _row_specs(tm, widths):
    return [pl.BlockSpec((tm, w), lambda i: (i, 0)) for w in widths]


def _vec_spec(w):
    return pl.BlockSpec((1, w), lambda i: (0, 0))


def _col_spec(tm, w):
    return pl.BlockSpec((w, tm), lambda i: (0, i))


def _prenorm(x, g, scale, shift, name):
    S, D = x.shape
    tm = _tile(S, 512)

    def body(x_ref, g_ref, sc_ref, sh_ref, h_ref, ht_ref):
        xv = x_ref[...]
        r = lax.rsqrt(jnp.mean(xv * xv, axis=-1, keepdims=True) + EPS)
        h = (xv * r) * g_ref[...] * (1.0 + sc_ref[...]) + sh_ref[...]
        h_ref[...] = h.astype(BF16)
        ht_ref[...] = h.T.astype(BF16)

    return pl.pallas_call(
        body, name=name, grid=(S // tm,),
        out_shape=(jax.ShapeDtypeStruct((S, D), BF16), jax.ShapeDtypeStruct((D, S), BF16)),
        in_specs=_row_specs(tm, [D]) + [_vec_spec(D)] * 3,
        out_specs=(_row_specs(tm, [D])[0], _col_spec(tm, D)),
        compiler_params=_params(dimension_semantics=("parallel",)),
    )(x, g, scale, shift)


def _group_ones():
    r = (lax.broadcasted_iota(jnp.int32, (2 * LANES, LANES), 0) % LANES) // HEAD_DIM
    c = lax.broadcasted_iota(jnp.int32, (2 * LANES, LANES), 1) // HEAD_DIM
    return (r == c).astype(BF16)


def _group_sum(t, ones):
    hi = t.astype(BF16)
    lo = (t - hi.astype(F32)).astype(BF16)
    return _dot(jnp.concatenate([hi, lo], axis=1), ones)


def _headnorm_fwd(o_f, o_s, g_f, g_s):
    dh, S = o_f.shape
    tm = _tile(S, 512)

    def body(of_ref, os_ref, gf_ref, gs_ref, mix_ref, mixt_ref):
        ones = _group_ones()
        for part, (o_ref, g_ref) in enumerate(((of_ref, gf_ref), (os_ref, gs_ref))):
            for t in range(dh // LANES):
                cols = slice(t * LANES, (t + 1) * LANES)
                out = slice(part * dh + t * LANES, part * dh + (t + 1) * LANES)
                o = o_ref[cols, :].T
                ms = _group_sum(o * o, ones) * (1.0 / HEAD_DIM)
                mixn = o * lax.rsqrt(ms + EPS) * g_ref[:, cols]
                mix_ref[:, out] = mixn.astype(BF16)
                mixt_ref[out, :] = mixn.T.astype(BF16)

    return pl.pallas_call(
        body, name="headnorm_fwd", grid=(S // tm,),
        out_shape=(jax.ShapeDtypeStruct((S, 2 * dh), BF16), jax.ShapeDtypeStruct((2 * dh, S), BF16)),
        in_specs=[_col_spec(tm, dh)] * 2 + [_vec_spec(dh)] * 2,
        out_specs=(_row_specs(tm, [2 * dh])[0], _col_spec(tm, 2 * dh)),
        compiler_params=_params(dimension_semantics=("parallel",)),
    )(o_f, o_s, g_f, g_s)


def _resid_prenorm(x, a_out, gate, g, scale, shift):
    S, D = x.shape
    tm = _tile(S, 512)

    def body(x_ref, a_ref, gt_ref, g_ref, sc_ref, sh_ref, x1_ref, h_ref, ht_ref):
        x1 = x_ref[...] + gt_ref[...] * a_ref[...]
        x1_ref[...] = x1
        r = lax.rsqrt(jnp.mean(x1 * x1, axis=-1, keepdims=True) + EPS)
        h = (x1 * r) * g_ref[...] * (1.0 + sc_ref[...]) + sh_ref[...]
        h_ref[...] = h.astype(BF16)
        ht_ref[...] = h.T.astype(BF16)

    return pl.pallas_call(
        body, name="resid_prenorm", grid=(S // tm,),
        out_shape=(jax.ShapeDtypeStruct((S, D), F32), jax.ShapeDtypeStruct((S, D), BF16),
                   jax.ShapeDtypeStruct((D, S), BF16)),
        in_specs=_row_specs(tm, [D, D]) + [_vec_spec(D)] * 4,
        out_specs=tuple(_row_specs(tm, [D, D]) + [_col_spec(tm, D)]),
        compiler_params=_params(dimension_semantics=("parallel",)),
    )(x, a_out, gate, g, scale, shift)


def _shift_down(main, halo, k):
    ext = jnp.concatenate([halo, main], axis=0)
    return pltpu.roll(ext, k, 0)[halo.shape[0]:]


def _shift_up(main, halo, k):
    ext = jnp.concatenate([main, halo], axis=0)
    n = ext.shape[0]
    return pltpu.roll(ext, n - k, 0)[:main.shape[0]]


def _conv(up, up_halo, w_ref, b_ref):
    return (w_ref[2:3, :] * up + w_ref[1:2, :] * _shift_down(up, up_halo, 1)
            + w_ref[0:1, :] * _shift_down(up, up_halo, 2) + b_ref[...])


def _prev_halo_map(tm):
    step = tm // HALO
    return lambda j, i: (jnp.maximum(i * step - 1, 0), j)


MLP_TM = 512
MLP_CT = 1408
CARRY = 8


def _mlp_up(h, wg, wv, cwg, cwv, cbg, cbv):
    S, D = h.shape
    F = wg.shape[1]
    tm, ct = _tile(S, MLP_TM), _tile(F, MLP_CT)
    nct = F // ct

    def body(h_ref, wg_ref, wv_ref, cwg_ref, cwv_ref, cbg_ref, cbv_ref,
             upg_ref, upv_ref, act_ref, actt_ref, hg_scr, hv_scr):
        i, j = pl.program_id(0), pl.program_id(1)
        hv = h_ref[...]
        us = []
        for w_ref, cw_ref, cb_ref, up_ref, scr in ((wg_ref, cwg_ref, cbg_ref, upg_ref, hg_scr),
                                                   (wv_ref, cwv_ref, cbv_ref, upv_ref, hv_scr)):
            up = _dot(hv, w_ref[...]).astype(BF16)
            up_ref[...] = up
            upf = up.astype(F32)
            halo = jnp.where(i == 0, 0.0, scr[j])
            us.append(_conv(upf, halo, cw_ref, cb_ref))
            scr[j] = upf[tm - CARRY:, :]
        act = _silu(us[0]) * us[1]
        act_ref[...] = act.astype(BF16)
        actt_ref[...] = act.T.astype(BF16)

    blk = pl.BlockSpec((tm, ct), lambda i, j: (i, j))
    wspec = pl.BlockSpec((D, ct), lambda i, j: (0, j))
    cwspec = pl.BlockSpec((CONV_W, ct), lambda i, j: (0, j))
    cbspec = pl.BlockSpec((1, ct), lambda i, j: (0, j))
    sds = jax.ShapeDtypeStruct((S, F), BF16)
    return pl.pallas_call(
        body, name="mlp_up", grid=(S // tm, nct),
        out_shape=(sds, sds, sds, jax.ShapeDtypeStruct((F, S), BF16)),
        in_specs=[pl.BlockSpec((tm, D), lambda i, j: (i, 0)), wspec, wspec, cwspec, cwspec, cbspec, cbspec],
        out_specs=(blk, blk, blk, pl.BlockSpec((ct, tm), lambda i, j: (j, i))),
        scratch_shapes=[pltpu.VMEM((nct, CARRY, ct), F32), pltpu.VMEM((nct, CARRY, ct), F32)],
        compiler_params=_params(dimension_semantics=("arbitrary", "arbitrary")),
    )(h, wg, wv, cwg, cwv, cbg, cbv)


def _conv_act_bwd(gm, w_down, up_g, up_v, cwg, cwv, cbg, cbv):
    S, F = up_g.shape
    D = gm.shape[1]
    tm, ct = _tile(S, 256), _tile(F, CONV_COLS)
    nct = F // ct

    def body(gm_ref, wd_ref, ug_ref, uv_ref, hg_ref, hv_ref, wg_ref, wv_ref, bg_ref, bv_ref,
             dug_ref, duv_ref, pg_ref, pv_ref):
        first = pl.program_id(1) == 0

        @pl.when(first)
        def _():
            pg_ref[...] = jnp.zeros_like(pg_ref)
            pv_ref[...] = jnp.zeros_like(pv_ref)

        da = _dot_nt(gm_ref[...], wd_ref[...])
        taps = []
        for u_ref, h_ref in ((ug_ref, hg_ref), (uv_ref, hv_ref)):
            h = jnp.where(first, 0.0, h_ref[...].astype(F32))
            uu = u_ref[...].astype(F32)
            taps.append((_shift_down(uu, h, 2), _shift_down(uu, h, 1), uu))
        u_g = wg_ref[0:1, :] * taps[0][0] + wg_ref[1:2, :] * taps[0][1] + wg_ref[2:3, :] * taps[0][2] + bg_ref[...]
        u_v = wv_ref[0:1, :] * taps[1][0] + wv_ref[1:2, :] * taps[1][1] + wv_ref[2:3, :] * taps[1][2] + bv_ref[...]
        sg = 1.0 / (1.0 + jnp.exp(-u_g))
        du_g = da * u_v * (sg * (1.0 + u_g * (1.0 - sg)))
        du_v = da * (u_g * sg)
        dug_ref[...] = du_g.astype(BF16)
        duv_ref[...] = du_v.astype(BF16)
        for du, tp, p_ref in ((du_g, taps[0], pg_ref), (du_v, taps[1], pv_ref)):
            for k in range(CONV_W):
                p_ref[k:k + 1, :] += jnp.sum(du * tp[k], axis=0, keepdims=True)
            p_ref[CONV_W:CONV_W + 1, :] += jnp.sum(du, axis=0, keepdims=True)

    main = pl.BlockSpec((tm, ct), lambda j, i: (i, j))
    halo = pl.BlockSpec((HALO, ct), _prev_halo_map(tm))
    wspec = pl.BlockSpec((CONV_W, ct), lambda j, i: (0, j))
    bspec = pl.BlockSpec((1, ct), lambda j, i: (0, j))
    pspec = pl.BlockSpec((8, ct), lambda j, i: (0, j))
    return pl.pallas_call(
        body, name="conv_act_bwd", grid=(nct, S // tm),
        out_shape=(jax.ShapeDtypeStruct((S, F), BF16), jax.ShapeDtypeStruct((S, F), BF16),
                   jax.ShapeDtypeStruct((8, F), F32), jax.ShapeDtypeStruct((8, F), F32)),
        in_specs=[pl.BlockSpec((tm, D), lambda j, i: (i, 0)), pl.BlockSpec((ct, D), lambda j, i: (j, 0)),
                  main, main, halo, halo, wspec, wspec, bspec, bspec],
        out_specs=(main, main, pspec, pspec),
        compiler_params=_params(dimension_semantics=("parallel", "arbitrary")),
    )(gm, w_down, up_g, up_v, up_g, up_v, cwg, cwv, cbg, cbv)


def _conv_bwd_taps(d, halo, w_ref):
    return w_ref[2:3, :] * d + w_ref[1:2, :] * _shift_up(d, halo, 1) + w_ref[0:1, :] * _shift_up(d, halo, 2)


def _scan_mats(R, nc, reverse):
    i = lax.broadcasted_iota(jnp.int32, (LANES, LANES), 0)
    j = lax.broadcasted_iota(jnp.int32, (LANES, LANES), 1)
    inner = ((i >= j) if reverse else (i <= j)).astype(F32)
    r = lax.broadcasted_iota(jnp.int32, (R, R), 0)
    c = lax.broadcasted_iota(jnp.int32, (R, R), 1)
    same = (r // nc) == (c // nc)
    outer = (same & ((c > r) if reverse else (c < r))).astype(F32)
    return inner, outer


def _chunk_scan(v, inner, outer, reverse):
    w = _dot(v, inner, precision=HIGHEST)
    col = 0 if reverse else LANES - 1
    carry = _dot(outer, w, precision=HIGHEST)[:, col:col + 1]
    return w + carry


def _fgate_fwd(z_rows, nc):
    R = z_rows.shape[0]

    def body(z_ref, f_ref):
        z = z_ref[...]
        logf = jnp.minimum(z, 0.0) - jnp.log(1.0 + jnp.exp(-jnp.abs(z)))
        inner, outer = _scan_mats(R, nc, False)
        f_ref[...] = _chunk_scan(logf, inner, outer, False)

    return pl.pallas_call(body, name="fgate_fwd", out_shape=jax.ShapeDtypeStruct((R, LANES), F32),
                          compiler_params=_params())(z_rows)


def _fgate_bwd(dfk_neg_rows, dfq_rows, z_rows, nc):
    R = z_rows.shape[0]
    nh = R // nc

    def body(dfk_ref, dfq_ref, z_ref, dz_ref, db_ref):
        inner, outer = _scan_mats(R, nc, True)
        dlogf = _chunk_scan(dfq_ref[...] - dfk_ref[...], inner, outer, True)
        dz = dlogf * (1.0 / (1.0 + jnp.exp(z_ref[...])))
        dz_ref[...] = dz
        hr = lax.broadcasted_iota(jnp.int32, (nh, R), 0)
        hc = lax.broadcasted_iota(jnp.int32, (nh, R), 1) // nc
        per_head = _dot((hr == hc).astype(F32), dz, precision=HIGHEST)
        db_ref[...] = jnp.sum(per_head, axis=1, keepdims=True)

    return pl.pallas_call(
        body, name="fgate_bwd",
        out_shape=(jax.ShapeDtypeStruct((R, LANES), F32), jax.ShapeDtypeStruct((nh, 1), F32)),
        compiler_params=_params())(dfk_neg_rows, dfq_rows, z_rows)


_NEG = -1e30
SKIP_BELOW = -106.0
_SCALE = HEAD_DIM ** -0.5
N_SCAN = ATT_BK // SCAN_BK
F_PARTS = 3
Q_F_LANE = HEAD_DIM
Q_ONE_LANE = HEAD_DIM + F_PARTS


def _kv_slice(j):
    return pl.ds(pl.multiple_of(j * ATT_BK, ATT_BK), ATT_BK)


def _mask_t(strict):
    s = lax.broadcasted_iota(jnp.int32, (ATT_BK, ATT_BQ), 0)
    t = lax.broadcasted_iota(jnp.int32, (ATT_BK, ATT_BQ), 1)
    return (s < t) if strict else (s <= t)


def _walk_down(i, step, alive, carry):
    carry = step(i, carry, True)

    def cond(st):
        n, go, _ = st
        return jnp.logical_and(n < i, go)

    def body(st):
        n, _, cr = st
        j = i - 1 - n
        cr = step(j, cr, False)
        return n + 1, alive(jnp.maximum(j - 1, 0), cr), cr

    return lax.while_loop(cond, body, (jnp.int32(0), alive(jnp.maximum(i - 1, 0), carry), carry))[2]


def _t_block(rows):
    return pl.BlockSpec((None, rows, ATT_BQ), lambda h, i, *_: (h, 0, i))


def _t_full(rows, S):
    return pl.BlockSpec((None, rows, S), lambda h, i, *_: (h, 0, 0))


def _n_block():
    return pl.BlockSpec((None, ATT_BQ, LANES), lambda h, i, *_: (h, i, 0))


def _n_full(S):
    return pl.BlockSpec((None, S, LANES), lambda h, i, *_: (h, 0, 0))


def _heads(t):
    S = t.shape[0]
    return jnp.transpose(t.reshape(S, -1, HEAD_DIM), (1, 0, 2))


def _skip_bounds(k_cols, f_rows):
    H, S = f_rows.shape
    f_end = f_rows.reshape(H, S // ATT_BK, ATT_BK)[:, :, -1]
    k_sq = jnp.sum(jnp.square(_heads(k_cols).astype(F32)), axis=-1).reshape(H, S // ATT_BK, ATT_BK)
    k_max = lax.cummax(jnp.sqrt(jnp.max(k_sq, axis=-1)), axis=1)
    return f_end, k_max


def _bf16_parts(f):
    hi = f.astype(BF16).astype(F32)
    mid = (f - hi).astype(BF16).astype(F32)
    return hi, mid, (f - hi - mid).astype(BF16).astype(F32)


def _att_prep(qkv, f_pairs):
    S = qkv.shape[0]
    n_pairs = qkv.shape[1] // (6 * LANES)
    H = 2 * n_pairs
    tm = _tile(S, 512)

    def body(qf_ref, kf_ref, vf_ref, qs_ref, ks_ref, vs_ref, f_ref,
             fqn, fqt, fkn, fkt, fvn, fvt, sqn, sqt, skn, skt, svn, svt):
        lane = lax.broadcasted_iota(jnp.int32, (1, LANES), 1)
        f = f_ref[...]

        def head(ref, e):
            t = ref[...].astype(F32)
            if e == 1:
                t = pltpu.roll(t, HEAD_DIM, 1)
            return jnp.where(lane < HEAD_DIM, t, 0.0)

        def at(first):
            return jnp.logical_and(lane >= first, lane < first + F_PARTS)

        for e in range(2):
            parts = _bf16_parts(f[:, e:e + 1])
            f_lanes = sum(jnp.where(lane == Q_F_LANE + k, parts[k], 0.0) for k in range(F_PARTS))
            nf_lanes = sum(jnp.where(lane == Q_ONE_LANE + k, parts[k], 0.0) for k in range(F_PARTS))
            vals = (
                (fqn, fqt, LANES, head(qf_ref, e) * _SCALE + f_lanes + jnp.where(at(Q_ONE_LANE), 1.0, 0.0)),
                (fkn, fkt, LANES, head(kf_ref, e) + jnp.where(at(Q_F_LANE), 1.0, 0.0) - nf_lanes),
                (fvn, fvt, HEAD_DIM, head(vf_ref, e)),
                (sqn, sqt, LANES, head(qs_ref, e) * _SCALE),
                (skn, skt, LANES, head(ks_ref, e)),
                (svn, svt, HEAD_DIM, head(vs_ref, e)),
            )
            for n_ref, t_ref, rows, val in vals:
                n_ref[e] = val.astype(BF16)
                t_ref[e] = val.T[:rows].astype(BF16)

    col = lambda base: pl.BlockSpec((tm, LANES), lambda i, p: (i, base + p))
    n_spec = pl.BlockSpec((2, tm, LANES), lambda i, p: (p, i, 0))
    t_spec = lambda rows: pl.BlockSpec((2, rows, tm), lambda i, p: (p, 0, i))
    n_sds = jax.ShapeDtypeStruct((H, S, LANES), BF16)
    t_sds = lambda rows: jax.ShapeDtypeStruct((H, rows, S), BF16)
    group = ([n_sds, t_sds(LANES), n_sds, t_sds(LANES), n_sds, t_sds(HEAD_DIM)],
             [n_spec, t_spec(LANES), n_spec, t_spec(LANES), n_spec, t_spec(HEAD_DIM)])
    res = pl.pallas_call(
        body, name="att_prep", grid=(S // tm, n_pairs),
        out_shape=tuple(group[0] * 2),
        in_specs=[col(k * n_pairs) for k in range(6)] + [pl.BlockSpec((None, tm, 2), lambda i, p: (p, i, 0))],
        out_specs=tuple(group[1] * 2),
        compiler_params=_params(dimension_semantics=("parallel", "parallel")),
    )(qkv, qkv, qkv, qkv, qkv, qkv, f_pairs)
    names = ("q_n", "q_t", "k_n", "k_t", "v_n", "v_t")
    return dict(zip(names, res[:6])), dict(zip(names, res[6:]))


def _fox_reach(qt, fend_ref, kmax_ref, h):
    qf = qt.astype(F32)
    q_norm = jnp.sqrt(jnp.sum(jnp.square(qf[:HEAD_DIM]), axis=0, keepdims=True))
    f_t = jnp.sum(qf[Q_F_LANE:Q_F_LANE + F_PARTS], axis=0, keepdims=True)
    return lambda j: q_norm * kmax_ref[h, j] + f_t - fend_ref[h, j]


def _fox_fwd(q_t, k_n, v_t, f_end, k_max, shards):
    H, _, S = q_t.shape
    n, nq = len(shards), S // ATT_BQ

    def body(fend_ref, kmax_ref, qt_ref, k_ref, vt_ref, *rest):
        ins, (ot_ref, lse_ref), outs, sems = rest[:n], rest[n:n + 2], rest[n + 2:2 * n + 2], rest[2 * n + 2:]
        h, i = pl.program_id(0), pl.program_id(1)

        @pl.when(jnp.logical_and(h == 0, i == 0))
        def _():
            for cp in _exchange_copies(ins, outs, *sems, False):
                cp.start()

        qt = qt_ref[...]
        reach = _fox_reach(qt, fend_ref, kmax_ref, h)

        def step(j, carry, masked):
            m, l, acc = carry
            ks = _kv_slice(j)
            s = _dot(k_ref[ks, :], qt)
            if masked:
                s = jnp.where(_mask_t(False), s, _NEG)
            mn = jnp.maximum(m, jnp.max(s, axis=0, keepdims=True))
            alpha = jnp.exp(m - mn)
            p = jnp.exp(s - mn)
            l = alpha * l + jnp.sum(p, axis=0, keepdims=True)
            acc = acc * alpha + _dot(vt_ref[:, ks], p.astype(BF16))
            return mn, l, acc

        def alive(j, carry):
            return jnp.max(reach(j) - carry[0]) > SKIP_BELOW

        row = jnp.zeros((1, ATT_BQ), F32)
        m, l, acc = _walk_down(i, step, alive, (row + _NEG, row, jnp.zeros((HEAD_DIM, ATT_BQ), F32)))
        ot_ref[...] = acc / l
        lse_ref[...] = m + jnp.log(l)

        @pl.when(jnp.logical_and(h == H - 1, i == nq - 1))
        def _():
            for cp in _exchange_copies(ins, outs, *sems, False):
                cp.wait()

    any_spec = pl.BlockSpec(memory_space=pl.ANY)
    grid_spec = pltpu.PrefetchScalarGridSpec(
        num_scalar_prefetch=2, grid=(H, nq),
        in_specs=[_t_block(LANES), _n_full(S), _t_full(HEAD_DIM, S)] + [any_spec] * n,
        out_specs=tuple([_t_block(HEAD_DIM), _t_block(1)] + [any_spec] * n),
        scratch_shapes=_exchange_sems(n))
    res = pl.pallas_call(
        body, name="fox_fwd", grid_spec=grid_spec,
        out_shape=tuple([jax.ShapeDtypeStruct((H, HEAD_DIM, S), F32), jax.ShapeDtypeStruct((H, 1, S), F32)]
                        + _exchange_out_shapes(shards, False)),
        compiler_params=_params(dimension_semantics=("arbitrary", "arbitrary"), has_side_effects=True),
    )(f_end, k_max, q_t, k_n, v_t, *shards)
    return res[0], res[1], res[2:]


def _fox_bwd(q_t, q_n, k_n, k_t, v_n, do_t, do_n, o_t, lse, f_end, k_max):
    H, _, S = q_t.shape

    def body(fend_ref, kmax_ref, qt_ref, qn_ref, k_ref, kt_ref, v_ref, dot_ref, don_ref, ot_ref, lse_ref,
             dqt_ref, dk_ref, dv_ref):
        h, i = pl.program_id(0), pl.program_id(1)

        @pl.when(i == 0)
        def _():
            dk_ref[...] = jnp.zeros_like(dk_ref)
            dv_ref[...] = jnp.zeros_like(dv_ref)

        qt, qn, dot, don = qt_ref[...], qn_ref[...], dot_ref[...], don_ref[...]
        lse = lse_ref[...]
        delta = jnp.sum(dot[:HEAD_DIM].astype(F32) * ot_ref[...], axis=0, keepdims=True)
        reach = _fox_reach(qt, fend_ref, kmax_ref, h)

        def alive(j, dq):
            return jnp.max(reach(j) - lse) > SKIP_BELOW

        def step(j, dq, masked):
            ks = _kv_slice(j)
            s = _dot(k_ref[ks, :], qt)
            if masked:
                s = jnp.where(_mask_t(False), s, _NEG)
            p = jnp.exp(s - lse)
            ds = (p * (_dot(v_ref[ks, :], dot) - delta)).astype(BF16)
            dk_ref[ks, :] += _dot(ds, qn)
            dv_ref[ks, :] += _dot(p.astype(BF16), don)
            return dq + _dot(kt_ref[:, ks], ds)

        dqt_ref[...] = _walk_down(i, step, alive, jnp.zeros((LANES, ATT_BQ), F32))

    grid_spec = pltpu.PrefetchScalarGridSpec(
        num_scalar_prefetch=2, grid=(H, S // ATT_BQ),
        in_specs=[_t_block(LANES), _n_block(), _n_full(S), _t_full(LANES, S), _n_full(S),
                  _t_block(LANES), _n_block(), _t_block(HEAD_DIM), _t_block(1)],
        out_specs=(_t_block(LANES), _n_full(S), _n_full(S)))
    return pl.pallas_call(
        body, name="fox_bwd", grid_spec=grid_spec,
        out_shape=(jax.ShapeDtypeStruct((H, LANES, S), F32), jax.ShapeDtypeStruct((H, S, LANES), F32),
                   jax.ShapeDtypeStruct((H, S, LANES), F32)),
        compiler_params=_params(dimension_semantics=("parallel", "arbitrary")),
    )(f_end, k_max, q_t, q_n, k_n, k_t, v_n, do_t, do_n, o_t, lse)


def _scan_lhs():
    r = lax.broadcasted_iota(jnp.int32, (SCAN_BK, 2 * SCAN_BK), 0)
    c = lax.broadcasted_iota(jnp.int32, (SCAN_BK, 2 * SCAN_BK), 1) % SCAN_BK
    return (c >= r).astype(BF16)


def _suffix_sum(t, lhs):
    hi = t.astype(BF16)
    lo = (t - hi.astype(F32)).astype(BF16)
    return _dot(lhs, jnp.concatenate([hi, lo], axis=0))


def _sb_scores(k, qt, mask):
    z = _dot(k, qt)
    e = jnp.exp(-jnp.abs(z))
    lb = -(jnp.maximum(z, 0.0) + jnp.log(1.0 + e))
    if mask is not None:
        lb = jnp.where(mask, lb, 0.0)
    return z, e, lb


def _scan_blocks():
    return [slice(u * SCAN_BK, (u + 1) * SCAN_BK) for u in reversed(range(N_SCAN))]


def _sb_fwd(q_t, k_n, v_t):
    H, _, S = q_t.shape

    def body(qt_ref, k_ref, vt_ref, ot_ref):
        i = pl.program_id(1)
        qt = qt_ref[...]
        lhs = _scan_lhs()

        def step(j, carry, masked):
            c, acc = carry
            ks = _kv_slice(j)
            mask = _mask_t(True) if masked else None
            z, _, lb = _sb_scores(k_ref[ks, :], qt, mask)
            parts = []
            for sl in _scan_blocks():
                rin = _suffix_sum(lb[sl], lhs)
                a = jnp.exp(z[sl] + rin + c)
                if masked:
                    a = jnp.where(mask[sl], a, 0.0)
                parts.append(a.astype(BF16))
                c = c + rin[0:1, :]
            a_all = jnp.concatenate(parts[::-1], axis=0)
            return c, acc + _dot(vt_ref[:, ks], a_all)

        carry = (jnp.zeros((1, ATT_BQ), F32), jnp.zeros((HEAD_DIM, ATT_BQ), F32))
        ot_ref[...] = _walk_down(i, step, lambda j, cr: jnp.max(cr[0]) > SKIP_BELOW, carry)[1]

    return pl.pallas_call(
        body, name="sb_fwd", grid=(H, S // ATT_BQ),
        out_shape=jax.ShapeDtypeStruct((H, HEAD_DIM, S), F32),
        in_specs=[_t_block(LANES), _n_full(S), _t_full(HEAD_DIM, S)],
        out_specs=_t_block(HEAD_DIM),
        compiler_params=_params(dimension_semantics=("parallel", "parallel")),
    )(q_t, k_n, v_t)


def _sb_bwd(q_t, q_n, k_n, k_t, v_n, do_t, do_n, o_t, bound):
    H, _, S = q_t.shape
    n, nq = len(bound), S // ATT_BQ

    def body(qt_ref, qn_ref, k_ref, kt_ref, v_ref, dot_ref, don_ref, ot_ref, *rest):
        ins, (dqt_ref, dk_ref, dv_ref) = rest[:n], rest[n:n + 3]
        outs, sems = rest[n + 3:2 * n + 3], rest[2 * n + 3:]
        h, i = pl.program_id(0), pl.program_id(1)

        @pl.when(jnp.logical_and(h == 0, i == 0))
        def _():
            for cp in _exchange_copies(ins, outs, *sems, True):
                cp.start()

        @pl.when(i == 0)
        def _():
            dk_ref[...] = jnp.zeros_like(dk_ref)
            dv_ref[...] = jnp.zeros_like(dv_ref)

        qt, qn, dot, don = qt_ref[...], qn_ref[...], dot_ref[...], don_ref[...]
        lhs = _scan_lhs()
        delta = jnp.sum(dot[:HEAD_DIM].astype(F32) * ot_ref[...], axis=0, keepdims=True)

        def step(j, carry, masked):
            c, g, dq = carry
            ks = _kv_slice(j)
            mask = _mask_t(True) if masked else None
            z, e, lb = _sb_scores(k_ref[ks, :], qt, mask)
            da = _dot(v_ref[ks, :], dot)
            a_parts, dz_parts = [], []
            for sl in _scan_blocks():
                rin = _suffix_sum(lb[sl], lhs)
                a = jnp.exp(z[sl] + rin + c)
                if masked:
                    a = jnp.where(mask[sl], a, 0.0)
                ab = a.astype(BF16)
                gg = ab.astype(F32) * da[sl]
                rgin = _suffix_sum(gg, lhs)
                rinv = 1.0 / (1.0 + e[sl])
                sig = jnp.where(z[sl] >= 0.0, rinv, e[sl] * rinv)
                dz = gg - sig * (delta - g - (rgin - gg))
                if masked:
                    dz = jnp.where(mask[sl], dz, 0.0)
                a_parts.append(ab)
                dz_parts.append(dz.astype(BF16))
                c = c + rin[0:1, :]
                g = g + rgin[0:1, :]
            ab_all = jnp.concatenate(a_parts[::-1], axis=0)
            dzb = jnp.concatenate(dz_parts[::-1], axis=0)
            dk_ref[ks, :] += _dot(dzb, qn)
            dv_ref[ks, :] += _dot(ab_all, don)
            return c, g, dq + _dot(kt_ref[:, ks], dzb)

        row = jnp.zeros((1, ATT_BQ), F32)
        carry = (row, row, jnp.zeros((LANES, ATT_BQ), F32))
        dqt_ref[...] = _walk_down(i, step, lambda j, cr: jnp.max(cr[0]) > SKIP_BELOW, carry)[2]

        @pl.when(jnp.logical_and(h == H - 1, i == nq - 1))
        def _():
            for cp in _exchange_copies(ins, outs, *sems, True):
                cp.wait()

    any_spec = pl.BlockSpec(memory_space=pl.ANY)
    res = pl.pallas_call(
        body, name="sb_bwd", grid=(H, nq),
        out_shape=tuple([jax.ShapeDtypeStruct((H, LANES, S), F32), jax.ShapeDtypeStruct((H, S, LANES), F32),
                         jax.ShapeDtypeStruct((H, S, LANES), F32)] + _exchange_out_shapes(bound, True)),
        in_specs=[_t_block(LANES), _n_block(), _n_full(S), _t_full(LANES, S), _n_full(S),
                  _t_block(LANES), _n_block(), _t_block(HEAD_DIM)] + [any_spec] * n,
        out_specs=tuple([_t_block(LANES), _n_full(S), _n_full(S)] + [any_spec] * n),
        scratch_shapes=_exchange_sems(n),
        compiler_params=_params(dimension_semantics=("arbitrary", "arbitrary"), has_side_effects=True),
    )(q_t, q_n, k_n, k_t, v_n, do_t, do_n, o_t, *bound)
    return res[0], res[1], res[2], res[3:]


def _dqkv_assemble(dqf_t, dkf, dvf, dqs_t, dks, dvs):
    H, _, S = dqf_t.shape
    n_pairs = H // 2
    tm = _tile(S, 512)

    def body(dqf_ref, dkf_ref, dvf_ref, dqs_ref, dks_ref, dvs_ref, out_ref, dfk_ref):
        lane = lax.broadcasted_iota(jnp.int32, (1, LANES), 1)
        slabs = ((dqf_ref, True), (dkf_ref, False), (dvf_ref, False),
                 (dqs_ref, True), (dks_ref, False), (dvs_ref, False))
        for k, (ref, transposed) in enumerate(slabs):
            if transposed:
                t0, t1 = ref[0].T * _SCALE, ref[1].T * _SCALE
            else:
                t0, t1 = ref[0], ref[1]
            out_ref[k] = jnp.where(lane < HEAD_DIM, t0, pltpu.roll(t1, HEAD_DIM, 1)).astype(BF16)
        for e in range(2):
            dfk_ref[e] = dkf_ref[e].T[Q_ONE_LANE:Q_ONE_LANE + 1, :]

    t_spec = pl.BlockSpec((2, LANES, tm), lambda i, p: (p, 0, i))
    n_spec = pl.BlockSpec((2, tm, LANES), lambda i, p: (p, i, 0))
    return pl.pallas_call(
        body, name="dqkv_assemble", grid=(S // tm, n_pairs),
        out_shape=(jax.ShapeDtypeStruct((6, S, n_pairs * LANES), BF16), jax.ShapeDtypeStruct((H, 1, S), F32)),
        in_specs=[t_spec, n_spec, n_spec, t_spec, n_spec, n_spec],
        out_specs=(pl.BlockSpec((6, tm, LANES), lambda i, p: (0, i, p)),
                   pl.BlockSpec((2, 1, tm), lambda i, p: (p, 0, i))),
        compiler_params=_params(dimension_semantics=("parallel", "parallel")),
    )(dqf_t, dkf, dvf, dqs_t, dks, dvs)


def _acc_spec(w):
    return pl.BlockSpec((1, w), lambda i: (0, 0))


def _loss_head(x1, act, w_down, gate_m, g_final, target):
    S, D = x1.shape
    F = act.shape[1]
    tm = _tile(S, 512)

    def body(x1_ref, act_ref, w_ref, gt_ref, gf_ref, tg_ref, dx2_ref, gm_ref, loss_ref, dgf_ref, dgt_ref):
        @pl.when(pl.program_id(0) == 0)
        def _():
            loss_ref[...] = jnp.zeros_like(loss_ref)
            dgf_ref[...] = jnp.zeros_like(dgf_ref)
            dgt_ref[...] = jnp.zeros_like(dgt_ref)

        mo = _dot(act_ref[...], w_ref[...])
        x2 = x1_ref[...] + gt_ref[...] * mo
        r = lax.rsqrt(jnp.mean(x2 * x2, axis=-1, keepdims=True) + EPS)
        xh = x2 * r
        diff = xh * gf_ref[...] - tg_ref[...]
        loss_ref[...] += (0.5 / D) * jnp.sum(diff * diff)
        dy = diff * (1.0 / D)
        dgf_ref[...] += jnp.sum(dy * xh, axis=0, keepdims=True)
        dxh = dy * gf_ref[...]
        dx2 = r * (dxh - xh * jnp.mean(dxh * xh, axis=-1, keepdims=True))
        dx2_ref[...] = dx2
        gm_ref[...] = (dx2 * gt_ref[...]).astype(BF16)
        dgt_ref[...] += jnp.sum(dx2 * mo, axis=0, keepdims=True)

    return pl.pallas_call(
        body, name="loss_head", grid=(S // tm,),
        out_shape=(jax.ShapeDtypeStruct((S, D), F32), jax.ShapeDtypeStruct((S, D), BF16),
                   jax.ShapeDtypeStruct((1, LANES), F32), jax.ShapeDtypeStruct((1, D), F32),
                   jax.ShapeDtypeStruct((1, D), F32)),
        in_specs=_row_specs(tm, [D, F]) + [pl.BlockSpec((F, D), lambda i: (0, 0))] + [_vec_spec(D)] * 2
        + _row_specs(tm, [D]),
        out_specs=tuple(_row_specs(tm, [D, D]) + [_acc_spec(LANES), _acc_spec(D), _acc_spec(D)]),
        compiler_params=_params(dimension_semantics=("arbitrary",)),
    )(x1, act, w_down, gate_m, g_final, target)


def _norm_bwd(lhs, rhs, xin, dres, g, scale, name, gate=None, branch=None, bound=(), conv=None, tm=256):
    S, D = xin.shape
    tm = _tile(S, tm)
    gated = gate is not None
    nl, nb, n_steps = len(lhs), len(bound), S // tm
    nc_ = nl if conv else 0
    n_out = (6 if gated else 4) + nc_

    def body(*refs):
        l_refs, r_refs, rest = refs[:nl], refs[nl:2 * nl], refs[2 * nl:]
        halo_refs, cw_refs, rest = rest[:nc_], rest[nc_:2 * nc_], rest[2 * nc_:]
        if gated:
            x_ref, dr_ref, g_ref, sc_ref, gt_ref, br_ref = rest[:6]
            rest = rest[6:]
        else:
            x_ref, dr_ref, g_ref, sc_ref = rest[:4]
            rest = rest[4:]
        ins, outs, ex_outs, sems = rest[:nb], rest[nb:nb + n_out], rest[nb + n_out:2 * nb + n_out], rest[2 * nb + n_out:]
        dx_ref, dsc_ref, dsh_ref, dg_ref = outs[:4]
        sums = (dsc_ref, dsh_ref, dg_ref) + ((outs[5],) if gated else ())
        dup_refs = outs[n_out - nc_:]
        i = pl.program_id(0)

        @pl.when(i == 0)
        def _():
            for s_ref in sums:
                s_ref[...] = jnp.zeros_like(s_ref)
            if nb:
                for cp in _chip_exchange_copies(ins, ex_outs, *sems):
                    cp.start()

        dhv = None
        for k, (l_ref, r_ref) in enumerate(zip(l_refs, r_refs)):
            if conv:
                halo = jnp.where(i == n_steps - 1, 0.0, halo_refs[k][...].astype(F32))
                dup = _conv_bwd_taps(l_ref[...].astype(F32), halo, cw_refs[k]).astype(BF16)
                dup_refs[k][...] = dup
                terms = [_dot_nt(dup, r_ref[...])]
            elif len(l_ref.shape) == 3:
                K = l_ref.shape[2]
                terms = [_dot_nt(l_ref[k], r_ref[:, k * K:(k + 1) * K]) for k in range(l_ref.shape[0])]
            else:
                terms = [_dot_nt(l_ref[...], r_ref[...])]
            for t in terms:
                dhv = t if dhv is None else dhv + t
        xv = x_ref[...]
        r = lax.rsqrt(jnp.mean(xv * xv, axis=-1, keepdims=True) + EPS)
        xh = xv * r
        dsc_ref[...] += jnp.sum(dhv * (xh * g_ref[...]), axis=0, keepdims=True)
        dsh_ref[...] += jnp.sum(dhv, axis=0, keepdims=True)
        dn = dhv * (1.0 + sc_ref[...])
        dg_ref[...] += jnp.sum(dn * xh, axis=0, keepdims=True)
        dxh = dn * g_ref[...]
        dx = dr_ref[...] + r * (dxh - xh * jnp.mean(dxh * xh, axis=-1, keepdims=True))
        dx_ref[...] = dx
        if gated:
            outs[4][...] = (dx * gt_ref[...]).astype(BF16)
            outs[5][...] += jnp.sum(dx * br_ref[...], axis=0, keepdims=True)

        if nb:
            @pl.when(i == n_steps - 1)
            def _():
                for cp in _chip_exchange_copies(ins, ex_outs, *sems):
                    cp.wait()

    def l_spec(a):
        if a.ndim == 3:
            return pl.BlockSpec((a.shape[0], tm, a.shape[2]), lambda i: (0, i, 0))
        return pl.BlockSpec((tm, a.shape[1]), lambda i: (i, 0))

    any_spec = pl.BlockSpec(memory_space=pl.ANY)
    vec = jax.ShapeDtypeStruct((1, D), F32)
    out_shape = [jax.ShapeDtypeStruct((S, D), F32), vec, vec, vec]
    out_specs = _row_specs(tm, [D]) + [_acc_spec(D)] * 3
    in_specs = [l_spec(a) for a in lhs] + [pl.BlockSpec(b.shape, lambda i: (0, 0)) for b in rhs]
    args = list(lhs) + list(rhs)
    if conv:
        step, last_halo = tm // HALO, S // HALO - 1
        in_specs += [pl.BlockSpec((HALO, a.shape[1]), lambda i: (jnp.minimum((i + 1) * step, last_halo), 0))
                     for a in lhs]
        in_specs += [pl.BlockSpec(w.shape, lambda i: (0, 0)) for w in conv]
        args += list(lhs) + list(conv)
    in_specs += _row_specs(tm, [D, D]) + [_vec_spec(D)] * 2
    args += [xin, dres, g, scale]
    if gated:
        out_shape += [jax.ShapeDtypeStruct((S, D), BF16), vec]
        out_specs += _row_specs(tm, [D]) + [_acc_spec(D)]
        in_specs += [_vec_spec(D)] + _row_specs(tm, [D])
        args += [gate, branch]
    if conv:
        out_shape += [jax.ShapeDtypeStruct(a.shape, BF16) for a in lhs]
        out_specs += [l_spec(a) for a in lhs]
    res = pl.pallas_call(
        body, name=name, grid=(n_steps,),
        out_shape=tuple(out_shape + [jax.ShapeDtypeStruct(b.shape, b.dtype) for b in bound]),
        in_specs=in_specs + [any_spec] * nb, out_specs=tuple(out_specs + [any_spec] * nb),
        scratch_shapes=_chip_exchange_sems(nb) if nb else [],
        compiler_params=_params(dimension_semantics=("arbitrary",), has_side_effects=bool(nb)),
    )(*args, *bound)
    return tuple(res[:n_out]) + (tuple(res[n_out:]),)


def _headnorm_bwd(dmix, o_f, o_s, g_f, g_s):
    dh, S = o_f.shape
    H = dh // HEAD_DIM
    tm = _tile(S, 256)

    def body(dm_ref, of_ref, os_ref, gf_ref, gs_ref, fn_ref, ft_ref, sn_ref, st_ref, dgf_ref, dgs_ref):
        @pl.when(pl.program_id(0) == 0)
        def _():
            dgf_ref[...] = jnp.zeros_like(dgf_ref)
            dgs_ref[...] = jnp.zeros_like(dgs_ref)

        ones = _group_ones()
        lane = lax.broadcasted_iota(jnp.int32, (1, LANES), 1)
        parts = ((of_ref, gf_ref, fn_ref, ft_ref, dgf_ref), (os_ref, gs_ref, sn_ref, st_ref, dgs_ref))
        for part, (o_ref, g_ref, n_ref, t_ref, dg_ref) in enumerate(parts):
            for t in range(dh // LANES):
                cols = slice(t * LANES, (t + 1) * LANES)
                o = o_ref[cols, :].T
                dm = dm_ref[:, part * dh + t * LANES: part * dh + (t + 1) * LANES]
                r = lax.rsqrt(_group_sum(o * o, ones) * (1.0 / HEAD_DIM) + EPS)
                oh = o * r
                dg_ref[:, cols] += jnp.sum(dm * oh, axis=0, keepdims=True)
                dn = dm * g_ref[:, cols]
                mean = _group_sum(dn * oh, ones) * (1.0 / HEAD_DIM)
                do = r * (dn - oh * mean)
                for e in range(2):
                    d = do if e == 0 else pltpu.roll(do, HEAD_DIM, 1)
                    d = jnp.where(lane < HEAD_DIM, d, 0.0)
                    n_ref[2 * t + e] = d.astype(BF16)
                    t_ref[2 * t + e] = d.T.astype(BF16)

    vec = jax.ShapeDtypeStruct((1, dh), F32)
    n_sds = jax.ShapeDtypeStruct((H, S, LANES), BF16)
    t_sds = jax.ShapeDtypeStruct((H, LANES, S), BF16)
    n_spec = pl.BlockSpec((H, tm, LANES), lambda i: (0, i, 0))
    t_spec = pl.BlockSpec((H, LANES, tm), lambda i: (0, 0, i))
    return pl.pallas_call(
        body, name="headnorm_bwd", grid=(S // tm,),
        out_shape=(n_sds, t_sds, n_sds, t_sds, vec, vec),
        in_specs=_row_specs(tm, [2 * dh]) + [_col_spec(tm, dh)] * 2 + [_vec_spec(dh)] * 2,
        out_specs=(n_spec, t_spec, n_spec, t_spec, _acc_spec(dh), _acc_spec(dh)),
        compiler_params=_params(dimension_semantics=("arbitrary",)),
    )(dmix, o_f, o_s, g_f, g_s)


def _adamw(w, gslots, m, v, name):
    R, C = w.shape
    n = gslots.shape[0]
    tr = 256 if (R % 256 == 0 and R > 256) else R
    bc1 = 1.0 - ADAM_B1 ** ADAM_STEP
    bc2 = 1.0 - ADAM_B2 ** ADAM_STEP

    def body(w_ref, gs_ref, m_ref, v_ref, g_ref, d_ref, nm_ref, nv_ref):
        g = gs_ref[0]
        for s in range(1, n):
            g = g + gs_ref[s]
        nm = ADAM_B1 * m_ref[...] + (1.0 - ADAM_B1) * g
        nv = ADAM_B2 * v_ref[...] + (1.0 - ADAM_B2) * (g * g)
        g_ref[...] = g
        nm_ref[...] = nm
        nv_ref[...] = nv
        d_ref[...] = -ADAM_LR * ((nm / bc1) / (jnp.sqrt(nv / bc2) + ADAM_EPS) + ADAM_WD * w_ref[...])

    blk = pl.BlockSpec((tr, C), lambda i: (i, 0))
    sds = jax.ShapeDtypeStruct((R, C), F32)
    return pl.pallas_call(
        body, name=name, grid=(R // tr,), out_shape=(sds,) * 4,
        in_specs=[blk, pl.BlockSpec((n, tr, C), lambda i: (0, i, 0)), blk, blk], out_specs=(blk,) * 4,
        compiler_params=_params(dimension_semantics=("parallel",)),
    )(w, gslots, m, v)


def _slot_sum(slots, name):
    n, _, C = slots.shape

    def body(s_ref, o_ref):
        acc = s_ref[0]
        for s in range(1, n):
            acc = acc + s_ref[s]
        o_ref[...] = acc

    return pl.pallas_call(body, name=name, out_shape=jax.ShapeDtypeStruct((1, C), F32),
                          compiler_params=_params())(slots)


def _pad_cols(a, n):
    return jnp.pad(a, ((0, 0), (0, n - a.shape[1])))


def _ungather(g, axis):
    if axis == 0:
        return g.reshape(g.shape[0] * g.shape[1], g.shape[2])
    return jnp.transpose(g, (1, 0, 2)).reshape(g.shape[1], g.shape[0] * g.shape[2])


def _to_slots(full, axis):
    R, C = full.shape
    if axis == 0:
        return full.reshape(N_DEV, R // N_DEV, C)
    return jnp.transpose(full.reshape(R, N_DEV, C // N_DEV), (1, 0, 2))


def kernel(x, c, w_ada, b_ada, g_attn, w_in, b_fgate, g_out_fox, g_out_sb, w_out, g_mlp, w_up, conv_w, conv_b, w_down, g_final, loss_target, m_w_ada, m_b_ada, m_g_attn, m_w_in, m_b_fgate, m_g_out_fox, m_g_out_sb, m_w_out, m_g_mlp, m_w_up, m_conv_w, m_conv_b, m_w_down, m_g_final, v_w_ada, v_b_ada, v_g_attn, v_w_in, v_b_fgate, v_g_out_fox, v_g_out_sb, v_w_out, v_g_mlp, v_w_up, v_conv_w, v_conv_b, v_w_down, v_g_final):
    S, D = x.shape[1], x.shape[2]
    dh = D // 2
    n_heads = dh // HEAD_DIM
    n_qkv = 6 * dh
    ff = w_down.shape[1] * N_DEV
    ffp = -(-ff // (2 * LANES)) * (2 * LANES)
    nc = S // LANES
    me = 4 * lax.axis_index("x") + 2 * lax.axis_index("y") + lax.axis_index("c")
    xs, tgt = x[0], loss_target[0]

    c_all, win_g = _gather_two_level([c, w_in[0].astype(BF16)], name="gather_first")
    c_all = c_all.reshape(N_DEV, D)
    W_in = _ungather(win_g, 1)
    W_qkv, W_f = W_in[:, :n_qkv], _pad_cols(W_in[:, n_qkv:], LANES)
    cb_g, cb_v = _pad_cols(conv_b[:, :ff], ffp), _pad_cols(conv_b[:, ff:], ffp)

    n_ada = w_ada.shape[2]
    b_shard = lax.dynamic_slice(b_ada, (0, me * n_ada), (1, n_ada))
    mod_cols = _ada_fwd(c_all, w_ada[0], b_shard)
    (mod_g,) = _exchange([mod_cols], scatter=False, name="gather_mod")
    mod = lax.dynamic_index_in_dim(mod_g, me, axis=1, keepdims=False).reshape(6, 1, D)
    shift_a, scale_a, gate_a, shift_m, scale_m, gate_m = [mod[k] for k in range(6)]

    h1, h1_t = _prenorm(xs, g_attn, scale_a, shift_a, "prenorm_attn")
    qkv = _mm(h1, W_qkv, BF16, "proj_qkv")
    flog = _mm(h1, W_f, F32, "proj_fgate")
    zf = flog[:, :n_heads] + b_fgate
    z_rows = zf.T.reshape(n_heads * nc, LANES)
    f_rows = _fgate_fwd(z_rows, nc).reshape(n_heads, S)
    f_pairs = jnp.transpose(f_rows.reshape(n_heads // 2, 2, S), (0, 2, 1))
    fox, sb = _att_prep(qkv, f_pairs)
    f_end, k_max = _skip_bounds(qkv[:, dh:2 * dh], f_rows)
    of_t, lse, (wout_g, wup_g, wdown_g, convw_g) = _fox_fwd(
        fox["q_t"], fox["k_n"], fox["v_t"], f_end, k_max,
        [w_out[0].astype(BF16), w_up[0].astype(BF16), w_down[0].astype(BF16), conv_w[0]])
    W_out = _ungather(wout_g, 0)
    W_up = _ungather(wup_g, 1)
    W_g, W_v = _pad_cols(W_up[:, :ff], ffp), _pad_cols(W_up[:, ff:], ffp)
    W_down = jnp.pad(_ungather(wdown_g, 0), ((0, ffp - ff), (0, 0)))
    cw_full = _ungather(convw_g, 1)
    cw_g, cw_v = _pad_cols(cw_full[:, :ff], ffp), _pad_cols(cw_full[:, ff:], ffp)
    os_t = _sb_fwd(sb["q_t"], sb["k_n"], sb["v_t"])
    o_f, o_s = of_t.reshape(dh, S), os_t.reshape(dh, S)
    mix, mix_t = _headnorm_fwd(o_f, o_s, g_out_fox, g_out_sb)
    a_out = _mm(mix, W_out, F32, "proj_out")
    x1, h2, h2_t = _resid_prenorm(xs, a_out, gate_a, g_mlp, scale_m, shift_m)
    up_g, up_v, act, act_t = _mlp_up(h2, W_g, W_v, cw_g, cw_v, cb_g, cb_v)

    dx2, gm, loss_p, dg_final, dgate_m = _loss_head(x1, act, W_down, gate_m, g_final.reshape(1, D), tgt)
    dW_down = _mm_acc(act_t, gm, "bwd_down_w")
    du_g, du_v, p_g, p_v = _conv_act_bwd(gm, W_down, up_g, up_v, cw_g, cw_v, cb_g, cb_v)
    dx1, dscale_m, dshift_m, dg_mlp, ga, dgate_a, dup_g, dup_v, _ = _norm_bwd(
        [du_g, du_v], [W_g, W_v], x1, dx2, g_mlp, scale_m, "norm_mlp_bwd", gate=gate_a, branch=a_out,
        conv=[cw_g, cw_v])
    dW_g = _mm(h2_t, dup_g, F32, "bwd_up_w_g", tm=512, tn=256)
    dW_v = _mm(h2_t, dup_v, F32, "bwd_up_w_v", tm=512, tn=256)
    dmix = _mm(ga, W_out, F32, "bwd_out_act", nt=True)
    dW_out = _mm(mix_t, ga, F32, "bwd_out_w", tm=512, tn=256)
    dof_n, dof_t, dos_n, dos_t, dg_fox, dg_sb = _headnorm_bwd(dmix, o_f, o_s, g_out_fox, g_out_sb)
    dqf_t, dkf, dvf = _fox_bwd(fox["q_t"], fox["q_n"], fox["k_n"], fox["k_t"], fox["v_n"], dof_t, dof_n, of_t, lse,
                              f_end, k_max)
    dW_upf = jnp.concatenate([dW_g[:, :ff], dW_v[:, :ff]], axis=1)
    dcw = jnp.concatenate([p_g[:CONV_W, :ff], p_v[:CONV_W, :ff]], axis=1)
    dqs_t, dks, dvs, (s_out, s_up, s_down, s_cw) = _sb_bwd(
        sb["q_t"], sb["q_n"], sb["k_n"], sb["k_t"], sb["v_n"], dos_t, dos_n, os_t,
        [_to_slots(dW_out, 0), _to_slots(dW_upf, 1), _to_slots(dW_down[:ff], 0), _to_slots(dcw, 1)])
    dparts, dfk = _dqkv_assemble(dqf_t, dkf, dvf, dqs_t, dks, dvs)
    dz_rows, db_fgate = _fgate_bwd(dfk.reshape(n_heads * nc, LANES),
                                   dqf_t[:, Q_F_LANE, :].reshape(n_heads * nc, LANES), z_rows, nc)
    dzf = _pad_cols(dz_rows.reshape(n_heads, S).T, LANES).astype(BF16)
    dW_qkv = _mm_acc_parts(h1_t, dparts, "bwd_in_w")
    dW_f = _mm_acc(h1_t, dzf, "bwd_in_w_fgate")
    dW_in = jnp.concatenate([jnp.transpose(dW_qkv, (1, 0, 2)).reshape(D, n_qkv), dW_f[:, :n_heads]], axis=1)
    bound_in = _to_slots(dW_in, 1)
    bound_in = bound_in.reshape((N_CHIPS, 2) + bound_in.shape[1:])
    (got_in,) = _scatter_to_sibling([bound_in], "scatter_sibling")
    c_idx = lax.axis_index("c").astype(jnp.int32).reshape(1)
    grad_x, dscale_a, dshift_a, dg_attn, (s_in,) = _norm_bwd(
        [dparts, dzf], [W_qkv, W_f], xs, dx1, g_attn, scale_a, "norm_attn_bwd",
        bound=[_pair_add(bound_in, got_in, c_idx, "pair_add")], tm=512)

    dconv_b = jnp.concatenate([p_g[CONV_W:CONV_W + 1, :ff], p_v[CONV_W:CONV_W + 1, :ff]], axis=1)
    parts = [dshift_a, dscale_a, dgate_a, dshift_m, dscale_m, dgate_m,
             dg_attn, db_fgate.reshape(1, n_heads), dg_fox, dg_sb, dg_mlp, dconv_b, dg_final,
             loss_p[:, :1]]
    sizes = [p.shape[1] for p in parts]
    vec = jnp.concatenate(parts, axis=1)
    n_vec = -(-vec.shape[1] // LANES) * LANES
    vec = _pad_cols(vec, n_vec)
    (vec_g,) = _exchange([vec], scatter=False, name="gather_small")
    offs = [0]
    for s in sizes:
        offs.append(offs[-1] + s)

    def small(k0, k1=None):
        k1 = k0 if k1 is None else k1
        return vec_g[:, :, offs[k0]:offs[k1 + 1]]

    dmod_all = small(0, 5).reshape(N_DEV, 6 * D)
    dmod_cols = lax.dynamic_slice(dmod_all, (0, me * n_ada), (N_DEV, n_ada))
    dW_ada = _ada_bwd(c_all.T, dmod_cols)


    res = {}
    res["w_ada"] = _adamw(w_ada[0], dW_ada[None], m_w_ada[0], v_w_ada[0], "adamw_w_ada")
    res["w_in"] = _adamw(w_in[0], s_in, m_w_in[0], v_w_in[0], "adamw_w_in")
    res["w_out"] = _adamw(w_out[0], s_out, m_w_out[0], v_w_out[0], "adamw_w_out")
    res["w_up"] = _adamw(w_up[0], s_up, m_w_up[0], v_w_up[0], "adamw_w_up")
    res["w_down"] = _adamw(w_down[0], s_down, m_w_down[0], v_w_down[0], "adamw_w_down")
    res["conv_w"] = _adamw(conv_w[0], s_cw, m_conv_w[0], v_conv_w[0], "adamw_conv_w")
    small_names = ["b_ada", "g_attn", "b_fgate", "g_out_fox", "g_out_sb", "g_mlp", "conv_b", "g_final"]
    small_w = [b_ada, g_attn, b_fgate, g_out_fox, g_out_sb, g_mlp, conv_b, g_final.reshape(1, D)]
    small_m = [m_b_ada, m_g_attn, m_b_fgate, m_g_out_fox, m_g_out_sb, m_g_mlp, m_conv_b, m_g_final.reshape(1, D)]
    small_v = [v_b_ada, v_g_attn, v_b_fgate, v_g_out_fox, v_g_out_sb, v_g_mlp, v_conv_b, v_g_final.reshape(1, D)]
    small_res = _adamw(jnp.concatenate(small_w, axis=1), small(0, 12), jnp.concatenate(small_m, axis=1),
                       jnp.concatenate(small_v, axis=1), "adamw_small")
    lo = 0
    for nm, wv in zip(small_names, small_w):
        res[nm] = tuple(r[:, lo:lo + wv.shape[1]] for r in small_res)
        lo += wv.shape[1]
    loss = _slot_sum(_pad_cols(small(13).reshape(N_DEV, 1), LANES).reshape(N_DEV, 1, LANES), "loss_sum")[0, 0]

    names = ["w_ada", "b_ada", "g_attn", "w_in", "b_fgate", "g_out_fox", "g_out_sb", "w_out", "g_mlp",
             "w_up", "conv_w", "conv_b", "w_down", "g_final"]

    def shaped(n, a):
        if n == "g_final":
            return a.reshape(D)
        if n in ("b_ada", "g_attn", "b_fgate", "g_out_fox", "g_out_sb", "g_mlp", "conv_b"):
            return a
        return a[None]

    outs = [loss, grad_x[None]]
    for k in range(4):
        outs += [shaped(n, res[n][k]) for n in names]
    return tuple(outs)
```

```python
import jax
import jax.numpy as jnp
from jax import lax
from jax.experimental import pallas as pl
from jax.experimental.pallas import tpu as pltpu

F32 = jnp.float32
BF16 = jnp.bfloat16
HIGHEST = lax.Precision.HIGHEST

N_DEV = 8
LANES = 128
HEAD_DIM = 64
EPS = 1e-6
CONV_W = 3
CONV_COLS = 1408
HALO = 16
ATT_BQ = 512
ATT_BK = 512
SCAN_BK = 128
VMEM_LIMIT = 56 * 1024 * 1024

ADAM_LR = 0.001
ADAM_B1 = 0.9
ADAM_B2 = 0.999
ADAM_EPS = 1e-08
ADAM_WD = 0.01
ADAM_STEP = 10


def _params(**kw):
    return pltpu.CompilerParams(vmem_limit_bytes=VMEM_LIMIT, **kw)


def _tile(n, cap):
    if n <= cap:
        return n
    best = None
    for t in range(LANES, cap + 1, LANES):
        if n % t == 0:
            best = t
    assert best is not None, (n, cap)
    return best


def _dot(a, b, **kw):
    return jnp.dot(a, b, preferred_element_type=F32, **kw)


def _exchange_copies(ins, outs, send_sems, recv_sems, loc_sems, scatter):
    n = len(ins)
    if n == 0:
        return []
    x, y, c = lax.axis_index("x"), lax.axis_index("y"), lax.axis_index("c")
    me = 4 * x + 2 * y + c
    copies = []
    for a in range(n):
        src = ins[a].at[me] if scatter else ins[a]
        copies.append(pltpu.make_async_copy(src, outs[a].at[me], loc_sems.at[a]))
    for k in range(1, N_DEV):
        px = 1 - x if k & 4 else x
        py = 1 - y if k & 2 else y
        pc = 1 - c if k & 1 else c
        peer = 4 * px + 2 * py + pc
        for a in range(n):
            src = ins[a].at[peer] if scatter else ins[a]
            copies.append(pltpu.make_async_remote_copy(
                src_ref=src, dst_ref=outs[a].at[me],
                send_sem=send_sems.at[a, k - 1], recv_sem=recv_sems.at[a, k - 1],
                device_id=(px, py, pc), device_id_type=pl.DeviceIdType.MESH))
    return copies


def _exchange_out_shapes(arrays, scatter):
    return [jax.ShapeDtypeStruct((N_DEV,) + tuple(a.shape[1:] if scatter else a.shape), a.dtype) for a in arrays]


def _exchange_sems(n):
    return [pltpu.SemaphoreType.DMA((n, N_DEV - 1)), pltpu.SemaphoreType.DMA((n, N_DEV - 1)),
            pltpu.SemaphoreType.DMA((n,))]


def _exchange(arrays, scatter, name):
    n = len(arrays)

    def body(*refs):
        copies = _exchange_copies(refs[:n], refs[n:2 * n], *refs[2 * n:], scatter)
        for cp in copies:
            cp.start()
        for cp in copies:
            cp.wait()

    any_spec = pl.BlockSpec(memory_space=pl.ANY)
    return pl.pallas_call(
        body, name=name, out_shape=tuple(_exchange_out_shapes(arrays, scatter)),
        in_specs=[any_spec] * n, out_specs=tuple([any_spec] * n),
        scratch_shapes=_exchange_sems(n),
        compiler_params=pltpu.CompilerParams(has_side_effects=True),
    )(*arrays)


def _gather_two_level(arrays, name):
    n = len(arrays)
    out_shape = [jax.ShapeDtypeStruct((N_DEV,) + tuple(a.shape), a.dtype) for a in arrays]

    def body(*refs):
        ins, outs = refs[:n], refs[n:2 * n]
        send_sems, recv_sems, loc_sems = refs[2 * n:]
        x, y, c = lax.axis_index("x"), lax.axis_index("y"), lax.axis_index("c")
        me, sibling = (x, y, c), (x, y, 1 - c)
        chips = [(1 - x, y), (x, 1 - y), (1 - x, 1 - y)]

        def slot(px, py, pc):
            return 4 * px + 2 * py + pc

        def copy(a, k, block, to, src=None):
            dst = outs[a].at[slot(*block)]
            return pltpu.make_async_remote_copy(
                src_ref=dst if src is None else src, dst_ref=dst,
                send_sem=send_sems.at[a, k], recv_sem=recv_sems.at[a, k],
                device_id=to, device_id_type=pl.DeviceIdType.MESH)

        local = [pltpu.make_async_copy(ins[a], outs[a].at[slot(*me)], loc_sems.at[a]) for a in range(n)]
        for cp in local:
            cp.start()
        first = []
        for a in range(n):
            first.append(copy(a, 0, me, sibling, src=ins[a]))
            first += [copy(a, 1 + j, me, (*chip, c), src=ins[a]) for j, chip in enumerate(chips)]
        for cp in first:
            cp.start()
        passed = []
        for j, chip in enumerate(chips):
            for a in range(n):
                copy(a, 1 + j, (*chip, c), me).wait_recv()
                cp = copy(a, 4 + j, (*chip, c), sibling)
                cp.start()
                passed.append(cp)
        for a in range(n):
            copy(a, 0, sibling, me).wait_recv()
            for j, chip in enumerate(chips):
                copy(a, 4 + j, (*chip, 1 - c), me).wait_recv()
        for cp in first + passed:
            cp.wait_send()
        for cp in local:
            cp.wait()

    any_spec = pl.BlockSpec(memory_space=pl.ANY)
    return pl.pallas_call(
        body, name=name, out_shape=tuple(out_shape),
        in_specs=[any_spec] * n, out_specs=tuple([any_spec] * n),
        scratch_shapes=[pltpu.SemaphoreType.DMA((n, N_DEV - 1)), pltpu.SemaphoreType.DMA((n, N_DEV - 1)),
                        pltpu.SemaphoreType.DMA((n,))],
        compiler_params=pltpu.CompilerParams(has_side_effects=True),
    )(*arrays)


N_CHIPS = 4


def _scatter_to_sibling(arrays, name):
    n = len(arrays)
    out_shape = [jax.ShapeDtypeStruct((N_CHIPS,) + tuple(a.shape[2:]), a.dtype) for a in arrays]

    def body(*refs):
        ins, outs = refs[:n], refs[n:2 * n]
        send_sems, recv_sems = refs[2 * n:]
        x, y, c = lax.axis_index("x"), lax.axis_index("y"), lax.axis_index("c")
        copies = []
        for a in range(n):
            for q in range(N_CHIPS):
                cp = pltpu.make_async_remote_copy(
                    src_ref=ins[a].at[q, 1 - c], dst_ref=outs[a].at[q],
                    send_sem=send_sems.at[a, q], recv_sem=recv_sems.at[a, q],
                    device_id=(x, y, 1 - c), device_id_type=pl.DeviceIdType.MESH)
                cp.start()
                copies.append(cp)
        for cp in copies:
            cp.wait()

    any_spec = pl.BlockSpec(memory_space=pl.ANY)
    return pl.pallas_call(
        body, name=name, out_shape=tuple(out_shape),
        in_specs=[any_spec] * n, out_specs=tuple([any_spec] * n),
        scratch_shapes=[pltpu.SemaphoreType.DMA((n, N_CHIPS)), pltpu.SemaphoreType.DMA((n, N_CHIPS))],
        compiler_params=pltpu.CompilerParams(has_side_effects=True),
    )(*arrays)


def _pair_add(mine, got, c_idx, name):
    _, _, R, C = mine.shape
    tr = 256 if (R % 256 == 0 and R > 256) else R

    def body(c_ref, m_ref, g_ref, o_ref):
        o_ref[...] = m_ref[...] + g_ref[...]

    grid_spec = pltpu.PrefetchScalarGridSpec(
        num_scalar_prefetch=1, grid=(N_CHIPS, R // tr),
        in_specs=[pl.BlockSpec((None, None, tr, C), lambda q, i, c_ref: (q, c_ref[0], i, 0)),
                  pl.BlockSpec((None, tr, C), lambda q, i, c_ref: (q, i, 0))],
        out_specs=pl.BlockSpec((None, tr, C), lambda q, i, c_ref: (q, i, 0)))
    return pl.pallas_call(
        body, name=name, grid_spec=grid_spec, out_shape=jax.ShapeDtypeStruct((N_CHIPS, R, C), mine.dtype),
        compiler_params=_params(dimension_semantics=("parallel", "parallel")),
    )(c_idx, mine, got)


def _chip_exchange_copies(ins, outs, send_sems, recv_sems, loc_sems):
    n = len(ins)
    x, y, c = lax.axis_index("x"), lax.axis_index("y"), lax.axis_index("c")
    myq = 2 * x + y
    copies = [pltpu.make_async_copy(ins[a].at[myq], outs[a].at[myq], loc_sems.at[a]) for a in range(n)]
    for k in range(1, N_CHIPS):
        qx = 1 - x if k & 2 else x
        qy = 1 - y if k & 1 else y
        for a in range(n):
            copies.append(pltpu.make_async_remote_copy(
                src_ref=ins[a].at[2 * qx + qy], dst_ref=outs[a].at[myq],
                send_sem=send_sems.at[a, k - 1], recv_sem=recv_sems.at[a, k - 1],
                device_id=(qx, qy, c), device_id_type=pl.DeviceIdType.MESH))
    return copies


def _chip_exchange_sems(n):
    return [pltpu.SemaphoreType.DMA((n, N_CHIPS - 1)), pltpu.SemaphoreType.DMA((n, N_CHIPS - 1)),
            pltpu.SemaphoreType.DMA((n,))]


def _dot_nt(a, b):
    return lax.dot_general(a, b, (((1,), (1,)), ((), ())), preferred_element_type=F32)


def _rhs_spec(b, tn, nt):
    if nt:
        return pl.BlockSpec((tn, b.shape[1]), lambda i, j: (j, 0))
    return pl.BlockSpec((b.shape[0], tn), lambda i, j: (0, j))


def _mm(a, b, out_dtype, name, tm=1024, tn=512, nt=False):
    M, K = a.shape
    N = b.shape[0] if nt else b.shape[1]
    tm, tn = _tile(M, tm), _tile(N, tn)
    dot = _dot_nt if nt else _dot

    def body(a_ref, b_ref, o_ref):
        o_ref[...] = dot(a_ref[...], b_ref[...]).astype(out_dtype)

    return pl.pallas_call(
        body, name=name, out_shape=jax.ShapeDtypeStruct((M, N), out_dtype),
        grid=(M // tm, N // tn),
        in_specs=[pl.BlockSpec((tm, K), lambda i, j: (i, 0)), _rhs_spec(b, tn, nt)],
        out_specs=pl.BlockSpec((tm, tn), lambda i, j: (i, j)),
        compiler_params=_params(dimension_semantics=("parallel", "parallel")),
    )(a, b)


def _mm_acc(a, b, name, tm=1408, tn=1408, tk=1024):
    M, S = a.shape
    _, N = b.shape
    tm, tn, tk = _tile(M, tm), _tile(N, tn), _tile(S, tk)

    def body(a_ref, b_ref, o_ref):
        @pl.when(pl.program_id(2) == 0)
        def _():
            o_ref[...] = jnp.zeros_like(o_ref)

        o_ref[...] += _dot(a_ref[...], b_ref[...])

    return pl.pallas_call(
        body, name=name, out_shape=jax.ShapeDtypeStruct((M, N), F32),
        grid=(M // tm, N // tn, S // tk),
        in_specs=[pl.BlockSpec((tm, tk), lambda i, j, k: (i, k)), pl.BlockSpec((tk, tn), lambda i, j, k: (k, j))],
        out_specs=pl.BlockSpec((tm, tn), lambda i, j, k: (i, j)),
        compiler_params=_params(dimension_semantics=("parallel", "parallel", "arbitrary")),
    )(a, b)


def _mm_acc_parts(a, parts, name, tm=1024, tk=1024):
    M, S = a.shape
    P, _, K = parts.shape
    tm, tk = _tile(M, tm), _tile(S, tk)

    def body(a_ref, b_ref, o_ref):
        @pl.when(pl.program_id(2) == 0)
        def _():
            o_ref[...] = jnp.zeros_like(o_ref)

        o_ref[...] += _dot(a_ref[...], b_ref[...])

    return pl.pallas_call(
        body, name=name, out_shape=jax.ShapeDtypeStruct((P, M, K), F32), grid=(P, M // tm, S // tk),
        in_specs=[pl.BlockSpec((tm, tk), lambda k, i, s: (i, s)), pl.BlockSpec((None, tk, K), lambda k, i, s: (k, s, 0))],
        out_specs=pl.BlockSpec((None, tm, K), lambda k, i, s: (k, i, 0)),
        compiler_params=_params(dimension_semantics=("parallel", "parallel", "arbitrary")),
    )(a, parts)


def _silu(z):
    return z * (1.0 / (1.0 + jnp.exp(-z)))


def _ada_fwd(c_all, w_shard, b_shard):
    n = w_shard.shape[1]

    def body(c_ref, w_ref, b_ref, o_ref):
        o_ref[...] = _dot(_silu(c_ref[...]), w_ref[...], precision=HIGHEST) + b_ref[...]

    return pl.pallas_call(body, name="ada_fwd", out_shape=jax.ShapeDtypeStruct((N_DEV, n), F32),
                          compiler_params=_params())(c_all, w_shard, b_shard)


def _ada_bwd(c_all_t, dmod_cols):
    D = c_all_t.shape[0]
    n = dmod_cols.shape[1]

    def body(ct_ref, dm_ref, o_ref):
        sc = _silu(ct_ref[...])
        dm = dm_ref[...]
        acc = sc[:, 0:1] * dm[0:1, :]
        for b in range(1, N_DEV):
            acc = acc + sc[:, b:b + 1] * dm[b:b + 1, :]
        o_ref[...] = acc

    return pl.pallas_call(body, name="ada_bwd", out_shape=jax.ShapeDtypeStruct((D, n), F32),
                          compiler_params=_params())(c_all_t, dmod_cols)


def _row_specs(tm, widths):
    return [pl.BlockSpec((tm, w), lambda i: (i, 0)) for w in widths]


def _vec_spec(w):
    return pl.BlockSpec((1, w), lambda i: (0, 0))


def _col_spec(tm, w):
    return pl.BlockSpec((w, tm), lambda i: (0, i))


def _prenorm(x, g, scale, shift, name):
    S, D = x.shape
    tm = _tile(S, 512)

    def body(x_ref, g_ref, sc_ref, sh_ref, h_ref, ht_ref):
        xv = x_ref[...]
        r = lax.rsqrt(jnp.mean(xv * xv, axis=-1, keepdims=True) + EPS)
        h = (xv * r) * g_ref[...] * (1.0 + sc_ref[...]) + sh_ref[...]
        h_ref[...] = h.astype(BF16)
        ht_ref[...] = h.T.astype(BF16)

    return pl.pallas_call(
        body, name=name, grid=(S // tm,),
        out_shape=(jax.ShapeDtypeStruct((S, D), BF16), jax.ShapeDtypeStruct((D, S), BF16)),
        in_specs=_row_specs(tm, [D]) + [_vec_spec(D)] * 3,
        out_specs=(_row_specs(tm, [D])[0], _col_spec(tm, D)),
        compiler_params=_params(dimension_semantics=("parallel",)),
    )(x, g, scale, shift)


def _group_ones():
    r = (lax.broadcasted_iota(jnp.int32, (2 * LANES, LANES), 0) % LANES) // HEAD_DIM
    c = lax.broadcasted_iota(jnp.int32, (2 * LANES, LANES), 1) // HEAD_DIM
    return (r == c).astype(BF16)


def _group_sum(t, ones):
    hi = t.astype(BF16)
    lo = (t - hi.astype(F32)).astype(BF16)
    return _dot(jnp.concatenate([hi, lo], axis=1), ones)


def _headnorm_fwd(o_f, o_s, g_f, g_s):
    dh, S = o_f.shape
    tm = _tile(S, 512)

    def body(of_ref, os_ref, gf_ref, gs_ref, mix_ref, mixt_ref):
        ones = _group_ones()
        for part, (o_ref, g_ref) in enumerate(((of_ref, gf_ref), (os_ref, gs_ref))):
            for t in range(dh // LANES):
                cols = slice(t * LANES, (t + 1) * LANES)
                out = slice(part * dh + t * LANES, part * dh + (t + 1) * LANES)
                o = o_ref[cols, :].T
                ms = _group_sum(o * o, ones) * (1.0 / HEAD_DIM)
                mixn = o * lax.rsqrt(ms + EPS) * g_ref[:, cols]
                mix_ref[:, out] = mixn.astype(BF16)
                mixt_ref[out, :] = mixn.T.astype(BF16)

    return pl.pallas_call(
        body, name="headnorm_fwd", grid=(S // tm,),
        out_shape=(jax.ShapeDtypeStruct((S, 2 * dh), BF16), jax.ShapeDtypeStruct((2 * dh, S), BF16)),
        in_specs=[_col_spec(tm, dh)] * 2 + [_vec_spec(dh)] * 2,
        out_specs=(_row_specs(tm, [2 * dh])[0], _col_spec(tm, 2 * dh)),
        compiler_params=_params(dimension_semantics=("parallel",)),
    )(o_f, o_s, g_f, g_s)


def _resid_prenorm(x, a_out, gate, g, scale, shift):
    S, D = x.shape
    tm = _tile(S, 512)

    def body(x_ref, a_ref, gt_ref, g_ref, sc_ref, sh_ref, x1_ref, h_ref, ht_ref):
        x1 = x_ref[...] + gt_ref[...] * a_ref[...]
        x1_ref[...] = x1
        r = lax.rsqrt(jnp.mean(x1 * x1, axis=-1, keepdims=True) + EPS)
        h = (x1 * r) * g_ref[...] * (1.0 + sc_ref[...]) + sh_ref[...]
        h_ref[...] = h.astype(BF16)
        ht_ref[...] = h.T.astype(BF16)

    return pl.pallas_call(
        body, name="resid_prenorm", grid=(S // tm,),
        out_shape=(jax.ShapeDtypeStruct((S, D), F32), jax.ShapeDtypeStruct((S, D), BF16),
                   jax.ShapeDtypeStruct((D, S), BF16)),
        in_specs=_row_specs(tm, [D, D]) + [_vec_spec(D)] * 4,
        out_specs=tuple(_row_specs(tm, [D, D]) + [_col_spec(tm, D)]),
        compiler_params=_params(dimension_semantics=("parallel",)),
    )(x, a_out, gate, g, scale, shift)


def _shift_down(main, halo, k):
    ext = jnp.concatenate([halo, main], axis=0)
    return pltpu.roll(ext, k, 0)[halo.shape[0]:]


def _shift_up(main, halo, k):
    ext = jnp.concatenate([main, halo], axis=0)
    n = ext.shape[0]
    return pltpu.roll(ext, n - k, 0)[:main.shape[0]]


def _conv(up, up_halo, w_ref, b_ref):
    return (w_ref[2:3, :] * up + w_ref[1:2, :] * _shift_down(up, up_halo, 1)
            + w_ref[0:1, :] * _shift_down(up, up_halo, 2) + b_ref[...])


def _prev_halo_map(tm):
    step = tm // HALO
    return lambda j, i: (jnp.maximum(i * step - 1, 0), j)


MLP_TM = 512
MLP_CT = 1408
CARRY = 8


def _mlp_up(h, wg, wv, cwg, cwv, cbg, cbv):
    S, D = h.shape
    F = wg.shape[1]
    tm, ct = _tile(S, MLP_TM), _tile(F, MLP_CT)
    nct = F // ct

    def body(h_ref, wg_ref, wv_ref, cwg_ref, cwv_ref, cbg_ref, cbv_ref,
             upg_ref, upv_ref, act_ref, actt_ref, hg_scr, hv_scr):
        i, j = pl.program_id(0), pl.program_id(1)
        hv = h_ref[...]
        us = []
        for w_ref, cw_ref, cb_ref, up_ref, scr in ((wg_ref, cwg_ref, cbg_ref, upg_ref, hg_scr),
                                                   (wv_ref, cwv_ref, cbv_ref, upv_ref, hv_scr)):
            up = _dot(hv, w_ref[...]).astype(BF16)
            up_ref[...] = up
            upf = up.astype(F32)
            halo = jnp.where(i == 0, 0.0, scr[j])
            us.append(_conv(upf, halo, cw_ref, cb_ref))
            scr[j] = upf[tm - CARRY:, :]
        act = _silu(us[0]) * us[1]
        act_ref[...] = act.astype(BF16)
        actt_ref[...] = act.T.astype(BF16)

    blk = pl.BlockSpec((tm, ct), lambda i, j: (i, j))
    wspec = pl.BlockSpec((D, ct), lambda i, j: (0, j))
    cwspec = pl.BlockSpec((CONV_W, ct), lambda i, j: (0, j))
    cbspec = pl.BlockSpec((1, ct), lambda i, j: (0, j))
    sds = jax.ShapeDtypeStruct((S, F), BF16)
    return pl.pallas_call(
        body, name="mlp_up", grid=(S // tm, nct),
        out_shape=(sds, sds, sds, jax.ShapeDtypeStruct((F, S), BF16)),
        in_specs=[pl.BlockSpec((tm, D), lambda i, j: (i, 0)), wspec, wspec, cwspec, cwspec, cbspec, cbspec],
        out_specs=(blk, blk, blk, pl.BlockSpec((ct, tm), lambda i, j: (j, i))),
        scratch_shapes=[pltpu.VMEM((nct, CARRY, ct), F32), pltpu.VMEM((nct, CARRY, ct), F32)],
        compiler_params=_params(dimension_semantics=("arbitrary", "arbitrary")),
    )(h, wg, wv, cwg, cwv, cbg, cbv)


def _conv_act_bwd(gm, w_down, up_g, up_v, cwg, cwv, cbg, cbv):
    S, F = up_g.shape
    D = gm.shape[1]
    tm, ct = _tile(S, 256), _tile(F, CONV_COLS)
    nct = F // ct

    def body(gm_ref, wd_ref, ug_ref, uv_ref, hg_ref, hv_ref, wg_ref, wv_ref, bg_ref, bv_ref,
             dug_ref, duv_ref, pg_ref, pv_ref):
        first = pl.program_id(1) == 0

        @pl.when(first)
        def _():
            pg_ref[...] = jnp.zeros_like(pg_ref)
            pv_ref[...] = jnp.zeros_like(pv_ref)

        da = _dot_nt(gm_ref[...], wd_ref[...])
        taps = []
        for u_ref, h_ref in ((ug_ref, hg_ref), (uv_ref, hv_ref)):
            h = jnp.where(first, 0.0, h_ref[...].astype(F32))
            uu = u_ref[...].astype(F32)
            taps.append((_shift_down(uu, h, 2), _shift_down(uu, h, 1), uu))
        u_g = wg_ref[0:1, :] * taps[0][0] + wg_ref[1:2, :] * taps[0][1] + wg_ref[2:3, :] * taps[0][2] + bg_ref[...]
        u_v = wv_ref[0:1, :] * taps[1][0] + wv_ref[1:2, :] * taps[1][1] + wv_ref[2:3, :] * taps[1][2] + bv_ref[...]
        sg = 1.0 / (1.0 + jnp.exp(-u_g))
        du_g = da * u_v * (sg * (1.0 + u_g * (1.0 - sg)))
        du_v = da * (u_g * sg)
        dug_ref[...] = du_g.astype(BF16)
        duv_ref[...] = du_v.astype(BF16)
        for du, tp, p_ref in ((du_g, taps[0], pg_ref), (du_v, taps[1], pv_ref)):
            for k in range(CONV_W):
                p_ref[k:k + 1, :] += jnp.sum(du * tp[k], axis=0, keepdims=True)
            p_ref[CONV_W:CONV_W + 1, :] += jnp.sum(du, axis=0, keepdims=True)

    main = pl.BlockSpec((tm, ct), lambda j, i: (i, j))
    halo = pl.BlockSpec((HALO, ct), _prev_halo_map(tm))
    wspec = pl.BlockSpec((CONV_W, ct), lambda j, i: (0, j))
    bspec = pl.BlockSpec((1, ct), lambda j, i: (0, j))
    pspec = pl.BlockSpec((8, ct), lambda j, i: (0, j))
    return pl.pallas_call(
        body, name="conv_act_bwd", grid=(nct, S // tm),
        out_shape=(jax.ShapeDtypeStruct((S, F), BF16), jax.ShapeDtypeStruct((S, F), BF16),
                   jax.ShapeDtypeStruct((8, F), F32), jax.ShapeDtypeStruct((8, F), F32)),
        in_specs=[pl.BlockSpec((tm, D), lambda j, i: (i, 0)), pl.BlockSpec((ct, D), lambda j, i: (j, 0)),
                  main, main, halo, halo, wspec, wspec, bspec, bspec],
        out_specs=(main, main, pspec, pspec),
        compiler_params=_params(dimension_semantics=("parallel", "arbitrary")),
    )(gm, w_down, up_g, up_v, up_g, up_v, cwg, cwv, cbg, cbv)


def _conv_bwd_taps(d, halo, w_ref):
    return w_ref[2:3, :] * d + w_ref[1:2, :] * _shift_up(d, halo, 1) + w_ref[0:1, :] * _shift_up(d, halo, 2)


def _scan_mats(R, nc, reverse):
    i = lax.broadcasted_iota(jnp.int32, (LANES, LANES), 0)
    j = lax.broadcasted_iota(jnp.int32, (LANES, LANES), 1)
    inner = ((i >= j) if reverse else (i <= j)).astype(F32)
    r = lax.broadcasted_iota(jnp.int32, (R, R), 0)
    c = lax.broadcasted_iota(jnp.int32, (R, R), 1)
    same = (r // nc) == (c // nc)
    outer = (same & ((c > r) if reverse else (c < r))).astype(F32)
    return inner, outer


def _chunk_scan(v, inner, outer, reverse):
    w = _dot(v, inner, precision=HIGHEST)
    col = 0 if reverse else LANES - 1
    carry = _dot(outer, w, precision=HIGHEST)[:, col:col + 1]
    return w + carry


def _fgate_fwd(z_rows, nc):
    R = z_rows.shape[0]

    def body(z_ref, f_ref):
        z = z_ref[...]
        logf = jnp.minimum(z, 0.0) - jnp.log(1.0 + jnp.exp(-jnp.abs(z)))
        inner, outer = _scan_mats(R, nc, False)
        f_ref[...] = _chunk_scan(logf, inner, outer, False)

    return pl.pallas_call(body, name="fgate_fwd", out_shape=jax.ShapeDtypeStruct((R, LANES), F32),
                          compiler_params=_params())(z_rows)


def _fgate_bwd(dfk_neg_rows, dfq_rows, z_rows, nc):
    R = z_rows.shape[0]
    nh = R // nc

    def body(dfk_ref, dfq_ref, z_ref, dz_ref, db_ref):
        inner, outer = _scan_mats(R, nc, True)
        dlogf = _chunk_scan(dfq_ref[...] - dfk_ref[...], inner, outer, True)
        dz = dlogf * (1.0 / (1.0 + jnp.exp(z_ref[...])))
        dz_ref[...] = dz
        hr = lax.broadcasted_iota(jnp.int32, (nh, R), 0)
        hc = lax.broadcasted_iota(jnp.int32, (nh, R), 1) // nc
        per_head = _dot((hr == hc).astype(F32), dz, precision=HIGHEST)
        db_ref[...] = jnp.sum(per_head, axis=1, keepdims=True)

    return pl.pallas_call(
        body, name="fgate_bwd",
        out_shape=(jax.ShapeDtypeStruct((R, LANES), F32), jax.ShapeDtypeStruct((nh, 1), F32)),
        compiler_params=_params())(dfk_neg_rows, dfq_rows, z_rows)


_NEG = -1e30
SKIP_BELOW = -106.0
_SCALE = HEAD_DIM ** -0.5
N_SCAN = ATT_BK // SCAN_BK
F_PARTS = 3
Q_F_LANE = HEAD_DIM
Q_ONE_LANE = HEAD_DIM + F_PARTS


def _kv_slice(j):
    return pl.ds(pl.multiple_of(j * ATT_BK, ATT_BK), ATT_BK)


def _mask_t(strict):
    s = lax.broadcasted_iota(jnp.int32, (ATT_BK, ATT_BQ), 0)
    t = lax.broadcasted_iota(jnp.int32, (ATT_BK, ATT_BQ), 1)
    return (s < t) if strict else (s <= t)


def _walk_down(i, step, alive, carry):
    carry = step(i, carry, True)

    def cond(st):
        n, go, _ = st
        return jnp.logical_and(n < i, go)

    def body(st):
        n, _, cr = st
        j = i - 1 - n
        cr = step(j, cr, False)
        return n + 1, alive(jnp.maximum(j - 1, 0), cr), cr

    return lax.while_loop(cond, body, (jnp.int32(0), alive(jnp.maximum(i - 1, 0), carry), carry))[2]


def _t_block(rows):
    return pl.BlockSpec((None, rows, ATT_BQ), lambda h, i, *_: (h, 0, i))


def _t_full(rows, S):
    return pl.BlockSpec((None, rows, S), lambda h, i, *_: (h, 0, 0))


def _n_block():
    return pl.BlockSpec((None, ATT_BQ, LANES), lambda h, i, *_: (h, i, 0))


def _n_full(S):
    return pl.BlockSpec((None, S, LANES), lambda h, i, *_: (h, 0, 0))


def _heads(t):
    S = t.shape[0]
    return jnp.transpose(t.reshape(S, -1, HEAD_DIM), (1, 0, 2))


def _skip_bounds(k_sq, f_rows):
    H, S = f_rows.shape
    f_end = f_rows.reshape(H, S // ATT_BK, ATT_BK)[:, :, -1]
    return f_end, lax.cummax(jnp.sqrt(k_sq), axis=1)


def _bf16_parts(f):
    hi = f.astype(BF16).astype(F32)
    mid = (f - hi).astype(BF16).astype(F32)
    return hi, mid, (f - hi - mid).astype(BF16).astype(F32)


def _att_prep(qkv, f_pairs):
    S = qkv.shape[0]
    n_pairs = qkv.shape[1] // (6 * LANES)
    H = 2 * n_pairs
    tm = ATT_BK
    assert S % tm == 0, S

    def body(qf_ref, kf_ref, vf_ref, qs_ref, ks_ref, vs_ref, f_ref,
             fqn, fqt, fkn, fkt, fvn, fvt, sqn, sqt, skn, skt, svn, svt, ksq_ref):
        lane = lax.broadcasted_iota(jnp.int32, (1, LANES), 1)
        f = f_ref[...]

        def head(ref, e):
            t = ref[...].astype(F32)
            if e == 1:
                t = pltpu.roll(t, HEAD_DIM, 1)
            return jnp.where(lane < HEAD_DIM, t, 0.0)

        def at(first):
            return jnp.logical_and(lane >= first, lane < first + F_PARTS)

        for e in range(2):
            parts = _bf16_parts(f[:, e:e + 1])
            f_lanes = sum(jnp.where(lane == Q_F_LANE + k, parts[k], 0.0) for k in range(F_PARTS))
            nf_lanes = sum(jnp.where(lane == Q_ONE_LANE + k, parts[k], 0.0) for k in range(F_PARTS))
            kf = head(kf_ref, e)
            k_sq = jnp.max(jnp.sum(kf * kf, axis=1, keepdims=True), axis=0, keepdims=True)
            ksq_ref[e, 0] = jnp.broadcast_to(k_sq, (1, LANES))
            vals = (
                (fqn, fqt, LANES, head(qf_ref, e) * _SCALE + f_lanes + jnp.where(at(Q_ONE_LANE), 1.0, 0.0)),
                (fkn, fkt, LANES, kf + jnp.where(at(Q_F_LANE), 1.0, 0.0) - nf_lanes),
                (fvn, fvt, HEAD_DIM, head(vf_ref, e)),
                (sqn, sqt, LANES, head(qs_ref, e) * _SCALE),
                (skn, skt, LANES, head(ks_ref, e)),
                (svn, svt, HEAD_DIM, head(vs_ref, e)),
            )
            for n_ref, t_ref, rows, val in vals:
                n_ref[e] = val.astype(BF16)
                t_ref[e] = val.T[:rows].astype(BF16)

    col = lambda base: pl.BlockSpec((tm, LANES), lambda i, p: (i, base + p))
    n_spec = pl.BlockSpec((2, tm, LANES), lambda i, p: (p, i, 0))
    t_spec = lambda rows: pl.BlockSpec((2, rows, tm), lambda i, p: (p, 0, i))
    n_sds = jax.ShapeDtypeStruct((H, S, LANES), BF16)
    t_sds = lambda rows: jax.ShapeDtypeStruct((H, rows, S), BF16)
    group = ([n_sds, t_sds(LANES), n_sds, t_sds(LANES), n_sds, t_sds(HEAD_DIM)],
             [n_spec, t_spec(LANES), n_spec, t_spec(LANES), n_spec, t_spec(HEAD_DIM)])
    res = pl.pallas_call(
        body, name="att_prep", grid=(S // tm, n_pairs),
        out_shape=tuple(group[0] * 2 + [jax.ShapeDtypeStruct((H, S // tm, 1, LANES), F32)]),
        in_specs=[col(k * n_pairs) for k in range(6)] + [pl.BlockSpec((None, tm, 2), lambda i, p: (p, i, 0))],
        out_specs=tuple(group[1] * 2 + [pl.BlockSpec((2, 1, 1, LANES), lambda i, p: (p, i, 0, 0))]),
        compiler_params=_params(dimension_semantics=("parallel", "parallel")),
    )(qkv, qkv, qkv, qkv, qkv, qkv, f_pairs)
    names = ("q_n", "q_t", "k_n", "k_t", "v_n", "v_t")
    return dict(zip(names, res[:6])), dict(zip(names, res[6:12])), res[12][:, :, 0, 0]


def _fox_reach(qt, fend_ref, kmax_ref, h):
    qf = qt.astype(F32)
    q_norm = jnp.sqrt(jnp.sum(jnp.square(qf[:HEAD_DIM]), axis=0, keepdims=True))
    f_t = jnp.sum(qf[Q_F_LANE:Q_F_LANE + F_PARTS], axis=0, keepdims=True)
    return lambda j: q_norm * kmax_ref[h, j] + f_t - fend_ref[h, j]


def _fox_fwd(q_t, k_n, v_t, f_end, k_max, shards):
    H, _, S = q_t.shape
    n, nq = len(shards), S // ATT_BQ

    def body(fend_ref, kmax_ref, qt_ref, k_ref, vt_ref, *rest):
        ins, (ot_ref, lse_ref), outs, sems = rest[:n], rest[n:n + 2], rest[n + 2:2 * n + 2], rest[2 * n + 2:]
        h, i = pl.program_id(0), pl.program_id(1)

        @pl.when(jnp.logical_and(h == 0, i == 0))
        def _():
            for cp in _exchange_copies(ins, outs, *sems, False):
                cp.start()

        qt = qt_ref[...]
        reach = _fox_reach(qt, fend_ref, kmax_ref, h)

        def step(j, carry, masked):
            m, l, acc = carry
            ks = _kv_slice(j)
            s = _dot(k_ref[ks, :], qt)
            if masked:
                s = jnp.where(_mask_t(False), s, _NEG)
            mn = jnp.maximum(m, jnp.max(s, axis=0, keepdims=True))
            alpha = jnp.exp(m - mn)
            p = jnp.exp(s - mn)
            l = alpha * l + jnp.sum(p, axis=0, keepdims=True)
            acc = acc * alpha + _dot(vt_ref[:, ks], p.astype(BF16))
            return mn, l, acc

        def alive(j, carry):
            return jnp.max(reach(j) - carry[0]) > SKIP_BELOW

        row = jnp.zeros((1, ATT_BQ), F32)
        m, l, acc = _walk_down(i, step, alive, (row + _NEG, row, jnp.zeros((HEAD_DIM, ATT_BQ), F32)))
        ot_ref[...] = acc / l
        lse_ref[...] = m + jnp.log(l)

        @pl.when(jnp.logical_and(h == H - 1, i == nq - 1))
        def _():
            for cp in _exchange_copies(ins, outs, *sems, False):
                cp.wait()

    any_spec = pl.BlockSpec(memory_space=pl.ANY)
    grid_spec = pltpu.PrefetchScalarGridSpec(
        num_scalar_prefetch=2, grid=(H, nq),
        in_specs=[_t_block(LANES), _n_full(S), _t_full(HEAD_DIM, S)] + [any_spec] * n,
        out_specs=tuple([_t_block(HEAD_DIM), _t_block(1)] + [any_spec] * n),
        scratch_shapes=_exchange_sems(n))
    res = pl.pallas_call(
        body, name="fox_fwd", grid_spec=grid_spec,
        out_shape=tuple([jax.ShapeDtypeStruct((H, HEAD_DIM, S), F32), jax.ShapeDtypeStruct((H, 1, S), F32)]
                        + _exchange_out_shapes(shards, False)),
        compiler_params=_params(dimension_semantics=("arbitrary", "arbitrary"), has_side_effects=True),
    )(f_end, k_max, q_t, k_n, v_t, *shards)
    return res[0], res[1], res[2:]


def _fox_bwd(q_t, q_n, k_n, k_t, v_n, do_t, do_n, o_t, lse, f_end, k_max):
    H, _, S = q_t.shape

    def body(fend_ref, kmax_ref, qt_ref, qn_ref, k_ref, kt_ref, v_ref, dot_ref, don_ref, ot_ref, lse_ref,
             dqt_ref, dk_ref, dv_ref):
        h, i = pl.program_id(0), pl.program_id(1)

        @pl.when(i == 0)
        def _():
            dk_ref[...] = jnp.zeros_like(dk_ref)
            dv_ref[...] = jnp.zeros_like(dv_ref)

        qt, qn, dot, don = qt_ref[...], qn_ref[...], dot_ref[...], don_ref[...]
        lse = lse_ref[...]
        delta = jnp.sum(dot[:HEAD_DIM].astype(F32) * ot_ref[...], axis=0, keepdims=True)
        reach = _fox_reach(qt, fend_ref, kmax_ref, h)

        def alive(j, dq):
            return jnp.max(reach(j) - lse) > SKIP_BELOW

        def step(j, dq, masked):
            ks = _kv_slice(j)
            s = _dot(k_ref[ks, :], qt)
            if masked:
                s = jnp.where(_mask_t(False), s, _NEG)
            p = jnp.exp(s - lse)
            ds = (p * (_dot(v_ref[ks, :], dot) - delta)).astype(BF16)
            dk_ref[ks, :] += _dot(ds, qn)
            dv_ref[ks, :] += _dot(p.astype(BF16), don)
            return dq + _dot(kt_ref[:, ks], ds)

        dqt_ref[...] = _walk_down(i, step, alive, jnp.zeros((LANES, ATT_BQ), F32))

    grid_spec = pltpu.PrefetchScalarGridSpec(
        num_scalar_prefetch=2, grid=(H, S // ATT_BQ),
        in_specs=[_t_block(LANES), _n_block(), _n_full(S), _t_full(LANES, S), _n_full(S),
                  _t_block(LANES), _n_block(), _t_block(HEAD_DIM), _t_block(1)],
        out_specs=(_t_block(LANES), _n_full(S), _n_full(S)))
    return pl.pallas_call(
        body, name="fox_bwd", grid_spec=grid_spec,
        out_shape=(jax.ShapeDtypeStruct((H, LANES, S), F32), jax.ShapeDtypeStruct((H, S, LANES), F32),
                   jax.ShapeDtypeStruct((H, S, LANES), F32)),
        compiler_params=_params(dimension_semantics=("parallel", "arbitrary")),
    )(f_end, k_max, q_t, q_n, k_n, k_t, v_n, do_t, do_n, o_t, lse)


def _scan_lhs():
    r = lax.broadcasted_iota(jnp.int32, (SCAN_BK, 2 * SCAN_BK), 0)
    c = lax.broadcasted_iota(jnp.int32, (SCAN_BK, 2 * SCAN_BK), 1) % SCAN_BK
    return (c >= r).astype(BF16)


def _suffix_sum(t, lhs):
    hi = t.astype(BF16)
    lo = (t - hi.astype(F32)).astype(BF16)
    return _dot(lhs, jnp.concatenate([hi, lo], axis=0))


def _sb_scores(k, qt, mask):
    z = _dot(k, qt)
    e = jnp.exp(-jnp.abs(z))
    lb = -(jnp.maximum(z, 0.0) + jnp.log(1.0 + e))
    if mask is not None:
        lb = jnp.where(mask, lb, 0.0)
    return z, e, lb


def _scan_blocks():
    return [slice(u * SCAN_BK, (u + 1) * SCAN_BK) for u in reversed(range(N_SCAN))]


def _sb_fwd(q_t, k_n, v_t):
    H, _, S = q_t.shape

    def body(qt_ref, k_ref, vt_ref, ot_ref):
        i = pl.program_id(1)
        qt = qt_ref[...]
        lhs = _scan_lhs()

        def step(j, carry, masked):
            c, acc = carry
            ks = _kv_slice(j)
            mask = _mask_t(True) if masked else None
            z, _, lb = _sb_scores(k_ref[ks, :], qt, mask)
            parts = []
            for sl in _scan_blocks():
                rin = _suffix_sum(lb[sl], lhs)
                a = jnp.exp(z[sl] + rin + c)
                if masked:
                    a = jnp.where(mask[sl], a, 0.0)
                parts.append(a.astype(BF16))
                c = c + rin[0:1, :]
            a_all = jnp.concatenate(parts[::-1], axis=0)
            return c, acc + _dot(vt_ref[:, ks], a_all)

        carry = (jnp.zeros((1, ATT_BQ), F32), jnp.zeros((HEAD_DIM, ATT_BQ), F32))
        ot_ref[...] = _walk_down(i, step, lambda j, cr: jnp.max(cr[0]) > SKIP_BELOW, carry)[1]

    return pl.pallas_call(
        body, name="sb_fwd", grid=(H, S // ATT_BQ),
        out_shape=jax.ShapeDtypeStruct((H, HEAD_DIM, S), F32),
        in_specs=[_t_block(LANES), _n_full(S), _t_full(HEAD_DIM, S)],
        out_specs=_t_block(HEAD_DIM),
        compiler_params=_params(dimension_semantics=("parallel", "parallel")),
    )(q_t, k_n, v_t)


def _sb_bwd(q_t, q_n, k_n, k_t, v_n, do_t, do_n, o_t, bound):
    H, _, S = q_t.shape
    n, nq = len(bound), S // ATT_BQ

    def body(qt_ref, qn_ref, k_ref, kt_ref, v_ref, dot_ref, don_ref, ot_ref, *rest):
        ins, (dqt_ref, dk_ref, dv_ref) = rest[:n], rest[n:n + 3]
        outs, sems = rest[n + 3:2 * n + 3], rest[2 * n + 3:]
        h, i = pl.program_id(0), pl.program_id(1)

        @pl.when(jnp.logical_and(h == 0, i == 0))
        def _():
            for cp in _exchange_copies(ins, outs, *sems, True):
                cp.start()

        @pl.when(i == 0)
        def _():
            dk_ref[...] = jnp.zeros_like(dk_ref)
            dv_ref[...] = jnp.zeros_like(dv_ref)

        qt, qn, dot, don = qt_ref[...], qn_ref[...], dot_ref[...], don_ref[...]
        lhs = _scan_lhs()
        delta = jnp.sum(dot[:HEAD_DIM].astype(F32) * ot_ref[...], axis=0, keepdims=True)

        def step(j, carry, masked):
            c, g, dq = carry
            ks = _kv_slice(j)
            mask = _mask_t(True) if masked else None
            z, e, lb = _sb_scores(k_ref[ks, :], qt, mask)
            da = _dot(v_ref[ks, :], dot)
            a_parts, dz_parts = [], []
            for sl in _scan_blocks():
                rin = _suffix_sum(lb[sl], lhs)
                a = jnp.exp(z[sl] + rin + c)
                if masked:
                    a = jnp.where(mask[sl], a, 0.0)
                ab = a.astype(BF16)
                gg = ab.astype(F32) * da[sl]
                rgin = _suffix_sum(gg, lhs)
                rinv = 1.0 / (1.0 + e[sl])
                sig = jnp.where(z[sl] >= 0.0, rinv, e[sl] * rinv)
                dz = gg - sig * (delta - g - (rgin - gg))
                if masked:
                    dz = jnp.where(mask[sl], dz, 0.0)
                a_parts.append(ab)
                dz_parts.append(dz.astype(BF16))
                c = c + rin[0:1, :]
                g = g + rgin[0:1, :]
            ab_all = jnp.concatenate(a_parts[::-1], axis=0)
            dzb = jnp.concatenate(dz_parts[::-1], axis=0)
            dk_ref[ks, :] += _dot(dzb, qn)
            dv_ref[ks, :] += _dot(ab_all, don)
            return c, g, dq + _dot(kt_ref[:, ks], dzb)

        row = jnp.zeros((1, ATT_BQ), F32)
        carry = (row, row, jnp.zeros((LANES, ATT_BQ), F32))
        dqt_ref[...] = _walk_down(i, step, lambda j, cr: jnp.max(cr[0]) > SKIP_BELOW, carry)[2]

        @pl.when(jnp.logical_and(h == H - 1, i == nq - 1))
        def _():
            for cp in _exchange_copies(ins, outs, *sems, True):
                cp.wait()

    any_spec = pl.BlockSpec(memory_space=pl.ANY)
    res = pl.pallas_call(
        body, name="sb_bwd", grid=(H, nq),
        out_shape=tuple([jax.ShapeDtypeStruct((H, LANES, S), F32), jax.ShapeDtypeStruct((H, S, LANES), F32),
                         jax.ShapeDtypeStruct((H, S, LANES), F32)] + _exchange_out_shapes(bound, True)),
        in_specs=[_t_block(LANES), _n_block(), _n_full(S), _t_full(LANES, S), _n_full(S),
                  _t_block(LANES), _n_block(), _t_block(HEAD_DIM)] + [any_spec] * n,
        out_specs=tuple([_t_block(LANES), _n_full(S), _n_full(S)] + [any_spec] * n),
        scratch_shapes=_exchange_sems(n),
        compiler_params=_params(dimension_semantics=("arbitrary", "arbitrary"), has_side_effects=True),
    )(q_t, q_n, k_n, k_t, v_n, do_t, do_n, o_t, *bound)
    return res[0], res[1], res[2], res[3:]


def _dqkv_assemble(dqf_t, dkf, dvf, dqs_t, dks, dvs):
    H, _, S = dqf_t.shape
    n_pairs = H // 2
    tm = _tile(S, 512)

    def body(dqf_ref, dkf_ref, dvf_ref, dqs_ref, dks_ref, dvs_ref, out_ref, dfk_ref):
        lane = lax.broadcasted_iota(jnp.int32, (1, LANES), 1)
        slabs = ((dqf_ref, True), (dkf_ref, False), (dvf_ref, False),
                 (dqs_ref, True), (dks_ref, False), (dvs_ref, False))
        for k, (ref, transposed) in enumerate(slabs):
            if transposed:
                t0, t1 = ref[0].T * _SCALE, ref[1].T * _SCALE
            else:
                t0, t1 = ref[0], ref[1]
            out_ref[k] = jnp.where(lane < HEAD_DIM, t0, pltpu.roll(t1, HEAD_DIM, 1)).astype(BF16)
        for e in range(2):
            dfk_ref[e] = dkf_ref[e].T[Q_ONE_LANE:Q_ONE_LANE + 1, :]

    t_spec = pl.BlockSpec((2, LANES, tm), lambda i, p: (p, 0, i))
    n_spec = pl.BlockSpec((2, tm, LANES), lambda i, p: (p, i, 0))
    return pl.pallas_call(
        body, name="dqkv_assemble", grid=(S // tm, n_pairs),
        out_shape=(jax.ShapeDtypeStruct((6, S, n_pairs * LANES), BF16), jax.ShapeDtypeStruct((H, 1, S), F32)),
        in_specs=[t_spec, n_spec, n_spec, t_spec, n_spec, n_spec],
        out_specs=(pl.BlockSpec((6, tm, LANES), lambda i, p: (0, i, p)),
                   pl.BlockSpec((2, 1, tm), lambda i, p: (p, 0, i))),
        compiler_params=_params(dimension_semantics=("parallel", "parallel")),
    )(dqf_t, dkf, dvf, dqs_t, dks, dvs)


def _acc_spec(w):
    return pl.BlockSpec((1, w), lambda i: (0, 0))


def _loss_head(x1, act, w_down, gate_m, g_final, target):
    S, D = x1.shape
    F = act.shape[1]
    tm = _tile(S, 512)

    def body(x1_ref, act_ref, w_ref, gt_ref, gf_ref, tg_ref, dx2_ref, gm_ref, loss_ref, dgf_ref, dgt_ref):
        @pl.when(pl.program_id(0) == 0)
        def _():
            loss_ref[...] = jnp.zeros_like(loss_ref)
            dgf_ref[...] = jnp.zeros_like(dgf_ref)
            dgt_ref[...] = jnp.zeros_like(dgt_ref)

        mo = _dot(act_ref[...], w_ref[...])
        x2 = x1_ref[...] + gt_ref[...] * mo
        r = lax.rsqrt(jnp.mean(x2 * x2, axis=-1, keepdims=True) + EPS)
        xh = x2 * r
        diff = xh * gf_ref[...] - tg_ref[...]
        loss_ref[...] += (0.5 / D) * jnp.sum(diff * diff)
        dy = diff * (1.0 / D)
        dgf_ref[...] += jnp.sum(dy * xh, axis=0, keepdims=True)
        dxh = dy * gf_ref[...]
        dx2 = r * (dxh - xh * jnp.mean(dxh * xh, axis=-1, keepdims=True))
        dx2_ref[...] = dx2
        gm_ref[...] = (dx2 * gt_ref[...]).astype(BF16)
        dgt_ref[...] += jnp.sum(dx2 * mo, axis=0, keepdims=True)

    return pl.pallas_call(
        body, name="loss_head", grid=(S // tm,),
        out_shape=(jax.ShapeDtypeStruct((S, D), F32), jax.ShapeDtypeStruct((S, D), BF16),
                   jax.ShapeDtypeStruct((1, LANES), F32), jax.ShapeDtypeStruct((1, D), F32),
                   jax.ShapeDtypeStruct((1, D), F32)),
        in_specs=_row_specs(tm, [D, F]) + [pl.BlockSpec((F, D), lambda i: (0, 0))] + [_vec_spec(D)] * 2
        + _row_specs(tm, [D]),
        out_specs=tuple(_row_specs(tm, [D, D]) + [_acc_spec(LANES), _acc_spec(D), _acc_spec(D)]),
        compiler_params=_params(dimension_semantics=("arbitrary",)),
    )(x1, act, w_down, gate_m, g_final, target)


def _norm_bwd(lhs, rhs, xin, dres, g, scale, name, gate=None, branch=None, bound=(), conv=None, tm=256):
    S, D = xin.shape
    tm = _tile(S, tm)
    gated = gate is not None
    nl, nb, n_steps = len(lhs), len(bound), S // tm
    nc_ = nl if conv else 0
    n_out = (6 if gated else 4) + nc_

    def body(*refs):
        l_refs, r_refs, rest = refs[:nl], refs[nl:2 * nl], refs[2 * nl:]
        halo_refs, cw_refs, rest = rest[:nc_], rest[nc_:2 * nc_], rest[2 * nc_:]
        if gated:
            x_ref, dr_ref, g_ref, sc_ref, gt_ref, br_ref = rest[:6]
            rest = rest[6:]
        else:
            x_ref, dr_ref, g_ref, sc_ref = rest[:4]
            rest = rest[4:]
        ins, outs, ex_outs, sems = rest[:nb], rest[nb:nb + n_out], rest[nb + n_out:2 * nb + n_out], rest[2 * nb + n_out:]
        dx_ref, dsc_ref, dsh_ref, dg_ref = outs[:4]
        sums = (dsc_ref, dsh_ref, dg_ref) + ((outs[5],) if gated else ())
        dup_refs = outs[n_out - nc_:]
        i = pl.program_id(0)

        @pl.when(i == 0)
        def _():
            for s_ref in sums:
                s_ref[...] = jnp.zeros_like(s_ref)
            if nb:
                for cp in _chip_exchange_copies(ins, ex_outs, *sems):
                    cp.start()

        dhv = None
        for k, (l_ref, r_ref) in enumerate(zip(l_refs, r_refs)):
            if conv:
                halo = jnp.where(i == n_steps - 1, 0.0, halo_refs[k][...].astype(F32))
                dup = _conv_bwd_taps(l_ref[...].astype(F32), halo, cw_refs[k]).astype(BF16)
                dup_refs[k][...] = dup
                terms = [_dot_nt(dup, r_ref[...])]
            elif len(l_ref.shape) == 3:
                K = l_ref.shape[2]
                terms = [_dot_nt(l_ref[k], r_ref[:, k * K:(k + 1) * K]) for k in range(l_ref.shape[0])]
            else:
                terms = [_dot_nt(l_ref[...], r_ref[...])]
            for t in terms:
                dhv = t if dhv is None else dhv + t
        xv = x_ref[...]
        r = lax.rsqrt(jnp.mean(xv * xv, axis=-1, keepdims=True) + EPS)
        xh = xv * r
        dsc_ref[...] += jnp.sum(dhv * (xh * g_ref[...]), axis=0, keepdims=True)
        dsh_ref[...] += jnp.sum(dhv, axis=0, keepdims=True)
        dn = dhv * (1.0 + sc_ref[...])
        dg_ref[...] += jnp.sum(dn * xh, axis=0, keepdims=True)
        dxh = dn * g_ref[...]
        dx = dr_ref[...] + r * (dxh - xh * jnp.mean(dxh * xh, axis=-1, keepdims=True))
        dx_ref[...] = dx
        if gated:
            outs[4][...] = (dx * gt_ref[...]).astype(BF16)
            outs[5][...] += jnp.sum(dx * br_ref[...], axis=0, keepdims=True)

        if nb:
            @pl.when(i == n_steps - 1)
            def _():
                for cp in _chip_exchange_copies(ins, ex_outs, *sems):
                    cp.wait()

    def l_spec(a):
        if a.ndim == 3:
            return pl.BlockSpec((a.shape[0], tm, a.shape[2]), lambda i: (0, i, 0))
        return pl.BlockSpec((tm, a.shape[1]), lambda i: (i, 0))

    any_spec = pl.BlockSpec(memory_space=pl.ANY)
    vec = jax.ShapeDtypeStruct((1, D), F32)
    out_shape = [jax.ShapeDtypeStruct((S, D), F32), vec, vec, vec]
    out_specs = _row_specs(tm, [D]) + [_acc_spec(D)] * 3
    in_specs = [l_spec(a) for a in lhs] + [pl.BlockSpec(b.shape, lambda i: (0, 0)) for b in rhs]
    args = list(lhs) + list(rhs)
    if conv:
        step, last_halo = tm // HALO, S // HALO - 1
        in_specs += [pl.BlockSpec((HALO, a.shape[1]), lambda i: (jnp.minimum((i + 1) * step, last_halo), 0))
                     for a in lhs]
        in_specs += [pl.BlockSpec(w.shape, lambda i: (0, 0)) for w in conv]
        args += list(lhs) + list(conv)
    in_specs += _row_specs(tm, [D, D]) + [_vec_spec(D)] * 2
    args += [xin, dres, g, scale]
    if gated:
        out_shape += [jax.ShapeDtypeStruct((S, D), BF16), vec]
        out_specs += _row_specs(tm, [D]) + [_acc_spec(D)]
        in_specs += [_vec_spec(D)] + _row_specs(tm, [D])
        args += [gate, branch]
    if conv:
        out_shape += [jax.ShapeDtypeStruct(a.shape, BF16) for a in lhs]
        out_specs += [l_spec(a) for a in lhs]
    res = pl.pallas_call(
        body, name=name, grid=(n_steps,),
        out_shape=tuple(out_shape + [jax.ShapeDtypeStruct(b.shape, b.dtype) for b in bound]),
        in_specs=in_specs + [any_spec] * nb, out_specs=tuple(out_specs + [any_spec] * nb),
        scratch_shapes=_chip_exchange_sems(nb) if nb else [],
        compiler_params=_params(dimension_semantics=("arbitrary",), has_side_effects=bool(nb)),
    )(*args, *bound)
    return tuple(res[:n_out]) + (tuple(res[n_out:]),)


def _headnorm_bwd(dmix, o_f, o_s, g_f, g_s):
    dh, S = o_f.shape
    H = dh // HEAD_DIM
    tm = _tile(S, 256)

    def body(dm_ref, of_ref, os_ref, gf_ref, gs_ref, fn_ref, ft_ref, sn_ref, st_ref, dgf_ref, dgs_ref):
        @pl.when(pl.program_id(0) == 0)
        def _():
            dgf_ref[...] = jnp.zeros_like(dgf_ref)
            dgs_ref[...] = jnp.zeros_like(dgs_ref)

        ones = _group_ones()
        lane = lax.broadcasted_iota(jnp.int32, (1, LANES), 1)
        parts = ((of_ref, gf_ref, fn_ref, ft_ref, dgf_ref), (os_ref, gs_ref, sn_ref, st_ref, dgs_ref))
        for part, (o_ref, g_ref, n_ref, t_ref, dg_ref) in enumerate(parts):
            for t in range(dh // LANES):
                cols = slice(t * LANES, (t + 1) * LANES)
                o = o_ref[cols, :].T
                dm = dm_ref[:, part * dh + t * LANES: part * dh + (t + 1) * LANES]
                r = lax.rsqrt(_group_sum(o * o, ones) * (1.0 / HEAD_DIM) + EPS)
                oh = o * r
                dg_ref[:, cols] += jnp.sum(dm * oh, axis=0, keepdims=True)
                dn = dm * g_ref[:, cols]
                mean = _group_sum(dn * oh, ones) * (1.0 / HEAD_DIM)
                do = r * (dn - oh * mean)
                for e in range(2):
                    d = do if e == 0 else pltpu.roll(do, HEAD_DIM, 1)
                    d = jnp.where(lane < HEAD_DIM, d, 0.0)
                    n_ref[2 * t + e] = d.astype(BF16)
                    t_ref[2 * t + e] = d.T.astype(BF16)

    vec = jax.ShapeDtypeStruct((1, dh), F32)
    n_sds = jax.ShapeDtypeStruct((H, S, LANES), BF16)
    t_sds = jax.ShapeDtypeStruct((H, LANES, S), BF16)
    n_spec = pl.BlockSpec((H, tm, LANES), lambda i: (0, i, 0))
    t_spec = pl.BlockSpec((H, LANES, tm), lambda i: (0, 0, i))
    return pl.pallas_call(
        body, name="headnorm_bwd", grid=(S // tm,),
        out_shape=(n_sds, t_sds, n_sds, t_sds, vec, vec),
        in_specs=_row_specs(tm, [2 * dh]) + [_col_spec(tm, dh)] * 2 + [_vec_spec(dh)] * 2,
        out_specs=(n_spec, t_spec, n_spec, t_spec, _acc_spec(dh), _acc_spec(dh)),
        compiler_params=_params(dimension_semantics=("arbitrary",)),
    )(dmix, o_f, o_s, g_f, g_s)


def _adamw(w, gslots, m, v, name):
    R, C = w.shape
    n = gslots.shape[0]
    tr = 256 if (R % 256 == 0 and R > 256) else R
    bc1 = 1.0 - ADAM_B1 ** ADAM_STEP
    bc2 = 1.0 - ADAM_B2 ** ADAM_STEP

    def body(w_ref, gs_ref, m_ref, v_ref, g_ref, d_ref, nm_ref, nv_ref):
        g = gs_ref[0]
        for s in range(1, n):
            g = g + gs_ref[s]
        nm = ADAM_B1 * m_ref[...] + (1.0 - ADAM_B1) * g
        nv = ADAM_B2 * v_ref[...] + (1.0 - ADAM_B2) * (g * g)
        g_ref[...] = g
        nm_ref[...] = nm
        nv_ref[...] = nv
        d_ref[...] = -ADAM_LR * ((nm / bc1) / (jnp.sqrt(nv / bc2) + ADAM_EPS) + ADAM_WD * w_ref[...])

    blk = pl.BlockSpec((tr, C), lambda i: (i, 0))
    sds = jax.ShapeDtypeStruct((R, C), F32)
    return pl.pallas_call(
        body, name=name, grid=(R // tr,), out_shape=(sds,) * 4,
        in_specs=[blk, pl.BlockSpec((n, tr, C), lambda i: (0, i, 0)), blk, blk], out_specs=(blk,) * 4,
        compiler_params=_params(dimension_semantics=("parallel",)),
    )(w, gslots, m, v)


def _slot_sum(slots, name):
    n, _, C = slots.shape

    def body(s_ref, o_ref):
        acc = s_ref[0]
        for s in range(1, n):
            acc = acc + s_ref[s]
        o_ref[...] = acc

    return pl.pallas_call(body, name=name, out_shape=jax.ShapeDtypeStruct((1, C), F32),
                          compiler_params=_params())(slots)


def _pad_cols(a, n):
    return jnp.pad(a, ((0, 0), (0, n - a.shape[1])))


def _ungather(g, axis):
    if axis == 0:
        return g.reshape(g.shape[0] * g.shape[1], g.shape[2])
    return jnp.transpose(g, (1, 0, 2)).reshape(g.shape[1], g.shape[0] * g.shape[2])


def _to_slots(full, axis):
    R, C = full.shape
    if axis == 0:
        return full.reshape(N_DEV, R // N_DEV, C)
    return jnp.transpose(full.reshape(R, N_DEV, C // N_DEV), (1, 0, 2))


def kernel(x, c, w_ada, b_ada, g_attn, w_in, b_fgate, g_out_fox, g_out_sb, w_out, g_mlp, w_up, conv_w, conv_b, w_down, g_final, loss_target, m_w_ada, m_b_ada, m_g_attn, m_w_in, m_b_fgate, m_g_out_fox, m_g_out_sb, m_w_out, m_g_mlp, m_w_up, m_conv_w, m_conv_b, m_w_down, m_g_final, v_w_ada, v_b_ada, v_g_attn, v_w_in, v_b_fgate, v_g_out_fox, v_g_out_sb, v_w_out, v_g_mlp, v_w_up, v_conv_w, v_conv_b, v_w_down, v_g_final):
    S, D = x.shape[1], x.shape[2]
    dh = D // 2
    n_heads = dh // HEAD_DIM
    n_qkv = 6 * dh
    ff = w_down.shape[1] * N_DEV
    ffp = -(-ff // (2 * LANES)) * (2 * LANES)
    nc = S // LANES
    me = 4 * lax.axis_index("x") + 2 * lax.axis_index("y") + lax.axis_index("c")
    xs, tgt = x[0], loss_target[0]

    c_all, win_g = _gather_two_level([c, w_in[0].astype(BF16)], name="gather_first")
    c_all = c_all.reshape(N_DEV, D)
    W_in = _ungather(win_g, 1)
    W_qkv, W_f = W_in[:, :n_qkv], _pad_cols(W_in[:, n_qkv:], LANES)
    cb_g, cb_v = _pad_cols(conv_b[:, :ff], ffp), _pad_cols(conv_b[:, ff:], ffp)

    n_ada = w_ada.shape[2]
    b_shard = lax.dynamic_slice(b_ada, (0, me * n_ada), (1, n_ada))
    mod_cols = _ada_fwd(c_all, w_ada[0], b_shard)
    (mod_g,) = _exchange([mod_cols], scatter=False, name="gather_mod")
    mod = lax.dynamic_index_in_dim(mod_g, me, axis=1, keepdims=False).reshape(6, 1, D)
    shift_a, scale_a, gate_a, shift_m, scale_m, gate_m = [mod[k] for k in range(6)]

    h1, h1_t = _prenorm(xs, g_attn, scale_a, shift_a, "prenorm_attn")
    qkv = _mm(h1, W_qkv, BF16, "proj_qkv")
    flog = _mm(h1, W_f, F32, "proj_fgate")
    zf = flog[:, :n_heads] + b_fgate
    z_rows = zf.T.reshape(n_heads * nc, LANES)
    f_rows = _fgate_fwd(z_rows, nc).reshape(n_heads, S)
    f_pairs = jnp.transpose(f_rows.reshape(n_heads // 2, 2, S), (0, 2, 1))
    fox, sb, k_sq = _att_prep(qkv, f_pairs)
    f_end, k_max = _skip_bounds(k_sq, f_rows)
    of_t, lse, (wout_g, wup_g, wdown_g, convw_g) = _fox_fwd(
        fox["q_t"], fox["k_n"], fox["v_t"], f_end, k_max,
        [w_out[0].astype(BF16), w_up[0].astype(BF16), w_down[0].astype(BF16), conv_w[0]])
    W_out = _ungather(wout_g, 0)
    W_up = _ungather(wup_g, 1)
    W_g, W_v = _pad_cols(W_up[:, :ff], ffp), _pad_cols(W_up[:, ff:], ffp)
    W_down = jnp.pad(_ungather(wdown_g, 0), ((0, ffp - ff), (0, 0)))
    cw_full = _ungather(convw_g, 1)
    cw_g, cw_v = _pad_cols(cw_full[:, :ff], ffp), _pad_cols(cw_full[:, ff:], ffp)
    os_t = _sb_fwd(sb["q_t"], sb["k_n"], sb["v_t"])
    o_f, o_s = of_t.reshape(dh, S), os_t.reshape(dh, S)
    mix, mix_t = _headnorm_fwd(o_f, o_s, g_out_fox, g_out_sb)
    a_out = _mm(mix, W_out, F32, "proj_out")
    x1, h2, h2_t = _resid_prenorm(xs, a_out, gate_a, g_mlp, scale_m, shift_m)
    up_g, up_v, act, act_t = _mlp_up(h2, W_g, W_v, cw_g, cw_v, cb_g, cb_v)

    dx2, gm, loss_p, dg_final, dgate_m = _loss_head(x1, act, W_down, gate_m, g_final.reshape(1, D), tgt)
    dW_down = _mm_acc(act_t, gm, "bwd_down_w")
    du_g, du_v, p_g, p_v = _conv_act_bwd(gm, W_down, up_g, up_v, cw_g, cw_v, cb_g, cb_v)
    dx1, dscale_m, dshift_m, dg_mlp, ga, dgate_a, dup_g, dup_v, _ = _norm_bwd(
        [du_g, du_v], [W_g, W_v], x1, dx2, g_mlp, scale_m, "norm_mlp_bwd", gate=gate_a, branch=a_out,
        conv=[cw_g, cw_v])
    dW_g = _mm_acc(h2_t, dup_g, "bwd_up_w_g")
    dW_v = _mm_acc(h2_t, dup_v, "bwd_up_w_v")
    dmix = _mm(ga, W_out, F32, "bwd_out_act", nt=True)
    dW_out = _mm_acc(mix_t, ga, "bwd_out_w")
    dof_n, dof_t, dos_n, dos_t, dg_fox, dg_sb = _headnorm_bwd(dmix, o_f, o_s, g_out_fox, g_out_sb)
    dqf_t, dkf, dvf = _fox_bwd(fox["q_t"], fox["q_n"], fox["k_n"], fox["k_t"], fox["v_n"], dof_t, dof_n, of_t, lse,
                              f_end, k_max)
    dW_upf = jnp.concatenate([dW_g[:, :ff], dW_v[:, :ff]], axis=1)
    dcw = jnp.concatenate([p_g[:CONV_W, :ff], p_v[:CONV_W, :ff]], axis=1)
    dqs_t, dks, dvs, (s_out, s_up, s_down, s_cw) = _sb_bwd(
        sb["q_t"], sb["q_n"], sb["k_n"], sb["k_t"], sb["v_n"], dos_t, dos_n, os_t,
        [_to_slots(dW_out, 0), _to_slots(dW_upf, 1), _to_slots(dW_down[:ff], 0), _to_slots(dcw, 1)])
    dparts, dfk = _dqkv_assemble(dqf_t, dkf, dvf, dqs_t, dks, dvs)
    dz_rows, db_fgate = _fgate_bwd(dfk.reshape(n_heads * nc, LANES),
                                   dqf_t[:, Q_F_LANE, :].reshape(n_heads * nc, LANES), z_rows, nc)
    dzf = _pad_cols(dz_rows.reshape(n_heads, S).T, LANES).astype(BF16)
    dW_qkv = _mm_acc_parts(h1_t, dparts, "bwd_in_w")
    dW_f = _mm_acc(h1_t, dzf, "bwd_in_w_fgate")
    dW_in = jnp.concatenate([jnp.transpose(dW_qkv, (1, 0, 2)).reshape(D, n_qkv), dW_f[:, :n_heads]], axis=1)
    bound_in = _to_slots(dW_in, 1)
    bound_in = bound_in.reshape((N_CHIPS, 2) + bound_in.shape[1:])
    (got_in,) = _scatter_to_sibling([bound_in], "scatter_sibling")
    c_idx = lax.axis_index("c").astype(jnp.int32).reshape(1)
    grad_x, dscale_a, dshift_a, dg_attn, (s_in,) = _norm_bwd(
        [dparts, dzf], [W_qkv, W_f], xs, dx1, g_attn, scale_a, "norm_attn_bwd",
        bound=[_pair_add(bound_in, got_in, c_idx, "pair_add")], tm=512)

    dconv_b = jnp.concatenate([p_g[CONV_W:CONV_W + 1, :ff], p_v[CONV_W:CONV_W + 1, :ff]], axis=1)
    parts = [dshift_a, dscale_a, dgate_a, dshift_m, dscale_m, dgate_m,
             dg_attn, db_fgate.reshape(1, n_heads), dg_fox, dg_sb, dg_mlp, dconv_b, dg_final,
             loss_p[:, :1]]
    sizes = [p.shape[1] for p in parts]
    vec = jnp.concatenate(parts, axis=1)
    n_vec = -(-vec.shape[1] // LANES) * LANES
    vec = _pad_cols(vec, n_vec)
    (vec_g,) = _exchange([vec], scatter=False, name="gather_small")
    offs = [0]
    for s in sizes:
        offs.append(offs[-1] + s)

    def small(k0, k1=None):
        k1 = k0 if k1 is None else k1
        return vec_g[:, :, offs[k0]:offs[k1 + 1]]

    dmod_all = small(0, 5).reshape(N_DEV, 6 * D)
    dmod_cols = lax.dynamic_slice(dmod_all, (0, me * n_ada), (N_DEV, n_ada))
    dW_ada = _ada_bwd(c_all.T, dmod_cols)


    res = {}
    res["w_ada"] = _adamw(w_ada[0], dW_ada[None], m_w_ada[0], v_w_ada[0], "adamw_w_ada")
    res["w_in"] = _adamw(w_in[0], s_in, m_w_in[0], v_w_in[0], "adamw_w_in")
    res["w_out"] = _adamw(w_out[0], s_out, m_w_out[0], v_w_out[0], "adamw_w_out")
    res["w_up"] = _adamw(w_up[0], s_up, m_w_up[0], v_w_up[0], "adamw_w_up")
    res["w_down"] = _adamw(w_down[0], s_down, m_w_down[0], v_w_down[0], "adamw_w_down")
    res["conv_w"] = _adamw(conv_w[0], s_cw, m_conv_w[0], v_conv_w[0], "adamw_conv_w")
    small_names = ["b_ada", "g_attn", "b_fgate", "g_out_fox", "g_out_sb", "g_mlp", "conv_b", "g_final"]
    small_w = [b_ada, g_attn, b_fgate, g_out_fox, g_out_sb, g_mlp, conv_b, g_final.reshape(1, D)]
    small_m = [m_b_ada, m_g_attn, m_b_fgate, m_g_out_fox, m_g_out_sb, m_g_mlp, m_conv_b, m_g_final.reshape(1, D)]
    small_v = [v_b_ada, v_g_attn, v_b_fgate, v_g_out_fox, v_g_out_sb, v_g_mlp, v_conv_b, v_g_final.reshape(1, D)]
    small_res = _adamw(jnp.concatenate(small_w, axis=1), small(0, 12), jnp.concatenate(small_m, axis=1),
                       jnp.concatenate(small_v, axis=1), "adamw_small")
    lo = 0
    for nm, wv in zip(small_names, small_w):
        res[nm] = tuple(r[:, lo:lo + wv.shape[1]] for r in small_res)
        lo += wv.shape[1]
    loss = _slot_sum(_pad_cols(small(13).reshape(N_DEV, 1), LANES).reshape(N_DEV, 1, LANES), "loss_sum")[0, 0]

    names = ["w_ada", "b_ada", "g_attn", "w_in", "b_fgate", "g_out_fox", "g_out_sb", "w_out", "g_mlp",
             "w_up", "conv_w", "conv_b", "w_down", "g_final"]

    def shaped(n, a):
        if n == "g_final":
            return a.reshape(D)
        if n in ("b_ada", "g_attn", "b_fgate", "g_out_fox", "g_out_sb", "g_mlp", "conv_b"):
            return a
        return a[None]

    outs = [loss, grad_x[None]]
    for k in range(4):
        outs += [shaped(n, res[n][k]) for n in names]
    return tuple(outs)
```

```python
import jax
import jax.numpy as jnp
from jax import lax
from jax.experimental import pallas as pl
from jax.experimental.pallas import tpu as pltpu

F32 = jnp.float32
BF16 = jnp.bfloat16
HIGHEST = lax.Precision.HIGHEST

N_DEV = 8
LANES = 128
HEAD_DIM = 64
EPS = 1e-6
CONV_W = 3
CONV_COLS = 1408
HALO = 16
ATT_BQ = 512
ATT_BK = 512
SCAN_BK = 128
VMEM_LIMIT = 56 * 1024 * 1024

ADAM_LR = 0.001
ADAM_B1 = 0.9
ADAM_B2 = 0.999
ADAM_EPS = 1e-08
ADAM_WD = 0.01
ADAM_STEP = 10


def _params(**kw):
    return pltpu.CompilerParams(vmem_limit_bytes=VMEM_LIMIT, **kw)


def _tile(n, cap):
    if n <= cap:
        return n
    best = None
    for t in range(LANES, cap + 1, LANES):
        if n % t == 0:
            best = t
    assert best is not None, (n, cap)
    return best


def _dot(a, b, **kw):
    return jnp.dot(a, b, preferred_element_type=F32, **kw)


def _exchange_copies(ins, outs, send_sems, recv_sems, loc_sems, scatter):
    n = len(ins)
    if n == 0:
        return []
    x, y, c = lax.axis_index("x"), lax.axis_index("y"), lax.axis_index("c")
    me = 4 * x + 2 * y + c
    copies = []
    for a in range(n):
        src = ins[a].at[me] if scatter else ins[a]
        copies.append(pltpu.make_async_copy(src, outs[a].at[me], loc_sems.at[a]))
    for k in range(1, N_DEV):
        px = 1 - x if k & 4 else x
        py = 1 - y if k & 2 else y
        pc = 1 - c if k & 1 else c
        peer = 4 * px + 2 * py + pc
        for a in range(n):
            src = ins[a].at[peer] if scatter else ins[a]
            copies.append(pltpu.make_async_remote_copy(
                src_ref=src, dst_ref=outs[a].at[me],
                send_sem=send_sems.at[a, k - 1], recv_sem=recv_sems.at[a, k - 1],
                device_id=(px, py, pc), device_id_type=pl.DeviceIdType.MESH))
    return copies


def _exchange_out_shapes(arrays, scatter):
    return [jax.ShapeDtypeStruct((N_DEV,) + tuple(a.shape[1:] if scatter else a.shape), a.dtype) for a in arrays]


def _exchange_sems(n):
    return [pltpu.SemaphoreType.DMA((n, N_DEV - 1)), pltpu.SemaphoreType.DMA((n, N_DEV - 1)),
            pltpu.SemaphoreType.DMA((n,))]


def _exchange(arrays, scatter, name):
    n = len(arrays)

    def body(*refs):
        copies = _exchange_copies(refs[:n], refs[n:2 * n], *refs[2 * n:], scatter)
        for cp in copies:
            cp.start()
        for cp in copies:
            cp.wait()

    any_spec = pl.BlockSpec(memory_space=pl.ANY)
    return pl.pallas_call(
        body, name=name, out_shape=tuple(_exchange_out_shapes(arrays, scatter)),
        in_specs=[any_spec] * n, out_specs=tuple([any_spec] * n),
        scratch_shapes=_exchange_sems(n),
        compiler_params=pltpu.CompilerParams(has_side_effects=True),
    )(*arrays)


def _gather_two_level(arrays, name):
    n = len(arrays)
    out_shape = [jax.ShapeDtypeStruct((N_DEV,) + tuple(a.shape), a.dtype) for a in arrays]

    def body(*refs):
        ins, outs = refs[:n], refs[n:2 * n]
        send_sems, recv_sems, loc_sems = refs[2 * n:]
        x, y, c = lax.axis_index("x"), lax.axis_index("y"), lax.axis_index("c")
        me, sibling = (x, y, c), (x, y, 1 - c)
        chips = [(1 - x, y), (x, 1 - y), (1 - x, 1 - y)]

        def slot(px, py, pc):
            return 4 * px + 2 * py + pc

        def copy(a, k, block, to, src=None):
            dst = outs[a].at[slot(*block)]
            return pltpu.make_async_remote_copy(
                src_ref=dst if src is None else src, dst_ref=dst,
                send_sem=send_sems.at[a, k], recv_sem=recv_sems.at[a, k],
                device_id=to, device_id_type=pl.DeviceIdType.MESH)

        local = [pltpu.make_async_copy(ins[a], outs[a].at[slot(*me)], loc_sems.at[a]) for a in range(n)]
        for cp in local:
            cp.start()
        first = []
        for a in range(n):
            first.append(copy(a, 0, me, sibling, src=ins[a]))
            first += [copy(a, 1 + j, me, (*chip, c), src=ins[a]) for j, chip in enumerate(chips)]
        for cp in first:
            cp.start()
        passed = []
        for j, chip in enumerate(chips):
            for a in range(n):
                copy(a, 1 + j, (*chip, c), me).wait_recv()
                cp = copy(a, 4 + j, (*chip, c), sibling)
                cp.start()
                passed.append(cp)
        for a in range(n):
            copy(a, 0, sibling, me).wait_recv()
            for j, chip in enumerate(chips):
                copy(a, 4 + j, (*chip, 1 - c), me).wait_recv()
        for cp in first + passed:
            cp.wait_send()
        for cp in local:
            cp.wait()

    any_spec = pl.BlockSpec(memory_space=pl.ANY)
    return pl.pallas_call(
        body, name=name, out_shape=tuple(out_shape),
        in_specs=[any_spec] * n, out_specs=tuple([any_spec] * n),
        scratch_shapes=[pltpu.SemaphoreType.DMA((n, N_DEV - 1)), pltpu.SemaphoreType.DMA((n, N_DEV - 1)),
                        pltpu.SemaphoreType.DMA((n,))],
        compiler_params=pltpu.CompilerParams(has_side_effects=True),
    )(*arrays)


N_CHIPS = 4


def _scatter_to_sibling(arrays, name):
    n = len(arrays)
    out_shape = [jax.ShapeDtypeStruct((N_CHIPS,) + tuple(a.shape[2:]), a.dtype) for a in arrays]

    def body(*refs):
        ins, outs = refs[:n], refs[n:2 * n]
        send_sems, recv_sems = refs[2 * n:]
        x, y, c = lax.axis_index("x"), lax.axis_index("y"), lax.axis_index("c")
        copies = []
        for a in range(n):
            for q in range(N_CHIPS):
                cp = pltpu.make_async_remote_copy(
                    src_ref=ins[a].at[q, 1 - c], dst_ref=outs[a].at[q],
                    send_sem=send_sems.at[a, q], recv_sem=recv_sems.at[a, q],
                    device_id=(x, y, 1 - c), device_id_type=pl.DeviceIdType.MESH)
                cp.start()
                copies.append(cp)
        for cp in copies:
            cp.wait()

    any_spec = pl.BlockSpec(memory_space=pl.ANY)
    return pl.pallas_call(
        body, name=name, out_shape=tuple(out_shape),
        in_specs=[any_spec] * n, out_specs=tuple([any_spec] * n),
        scratch_shapes=[pltpu.SemaphoreType.DMA((n, N_CHIPS)), pltpu.SemaphoreType.DMA((n, N_CHIPS))],
        compiler_params=pltpu.CompilerParams(has_side_effects=True),
    )(*arrays)


def _pair_add(mine, got, c_idx, name):
    _, _, R, C = mine.shape
    tr = 256 if (R % 256 == 0 and R > 256) else R

    def body(c_ref, m_ref, g_ref, o_ref):
        o_ref[...] = m_ref[...] + g_ref[...]

    grid_spec = pltpu.PrefetchScalarGridSpec(
        num_scalar_prefetch=1, grid=(N_CHIPS, R // tr),
        in_specs=[pl.BlockSpec((None, None, tr, C), lambda q, i, c_ref: (q, c_ref[0], i, 0)),
                  pl.BlockSpec((None, tr, C), lambda q, i, c_ref: (q, i, 0))],
        out_specs=pl.BlockSpec((None, tr, C), lambda q, i, c_ref: (q, i, 0)))
    return pl.pallas_call(
        body, name=name, grid_spec=grid_spec, out_shape=jax.ShapeDtypeStruct((N_CHIPS, R, C), mine.dtype),
        compiler_params=_params(dimension_semantics=("parallel", "parallel")),
    )(c_idx, mine, got)


def _chip_exchange_copies(ins, outs, send_sems, recv_sems, loc_sems):
    n = len(ins)
    x, y, c = lax.axis_index("x"), lax.axis_index("y"), lax.axis_index("c")
    myq = 2 * x + y
    copies = [pltpu.make_async_copy(ins[a].at[myq], outs[a].at[myq], loc_sems.at[a]) for a in range(n)]
    for k in range(1, N_CHIPS):
        qx = 1 - x if k & 2 else x
        qy = 1 - y if k & 1 else y
        for a in range(n):
            copies.append(pltpu.make_async_remote_copy(
                src_ref=ins[a].at[2 * qx + qy], dst_ref=outs[a].at[myq],
                send_sem=send_sems.at[a, k - 1], recv_sem=recv_sems.at[a, k - 1],
                device_id=(qx, qy, c), device_id_type=pl.DeviceIdType.MESH))
    return copies


def _chip_exchange_sems(n):
    return [pltpu.SemaphoreType.DMA((n, N_CHIPS - 1)), pltpu.SemaphoreType.DMA((n, N_CHIPS - 1)),
            pltpu.SemaphoreType.DMA((n,))]


def _dot_nt(a, b):
    return lax.dot_general(a, b, (((1,), (1,)), ((), ())), preferred_element_type=F32)


def _rhs_spec(b, tn, nt):
    if nt:
        return pl.BlockSpec((tn, b.shape[1]), lambda i, j: (j, 0))
    return pl.BlockSpec((b.shape[0], tn), lambda i, j: (0, j))


def _mm(a, b, out_dtype, name, tm=1024, tn=512, nt=False):
    M, K = a.shape
    N = b.shape[0] if nt else b.shape[1]
    tm, tn = _tile(M, tm), _tile(N, tn)
    dot = _dot_nt if nt else _dot

    def body(a_ref, b_ref, o_ref):
        o_ref[...] = dot(a_ref[...], b_ref[...]).astype(out_dtype)

    return pl.pallas_call(
        body, name=name, out_shape=jax.ShapeDtypeStruct((M, N), out_dtype),
        grid=(M // tm, N // tn),
        in_specs=[pl.BlockSpec((tm, K), lambda i, j: (i, 0)), _rhs_spec(b, tn, nt)],
        out_specs=pl.BlockSpec((tm, tn), lambda i, j: (i, j)),
        compiler_params=_params(dimension_semantics=("parallel", "parallel")),
    )(a, b)


def _mm_acc(a, b, name, tm=1408, tn=1408, tk=1024):
    M, S = a.shape
    _, N = b.shape
    tm, tn, tk = _tile(M, tm), _tile(N, tn), _tile(S, tk)

    def body(a_ref, b_ref, o_ref):
        @pl.when(pl.program_id(2) == 0)
        def _():
            o_ref[...] = jnp.zeros_like(o_ref)

        o_ref[...] += _dot(a_ref[...], b_ref[...])

    return pl.pallas_call(
        body, name=name, out_shape=jax.ShapeDtypeStruct((M, N), F32),
        grid=(M // tm, N // tn, S // tk),
        in_specs=[pl.BlockSpec((tm, tk), lambda i, j, k: (i, k)), pl.BlockSpec((tk, tn), lambda i, j, k: (k, j))],
        out_specs=pl.BlockSpec((tm, tn), lambda i, j, k: (i, j)),
        compiler_params=_params(dimension_semantics=("parallel", "parallel", "arbitrary")),
    )(a, b)


def _mm_acc_parts(a, parts, name, tm=1024, tk=1024):
    M, S = a.shape
    P, _, K = parts.shape
    tm, tk = _tile(M, tm), _tile(S, tk)

    def body(a_ref, b_ref, o_ref):
        @pl.when(pl.program_id(2) == 0)
        def _():
            o_ref[...] = jnp.zeros_like(o_ref)

        o_ref[...] += _dot(a_ref[...], b_ref[...])

    return pl.pallas_call(
        body, name=name, out_shape=jax.ShapeDtypeStruct((P, M, K), F32), grid=(P, M // tm, S // tk),
        in_specs=[pl.BlockSpec((tm, tk), lambda k, i, s: (i, s)), pl.BlockSpec((None, tk, K), lambda k, i, s: (k, s, 0))],
        out_specs=pl.BlockSpec((None, tm, K), lambda k, i, s: (k, i, 0)),
        compiler_params=_params(dimension_semantics=("parallel", "parallel", "arbitrary")),
    )(a, parts)


def _silu(z):
    return z * (1.0 / (1.0 + jnp.exp(-z)))


def _ada_fwd(c_all, w_shard, b_shard):
    n = w_shard.shape[1]

    def body(c_ref, w_ref, b_ref, o_ref):
        o_ref[...] = _dot(_silu(c_ref[...]), w_ref[...], precision=HIGHEST) + b_ref[...]

    return pl.pallas_call(body, name="ada_fwd", out_shape=jax.ShapeDtypeStruct((N_DEV, n), F32),
                          compiler_params=_params())(c_all, w_shard, b_shard)


def _ada_bwd(c_all_t, dmod_cols):
    D = c_all_t.shape[0]
    n = dmod_cols.shape[1]

    def body(ct_ref, dm_ref, o_ref):
        sc = _silu(ct_ref[...])
        dm = dm_ref[...]
        acc = sc[:, 0:1] * dm[0:1, :]
        for b in range(1, N_DEV):
            acc = acc + sc[:, b:b + 1] * dm[b:b + 1, :]
        o_ref[...] = acc

    return pl.pallas_call(body, name="ada_bwd", out_shape=jax.ShapeDtypeStruct((D, n), F32),
                          compiler_params=_params())(c_all_t, dmod_cols)


def _row_specs(tm, widths):
    return [pl.BlockSpec((tm, w), lambda i: (i, 0)) for w in widths]


def _vec_spec(w):
    return pl.BlockSpec((1, w), lambda i: (0, 0))


def _col_spec(tm, w):
    return pl.BlockSpec((w, tm), lambda i: (0, i))


def _prenorm(x, g, scale, shift, name):
    S, D = x.shape
    tm = _tile(S, 512)

    def body(x_ref, g_ref, sc_ref, sh_ref, h_ref, ht_ref):
        xv = x_ref[...]
        r = lax.rsqrt(jnp.mean(xv * xv, axis=-1, keepdims=True) + EPS)
        h = (xv * r) * g_ref[...] * (1.0 + sc_ref[...]) + sh_ref[...]
        h_ref[...] = h.astype(BF16)
        ht_ref[...] = h.T.astype(BF16)

    return pl.pallas_call(
        body, name=name, grid=(S // tm,),
        out_shape=(jax.ShapeDtypeStruct((S, D), BF16), jax.ShapeDtypeStruct((D, S), BF16)),
        in_specs=_row_specs(tm, [D]) + [_vec_spec(D)] * 3,
        out_specs=(_row_specs(tm, [D])[0], _col_spec(tm, D)),
        compiler_params=_params(dimension_semantics=("parallel",)),
    )(x, g, scale, shift)


def _group_ones():
    r = (lax.broadcasted_iota(jnp.int32, (2 * LANES, LANES), 0) % LANES) // HEAD_DIM
    c = lax.broadcasted_iota(jnp.int32, (2 * LANES, LANES), 1) // HEAD_DIM
    return (r == c).astype(BF16)


def _group_sum(t, ones):
    hi = t.astype(BF16)
    lo = (t - hi.astype(F32)).astype(BF16)
    return _dot(jnp.concatenate([hi, lo], axis=1), ones)


def _headnorm_fwd(o_f, o_s, g_f, g_s):
    dh, S = o_f.shape
    tm = _tile(S, 512)

    def body(of_ref, os_ref, gf_ref, gs_ref, mix_ref, mixt_ref):
        ones = _group_ones()
        for part, (o_ref, g_ref) in enumerate(((of_ref, gf_ref), (os_ref, gs_ref))):
            for t in range(dh // LANES):
                cols = slice(t * LANES, (t + 1) * LANES)
                out = slice(part * dh + t * LANES, part * dh + (t + 1) * LANES)
                o = o_ref[cols, :].T
                ms = _group_sum(o * o, ones) * (1.0 / HEAD_DIM)
                mixn = o * lax.rsqrt(ms + EPS) * g_ref[:, cols]
                mix_ref[:, out] = mixn.astype(BF16)
                mixt_ref[out, :] = mixn.T.astype(BF16)

    return pl.pallas_call(
        body, name="headnorm_fwd", grid=(S // tm,),
        out_shape=(jax.ShapeDtypeStruct((S, 2 * dh), BF16), jax.ShapeDtypeStruct((2 * dh, S), BF16)),
        in_specs=[_col_spec(tm, dh)] * 2 + [_vec_spec(dh)] * 2,
        out_specs=(_row_specs(tm, [2 * dh])[0], _col_spec(tm, 2 * dh)),
        compiler_params=_params(dimension_semantics=("parallel",)),
    )(o_f, o_s, g_f, g_s)


def _resid_prenorm(x, a_out, gate, g, scale, shift):
    S, D = x.shape
    tm = _tile(S, 512)

    def body(x_ref, a_ref, gt_ref, g_ref, sc_ref, sh_ref, x1_ref, h_ref, ht_ref):
        x1 = x_ref[...] + gt_ref[...] * a_ref[...]
        x1_ref[...] = x1
        r = lax.rsqrt(jnp.mean(x1 * x1, axis=-1, keepdims=True) + EPS)
        h = (x1 * r) * g_ref[...] * (1.0 + sc_ref[...]) + sh_ref[...]
        h_ref[...] = h.astype(BF16)
        ht_ref[...] = h.T.astype(BF16)

    return pl.pallas_call(
        body, name="resid_prenorm", grid=(S // tm,),
        out_shape=(jax.ShapeDtypeStruct((S, D), F32), jax.ShapeDtypeStruct((S, D), BF16),
                   jax.ShapeDtypeStruct((D, S), BF16)),
        in_specs=_row_specs(tm, [D, D]) + [_vec_spec(D)] * 4,
        out_specs=tuple(_row_specs(tm, [D, D]) + [_col_spec(tm, D)]),
        compiler_params=_params(dimension_semantics=("parallel",)),
    )(x, a_out, gate, g, scale, shift)


def _shift_down(main, halo, k):
    ext = jnp.concatenate([halo, main], axis=0)
    return pltpu.roll(ext, k, 0)[halo.shape[0]:]


def _shift_up(main, halo, k):
    ext = jnp.concatenate([main, halo], axis=0)
    n = ext.shape[0]
    return pltpu.roll(ext, n - k, 0)[:main.shape[0]]


def _conv(up, up_halo, w_ref, b_ref):
    return (w_ref[2:3, :] * up + w_ref[1:2, :] * _shift_down(up, up_halo, 1)
            + w_ref[0:1, :] * _shift_down(up, up_halo, 2) + b_ref[...])


def _prev_halo_map(tm):
    step = tm // HALO
    return lambda j, i: (jnp.maximum(i * step - 1, 0), j)


MLP_TM = 512
MLP_CT = 1408
CARRY = 8


def _mlp_up(h, wg, wv, cwg, cwv, cbg, cbv):
    S, D = h.shape
    F = wg.shape[1]
    tm, ct = _tile(S, MLP_TM), _tile(F, MLP_CT)
    nct = F // ct

    def body(h_ref, wg_ref, wv_ref, cwg_ref, cwv_ref, cbg_ref, cbv_ref,
             upg_ref, upv_ref, act_ref, actt_ref, hg_scr, hv_scr):
        i, j = pl.program_id(0), pl.program_id(1)
        hv = h_ref[...]
        us = []
        for w_ref, cw_ref, cb_ref, up_ref, scr in ((wg_ref, cwg_ref, cbg_ref, upg_ref, hg_scr),
                                                   (wv_ref, cwv_ref, cbv_ref, upv_ref, hv_scr)):
            up = _dot(hv, w_ref[...]).astype(BF16)
            up_ref[...] = up
            upf = up.astype(F32)
            halo = jnp.where(i == 0, 0.0, scr[j])
            us.append(_conv(upf, halo, cw_ref, cb_ref))
            scr[j] = upf[tm - CARRY:, :]
        act = _silu(us[0]) * us[1]
        act_ref[...] = act.astype(BF16)
        actt_ref[...] = act.T.astype(BF16)

    blk = pl.BlockSpec((tm, ct), lambda i, j: (i, j))
    wspec = pl.BlockSpec((D, ct), lambda i, j: (0, j))
    cwspec = pl.BlockSpec((CONV_W, ct), lambda i, j: (0, j))
    cbspec = pl.BlockSpec((1, ct), lambda i, j: (0, j))
    sds = jax.ShapeDtypeStruct((S, F), BF16)
    return pl.pallas_call(
        body, name="mlp_up", grid=(S // tm, nct),
        out_shape=(sds, sds, sds, jax.ShapeDtypeStruct((F, S), BF16)),
        in_specs=[pl.BlockSpec((tm, D), lambda i, j: (i, 0)), wspec, wspec, cwspec, cwspec, cbspec, cbspec],
        out_specs=(blk, blk, blk, pl.BlockSpec((ct, tm), lambda i, j: (j, i))),
        scratch_shapes=[pltpu.VMEM((nct, CARRY, ct), F32), pltpu.VMEM((nct, CARRY, ct), F32)],
        compiler_params=_params(dimension_semantics=("arbitrary", "arbitrary")),
    )(h, wg, wv, cwg, cwv, cbg, cbv)


def _conv_act_bwd(gm, w_down, up_g, up_v, cwg, cwv, cbg, cbv):
    S, F = up_g.shape
    D = gm.shape[1]
    tm, ct = _tile(S, 256), _tile(F, CONV_COLS)
    nct = F // ct

    def body(gm_ref, wd_ref, ug_ref, uv_ref, hg_ref, hv_ref, wg_ref, wv_ref, bg_ref, bv_ref,
             dug_ref, duv_ref, pg_ref, pv_ref):
        first = pl.program_id(1) == 0

        @pl.when(first)
        def _():
            pg_ref[...] = jnp.zeros_like(pg_ref)
            pv_ref[...] = jnp.zeros_like(pv_ref)

        da = _dot_nt(gm_ref[...], wd_ref[...])
        taps = []
        for u_ref, h_ref in ((ug_ref, hg_ref), (uv_ref, hv_ref)):
            h = jnp.where(first, 0.0, h_ref[...].astype(F32))
            uu = u_ref[...].astype(F32)
            taps.append((_shift_down(uu, h, 2), _shift_down(uu, h, 1), uu))
        u_g = wg_ref[0:1, :] * taps[0][0] + wg_ref[1:2, :] * taps[0][1] + wg_ref[2:3, :] * taps[0][2] + bg_ref[...]
        u_v = wv_ref[0:1, :] * taps[1][0] + wv_ref[1:2, :] * taps[1][1] + wv_ref[2:3, :] * taps[1][2] + bv_ref[...]
        sg = 1.0 / (1.0 + jnp.exp(-u_g))
        du_g = da * u_v * (sg * (1.0 + u_g * (1.0 - sg)))
        du_v = da * (u_g * sg)
        dug_ref[...] = du_g.astype(BF16)
        duv_ref[...] = du_v.astype(BF16)
        for du, tp, p_ref in ((du_g, taps[0], pg_ref), (du_v, taps[1], pv_ref)):
            for k in range(CONV_W):
                p_ref[k:k + 1, :] += jnp.sum(du * tp[k], axis=0, keepdims=True)
            p_ref[CONV_W:CONV_W + 1, :] += jnp.sum(du, axis=0, keepdims=True)

    main = pl.BlockSpec((tm, ct), lambda j, i: (i, j))
    halo = pl.BlockSpec((HALO, ct), _prev_halo_map(tm))
    wspec = pl.BlockSpec((CONV_W, ct), lambda j, i: (0, j))
    bspec = pl.BlockSpec((1, ct), lambda j, i: (0, j))
    pspec = pl.BlockSpec((8, ct), lambda j, i: (0, j))
    return pl.pallas_call(
        body, name="conv_act_bwd", grid=(nct, S // tm),
        out_shape=(jax.ShapeDtypeStruct((S, F), BF16), jax.ShapeDtypeStruct((S, F), BF16),
                   jax.ShapeDtypeStruct((8, F), F32), jax.ShapeDtypeStruct((8, F), F32)),
        in_specs=[pl.BlockSpec((tm, D), lambda j, i: (i, 0)), pl.BlockSpec((ct, D), lambda j, i: (j, 0)),
                  main, main, halo, halo, wspec, wspec, bspec, bspec],
        out_specs=(main, main, pspec, pspec),
        compiler_params=_params(dimension_semantics=("parallel", "arbitrary")),
    )(gm, w_down, up_g, up_v, up_g, up_v, cwg, cwv, cbg, cbv)


def _conv_bwd_taps(d, halo, w_ref):
    return w_ref[2:3, :] * d + w_ref[1:2, :] * _shift_up(d, halo, 1) + w_ref[0:1, :] * _shift_up(d, halo, 2)


def _scan_mats(R, nc, reverse):
    i = lax.broadcasted_iota(jnp.int32, (LANES, LANES), 0)
    j = lax.broadcasted_iota(jnp.int32, (LANES, LANES), 1)
    inner = ((i >= j) if reverse else (i <= j)).astype(F32)
    r = lax.broadcasted_iota(jnp.int32, (R, R), 0)
    c = lax.broadcasted_iota(jnp.int32, (R, R), 1)
    same = (r // nc) == (c // nc)
    outer = (same & ((c > r) if reverse else (c < r))).astype(F32)
    return inner, outer


def _chunk_scan(v, inner, outer, reverse):
    w = _dot(v, inner, precision=HIGHEST)
    col = 0 if reverse else LANES - 1
    carry = _dot(outer, w, precision=HIGHEST)[:, col:col + 1]
    return w + carry


def _fgate_fwd(z_rows, nc):
    R = z_rows.shape[0]

    def body(z_ref, f_ref):
        z = z_ref[...]
        logf = jnp.minimum(z, 0.0) - jnp.log(1.0 + jnp.exp(-jnp.abs(z)))
        inner, outer = _scan_mats(R, nc, False)
        f_ref[...] = _chunk_scan(logf, inner, outer, False)

    return pl.pallas_call(body, name="fgate_fwd", out_shape=jax.ShapeDtypeStruct((R, LANES), F32),
                          compiler_params=_params())(z_rows)


def _fgate_bwd(dfk_neg_rows, dfq_rows, z_rows, nc):
    R = z_rows.shape[0]
    nh = R // nc

    def body(dfk_ref, dfq_ref, z_ref, dz_ref, db_ref):
        inner, outer = _scan_mats(R, nc, True)
        dlogf = _chunk_scan(dfq_ref[...] - dfk_ref[...], inner, outer, True)
        dz = dlogf * (1.0 / (1.0 + jnp.exp(z_ref[...])))
        dz_ref[...] = dz
        hr = lax.broadcasted_iota(jnp.int32, (nh, R), 0)
        hc = lax.broadcasted_iota(jnp.int32, (nh, R), 1) // nc
        per_head = _dot((hr == hc).astype(F32), dz, precision=HIGHEST)
        db_ref[...] = jnp.sum(per_head, axis=1, keepdims=True)

    return pl.pallas_call(
        body, name="fgate_bwd",
        out_shape=(jax.ShapeDtypeStruct((R, LANES), F32), jax.ShapeDtypeStruct((nh, 1), F32)),
        compiler_params=_params())(dfk_neg_rows, dfq_rows, z_rows)


_NEG = -1e30
SKIP_BELOW = -106.0
_SCALE = HEAD_DIM ** -0.5
N_SCAN = ATT_BK // SCAN_BK
F_PARTS = 3
Q_F_LANE = HEAD_DIM
Q_ONE_LANE = HEAD_DIM + F_PARTS


def _kv_slice(j):
    return pl.ds(pl.multiple_of(j * ATT_BK, ATT_BK), ATT_BK)


def _mask_t(strict):
    s = lax.broadcasted_iota(jnp.int32, (ATT_BK, ATT_BQ), 0)
    t = lax.broadcasted_iota(jnp.int32, (ATT_BK, ATT_BQ), 1)
    return (s < t) if strict else (s <= t)


def _walk_down(i, step, alive, carry):
    carry = step(i, carry, True)

    def cond(st):
        n, go, _ = st
        return jnp.logical_and(n < i, go)

    def body(st):
        n, _, cr = st
        j = i - 1 - n
        cr = step(j, cr, False)
        return n + 1, alive(jnp.maximum(j - 1, 0), cr), cr

    return lax.while_loop(cond, body, (jnp.int32(0), alive(jnp.maximum(i - 1, 0), carry), carry))[2]


def _t_block(rows):
    return pl.BlockSpec((None, rows, ATT_BQ), lambda h, i, *_: (h, 0, i))


def _t_full(rows, S):
    return pl.BlockSpec((None, rows, S), lambda h, i, *_: (h, 0, 0))


def _n_block():
    return pl.BlockSpec((None, ATT_BQ, LANES), lambda h, i, *_: (h, i, 0))


def _n_full(S):
    return pl.BlockSpec((None, S, LANES), lambda h, i, *_: (h, 0, 0))


def _skip_bounds(k_sq, f_rows):
    H, S = f_rows.shape
    f_end = f_rows.reshape(H, S // ATT_BK, ATT_BK)[:, :, -1]
    return f_end, lax.cummax(jnp.sqrt(k_sq), axis=1)


def _bf16_parts(f):
    hi = f.astype(BF16).astype(F32)
    mid = (f - hi).astype(BF16).astype(F32)
    return hi, mid, (f - hi - mid).astype(BF16).astype(F32)


def _att_prep(qkv, f_pairs):
    S = qkv.shape[0]
    n_pairs = qkv.shape[1] // (6 * LANES)
    H = 2 * n_pairs
    tm = ATT_BK
    assert S % tm == 0, S

    def body(qf_ref, kf_ref, vf_ref, qs_ref, ks_ref, vs_ref, f_ref,
             fqn, fqt, fkn, fkt, fvn, fvt, sqn, sqt, skn, skt, svn, svt, ksq_ref):
        lane = lax.broadcasted_iota(jnp.int32, (1, LANES), 1)
        f = f_ref[...]

        def head(ref, e):
            t = ref[...].astype(F32)
            if e == 1:
                t = pltpu.roll(t, HEAD_DIM, 1)
            return jnp.where(lane < HEAD_DIM, t, 0.0)

        def at(first):
            return jnp.logical_and(lane >= first, lane < first + F_PARTS)

        for e in range(2):
            parts = _bf16_parts(f[:, e:e + 1])
            f_lanes = sum(jnp.where(lane == Q_F_LANE + k, parts[k], 0.0) for k in range(F_PARTS))
            nf_lanes = sum(jnp.where(lane == Q_ONE_LANE + k, parts[k], 0.0) for k in range(F_PARTS))
            kf = head(kf_ref, e)
            k_sq = jnp.max(jnp.sum(kf * kf, axis=1, keepdims=True), axis=0, keepdims=True)
            ksq_ref[e, 0] = jnp.broadcast_to(k_sq, (1, LANES))
            vals = (
                (fqn, fqt, LANES, head(qf_ref, e) * _SCALE + f_lanes + jnp.where(at(Q_ONE_LANE), 1.0, 0.0)),
                (fkn, fkt, LANES, kf + jnp.where(at(Q_F_LANE), 1.0, 0.0) - nf_lanes),
                (fvn, fvt, HEAD_DIM, head(vf_ref, e)),
                (sqn, sqt, LANES, head(qs_ref, e) * _SCALE),
                (skn, skt, LANES, head(ks_ref, e)),
                (svn, svt, HEAD_DIM, head(vs_ref, e)),
            )
            for n_ref, t_ref, rows, val in vals:
                n_ref[e] = val.astype(BF16)
                t_ref[e] = val.T[:rows].astype(BF16)

    col = lambda base: pl.BlockSpec((tm, LANES), lambda i, p: (i, base + p))
    n_spec = pl.BlockSpec((2, tm, LANES), lambda i, p: (p, i, 0))
    t_spec = lambda rows: pl.BlockSpec((2, rows, tm), lambda i, p: (p, 0, i))
    n_sds = jax.ShapeDtypeStruct((H, S, LANES), BF16)
    t_sds = lambda rows: jax.ShapeDtypeStruct((H, rows, S), BF16)
    group = ([n_sds, t_sds(LANES), n_sds, t_sds(LANES), n_sds, t_sds(HEAD_DIM)],
             [n_spec, t_spec(LANES), n_spec, t_spec(LANES), n_spec, t_spec(HEAD_DIM)])
    res = pl.pallas_call(
        body, name="att_prep", grid=(S // tm, n_pairs),
        out_shape=tuple(group[0] * 2 + [jax.ShapeDtypeStruct((H, S // tm, 1, LANES), F32)]),
        in_specs=[col(k * n_pairs) for k in range(6)] + [pl.BlockSpec((None, tm, 2), lambda i, p: (p, i, 0))],
        out_specs=tuple(group[1] * 2 + [pl.BlockSpec((2, 1, 1, LANES), lambda i, p: (p, i, 0, 0))]),
        compiler_params=_params(dimension_semantics=("parallel", "parallel")),
    )(qkv, qkv, qkv, qkv, qkv, qkv, f_pairs)
    names = ("q_n", "q_t", "k_n", "k_t", "v_n", "v_t")
    return dict(zip(names, res[:6])), dict(zip(names, res[6:12])), res[12][:, :, 0, 0]


def _fox_reach(qt, fend_ref, kmax_ref, h):
    qf = qt.astype(F32)
    q_norm = jnp.sqrt(jnp.sum(jnp.square(qf[:HEAD_DIM]), axis=0, keepdims=True))
    f_t = jnp.sum(qf[Q_F_LANE:Q_F_LANE + F_PARTS], axis=0, keepdims=True)
    return lambda j: q_norm * kmax_ref[h, j] + f_t - fend_ref[h, j]


def _fox_fwd(q_t, k_n, v_t, f_end, k_max, shards):
    H, _, S = q_t.shape
    n, nq = len(shards), S // ATT_BQ

    def body(fend_ref, kmax_ref, qt_ref, k_ref, vt_ref, *rest):
        ins, (ot_ref, lse_ref), outs, sems = rest[:n], rest[n:n + 2], rest[n + 2:2 * n + 2], rest[2 * n + 2:]
        h, i = pl.program_id(0), pl.program_id(1)

        @pl.when(jnp.logical_and(h == 0, i == 0))
        def _():
            for cp in _exchange_copies(ins, outs, *sems, False):
                cp.start()

        qt = qt_ref[...]
        reach = _fox_reach(qt, fend_ref, kmax_ref, h)

        def step(j, carry, masked):
            m, l, acc = carry
            ks = _kv_slice(j)
            s = _dot(k_ref[ks, :], qt)
            if masked:
                s = jnp.where(_mask_t(False), s, _NEG)
            mn = jnp.maximum(m, jnp.max(s, axis=0, keepdims=True))
            alpha = jnp.exp(m - mn)
            p = jnp.exp(s - mn)
            l = alpha * l + jnp.sum(p, axis=0, keepdims=True)
            acc = acc * alpha + _dot(vt_ref[:, ks], p.astype(BF16))
            return mn, l, acc

        def alive(j, carry):
            return jnp.max(reach(j) - carry[0]) > SKIP_BELOW

        row = jnp.zeros((1, ATT_BQ), F32)
        m, l, acc = _walk_down(i, step, alive, (row + _NEG, row, jnp.zeros((HEAD_DIM, ATT_BQ), F32)))
        ot_ref[...] = acc / l
        lse_ref[...] = m + jnp.log(l)

        @pl.when(jnp.logical_and(h == H - 1, i == nq - 1))
        def _():
            for cp in _exchange_copies(ins, outs, *sems, False):
                cp.wait()

    any_spec = pl.BlockSpec(memory_space=pl.ANY)
    grid_spec = pltpu.PrefetchScalarGridSpec(
        num_scalar_prefetch=2, grid=(H, nq),
        in_specs=[_t_block(LANES), _n_full(S), _t_full(HEAD_DIM, S)] + [any_spec] * n,
        out_specs=tuple([_t_block(HEAD_DIM), _t_block(1)] + [any_spec] * n),
        scratch_shapes=_exchange_sems(n))
    res = pl.pallas_call(
        body, name="fox_fwd", grid_spec=grid_spec,
        out_shape=tuple([jax.ShapeDtypeStruct((H, HEAD_DIM, S), F32), jax.ShapeDtypeStruct((H, 1, S), F32)]
                        + _exchange_out_shapes(shards, False)),
        compiler_params=_params(dimension_semantics=("arbitrary", "arbitrary"), has_side_effects=True),
    )(f_end, k_max, q_t, k_n, v_t, *shards)
    return res[0], res[1], res[2:]


def _fox_bwd(q_t, q_n, k_n, k_t, v_n, do_t, do_n, o_t, lse, f_end, k_max):
    H, _, S = q_t.shape

    def body(fend_ref, kmax_ref, qt_ref, qn_ref, k_ref, kt_ref, v_ref, dot_ref, don_ref, ot_ref, lse_ref,
             dqt_ref, dk_ref, dv_ref):
        h, i = pl.program_id(0), pl.program_id(1)

        @pl.when(i == 0)
        def _():
            dk_ref[...] = jnp.zeros_like(dk_ref)
            dv_ref[...] = jnp.zeros_like(dv_ref)

        qt, qn, dot, don = qt_ref[...], qn_ref[...], dot_ref[...], don_ref[...]
        lse = lse_ref[...]
        delta = jnp.sum(dot[:HEAD_DIM].astype(F32) * ot_ref[...], axis=0, keepdims=True)
        reach = _fox_reach(qt, fend_ref, kmax_ref, h)

        def alive(j, dq):
            return jnp.max(reach(j) - lse) > SKIP_BELOW

        def step(j, dq, masked):
            ks = _kv_slice(j)
            s = _dot(k_ref[ks, :], qt)
            if masked:
                s = jnp.where(_mask_t(False), s, _NEG)
            p = jnp.exp(s - lse)
            ds = (p * (_dot(v_ref[ks, :], dot) - delta)).astype(BF16)
            dk_ref[ks, :] += _dot(ds, qn)
            dv_ref[ks, :] += _dot(p.astype(BF16), don)
            return dq + _dot(kt_ref[:, ks], ds)

        dqt_ref[...] = _walk_down(i, step, alive, jnp.zeros((LANES, ATT_BQ), F32))

    grid_spec = pltpu.PrefetchScalarGridSpec(
        num_scalar_prefetch=2, grid=(H, S // ATT_BQ),
        in_specs=[_t_block(LANES), _n_block(), _n_full(S), _t_full(LANES, S), _n_full(S),
                  _t_block(LANES), _n_block(), _t_block(HEAD_DIM), _t_block(1)],
        out_specs=(_t_block(LANES), _n_full(S), _n_full(S)))
    return pl.pallas_call(
        body, name="fox_bwd", grid_spec=grid_spec,
        out_shape=(jax.ShapeDtypeStruct((H, LANES, S), F32), jax.ShapeDtypeStruct((H, S, LANES), F32),
                   jax.ShapeDtypeStruct((H, S, LANES), F32)),
        compiler_params=_params(dimension_semantics=("parallel", "arbitrary")),
    )(f_end, k_max, q_t, q_n, k_n, k_t, v_n, do_t, do_n, o_t, lse)


def _scan_lhs():
    r = lax.broadcasted_iota(jnp.int32, (SCAN_BK, 2 * SCAN_BK), 0)
    c = lax.broadcasted_iota(jnp.int32, (SCAN_BK, 2 * SCAN_BK), 1) % SCAN_BK
    return (c >= r).astype(BF16)


def _suffix_sum(t, lhs):
    hi = t.astype(BF16)
    lo = (t - hi.astype(F32)).astype(BF16)
    return _dot(lhs, jnp.concatenate([hi, lo], axis=0))


def _sb_scores(k, qt, mask):
    z = _dot(k, qt)
    e = jnp.exp(-jnp.abs(z))
    lb = -(jnp.maximum(z, 0.0) + jnp.log(1.0 + e))
    if mask is not None:
        lb = jnp.where(mask, lb, 0.0)
    return z, e, lb


def _scan_blocks():
    return [slice(u * SCAN_BK, (u + 1) * SCAN_BK) for u in reversed(range(N_SCAN))]


def _sb_fwd(q_t, k_n, v_t):
    H, _, S = q_t.shape

    def body(qt_ref, k_ref, vt_ref, ot_ref):
        i = pl.program_id(1)
        qt = qt_ref[...]
        lhs = _scan_lhs()

        def step(j, carry, masked):
            c, acc = carry
            ks = _kv_slice(j)
            mask = _mask_t(True) if masked else None
            z, _, lb = _sb_scores(k_ref[ks, :], qt, mask)
            parts = []
            for sl in _scan_blocks():
                rin = _suffix_sum(lb[sl], lhs)
                a = jnp.exp(z[sl] + rin + c)
                if masked:
                    a = jnp.where(mask[sl], a, 0.0)
                parts.append(a.astype(BF16))
                c = c + rin[0:1, :]
            a_all = jnp.concatenate(parts[::-1], axis=0)
            return c, acc + _dot(vt_ref[:, ks], a_all)

        carry = (jnp.zeros((1, ATT_BQ), F32), jnp.zeros((HEAD_DIM, ATT_BQ), F32))
        ot_ref[...] = _walk_down(i, step, lambda j, cr: jnp.max(cr[0]) > SKIP_BELOW, carry)[1]

    return pl.pallas_call(
        body, name="sb_fwd", grid=(H, S // ATT_BQ),
        out_shape=jax.ShapeDtypeStruct((H, HEAD_DIM, S), F32),
        in_specs=[_t_block(LANES), _n_full(S), _t_full(HEAD_DIM, S)],
        out_specs=_t_block(HEAD_DIM),
        compiler_params=_params(dimension_semantics=("parallel", "parallel")),
    )(q_t, k_n, v_t)


def _sb_bwd(q_t, q_n, k_n, k_t, v_n, do_t, do_n, o_t, bound):
    H, _, S = q_t.shape
    n, nq = len(bound), S // ATT_BQ

    def body(qt_ref, qn_ref, k_ref, kt_ref, v_ref, dot_ref, don_ref, ot_ref, *rest):
        ins, (dqt_ref, dk_ref, dv_ref) = rest[:n], rest[n:n + 3]
        outs, sems = rest[n + 3:2 * n + 3], rest[2 * n + 3:]
        h, i = pl.program_id(0), pl.program_id(1)

        @pl.when(jnp.logical_and(h == 0, i == 0))
        def _():
            for cp in _exchange_copies(ins, outs, *sems, True):
                cp.start()

        @pl.when(i == 0)
        def _():
            dk_ref[...] = jnp.zeros_like(dk_ref)
            dv_ref[...] = jnp.zeros_like(dv_ref)

        qt, qn, dot, don = qt_ref[...], qn_ref[...], dot_ref[...], don_ref[...]
        lhs = _scan_lhs()
        delta = jnp.sum(dot[:HEAD_DIM].astype(F32) * ot_ref[...], axis=0, keepdims=True)

        def step(j, carry, masked):
            c, g, dq = carry
            ks = _kv_slice(j)
            mask = _mask_t(True) if masked else None
            z, e, lb = _sb_scores(k_ref[ks, :], qt, mask)
            da = _dot(v_ref[ks, :], dot)
            a_parts, dz_parts = [], []
            for sl in _scan_blocks():
                rin = _suffix_sum(lb[sl], lhs)
                a = jnp.exp(z[sl] + rin + c)
                if masked:
                    a = jnp.where(mask[sl], a, 0.0)
                ab = a.astype(BF16)
                gg = ab.astype(F32) * da[sl]
                rgin = _suffix_sum(gg, lhs)
                rinv = 1.0 / (1.0 + e[sl])
                sig = jnp.where(z[sl] >= 0.0, rinv, e[sl] * rinv)
                dz = gg - sig * (delta - g - (rgin - gg))
                if masked:
                    dz = jnp.where(mask[sl], dz, 0.0)
                a_parts.append(ab)
                dz_parts.append(dz.astype(BF16))
                c = c + rin[0:1, :]
                g = g + rgin[0:1, :]
            ab_all = jnp.concatenate(a_parts[::-1], axis=0)
            dzb = jnp.concatenate(dz_parts[::-1], axis=0)
            dk_ref[ks, :] += _dot(dzb, qn)
            dv_ref[ks, :] += _dot(ab_all, don)
            return c, g, dq + _dot(kt_ref[:, ks], dzb)

        row = jnp.zeros((1, ATT_BQ), F32)
        carry = (row, row, jnp.zeros((LANES, ATT_BQ), F32))
        dqt_ref[...] = _walk_down(i, step, lambda j, cr: jnp.max(cr[0]) > SKIP_BELOW, carry)[2]

        @pl.when(jnp.logical_and(h == H - 1, i == nq - 1))
        def _():
            for cp in _exchange_copies(ins, outs, *sems, True):
                cp.wait()

    any_spec = pl.BlockSpec(memory_space=pl.ANY)
    res = pl.pallas_call(
        body, name="sb_bwd", grid=(H, nq),
        out_shape=tuple([jax.ShapeDtypeStruct((H, LANES, S), F32), jax.ShapeDtypeStruct((H, S, LANES), F32),
                         jax.ShapeDtypeStruct((H, S, LANES), F32)] + _exchange_out_shapes(bound, True)),
        in_specs=[_t_block(LANES), _n_block(), _n_full(S), _t_full(LANES, S), _n_full(S),
                  _t_block(LANES), _n_block(), _t_block(HEAD_DIM)] + [any_spec] * n,
        out_specs=tuple([_t_block(LANES), _n_full(S), _n_full(S)] + [any_spec] * n),
        scratch_shapes=_exchange_sems(n),
        compiler_params=_params(dimension_semantics=("arbitrary", "arbitrary"), has_side_effects=True),
    )(q_t, q_n, k_n, k_t, v_n, do_t, do_n, o_t, *bound)
    return res[0], res[1], res[2], res[3:]


def _dqkv_assemble(dqf_t, dkf, dvf, dqs_t, dks, dvs):
    H, _, S = dqf_t.shape
    n_pairs = H // 2
    tm = _tile(S, 512)

    def body(dqf_ref, dkf_ref, dvf_ref, dqs_ref, dks_ref, dvs_ref, out_ref, dfk_ref):
        lane = lax.broadcasted_iota(jnp.int32, (1, LANES), 1)
        slabs = ((dqf_ref, True), (dkf_ref, False), (dvf_ref, False),
                 (dqs_ref, True), (dks_ref, False), (dvs_ref, False))
        for k, (ref, transposed) in enumerate(slabs):
            if transposed:
                t0, t1 = ref[0].T * _SCALE, ref[1].T * _SCALE
            else:
                t0, t1 = ref[0], ref[1]
            out_ref[k] = jnp.where(lane < HEAD_DIM, t0, pltpu.roll(t1, HEAD_DIM, 1)).astype(BF16)
        for e in range(2):
            dfk_ref[e] = dkf_ref[e].T[Q_ONE_LANE:Q_ONE_LANE + 1, :]

    t_spec = pl.BlockSpec((2, LANES, tm), lambda i, p: (p, 0, i))
    n_spec = pl.BlockSpec((2, tm, LANES), lambda i, p: (p, i, 0))
    return pl.pallas_call(
        body, name="dqkv_assemble", grid=(S // tm, n_pairs),
        out_shape=(jax.ShapeDtypeStruct((6, S, n_pairs * LANES), BF16), jax.ShapeDtypeStruct((H, 1, S), F32)),
        in_specs=[t_spec, n_spec, n_spec, t_spec, n_spec, n_spec],
        out_specs=(pl.BlockSpec((6, tm, LANES), lambda i, p: (0, i, p)),
                   pl.BlockSpec((2, 1, tm), lambda i, p: (p, 0, i))),
        compiler_params=_params(dimension_semantics=("parallel", "parallel")),
    )(dqf_t, dkf, dvf, dqs_t, dks, dvs)


def _acc_spec(w):
    return pl.BlockSpec((1, w), lambda i: (0, 0))


def _loss_head(x1, act, w_down, gate_m, g_final, target):
    S, D = x1.shape
    F = act.shape[1]
    tm = _tile(S, 512)

    def body(x1_ref, act_ref, w_ref, gt_ref, gf_ref, tg_ref, dx2_ref, gm_ref, loss_ref, dgf_ref, dgt_ref):
        @pl.when(pl.program_id(0) == 0)
        def _():
            loss_ref[...] = jnp.zeros_like(loss_ref)
            dgf_ref[...] = jnp.zeros_like(dgf_ref)
            dgt_ref[...] = jnp.zeros_like(dgt_ref)

        mo = _dot(act_ref[...], w_ref[...])
        x2 = x1_ref[...] + gt_ref[...] * mo
        r = lax.rsqrt(jnp.mean(x2 * x2, axis=-1, keepdims=True) + EPS)
        xh = x2 * r
        diff = xh * gf_ref[...] - tg_ref[...]
        loss_ref[...] += (0.5 / D) * jnp.sum(diff * diff)
        dy = diff * (1.0 / D)
        dgf_ref[...] += jnp.sum(dy * xh, axis=0, keepdims=True)
        dxh = dy * gf_ref[...]
        dx2 = r * (dxh - xh * jnp.mean(dxh * xh, axis=-1, keepdims=True))
        dx2_ref[...] = dx2
        gm_ref[...] = (dx2 * gt_ref[...]).astype(BF16)
        dgt_ref[...] += jnp.sum(dx2 * mo, axis=0, keepdims=True)

    return pl.pallas_call(
        body, name="loss_head", grid=(S // tm,),
        out_shape=(jax.ShapeDtypeStruct((S, D), F32), jax.ShapeDtypeStruct((S, D), BF16),
                   jax.ShapeDtypeStruct((1, LANES), F32), jax.ShapeDtypeStruct((1, D), F32),
                   jax.ShapeDtypeStruct((1, D), F32)),
        in_specs=_row_specs(tm, [D, F]) + [pl.BlockSpec((F, D), lambda i: (0, 0))] + [_vec_spec(D)] * 2
        + _row_specs(tm, [D]),
        out_specs=tuple(_row_specs(tm, [D, D]) + [_acc_spec(LANES), _acc_spec(D), _acc_spec(D)]),
        compiler_params=_params(dimension_semantics=("arbitrary",)),
    )(x1, act, w_down, gate_m, g_final, target)


def _norm_bwd(lhs, rhs, xin, dres, g, scale, name, gate=None, branch=None, bound=(), conv=None, tm=256):
    S, D = xin.shape
    tm = _tile(S, tm)
    gated = gate is not None
    nl, nb, n_steps = len(lhs), len(bound), S // tm
    nc_ = nl if conv else 0
    n_out = (6 if gated else 4) + nc_

    def body(*refs):
        l_refs, r_refs, rest = refs[:nl], refs[nl:2 * nl], refs[2 * nl:]
        halo_refs, cw_refs, rest = rest[:nc_], rest[nc_:2 * nc_], rest[2 * nc_:]
        if gated:
            x_ref, dr_ref, g_ref, sc_ref, gt_ref, br_ref = rest[:6]
            rest = rest[6:]
        else:
            x_ref, dr_ref, g_ref, sc_ref = rest[:4]
            rest = rest[4:]
        ins, outs, ex_outs, sems = rest[:nb], rest[nb:nb + n_out], rest[nb + n_out:2 * nb + n_out], rest[2 * nb + n_out:]
        dx_ref, dsc_ref, dsh_ref, dg_ref = outs[:4]
        sums = (dsc_ref, dsh_ref, dg_ref) + ((outs[5],) if gated else ())
        dup_refs = outs[n_out - nc_:]
        i = pl.program_id(0)

        @pl.when(i == 0)
        def _():
            for s_ref in sums:
                s_ref[...] = jnp.zeros_like(s_ref)
            if nb:
                for cp in _chip_exchange_copies(ins, ex_outs, *sems):
                    cp.start()

        dhv = None
        for k, (l_ref, r_ref) in enumerate(zip(l_refs, r_refs)):
            if conv:
                halo = jnp.where(i == n_steps - 1, 0.0, halo_refs[k][...].astype(F32))
                dup = _conv_bwd_taps(l_ref[...].astype(F32), halo, cw_refs[k]).astype(BF16)
                dup_refs[k][...] = dup
                terms = [_dot_nt(dup, r_ref[...])]
            elif len(l_ref.shape) == 3:
                K = l_ref.shape[2]
                terms = [_dot_nt(l_ref[k], r_ref[:, k * K:(k + 1) * K]) for k in range(l_ref.shape[0])]
            else:
                terms = [_dot_nt(l_ref[...], r_ref[...])]
            for t in terms:
                dhv = t if dhv is None else dhv + t
        xv = x_ref[...]
        r = lax.rsqrt(jnp.mean(xv * xv, axis=-1, keepdims=True) + EPS)
        xh = xv * r
        dsc_ref[...] += jnp.sum(dhv * (xh * g_ref[...]), axis=0, keepdims=True)
        dsh_ref[...] += jnp.sum(dhv, axis=0, keepdims=True)
        dn = dhv * (1.0 + sc_ref[...])
        dg_ref[...] += jnp.sum(dn * xh, axis=0, keepdims=True)
        dxh = dn * g_ref[...]
        dx = dr_ref[...] + r * (dxh - xh * jnp.mean(dxh * xh, axis=-1, keepdims=True))
        dx_ref[...] = dx
        if gated:
            outs[4][...] = (dx * gt_ref[...]).astype(BF16)
            outs[5][...] += jnp.sum(dx * br_ref[...], axis=0, keepdims=True)

        if nb:
            @pl.when(i == n_steps - 1)
            def _():
                for cp in _chip_exchange_copies(ins, ex_outs, *sems):
                    cp.wait()

    def l_spec(a):
        if a.ndim == 3:
            return pl.BlockSpec((a.shape[0], tm, a.shape[2]), lambda i: (0, i, 0))
        return pl.BlockSpec((tm, a.shape[1]), lambda i: (i, 0))

    any_spec = pl.BlockSpec(memory_space=pl.ANY)
    vec = jax.ShapeDtypeStruct((1, D), F32)
    out_shape = [jax.ShapeDtypeStruct((S, D), F32), vec, vec, vec]
    out_specs = _row_specs(tm, [D]) + [_acc_spec(D)] * 3
    in_specs = [l_spec(a) for a in lhs] + [pl.BlockSpec(b.shape, lambda i: (0, 0)) for b in rhs]
    args = list(lhs) + list(rhs)
    if conv:
        step, last_halo = tm // HALO, S // HALO - 1
        in_specs += [pl.BlockSpec((HALO, a.shape[1]), lambda i: (jnp.minimum((i + 1) * step, last_halo), 0))
                     for a in lhs]
        in_specs += [pl.BlockSpec(w.shape, lambda i: (0, 0)) for w in conv]
        args += list(lhs) + list(conv)
    in_specs += _row_specs(tm, [D, D]) + [_vec_spec(D)] * 2
    args += [xin, dres, g, scale]
    if gated:
        out_shape += [jax.ShapeDtypeStruct((S, D), BF16), vec]
        out_specs += _row_specs(tm, [D]) + [_acc_spec(D)]
        in_specs += [_vec_spec(D)] + _row_specs(tm, [D])
        args += [gate, branch]
    if conv:
        out_shape += [jax.ShapeDtypeStruct(a.shape, BF16) for a in lhs]
        out_specs += [l_spec(a) for a in lhs]
    res = pl.pallas_call(
        body, name=name, grid=(n_steps,),
        out_shape=tuple(out_shape + [jax.ShapeDtypeStruct(b.shape, b.dtype) for b in bound]),
        in_specs=in_specs + [any_spec] * nb, out_specs=tuple(out_specs + [any_spec] * nb),
        scratch_shapes=_chip_exchange_sems(nb) if nb else [],
        compiler_params=_params(dimension_semantics=("arbitrary",)),
    )(*args, *bound)
    return tuple(res[:n_out]) + (tuple(res[n_out:]),)


def _headnorm_bwd(dmix, o_f, o_s, g_f, g_s):
    dh, S = o_f.shape
    H = dh // HEAD_DIM
    tm = _tile(S, 256)

    def body(dm_ref, of_ref, os_ref, gf_ref, gs_ref, fn_ref, ft_ref, sn_ref, st_ref, dgf_ref, dgs_ref):
        @pl.when(pl.program_id(0) == 0)
        def _():
            dgf_ref[...] = jnp.zeros_like(dgf_ref)
            dgs_ref[...] = jnp.zeros_like(dgs_ref)

        ones = _group_ones()
        lane = lax.broadcasted_iota(jnp.int32, (1, LANES), 1)
        parts = ((of_ref, gf_ref, fn_ref, ft_ref, dgf_ref), (os_ref, gs_ref, sn_ref, st_ref, dgs_ref))
        for part, (o_ref, g_ref, n_ref, t_ref, dg_ref) in enumerate(parts):
            for t in range(dh // LANES):
                cols = slice(t * LANES, (t + 1) * LANES)
                o = o_ref[cols, :].T
                dm = dm_ref[:, part * dh + t * LANES: part * dh + (t + 1) * LANES]
                r = lax.rsqrt(_group_sum(o * o, ones) * (1.0 / HEAD_DIM) + EPS)
                oh = o * r
                dg_ref[:, cols] += jnp.sum(dm * oh, axis=0, keepdims=True)
                dn = dm * g_ref[:, cols]
                mean = _group_sum(dn * oh, ones) * (1.0 / HEAD_DIM)
                do = r * (dn - oh * mean)
                for e in range(2):
                    d = do if e == 0 else pltpu.roll(do, HEAD_DIM, 1)
                    d = jnp.where(lane < HEAD_DIM, d, 0.0)
                    n_ref[2 * t + e] = d.astype(BF16)
                    t_ref[2 * t + e] = d.T.astype(BF16)

    vec = jax.ShapeDtypeStruct((1, dh), F32)
    n_sds = jax.ShapeDtypeStruct((H, S, LANES), BF16)
    t_sds = jax.ShapeDtypeStruct((H, LANES, S), BF16)
    n_spec = pl.BlockSpec((H, tm, LANES), lambda i: (0, i, 0))
    t_spec = pl.BlockSpec((H, LANES, tm), lambda i: (0, 0, i))
    return pl.pallas_call(
        body, name="headnorm_bwd", grid=(S // tm,),
        out_shape=(n_sds, t_sds, n_sds, t_sds, vec, vec),
        in_specs=_row_specs(tm, [2 * dh]) + [_col_spec(tm, dh)] * 2 + [_vec_spec(dh)] * 2,
        out_specs=(n_spec, t_spec, n_spec, t_spec, _acc_spec(dh), _acc_spec(dh)),
        compiler_params=_params(dimension_semantics=("arbitrary",)),
    )(dmix, o_f, o_s, g_f, g_s)


def _adamw(w, gslots, m, v, name):
    R, C = w.shape
    n = gslots.shape[0]
    tr = 256 if (R % 256 == 0 and R > 256) else R
    bc1 = 1.0 - ADAM_B1 ** ADAM_STEP
    bc2 = 1.0 - ADAM_B2 ** ADAM_STEP

    def body(w_ref, gs_ref, m_ref, v_ref, g_ref, d_ref, nm_ref, nv_ref):
        g = gs_ref[0]
        for s in range(1, n):
            g = g + gs_ref[s]
        nm = ADAM_B1 * m_ref[...] + (1.0 - ADAM_B1) * g
        nv = ADAM_B2 * v_ref[...] + (1.0 - ADAM_B2) * (g * g)
        g_ref[...] = g
        nm_ref[...] = nm
        nv_ref[...] = nv
        d_ref[...] = -ADAM_LR * ((nm / bc1) / (jnp.sqrt(nv / bc2) + ADAM_EPS) + ADAM_WD * w_ref[...])

    blk = pl.BlockSpec((tr, C), lambda i: (i, 0))
    sds = jax.ShapeDtypeStruct((R, C), F32)
    return pl.pallas_call(
        body, name=name, grid=(R // tr,), out_shape=(sds,) * 4,
        in_specs=[blk, pl.BlockSpec((n, tr, C), lambda i: (0, i, 0)), blk, blk], out_specs=(blk,) * 4,
        compiler_params=_params(dimension_semantics=("parallel",)),
    )(w, gslots, m, v)


def _slot_sum(slots, name):
    n, _, C = slots.shape

    def body(s_ref, o_ref):
        acc = s_ref[0]
        for s in range(1, n):
            acc = acc + s_ref[s]
        o_ref[...] = acc

    return pl.pallas_call(body, name=name, out_shape=jax.ShapeDtypeStruct((1, C), F32),
                          compiler_params=_params())(slots)


def _pad_cols(a, n):
    return jnp.pad(a, ((0, 0), (0, n - a.shape[1])))


def _ungather(g, axis):
    if axis == 0:
        return g.reshape(g.shape[0] * g.shape[1], g.shape[2])
    return jnp.transpose(g, (1, 0, 2)).reshape(g.shape[1], g.shape[0] * g.shape[2])


def _to_slots(full, axis):
    R, C = full.shape
    if axis == 0:
        return full.reshape(N_DEV, R // N_DEV, C)
    return jnp.transpose(full.reshape(R, N_DEV, C // N_DEV), (1, 0, 2))


def kernel(x, c, w_ada, b_ada, g_attn, w_in, b_fgate, g_out_fox, g_out_sb, w_out, g_mlp, w_up, conv_w, conv_b, w_down, g_final, loss_target, m_w_ada, m_b_ada, m_g_attn, m_w_in, m_b_fgate, m_g_out_fox, m_g_out_sb, m_w_out, m_g_mlp, m_w_up, m_conv_w, m_conv_b, m_w_down, m_g_final, v_w_ada, v_b_ada, v_g_attn, v_w_in, v_b_fgate, v_g_out_fox, v_g_out_sb, v_w_out, v_g_mlp, v_w_up, v_conv_w, v_conv_b, v_w_down, v_g_final):
    S, D = x.shape[1], x.shape[2]
    dh = D // 2
    n_heads = dh // HEAD_DIM
    n_qkv = 6 * dh
    ff = w_down.shape[1] * N_DEV
    ffp = -(-ff // (2 * LANES)) * (2 * LANES)
    nc = S // LANES
    me = 4 * lax.axis_index("x") + 2 * lax.axis_index("y") + lax.axis_index("c")
    xs, tgt = x[0], loss_target[0]

    c_all, win_g = _gather_two_level([c, w_in[0].astype(BF16)], name="gather_first")
    c_all = c_all.reshape(N_DEV, D)
    W_in = _ungather(win_g, 1)
    W_qkv, W_f = W_in[:, :n_qkv], _pad_cols(W_in[:, n_qkv:], LANES)
    cb_g, cb_v = _pad_cols(conv_b[:, :ff], ffp), _pad_cols(conv_b[:, ff:], ffp)

    n_ada = w_ada.shape[2]
    b_shard = lax.dynamic_slice(b_ada, (0, me * n_ada), (1, n_ada))
    mod_cols = _ada_fwd(c_all, w_ada[0], b_shard)
    (mod_g,) = _exchange([mod_cols], scatter=False, name="gather_mod")
    mod = lax.dynamic_index_in_dim(mod_g, me, axis=1, keepdims=False).reshape(6, 1, D)
    shift_a, scale_a, gate_a, shift_m, scale_m, gate_m = [mod[k] for k in range(6)]

    h1, h1_t = _prenorm(xs, g_attn, scale_a, shift_a, "prenorm_attn")
    qkv = _mm(h1, W_qkv, BF16, "proj_qkv")
    flog = _mm(h1, W_f, F32, "proj_fgate")
    zf = flog[:, :n_heads] + b_fgate
    z_rows = zf.T.reshape(n_heads * nc, LANES)
    f_rows = _fgate_fwd(z_rows, nc).reshape(n_heads, S)
    f_pairs = jnp.transpose(f_rows.reshape(n_heads // 2, 2, S), (0, 2, 1))
    fox, sb, k_sq = _att_prep(qkv, f_pairs)
    f_end, k_max = _skip_bounds(k_sq, f_rows)
    of_t, lse, (wout_g, wup_g, wdown_g, convw_g) = _fox_fwd(
        fox["q_t"], fox["k_n"], fox["v_t"], f_end, k_max,
        [w_out[0].astype(BF16), w_up[0].astype(BF16), w_down[0].astype(BF16), conv_w[0]])
    W_out = _ungather(wout_g, 0)
    W_up = _ungather(wup_g, 1)
    W_g, W_v = _pad_cols(W_up[:, :ff], ffp), _pad_cols(W_up[:, ff:], ffp)
    W_down = jnp.pad(_ungather(wdown_g, 0), ((0, ffp - ff), (0, 0)))
    cw_full = _ungather(convw_g, 1)
    cw_g, cw_v = _pad_cols(cw_full[:, :ff], ffp), _pad_cols(cw_full[:, ff:], ffp)
    os_t = _sb_fwd(sb["q_t"], sb["k_n"], sb["v_t"])
    o_f, o_s = of_t.reshape(dh, S), os_t.reshape(dh, S)
    mix, mix_t = _headnorm_fwd(o_f, o_s, g_out_fox, g_out_sb)
    a_out = _mm(mix, W_out, F32, "proj_out")
    x1, h2, h2_t = _resid_prenorm(xs, a_out, gate_a, g_mlp, scale_m, shift_m)
    up_g, up_v, act, act_t = _mlp_up(h2, W_g, W_v, cw_g, cw_v, cb_g, cb_v)

    dx2, gm, loss_p, dg_final, dgate_m = _loss_head(x1, act, W_down, gate_m, g_final.reshape(1, D), tgt)
    dW_down = _mm_acc(act_t, gm, "bwd_down_w")
    du_g, du_v, p_g, p_v = _conv_act_bwd(gm, W_down, up_g, up_v, cw_g, cw_v, cb_g, cb_v)
    dx1, dscale_m, dshift_m, dg_mlp, ga, dgate_a, dup_g, dup_v, _ = _norm_bwd(
        [du_g, du_v], [W_g, W_v], x1, dx2, g_mlp, scale_m, "norm_mlp_bwd", gate=gate_a, branch=a_out,
        conv=[cw_g, cw_v])
    dW_g = _mm_acc(h2_t, dup_g, "bwd_up_w_g")
    dW_v = _mm_acc(h2_t, dup_v, "bwd_up_w_v")
    dmix = _mm(ga, W_out, F32, "bwd_out_act", nt=True)
    dW_out = _mm_acc(mix_t, ga, "bwd_out_w")
    dof_n, dof_t, dos_n, dos_t, dg_fox, dg_sb = _headnorm_bwd(dmix, o_f, o_s, g_out_fox, g_out_sb)
    dqf_t, dkf, dvf = _fox_bwd(fox["q_t"], fox["q_n"], fox["k_n"], fox["k_t"], fox["v_n"], dof_t, dof_n, of_t, lse,
                              f_end, k_max)
    dW_upf = jnp.concatenate([dW_g[:, :ff], dW_v[:, :ff]], axis=1)
    dcw = jnp.concatenate([p_g[:CONV_W, :ff], p_v[:CONV_W, :ff]], axis=1)
    dqs_t, dks, dvs, (s_out, s_up, s_down, s_cw) = _sb_bwd(
        sb["q_t"], sb["q_n"], sb["k_n"], sb["k_t"], sb["v_n"], dos_t, dos_n, os_t,
        [_to_slots(dW_out, 0), _to_slots(dW_upf, 1), _to_slots(dW_down[:ff], 0), _to_slots(dcw, 1)])
    dparts, dfk = _dqkv_assemble(dqf_t, dkf, dvf, dqs_t, dks, dvs)
    dz_rows, db_fgate = _fgate_bwd(dfk.reshape(n_heads * nc, LANES),
                                   dqf_t[:, Q_F_LANE, :].reshape(n_heads * nc, LANES), z_rows, nc)
    dzf = _pad_cols(dz_rows.reshape(n_heads, S).T, LANES).astype(BF16)
    dW_qkv = _mm_acc_parts(h1_t, dparts, "bwd_in_w")
    dW_f = _mm_acc(h1_t, dzf, "bwd_in_w_fgate")
    dW_in = jnp.concatenate([jnp.transpose(dW_qkv, (1, 0, 2)).reshape(D, n_qkv), dW_f[:, :n_heads]], axis=1)
    bound_in = _to_slots(dW_in, 1)
    bound_in = bound_in.reshape((N_CHIPS, 2) + bound_in.shape[1:])
    (got_in,) = _scatter_to_sibling([bound_in], "scatter_sibling")
    c_idx = lax.axis_index("c").astype(jnp.int32).reshape(1)
    grad_x, dscale_a, dshift_a, dg_attn, (s_in,) = _norm_bwd(
        [dparts, dzf], [W_qkv, W_f], xs, dx1, g_attn, scale_a, "norm_attn_bwd",
        bound=[_pair_add(bound_in, got_in, c_idx, "pair_add")], tm=512)

    dconv_b = jnp.concatenate([p_g[CONV_W:CONV_W + 1, :ff], p_v[CONV_W:CONV_W + 1, :ff]], axis=1)
    parts = [dshift_a, dscale_a, dgate_a, dshift_m, dscale_m, dgate_m,
             dg_attn, db_fgate.reshape(1, n_heads), dg_fox, dg_sb, dg_mlp, dconv_b, dg_final,
             loss_p[:, :1]]
    sizes = [p.shape[1] for p in parts]
    vec = jnp.concatenate(parts, axis=1)
    n_vec = -(-vec.shape[1] // LANES) * LANES
    vec = _pad_cols(vec, n_vec)
    (vec_g,) = _exchange([vec], scatter=False, name="gather_small")
    offs = [0]
    for s in sizes:
        offs.append(offs[-1] + s)

    def small(k0, k1=None):
        k1 = k0 if k1 is None else k1
        return vec_g[:, :, offs[k0]:offs[k1 + 1]]

    dmod_all = small(0, 5).reshape(N_DEV, 6 * D)
    dmod_cols = lax.dynamic_slice(dmod_all, (0, me * n_ada), (N_DEV, n_ada))
    dW_ada = _ada_bwd(c_all.T, dmod_cols)


    res = {}
    res["w_ada"] = _adamw(w_ada[0], dW_ada[None], m_w_ada[0], v_w_ada[0], "adamw_w_ada")
    res["w_in"] = _adamw(w_in[0], s_in, m_w_in[0], v_w_in[0], "adamw_w_in")
    res["w_out"] = _adamw(w_out[0], s_out, m_w_out[0], v_w_out[0], "adamw_w_out")
    res["w_up"] = _adamw(w_up[0], s_up, m_w_up[0], v_w_up[0], "adamw_w_up")
    res["w_down"] = _adamw(w_down[0], s_down, m_w_down[0], v_w_down[0], "adamw_w_down")
    res["conv_w"] = _adamw(conv_w[0], s_cw, m_conv_w[0], v_conv_w[0], "adamw_conv_w")
    small_names = ["b_ada", "g_attn", "b_fgate", "g_out_fox", "g_out_sb", "g_mlp", "conv_b", "g_final"]
    small_w = [b_ada, g_attn, b_fgate, g_out_fox, g_out_sb, g_mlp, conv_b, g_final.reshape(1, D)]
    small_m = [m_b_ada, m_g_attn, m_b_fgate, m_g_out_fox, m_g_out_sb, m_g_mlp, m_conv_b, m_g_final.reshape(1, D)]
    small_v = [v_b_ada, v_g_attn, v_b_fgate, v_g_out_fox, v_g_out_sb, v_g_mlp, v_conv_b, v_g_final.reshape(1, D)]
    small_res = _adamw(jnp.concatenate(small_w, axis=1), small(0, 12), jnp.concatenate(small_m, axis=1),
                       jnp.concatenate(small_v, axis=1), "adamw_small")
    lo = 0
    for nm, wv in zip(small_names, small_w):
        res[nm] = tuple(r[:, lo:lo + wv.shape[1]] for r in small_res)
        lo += wv.shape[1]
    loss = _slot_sum(_pad_cols(small(13).reshape(N_DEV, 1), LANES).reshape(N_DEV, 1, LANES), "loss_sum")[0, 0]

    names = ["w_ada", "b_ada", "g_attn", "w_in", "b_fgate", "g_out_fox", "g_out_sb", "w_out", "g_mlp",
             "w_up", "conv_w", "conv_b", "w_down", "g_final"]

    def shaped(n, a):
        if n == "g_final":
            return a.reshape(D)
        if n in ("b_ada", "g_attn", "b_fgate", "g_out_fox", "g_out_sb", "g_mlp", "conv_b"):
            return a
        return a[None]

    outs = [loss, grad_x[None]]
    for k in range(4):
        outs += [shaped(n, res[n][k]) for n in names]
    return tuple(outs)
```
